```python
import jax, jax.numpy as jnp
from jax import lax
import numpy as np

D_MODEL = 4096
BATCH = 4
SEQ = 2048
DEPTH = 1

MEM_LEN = 256
EPS = 1e-6
ROPE_THETA = 500000.0
CHUNK = 128
A_GROUPS = 16
A_WIDTH = D_MODEL // 2
A_GROUP_DIM = A_WIDTH // A_GROUPS
B_HEADS = 16
B_HEAD_DIM = 128
B_KV_HEADS = 4
B_WIDTH = B_HEADS * B_HEAD_DIM
B_ROT = B_HEAD_DIM // 4
IDX_HEADS = 16
IDX_DIM = 64
IDX_ROT = IDX_DIM // 4
TOPK_MAX = 256
Q_BLOCK = 128
X_HEADS = 4
X_HEAD_DIM = 256
X_WIDTH = X_HEADS * X_HEAD_DIM
FFN_HIDDEN = -(-8 * D_MODEL // (3 * 256)) * 256
IN_SIZES = (2 * A_WIDTH, B_WIDTH, B_KV_HEADS * B_HEAD_DIM, B_KV_HEADS * B_HEAD_DIM,
            IDX_HEADS * IDX_DIM, IDX_DIM, IDX_HEADS, 2 * D_MODEL)
IN_WIDTH = sum(IN_SIZES)

kernel_name = 'hybrid_gmlp_dsa_gated_block'


def rmsnorm(x, g):
    xf = x.astype(jnp.float32)
    y = xf * lax.rsqrt(jnp.mean(xf * xf, axis=-1, keepdims=True) + EPS)
    return (y * g.astype(jnp.float32)).astype(x.dtype)


def rope_angles(positions, rot_dim):
    inv_freq = ROPE_THETA ** (-jnp.arange(0, rot_dim, 2, dtype=jnp.float32) / rot_dim)
    ang = positions.astype(jnp.float32)[..., None] * inv_freq
    return jnp.cos(ang), jnp.sin(ang)


def apply_partial_rope(x, cos, sin):
    r = 2 * cos.shape[-1]
    xr, xp = x[..., :r], x[..., r:]
    x1, x2 = xr[..., : r // 2], xr[..., r // 2:]
    c = cos[:, :, None, :].astype(x.dtype)
    s = sin[:, :, None, :].astype(x.dtype)
    return jnp.concatenate([x1 * c - x2 * s, x2 * c + x1 * s, xp], axis=-1)


def chunked_spatial_gating(z, norm_g, w_s, b_s):
    bsz, seq, _ = z.shape
    u, v = jnp.split(z, 2, axis=-1)
    v = rmsnorm(v, norm_g)
    v = v.reshape(bsz, seq // CHUNK, CHUNK, A_GROUPS, A_GROUP_DIM)
    causal = jnp.tril(jnp.ones((CHUNK, CHUNK), dtype=bool))
    w = jnp.where(causal[None], w_s, 0).astype(v.dtype)
    s = jnp.einsum('gts,bcsgd->bctgd', w, v) + b_s.T.astype(v.dtype)[None, None, :, :, None]
    return u * s.reshape(bsz, seq, A_WIDTH)


def dsa_attention(q, k, v, qi, ki, wi):
    bsz, seq = q.shape[0], q.shape[1]
    n_sel = min(TOPK_MAX, seq // 4)
    n_blocks = seq // Q_BLOCK
    grp = B_HEADS // B_KV_HEADS
    scale = B_HEAD_DIM ** -0.5
    idx_scale = (IDX_DIM ** -0.5) * (IDX_HEADS ** -0.5)
    key_pos = jnp.arange(seq)
    gather = jax.vmap(lambda table, ids: table[ids])

    def block(bi):
        start = bi * Q_BLOCK
        qb = lax.dynamic_slice_in_dim(q, start, Q_BLOCK, axis=1)
        qib = lax.dynamic_slice_in_dim(qi, start, Q_BLOCK, axis=1)
        wib = lax.dynamic_slice_in_dim(wi, start, Q_BLOCK, axis=1)
        qpos = start + jnp.arange(Q_BLOCK)
        causal = key_pos[None, :] <= qpos[:, None]
        dots = jnp.einsum('bthd,bsd->bths', qib, ki).astype(jnp.float32)
        iscore = jnp.einsum('bth,bths->bts', wib.astype(jnp.float32), jax.nn.relu(dots)) * idx_scale
        iscore = jnp.where(causal[None], iscore, -jnp.inf)
        _, sel = lax.top_k(iscore, n_sel)
        valid = sel <= qpos[None, :, None]
        ks = gather(k, sel)
        vs = gather(v, sel)
        qg = qb.reshape(bsz, Q_BLOCK, B_KV_HEADS, grp, B_HEAD_DIM)
        logits = jnp.einsum('btkgd,btskd->btkgs', qg, ks).astype(jnp.float32) * scale
        logits = jnp.where(valid[:, :, None, None, :], logits, -jnp.inf)
        p = jax.nn.softmax(logits, axis=-1).astype(v.dtype)
        o = jnp.einsum('btkgs,btskd->btkgd', p, vs)
        return o.reshape(bsz, Q_BLOCK, B_WIDTH)

    out = lax.map(block, jnp.arange(n_blocks))
    return out.transpose(1, 0, 2, 3).reshape(bsz, seq, B_WIDTH)


def memory_cross_attention(h, mem_n, wq, wk, wv, wo):
    bsz, seq, _ = h.shape
    m = mem_n.shape[1]
    q = (h @ wq).reshape(bsz, seq, X_HEADS, X_HEAD_DIM)
    k = (mem_n @ wk).reshape(bsz, m, X_HEADS, X_HEAD_DIM)
    v = (mem_n @ wv).reshape(bsz, m, X_HEADS, X_HEAD_DIM)
    logits = jnp.einsum('bshd,bmhd->bhsm', q, k).astype(jnp.float32) * (X_HEAD_DIM ** -0.5)
    p = jax.nn.softmax(logits, axis=-1).astype(v.dtype)
    o = jnp.einsum('bhsm,bmhd->bshd', p, v).reshape(bsz, seq, X_WIDTH)
    return o @ wo


def setup_inputs(seed: int = 0) -> dict:
    key = jax.random.key(seed)
    ks = jax.random.split(key, 24)

    def nrm(k, shape, scale):
        return jax.random.normal(k, shape, jnp.float32) * scale

    def gain(k, shape):
        return 1.0 + 0.02 * jax.random.normal(k, shape, jnp.float32)

    L = DEPTH
    x = jax.random.normal(ks[0], (BATCH, SEQ, D_MODEL), jnp.float32)
    mem = jax.random.normal(ks[1], (BATCH, MEM_LEN, D_MODEL), jnp.float32)
    offsets = jax.random.randint(ks[2], (BATCH, 1), 0, 4096, dtype=jnp.int32)
    positions = (offsets + jnp.arange(SEQ, dtype=jnp.int32)[None, :]).astype(jnp.int32)
    return {
        'x': x,
        'mem': mem,
        'positions': positions,
        'norm_mix_g': gain(ks[3], (L, D_MODEL)),
        'w_in': nrm(ks[4], (L, D_MODEL, IN_WIDTH), D_MODEL ** -0.5),
        'a_norm_g': gain(ks[5], (L, A_WIDTH)),
        'a_spatial_w': nrm(ks[6], (L, A_GROUPS, CHUNK, CHUNK), CHUNK ** -0.5),
        'a_spatial_b': gain(ks[7], (L, A_GROUPS, CHUNK)),
        'p_a': nrm(ks[8], (L, A_WIDTH, D_MODEL), A_WIDTH ** -0.5),
        'p_b': nrm(ks[9], (L, B_WIDTH, D_MODEL), B_WIDTH ** -0.5),
        'w_out': nrm(ks[10], (L, D_MODEL, D_MODEL), D_MODEL ** -0.5),
        'norm_x_g': gain(ks[11], (L, D_MODEL)),
        'norm_mem_g': gain(ks[12], (L, D_MODEL)),
        'xq_w': nrm(ks[13], (L, D_MODEL, X_WIDTH), D_MODEL ** -0.5),
        'xk_w': nrm(ks[14], (L, D_MODEL, X_WIDTH), D_MODEL ** -0.5),
        'xv_w': nrm(ks[15], (L, D_MODEL, X_WIDTH), D_MODEL ** -0.5),
        'xo_w': nrm(ks[16], (L, X_WIDTH, D_MODEL), X_WIDTH ** -0.5),
        'norm_ffn_g': gain(ks[17], (L, D_MODEL)),
        'ffn_w1': nrm(ks[18], (L, D_MODEL, FFN_HIDDEN), D_MODEL ** -0.5),
        'ffn_w3': nrm(ks[19], (L, D_MODEL, FFN_HIDDEN), D_MODEL ** -0.5),
        'ffn_w2': nrm(ks[20], (L, FFN_HIDDEN, D_MODEL), FFN_HIDDEN ** -0.5),
        'final_norm_g': gain(ks[21], (D_MODEL,)),
    }


def reference(x, mem, positions, norm_mix_g, w_in, a_norm_g, a_spatial_w, a_spatial_b,
              p_a, p_b, w_out, norm_x_g, norm_mem_g, xq_w, xk_w, xv_w, xo_w,
              norm_ffn_g, ffn_w1, ffn_w3, ffn_w2, final_norm_g):
    bsz, seq, _ = x.shape
    split_points = []
    acc = 0
    for sz in IN_SIZES[:-1]:
        acc += sz
        split_points.append(acc)
    cos_b, sin_b = rope_angles(positions, B_ROT)
    cos_i, sin_i = rope_angles(positions, IDX_ROT)

    for l in range(DEPTH):
        h = rmsnorm(x, norm_mix_g[l])
        proj = h @ w_in[l]
        za, q, k, v, qi, ki, wi, gates = jnp.split(proj, split_points, axis=-1)
        y_a = chunked_spatial_gating(jax.nn.gelu(za), a_norm_g[l], a_spatial_w[l], a_spatial_b[l])
        q = apply_partial_rope(q.reshape(bsz, seq, B_HEADS, B_HEAD_DIM), cos_b, sin_b)
        k = apply_partial_rope(k.reshape(bsz, seq, B_KV_HEADS, B_HEAD_DIM), cos_b, sin_b)
        v = v.reshape(bsz, seq, B_KV_HEADS, B_HEAD_DIM)
        qi = apply_partial_rope(qi.reshape(bsz, seq, IDX_HEADS, IDX_DIM), cos_i, sin_i)
        ki = apply_partial_rope(ki[:, :, None, :], cos_i, sin_i)[:, :, 0, :]
        y_b = dsa_attention(q, k, v, qi, ki, wi)
        g_a, g_b = jnp.split(jax.nn.sigmoid(gates), 2, axis=-1)
        merged = g_a * (y_a @ p_a[l]) + g_b * (y_b @ p_b[l])
        x = x + merged @ w_out[l]
        x = x + memory_cross_attention(rmsnorm(x, norm_x_g[l]), rmsnorm(mem, norm_mem_g[l]),
                                       xq_w[l], xk_w[l], xv_w[l], xo_w[l])
        h = rmsnorm(x, norm_ffn_g[l])
        x = x + (jax.nn.silu(h @ ffn_w1[l]) * (h @ ffn_w3[l])) @ ffn_w2[l]

    return rmsnorm(x, final_norm_g)
```

```python
import functools

import jax
import jax.numpy as jnp
from jax import lax
from jax.experimental import pallas as pl
from jax.experimental.pallas import tpu as pltpu

D_MODEL = 4096
MEM_LEN = 256
EPS = 1e-6
ROPE_THETA = 500000.0
CHUNK = 128
A_GROUPS = 16
A_WIDTH = D_MODEL // 2
A_GROUP_DIM = A_WIDTH // A_GROUPS
B_HEADS = 16
B_HEAD_DIM = 128
B_KV_HEADS = 4
B_WIDTH = B_HEADS * B_HEAD_DIM
B_KV_WIDTH = B_KV_HEADS * B_HEAD_DIM
B_ROT = B_HEAD_DIM // 4
IDX_HEADS = 16
IDX_DIM = 64
IDX_WIDTH = IDX_HEADS * IDX_DIM
IDX_ROT = IDX_DIM // 4
TOPK_MAX = 256
Q_BLOCK = 128
X_HEADS = 4
X_HEAD_DIM = 256
X_WIDTH = X_HEADS * X_HEAD_DIM
IN_SIZES = (2 * A_WIDTH, B_WIDTH, B_KV_WIDTH, B_KV_WIDTH, IDX_WIDTH, IDX_DIM, IDX_HEADS,
            2 * D_MODEL)

LANES = 128
ATT_Q0 = 0
ATT_K0 = ATT_Q0 + B_WIDTH
ATT_V0 = ATT_K0 + B_KV_WIDTH
ATT_QI0 = ATT_V0 + B_KV_WIDTH
ATT_KW0 = ATT_QI0 + IDX_WIDTH
ATT_WIDTH = ATT_KW0 + LANES

VMEM_LIMIT_BYTES = 56 * 1024 * 1024

_BF16 = jnp.bfloat16
_F32 = jnp.float32
_NEG_INF = float("-inf")


def _compiler_params(n_axes):
    return pltpu.CompilerParams(
        dimension_semantics=("arbitrary",) * n_axes,
        vmem_limit_bytes=VMEM_LIMIT_BYTES,
    )


def _rmsnorm_kernel(x_ref, g_ref, o_ref):
    x = x_ref[...]
    ms = jnp.mean(x * x, axis=-1, keepdims=True)
    y = x * lax.rsqrt(ms + EPS)
    o_ref[...] = (y * g_ref[...]).astype(o_ref.dtype)


def _rmsnorm(x, g, out_dtype, tm=512):
    m, d = x.shape
    return pl.pallas_call(
        _rmsnorm_kernel,
        grid=(m // tm,),
        in_specs=[pl.BlockSpec((tm, d), lambda i: (i, 0)),
                  pl.BlockSpec((1, d), lambda i: (0, 0))],
        out_specs=pl.BlockSpec((tm, d), lambda i: (i, 0)),
        out_shape=jax.ShapeDtypeStruct((m, d), out_dtype),
        compiler_params=_compiler_params(1),
        name="rmsnorm",
    )(x, g.reshape(1, d))


def _ep_identity(accs, extras):
    return accs[0]


def _ep_residual(accs, extras):
    return extras[0] + accs[0]


def _ep_gated_merge(accs, extras):
    y_a, y_b, gate_a, gate_b = accs
    return jax.nn.sigmoid(gate_a) * y_a + jax.nn.sigmoid(gate_b) * y_b


def _ep_swiglu(accs, extras):
    return jax.nn.silu(accs[0]) * accs[1]


def _mm_kernel(*refs, n_lhs, pair_lhs, epilogue):
    n_rhs = len(pair_lhs)
    lhs_refs = refs[:n_lhs]
    rhs_refs = refs[n_lhs:n_lhs + n_rhs]
    extra_refs = refs[n_lhs + n_rhs:-1]
    o_ref = refs[-1]
    accs = [jnp.dot(lhs_refs[li][...], r[...], preferred_element_type=_F32)
            for li, r in zip(pair_lhs, rhs_refs)]
    o_ref[...] = epilogue(accs, [e[...] for e in extra_refs]).astype(o_ref.dtype)


def _fused_matmul(lhs, pairs, extras, epilogue, n_out, out_dtype, tm, tn, name):
    m = lhs[0].shape[0]
    in_specs = []
    for a in lhs:
        in_specs.append(pl.BlockSpec((tm, a.shape[1]), lambda i, j: (i, 0),
                                     pipeline_mode=pl.Buffered(1)))
    for _, w, off in pairs:
        in_specs.append(pl.BlockSpec((w.shape[0], tn), functools.partial(_rhs_index, off=off)))
    for _ in extras:
        in_specs.append(pl.BlockSpec((tm, tn), lambda i, j: (i, j)))
    kern = functools.partial(_mm_kernel, n_lhs=len(lhs),
                             pair_lhs=tuple(p[0] for p in pairs), epilogue=epilogue)
    return pl.pallas_call(
        kern,
        grid=(m // tm, n_out // tn),
        in_specs=in_specs,
        out_specs=pl.BlockSpec((tm, tn), lambda i, j: (i, j)),
        out_shape=jax.ShapeDtypeStruct((m, n_out), out_dtype),
        compiler_params=_compiler_params(2),
        name=name,
    )(*lhs, *[p[1] for p in pairs], *extras)


def _rhs_index(i, j, *, off):
    return (0, j + off)


def _gating_kernel(za_ref, ng_ref, w_ref, bt_ref, o_ref):
    z = jax.nn.gelu(za_ref[...])
    u = z[:, :A_WIDTH]
    v = z[:, A_WIDTH:]
    ms = jnp.mean(v * v, axis=-1, keepdims=True)
    vn = ((v * lax.rsqrt(ms + EPS)) * ng_ref[...]).astype(_BF16)
    row = lax.broadcasted_iota(jnp.int32, (CHUNK, CHUNK), 0)
    col = lax.broadcasted_iota(jnp.int32, (CHUNK, CHUNK), 1)
    causal = col <= row
    for g in range(A_GROUPS):
        sl = slice(g * A_GROUP_DIM, (g + 1) * A_GROUP_DIM)
        w = jnp.where(causal, w_ref[g], 0.0).astype(_BF16)
        s = jnp.dot(w, vn[:, sl], preferred_element_type=_F32) + bt_ref[:, g:g + 1]
        o_ref[:, sl] = (u[:, sl] * s).astype(o_ref.dtype)


def _spatial_gating(za, norm_g, w_s, b_s):
    m = za.shape[0]
    return pl.pallas_call(
        _gating_kernel,
        grid=(m // CHUNK,),
        in_specs=[pl.BlockSpec((CHUNK, 2 * A_WIDTH), lambda i: (i, 0)),
                  pl.BlockSpec((1, A_WIDTH), lambda i: (0, 0)),
                  pl.BlockSpec((A_GROUPS, CHUNK, CHUNK), lambda i: (0, 0, 0)),
                  pl.BlockSpec((CHUNK, A_GROUPS), lambda i: (0, 0))],
        out_specs=pl.BlockSpec((CHUNK, A_WIDTH), lambda i: (i, 0)),
        out_shape=jax.ShapeDtypeStruct((m, A_WIDTH), _BF16),
        compiler_params=_compiler_params(1),
        name="spatial_gating",
    )(za, norm_g.reshape(1, A_WIDTH), w_s, b_s.T)


def _rope_tables(positions, rot_dim, period):
    half = rot_dim // 2
    inv_freq = ROPE_THETA ** (-jnp.arange(0, rot_dim, 2, dtype=_F32) / rot_dim)
    ang = positions.astype(_F32)[..., None] * inv_freq
    cos, sin = jnp.cos(ang), jnp.sin(ang)
    lane = jnp.arange(LANES) % period
    idx = lane % half
    cos_l = jnp.take(cos, idx, axis=-1)
    sin_l = jnp.take(sin, idx, axis=-1)
    c = jnp.where(lane < rot_dim, cos_l, 1.0)
    s1 = jnp.where((lane >= half) & (lane < rot_dim), sin_l, 0.0)
    s2 = jnp.where(lane < half, -sin_l, 0.0)
    return c, s1, s2


def _rope(x, c, s1, s2, half):
    return x * c + pltpu.roll(x, half, 1) * s1 + pltpu.roll(x, LANES - half, 1) * s2


def _prep_kernel(att_ref, cb_ref, s1b_ref, s2b_ref, ci_ref, s1i_ref, s2i_ref,
                 q_ref, k_ref, v_ref, qi_ref, ki_ref, wi_ref):
    cb, s1b, s2b = cb_ref[0], s1b_ref[0], s2b_ref[0]
    ci, s1i, s2i = ci_ref[0], s1i_ref[0], s2i_ref[0]
    for h in range(B_HEADS):
        x = att_ref[0, :, ATT_Q0 + h * LANES:ATT_Q0 + (h + 1) * LANES]
        q_ref[0, :, h * LANES:(h + 1) * LANES] = _rope(x, cb, s1b, s2b, B_ROT // 2).astype(_BF16)
    for h in range(B_KV_HEADS):
        x = att_ref[0, :, ATT_K0 + h * LANES:ATT_K0 + (h + 1) * LANES]
        k_ref[0, :, h * LANES:(h + 1) * LANES] = _rope(x, cb, s1b, s2b, B_ROT // 2).astype(_BF16)
    v_ref[0] = att_ref[0, :, ATT_V0:ATT_V0 + B_KV_WIDTH].astype(_BF16)
    lane = lax.broadcasted_iota(jnp.int32, ci.shape, 1)
    is_ki = lane < IDX_DIM
    heads_per_group = LANES // IDX_DIM
    for j in range(IDX_WIDTH // LANES):
        x = att_ref[0, :, ATT_QI0 + j * LANES:ATT_QI0 + (j + 1) * LANES]
        y = _rope(x, ci, s1i, s2i, IDX_ROT // 2)
        for r in range(heads_per_group):
            h = j * heads_per_group + r
            yr = y if r == 0 else pltpu.roll(y, LANES - r * IDX_DIM, 1)
            qi_ref[0, :, h * LANES:(h + 1) * LANES] = jnp.where(is_ki, yr, 0.0).astype(_BF16)
    x = att_ref[0, :, ATT_KW0:ATT_KW0 + LANES]
    y = _rope(x, jnp.where(is_ki, ci, 1.0), jnp.where(is_ki, s1i, 0.0),
              jnp.where(is_ki, s2i, 0.0), IDX_ROT // 2)
    ki_ref[0] = jnp.where(is_ki, y, 0.0).astype(_BF16)
    wi_ref[0] = y[:, IDX_DIM:IDX_DIM + IDX_HEADS]


def _prep_attention_inputs(att, tables_b, tables_i, ts=256):
    bsz, seq, _ = att.shape
    tab_spec = pl.BlockSpec((1, ts, LANES), lambda b, i: (b, i, 0))

    def out(width, dtype):
        return (pl.BlockSpec((1, ts, width), lambda b, i: (b, i, 0)),
                jax.ShapeDtypeStruct((bsz, seq, width), dtype))

    outs = [out(B_WIDTH, _BF16), out(B_KV_WIDTH, _BF16), out(B_KV_WIDTH, _BF16),
            out(IDX_HEADS * LANES, _BF16), out(LANES, _BF16), out(IDX_HEADS, _F32)]
    return pl.pallas_call(
        _prep_kernel,
        grid=(bsz, seq // ts),
        in_specs=[pl.BlockSpec((1, ts, ATT_WIDTH), lambda b, i: (b, i, 0))] + [tab_spec] * 6,
        out_specs=[o[0] for o in outs],
        out_shape=[o[1] for o in outs],
        compiler_params=_compiler_params(2),
        name="rope_split",
    )(att, *tables_b, *tables_i)


_NT_DIMS = (((1,), (1,)), ((), ()))


def _ordered_int_to_float(key):
    bits = jnp.where(key < 0, key ^ jnp.int32(0x7FFFFFFF), key)
    return lax.bitcast_convert_type(bits, _F32)


def _dsa_kernel(qi_ref, wi_ref, ki_ref, q_ref, k_ref, v_ref, o_ref, score_ref, bias_ref,
                *, n_sel, key_chunk):
    seq = k_ref.shape[1]
    blk = pl.program_id(1)
    qpos = blk * Q_BLOCK + lax.broadcasted_iota(jnp.int32, (Q_BLOCK, 1), 0)
    kpos = lax.broadcasted_iota(jnp.int32, (1, seq), 1)
    idx_scale = (IDX_DIM ** -0.5) * (IDX_HEADS ** -0.5)

    wi = wi_ref[0]
    for c in range(seq // key_chunk):
        ks = slice(c * key_chunk, (c + 1) * key_chunk)
        ki = ki_ref[0, ks, :]
        acc = jnp.zeros((Q_BLOCK, key_chunk), _F32)
        for h in range(IDX_HEADS):
            qh = qi_ref[0, :, h * LANES:(h + 1) * LANES]
            d = lax.dot_general(qh, ki, _NT_DIMS, preferred_element_type=_F32)
            acc = acc + wi[:, h:h + 1] * jnp.maximum(d, 0.0)
        kpos_c = c * key_chunk + lax.broadcasted_iota(jnp.int32, (1, key_chunk), 1)
        score_ref[:, ks] = jnp.where(kpos_c <= qpos, acc * idx_scale + 0.0, _NEG_INF)

    def count_ge(cand_f):
        return jnp.sum(jnp.where(score_ref[...] >= cand_f, 1.0, 0.0), axis=-1, keepdims=True)

    int_min = jnp.int32(-2 ** 31)
    base = jnp.where(count_ge(jnp.zeros((Q_BLOCK, 1), _F32)) >= n_sel, jnp.int32(0), int_min)

    def search_step(it, base):
        cand = base | lax.shift_left(jnp.int32(1), jnp.int32(30) - it)
        cnt = count_ge(_ordered_int_to_float(cand))
        return jnp.where(cnt >= n_sel, cand, base)

    base = lax.fori_loop(0, 31, search_step, base)
    thr = jnp.where(qpos + 1 < n_sel, _NEG_INF, _ordered_int_to_float(base))

    causal = kpos <= qpos
    ge = score_ref[...] >= thr
    n_ge = jnp.sum(jnp.where(causal, jnp.where(ge, 1.0, 0.0), 0.0), axis=-1, keepdims=True)
    bias_ref[...] = jnp.where(causal, jnp.where(ge, 0.0, _NEG_INF), _NEG_INF)

    @pl.when(jnp.max(n_ge) > n_sel)
    def _():
        score = score_ref[...]
        gt = score > thr
        eq = score == thr
        n_gt = jnp.sum(jnp.where(gt, 1.0, 0.0), axis=-1, keepdims=True)
        need = n_sel - n_gt
        last = jnp.zeros((Q_BLOCK, 1), jnp.int32)
        for bit in reversed(range((seq - 1).bit_length())):
            cand = last | (1 << bit)
            cnt = jnp.sum(jnp.where(eq, jnp.where(kpos < cand, 1.0, 0.0), 0.0),
                          axis=-1, keepdims=True)
            last = jnp.where(cnt < need, cand, last)
        keep_eq = jnp.where(eq, jnp.where(kpos <= last, 0.0, _NEG_INF), _NEG_INF)
        bias_ref[...] = jnp.where(kpos <= qpos, jnp.where(gt, 0.0, keep_eq), _NEG_INF)

    scale = B_HEAD_DIM ** -0.5
    grp = B_HEADS // B_KV_HEADS
    for g in range(B_KV_HEADS):
        kg = k_ref[0, :, g * B_HEAD_DIM:(g + 1) * B_HEAD_DIM]
        vg = v_ref[0, :, g * B_HEAD_DIM:(g + 1) * B_HEAD_DIM]
        for hh in range(grp):
            hs = slice((g * grp + hh) * B_HEAD_DIM, (g * grp + hh + 1) * B_HEAD_DIM)
            logits = lax.dot_general(q_ref[0, :, hs], kg, _NT_DIMS,
                                     preferred_element_type=_F32) * scale + bias_ref[...]
            mx = jnp.max(logits, axis=-1, keepdims=True)
            p = jnp.exp(logits - mx)
            denom = jnp.sum(p, axis=-1, keepdims=True)
            o = jnp.dot(p.astype(_BF16), vg, preferred_element_type=_F32)
            o_ref[0, :, hs] = (o / denom).astype(o_ref.dtype)


def _dsa_attention(q, k, v, qi, ki, wi):
    bsz, seq, _ = q.shape
    n_sel = min(TOPK_MAX, seq // 4)

    def q_spec(width):
        return pl.BlockSpec((1, Q_BLOCK, width), lambda b, i: (b, i, 0))

    def kv_spec(width):
        return pl.BlockSpec((1, seq, width), lambda b, i: (b, 0, 0))

    kern = functools.partial(_dsa_kernel, n_sel=n_sel, key_chunk=512)
    return pl.pallas_call(
        kern,
        grid=(bsz, seq // Q_BLOCK),
        in_specs=[q_spec(IDX_HEADS * LANES), q_spec(IDX_HEADS), kv_spec(LANES),
                  q_spec(B_WIDTH), kv_spec(B_KV_WIDTH), kv_spec(B_KV_WIDTH)],
        out_specs=q_spec(B_WIDTH),
        out_shape=jax.ShapeDtypeStruct((bsz, seq, B_WIDTH), _BF16),
        scratch_shapes=[pltpu.VMEM((Q_BLOCK, seq), _F32), pltpu.VMEM((Q_BLOCK, seq), _F32)],
        compiler_params=_compiler_params(2),
        name="dsa_attention",
    )(qi, wi, ki, q, k, v)


def _xattn_kernel(q_ref, k_ref, v_ref, o_ref):
    scale = X_HEAD_DIM ** -0.5
    for h in range(X_HEADS):
        hs = slice(h * X_HEAD_DIM, (h + 1) * X_HEAD_DIM)
        logits = lax.dot_general(q_ref[0, :, hs], k_ref[0, :, hs], _NT_DIMS,
                                 preferred_element_type=_F32) * scale
        mx = jnp.max(logits, axis=-1, keepdims=True)
        p = jnp.exp(logits - mx)
        denom = jnp.sum(p, axis=-1, keepdims=True)
        o = jnp.dot(p.astype(_BF16), v_ref[0, :, hs], preferred_element_type=_F32)
        o_ref[0, :, hs] = (o / denom).astype(o_ref.dtype)


def _cross_attention(q, kv, tq=512):
    bsz, seq, _ = q.shape
    m = kv.shape[1]
    return pl.pallas_call(
        _xattn_kernel,
        grid=(bsz, seq // tq),
        in_specs=[pl.BlockSpec((1, tq, X_WIDTH), lambda b, i: (b, i, 0)),
                  pl.BlockSpec((1, m, X_WIDTH), lambda b, i: (b, 0, 0)),
                  pl.BlockSpec((1, m, X_WIDTH), lambda b, i: (b, 0, 1))],
        out_specs=pl.BlockSpec((1, tq, X_WIDTH), lambda b, i: (b, i, 0)),
        out_shape=jax.ShapeDtypeStruct((bsz, seq, X_WIDTH), _BF16),
        compiler_params=_compiler_params(2),
        name="cross_attention",
    )(q, kv, kv)


def kernel(x, mem, positions, norm_mix_g, w_in, a_norm_g, a_spatial_w, a_spatial_b, p_a, p_b,
           w_out, norm_x_g, norm_mem_g, xq_w, xk_w, xv_w, xo_w, norm_ffn_g, ffn_w1, ffn_w3,
           ffn_w2, final_norm_g):
    bsz, seq, d = x.shape
    m = bsz * seq
    depth = w_in.shape[0]
    ffn_hidden = ffn_w1.shape[-1]
    tables_b = _rope_tables(positions, B_ROT, B_HEAD_DIM)
    tables_i = _rope_tables(positions, IDX_ROT, IDX_DIM)
    za_end = IN_SIZES[0]
    att_end = za_end + sum(IN_SIZES[1:7])

    xf = x.reshape(m, d)
    mem_f = mem.reshape(bsz * MEM_LEN, d)
    for l in range(depth):
        w = w_in[l]
        w_za = w[:, :za_end].astype(_BF16)
        w_att = jnp.pad(w[:, za_end:att_end], ((0, 0), (0, ATT_WIDTH - (att_end - za_end)))
                        ).astype(_BF16)
        w_gate = w[:, att_end:].astype(_BF16)

        h = _rmsnorm(xf, norm_mix_g[l], _BF16)
        za = _fused_matmul([h], [(0, w_za, 0)], [], _ep_identity, 2 * A_WIDTH, _F32,
                           1024, 512, "proj_za")
        y_a = _spatial_gating(za, a_norm_g[l], a_spatial_w[l], a_spatial_b[l])
        att = _fused_matmul([h], [(0, w_att, 0)], [], _ep_identity, ATT_WIDTH, _F32,
                            512, ATT_WIDTH // 3, "proj_att")
        q, k, v, qi, ki, wi = _prep_attention_inputs(att.reshape(bsz, seq, ATT_WIDTH),
                                                     tables_b, tables_i)
        y_b = _dsa_attention(q, k, v, qi, ki, wi).reshape(m, B_WIDTH)
        tn = 256
        merged = _fused_matmul(
            [y_a, y_b, h],
            [(0, p_a[l].astype(_BF16), 0), (1, p_b[l].astype(_BF16), 0),
             (2, w_gate, 0), (2, w_gate, d // tn)],
            [], _ep_gated_merge, d, _BF16, 1024, tn, "gated_merge")
        x1 = _fused_matmul([merged], [(0, w_out[l].astype(_BF16), 0)], [xf], _ep_residual,
                           d, _F32, 1024, 512, "out_proj")

        hx = _rmsnorm(x1, norm_x_g[l], _BF16)
        mem_n = _rmsnorm(mem_f, norm_mem_g[l], _BF16)
        qx = _fused_matmul([hx], [(0, xq_w[l].astype(_BF16), 0)], [], _ep_identity,
                           X_WIDTH, _BF16, 1024, 512, "xattn_q")
        w_kv = jnp.concatenate([xk_w[l], xv_w[l]], axis=1).astype(_BF16)
        kvx = _fused_matmul([mem_n], [(0, w_kv, 0)], [], _ep_identity,
                            2 * X_WIDTH, _BF16, bsz * MEM_LEN, 512, "xattn_kv")
        ox = _cross_attention(qx.reshape(bsz, seq, X_WIDTH),
                              kvx.reshape(bsz, MEM_LEN, 2 * X_WIDTH)).reshape(m, X_WIDTH)
        x2 = _fused_matmul([ox], [(0, xo_w[l].astype(_BF16), 0)], [x1], _ep_residual,
                           d, _F32, 1024, 1024, "xattn_out")

        h2 = _rmsnorm(x2, norm_ffn_g[l], _BF16)
        act = _fused_matmul([h2], [(0, ffn_w1[l].astype(_BF16), 0),
                                   (0, ffn_w3[l].astype(_BF16), 0)],
                            [], _ep_swiglu, ffn_hidden, _BF16, 1024, 256, "ffn_up")
        xf = _fused_matmul([act], [(0, ffn_w2[l].astype(_BF16), 0)], [x2], _ep_residual,
                           d, _F32, 1024, 256, "ffn_down")

    return _rmsnorm(xf, final_norm_g, _F32).reshape(bsz, seq, d)
```

```python
import functools

import jax
import jax.numpy as jnp
from jax import lax
from jax.experimental import pallas as pl
from jax.experimental.pallas import tpu as pltpu

D_MODEL = 4096
MEM_LEN = 256
EPS = 1e-6
ROPE_THETA = 500000.0
CHUNK = 128
A_GROUPS = 16
A_WIDTH = D_MODEL // 2
A_GROUP_DIM = A_WIDTH // A_GROUPS
B_HEADS = 16
B_HEAD_DIM = 128
B_KV_HEADS = 4
B_WIDTH = B_HEADS * B_HEAD_DIM
B_KV_WIDTH = B_KV_HEADS * B_HEAD_DIM
B_ROT = B_HEAD_DIM // 4
IDX_HEADS = 16
IDX_DIM = 64
IDX_WIDTH = IDX_HEADS * IDX_DIM
IDX_ROT = IDX_DIM // 4
TOPK_MAX = 256
Q_BLOCK = 128
X_HEADS = 4
X_HEAD_DIM = 256
X_WIDTH = X_HEADS * X_HEAD_DIM
IN_SIZES = (2 * A_WIDTH, B_WIDTH, B_KV_WIDTH, B_KV_WIDTH, IDX_WIDTH, IDX_DIM, IDX_HEADS,
            2 * D_MODEL)

LANES = 128
PROJ_TN = 512
ZA_END = IN_SIZES[0]
ATT_END = ZA_END + sum(IN_SIZES[1:7])
ATT_Q0 = 0
ATT_K0 = ATT_Q0 + B_WIDTH
ATT_V0 = ATT_K0 + B_KV_WIDTH
ATT_QI0 = ATT_V0 + B_KV_WIDTH
ATT_KW0 = ATT_QI0 + IDX_WIDTH
ATT_WIDTH = -(-(ATT_END - ZA_END) // PROJ_TN) * PROJ_TN
assert ZA_END % PROJ_TN == 0 and ATT_KW0 + LANES <= ATT_WIDTH
GATE_ALIGNED0 = ATT_END // LANES * LANES
GATE_LANE_SHIFT = ATT_END - GATE_ALIGNED0

VMEM_LIMIT_BYTES = 56 * 1024 * 1024

_BF16 = jnp.bfloat16
_F32 = jnp.float32
_NEG_INF = float("-inf")


def _compiler_params(n_axes):
    return pltpu.CompilerParams(
        dimension_semantics=("arbitrary",) * n_axes,
        vmem_limit_bytes=VMEM_LIMIT_BYTES,
    )


def _rmsnorm_kernel(x_ref, g_ref, o_ref):
    x = x_ref[...]
    ms = jnp.mean(x * x, axis=-1, keepdims=True)
    y = x * lax.rsqrt(ms + EPS)
    o_ref[...] = (y * g_ref[...]).astype(o_ref.dtype)


def _rmsnorm(x, g, out_dtype, tm=512):
    m, d = x.shape
    return pl.pallas_call(
        _rmsnorm_kernel,
        grid=(m // tm,),
        in_specs=[pl.BlockSpec((tm, d), lambda i: (i, 0)),
                  pl.BlockSpec((1, d), lambda i: (0, 0))],
        out_specs=pl.BlockSpec((tm, d), lambda i: (i, 0)),
        out_shape=jax.ShapeDtypeStruct((m, d), out_dtype),
        compiler_params=_compiler_params(1),
        name="rmsnorm",
    )(x, g.reshape(1, d))


def _ep_identity(accs, extras):
    return accs[0]


def _ep_residual(accs, extras):
    return extras[0] + accs[0]


def _ep_gated_merge(accs, extras):
    y_a, y_b, gate_a, gate_b = accs
    return jax.nn.sigmoid(gate_a) * y_a + jax.nn.sigmoid(gate_b) * y_b


def _ep_swiglu(accs, extras):
    return jax.nn.silu(accs[0]) * accs[1]


def _mm_kernel(*refs, n_lhs, pair_lhs, epilogue):
    n_rhs = len(pair_lhs)
    lhs_refs = refs[:n_lhs]
    rhs_refs = refs[n_lhs:n_lhs + n_rhs]
    extra_refs = refs[n_lhs + n_rhs:-1]
    o_ref = refs[-1]
    accs = [jnp.dot(lhs_refs[li][...], r[...].astype(_BF16), preferred_element_type=_F32)
            for li, r in zip(pair_lhs, rhs_refs)]
    o_ref[...] = epilogue(accs, [e[...] for e in extra_refs]).astype(o_ref.dtype)


def _fused_matmul(lhs, pairs, extras, epilogue, n_out, out_dtype, tm, tn, name):
    m = lhs[0].shape[0]
    in_specs = []
    for a in lhs:
        in_specs.append(pl.BlockSpec((tm, a.shape[1]), lambda i, j: (i, 0),
                                     pipeline_mode=pl.Buffered(1)))
    for _, w, off in pairs:
        in_specs.append(pl.BlockSpec((w.shape[0], tn), functools.partial(_rhs_index, off=off)))
    for _ in extras:
        in_specs.append(pl.BlockSpec((tm, tn), lambda i, j: (i, j)))
    kern = functools.partial(_mm_kernel, n_lhs=len(lhs),
                             pair_lhs=tuple(p[0] for p in pairs), epilogue=epilogue)
    return pl.pallas_call(
        kern,
        grid=(m // tm, n_out // tn),
        in_specs=in_specs,
        out_specs=pl.BlockSpec((tm, tn), lambda i, j: (i, j)),
        out_shape=jax.ShapeDtypeStruct((m, n_out), out_dtype),
        compiler_params=_compiler_params(2),
        name=name,
    )(*lhs, *[p[1] for p in pairs], *extras)


def _rhs_index(i, j, *, off):
    return (0, j + off)


def _gate_weight_kernel(a_ref, b_ref, o_ref):
    tc = a_ref.shape[1]
    x = jnp.concatenate([a_ref[...], b_ref[...]], axis=1)
    y = pltpu.roll(x, x.shape[1] - GATE_LANE_SHIFT, 1)
    o_ref[...] = y[:, :tc].astype(o_ref.dtype)


def _gate_weights(w, tr=512, tc=1024):
    k, n_total = w.shape
    n_gate = n_total - ATT_END
    assert GATE_ALIGNED0 % tc == 0 and n_gate % tc == 0 and k % tr == 0
    a_off = GATE_ALIGNED0 // tc
    b_off = GATE_ALIGNED0 // LANES
    return pl.pallas_call(
        _gate_weight_kernel,
        grid=(k // tr, n_gate // tc),
        in_specs=[pl.BlockSpec((tr, tc), lambda i, j: (i, j + a_off)),
                  pl.BlockSpec((tr, LANES), lambda i, j: (i, (j + 1) * (tc // LANES) + b_off))],
        out_specs=pl.BlockSpec((tr, tc), lambda i, j: (i, j)),
        out_shape=jax.ShapeDtypeStruct((k, n_gate), _BF16),
        compiler_params=_compiler_params(2),
        name="gate_weights",
    )(w, w)


def _gating_kernel(za_ref, ng_ref, w_ref, bt_ref, o_ref):
    z = jax.nn.gelu(za_ref[...])
    u = z[:, :A_WIDTH]
    v = z[:, A_WIDTH:]
    ms = jnp.mean(v * v, axis=-1, keepdims=True)
    vn = ((v * lax.rsqrt(ms + EPS)) * ng_ref[...]).astype(_BF16)
    row = lax.broadcasted_iota(jnp.int32, (CHUNK, CHUNK), 0)
    col = lax.broadcasted_iota(jnp.int32, (CHUNK, CHUNK), 1)
    causal = col <= row
    for g in range(A_GROUPS):
        sl = slice(g * A_GROUP_DIM, (g + 1) * A_GROUP_DIM)
        w = jnp.where(causal, w_ref[g], 0.0).astype(_BF16)
        s = jnp.dot(w, vn[:, sl], preferred_element_type=_F32) + bt_ref[:, g:g + 1]
        o_ref[:, sl] = (u[:, sl] * s).astype(o_ref.dtype)


def _spatial_gating(za, norm_g, w_s, b_s):
    m = za.shape[0]
    return pl.pallas_call(
        _gating_kernel,
        grid=(m // CHUNK,),
        in_specs=[pl.BlockSpec((CHUNK, 2 * A_WIDTH), lambda i: (i, 0)),
                  pl.BlockSpec((1, A_WIDTH), lambda i: (0, 0)),
                  pl.BlockSpec((A_GROUPS, CHUNK, CHUNK), lambda i: (0, 0, 0)),
                  pl.BlockSpec((CHUNK, A_GROUPS), lambda i: (0, 0))],
        out_specs=pl.BlockSpec((CHUNK, A_WIDTH), lambda i: (i, 0)),
        out_shape=jax.ShapeDtypeStruct((m, A_WIDTH), _BF16),
        compiler_params=_compiler_params(1),
        name="spatial_gating",
    )(za, norm_g.reshape(1, A_WIDTH), w_s, b_s.T)


def _rope_tables(positions, rot_dim, period):
    half = rot_dim // 2
    inv_freq = ROPE_THETA ** (-jnp.arange(0, rot_dim, 2, dtype=_F32) / rot_dim)
    ang = positions.astype(_F32)[..., None] * inv_freq
    cos, sin = jnp.cos(ang), jnp.sin(ang)
    lane = jnp.arange(LANES) % period
    idx = lane % half
    cos_l = jnp.take(cos, idx, axis=-1)
    sin_l = jnp.take(sin, idx, axis=-1)
    c = jnp.where(lane < rot_dim, cos_l, 1.0)
    s1 = jnp.where((lane >= half) & (lane < rot_dim), sin_l, 0.0)
    s2 = jnp.where(lane < half, -sin_l, 0.0)
    return c, s1, s2


def _rope(x, c, s1, s2, half):
    return x * c + pltpu.roll(x, half, 1) * s1 + pltpu.roll(x, LANES - half, 1) * s2


def _prep_kernel(att_ref, cb_ref, s1b_ref, s2b_ref, ci_ref, s1i_ref, s2i_ref,
                 q_ref, k_ref, v_ref, qi_ref, ki_ref, wi_ref):
    cb, s1b, s2b = cb_ref[0], s1b_ref[0], s2b_ref[0]
    ci, s1i, s2i = ci_ref[0], s1i_ref[0], s2i_ref[0]
    for h in range(B_HEADS):
        x = att_ref[0, :, ATT_Q0 + h * LANES:ATT_Q0 + (h + 1) * LANES]
        q_ref[0, :, h * LANES:(h + 1) * LANES] = _rope(x, cb, s1b, s2b, B_ROT // 2).astype(_BF16)
    for h in range(B_KV_HEADS):
        x = att_ref[0, :, ATT_K0 + h * LANES:ATT_K0 + (h + 1) * LANES]
        k_ref[0, :, h * LANES:(h + 1) * LANES] = _rope(x, cb, s1b, s2b, B_ROT // 2).astype(_BF16)
    v_ref[0] = att_ref[0, :, ATT_V0:ATT_V0 + B_KV_WIDTH].astype(_BF16)
    lane = lax.broadcasted_iota(jnp.int32, ci.shape, 1)
    is_ki = lane < IDX_DIM
    heads_per_group = LANES // IDX_DIM
    for j in range(IDX_WIDTH // LANES):
        x = att_ref[0, :, ATT_QI0 + j * LANES:ATT_QI0 + (j + 1) * LANES]
        y = _rope(x, ci, s1i, s2i, IDX_ROT // 2)
        for r in range(heads_per_group):
            h = j * heads_per_group + r
            yr = y if r == 0 else pltpu.roll(y, LANES - r * IDX_DIM, 1)
            qi_ref[0, :, h * LANES:(h + 1) * LANES] = jnp.where(is_ki, yr, 0.0).astype(_BF16)
    x = att_ref[0, :, ATT_KW0:ATT_KW0 + LANES]
    y = _rope(x, jnp.where(is_ki, ci, 1.0), jnp.where(is_ki, s1i, 0.0),
              jnp.where(is_ki, s2i, 0.0), IDX_ROT // 2)
    ki_ref[0] = jnp.where(is_ki, y, 0.0).astype(_BF16)
    wi_ref[0] = y[:, IDX_DIM:IDX_DIM + IDX_HEADS]


def _prep_attention_inputs(att, tables_b, tables_i, ts=256):
    bsz, seq, _ = att.shape
    tab_spec = pl.BlockSpec((1, ts, LANES), lambda b, i: (b, i, 0))

    def out(width, dtype):
        return (pl.BlockSpec((1, ts, width), lambda b, i: (b, i, 0)),
                jax.ShapeDtypeStruct((bsz, seq, width), dtype))

    outs = [out(B_WIDTH, _BF16), out(B_KV_WIDTH, _BF16), out(B_KV_WIDTH, _BF16),
            out(IDX_HEADS * LANES, _BF16), out(LANES, _BF16), out(IDX_HEADS, _F32)]
    return pl.pallas_call(
        _prep_kernel,
        grid=(bsz, seq // ts),
        in_specs=[pl.BlockSpec((1, ts, ATT_WIDTH), lambda b, i: (b, i, 0))] + [tab_spec] * 6,
        out_specs=[o[0] for o in outs],
        out_shape=[o[1] for o in outs],
        compiler_params=_compiler_params(2),
        name="rope_split",
    )(att, *tables_b, *tables_i)


_NT_DIMS = (((1,), (1,)), ((), ()))


def _ordered_int_to_float(key):
    bits = jnp.where(key < 0, key ^ jnp.int32(0x7FFFFFFF), key)
    return lax.bitcast_convert_type(bits, _F32)


def _dsa_kernel(qi_ref, wi_ref, ki_ref, q_ref, k_ref, v_ref, o_ref, score_ref, bias_ref,
                *, n_sel, key_chunk):
    seq = k_ref.shape[1]
    blk = pl.program_id(1)
    qpos = blk * Q_BLOCK + lax.broadcasted_iota(jnp.int32, (Q_BLOCK, 1), 0)
    kpos = lax.broadcasted_iota(jnp.int32, (1, seq), 1)
    idx_scale = (IDX_DIM ** -0.5) * (IDX_HEADS ** -0.5)

    wi = wi_ref[0]
    for c in range(seq // key_chunk):
        ks = slice(c * key_chunk, (c + 1) * key_chunk)
        ki = ki_ref[0, ks, :]
        acc = jnp.zeros((Q_BLOCK, key_chunk), _F32)
        for h in range(IDX_HEADS):
            qh = qi_ref[0, :, h * LANES:(h + 1) * LANES]
            d = lax.dot_general(qh, ki, _NT_DIMS, preferred_element_type=_F32)
            acc = acc + wi[:, h:h + 1] * jnp.maximum(d, 0.0)
        kpos_c = c * key_chunk + lax.broadcasted_iota(jnp.int32, (1, key_chunk), 1)
        score_ref[:, ks] = jnp.where(kpos_c <= qpos, acc * idx_scale + 0.0, _NEG_INF)

    def count_ge(cand_f):
        return jnp.sum(jnp.where(score_ref[...] >= cand_f, 1.0, 0.0), axis=-1, keepdims=True)

    int_min = jnp.int32(-2 ** 31)
    base = jnp.where(count_ge(jnp.zeros((Q_BLOCK, 1), _F32)) >= n_sel, jnp.int32(0), int_min)

    def search_step(it, base):
        cand = base | lax.shift_left(jnp.int32(1), jnp.int32(30) - it)
        cnt = count_ge(_ordered_int_to_float(cand))
        return jnp.where(cnt >= n_sel, cand, base)

    base = lax.fori_loop(0, 31, search_step, base)
    thr = jnp.where(qpos + 1 < n_sel, _NEG_INF, _ordered_int_to_float(base))

    causal = kpos <= qpos
    ge = score_ref[...] >= thr
    n_ge = jnp.sum(jnp.where(causal, jnp.where(ge, 1.0, 0.0), 0.0), axis=-1, keepdims=True)
    bias_ref[...] = jnp.where(causal, jnp.where(ge, 0.0, _NEG_INF), _NEG_INF)

    @pl.when(jnp.max(n_ge) > n_sel)
    def _():
        score = score_ref[...]
        gt = score > thr
        eq = score == thr
        n_gt = jnp.sum(jnp.where(gt, 1.0, 0.0), axis=-1, keepdims=True)
        need = n_sel - n_gt
        last = jnp.zeros((Q_BLOCK, 1), jnp.int32)
        for bit in reversed(range((seq - 1).bit_length())):
            cand = last | (1 << bit)
            cnt = jnp.sum(jnp.where(eq, jnp.where(kpos < cand, 1.0, 0.0), 0.0),
                          axis=-1, keepdims=True)
            last = jnp.where(cnt < need, cand, last)
        keep_eq = jnp.where(eq, jnp.where(kpos <= last, 0.0, _NEG_INF), _NEG_INF)
        bias_ref[...] = jnp.where(kpos <= qpos, jnp.where(gt, 0.0, keep_eq), _NEG_INF)

    scale = B_HEAD_DIM ** -0.5
    grp = B_HEADS // B_KV_HEADS
    for g in range(B_KV_HEADS):
        kg = k_ref[0, :, g * B_HEAD_DIM:(g + 1) * B_HEAD_DIM]
        vg = v_ref[0, :, g * B_HEAD_DIM:(g + 1) * B_HEAD_DIM]
        for hh in range(grp):
            hs = slice((g * grp + hh) * B_HEAD_DIM, (g * grp + hh + 1) * B_HEAD_DIM)
            logits = lax.dot_general(q_ref[0, :, hs], kg, _NT_DIMS,
                                     preferred_element_type=_F32) * scale + bias_ref[...]
            mx = jnp.max(logits, axis=-1, keepdims=True)
            p = jnp.exp(logits - mx)
            denom = jnp.sum(p, axis=-1, keepdims=True)
            o = jnp.dot(p.astype(_BF16), vg, preferred_element_type=_F32)
            o_ref[0, :, hs] = (o / denom).astype(o_ref.dtype)


def _dsa_attention(q, k, v, qi, ki, wi):
    bsz, seq, _ = q.shape
    n_sel = min(TOPK_MAX, seq // 4)

    def q_spec(width):
        return pl.BlockSpec((1, Q_BLOCK, width), lambda b, i: (b, i, 0))

    def kv_spec(width):
        return pl.BlockSpec((1, seq, width), lambda b, i: (b, 0, 0))

    kern = functools.partial(_dsa_kernel, n_sel=n_sel, key_chunk=512)
    return pl.pallas_call(
        kern,
        grid=(bsz, seq // Q_BLOCK),
        in_specs=[q_spec(IDX_HEADS * LANES), q_spec(IDX_HEADS), kv_spec(LANES),
                  q_spec(B_WIDTH), kv_spec(B_KV_WIDTH), kv_spec(B_KV_WIDTH)],
        out_specs=q_spec(B_WIDTH),
        out_shape=jax.ShapeDtypeStruct((bsz, seq, B_WIDTH), _BF16),
        scratch_shapes=[pltpu.VMEM((Q_BLOCK, seq), _F32), pltpu.VMEM((Q_BLOCK, seq), _F32)],
        compiler_params=_compiler_params(2),
        name="dsa_attention",
    )(qi, wi, ki, q, k, v)


def _xattn_kernel(q_ref, k_ref, v_ref, o_ref):
    scale = X_HEAD_DIM ** -0.5
    for h in range(X_HEADS):
        hs = slice(h * X_HEAD_DIM, (h + 1) * X_HEAD_DIM)
        logits = lax.dot_general(q_ref[0, :, hs], k_ref[0, :, hs], _NT_DIMS,
                                 preferred_element_type=_F32) * scale
        mx = jnp.max(logits, axis=-1, keepdims=True)
        p = jnp.exp(logits - mx)
        denom = jnp.sum(p, axis=-1, keepdims=True)
        o = jnp.dot(p.astype(_BF16), v_ref[0, :, hs], preferred_element_type=_F32)
        o_ref[0, :, hs] = (o / denom).astype(o_ref.dtype)


def _cross_attention(q, kv, tq=512):
    bsz, seq, _ = q.shape
    m = kv.shape[1]
    return pl.pallas_call(
        _xattn_kernel,
        grid=(bsz, seq // tq),
        in_specs=[pl.BlockSpec((1, tq, X_WIDTH), lambda b, i: (b, i, 0)),
                  pl.BlockSpec((1, m, X_WIDTH), lambda b, i: (b, 0, 0)),
                  pl.BlockSpec((1, m, X_WIDTH), lambda b, i: (b, 0, 1))],
        out_specs=pl.BlockSpec((1, tq, X_WIDTH), lambda b, i: (b, i, 0)),
        out_shape=jax.ShapeDtypeStruct((bsz, seq, X_WIDTH), _BF16),
        compiler_params=_compiler_params(2),
        name="cross_attention",
    )(q, kv, kv)


def kernel(x, mem, positions, norm_mix_g, w_in, a_norm_g, a_spatial_w, a_spatial_b, p_a, p_b,
           w_out, norm_x_g, norm_mem_g, xq_w, xk_w, xv_w, xo_w, norm_ffn_g, ffn_w1, ffn_w3,
           ffn_w2, final_norm_g):
    bsz, seq, d = x.shape
    m = bsz * seq
    depth = w_in.shape[0]
    ffn_hidden = ffn_w1.shape[-1]
    tables_b = _rope_tables(positions, B_ROT, B_HEAD_DIM)
    tables_i = _rope_tables(positions, IDX_ROT, IDX_DIM)

    xf = x.reshape(m, d)
    mem_f = mem.reshape(bsz * MEM_LEN, d)
    for l in range(depth):
        w = w_in[l]
        w_gate = _gate_weights(w)

        h = _rmsnorm(xf, norm_mix_g[l], _BF16)
        za = _fused_matmul([h], [(0, w, 0)], [], _ep_identity, ZA_END, _F32,
                           1024, PROJ_TN, "proj_za")
        y_a = _spatial_gating(za, a_norm_g[l], a_spatial_w[l], a_spatial_b[l])
        att = _fused_matmul([h], [(0, w, ZA_END // PROJ_TN)], [], _ep_identity, ATT_WIDTH, _F32,
                            1024, PROJ_TN, "proj_att")
        q, k, v, qi, ki, wi = _prep_attention_inputs(att.reshape(bsz, seq, ATT_WIDTH),
                                                     tables_b, tables_i)
        y_b = _dsa_attention(q, k, v, qi, ki, wi).reshape(m, B_WIDTH)
        tn = 256
        merged = _fused_matmul(
            [y_a, y_b, h],
            [(0, p_a[l], 0), (1, p_b[l], 0), (2, w_gate, 0), (2, w_gate, d // tn)],
            [], _ep_gated_merge, d, _BF16, 1024, tn, "gated_merge")
        x1 = _fused_matmul([merged], [(0, w_out[l], 0)], [xf], _ep_residual,
                           d, _F32, 1024, 512, "out_proj")

        hx = _rmsnorm(x1, norm_x_g[l], _BF16)
        mem_n = _rmsnorm(mem_f, norm_mem_g[l], _BF16)
        qx = _fused_matmul([hx], [(0, xq_w[l], 0)], [], _ep_identity,
                           X_WIDTH, _BF16, 1024, 512, "xattn_q")
        w_kv = jnp.concatenate([xk_w[l], xv_w[l]], axis=1).astype(_BF16)
        kvx = _fused_matmul([mem_n], [(0, w_kv, 0)], [], _ep_identity,
                            2 * X_WIDTH, _BF16, bsz * MEM_LEN, 512, "xattn_kv")
        ox = _cross_attention(qx.reshape(bsz, seq, X_WIDTH),
                              kvx.reshape(bsz, MEM_LEN, 2 * X_WIDTH)).reshape(m, X_WIDTH)
        x2 = _fused_matmul([ox], [(0, xo_w[l], 0)], [x1], _ep_residual,
                           d, _F32, 1024, 1024, "xattn_out")

        h2 = _rmsnorm(x2, norm_ffn_g[l], _BF16)
        act = _fused_matmul([h2], [(0, ffn_w1[l], 0), (0, ffn_w3[l], 0)],
                            [], _ep_swiglu, ffn_hidden, _BF16, 2048, 256, "ffn_up")
        xf = _fused_matmul([act], [(0, ffn_w2[l].astype(_BF16), 0)], [x2], _ep_residual,
                           d, _F32, 1024, 256, "ffn_down")

    return _rmsnorm(xf, final_norm_g, _F32).reshape(bsz, seq, d)
```

```python
import functools

import jax
import jax.numpy as jnp
from jax import lax
from jax.experimental import pallas as pl
from jax.experimental.pallas import tpu as pltpu

D_MODEL = 4096
MEM_LEN = 256
EPS = 1e-6
ROPE_THETA = 500000.0
CHUNK = 128
A_GROUPS = 16
A_WIDTH = D_MODEL // 2
A_GROUP_DIM = A_WIDTH // A_GROUPS
B_HEADS = 16
B_HEAD_DIM = 128
B_KV_HEADS = 4
B_WIDTH = B_HEADS * B_HEAD_DIM
B_KV_WIDTH = B_KV_HEADS * B_HEAD_DIM
B_ROT = B_HEAD_DIM // 4
IDX_HEADS = 16
IDX_DIM = 64
IDX_WIDTH = IDX_HEADS * IDX_DIM
IDX_ROT = IDX_DIM // 4
TOPK_MAX = 256
Q_BLOCK = 128
X_HEADS = 4
X_HEAD_DIM = 256
X_WIDTH = X_HEADS * X_HEAD_DIM
IN_SIZES = (2 * A_WIDTH, B_WIDTH, B_KV_WIDTH, B_KV_WIDTH, IDX_WIDTH, IDX_DIM, IDX_HEADS,
            2 * D_MODEL)

LANES = 128
SUBLANES = 8
KEY_CHUNK = 512
PROJ_TN = 512
ZA_END = IN_SIZES[0]
ATT_END = ZA_END + sum(IN_SIZES[1:7])
ATT_Q0 = 0
ATT_K0 = ATT_Q0 + B_WIDTH
ATT_V0 = ATT_K0 + B_KV_WIDTH
ATT_QI0 = ATT_V0 + B_KV_WIDTH
ATT_KW0 = ATT_QI0 + IDX_WIDTH
ATT_WIDTH = -(-(ATT_END - ZA_END) // PROJ_TN) * PROJ_TN
assert ZA_END % PROJ_TN == 0 and ATT_KW0 + LANES <= ATT_WIDTH
GATE_ALIGNED0 = ATT_END // LANES * LANES
GATE_LANE_SHIFT = ATT_END - GATE_ALIGNED0

VMEM_LIMIT_BYTES = 56 * 1024 * 1024

_BF16 = jnp.bfloat16
_F32 = jnp.float32
_NEG_INF = float("-inf")


def _compiler_params(n_axes):
    return pltpu.CompilerParams(
        dimension_semantics=("arbitrary",) * n_axes,
        vmem_limit_bytes=VMEM_LIMIT_BYTES,
    )


def _rmsnorm_kernel(x_ref, g_ref, o_ref):
    x = x_ref[...]
    ms = jnp.mean(x * x, axis=-1, keepdims=True)
    y = x * lax.rsqrt(ms + EPS)
    o_ref[...] = (y * g_ref[...]).astype(o_ref.dtype)


def _rmsnorm(x, g, out_dtype, tm=512):
    m, d = x.shape
    return pl.pallas_call(
        _rmsnorm_kernel,
        grid=(m // tm,),
        in_specs=[pl.BlockSpec((tm, d), lambda i: (i, 0)),
                  pl.BlockSpec((1, d), lambda i: (0, 0))],
        out_specs=pl.BlockSpec((tm, d), lambda i: (i, 0)),
        out_shape=jax.ShapeDtypeStruct((m, d), out_dtype),
        compiler_params=_compiler_params(1),
        name="rmsnorm",
    )(x, g.reshape(1, d))


def _ep_identity(accs, extras):
    return accs[0]


def _ep_residual(accs, extras):
    return extras[0] + accs[0]


def _ep_gated_merge(accs, extras):
    y_a, y_b, gate_a, gate_b = accs
    return jax.nn.sigmoid(gate_a) * y_a + jax.nn.sigmoid(gate_b) * y_b


def _ep_swiglu(accs, extras):
    return jax.nn.silu(accs[0]) * accs[1]


def _mm_kernel(*refs, n_lhs, pair_lhs, epilogue):
    n_rhs = len(pair_lhs)
    lhs_refs = refs[:n_lhs]
    rhs_refs = refs[n_lhs:n_lhs + n_rhs]
    extra_refs = refs[n_lhs + n_rhs:-1]
    o_ref = refs[-1]
    accs = [jnp.dot(lhs_refs[li][...], r[...].astype(_BF16), preferred_element_type=_F32)
            for li, r in zip(pair_lhs, rhs_refs)]
    o_ref[...] = epilogue(accs, [e[...] for e in extra_refs]).astype(o_ref.dtype)


def _fused_matmul(lhs, pairs, extras, epilogue, n_out, out_dtype, tm, tn, name):
    m = lhs[0].shape[0]
    in_specs = []
    rhs_arrays = []
    for a in lhs:
        in_specs.append(pl.BlockSpec((tm, a.shape[1]), lambda i, j: (i, 0),
                                     pipeline_mode=pl.Buffered(1)))
    for _, w, off in pairs:
        if isinstance(w, tuple):
            w, layer = w
            in_specs.append(pl.BlockSpec((None, w.shape[1], tn),
                                         functools.partial(_rhs_index3, off=off, layer=layer)))
        else:
            in_specs.append(pl.BlockSpec((w.shape[0], tn),
                                         functools.partial(_rhs_index, off=off)))
        rhs_arrays.append(w)
    for _ in extras:
        in_specs.append(pl.BlockSpec((tm, tn), lambda i, j: (i, j)))
    kern = functools.partial(_mm_kernel, n_lhs=len(lhs),
                             pair_lhs=tuple(p[0] for p in pairs), epilogue=epilogue)
    return pl.pallas_call(
        kern,
        grid=(m // tm, n_out // tn),
        in_specs=in_specs,
        out_specs=pl.BlockSpec((tm, tn), lambda i, j: (i, j)),
        out_shape=jax.ShapeDtypeStruct((m, n_out), out_dtype),
        compiler_params=_compiler_params(2),
        name=name,
    )(*lhs, *rhs_arrays, *extras)


def _rhs_index(i, j, *, off):
    return (0, j + off)


def _rhs_index3(i, j, *, off, layer):
    return (layer, 0, j + off)


def _gate_weight_kernel(a_ref, b_ref, o_ref):
    tc = a_ref.shape[1]
    x = jnp.concatenate([a_ref[...], b_ref[...]], axis=1)
    y = pltpu.roll(x, x.shape[1] - GATE_LANE_SHIFT, 1)
    o_ref[...] = y[:, :tc].astype(o_ref.dtype)


def _gate_weights(w_stack, layer, tr=512, tc=1024):
    _, k, n_total = w_stack.shape
    n_gate = n_total - ATT_END
    assert GATE_ALIGNED0 % tc == 0 and n_gate % tc == 0 and k % tr == 0
    a_off = GATE_ALIGNED0 // tc
    b_off = GATE_ALIGNED0 // LANES
    return pl.pallas_call(
        _gate_weight_kernel,
        grid=(k // tr, n_gate // tc),
        in_specs=[pl.BlockSpec((None, tr, tc), lambda i, j: (layer, i, j + a_off)),
                  pl.BlockSpec((None, tr, LANES),
                               lambda i, j: (layer, i, (j + 1) * (tc // LANES) + b_off))],
        out_specs=pl.BlockSpec((tr, tc), lambda i, j: (i, j)),
        out_shape=jax.ShapeDtypeStruct((k, n_gate), _BF16),
        compiler_params=_compiler_params(2),
        name="gate_weights",
    )(w_stack, w_stack)


def _gating_kernel(za_ref, ng_ref, w_ref, bt_ref, o_ref):
    z = jax.nn.gelu(za_ref[...])
    u = z[:, :A_WIDTH]
    v = z[:, A_WIDTH:]
    ms = jnp.mean(v * v, axis=-1, keepdims=True)
    vn = ((v * lax.rsqrt(ms + EPS)) * ng_ref[...]).astype(_BF16)
    row = lax.broadcasted_iota(jnp.int32, (CHUNK, CHUNK), 0)
    col = lax.broadcasted_iota(jnp.int32, (CHUNK, CHUNK), 1)
    causal = col <= row
    for g in range(A_GROUPS):
        sl = slice(g * A_GROUP_DIM, (g + 1) * A_GROUP_DIM)
        w = jnp.where(causal, w_ref[g], 0.0).astype(_BF16)
        s = jnp.dot(w, vn[:, sl], preferred_element_type=_F32) + bt_ref[:, g:g + 1]
        o_ref[:, sl] = (u[:, sl] * s).astype(o_ref.dtype)


def _spatial_gating(za, norm_g, w_s, b_s):
    m = za.shape[0]
    return pl.pallas_call(
        _gating_kernel,
        grid=(m // CHUNK,),
        in_specs=[pl.BlockSpec((CHUNK, 2 * A_WIDTH), lambda i: (i, 0)),
                  pl.BlockSpec((1, A_WIDTH), lambda i: (0, 0)),
                  pl.BlockSpec((A_GROUPS, CHUNK, CHUNK), lambda i: (0, 0, 0)),
                  pl.BlockSpec((CHUNK, A_GROUPS), lambda i: (0, 0))],
        out_specs=pl.BlockSpec((CHUNK, A_WIDTH), lambda i: (i, 0)),
        out_shape=jax.ShapeDtypeStruct((m, A_WIDTH), _BF16),
        compiler_params=_compiler_params(1),
        name="spatial_gating",
    )(za, norm_g.reshape(1, A_WIDTH), w_s, b_s.T)


def _rope_tables(positions, rot_dim, period):
    half = rot_dim // 2
    inv_freq = ROPE_THETA ** (-jnp.arange(0, rot_dim, 2, dtype=_F32) / rot_dim)
    ang = positions.astype(_F32)[..., None] * inv_freq
    cos, sin = jnp.cos(ang), jnp.sin(ang)
    lane = jnp.arange(LANES) % period
    idx = lane % half
    cos_l = jnp.take(cos, idx, axis=-1)
    sin_l = jnp.take(sin, idx, axis=-1)
    c = jnp.where(lane < rot_dim, cos_l, 1.0)
    s1 = jnp.where((lane >= half) & (lane < rot_dim), sin_l, 0.0)
    s2 = jnp.where(lane < half, -sin_l, 0.0)
    return c, s1, s2


def _rope(x, c, s1, s2, half):
    return x * c + pltpu.roll(x, half, 1) * s1 + pltpu.roll(x, LANES - half, 1) * s2


def _prep_kernel(att_ref, cb_ref, s1b_ref, s2b_ref, ci_ref, s1i_ref, s2i_ref,
                 q_ref, k_ref, vt_ref, qi_ref, ki_ref, wi_ref):
    cb, s1b, s2b = cb_ref[0], s1b_ref[0], s2b_ref[0]
    ci, s1i, s2i = ci_ref[0], s1i_ref[0], s2i_ref[0]
    for h in range(B_HEADS):
        x = att_ref[0, :, ATT_Q0 + h * LANES:ATT_Q0 + (h + 1) * LANES]
        q_ref[0, :, h * LANES:(h + 1) * LANES] = _rope(x, cb, s1b, s2b, B_ROT // 2).astype(_BF16)
    for h in range(B_KV_HEADS):
        x = att_ref[0, :, ATT_K0 + h * LANES:ATT_K0 + (h + 1) * LANES]
        k_ref[0, :, h * LANES:(h + 1) * LANES] = _rope(x, cb, s1b, s2b, B_ROT // 2).astype(_BF16)
    vt_ref[0, 0] = att_ref[0, :, ATT_V0:ATT_V0 + B_KV_WIDTH].T.astype(_BF16)
    lane = lax.broadcasted_iota(jnp.int32, ci.shape, 1)
    is_ki = lane < IDX_DIM
    heads_per_group = LANES // IDX_DIM
    for j in range(IDX_WIDTH // LANES):
        x = att_ref[0, :, ATT_QI0 + j * LANES:ATT_QI0 + (j + 1) * LANES]
        y = _rope(x, ci, s1i, s2i, IDX_ROT // 2)
        for r in range(heads_per_group):
            h = j * heads_per_group + r
            yr = y if r == 0 else pltpu.roll(y, LANES - r * IDX_DIM, 1)
            qi_ref[0, :, h * LANES:(h + 1) * LANES] = jnp.where(is_ki, yr, 0.0).astype(_BF16)
    x = att_ref[0, :, ATT_KW0:ATT_KW0 + LANES]
    y = _rope(x, jnp.where(is_ki, ci, 1.0), jnp.where(is_ki, s1i, 0.0),
              jnp.where(is_ki, s2i, 0.0), IDX_ROT // 2)
    ki_ref[0] = jnp.where(is_ki, y, 0.0).astype(_BF16)
    wi_ref[0] = y[:, IDX_DIM:IDX_DIM + IDX_HEADS]


def _prep_attention_inputs(att, tables_b, tables_i, ts=256):
    bsz, seq, _ = att.shape
    tab_spec = pl.BlockSpec((1, ts, LANES), lambda b, i: (b, i, 0))
    per_chunk = KEY_CHUNK // ts

    def out(width, dtype):
        return (pl.BlockSpec((1, ts, width), lambda b, i: (b, i, 0)),
                jax.ShapeDtypeStruct((bsz, seq, width), dtype))

    vt_out = (pl.BlockSpec((1, 1, B_KV_WIDTH, ts),
                           lambda b, i: (b, i // per_chunk, 0, i % per_chunk)),
              jax.ShapeDtypeStruct((bsz, seq // KEY_CHUNK, B_KV_WIDTH, KEY_CHUNK), _BF16))
    outs = [out(B_WIDTH, _BF16), out(B_KV_WIDTH, _BF16), vt_out,
            out(IDX_HEADS * LANES, _BF16), out(LANES, _BF16), out(IDX_HEADS, _F32)]
    return pl.pallas_call(
        _prep_kernel,
        grid=(bsz, seq // ts),
        in_specs=[pl.BlockSpec((1, ts, ATT_WIDTH), lambda b, i: (b, i, 0))] + [tab_spec] * 6,
        out_specs=[o[0] for o in outs],
        out_shape=[o[1] for o in outs],
        compiler_params=_compiler_params(2),
        name="rope_split",
    )(att, *tables_b, *tables_i)


_NT_DIMS = (((1,), (1,)), ((), ()))


def _ordered_int_to_float(key):
    bits = jnp.where(key < 0, key ^ jnp.int32(0x7FFFFFFF), key)
    return lax.bitcast_convert_type(bits, _F32)


def _fold_rows(x, op):
    r, c = x.shape
    return op(x.reshape(r // SUBLANES, SUBLANES, c), axis=0)


def _dsa_kernel(qi_ref, wit_ref, ki_ref, q_ref, k_ref, vt_ref, o_ref,
                score_ref, bias_ref, logit_ref, *, n_sel):
    t = Q_BLOCK
    blk = pl.program_id(1)
    n_chunks = (blk * t + t + KEY_CHUNK - 1) // KEY_CHUNK
    qpos = blk * t + lax.broadcasted_iota(jnp.int32, (1, t), 1)
    kiota = lax.broadcasted_iota(jnp.int32, (KEY_CHUNK, 1), 0)
    idx_scale = (IDX_DIM ** -0.5) * (IDX_HEADS ** -0.5)

    def rows(c):
        return pl.ds(pl.multiple_of(c * KEY_CHUNK, KEY_CHUNK), KEY_CHUNK)

    def for_chunks(body, init):
        return lax.fori_loop(0, n_chunks, body, init)

    def indexer_chunk(c, carry):
        ki = ki_ref[0, rows(c), :]
        acc = jnp.zeros((KEY_CHUNK, t), _F32)
        for h in range(0, IDX_HEADS, 2):
            qpair = jnp.concatenate([qi_ref[0, :, h * LANES:(h + 1) * LANES],
                                     qi_ref[0, :, (h + 1) * LANES:(h + 2) * LANES]], axis=0)
            d = lax.dot_general(ki, qpair, _NT_DIMS, preferred_element_type=_F32)
            acc = acc + wit_ref[0, h:h + 1, :] * jnp.maximum(d[:, :t], 0.0)
            acc = acc + wit_ref[0, h + 1:h + 2, :] * jnp.maximum(d[:, t:], 0.0)
        kpos = c * KEY_CHUNK + kiota
        score_ref[rows(c), :] = jnp.where(kpos <= qpos, acc * idx_scale + 0.0, _NEG_INF)
        return carry

    for_chunks(indexer_chunk, 0)

    def count(indicator):
        def chunk(c, part):
            return part + _fold_rows(indicator(score_ref[rows(c), :], c), jnp.sum)
        part = for_chunks(chunk, jnp.zeros((SUBLANES, t), _F32))
        return jnp.sum(part, axis=0, keepdims=True)

    def count_ge(cand_f):
        return count(lambda s, c: jnp.where(s >= cand_f, 1.0, 0.0))

    int_min = jnp.int32(-2 ** 31)
    base = jnp.where(count_ge(jnp.zeros((1, t), _F32)) >= n_sel, jnp.int32(0), int_min)

    def search_step(it, base):
        cand = base | lax.shift_left(jnp.int32(1), jnp.int32(30) - it)
        cnt = count_ge(_ordered_int_to_float(cand))
        return jnp.where(cnt >= n_sel, cand, base)

    base = lax.fori_loop(0, 31, search_step, base)
    thr = jnp.where(qpos + 1 < n_sel, _NEG_INF, _ordered_int_to_float(base))

    def bias_chunk(c, part):
        s = score_ref[rows(c), :]
        sel = jnp.where(c * KEY_CHUNK + kiota <= qpos, jnp.where(s >= thr, 1.0, 0.0), 0.0)
        bias_ref[rows(c), :] = jnp.where(sel > 0.0, 0.0, _NEG_INF)
        return part + _fold_rows(sel, jnp.sum)

    n_ge = jnp.sum(for_chunks(bias_chunk, jnp.zeros((SUBLANES, t), _F32)), axis=0, keepdims=True)

    @pl.when(jnp.max(n_ge) > n_sel)
    def _():
        n_gt = count(lambda s, c: jnp.where(s > thr, 1.0, 0.0))
        need = n_sel - n_gt
        last = jnp.zeros((1, t), jnp.int32)
        for bit in reversed(range((k_ref.shape[1] - 1).bit_length())):
            cand = last | (1 << bit)
            cnt = count(lambda s, c, cand=cand: jnp.where(
                s == thr, jnp.where(c * KEY_CHUNK + kiota < cand, 1.0, 0.0), 0.0))
            last = jnp.where(cnt < need, cand, last)

        def tie_chunk(c, carry):
            s = score_ref[rows(c), :]
            kpos = c * KEY_CHUNK + kiota
            keep_eq = jnp.where(s == thr, jnp.where(kpos <= last, 0.0, _NEG_INF), _NEG_INF)
            bias_ref[rows(c), :] = jnp.where(kpos <= qpos, jnp.where(s > thr, 0.0, keep_eq),
                                             _NEG_INF)
            return carry

        for_chunks(tie_chunk, 0)

    grp = B_HEADS // B_KV_HEADS
    exp2_scale = (B_HEAD_DIM ** -0.5) * 1.4426950408889634
    for g in range(B_KV_HEADS):
        gs = slice(g * B_HEAD_DIM, (g + 1) * B_HEAD_DIM)
        qg = jnp.concatenate(
            [q_ref[0, :, (g * grp + hh) * B_HEAD_DIM:(g * grp + hh + 1) * B_HEAD_DIM]
             for hh in range(grp)], axis=0)

        def logit_chunk(c, mx, gs=gs, qg=qg):
            l = lax.dot_general(k_ref[0, rows(c), gs], qg, _NT_DIMS,
                                preferred_element_type=_F32)
            l = l + jnp.concatenate([bias_ref[rows(c), :]] * grp, axis=1)
            logit_ref[rows(c), :] = l
            return jnp.maximum(mx, _fold_rows(l, jnp.max))

        mx = for_chunks(logit_chunk, jnp.full((SUBLANES, grp * t), _NEG_INF, _F32))
        mx = jnp.max(mx, axis=0, keepdims=True)

        def pv_chunk(c, carry, gs=gs, mx=mx):
            acc, den = carry
            p = jnp.exp2((logit_ref[rows(c), :] - mx) * exp2_scale)
            den = den + _fold_rows(p, jnp.sum)
            acc = acc + jnp.dot(vt_ref[0, c, gs, :], p.astype(_BF16),
                                preferred_element_type=_F32)
            return acc, den

        acc, den = for_chunks(pv_chunk, (jnp.zeros((B_HEAD_DIM, grp * t), _F32),
                                         jnp.zeros((SUBLANES, grp * t), _F32)))
        out_t = acc / jnp.sum(den, axis=0, keepdims=True)
        for hh in range(grp):
            hs = slice((g * grp + hh) * B_HEAD_DIM, (g * grp + hh + 1) * B_HEAD_DIM)
            o_ref[0, :, hs] = out_t[:, hh * t:(hh + 1) * t].T.astype(o_ref.dtype)


def _dsa_attention(q, k, vt, qi, ki, wit):
    bsz, seq, _ = q.shape
    assert Q_BLOCK == LANES and seq % KEY_CHUNK == 0
    n_sel = min(TOPK_MAX, seq // 4)
    grp = B_HEADS // B_KV_HEADS

    def q_spec(width):
        return pl.BlockSpec((1, Q_BLOCK, width), lambda b, i: (b, i, 0))

    def kv_spec(width):
        return pl.BlockSpec((1, seq, width), lambda b, i: (b, 0, 0))

    return pl.pallas_call(
        functools.partial(_dsa_kernel, n_sel=n_sel),
        grid=(bsz, seq // Q_BLOCK),
        in_specs=[q_spec(IDX_HEADS * LANES),
                  pl.BlockSpec((1, IDX_HEADS, Q_BLOCK), lambda b, i: (b, 0, i)),
                  kv_spec(LANES), q_spec(B_WIDTH), kv_spec(B_KV_WIDTH),
                  pl.BlockSpec((1, seq // KEY_CHUNK, B_KV_WIDTH, KEY_CHUNK),
                               lambda b, i: (b, 0, 0, 0))],
        out_specs=q_spec(B_WIDTH),
        out_shape=jax.ShapeDtypeStruct((bsz, seq, B_WIDTH), _BF16),
        scratch_shapes=[pltpu.VMEM((seq, Q_BLOCK), _F32), pltpu.VMEM((seq, Q_BLOCK), _F32),
                        pltpu.VMEM((seq, grp * Q_BLOCK), _F32)],
        compiler_params=_compiler_params(2),
        name="dsa_attention",
    )(qi, wit, ki, q, k, vt)


def _xattn_kernel(q_ref, k_ref, v_ref, o_ref):
    scale = X_HEAD_DIM ** -0.5
    for h in range(X_HEADS):
        hs = slice(h * X_HEAD_DIM, (h + 1) * X_HEAD_DIM)
        logits = lax.dot_general(q_ref[0, :, hs], k_ref[0, :, hs], _NT_DIMS,
                                 preferred_element_type=_F32) * scale
        mx = jnp.max(logits, axis=-1, keepdims=True)
        p = jnp.exp(logits - mx)
        denom = jnp.sum(p, axis=-1, keepdims=True)
        o = jnp.dot(p.astype(_BF16), v_ref[0, :, hs], preferred_element_type=_F32)
        o_ref[0, :, hs] = (o / denom).astype(o_ref.dtype)


def _cross_attention(q, kv, tq=512):
    bsz, seq, _ = q.shape
    m = kv.shape[1]
    return pl.pallas_call(
        _xattn_kernel,
        grid=(bsz, seq // tq),
        in_specs=[pl.BlockSpec((1, tq, X_WIDTH), lambda b, i: (b, i, 0)),
                  pl.BlockSpec((1, m, X_WIDTH), lambda b, i: (b, 0, 0)),
                  pl.BlockSpec((1, m, X_WIDTH), lambda b, i: (b, 0, 1))],
        out_specs=pl.BlockSpec((1, tq, X_WIDTH), lambda b, i: (b, i, 0)),
        out_shape=jax.ShapeDtypeStruct((bsz, seq, X_WIDTH), _BF16),
        compiler_params=_compiler_params(2),
        name="cross_attention",
    )(q, kv, kv)


def kernel(x, mem, positions, norm_mix_g, w_in, a_norm_g, a_spatial_w, a_spatial_b, p_a, p_b,
           w_out, norm_x_g, norm_mem_g, xq_w, xk_w, xv_w, xo_w, norm_ffn_g, ffn_w1, ffn_w3,
           ffn_w2, final_norm_g):
    bsz, seq, d = x.shape
    m = bsz * seq
    depth = w_in.shape[0]
    ffn_hidden = ffn_w1.shape[-1]
    tables_b = _rope_tables(positions, B_ROT, B_HEAD_DIM)
    tables_i = _rope_tables(positions, IDX_ROT, IDX_DIM)

    xf = x.reshape(m, d)
    mem_f = mem.reshape(bsz * MEM_LEN, d)
    for l in range(depth):
        w_gate = _gate_weights(w_in, l)

        h = _rmsnorm(xf, norm_mix_g[l], _BF16)
        za = _fused_matmul([h], [(0, (w_in, l), 0)], [], _ep_identity, ZA_END, _F32,
                           1024, PROJ_TN, "proj_za")
        y_a = _spatial_gating(za, a_norm_g[l], a_spatial_w[l], a_spatial_b[l])
        att = _fused_matmul([h], [(0, (w_in, l), ZA_END // PROJ_TN)], [], _ep_identity,
                            ATT_WIDTH, _F32, 1024, PROJ_TN, "proj_att")
        q, k, vt, qi, ki, wi = _prep_attention_inputs(att.reshape(bsz, seq, ATT_WIDTH),
                                                      tables_b, tables_i)
        y_b = _dsa_attention(q, k, vt, qi, ki, wi.transpose(0, 2, 1)).reshape(m, B_WIDTH)
        tn = 256
        merged = _fused_matmul(
            [y_a, y_b, h],
            [(0, (p_a, l), 0), (1, (p_b, l), 0), (2, w_gate, 0), (2, w_gate, d // tn)],
            [], _ep_gated_merge, d, _BF16, 1024, tn, "gated_merge")
        x1 = _fused_matmul([merged], [(0, (w_out, l), 0)], [xf], _ep_residual,
                           d, _F32, 1024, 512, "out_proj")

        hx = _rmsnorm(x1, norm_x_g[l], _BF16)
        mem_n = _rmsnorm(mem_f, norm_mem_g[l], _BF16)
        qx = _fused_matmul([hx], [(0, (xq_w, l), 0)], [], _ep_identity,
                           X_WIDTH, _BF16, 1024, 512, "xattn_q")
        w_kv = jnp.concatenate([xk_w[l], xv_w[l]], axis=1).astype(_BF16)
        kvx = _fused_matmul([mem_n], [(0, w_kv, 0)], [], _ep_identity,
                            2 * X_WIDTH, _BF16, bsz * MEM_LEN, 512, "xattn_kv")
        ox = _cross_attention(qx.reshape(bsz, seq, X_WIDTH),
                              kvx.reshape(bsz, MEM_LEN, 2 * X_WIDTH)).reshape(m, X_WIDTH)
        x2 = _fused_matmul([ox], [(0, (xo_w, l), 0)], [x1], _ep_residual,
                           d, _F32, 1024, 1024, "xattn_out")

        h2 = _rmsnorm(x2, norm_ffn_g[l], _BF16)
        act = _fused_matmul([h2], [(0, (ffn_w1, l), 0), (0, (ffn_w3, l), 0)],
                            [], _ep_swiglu, ffn_hidden, _BF16, 2048, 256, "ffn_up")
        xf = _fused_matmul([act], [(0, ffn_w2[l].astype(_BF16), 0)], [x2], _ep_residual,
                           d, _F32, 1024, 256, "ffn_down")

    return _rmsnorm(xf, final_norm_g, _F32).reshape(bsz, seq, d)
```

```python
import functools
from typing import NamedTuple, Optional

import jax
import jax.numpy as jnp
from jax import lax
from jax.experimental import pallas as pl
from jax.experimental.pallas import tpu as pltpu

D_MODEL = 4096
MEM_LEN = 256
EPS = 1e-6
ROPE_THETA = 500000.0
CHUNK = 128
A_GROUPS = 16
A_WIDTH = D_MODEL // 2
A_GROUP_DIM = A_WIDTH // A_GROUPS
B_HEADS = 16
B_HEAD_DIM = 128
B_KV_HEADS = 4
B_WIDTH = B_HEADS * B_HEAD_DIM
B_KV_WIDTH = B_KV_HEADS * B_HEAD_DIM
B_ROT = B_HEAD_DIM // 4
IDX_HEADS = 16
IDX_DIM = 64
IDX_WIDTH = IDX_HEADS * IDX_DIM
IDX_ROT = IDX_DIM // 4
TOPK_MAX = 256
Q_BLOCK = 128
X_HEADS = 4
X_HEAD_DIM = 256
X_WIDTH = X_HEADS * X_HEAD_DIM
IN_SIZES = (2 * A_WIDTH, B_WIDTH, B_KV_WIDTH, B_KV_WIDTH, IDX_WIDTH, IDX_DIM, IDX_HEADS,
            2 * D_MODEL)

LANES = 128
SUBLANES = 8
KEY_CHUNK = 512
PROJ_TN = 512
ZA_END = IN_SIZES[0]
ATT_END = ZA_END + sum(IN_SIZES[1:7])
ATT_Q0 = 0
ATT_K0 = ATT_Q0 + B_WIDTH
ATT_V0 = ATT_K0 + B_KV_WIDTH
ATT_QI0 = ATT_V0 + B_KV_WIDTH
ATT_KW0 = ATT_QI0 + IDX_WIDTH
ATT_WIDTH = -(-(ATT_END - ZA_END) // PROJ_TN) * PROJ_TN
assert ZA_END % PROJ_TN == 0 and ATT_KW0 + LANES <= ATT_WIDTH

VMEM_LIMIT_BYTES = 56 * 1024 * 1024

_BF16 = jnp.bfloat16
_F32 = jnp.float32
_NEG_INF = float("-inf")
_NN_DIMS = (((1,), (0,)), ((), ()))
_NT_DIMS = (((1,), (1,)), ((), ()))


def _compiler_params(n_axes):
    return pltpu.CompilerParams(
        dimension_semantics=("arbitrary",) * n_axes,
        vmem_limit_bytes=VMEM_LIMIT_BYTES,
    )


def _rmsnorm_kernel(x_ref, g_ref, o_ref):
    x = x_ref[...]
    ms = jnp.mean(x * x, axis=-1, keepdims=True)
    y = x * lax.rsqrt(ms + EPS)
    o_ref[...] = (y * g_ref[...]).astype(o_ref.dtype)


def _rmsnorm(x, g, out_dtype, tm=512):
    m, d = x.shape
    return pl.pallas_call(
        _rmsnorm_kernel,
        grid=(m // tm,),
        in_specs=[pl.BlockSpec((tm, d), lambda i: (i, 0)),
                  pl.BlockSpec((1, d), lambda i: (0, 0))],
        out_specs=pl.BlockSpec((tm, d), lambda i: (i, 0)),
        out_shape=jax.ShapeDtypeStruct((m, d), out_dtype),
        compiler_params=_compiler_params(1),
        name="rmsnorm",
    )(x, g.reshape(1, d))


def _ep_identity(accs, extras):
    return accs[0]


def _ep_residual(accs, extras):
    return extras[0] + accs[0]


def _ep_gated_merge(accs, extras):
    y_a, y_b, gate_a, gate_b = accs
    return jax.nn.sigmoid(gate_a) * y_a + jax.nn.sigmoid(gate_b) * y_b


def _ep_swiglu(accs, extras):
    return jax.nn.silu(accs[0]) * accs[1]


class Rhs(NamedTuple):
    array: jax.Array
    layer: Optional[int] = None
    col0: int = 0
    transposed: bool = False


def _rhs_spec(r, tn):
    if r.transposed:
        k = r.array.shape[2]
        if r.col0 % tn == 0:
            return pl.BlockSpec((None, tn, k), lambda i, j: (r.layer, j + r.col0 // tn, 0))
        assert r.col0 % SUBLANES == 0
        return pl.BlockSpec((pl.Element(1), pl.Element(tn), pl.Element(k)),
                            lambda i, j: (r.layer, (r.col0 // SUBLANES + j * (tn // SUBLANES))
                                          * SUBLANES, 0))
    assert r.col0 % tn == 0
    off = r.col0 // tn
    if r.layer is None:
        return pl.BlockSpec((r.array.shape[0], tn), lambda i, j: (0, j + off))
    return pl.BlockSpec((None, r.array.shape[1], tn), lambda i, j: (r.layer, 0, j + off))


def _mm_kernel(*refs, n_lhs, pair_lhs, rhs_transposed, epilogue):
    n_rhs = len(pair_lhs)
    lhs_refs = refs[:n_lhs]
    rhs_refs = refs[n_lhs:n_lhs + n_rhs]
    extra_refs = refs[n_lhs + n_rhs:-1]
    o_ref = refs[-1]
    accs = []
    for li, r, tr in zip(pair_lhs, rhs_refs, rhs_transposed):
        dims = _NT_DIMS if tr else _NN_DIMS
        w = r[0] if len(r.shape) == 3 else r[...]
        accs.append(lax.dot_general(lhs_refs[li][...], w.astype(_BF16), dims,
                                    preferred_element_type=_F32))
    o_ref[...] = epilogue(accs, [e[...] for e in extra_refs]).astype(o_ref.dtype)


def _fused_matmul(lhs, pairs, extras, epilogue, n_out, out_dtype, tm, tn, name):
    m = lhs[0].shape[0]
    in_specs = [pl.BlockSpec((tm, a.shape[1]), lambda i, j: (i, 0), pipeline_mode=pl.Buffered(1))
                for a in lhs]
    in_specs += [_rhs_spec(r, tn) for _, r in pairs]
    in_specs += [pl.BlockSpec((tm, tn), lambda i, j: (i, j)) for _ in extras]
    kern = functools.partial(_mm_kernel, n_lhs=len(lhs), pair_lhs=tuple(p[0] for p in pairs),
                             rhs_transposed=tuple(p[1].transposed for p in pairs),
                             epilogue=epilogue)
    return pl.pallas_call(
        kern,
        grid=(m // tm, n_out // tn),
        in_specs=in_specs,
        out_specs=pl.BlockSpec((tm, tn), lambda i, j: (i, j)),
        out_shape=jax.ShapeDtypeStruct((m, n_out), out_dtype),
        compiler_params=_compiler_params(2),
        name=name,
    )(*lhs, *[p[1].array for p in pairs], *extras)


def _gating_kernel(za_ref, ng_ref, w_ref, bt_ref, o_ref):
    z = jax.nn.gelu(za_ref[...])
    u = z[:, :A_WIDTH]
    v = z[:, A_WIDTH:]
    ms = jnp.mean(v * v, axis=-1, keepdims=True)
    vn = ((v * lax.rsqrt(ms + EPS)) * ng_ref[...]).astype(_BF16)
    row = lax.broadcasted_iota(jnp.int32, (CHUNK, CHUNK), 0)
    col = lax.broadcasted_iota(jnp.int32, (CHUNK, CHUNK), 1)
    causal = col <= row
    for g in range(A_GROUPS):
        sl = slice(g * A_GROUP_DIM, (g + 1) * A_GROUP_DIM)
        w = jnp.where(causal, w_ref[g], 0.0).astype(_BF16)
        s = jnp.dot(w, vn[:, sl], preferred_element_type=_F32) + bt_ref[:, g:g + 1]
        o_ref[:, sl] = (u[:, sl] * s).astype(o_ref.dtype)


def _spatial_gating(za, norm_g, w_s, b_s):
    m = za.shape[0]
    return pl.pallas_call(
        _gating_kernel,
        grid=(m // CHUNK,),
        in_specs=[pl.BlockSpec((CHUNK, 2 * A_WIDTH), lambda i: (i, 0)),
                  pl.BlockSpec((1, A_WIDTH), lambda i: (0, 0)),
                  pl.BlockSpec((A_GROUPS, CHUNK, CHUNK), lambda i: (0, 0, 0)),
                  pl.BlockSpec((CHUNK, A_GROUPS), lambda i: (0, 0))],
        out_specs=pl.BlockSpec((CHUNK, A_WIDTH), lambda i: (i, 0)),
        out_shape=jax.ShapeDtypeStruct((m, A_WIDTH), _BF16),
        compiler_params=_compiler_params(1),
        name="spatial_gating",
    )(za, norm_g.reshape(1, A_WIDTH), w_s, b_s.T)


def _rope_tables(positions, rot_dim, period):
    half = rot_dim // 2
    inv_freq = ROPE_THETA ** (-jnp.arange(0, rot_dim, 2, dtype=_F32) / rot_dim)
    ang = positions.astype(_F32)[..., None] * inv_freq
    cos, sin = jnp.cos(ang), jnp.sin(ang)
    lane = jnp.arange(LANES) % period
    idx = lane % half
    cos_l = jnp.take(cos, idx, axis=-1)
    sin_l = jnp.take(sin, idx, axis=-1)
    c = jnp.where(lane < rot_dim, cos_l, 1.0)
    s1 = jnp.where((lane >= half) & (lane < rot_dim), sin_l, 0.0)
    s2 = jnp.where(lane < half, -sin_l, 0.0)
    return c, s1, s2


def _rope(x, c, s1, s2, half):
    return x * c + pltpu.roll(x, half, 1) * s1 + pltpu.roll(x, LANES - half, 1) * s2


def _prep_kernel(att_ref, cb_ref, s1b_ref, s2b_ref, ci_ref, s1i_ref, s2i_ref,
                 q_ref, k_ref, vt_ref, qi_ref, ki_ref, wi_ref):
    cb, s1b, s2b = cb_ref[0], s1b_ref[0], s2b_ref[0]
    ci, s1i, s2i = ci_ref[0], s1i_ref[0], s2i_ref[0]
    for h in range(B_HEADS):
        x = att_ref[0, :, ATT_Q0 + h * LANES:ATT_Q0 + (h + 1) * LANES]
        q_ref[0, :, h * LANES:(h + 1) * LANES] = _rope(x, cb, s1b, s2b, B_ROT // 2).astype(_BF16)
    for h in range(B_KV_HEADS):
        x = att_ref[0, :, ATT_K0 + h * LANES:ATT_K0 + (h + 1) * LANES]
        k_ref[0, :, h * LANES:(h + 1) * LANES] = _rope(x, cb, s1b, s2b, B_ROT // 2).astype(_BF16)
    vt_ref[0, 0] = att_ref[0, :, ATT_V0:ATT_V0 + B_KV_WIDTH].T.astype(_BF16)
    lane = lax.broadcasted_iota(jnp.int32, ci.shape, 1)
    is_ki = lane < IDX_DIM
    heads_per_group = LANES // IDX_DIM
    for j in range(IDX_WIDTH // LANES):
        x = att_ref[0, :, ATT_QI0 + j * LANES:ATT_QI0 + (j + 1) * LANES]
        y = _rope(x, ci, s1i, s2i, IDX_ROT // 2)
        for r in range(heads_per_group):
            h = j * heads_per_group + r
            yr = y if r == 0 else pltpu.roll(y, LANES - r * IDX_DIM, 1)
            qi_ref[0, :, h * LANES:(h + 1) * LANES] = jnp.where(is_ki, yr, 0.0).astype(_BF16)
    x = att_ref[0, :, ATT_KW0:ATT_KW0 + LANES]
    y = _rope(x, jnp.where(is_ki, ci, 1.0), jnp.where(is_ki, s1i, 0.0),
              jnp.where(is_ki, s2i, 0.0), IDX_ROT // 2)
    ki_ref[0] = jnp.where(is_ki, y, 0.0).astype(_BF16)
    wi_ref[0] = y[:, IDX_DIM:IDX_DIM + IDX_HEADS]


def _prep_attention_inputs(att, tables_b, tables_i, ts=256):
    bsz, seq, _ = att.shape
    tab_spec = pl.BlockSpec((1, ts, LANES), lambda b, i: (b, i, 0))
    per_chunk = KEY_CHUNK // ts

    def out(width, dtype):
        return (pl.BlockSpec((1, ts, width), lambda b, i: (b, i, 0)),
                jax.ShapeDtypeStruct((bsz, seq, width), dtype))

    vt_out = (pl.BlockSpec((1, 1, B_KV_WIDTH, ts),
                           lambda b, i: (b, i // per_chunk, 0, i % per_chunk)),
              jax.ShapeDtypeStruct((bsz, seq // KEY_CHUNK, B_KV_WIDTH, KEY_CHUNK), _BF16))
    outs = [out(B_WIDTH, _BF16), out(B_KV_WIDTH, _BF16), vt_out,
            out(IDX_HEADS * LANES, _BF16), out(LANES, _BF16), out(IDX_HEADS, _F32)]
    return pl.pallas_call(
        _prep_kernel,
        grid=(bsz, seq // ts),
        in_specs=[pl.BlockSpec((1, ts, ATT_WIDTH), lambda b, i: (b, i, 0))] + [tab_spec] * 6,
        out_specs=[o[0] for o in outs],
        out_shape=[o[1] for o in outs],
        compiler_params=_compiler_params(2),
        name="rope_split",
    )(att, *tables_b, *tables_i)


def _ordered_int_to_float(key):
    bits = jnp.where(key < 0, key ^ jnp.int32(0x7FFFFFFF), key)
    return lax.bitcast_convert_type(bits, _F32)


def _fold_rows(x, op):
    while x.shape[0] > SUBLANES:
        half = x.shape[0] // 2
        x = op(x[:half], x[half:])
    return x


def _dsa_kernel(qi_ref, wit_ref, ki_ref, q_ref, k_ref, vt_ref, o_ref,
                score_ref, bias_ref, logit_ref, acc_ref, *, n_sel):
    t = Q_BLOCK
    blk = pl.program_id(1)
    n_chunks = (blk * t + t + KEY_CHUNK - 1) // KEY_CHUNK
    qpos = blk * t + lax.broadcasted_iota(jnp.int32, (1, t), 1)
    kiota = lax.broadcasted_iota(jnp.int32, (KEY_CHUNK, 1), 0)
    idx_scale = (IDX_DIM ** -0.5) * (IDX_HEADS ** -0.5)

    def rows(c):
        return pl.ds(pl.multiple_of(c * KEY_CHUNK, KEY_CHUNK), KEY_CHUNK)

    def for_chunks(body, init):
        return lax.fori_loop(0, n_chunks, body, init)

    def indexer_chunk(c, carry):
        ki = ki_ref[0, rows(c), :]
        acc = jnp.zeros((KEY_CHUNK, t), _F32)
        for h in range(0, IDX_HEADS, 2):
            qpair = jnp.concatenate([qi_ref[0, :, h * LANES:(h + 1) * LANES],
                                     qi_ref[0, :, (h + 1) * LANES:(h + 2) * LANES]], axis=0)
            d = lax.dot_general(ki, qpair, _NT_DIMS, preferred_element_type=_F32)
            acc = acc + wit_ref[0, h:h + 1, :] * jnp.maximum(d[:, :t], 0.0)
            acc = acc + wit_ref[0, h + 1:h + 2, :] * jnp.maximum(d[:, t:], 0.0)
        kpos = c * KEY_CHUNK + kiota
        score_ref[rows(c), :] = jnp.where(kpos <= qpos, acc * idx_scale + 0.0, _NEG_INF)
        return carry

    for_chunks(indexer_chunk, 0)

    def count(indicator):
        def chunk(c, part):
            return part + _fold_rows(indicator(score_ref[rows(c), :], c), jnp.add)
        part = for_chunks(chunk, jnp.zeros((SUBLANES, t), _F32))
        return jnp.sum(part, axis=0, keepdims=True)

    def count_ge(cand_f):
        return count(lambda s, c: jnp.where(s >= cand_f, 1.0, 0.0))

    int_min = jnp.int32(-2 ** 31)
    base = jnp.where(count_ge(jnp.zeros((1, t), _F32)) >= n_sel, jnp.int32(0), int_min)

    def search_step(it, base):
        cand = base | lax.shift_left(jnp.int32(1), jnp.int32(30) - it)
        cnt = count_ge(_ordered_int_to_float(cand))
        return jnp.where(cnt >= n_sel, cand, base)

    base = lax.fori_loop(0, 31, search_step, base)
    thr = jnp.where(qpos + 1 < n_sel, _NEG_INF, _ordered_int_to_float(base))

    def bias_chunk(c, part):
        s = score_ref[rows(c), :]
        sel = jnp.where(c * KEY_CHUNK + kiota <= qpos, jnp.where(s >= thr, 1.0, 0.0), 0.0)
        bias_ref[rows(c), :] = jnp.where(sel > 0.0, 0.0, _NEG_INF)
        return part + _fold_rows(sel, jnp.add)

    n_ge = jnp.sum(for_chunks(bias_chunk, jnp.zeros((SUBLANES, t), _F32)), axis=0, keepdims=True)

    @pl.when(jnp.max(n_ge) > n_sel)
    def _():
        n_gt = count(lambda s, c: jnp.where(s > thr, 1.0, 0.0))
        need = n_sel - n_gt
        last = jnp.zeros((1, t), jnp.int32)
        for bit in reversed(range((k_ref.shape[1] - 1).bit_length())):
            cand = last | (1 << bit)
            cnt = count(lambda s, c, cand=cand: jnp.where(
                s == thr, jnp.where(c * KEY_CHUNK + kiota < cand, 1.0, 0.0), 0.0))
            last = jnp.where(cnt < need, cand, last)

        def tie_chunk(c, carry):
            s = score_ref[rows(c), :]
            kpos = c * KEY_CHUNK + kiota
            keep_eq = jnp.where(s == thr, jnp.where(kpos <= last, 0.0, _NEG_INF), _NEG_INF)
            bias_ref[rows(c), :] = jnp.where(kpos <= qpos, jnp.where(s > thr, 0.0, keep_eq),
                                             _NEG_INF)
            return carry

        for_chunks(tie_chunk, 0)

    grp = B_HEADS // B_KV_HEADS
    gw = grp * t
    exp2_scale = (B_HEAD_DIM ** -0.5) * 1.4426950408889634

    def logit_chunk(c, mx):
        bias = jnp.concatenate([bias_ref[rows(c), :]] * grp, axis=1)
        folded = []
        for g in range(B_KV_HEADS):
            qg = jnp.concatenate(
                [q_ref[0, :, (g * grp + hh) * B_HEAD_DIM:(g * grp + hh + 1) * B_HEAD_DIM]
                 for hh in range(grp)], axis=0)
            l = lax.dot_general(k_ref[0, rows(c), g * B_HEAD_DIM:(g + 1) * B_HEAD_DIM], qg,
                                _NT_DIMS, preferred_element_type=_F32) + bias
            logit_ref[rows(c), g * gw:(g + 1) * gw] = l
            folded.append(_fold_rows(l, jnp.maximum))
        return jnp.maximum(mx, jnp.concatenate(folded, axis=1))

    mx = for_chunks(logit_chunk, jnp.full((SUBLANES, B_HEADS * t), _NEG_INF, _F32))
    mx = jnp.max(mx, axis=0, keepdims=True)

    acc_ref[...] = jnp.zeros_like(acc_ref)

    def pv_chunk(c, den):
        folded = []
        for g in range(B_KV_HEADS):
            cols = slice(g * gw, (g + 1) * gw)
            p = jnp.exp2((logit_ref[rows(c), cols] - mx[:, cols]) * exp2_scale)
            folded.append(_fold_rows(p, jnp.add))
            acc_ref[:, cols] += jnp.dot(vt_ref[0, c, g * B_HEAD_DIM:(g + 1) * B_HEAD_DIM, :],
                                        p.astype(_BF16), preferred_element_type=_F32)
        return den + jnp.concatenate(folded, axis=1)

    den = for_chunks(pv_chunk, jnp.zeros((SUBLANES, B_HEADS * t), _F32))
    out_t = acc_ref[...] / jnp.sum(den, axis=0, keepdims=True)
    for h in range(B_HEADS):
        o_ref[0, :, h * B_HEAD_DIM:(h + 1) * B_HEAD_DIM] = (
            out_t[:, h * t:(h + 1) * t].T.astype(o_ref.dtype))


def _dsa_attention(q, k, vt, qi, ki, wit):
    bsz, seq, _ = q.shape
    assert Q_BLOCK == LANES and seq % KEY_CHUNK == 0
    n_sel = min(TOPK_MAX, seq // 4)

    def q_spec(width):
        return pl.BlockSpec((1, Q_BLOCK, width), lambda b, i: (b, i, 0))

    def kv_spec(width):
        return pl.BlockSpec((1, seq, width), lambda b, i: (b, 0, 0))

    return pl.pallas_call(
        functools.partial(_dsa_kernel, n_sel=n_sel),
        grid=(bsz, seq // Q_BLOCK),
        in_specs=[q_spec(IDX_HEADS * LANES),
                  pl.BlockSpec((1, IDX_HEADS, Q_BLOCK), lambda b, i: (b, 0, i)),
                  kv_spec(LANES), q_spec(B_WIDTH), kv_spec(B_KV_WIDTH),
                  pl.BlockSpec((1, seq // KEY_CHUNK, B_KV_WIDTH, KEY_CHUNK),
                               lambda b, i: (b, 0, 0, 0))],
        out_specs=q_spec(B_WIDTH),
        out_shape=jax.ShapeDtypeStruct((bsz, seq, B_WIDTH), _BF16),
        scratch_shapes=[pltpu.VMEM((seq, Q_BLOCK), _F32), pltpu.VMEM((seq, Q_BLOCK), _F32),
                        pltpu.VMEM((seq, B_HEADS * Q_BLOCK), _F32),
                        pltpu.VMEM((B_HEAD_DIM, B_HEADS * Q_BLOCK), _F32)],
        compiler_params=_compiler_params(2),
        name="dsa_attention",
    )(qi, wit, ki, q, k, vt)


def _xattn_kernel(q_ref, k_ref, v_ref, o_ref):
    scale = X_HEAD_DIM ** -0.5
    for h in range(X_HEADS):
        hs = slice(h * X_HEAD_DIM, (h + 1) * X_HEAD_DIM)
        logits = lax.dot_general(q_ref[0, :, hs], k_ref[0, :, hs], _NT_DIMS,
                                 preferred_element_type=_F32) * scale
        mx = jnp.max(logits, axis=-1, keepdims=True)
        p = jnp.exp(logits - mx)
        denom = jnp.sum(p, axis=-1, keepdims=True)
        o = jnp.dot(p.astype(_BF16), v_ref[0, :, hs], preferred_element_type=_F32)
        o_ref[0, :, hs] = (o / denom).astype(o_ref.dtype)


def _cross_attention(q, kv, tq=512):
    bsz, seq, _ = q.shape
    m = kv.shape[1]
    return pl.pallas_call(
        _xattn_kernel,
        grid=(bsz, seq // tq),
        in_specs=[pl.BlockSpec((1, tq, X_WIDTH), lambda b, i: (b, i, 0)),
                  pl.BlockSpec((1, m, X_WIDTH), lambda b, i: (b, 0, 0)),
                  pl.BlockSpec((1, m, X_WIDTH), lambda b, i: (b, 0, 1))],
        out_specs=pl.BlockSpec((1, tq, X_WIDTH), lambda b, i: (b, i, 0)),
        out_shape=jax.ShapeDtypeStruct((bsz, seq, X_WIDTH), _BF16),
        compiler_params=_compiler_params(2),
        name="cross_attention",
    )(q, kv, kv)


def kernel(x, mem, positions, norm_mix_g, w_in, a_norm_g, a_spatial_w, a_spatial_b, p_a, p_b,
           w_out, norm_x_g, norm_mem_g, xq_w, xk_w, xv_w, xo_w, norm_ffn_g, ffn_w1, ffn_w3,
           ffn_w2, final_norm_g):
    bsz, seq, d = x.shape
    m = bsz * seq
    depth = w_in.shape[0]
    ffn_hidden = ffn_w1.shape[-1]
    tables_b = _rope_tables(positions, B_ROT, B_HEAD_DIM)
    tables_i = _rope_tables(positions, IDX_ROT, IDX_DIM)

    xf = x.reshape(m, d)
    mem_f = mem.reshape(bsz * MEM_LEN, d)
    w_in_t = jnp.transpose(w_in, (0, 2, 1))
    for l in range(depth):
        def w_in_cols(col0):
            return Rhs(w_in_t, l, col0, transposed=True)

        h = _rmsnorm(xf, norm_mix_g[l], _BF16)
        za = _fused_matmul([h], [(0, w_in_cols(0))], [], _ep_identity, ZA_END, _F32,
                           1024, PROJ_TN, "proj_za")
        y_a = _spatial_gating(za, a_norm_g[l], a_spatial_w[l], a_spatial_b[l])
        att = _fused_matmul([h], [(0, w_in_cols(ZA_END))], [], _ep_identity,
                            ATT_WIDTH, _F32, 1024, PROJ_TN, "proj_att")
        q, k, vt, qi, ki, wi = _prep_attention_inputs(att.reshape(bsz, seq, ATT_WIDTH),
                                                      tables_b, tables_i)
        y_b = _dsa_attention(q, k, vt, qi, ki, wi.transpose(0, 2, 1)).reshape(m, B_WIDTH)
        merged = _fused_matmul(
            [y_a, y_b, h],
            [(0, Rhs(p_a, l)), (1, Rhs(p_b, l)), (2, w_in_cols(ATT_END)),
             (2, w_in_cols(ATT_END + d))],
            [], _ep_gated_merge, d, _BF16, 1024, 256, "gated_merge")
        x1 = _fused_matmul([merged], [(0, Rhs(w_out, l))], [xf], _ep_residual,
                           d, _F32, 1024, 512, "out_proj")

        hx = _rmsnorm(x1, norm_x_g[l], _BF16)
        mem_n = _rmsnorm(mem_f, norm_mem_g[l], _BF16)
        qx = _fused_matmul([hx], [(0, Rhs(xq_w, l))], [], _ep_identity,
                           X_WIDTH, _BF16, 1024, 512, "xattn_q")
        w_kv = jnp.concatenate([xk_w[l], xv_w[l]], axis=1).astype(_BF16)
        kvx = _fused_matmul([mem_n], [(0, Rhs(w_kv))], [], _ep_identity,
                            2 * X_WIDTH, _BF16, bsz * MEM_LEN, 512, "xattn_kv")
        ox = _cross_attention(qx.reshape(bsz, seq, X_WIDTH),
                              kvx.reshape(bsz, MEM_LEN, 2 * X_WIDTH)).reshape(m, X_WIDTH)
        x2 = _fused_matmul([ox], [(0, Rhs(xo_w, l))], [x1], _ep_residual,
                           d, _F32, 1024, 1024, "xattn_out")

        h2 = _rmsnorm(x2, norm_ffn_g[l], _BF16)
        act = _fused_matmul([h2], [(0, Rhs(ffn_w1, l)), (0, Rhs(ffn_w3, l))],
                            [], _ep_swiglu, ffn_hidden, _BF16, 2048, 256, "ffn_up")
        xf = _fused_matmul([act], [(0, Rhs(ffn_w2[l].astype(_BF16)))], [x2], _ep_residual,
                           d, _F32, 1024, 256, "ffn_down")

    return _rmsnorm(xf, final_norm_g, _F32).reshape(bsz, seq, d)
```

```python
import functools
from typing import NamedTuple, Optional

import jax
import jax.numpy as jnp
from jax import lax
from jax.experimental import pallas as pl
from jax.experimental.pallas import tpu as pltpu

D_MODEL = 4096
MEM_LEN = 256
EPS = 1e-6
ROPE_THETA = 500000.0
CHUNK = 128
A_GROUPS = 16
A_WIDTH = D_MODEL // 2
A_GROUP_DIM = A_WIDTH // A_GROUPS
B_HEADS = 16
B_HEAD_DIM = 128
B_KV_HEADS = 4
B_WIDTH = B_HEADS * B_HEAD_DIM
B_KV_WIDTH = B_KV_HEADS * B_HEAD_DIM
B_ROT = B_HEAD_DIM // 4
IDX_HEADS = 16
IDX_DIM = 64
IDX_WIDTH = IDX_HEADS * IDX_DIM
IDX_ROT = IDX_DIM // 4
TOPK_MAX = 256
Q_BLOCK = 128
X_HEADS = 4
X_HEAD_DIM = 256
X_WIDTH = X_HEADS * X_HEAD_DIM
IN_SIZES = (2 * A_WIDTH, B_WIDTH, B_KV_WIDTH, B_KV_WIDTH, IDX_WIDTH, IDX_DIM, IDX_HEADS,
            2 * D_MODEL)

LANES = 128
SUBLANES = 8
KEY_CHUNK = 512
PROJ_TN = 256
ZA_END = IN_SIZES[0]
ATT_END = ZA_END + sum(IN_SIZES[1:7])
ATT_Q0 = 0
ATT_K0 = ATT_Q0 + B_WIDTH
ATT_V0 = ATT_K0 + B_KV_WIDTH
ATT_QI0 = ATT_V0 + B_KV_WIDTH
ATT_KW0 = ATT_QI0 + IDX_WIDTH
ATT_WIDTH = -(-(ATT_END - ZA_END) // PROJ_TN) * PROJ_TN
assert ZA_END % PROJ_TN == 0 and ATT_KW0 + LANES <= ATT_WIDTH

VMEM_LIMIT_BYTES = 56 * 1024 * 1024

_BF16 = jnp.bfloat16
_F32 = jnp.float32
_NEG_INF = float("-inf")
_NN_DIMS = (((1,), (0,)), ((), ()))
_NT_DIMS = (((1,), (1,)), ((), ()))


def _compiler_params(n_axes):
    return pltpu.CompilerParams(
        dimension_semantics=("arbitrary",) * n_axes,
        vmem_limit_bytes=VMEM_LIMIT_BYTES,
    )


def _rmsnorm_kernel(x_ref, g_ref, o_ref):
    x = x_ref[...]
    ms = jnp.mean(x * x, axis=-1, keepdims=True)
    y = x * lax.rsqrt(ms + EPS)
    o_ref[...] = (y * g_ref[...]).astype(o_ref.dtype)


def _rmsnorm(x, g, out_dtype, tm=512):
    m, d = x.shape
    return pl.pallas_call(
        _rmsnorm_kernel,
        grid=(m // tm,),
        in_specs=[pl.BlockSpec((tm, d), lambda i: (i, 0)),
                  pl.BlockSpec((1, d), lambda i: (0, 0))],
        out_specs=pl.BlockSpec((tm, d), lambda i: (i, 0)),
        out_shape=jax.ShapeDtypeStruct((m, d), out_dtype),
        compiler_params=_compiler_params(1),
        name="rmsnorm",
    )(x, g.reshape(1, d))


def _ep_identity(accs, extras):
    return accs[0]


def _ep_residual(accs, extras):
    return extras[0] + accs[0]


def _ep_gated_merge(accs, extras):
    y_a, y_b, gate_a, gate_b = accs
    return jax.nn.sigmoid(gate_a) * y_a + jax.nn.sigmoid(gate_b) * y_b


def _ep_swiglu(accs, extras):
    return jax.nn.silu(accs[0]) * accs[1]


class Rhs(NamedTuple):
    array: jax.Array
    layer: Optional[int] = None
    col0: int = 0
    transposed: bool = False


def _rhs_spec(r, tn):
    if r.transposed:
        k = r.array.shape[2]
        if r.col0 % tn == 0:
            return pl.BlockSpec((None, tn, k), lambda i, j: (r.layer, j + r.col0 // tn, 0))
        assert r.col0 % SUBLANES == 0
        return pl.BlockSpec((pl.Element(1), pl.Element(tn), pl.Element(k)),
                            lambda i, j: (r.layer, (r.col0 // SUBLANES + j * (tn // SUBLANES))
                                          * SUBLANES, 0))
    assert r.col0 % tn == 0
    off = r.col0 // tn
    if r.layer is None:
        return pl.BlockSpec((r.array.shape[0], tn), lambda i, j: (0, j + off))
    return pl.BlockSpec((None, r.array.shape[1], tn), lambda i, j: (r.layer, 0, j + off))


def _mm_kernel(*refs, n_lhs, pair_lhs, rhs_transposed, epilogue):
    n_rhs = len(pair_lhs)
    lhs_refs = refs[:n_lhs]
    rhs_refs = refs[n_lhs:n_lhs + n_rhs]
    extra_refs = refs[n_lhs + n_rhs:-1]
    o_ref = refs[-1]
    accs = []
    for li, r, tr in zip(pair_lhs, rhs_refs, rhs_transposed):
        dims = _NT_DIMS if tr else _NN_DIMS
        w = r[0] if len(r.shape) == 3 else r[...]
        accs.append(lax.dot_general(lhs_refs[li][...], w.astype(_BF16), dims,
                                    preferred_element_type=_F32))
    o_ref[...] = epilogue(accs, [e[...] for e in extra_refs]).astype(o_ref.dtype)


def _fused_matmul(lhs, pairs, extras, epilogue, n_out, out_dtype, tm, tn, name):
    m = lhs[0].shape[0]
    in_specs = [pl.BlockSpec((tm, a.shape[1]), lambda i, j: (i, 0), pipeline_mode=pl.Buffered(1))
                for a in lhs]
    in_specs += [_rhs_spec(r, tn) for _, r in pairs]
    in_specs += [pl.BlockSpec((tm, tn), lambda i, j: (i, j)) for _ in extras]
    kern = functools.partial(_mm_kernel, n_lhs=len(lhs), pair_lhs=tuple(p[0] for p in pairs),
                             rhs_transposed=tuple(p[1].transposed for p in pairs),
                             epilogue=epilogue)
    return pl.pallas_call(
        kern,
        grid=(m // tm, n_out // tn),
        in_specs=in_specs,
        out_specs=pl.BlockSpec((tm, tn), lambda i, j: (i, j)),
        out_shape=jax.ShapeDtypeStruct((m, n_out), out_dtype),
        compiler_params=_compiler_params(2),
        name=name,
    )(*lhs, *[p[1].array for p in pairs], *extras)


def _gating_kernel(za_ref, ng_ref, w_ref, bt_ref, o_ref):
    z = jax.nn.gelu(za_ref[...])
    u = z[:, :A_WIDTH]
    v = z[:, A_WIDTH:]
    ms = jnp.mean(v * v, axis=-1, keepdims=True)
    vn = ((v * lax.rsqrt(ms + EPS)) * ng_ref[...]).astype(_BF16)
    row = lax.broadcasted_iota(jnp.int32, (CHUNK, CHUNK), 0)
    col = lax.broadcasted_iota(jnp.int32, (CHUNK, CHUNK), 1)
    causal = col <= row
    for g in range(A_GROUPS):
        sl = slice(g * A_GROUP_DIM, (g + 1) * A_GROUP_DIM)
        w = jnp.where(causal, w_ref[g], 0.0).astype(_BF16)
        s = jnp.dot(w, vn[:, sl], preferred_element_type=_F32) + bt_ref[:, g:g + 1]
        o_ref[:, sl] = (u[:, sl] * s).astype(o_ref.dtype)


def _spatial_gating(za, norm_g, w_s, b_s):
    m = za.shape[0]
    return pl.pallas_call(
        _gating_kernel,
        grid=(m // CHUNK,),
        in_specs=[pl.BlockSpec((CHUNK, 2 * A_WIDTH), lambda i: (i, 0)),
                  pl.BlockSpec((1, A_WIDTH), lambda i: (0, 0)),
                  pl.BlockSpec((A_GROUPS, CHUNK, CHUNK), lambda i: (0, 0, 0)),
                  pl.BlockSpec((CHUNK, A_GROUPS), lambda i: (0, 0))],
        out_specs=pl.BlockSpec((CHUNK, A_WIDTH), lambda i: (i, 0)),
        out_shape=jax.ShapeDtypeStruct((m, A_WIDTH), _BF16),
        compiler_params=_compiler_params(1),
        name="spatial_gating",
    )(za, norm_g.reshape(1, A_WIDTH), w_s, b_s.T)


def _rope_tables(positions, rot_dim, period):
    half = rot_dim // 2
    inv_freq = ROPE_THETA ** (-jnp.arange(0, rot_dim, 2, dtype=_F32) / rot_dim)
    ang = positions.astype(_F32)[..., None] * inv_freq
    cos, sin = jnp.cos(ang), jnp.sin(ang)
    lane = jnp.arange(LANES) % period
    idx = lane % half
    cos_l = jnp.take(cos, idx, axis=-1)
    sin_l = jnp.take(sin, idx, axis=-1)
    c = jnp.where(lane < rot_dim, cos_l, 1.0)
    s1 = jnp.where((lane >= half) & (lane < rot_dim), sin_l, 0.0)
    s2 = jnp.where(lane < half, -sin_l, 0.0)
    return c, s1, s2


def _rope(x, c, s1, s2, half):
    return x * c + pltpu.roll(x, half, 1) * s1 + pltpu.roll(x, LANES - half, 1) * s2


def _prep_kernel(att_ref, cb_ref, s1b_ref, s2b_ref, ci_ref, s1i_ref, s2i_ref,
                 q_ref, k_ref, vt_ref, qi_ref, ki_ref, wi_ref):
    cb, s1b, s2b = cb_ref[0], s1b_ref[0], s2b_ref[0]
    ci, s1i, s2i = ci_ref[0], s1i_ref[0], s2i_ref[0]
    for h in range(B_HEADS):
        x = att_ref[0, :, ATT_Q0 + h * LANES:ATT_Q0 + (h + 1) * LANES]
        q_ref[0, :, h * LANES:(h + 1) * LANES] = _rope(x, cb, s1b, s2b, B_ROT // 2).astype(_BF16)
    for h in range(B_KV_HEADS):
        x = att_ref[0, :, ATT_K0 + h * LANES:ATT_K0 + (h + 1) * LANES]
        k_ref[0, :, h * LANES:(h + 1) * LANES] = _rope(x, cb, s1b, s2b, B_ROT // 2).astype(_BF16)
    vt_ref[0, 0] = att_ref[0, :, ATT_V0:ATT_V0 + B_KV_WIDTH].T.astype(_BF16)
    lane = lax.broadcasted_iota(jnp.int32, ci.shape, 1)
    is_ki = lane < IDX_DIM
    heads_per_group = LANES // IDX_DIM
    for j in range(IDX_WIDTH // LANES):
        x = att_ref[0, :, ATT_QI0 + j * LANES:ATT_QI0 + (j + 1) * LANES]
        y = _rope(x, ci, s1i, s2i, IDX_ROT // 2)
        for r in range(heads_per_group):
            h = j * heads_per_group + r
            yr = y if r == 0 else pltpu.roll(y, LANES - r * IDX_DIM, 1)
            qi_ref[0, :, h * LANES:(h + 1) * LANES] = jnp.where(is_ki, yr, 0.0).astype(_BF16)
    x = att_ref[0, :, ATT_KW0:ATT_KW0 + LANES]
    y = _rope(x, jnp.where(is_ki, ci, 1.0), jnp.where(is_ki, s1i, 0.0),
              jnp.where(is_ki, s2i, 0.0), IDX_ROT // 2)
    ki_ref[0] = jnp.where(is_ki, y, 0.0).astype(_BF16)
    wi_ref[0] = y[:, IDX_DIM:IDX_DIM + IDX_HEADS]


def _prep_attention_inputs(att, tables_b, tables_i, ts=256):
    bsz, seq, _ = att.shape
    tab_spec = pl.BlockSpec((1, ts, LANES), lambda b, i: (b, i, 0))
    per_chunk = KEY_CHUNK // ts

    def out(width, dtype):
        return (pl.BlockSpec((1, ts, width), lambda b, i: (b, i, 0)),
                jax.ShapeDtypeStruct((bsz, seq, width), dtype))

    vt_out = (pl.BlockSpec((1, 1, B_KV_WIDTH, ts),
                           lambda b, i: (b, i // per_chunk, 0, i % per_chunk)),
              jax.ShapeDtypeStruct((bsz, seq // KEY_CHUNK, B_KV_WIDTH, KEY_CHUNK), _BF16))
    outs = [out(B_WIDTH, _BF16), out(B_KV_WIDTH, _BF16), vt_out,
            out(IDX_HEADS * LANES, _BF16), out(LANES, _BF16), out(IDX_HEADS, _F32)]
    return pl.pallas_call(
        _prep_kernel,
        grid=(bsz, seq // ts),
        in_specs=[pl.BlockSpec((1, ts, ATT_WIDTH), lambda b, i: (b, i, 0))] + [tab_spec] * 6,
        out_specs=[o[0] for o in outs],
        out_shape=[o[1] for o in outs],
        compiler_params=_compiler_params(2),
        name="rope_split",
    )(att, *tables_b, *tables_i)


def _ordered_int_to_float(key):
    bits = jnp.where(key < 0, key ^ jnp.int32(0x7FFFFFFF), key)
    return lax.bitcast_convert_type(bits, _F32)


def _fold_rows(x, op):
    while x.shape[0] > SUBLANES:
        half = x.shape[0] // 2
        x = op(x[:half], x[half:])
    return x


def _dsa_kernel(qi_ref, wit_ref, ki_ref, q_ref, k_ref, vt_ref, o_ref,
                score_ref, bias_ref, logit_ref, acc_ref, *, n_sel):
    t = Q_BLOCK
    blk = pl.program_id(1)
    n_chunks = (blk * t + t + KEY_CHUNK - 1) // KEY_CHUNK
    qpos = blk * t + lax.broadcasted_iota(jnp.int32, (1, t), 1)
    kiota = lax.broadcasted_iota(jnp.int32, (KEY_CHUNK, 1), 0)
    idx_scale = (IDX_DIM ** -0.5) * (IDX_HEADS ** -0.5)

    def rows(c):
        return pl.ds(pl.multiple_of(c * KEY_CHUNK, KEY_CHUNK), KEY_CHUNK)

    def for_chunks(body, init):
        return lax.fori_loop(0, n_chunks, body, init)

    def indexer_chunk(c, carry):
        ki = ki_ref[0, rows(c), :]
        acc = jnp.zeros((KEY_CHUNK, t), _F32)
        for h in range(0, IDX_HEADS, 2):
            qpair = jnp.concatenate([qi_ref[0, :, h * LANES:(h + 1) * LANES],
                                     qi_ref[0, :, (h + 1) * LANES:(h + 2) * LANES]], axis=0)
            d = lax.dot_general(ki, qpair, _NT_DIMS, preferred_element_type=_F32)
            acc = acc + wit_ref[0, h:h + 1, :] * jnp.maximum(d[:, :t], 0.0)
            acc = acc + wit_ref[0, h + 1:h + 2, :] * jnp.maximum(d[:, t:], 0.0)
        kpos = c * KEY_CHUNK + kiota
        score_ref[rows(c), :] = jnp.where(kpos <= qpos, acc * idx_scale + 0.0, _NEG_INF)
        return carry

    for_chunks(indexer_chunk, 0)

    def count(indicator):
        def chunk(c, part):
            return part + _fold_rows(indicator(score_ref[rows(c), :], c), jnp.add)
        part = for_chunks(chunk, jnp.zeros((SUBLANES, t), _F32))
        return jnp.sum(part, axis=0, keepdims=True)

    def count_ge(cand_f):
        return count(lambda s, c: jnp.where(s >= cand_f, 1.0, 0.0))

    int_min = jnp.int32(-2 ** 31)
    base = jnp.where(count_ge(jnp.zeros((1, t), _F32)) >= n_sel, jnp.int32(0), int_min)

    def search_step(it, base):
        cand = base | lax.shift_left(jnp.int32(1), jnp.int32(30) - it)
        cnt = count_ge(_ordered_int_to_float(cand))
        return jnp.where(cnt >= n_sel, cand, base)

    base = lax.fori_loop(0, 31, search_step, base)
    thr = jnp.where(qpos + 1 < n_sel, _NEG_INF, _ordered_int_to_float(base))

    def bias_chunk(c, part):
        s = score_ref[rows(c), :]
        sel = jnp.where(c * KEY_CHUNK + kiota <= qpos, jnp.where(s >= thr, 1.0, 0.0), 0.0)
        bias_ref[rows(c), :] = jnp.where(sel > 0.0, 0.0, _NEG_INF)
        return part + _fold_rows(sel, jnp.add)

    n_ge = jnp.sum(for_chunks(bias_chunk, jnp.zeros((SUBLANES, t), _F32)), axis=0, keepdims=True)

    @pl.when(jnp.max(n_ge) > n_sel)
    def _():
        nxt = _ordered_int_to_float(base + 1)
        tied = n_ge > n_sel
        n_above = count(lambda s, c: jnp.where(s >= nxt, 1.0, 0.0))
        need = jnp.where(tied, n_sel - n_above, 0.0)
        no_index = float(k_ref.shape[1])
        front_v = jnp.where(tied, jnp.inf, _NEG_INF)
        front_i = jnp.where(tied, -1.0, no_index)

        def beyond(s, kposf, front_v, front_i):
            after = jnp.where(s < front_v, 1.0,
                              jnp.where(s == front_v, jnp.where(kposf > front_i, 1.0, 0.0), 0.0))
            return jnp.where(s >= thr, jnp.where(s < nxt, after, 0.0), 0.0)

        def kposf(c):
            return (c * KEY_CHUNK + kiota).astype(_F32)

        def advance(_, carry):
            front_v, front_i, need = carry

            def best_value(c, part):
                s = score_ref[rows(c), :]
                cand = jnp.where(beyond(s, kposf(c), front_v, front_i) > 0.0, s, _NEG_INF)
                return jnp.maximum(part, _fold_rows(cand, jnp.maximum))

            v = jnp.max(for_chunks(best_value, jnp.full((SUBLANES, t), _NEG_INF, _F32)),
                        axis=0, keepdims=True)

            def first_index(c, part):
                s = score_ref[rows(c), :]
                hit = jnp.where(s == v, beyond(s, kposf(c), front_v, front_i), 0.0)
                cand = jnp.where(hit > 0.0, kposf(c), no_index)
                return jnp.minimum(part, _fold_rows(cand, jnp.minimum))

            i = jnp.min(for_chunks(first_index, jnp.full((SUBLANES, t), no_index, _F32)),
                        axis=0, keepdims=True)
            active = need > 0.0
            return (jnp.where(active, v, front_v), jnp.where(active, i, front_i),
                    jnp.where(active, need - 1.0, need))

        front_v, front_i, _ = lax.fori_loop(0, jnp.max(need).astype(jnp.int32), advance,
                                            (front_v, front_i, need))

        def tie_chunk(c, carry):
            s = score_ref[rows(c), :]
            upto = jnp.where(s > front_v, 0.0,
                             jnp.where(s == front_v,
                                       jnp.where(kposf(c) <= front_i, 0.0, _NEG_INF), _NEG_INF))
            keep = jnp.where(s >= nxt, 0.0, jnp.where(s >= thr, upto, _NEG_INF))
            bias_ref[rows(c), :] = jnp.where(c * KEY_CHUNK + kiota <= qpos, keep, _NEG_INF)
            return carry

        for_chunks(tie_chunk, 0)

    grp = B_HEADS // B_KV_HEADS
    gw = grp * t
    exp2_scale = (B_HEAD_DIM ** -0.5) * 1.4426950408889634

    def logit_chunk(c, mx):
        bias = jnp.concatenate([bias_ref[rows(c), :]] * grp, axis=1)
        folded = []
        for g in range(B_KV_HEADS):
            qg = jnp.concatenate(
                [q_ref[0, :, (g * grp + hh) * B_HEAD_DIM:(g * grp + hh + 1) * B_HEAD_DIM]
                 for hh in range(grp)], axis=0)
            l = lax.dot_general(k_ref[0, rows(c), g * B_HEAD_DIM:(g + 1) * B_HEAD_DIM], qg,
                                _NT_DIMS, preferred_element_type=_F32) + bias
            logit_ref[rows(c), g * gw:(g + 1) * gw] = l
            folded.append(_fold_rows(l, jnp.maximum))
        return jnp.maximum(mx, jnp.concatenate(folded, axis=1))

    mx = for_chunks(logit_chunk, jnp.full((SUBLANES, B_HEADS * t), _NEG_INF, _F32))
    mx = jnp.max(mx, axis=0, keepdims=True)

    acc_ref[...] = jnp.zeros_like(acc_ref)

    def pv_chunk(c, den):
        folded = []
        for g in range(B_KV_HEADS):
            cols = slice(g * gw, (g + 1) * gw)
            p = jnp.exp2((logit_ref[rows(c), cols] - mx[:, cols]) * exp2_scale)
            folded.append(_fold_rows(p, jnp.add))
            acc_ref[:, cols] += jnp.dot(vt_ref[0, c, g * B_HEAD_DIM:(g + 1) * B_HEAD_DIM, :],
                                        p.astype(_BF16), preferred_element_type=_F32)
        return den + jnp.concatenate(folded, axis=1)

    den = for_chunks(pv_chunk, jnp.zeros((SUBLANES, B_HEADS * t), _F32))
    out_t = acc_ref[...] / jnp.sum(den, axis=0, keepdims=True)
    for h in range(B_HEADS):
        o_ref[0, :, h * B_HEAD_DIM:(h + 1) * B_HEAD_DIM] = (
            out_t[:, h * t:(h + 1) * t].T.astype(o_ref.dtype))


def _dsa_attention(q, k, vt, qi, ki, wit):
    bsz, seq, _ = q.shape
    assert Q_BLOCK == LANES and seq % KEY_CHUNK == 0
    n_sel = min(TOPK_MAX, seq // 4)

    def q_spec(width):
        return pl.BlockSpec((1, Q_BLOCK, width), lambda b, i: (b, i, 0))

    def kv_spec(width):
        return pl.BlockSpec((1, seq, width), lambda b, i: (b, 0, 0))

    return pl.pallas_call(
        functools.partial(_dsa_kernel, n_sel=n_sel),
        grid=(bsz, seq // Q_BLOCK),
        in_specs=[q_spec(IDX_HEADS * LANES),
                  pl.BlockSpec((1, IDX_HEADS, Q_BLOCK), lambda b, i: (b, 0, i)),
                  kv_spec(LANES), q_spec(B_WIDTH), kv_spec(B_KV_WIDTH),
                  pl.BlockSpec((1, seq // KEY_CHUNK, B_KV_WIDTH, KEY_CHUNK),
                               lambda b, i: (b, 0, 0, 0))],
        out_specs=q_spec(B_WIDTH),
        out_shape=jax.ShapeDtypeStruct((bsz, seq, B_WIDTH), _BF16),
        scratch_shapes=[pltpu.VMEM((seq, Q_BLOCK), _F32), pltpu.VMEM((seq, Q_BLOCK), _F32),
                        pltpu.VMEM((seq, B_HEADS * Q_BLOCK), _F32),
                        pltpu.VMEM((B_HEAD_DIM, B_HEADS * Q_BLOCK), _F32)],
        compiler_params=_compiler_params(2),
        name="dsa_attention",
    )(qi, wit, ki, q, k, vt)


def _xattn_kernel(q_ref, k_ref, v_ref, o_ref):
    scale = X_HEAD_DIM ** -0.5
    for h in range(X_HEADS):
        hs = slice(h * X_HEAD_DIM, (h + 1) * X_HEAD_DIM)
        logits = lax.dot_general(q_ref[0, :, hs], k_ref[0, :, hs], _NT_DIMS,
                                 preferred_element_type=_F32) * scale
        mx = jnp.max(logits, axis=-1, keepdims=True)
        p = jnp.exp(logits - mx)
        denom = jnp.sum(p, axis=-1, keepdims=True)
        o = jnp.dot(p.astype(_BF16), v_ref[0, :, hs], preferred_element_type=_F32)
        o_ref[0, :, hs] = (o / denom).astype(o_ref.dtype)


def _cross_attention(q, kv, tq=512):
    bsz, seq, _ = q.shape
    m = kv.shape[1]
    return pl.pallas_call(
        _xattn_kernel,
        grid=(bsz, seq // tq),
        in_specs=[pl.BlockSpec((1, tq, X_WIDTH), lambda b, i: (b, i, 0)),
                  pl.BlockSpec((1, m, X_WIDTH), lambda b, i: (b, 0, 0)),
                  pl.BlockSpec((1, m, X_WIDTH), lambda b, i: (b, 0, 1))],
        out_specs=pl.BlockSpec((1, tq, X_WIDTH), lambda b, i: (b, i, 0)),
        out_shape=jax.ShapeDtypeStruct((bsz, seq, X_WIDTH), _BF16),
        compiler_params=_compiler_params(2),
        name="cross_attention",
    )(q, kv, kv)


def _xattn_out_kernel(o_ref, w_ref, x_ref, g_ref, x_out_ref, h_out_ref, w_bf16_ref):
    @pl.when(pl.program_id(0) == 0)
    def _():
        w_bf16_ref[...] = w_ref[...].astype(_BF16)

    x = x_ref[...] + jnp.dot(o_ref[...], w_bf16_ref[...], preferred_element_type=_F32)
    x_out_ref[...] = x
    ms = jnp.mean(x * x, axis=-1, keepdims=True)
    h_out_ref[...] = ((x * lax.rsqrt(ms + EPS)) * g_ref[...]).astype(h_out_ref.dtype)


def _xattn_out_and_norm(o, w_stack, layer, x, g, tm=256):
    m, k = o.shape
    d = x.shape[1]
    row_spec = pl.BlockSpec((tm, d), lambda i: (i, 0))
    return pl.pallas_call(
        _xattn_out_kernel,
        grid=(m // tm,),
        in_specs=[pl.BlockSpec((tm, k), lambda i: (i, 0)),
                  pl.BlockSpec((None, k, d), lambda i: (layer, 0, 0),
                               pipeline_mode=pl.Buffered(1)),
                  row_spec,
                  pl.BlockSpec((1, d), lambda i: (0, 0))],
        out_specs=[row_spec, row_spec],
        out_shape=[jax.ShapeDtypeStruct((m, d), _F32), jax.ShapeDtypeStruct((m, d), _BF16)],
        scratch_shapes=[pltpu.VMEM((k, d), _BF16)],
        compiler_params=_compiler_params(1),
        name="xattn_out_norm",
    )(o, w_stack, x, g.reshape(1, d))


def kernel(x, mem, positions, norm_mix_g, w_in, a_norm_g, a_spatial_w, a_spatial_b, p_a, p_b,
           w_out, norm_x_g, norm_mem_g, xq_w, xk_w, xv_w, xo_w, norm_ffn_g, ffn_w1, ffn_w3,
           ffn_w2, final_norm_g):
    bsz, seq, d = x.shape
    m = bsz * seq
    depth = w_in.shape[0]
    ffn_hidden = ffn_w1.shape[-1]
    tables_b = _rope_tables(positions, B_ROT, B_HEAD_DIM)
    tables_i = _rope_tables(positions, IDX_ROT, IDX_DIM)

    xf = x.reshape(m, d)
    mem_f = mem.reshape(bsz * MEM_LEN, d)
    w_in_t = jnp.transpose(w_in, (0, 2, 1))
    for l in range(depth):
        def w_in_cols(col0):
            return Rhs(w_in_t, l, col0, transposed=True)

        h = _rmsnorm(xf, norm_mix_g[l], _BF16)
        za = _fused_matmul([h], [(0, w_in_cols(0))], [], _ep_identity, ZA_END, _F32,
                           2048, PROJ_TN, "proj_za")
        y_a = _spatial_gating(za, a_norm_g[l], a_spatial_w[l], a_spatial_b[l])
        att = _fused_matmul([h], [(0, w_in_cols(ZA_END))], [], _ep_identity,
                            ATT_WIDTH, _F32, 2048, PROJ_TN, "proj_att")
        q, k, vt, qi, ki, wi = _prep_attention_inputs(att.reshape(bsz, seq, ATT_WIDTH),
                                                      tables_b, tables_i)
        y_b = _dsa_attention(q, k, vt, qi, ki, wi.transpose(0, 2, 1)).reshape(m, B_WIDTH)
        merged = _fused_matmul(
            [y_a, y_b, h],
            [(0, Rhs(p_a, l)), (1, Rhs(p_b, l)), (2, w_in_cols(ATT_END)),
             (2, w_in_cols(ATT_END + d))],
            [], _ep_gated_merge, d, _BF16, 1024, 256, "gated_merge")
        x1 = _fused_matmul([merged], [(0, Rhs(w_out, l))], [xf], _ep_residual,
                           d, _F32, 2048, 256, "out_proj")

        hx = _rmsnorm(x1, norm_x_g[l], _BF16)
        mem_n = _rmsnorm(mem_f, norm_mem_g[l], _BF16)
        qx = _fused_matmul([hx], [(0, Rhs(xq_w, l))], [], _ep_identity,
                           X_WIDTH, _BF16, 2048, 256, "xattn_q")
        w_kv = jnp.concatenate([xk_w[l], xv_w[l]], axis=1).astype(_BF16)
        kvx = _fused_matmul([mem_n], [(0, Rhs(w_kv))], [], _ep_identity,
                            2 * X_WIDTH, _BF16, bsz * MEM_LEN, 512, "xattn_kv")
        ox = _cross_attention(qx.reshape(bsz, seq, X_WIDTH),
                              kvx.reshape(bsz, MEM_LEN, 2 * X_WIDTH)).reshape(m, X_WIDTH)
        x2, h2 = _xattn_out_and_norm(ox, xo_w, l, x1, norm_ffn_g[l])

        act = _fused_matmul([h2], [(0, Rhs(ffn_w1, l)), (0, Rhs(ffn_w3, l))],
                            [], _ep_swiglu, ffn_hidden, _BF16, 2048, 256, "ffn_up")
        xf = _fused_matmul([act], [(0, Rhs(ffn_w2[l].astype(_BF16)))], [x2], _ep_residual,
                           d, _F32, 1024, 256, "ffn_down")

    return _rmsnorm(xf, final_norm_g, _F32).reshape(bsz, seq, d)
```

```python
import functools
from typing import NamedTuple, Optional

import jax
import jax.numpy as jnp
from jax import lax
from jax.experimental import pallas as pl
from jax.experimental.pallas import tpu as pltpu

D_MODEL = 4096
MEM_LEN = 256
EPS = 1e-6
ROPE_THETA = 500000.0
CHUNK = 128
A_GROUPS = 16
A_WIDTH = D_MODEL // 2
A_GROUP_DIM = A_WIDTH // A_GROUPS
B_HEADS = 16
B_HEAD_DIM = 128
B_KV_HEADS = 4
B_WIDTH = B_HEADS * B_HEAD_DIM
B_KV_WIDTH = B_KV_HEADS * B_HEAD_DIM
B_ROT = B_HEAD_DIM // 4
IDX_HEADS = 16
IDX_DIM = 64
IDX_WIDTH = IDX_HEADS * IDX_DIM
IDX_ROT = IDX_DIM // 4
TOPK_MAX = 256
Q_BLOCK = 128
X_HEADS = 4
X_HEAD_DIM = 256
X_WIDTH = X_HEADS * X_HEAD_DIM
IN_SIZES = (2 * A_WIDTH, B_WIDTH, B_KV_WIDTH, B_KV_WIDTH, IDX_WIDTH, IDX_DIM, IDX_HEADS,
            2 * D_MODEL)

LANES = 128
SUBLANES = 8
KEY_CHUNK = 512
PROJ_TN = 256
ZA_END = IN_SIZES[0]
ATT_END = ZA_END + sum(IN_SIZES[1:7])
ATT_Q0 = 0
ATT_K0 = ATT_Q0 + B_WIDTH
ATT_V0 = ATT_K0 + B_KV_WIDTH
ATT_QI0 = ATT_V0 + B_KV_WIDTH
ATT_KW0 = ATT_QI0 + IDX_WIDTH
ATT_WIDTH = -(-(ATT_END - ZA_END) // PROJ_TN) * PROJ_TN
assert ZA_END % PROJ_TN == 0 and ATT_KW0 + LANES <= ATT_WIDTH

VMEM_LIMIT_BYTES = 56 * 1024 * 1024

_BF16 = jnp.bfloat16
_F32 = jnp.float32
_NEG_INF = float("-inf")
_NN_DIMS = (((1,), (0,)), ((), ()))
_NT_DIMS = (((1,), (1,)), ((), ()))


def _compiler_params(n_axes):
    return pltpu.CompilerParams(
        dimension_semantics=("arbitrary",) * n_axes,
        vmem_limit_bytes=VMEM_LIMIT_BYTES,
    )


def _rmsnorm_kernel(x_ref, g_ref, o_ref):
    x = x_ref[...]
    ms = jnp.mean(x * x, axis=-1, keepdims=True)
    y = x * lax.rsqrt(ms + EPS)
    o_ref[...] = (y * g_ref[...]).astype(o_ref.dtype)


def _rmsnorm(x, g, out_dtype, tm=512):
    m, d = x.shape
    return pl.pallas_call(
        _rmsnorm_kernel,
        grid=(m // tm,),
        in_specs=[pl.BlockSpec((tm, d), lambda i: (i, 0)),
                  pl.BlockSpec((1, d), lambda i: (0, 0))],
        out_specs=pl.BlockSpec((tm, d), lambda i: (i, 0)),
        out_shape=jax.ShapeDtypeStruct((m, d), out_dtype),
        compiler_params=_compiler_params(1),
        name="rmsnorm",
    )(x, g.reshape(1, d))


def _ep_identity(accs, extras):
    return accs[0]


def _ep_residual(accs, extras):
    return extras[0] + accs[0]


def _ep_gated_merge(accs, extras):
    y_a, y_b, gate_a, gate_b = accs
    return jax.nn.sigmoid(gate_a) * y_a + jax.nn.sigmoid(gate_b) * y_b


def _ep_swiglu(accs, extras):
    return jax.nn.silu(accs[0]) * accs[1]


class Rhs(NamedTuple):
    array: jax.Array
    layer: Optional[int] = None
    col0: int = 0
    transposed: bool = False


def _rhs_spec(r, tn):
    if r.transposed:
        k = r.array.shape[2]
        if r.col0 % tn == 0:
            return pl.BlockSpec((None, tn, k), lambda i, j: (r.layer, j + r.col0 // tn, 0))
        assert r.col0 % SUBLANES == 0
        return pl.BlockSpec((pl.Element(1), pl.Element(tn), pl.Element(k)),
                            lambda i, j: (r.layer, (r.col0 // SUBLANES + j * (tn // SUBLANES))
                                          * SUBLANES, 0))
    assert r.col0 % tn == 0
    off = r.col0 // tn
    if r.layer is None:
        return pl.BlockSpec((r.array.shape[0], tn), lambda i, j: (0, j + off))
    return pl.BlockSpec((None, r.array.shape[1], tn), lambda i, j: (r.layer, 0, j + off))


def _mm_kernel(*refs, n_lhs, pair_lhs, rhs_transposed, n_extra, epilogue, norm_lhs, side_cast):
    n_rhs = len(pair_lhs)
    lhs_refs = list(refs[:n_lhs])
    pos = n_lhs
    if norm_lhs:
        gain_ref = refs[pos]
        pos += 1
    rhs_refs = refs[pos:pos + n_rhs]
    pos += n_rhs
    extra_refs = refs[pos:pos + n_extra]
    pos += n_extra
    if side_cast:
        side_in_ref = refs[pos]
        pos += 1
    o_ref = refs[pos]
    pos += 1
    if side_cast:
        refs[pos][...] = side_in_ref[...].astype(_BF16)
        pos += 1
    if norm_lhs:
        normed_ref = refs[pos]

        @pl.when(pl.program_id(1) == 0)
        def _():
            x = lhs_refs[0][...]
            ms = jnp.mean(x * x, axis=-1, keepdims=True)
            normed_ref[...] = ((x * lax.rsqrt(ms + EPS)) * gain_ref[...]).astype(_BF16)

        lhs_refs[0] = normed_ref
    accs = []
    for li, r, tr in zip(pair_lhs, rhs_refs, rhs_transposed):
        dims = _NT_DIMS if tr else _NN_DIMS
        w = r[0] if len(r.shape) == 3 else r[...]
        accs.append(lax.dot_general(lhs_refs[li][...], w.astype(_BF16), dims,
                                    preferred_element_type=_F32))
    o_ref[...] = epilogue(accs, [e[...] for e in extra_refs]).astype(o_ref.dtype)


def _fused_matmul(lhs, pairs, extras, epilogue, n_out, out_dtype, tm, tn, name,
                  norm_gain=None, side_cast=None):
    m = lhs[0].shape[0]
    grid = (m // tm, n_out // tn)
    if norm_gain is None:
        in_specs = [pl.BlockSpec((tm, a.shape[1]), lambda i, j: (i, 0),
                                 pipeline_mode=pl.Buffered(1)) for a in lhs]
    else:
        in_specs = [pl.BlockSpec((tm, a.shape[1]), lambda i, j: (i, 0)) for a in lhs]
        in_specs.append(pl.BlockSpec((1, lhs[0].shape[1]), lambda i, j: (0, 0)))
    in_specs += [_rhs_spec(r, tn) for _, r in pairs]
    in_specs += [pl.BlockSpec((tm, tn), lambda i, j: (i, j)) for _ in extras]
    out_specs = [pl.BlockSpec((tm, tn), lambda i, j: (i, j))]
    out_shape = [jax.ShapeDtypeStruct((m, n_out), out_dtype)]
    operands = list(lhs)
    if norm_gain is not None:
        operands.append(norm_gain.reshape(1, -1))
    operands += [p[1].array for p in pairs] + list(extras)
    scratch = []
    if side_cast is not None:
        w_side, layer = side_cast
        _, rows, cols = w_side.shape
        n_steps = grid[0] * grid[1]
        slab = rows // n_steps
        assert slab * n_steps == rows and slab % (2 * SUBLANES) == 0
        in_specs.append(pl.BlockSpec((None, slab, cols),
                                     lambda i, j: (layer, i * grid[1] + j, 0)))
        out_specs.append(pl.BlockSpec((slab, cols), lambda i, j: (i * grid[1] + j, 0)))
        out_shape.append(jax.ShapeDtypeStruct((rows, cols), _BF16))
        operands.append(w_side)
    if norm_gain is not None:
        scratch.append(pltpu.VMEM((tm, lhs[0].shape[1]), _BF16))
    kern = functools.partial(_mm_kernel, n_lhs=len(lhs), pair_lhs=tuple(p[0] for p in pairs),
                             rhs_transposed=tuple(p[1].transposed for p in pairs),
                             n_extra=len(extras), epilogue=epilogue,
                             norm_lhs=norm_gain is not None, side_cast=side_cast is not None)
    outs = pl.pallas_call(
        kern,
        grid=grid,
        in_specs=in_specs,
        out_specs=out_specs,
        out_shape=out_shape,
        scratch_shapes=scratch,
        compiler_params=_compiler_params(2),
        name=name,
    )(*operands)
    return outs[0] if side_cast is None else tuple(outs)


def _gating_kernel(za_ref, ng_ref, w_ref, bt_ref, o_ref):
    row = lax.broadcasted_iota(jnp.int32, (CHUNK, CHUNK), 0)
    col = lax.broadcasted_iota(jnp.int32, (CHUNK, CHUNK), 1)
    causal = col <= row
    for c in range(za_ref.shape[0] // CHUNK):
        rows = slice(c * CHUNK, (c + 1) * CHUNK)
        z = jax.nn.gelu(za_ref[rows, :])
        u = z[:, :A_WIDTH]
        v = z[:, A_WIDTH:]
        ms = jnp.mean(v * v, axis=-1, keepdims=True)
        vn = ((v * lax.rsqrt(ms + EPS)) * ng_ref[...]).astype(_BF16)
        for g in range(A_GROUPS):
            sl = slice(g * A_GROUP_DIM, (g + 1) * A_GROUP_DIM)
            w = jnp.where(causal, w_ref[g], 0.0).astype(_BF16)
            s = jnp.dot(w, vn[:, sl], preferred_element_type=_F32) + bt_ref[:, g:g + 1]
            o_ref[rows, sl] = (u[:, sl] * s).astype(o_ref.dtype)


def _spatial_gating(za, norm_g, w_s, b_s, chunks_per_step=2):
    m = za.shape[0]
    tm = chunks_per_step * CHUNK
    return pl.pallas_call(
        _gating_kernel,
        grid=(m // tm,),
        in_specs=[pl.BlockSpec((tm, 2 * A_WIDTH), lambda i: (i, 0)),
                  pl.BlockSpec((1, A_WIDTH), lambda i: (0, 0)),
                  pl.BlockSpec((A_GROUPS, CHUNK, CHUNK), lambda i: (0, 0, 0)),
                  pl.BlockSpec((CHUNK, A_GROUPS), lambda i: (0, 0))],
        out_specs=pl.BlockSpec((tm, A_WIDTH), lambda i: (i, 0)),
        out_shape=jax.ShapeDtypeStruct((m, A_WIDTH), _BF16),
        compiler_params=_compiler_params(1),
        name="spatial_gating",
    )(za, norm_g.reshape(1, A_WIDTH), w_s, b_s.T)


def _rope_tables(positions, rot_dim, period):
    half = rot_dim // 2
    inv_freq = ROPE_THETA ** (-jnp.arange(0, rot_dim, 2, dtype=_F32) / rot_dim)
    ang = positions.astype(_F32)[..., None] * inv_freq
    cos, sin = jnp.cos(ang), jnp.sin(ang)
    zeros_half = jnp.zeros_like(sin)
    rest = jnp.zeros(sin.shape[:-1] + (period - rot_dim,), _F32)

    def lanes(*pieces):
        return jnp.tile(jnp.concatenate(pieces, axis=-1), LANES // period)

    return (lanes(cos, cos, rest + 1.0), lanes(zeros_half, sin, rest),
            lanes(-sin, zeros_half, rest))


def _rope(x, c, s1, s2, half):
    return x * c + pltpu.roll(x, half, 1) * s1 + pltpu.roll(x, LANES - half, 1) * s2


def _prep_kernel(att_ref, cb_ref, s1b_ref, s2b_ref, ci_ref, s1i_ref, s2i_ref,
                 q_ref, k_ref, vt_ref, qi_ref, ki_ref, wi_ref):
    cb, s1b, s2b = cb_ref[0], s1b_ref[0], s2b_ref[0]
    ci, s1i, s2i = ci_ref[0], s1i_ref[0], s2i_ref[0]
    for h in range(B_HEADS):
        x = att_ref[0, :, ATT_Q0 + h * LANES:ATT_Q0 + (h + 1) * LANES]
        q_ref[0, :, h * LANES:(h + 1) * LANES] = _rope(x, cb, s1b, s2b, B_ROT // 2).astype(_BF16)
    for h in range(B_KV_HEADS):
        x = att_ref[0, :, ATT_K0 + h * LANES:ATT_K0 + (h + 1) * LANES]
        k_ref[0, :, h * LANES:(h + 1) * LANES] = _rope(x, cb, s1b, s2b, B_ROT // 2).astype(_BF16)
    vt_ref[0, 0] = att_ref[0, :, ATT_V0:ATT_V0 + B_KV_WIDTH].T.astype(_BF16)
    lane = lax.broadcasted_iota(jnp.int32, ci.shape, 1)
    is_ki = lane < IDX_DIM
    heads_per_group = LANES // IDX_DIM
    for j in range(IDX_WIDTH // LANES):
        x = att_ref[0, :, ATT_QI0 + j * LANES:ATT_QI0 + (j + 1) * LANES]
        y = _rope(x, ci, s1i, s2i, IDX_ROT // 2)
        for r in range(heads_per_group):
            h = j * heads_per_group + r
            yr = y if r == 0 else pltpu.roll(y, LANES - r * IDX_DIM, 1)
            qi_ref[0, :, h * LANES:(h + 1) * LANES] = jnp.where(is_ki, yr, 0.0).astype(_BF16)
    x = att_ref[0, :, ATT_KW0:ATT_KW0 + LANES]
    y = _rope(x, jnp.where(is_ki, ci, 1.0), jnp.where(is_ki, s1i, 0.0),
              jnp.where(is_ki, s2i, 0.0), IDX_ROT // 2)
    ki_ref[0] = jnp.where(is_ki, y, 0.0).astype(_BF16)
    wi_ref[0] = y[:, IDX_DIM:IDX_DIM + IDX_HEADS]


def _prep_attention_inputs(att, tables_b, tables_i, ts=256):
    bsz, seq, _ = att.shape
    tab_spec = pl.BlockSpec((1, ts, LANES), lambda b, i: (b, i, 0))
    per_chunk = KEY_CHUNK // ts

    def out(width, dtype):
        return (pl.BlockSpec((1, ts, width), lambda b, i: (b, i, 0)),
                jax.ShapeDtypeStruct((bsz, seq, width), dtype))

    vt_out = (pl.BlockSpec((1, 1, B_KV_WIDTH, ts),
                           lambda b, i: (b, i // per_chunk, 0, i % per_chunk)),
              jax.ShapeDtypeStruct((bsz, seq // KEY_CHUNK, B_KV_WIDTH, KEY_CHUNK), _BF16))
    outs = [out(B_WIDTH, _BF16), out(B_KV_WIDTH, _BF16), vt_out,
            out(IDX_HEADS * LANES, _BF16), out(LANES, _BF16), out(IDX_HEADS, _F32)]
    return pl.pallas_call(
        _prep_kernel,
        grid=(bsz, seq // ts),
        in_specs=[pl.BlockSpec((1, ts, ATT_WIDTH), lambda b, i: (b, i, 0))] + [tab_spec] * 6,
        out_specs=[o[0] for o in outs],
        out_shape=[o[1] for o in outs],
        compiler_params=_compiler_params(2),
        name="rope_split",
    )(att, *tables_b, *tables_i)


def _ordered_int_to_float(key):
    bits = jnp.where(key < 0, key ^ jnp.int32(0x7FFFFFFF), key)
    return lax.bitcast_convert_type(bits, _F32)


def _fold_rows(x, op):
    while x.shape[0] > SUBLANES:
        half = x.shape[0] // 2
        x = op(x[:half], x[half:])
    return x


def _dsa_kernel(qi_ref, wit_ref, ki_ref, q_ref, k_ref, vt_ref, o_ref,
                score_ref, bias_ref, logit_ref, acc_ref, *, n_sel):
    t = Q_BLOCK
    blk = pl.program_id(1)
    n_chunks = (blk * t + t + KEY_CHUNK - 1) // KEY_CHUNK
    qpos = blk * t + lax.broadcasted_iota(jnp.int32, (1, t), 1)
    kiota = lax.broadcasted_iota(jnp.int32, (KEY_CHUNK, 1), 0)
    idx_scale = (IDX_DIM ** -0.5) * (IDX_HEADS ** -0.5)

    def rows(c):
        return pl.ds(pl.multiple_of(c * KEY_CHUNK, KEY_CHUNK), KEY_CHUNK)

    def for_chunks(body, init):
        return lax.fori_loop(0, n_chunks, body, init)

    def indexer_chunk(c, carry):
        ki = ki_ref[0, rows(c), :]
        acc = jnp.zeros((KEY_CHUNK, t), _F32)
        for h in range(0, IDX_HEADS, 2):
            qpair = jnp.concatenate([qi_ref[0, :, h * LANES:(h + 1) * LANES],
                                     qi_ref[0, :, (h + 1) * LANES:(h + 2) * LANES]], axis=0)
            d = lax.dot_general(ki, qpair, _NT_DIMS, preferred_element_type=_F32)
            acc = acc + wit_ref[0, h:h + 1, :] * jnp.maximum(d[:, :t], 0.0)
            acc = acc + wit_ref[0, h + 1:h + 2, :] * jnp.maximum(d[:, t:], 0.0)
        kpos = c * KEY_CHUNK + kiota
        score_ref[rows(c), :] = jnp.where(kpos <= qpos, acc * idx_scale + 0.0, _NEG_INF)
        return carry

    for_chunks(indexer_chunk, 0)

    def count(indicator):
        def chunk(c, part):
            return part + _fold_rows(indicator(score_ref[rows(c), :], c), jnp.add)
        part = for_chunks(chunk, jnp.zeros((SUBLANES, t), _F32))
        return jnp.sum(part, axis=0, keepdims=True)

    def count_ge(cand_f):
        return count(lambda s, c: jnp.where(s >= cand_f, 1.0, 0.0))

    int_min = jnp.int32(-2 ** 31)
    base = jnp.where(count_ge(jnp.zeros((1, t), _F32)) >= n_sel, jnp.int32(0), int_min)

    def search_step(it, base):
        cand = base | lax.shift_left(jnp.int32(1), jnp.int32(30) - it)
        cnt = count_ge(_ordered_int_to_float(cand))
        return jnp.where(cnt >= n_sel, cand, base)

    base = lax.fori_loop(0, 31, search_step, base)
    thr = jnp.where(qpos + 1 < n_sel, _NEG_INF, _ordered_int_to_float(base))

    def bias_chunk(c, part):
        s = score_ref[rows(c), :]
        sel = jnp.where(c * KEY_CHUNK + kiota <= qpos, jnp.where(s >= thr, 1.0, 0.0), 0.0)
        bias_ref[rows(c), :] = jnp.where(sel > 0.0, 0.0, _NEG_INF)
        return part + _fold_rows(sel, jnp.add)

    n_ge = jnp.sum(for_chunks(bias_chunk, jnp.zeros((SUBLANES, t), _F32)), axis=0, keepdims=True)

    @pl.when(jnp.max(n_ge) > n_sel)
    def _():
        nxt = _ordered_int_to_float(base + 1)
        tied = n_ge > n_sel
        n_above = count(lambda s, c: jnp.where(s >= nxt, 1.0, 0.0))
        need = jnp.where(tied, n_sel - n_above, 0.0)
        no_index = float(k_ref.shape[1])
        front_v = jnp.where(tied, jnp.inf, _NEG_INF)
        front_i = jnp.where(tied, -1.0, no_index)

        def beyond(s, kposf, front_v, front_i):
            after = jnp.where(s < front_v, 1.0,
                              jnp.where(s == front_v, jnp.where(kposf > front_i, 1.0, 0.0), 0.0))
            return jnp.where(s >= thr, jnp.where(s < nxt, after, 0.0), 0.0)

        def kposf(c):
            return (c * KEY_CHUNK + kiota).astype(_F32)

        def advance(_, carry):
            front_v, front_i, need = carry

            def best_value(c, part):
                s = score_ref[rows(c), :]
                cand = jnp.where(beyond(s, kposf(c), front_v, front_i) > 0.0, s, _NEG_INF)
                return jnp.maximum(part, _fold_rows(cand, jnp.maximum))

            v = jnp.max(for_chunks(best_value, jnp.full((SUBLANES, t), _NEG_INF, _F32)),
                        axis=0, keepdims=True)

            def first_index(c, part):
                s = score_ref[rows(c), :]
                hit = jnp.where(s == v, beyond(s, kposf(c), front_v, front_i), 0.0)
                cand = jnp.where(hit > 0.0, kposf(c), no_index)
                return jnp.minimum(part, _fold_rows(cand, jnp.minimum))

            i = jnp.min(for_chunks(first_index, jnp.full((SUBLANES, t), no_index, _F32)),
                        axis=0, keepdims=True)
            active = need > 0.0
            return (jnp.where(active, v, front_v), jnp.where(active, i, front_i),
                    jnp.where(active, need - 1.0, need))

        front_v, front_i, _ = lax.fori_loop(0, jnp.max(need).astype(jnp.int32), advance,
                                            (front_v, front_i, need))

        def tie_chunk(c, carry):
            s = score_ref[rows(c), :]
            upto = jnp.where(s > front_v, 0.0,
                             jnp.where(s == front_v,
                                       jnp.where(kposf(c) <= front_i, 0.0, _NEG_INF), _NEG_INF))
            keep = jnp.where(s >= nxt, 0.0, jnp.where(s >= thr, upto, _NEG_INF))
            bias_ref[rows(c), :] = jnp.where(c * KEY_CHUNK + kiota <= qpos, keep, _NEG_INF)
            return carry

        for_chunks(tie_chunk, 0)

    grp = B_HEADS // B_KV_HEADS
    gw = grp * t
    exp2_scale = (B_HEAD_DIM ** -0.5) * 1.4426950408889634

    def logit_chunk(c, mx):
        bias = jnp.concatenate([bias_ref[rows(c), :]] * grp, axis=1)
        folded = []
        for g in range(B_KV_HEADS):
            qg = jnp.concatenate(
                [q_ref[0, :, (g * grp + hh) * B_HEAD_DIM:(g * grp + hh + 1) * B_HEAD_DIM]
                 for hh in range(grp)], axis=0)
            l = lax.dot_general(k_ref[0, rows(c), g * B_HEAD_DIM:(g + 1) * B_HEAD_DIM], qg,
                                _NT_DIMS, preferred_element_type=_F32) + bias
            logit_ref[rows(c), g * gw:(g + 1) * gw] = l
            folded.append(_fold_rows(l, jnp.maximum))
        return jnp.maximum(mx, jnp.concatenate(folded, axis=1))

    mx = for_chunks(logit_chunk, jnp.full((SUBLANES, B_HEADS * t), _NEG_INF, _F32))
    mx = jnp.max(mx, axis=0, keepdims=True)

    acc_ref[...] = jnp.zeros_like(acc_ref)

    def pv_chunk(c, den):
        folded = []
        for g in range(B_KV_HEADS):
            cols = slice(g * gw, (g + 1) * gw)
            p = jnp.exp2((logit_ref[rows(c), cols] - mx[:, cols]) * exp2_scale)
            folded.append(_fold_rows(p, jnp.add))
            acc_ref[:, cols] += jnp.dot(vt_ref[0, c, g * B_HEAD_DIM:(g + 1) * B_HEAD_DIM, :],
                                        p.astype(_BF16), preferred_element_type=_F32)
        return den + jnp.concatenate(folded, axis=1)

    den = for_chunks(pv_chunk, jnp.zeros((SUBLANES, B_HEADS * t), _F32))
    out_t = acc_ref[...] / jnp.sum(den, axis=0, keepdims=True)
    for h in range(B_HEADS):
        o_ref[0, :, h * B_HEAD_DIM:(h + 1) * B_HEAD_DIM] = (
            out_t[:, h * t:(h + 1) * t].T.astype(o_ref.dtype))


def _dsa_attention(q, k, vt, qi, ki, wit):
    bsz, seq, _ = q.shape
    assert Q_BLOCK == LANES and seq % KEY_CHUNK == 0
    n_sel = min(TOPK_MAX, seq // 4)

    def q_spec(width):
        return pl.BlockSpec((1, Q_BLOCK, width), lambda b, i: (b, i, 0))

    def kv_spec(width):
        return pl.BlockSpec((1, seq, width), lambda b, i: (b, 0, 0))

    return pl.pallas_call(
        functools.partial(_dsa_kernel, n_sel=n_sel),
        grid=(bsz, seq // Q_BLOCK),
        in_specs=[q_spec(IDX_HEADS * LANES),
                  pl.BlockSpec((1, IDX_HEADS, Q_BLOCK), lambda b, i: (b, 0, i)),
                  kv_spec(LANES), q_spec(B_WIDTH), kv_spec(B_KV_WIDTH),
                  pl.BlockSpec((1, seq // KEY_CHUNK, B_KV_WIDTH, KEY_CHUNK),
                               lambda b, i: (b, 0, 0, 0))],
        out_specs=q_spec(B_WIDTH),
        out_shape=jax.ShapeDtypeStruct((bsz, seq, B_WIDTH), _BF16),
        scratch_shapes=[pltpu.VMEM((seq, Q_BLOCK), _F32), pltpu.VMEM((seq, Q_BLOCK), _F32),
                        pltpu.VMEM((seq, B_HEADS * Q_BLOCK), _F32),
                        pltpu.VMEM((B_HEAD_DIM, B_HEADS * Q_BLOCK), _F32)],
        compiler_params=_compiler_params(2),
        name="dsa_attention",
    )(qi, wit, ki, q, k, vt)


def _xattn_kernel(q_ref, k_ref, v_ref, o_ref):
    scale = X_HEAD_DIM ** -0.5
    for h in range(X_HEADS):
        hs = slice(h * X_HEAD_DIM, (h + 1) * X_HEAD_DIM)
        logits = lax.dot_general(q_ref[0, :, hs], k_ref[0, :, hs], _NT_DIMS,
                                 preferred_element_type=_F32) * scale
        mx = jnp.max(logits, axis=-1, keepdims=True)
        p = jnp.exp(logits - mx)
        denom = jnp.sum(p, axis=-1, keepdims=True)
        o = jnp.dot(p.astype(_BF16), v_ref[0, :, hs], preferred_element_type=_F32)
        o_ref[0, :, hs] = (o / denom).astype(o_ref.dtype)


def _cross_attention(q, kv, tq=512):
    bsz, seq, _ = q.shape
    m = kv.shape[1]
    return pl.pallas_call(
        _xattn_kernel,
        grid=(bsz, seq // tq),
        in_specs=[pl.BlockSpec((1, tq, X_WIDTH), lambda b, i: (b, i, 0)),
                  pl.BlockSpec((1, m, X_WIDTH), lambda b, i: (b, 0, 0)),
                  pl.BlockSpec((1, m, X_WIDTH), lambda b, i: (b, 0, 1))],
        out_specs=pl.BlockSpec((1, tq, X_WIDTH), lambda b, i: (b, i, 0)),
        out_shape=jax.ShapeDtypeStruct((bsz, seq, X_WIDTH), _BF16),
        compiler_params=_compiler_params(2),
        name="cross_attention",
    )(q, kv, kv)


def _xattn_out_kernel(o_ref, w_ref, x_ref, g_ref, x_out_ref, h_out_ref, w_bf16_ref):
    @pl.when(pl.program_id(0) == 0)
    def _():
        w_bf16_ref[...] = w_ref[...].astype(_BF16)

    x = x_ref[...] + jnp.dot(o_ref[...], w_bf16_ref[...], preferred_element_type=_F32)
    x_out_ref[...] = x
    ms = jnp.mean(x * x, axis=-1, keepdims=True)
    h_out_ref[...] = ((x * lax.rsqrt(ms + EPS)) * g_ref[...]).astype(h_out_ref.dtype)


def _xattn_out_and_norm(o, w_stack, layer, x, g, tm=256):
    m, k = o.shape
    d = x.shape[1]
    row_spec = pl.BlockSpec((tm, d), lambda i: (i, 0))
    return pl.pallas_call(
        _xattn_out_kernel,
        grid=(m // tm,),
        in_specs=[pl.BlockSpec((tm, k), lambda i: (i, 0)),
                  pl.BlockSpec((None, k, d), lambda i: (layer, 0, 0),
                               pipeline_mode=pl.Buffered(1)),
                  row_spec,
                  pl.BlockSpec((1, d), lambda i: (0, 0))],
        out_specs=[row_spec, row_spec],
        out_shape=[jax.ShapeDtypeStruct((m, d), _F32), jax.ShapeDtypeStruct((m, d), _BF16)],
        scratch_shapes=[pltpu.VMEM((k, d), _BF16)],
        compiler_params=_compiler_params(1),
        name="xattn_out_norm",
    )(o, w_stack, x, g.reshape(1, d))


def kernel(x, mem, positions, norm_mix_g, w_in, a_norm_g, a_spatial_w, a_spatial_b, p_a, p_b,
           w_out, norm_x_g, norm_mem_g, xq_w, xk_w, xv_w, xo_w, norm_ffn_g, ffn_w1, ffn_w3,
           ffn_w2, final_norm_g):
    bsz, seq, d = x.shape
    m = bsz * seq
    depth = w_in.shape[0]
    ffn_hidden = ffn_w1.shape[-1]
    tables_b = _rope_tables(positions, B_ROT, B_HEAD_DIM)
    tables_i = _rope_tables(positions, IDX_ROT, IDX_DIM)

    xf = x.reshape(m, d)
    mem_f = mem.reshape(bsz * MEM_LEN, d)
    w_in_t = jnp.transpose(w_in, (0, 2, 1))
    for l in range(depth):
        def w_in_cols(col0):
            return Rhs(w_in_t, l, col0, transposed=True)

        h = _rmsnorm(xf, norm_mix_g[l], _BF16)
        za = _fused_matmul([h], [(0, w_in_cols(0))], [], _ep_identity, ZA_END, _F32,
                           2048, PROJ_TN, "proj_za")
        y_a = _spatial_gating(za, a_norm_g[l], a_spatial_w[l], a_spatial_b[l])
        att = _fused_matmul([h], [(0, w_in_cols(ZA_END))], [], _ep_identity,
                            ATT_WIDTH, _F32, 2048, PROJ_TN, "proj_att")
        q, k, vt, qi, ki, wi = _prep_attention_inputs(att.reshape(bsz, seq, ATT_WIDTH),
                                                      tables_b, tables_i)
        y_b = _dsa_attention(q, k, vt, qi, ki, wi.transpose(0, 2, 1)).reshape(m, B_WIDTH)
        merged = _fused_matmul(
            [y_a, y_b, h],
            [(0, Rhs(p_a, l)), (1, Rhs(p_b, l)), (2, w_in_cols(ATT_END)),
             (2, w_in_cols(ATT_END + d))],
            [], _ep_gated_merge, d, _BF16, 1024, 256, "gated_merge")
        x1 = _fused_matmul([merged], [(0, Rhs(w_out, l))], [xf], _ep_residual,
                           d, _F32, 2048, 256, "out_proj")

        mem_n = _rmsnorm(mem_f, norm_mem_g[l], _BF16)
        qx = _fused_matmul([x1], [(0, Rhs(xq_w, l))], [], _ep_identity,
                           X_WIDTH, _BF16, 512, 512, "xattn_q", norm_gain=norm_x_g[l])
        w_kv = jnp.concatenate([xk_w[l], xv_w[l]], axis=1).astype(_BF16)
        kvx = _fused_matmul([mem_n], [(0, Rhs(w_kv))], [], _ep_identity,
                            2 * X_WIDTH, _BF16, bsz * MEM_LEN, 512, "xattn_kv")
        ox = _cross_attention(qx.reshape(bsz, seq, X_WIDTH),
                              kvx.reshape(bsz, MEM_LEN, 2 * X_WIDTH)).reshape(m, X_WIDTH)
        x2, h2 = _xattn_out_and_norm(ox, xo_w, l, x1, norm_ffn_g[l])

        act, w2_bf16 = _fused_matmul([h2], [(0, Rhs(ffn_w1, l)), (0, Rhs(ffn_w3, l))],
                                     [], _ep_swiglu, ffn_hidden, _BF16, 2048, 256, "ffn_up",
                                     side_cast=(ffn_w2, l))
        xf = _fused_matmul([act], [(0, Rhs(w2_bf16))], [x2], _ep_residual,
                           d, _F32, 1024, 256, "ffn_down")

    return _rmsnorm(xf, final_norm_g, _F32).reshape(bsz, seq, d)
```

```python
import functools
from typing import NamedTuple, Optional

import jax
import jax.numpy as jnp
from jax import lax
from jax.experimental import pallas as pl
from jax.experimental.pallas import tpu as pltpu

D_MODEL = 4096
MEM_LEN = 256
EPS = 1e-6
ROPE_THETA = 500000.0
CHUNK = 128
A_GROUPS = 16
A_WIDTH = D_MODEL // 2
A_GROUP_DIM = A_WIDTH // A_GROUPS
B_HEADS = 16
B_HEAD_DIM = 128
B_KV_HEADS = 4
B_WIDTH = B_HEADS * B_HEAD_DIM
B_KV_WIDTH = B_KV_HEADS * B_HEAD_DIM
B_ROT = B_HEAD_DIM // 4
IDX_HEADS = 16
IDX_DIM = 64
IDX_WIDTH = IDX_HEADS * IDX_DIM
IDX_ROT = IDX_DIM // 4
TOPK_MAX = 256
Q_BLOCK = 128
X_HEADS = 4
X_HEAD_DIM = 256
X_WIDTH = X_HEADS * X_HEAD_DIM
IN_SIZES = (2 * A_WIDTH, B_WIDTH, B_KV_WIDTH, B_KV_WIDTH, IDX_WIDTH, IDX_DIM, IDX_HEADS,
            2 * D_MODEL)

LANES = 128
SUBLANES = 8
KEY_CHUNK = 512
PROJ_TN = 256
ZA_END = IN_SIZES[0]
ATT_END = ZA_END + sum(IN_SIZES[1:7])
ATT_Q0 = 0
ATT_K0 = ATT_Q0 + B_WIDTH
ATT_V0 = ATT_K0 + B_KV_WIDTH
ATT_QI0 = ATT_V0 + B_KV_WIDTH
ATT_KW0 = ATT_QI0 + IDX_WIDTH
ATT_WIDTH = -(-(ATT_END - ZA_END) // PROJ_TN) * PROJ_TN
assert ZA_END % PROJ_TN == 0 and ATT_KW0 + LANES <= ATT_WIDTH

VMEM_LIMIT_BYTES = 56 * 1024 * 1024

_BF16 = jnp.bfloat16
_F32 = jnp.float32
_NEG_INF = float("-inf")
_NN_DIMS = (((1,), (0,)), ((), ()))
_NT_DIMS = (((1,), (1,)), ((), ()))


def _compiler_params(n_axes):
    return pltpu.CompilerParams(
        dimension_semantics=("arbitrary",) * n_axes,
        vmem_limit_bytes=VMEM_LIMIT_BYTES,
    )


def _rmsnorm_kernel(x_ref, g_ref, o_ref):
    x = x_ref[...]
    ms = jnp.mean(x * x, axis=-1, keepdims=True)
    y = x * lax.rsqrt(ms + EPS)
    o_ref[...] = (y * g_ref[...]).astype(o_ref.dtype)


def _rmsnorm(x, g, out_dtype, tm=512):
    m, d = x.shape
    return pl.pallas_call(
        _rmsnorm_kernel,
        grid=(m // tm,),
        in_specs=[pl.BlockSpec((tm, d), lambda i: (i, 0)),
                  pl.BlockSpec((1, d), lambda i: (0, 0))],
        out_specs=pl.BlockSpec((tm, d), lambda i: (i, 0)),
        out_shape=jax.ShapeDtypeStruct((m, d), out_dtype),
        compiler_params=_compiler_params(1),
        name="rmsnorm",
    )(x, g.reshape(1, d))


def _ep_identity(accs, extras):
    return accs[0]


def _ep_residual(accs, extras):
    return extras[0] + accs[0]


def _ep_gated_merge(accs, extras):
    y_a, y_b, gate_a, gate_b = accs
    return jax.nn.sigmoid(gate_a) * y_a + jax.nn.sigmoid(gate_b) * y_b


def _ep_swiglu(accs, extras):
    return jax.nn.silu(accs[0]) * accs[1]


class Rhs(NamedTuple):
    array: jax.Array
    layer: Optional[int] = None
    col0: int = 0
    transposed: bool = False


def _rhs_spec(r, tn):
    if r.transposed:
        k = r.array.shape[2]
        if r.col0 % tn == 0:
            return pl.BlockSpec((None, tn, k), lambda i, j: (r.layer, j + r.col0 // tn, 0))
        assert r.col0 % SUBLANES == 0
        return pl.BlockSpec((pl.Element(1), pl.Element(tn), pl.Element(k)),
                            lambda i, j: (r.layer, (r.col0 // SUBLANES + j * (tn // SUBLANES))
                                          * SUBLANES, 0))
    assert r.col0 % tn == 0
    off = r.col0 // tn
    if r.layer is None:
        return pl.BlockSpec((r.array.shape[0], tn), lambda i, j: (0, j + off))
    return pl.BlockSpec((None, r.array.shape[1], tn), lambda i, j: (r.layer, 0, j + off))


def _mm_kernel(*refs, n_lhs, pair_lhs, rhs_transposed, n_extra, epilogue, side_cast):
    n_rhs = len(pair_lhs)
    lhs_refs = refs[:n_lhs]
    rhs_refs = refs[n_lhs:n_lhs + n_rhs]
    extra_refs = refs[n_lhs + n_rhs:n_lhs + n_rhs + n_extra]
    o_ref = refs[n_lhs + n_rhs + n_extra + side_cast]
    if side_cast:
        side_in_ref, side_out_ref = refs[n_lhs + n_rhs + n_extra], refs[-1]
        side_out_ref[...] = side_in_ref[...].astype(_BF16)
    accs = []
    for li, r, tr in zip(pair_lhs, rhs_refs, rhs_transposed):
        dims = _NT_DIMS if tr else _NN_DIMS
        w = r[0] if len(r.shape) == 3 else r[...]
        accs.append(lax.dot_general(lhs_refs[li][...], w.astype(_BF16), dims,
                                    preferred_element_type=_F32))
    o_ref[...] = epilogue(accs, [e[...] for e in extra_refs]).astype(o_ref.dtype)


def _fused_matmul(lhs, pairs, extras, epilogue, n_out, out_dtype, tm, tn, name,
                  side_cast=None):
    m = lhs[0].shape[0]
    grid = (m // tm, n_out // tn)
    in_specs = [pl.BlockSpec((tm, a.shape[1]), lambda i, j: (i, 0), pipeline_mode=pl.Buffered(1))
                for a in lhs]
    in_specs += [_rhs_spec(r, tn) for _, r in pairs]
    in_specs += [pl.BlockSpec((tm, tn), lambda i, j: (i, j)) for _ in extras]
    out_specs = [pl.BlockSpec((tm, tn), lambda i, j: (i, j))]
    out_shape = [jax.ShapeDtypeStruct((m, n_out), out_dtype)]
    operands = list(lhs) + [p[1].array for p in pairs] + list(extras)
    if side_cast is not None:
        w_side, layer = side_cast
        _, rows, cols = w_side.shape
        n_steps = grid[0] * grid[1]
        slab = rows // n_steps
        assert slab * n_steps == rows and slab % (2 * SUBLANES) == 0
        in_specs.append(pl.BlockSpec((None, slab, cols),
                                     lambda i, j: (layer, i * grid[1] + j, 0)))
        out_specs.append(pl.BlockSpec((slab, cols), lambda i, j: (i * grid[1] + j, 0)))
        out_shape.append(jax.ShapeDtypeStruct((rows, cols), _BF16))
        operands.append(w_side)
    kern = functools.partial(_mm_kernel, n_lhs=len(lhs), pair_lhs=tuple(p[0] for p in pairs),
                             rhs_transposed=tuple(p[1].transposed for p in pairs),
                             n_extra=len(extras), epilogue=epilogue,
                             side_cast=side_cast is not None)
    outs = pl.pallas_call(
        kern,
        grid=grid,
        in_specs=in_specs,
        out_specs=out_specs,
        out_shape=out_shape,
        compiler_params=_compiler_params(2),
        name=name,
    )(*operands)
    return outs[0] if side_cast is None else tuple(outs)


def _gating_kernel(za_ref, ng_ref, w_ref, bt_ref, o_ref):
    row = lax.broadcasted_iota(jnp.int32, (CHUNK, CHUNK), 0)
    col = lax.broadcasted_iota(jnp.int32, (CHUNK, CHUNK), 1)
    causal = col <= row
    for c in range(za_ref.shape[0] // CHUNK):
        rows = slice(c * CHUNK, (c + 1) * CHUNK)
        z = jax.nn.gelu(za_ref[rows, :])
        u = z[:, :A_WIDTH]
        v = z[:, A_WIDTH:]
        ms = jnp.mean(v * v, axis=-1, keepdims=True)
        vn = ((v * lax.rsqrt(ms + EPS)) * ng_ref[...]).astype(_BF16)
        for g in range(A_GROUPS):
            sl = slice(g * A_GROUP_DIM, (g + 1) * A_GROUP_DIM)
            w = jnp.where(causal, w_ref[g], 0.0).astype(_BF16)
            s = jnp.dot(w, vn[:, sl], preferred_element_type=_F32) + bt_ref[:, g:g + 1]
            o_ref[rows, sl] = (u[:, sl] * s).astype(o_ref.dtype)


def _spatial_gating(za, norm_g, w_s, b_s, chunks_per_step=2):
    m = za.shape[0]
    tm = chunks_per_step * CHUNK
    return pl.pallas_call(
        _gating_kernel,
        grid=(m // tm,),
        in_specs=[pl.BlockSpec((tm, 2 * A_WIDTH), lambda i: (i, 0)),
                  pl.BlockSpec((1, A_WIDTH), lambda i: (0, 0)),
                  pl.BlockSpec((A_GROUPS, CHUNK, CHUNK), lambda i: (0, 0, 0)),
                  pl.BlockSpec((CHUNK, A_GROUPS), lambda i: (0, 0))],
        out_specs=pl.BlockSpec((tm, A_WIDTH), lambda i: (i, 0)),
        out_shape=jax.ShapeDtypeStruct((m, A_WIDTH), _BF16),
        compiler_params=_compiler_params(1),
        name="spatial_gating",
    )(za, norm_g.reshape(1, A_WIDTH), w_s, b_s.T)


def _rope_tables(positions, rot_dim, period):
    half = rot_dim // 2
    inv_freq = ROPE_THETA ** (-jnp.arange(0, rot_dim, 2, dtype=_F32) / rot_dim)
    ang = positions.astype(_F32)[..., None] * inv_freq
    cos, sin = jnp.cos(ang), jnp.sin(ang)
    zeros_half = jnp.zeros_like(sin)
    rest = jnp.zeros(sin.shape[:-1] + (period - rot_dim,), _F32)

    def lanes(*pieces):
        return jnp.tile(jnp.concatenate(pieces, axis=-1), LANES // period)

    return (lanes(cos, cos, rest + 1.0), lanes(zeros_half, sin, rest),
            lanes(-sin, zeros_half, rest))


def _rope(x, c, s1, s2, half):
    return x * c + pltpu.roll(x, half, 1) * s1 + pltpu.roll(x, LANES - half, 1) * s2


def _prep_kernel(att_ref, cb_ref, s1b_ref, s2b_ref, ci_ref, s1i_ref, s2i_ref,
                 q_ref, k_ref, vt_ref, qi_ref, ki_ref, wi_ref):
    cb, s1b, s2b = cb_ref[0], s1b_ref[0], s2b_ref[0]
    ci, s1i, s2i = ci_ref[0], s1i_ref[0], s2i_ref[0]
    for h in range(B_HEADS):
        x = att_ref[0, :, ATT_Q0 + h * LANES:ATT_Q0 + (h + 1) * LANES]
        q_ref[0, :, h * LANES:(h + 1) * LANES] = _rope(x, cb, s1b, s2b, B_ROT // 2).astype(_BF16)
    for h in range(B_KV_HEADS):
        x = att_ref[0, :, ATT_K0 + h * LANES:ATT_K0 + (h + 1) * LANES]
        k_ref[0, :, h * LANES:(h + 1) * LANES] = _rope(x, cb, s1b, s2b, B_ROT // 2).astype(_BF16)
    vt_ref[0, 0] = att_ref[0, :, ATT_V0:ATT_V0 + B_KV_WIDTH].T.astype(_BF16)
    lane = lax.broadcasted_iota(jnp.int32, ci.shape, 1)
    is_ki = lane < IDX_DIM
    heads_per_group = LANES // IDX_DIM
    for j in range(IDX_WIDTH // LANES):
        x = att_ref[0, :, ATT_QI0 + j * LANES:ATT_QI0 + (j + 1) * LANES]
        y = _rope(x, ci, s1i, s2i, IDX_ROT // 2)
        for r in range(heads_per_group):
            h = j * heads_per_group + r
            yr = y if r == 0 else pltpu.roll(y, LANES - r * IDX_DIM, 1)
            qi_ref[0, :, h * LANES:(h + 1) * LANES] = jnp.where(is_ki, yr, 0.0).astype(_BF16)
    x = att_ref[0, :, ATT_KW0:ATT_KW0 + LANES]
    y = _rope(x, jnp.where(is_ki, ci, 1.0), jnp.where(is_ki, s1i, 0.0),
              jnp.where(is_ki, s2i, 0.0), IDX_ROT // 2)
    ki_ref[0] = jnp.where(is_ki, y, 0.0).astype(_BF16)
    wi_ref[0] = y[:, IDX_DIM:IDX_DIM + IDX_HEADS]


def _prep_attention_inputs(att, tables_b, tables_i, ts=256):
    bsz, seq, _ = att.shape
    tab_spec = pl.BlockSpec((1, ts, LANES), lambda b, i: (b, i, 0))
    per_chunk = KEY_CHUNK // ts

    def out(width, dtype):
        return (pl.BlockSpec((1, ts, width), lambda b, i: (b, i, 0)),
                jax.ShapeDtypeStruct((bsz, seq, width), dtype))

    vt_out = (pl.BlockSpec((1, 1, B_KV_WIDTH, ts),
                           lambda b, i: (b, i // per_chunk, 0, i % per_chunk)),
              jax.ShapeDtypeStruct((bsz, seq // KEY_CHUNK, B_KV_WIDTH, KEY_CHUNK), _BF16))
    outs = [out(B_WIDTH, _BF16), out(B_KV_WIDTH, _BF16), vt_out,
            out(IDX_HEADS * LANES, _BF16), out(LANES, _BF16), out(IDX_HEADS, _F32)]
    return pl.pallas_call(
        _prep_kernel,
        grid=(bsz, seq // ts),
        in_specs=[pl.BlockSpec((1, ts, ATT_WIDTH), lambda b, i: (b, i, 0))] + [tab_spec] * 6,
        out_specs=[o[0] for o in outs],
        out_shape=[o[1] for o in outs],
        compiler_params=_compiler_params(2),
        name="rope_split",
    )(att, *tables_b, *tables_i)


def _ordered_int_to_float(key):
    bits = jnp.where(key < 0, key ^ jnp.int32(0x7FFFFFFF), key)
    return lax.bitcast_convert_type(bits, _F32)


def _fold_rows(x, op):
    while x.shape[0] > SUBLANES:
        half = x.shape[0] // 2
        x = op(x[:half], x[half:])
    return x


def _dsa_kernel(qi_ref, wit_ref, ki_ref, q_ref, k_ref, vt_ref, o_ref,
                score_ref, bias_ref, logit_ref, acc_ref, *, n_sel):
    t = Q_BLOCK
    blk = pl.program_id(1)
    n_chunks = (blk * t + t + KEY_CHUNK - 1) // KEY_CHUNK
    qpos = blk * t + lax.broadcasted_iota(jnp.int32, (1, t), 1)
    kiota = lax.broadcasted_iota(jnp.int32, (KEY_CHUNK, 1), 0)
    idx_scale = (IDX_DIM ** -0.5) * (IDX_HEADS ** -0.5)

    def rows(c):
        return pl.ds(pl.multiple_of(c * KEY_CHUNK, KEY_CHUNK), KEY_CHUNK)

    def for_chunks(body, init):
        return lax.fori_loop(0, n_chunks, body, init)

    def indexer_chunk(c, carry):
        ki = ki_ref[0, rows(c), :]
        acc = jnp.zeros((KEY_CHUNK, t), _F32)
        for h in range(0, IDX_HEADS, 2):
            qpair = jnp.concatenate([qi_ref[0, :, h * LANES:(h + 1) * LANES],
                                     qi_ref[0, :, (h + 1) * LANES:(h + 2) * LANES]], axis=0)
            d = lax.dot_general(ki, qpair, _NT_DIMS, preferred_element_type=_F32)
            acc = acc + wit_ref[0, h:h + 1, :] * jnp.maximum(d[:, :t], 0.0)
            acc = acc + wit_ref[0, h + 1:h + 2, :] * jnp.maximum(d[:, t:], 0.0)
        kpos = c * KEY_CHUNK + kiota
        score_ref[rows(c), :] = jnp.where(kpos <= qpos, acc * idx_scale + 0.0, _NEG_INF)
        return carry

    for_chunks(indexer_chunk, 0)

    def count(indicator):
        def chunk(c, part):
            return part + _fold_rows(indicator(score_ref[rows(c), :], c), jnp.add)
        part = for_chunks(chunk, jnp.zeros((SUBLANES, t), _F32))
        return jnp.sum(part, axis=0, keepdims=True)

    def count_ge(cand_f):
        return count(lambda s, c: jnp.where(s >= cand_f, 1.0, 0.0))

    int_min = jnp.int32(-2 ** 31)
    base = jnp.where(count_ge(jnp.zeros((1, t), _F32)) >= n_sel, jnp.int32(0), int_min)

    def search_step(it, base):
        cand = base | lax.shift_left(jnp.int32(1), jnp.int32(30) - it)
        cnt = count_ge(_ordered_int_to_float(cand))
        return jnp.where(cnt >= n_sel, cand, base)

    base = lax.fori_loop(0, 31, search_step, base)
    thr = jnp.where(qpos + 1 < n_sel, _NEG_INF, _ordered_int_to_float(base))

    def bias_chunk(c, part):
        s = score_ref[rows(c), :]
        sel = jnp.where(c * KEY_CHUNK + kiota <= qpos, jnp.where(s >= thr, 1.0, 0.0), 0.0)
        bias_ref[rows(c), :] = jnp.where(sel > 0.0, 0.0, _NEG_INF)
        return part + _fold_rows(sel, jnp.add)

    n_ge = jnp.sum(for_chunks(bias_chunk, jnp.zeros((SUBLANES, t), _F32)), axis=0, keepdims=True)

    @pl.when(jnp.max(n_ge) > n_sel)
    def _():
        nxt = _ordered_int_to_float(base + 1)
        tied = n_ge > n_sel
        n_above = count(lambda s, c: jnp.where(s >= nxt, 1.0, 0.0))
        need = jnp.where(tied, n_sel - n_above, 0.0)
        no_index = float(k_ref.shape[1])
        front_v = jnp.where(tied, jnp.inf, _NEG_INF)
        front_i = jnp.where(tied, -1.0, no_index)

        def beyond(s, kposf, front_v, front_i):
            after = jnp.where(s < front_v, 1.0,
                              jnp.where(s == front_v, jnp.where(kposf > front_i, 1.0, 0.0), 0.0))
            return jnp.where(s >= thr, jnp.where(s < nxt, after, 0.0), 0.0)

        def kposf(c):
            return (c * KEY_CHUNK + kiota).astype(_F32)

        def advance(_, carry):
            front_v, front_i, need = carry

            def best_value(c, part):
                s = score_ref[rows(c), :]
                cand = jnp.where(beyond(s, kposf(c), front_v, front_i) > 0.0, s, _NEG_INF)
                return jnp.maximum(part, _fold_rows(cand, jnp.maximum))

            v = jnp.max(for_chunks(best_value, jnp.full((SUBLANES, t), _NEG_INF, _F32)),
                        axis=0, keepdims=True)

            def first_index(c, part):
                s = score_ref[rows(c), :]
                hit = jnp.where(s == v, beyond(s, kposf(c), front_v, front_i), 0.0)
                cand = jnp.where(hit > 0.0, kposf(c), no_index)
                return jnp.minimum(part, _fold_rows(cand, jnp.minimum))

            i = jnp.min(for_chunks(first_index, jnp.full((SUBLANES, t), no_index, _F32)),
                        axis=0, keepdims=True)
            active = need > 0.0
            return (jnp.where(active, v, front_v), jnp.where(active, i, front_i),
                    jnp.where(active, need - 1.0, need))

        front_v, front_i, _ = lax.fori_loop(0, jnp.max(need).astype(jnp.int32), advance,
                                            (front_v, front_i, need))

        def tie_chunk(c, carry):
            s = score_ref[rows(c), :]
            upto = jnp.where(s > front_v, 0.0,
                             jnp.where(s == front_v,
                                       jnp.where(kposf(c) <= front_i, 0.0, _NEG_INF), _NEG_INF))
            keep = jnp.where(s >= nxt, 0.0, jnp.where(s >= thr, upto, _NEG_INF))
            bias_ref[rows(c), :] = jnp.where(c * KEY_CHUNK + kiota <= qpos, keep, _NEG_INF)
            return carry

        for_chunks(tie_chunk, 0)

    grp = B_HEADS // B_KV_HEADS
    gw = grp * t
    exp2_scale = (B_HEAD_DIM ** -0.5) * 1.4426950408889634

    def logit_chunk(c, mx):
        bias = jnp.concatenate([bias_ref[rows(c), :]] * grp, axis=1)
        folded = []
        for g in range(B_KV_HEADS):
            qg = jnp.concatenate(
                [q_ref[0, :, (g * grp + hh) * B_HEAD_DIM:(g * grp + hh + 1) * B_HEAD_DIM]
                 for hh in range(grp)], axis=0)
            l = lax.dot_general(k_ref[0, rows(c), g * B_HEAD_DIM:(g + 1) * B_HEAD_DIM], qg,
                                _NT_DIMS, preferred_element_type=_F32) + bias
            logit_ref[rows(c), g * gw:(g + 1) * gw] = l
            folded.append(_fold_rows(l, jnp.maximum))
        return jnp.maximum(mx, jnp.concatenate(folded, axis=1))

    mx = for_chunks(logit_chunk, jnp.full((SUBLANES, B_HEADS * t), _NEG_INF, _F32))
    mx = jnp.max(mx, axis=0, keepdims=True)

    acc_ref[...] = jnp.zeros_like(acc_ref)
    ones_rows = jnp.ones((acc_ref.shape[0] - B_HEAD_DIM, KEY_CHUNK), _BF16)

    def pv_chunk(c, carry):
        for g in range(B_KV_HEADS):
            cols = slice(g * gw, (g + 1) * gw)
            p = jnp.exp2((logit_ref[rows(c), cols] - mx[:, cols]) * exp2_scale)
            vt_ext = jnp.concatenate(
                [vt_ref[0, c, g * B_HEAD_DIM:(g + 1) * B_HEAD_DIM, :], ones_rows], axis=0)
            acc_ref[:, cols] += jnp.dot(vt_ext, p.astype(_BF16), preferred_element_type=_F32)
        return carry

    for_chunks(pv_chunk, 0)
    out_t = acc_ref[:B_HEAD_DIM, :] / acc_ref[B_HEAD_DIM:B_HEAD_DIM + 1, :]
    for h in range(B_HEADS):
        o_ref[0, :, h * B_HEAD_DIM:(h + 1) * B_HEAD_DIM] = (
            out_t[:, h * t:(h + 1) * t].T.astype(o_ref.dtype))


def _dsa_attention(q, k, vt, qi, ki, wit):
    bsz, seq, _ = q.shape
    assert Q_BLOCK == LANES and seq % KEY_CHUNK == 0
    n_sel = min(TOPK_MAX, seq // 4)

    def q_spec(width):
        return pl.BlockSpec((1, Q_BLOCK, width), lambda b, i: (b, i, 0))

    def kv_spec(width):
        return pl.BlockSpec((1, seq, width), lambda b, i: (b, 0, 0))

    return pl.pallas_call(
        functools.partial(_dsa_kernel, n_sel=n_sel),
        grid=(bsz, seq // Q_BLOCK),
        in_specs=[q_spec(IDX_HEADS * LANES),
                  pl.BlockSpec((1, IDX_HEADS, Q_BLOCK), lambda b, i: (b, 0, i)),
                  kv_spec(LANES), q_spec(B_WIDTH), kv_spec(B_KV_WIDTH),
                  pl.BlockSpec((1, seq // KEY_CHUNK, B_KV_WIDTH, KEY_CHUNK),
                               lambda b, i: (b, 0, 0, 0))],
        out_specs=q_spec(B_WIDTH),
        out_shape=jax.ShapeDtypeStruct((bsz, seq, B_WIDTH), _BF16),
        scratch_shapes=[pltpu.VMEM((seq, Q_BLOCK), _F32), pltpu.VMEM((seq, Q_BLOCK), _F32),
                        pltpu.VMEM((seq, B_HEADS * Q_BLOCK), _F32),
                        pltpu.VMEM((B_HEAD_DIM + 2 * SUBLANES, B_HEADS * Q_BLOCK), _F32)],
        compiler_params=_compiler_params(2),
        name="dsa_attention",
    )(qi, wit, ki, q, k, vt)


def _xattn_kernel(q_ref, k_ref, v_ref, o_ref):
    scale = X_HEAD_DIM ** -0.5
    for h in range(X_HEADS):
        hs = slice(h * X_HEAD_DIM, (h + 1) * X_HEAD_DIM)
        logits = lax.dot_general(q_ref[0, :, hs], k_ref[0, :, hs], _NT_DIMS,
                                 preferred_element_type=_F32) * scale
        mx = jnp.max(logits, axis=-1, keepdims=True)
        p = jnp.exp(logits - mx)
        denom = jnp.sum(p, axis=-1, keepdims=True)
        o = jnp.dot(p.astype(_BF16), v_ref[0, :, hs], preferred_element_type=_F32)
        o_ref[0, :, hs] = (o / denom).astype(o_ref.dtype)


def _cross_attention(q, k, v, tq=512):
    bsz, seq, _ = q.shape
    m = k.shape[1]
    mem_spec = pl.BlockSpec((1, m, X_WIDTH), lambda b, i: (b, 0, 0))
    return pl.pallas_call(
        _xattn_kernel,
        grid=(bsz, seq // tq),
        in_specs=[pl.BlockSpec((1, tq, X_WIDTH), lambda b, i: (b, i, 0)), mem_spec, mem_spec],
        out_specs=pl.BlockSpec((1, tq, X_WIDTH), lambda b, i: (b, i, 0)),
        out_shape=jax.ShapeDtypeStruct((bsz, seq, X_WIDTH), _BF16),
        compiler_params=_compiler_params(2),
        name="cross_attention",
    )(q, k, v)


def _norm_proj_kernel(x_ref, g_ref, w_ref, o_ref, w_bf16_ref):
    @pl.when(pl.program_id(0) == 0)
    def _():
        w_bf16_ref[...] = w_ref[...].astype(_BF16)

    x = x_ref[...]
    ms = jnp.mean(x * x, axis=-1, keepdims=True)
    h = ((x * lax.rsqrt(ms + EPS)) * g_ref[...]).astype(_BF16)
    o_ref[...] = jnp.dot(h, w_bf16_ref[...], preferred_element_type=_F32).astype(o_ref.dtype)


def _norm_proj(x, g, w_stack, layer, out_dtype, name, tm=512):
    m, d = x.shape
    n = w_stack.shape[2]
    return pl.pallas_call(
        _norm_proj_kernel,
        grid=(m // tm,),
        in_specs=[pl.BlockSpec((tm, d), lambda i: (i, 0)),
                  pl.BlockSpec((1, d), lambda i: (0, 0)),
                  pl.BlockSpec((None, d, n), lambda i: (layer, 0, 0),
                               pipeline_mode=pl.Buffered(1))],
        out_specs=pl.BlockSpec((tm, n), lambda i: (i, 0)),
        out_shape=jax.ShapeDtypeStruct((m, n), out_dtype),
        scratch_shapes=[pltpu.VMEM((d, n), _BF16)],
        compiler_params=_compiler_params(1),
        name=name,
    )(x, g.reshape(1, d), w_stack)


def _xattn_out_kernel(o_ref, w_ref, x_ref, g_ref, x_out_ref, h_out_ref, w_bf16_ref):
    @pl.when(pl.program_id(0) == 0)
    def _():
        w_bf16_ref[...] = w_ref[...].astype(_BF16)

    x = x_ref[...] + jnp.dot(o_ref[...], w_bf16_ref[...], preferred_element_type=_F32)
    x_out_ref[...] = x
    ms = jnp.mean(x * x, axis=-1, keepdims=True)
    h_out_ref[...] = ((x * lax.rsqrt(ms + EPS)) * g_ref[...]).astype(h_out_ref.dtype)


def _xattn_out_and_norm(o, w_stack, layer, x, g, tm=256):
    m, k = o.shape
    d = x.shape[1]
    row_spec = pl.BlockSpec((tm, d), lambda i: (i, 0))
    return pl.pallas_call(
        _xattn_out_kernel,
        grid=(m // tm,),
        in_specs=[pl.BlockSpec((tm, k), lambda i: (i, 0)),
                  pl.BlockSpec((None, k, d), lambda i: (layer, 0, 0),
                               pipeline_mode=pl.Buffered(1)),
                  row_spec,
                  pl.BlockSpec((1, d), lambda i: (0, 0))],
        out_specs=[row_spec, row_spec],
        out_shape=[jax.ShapeDtypeStruct((m, d), _F32), jax.ShapeDtypeStruct((m, d), _BF16)],
        scratch_shapes=[pltpu.VMEM((k, d), _BF16)],
        compiler_params=_compiler_params(1),
        name="xattn_out_norm",
    )(o, w_stack, x, g.reshape(1, d))


def kernel(x, mem, positions, norm_mix_g, w_in, a_norm_g, a_spatial_w, a_spatial_b, p_a, p_b,
           w_out, norm_x_g, norm_mem_g, xq_w, xk_w, xv_w, xo_w, norm_ffn_g, ffn_w1, ffn_w3,
           ffn_w2, final_norm_g):
    bsz, seq, d = x.shape
    m = bsz * seq
    depth = w_in.shape[0]
    ffn_hidden = ffn_w1.shape[-1]
    tables_b = _rope_tables(positions, B_ROT, B_HEAD_DIM)
    tables_i = _rope_tables(positions, IDX_ROT, IDX_DIM)

    xf = x.reshape(m, d)
    mem_f = mem.reshape(bsz * MEM_LEN, d)
    w_in_t = jnp.transpose(w_in, (0, 2, 1))
    for l in range(depth):
        def w_in_cols(col0):
            return Rhs(w_in_t, l, col0, transposed=True)

        h = _rmsnorm(xf, norm_mix_g[l], _BF16)
        za = _fused_matmul([h], [(0, w_in_cols(0))], [], _ep_identity, ZA_END, _F32,
                           2048, PROJ_TN, "proj_za")
        y_a = _spatial_gating(za, a_norm_g[l], a_spatial_w[l], a_spatial_b[l])
        att = _fused_matmul([h], [(0, w_in_cols(ZA_END))], [], _ep_identity,
                            ATT_WIDTH, _F32, 2048, PROJ_TN, "proj_att")
        q, k, vt, qi, ki, wi = _prep_attention_inputs(att.reshape(bsz, seq, ATT_WIDTH),
                                                      tables_b, tables_i)
        y_b = _dsa_attention(q, k, vt, qi, ki, wi.transpose(0, 2, 1)).reshape(m, B_WIDTH)
        merged = _fused_matmul(
            [y_a, y_b, h],
            [(0, Rhs(p_a, l)), (1, Rhs(p_b, l)), (2, w_in_cols(ATT_END)),
             (2, w_in_cols(ATT_END + d))],
            [], _ep_gated_merge, d, _BF16, 1024, 256, "gated_merge")
        x1 = _fused_matmul([merged], [(0, Rhs(w_out, l))], [xf], _ep_residual,
                           d, _F32, 2048, 256, "out_proj")

        qx = _norm_proj(x1, norm_x_g[l], xq_w, l, _BF16, "xattn_q")
        kx = _norm_proj(mem_f, norm_mem_g[l], xk_w, l, _BF16, "xattn_k")
        vx = _norm_proj(mem_f, norm_mem_g[l], xv_w, l, _BF16, "xattn_v")
        ox = _cross_attention(qx.reshape(bsz, seq, X_WIDTH),
                              kx.reshape(bsz, MEM_LEN, X_WIDTH),
                              vx.reshape(bsz, MEM_LEN, X_WIDTH)).reshape(m, X_WIDTH)
        x2, h2 = _xattn_out_and_norm(ox, xo_w, l, x1, norm_ffn_g[l])

        act, w2_bf16 = _fused_matmul([h2], [(0, Rhs(ffn_w1, l)), (0, Rhs(ffn_w3, l))],
                                     [], _ep_swiglu, ffn_hidden, _BF16, 2048, 256, "ffn_up",
                                     side_cast=(ffn_w2, l))
        xf = _fused_matmul([act], [(0, Rhs(w2_bf16))], [x2], _ep_residual,
                           d, _F32, 1024, 256, "ffn_down")

    return _rmsnorm(xf, final_norm_g, _F32).reshape(bsz, seq, d)
```

```python
import functools
from typing import Callable, NamedTuple, Optional

import jax
import jax.numpy as jnp
from jax import lax
from jax.experimental import pallas as pl
from jax.experimental.pallas import tpu as pltpu

D_MODEL = 4096
MEM_LEN = 256
EPS = 1e-6
ROPE_THETA = 500000.0
CHUNK = 128
A_GROUPS = 16
A_WIDTH = D_MODEL // 2
A_GROUP_DIM = A_WIDTH // A_GROUPS
B_HEADS = 16
B_HEAD_DIM = 128
B_KV_HEADS = 4
B_WIDTH = B_HEADS * B_HEAD_DIM
B_KV_WIDTH = B_KV_HEADS * B_HEAD_DIM
B_ROT = B_HEAD_DIM // 4
IDX_HEADS = 16
IDX_DIM = 64
IDX_WIDTH = IDX_HEADS * IDX_DIM
IDX_ROT = IDX_DIM // 4
TOPK_MAX = 256
Q_BLOCK = 128
X_HEADS = 4
X_HEAD_DIM = 256
X_WIDTH = X_HEADS * X_HEAD_DIM
IN_SIZES = (2 * A_WIDTH, B_WIDTH, B_KV_WIDTH, B_KV_WIDTH, IDX_WIDTH, IDX_DIM, IDX_HEADS,
            2 * D_MODEL)

LANES = 128
SUBLANES = 8
KEY_CHUNK = 512
PROJ_TN = 256
ZA_END = IN_SIZES[0]
ATT_END = ZA_END + sum(IN_SIZES[1:7])
ATT_Q0 = 0
ATT_K0 = ATT_Q0 + B_WIDTH
ATT_V0 = ATT_K0 + B_KV_WIDTH
ATT_QI0 = ATT_V0 + B_KV_WIDTH
ATT_KW0 = ATT_QI0 + IDX_WIDTH
ATT_WIDTH = -(-(ATT_END - ZA_END) // PROJ_TN) * PROJ_TN
assert ZA_END % PROJ_TN == 0 and ATT_KW0 + LANES <= ATT_WIDTH

VMEM_LIMIT_BYTES = 56 * 1024 * 1024

_BF16 = jnp.bfloat16
_F32 = jnp.float32
_NEG_INF = float("-inf")
_NN_DIMS = (((1,), (0,)), ((), ()))
_NT_DIMS = (((1,), (1,)), ((), ()))


def _compiler_params(n_axes):
    return pltpu.CompilerParams(
        dimension_semantics=("arbitrary",) * n_axes,
        vmem_limit_bytes=VMEM_LIMIT_BYTES,
    )


def _rmsnorm_kernel(x_ref, g_ref, o_ref):
    x = x_ref[...]
    ms = jnp.mean(x * x, axis=-1, keepdims=True)
    y = x * lax.rsqrt(ms + EPS)
    o_ref[...] = (y * g_ref[...]).astype(o_ref.dtype)


def _rmsnorm(x, g, out_dtype, tm=512):
    m, d = x.shape
    return pl.pallas_call(
        _rmsnorm_kernel,
        grid=(m // tm,),
        in_specs=[pl.BlockSpec((tm, d), lambda i: (i, 0)),
                  pl.BlockSpec((1, d), lambda i: (0, 0))],
        out_specs=pl.BlockSpec((tm, d), lambda i: (i, 0)),
        out_shape=jax.ShapeDtypeStruct((m, d), out_dtype),
        compiler_params=_compiler_params(1),
        name="rmsnorm",
    )(x, g.reshape(1, d))


def _ep_identity(accs, extras):
    return accs[0]


def _ep_residual(accs, extras):
    return extras[0] + accs[0]


def _ep_gated_merge(accs, extras):
    y_a, y_b, gate_a, gate_b = accs
    return jax.nn.sigmoid(gate_a) * y_a + jax.nn.sigmoid(gate_b) * y_b


def _ep_swiglu(accs, extras):
    return jax.nn.silu(accs[0]) * accs[1]


class Rhs(NamedTuple):
    array: jax.Array
    layer: Optional[int] = None
    col0: int = 0
    transposed: bool = False


def _rhs_spec(r, tn):
    if r.transposed:
        k = r.array.shape[2]
        if r.col0 % tn == 0:
            return pl.BlockSpec((None, tn, k), lambda i, j: (r.layer, j + r.col0 // tn, 0))
        assert r.col0 % SUBLANES == 0
        return pl.BlockSpec((pl.Element(1), pl.Element(tn), pl.Element(k)),
                            lambda i, j: (r.layer, (r.col0 // SUBLANES + j * (tn // SUBLANES))
                                          * SUBLANES, 0))
    assert r.col0 % tn == 0
    off = r.col0 // tn
    if r.layer is None:
        return pl.BlockSpec((r.array.shape[0], tn), lambda i, j: (0, j + off))
    return pl.BlockSpec((None, r.array.shape[1], tn), lambda i, j: (r.layer, 0, j + off))


class SideJob(NamedTuple):
    body: Callable
    arrays: tuple
    in_specs: tuple
    out_specs: tuple
    out_shape: tuple


def _cast_body(x_ref, o_ref):
    o_ref[...] = x_ref[...].astype(o_ref.dtype)


def _cast_side_job(w_stack, layer, n_steps):
    _, rows, cols = w_stack.shape
    slab = rows // n_steps
    assert slab * n_steps == rows and slab % (2 * SUBLANES) == 0
    return SideJob(_cast_body, (w_stack,),
                   (pl.BlockSpec((None, slab, cols), lambda s: (layer, s, 0)),),
                   (pl.BlockSpec((slab, cols), lambda s: (s, 0)),),
                   (jax.ShapeDtypeStruct((rows, cols), _BF16),))


def _mm_kernel(*refs, n_lhs, pair_lhs, rhs_transposed, n_extra, epilogue, side_body, n_side_in):
    n_rhs = len(pair_lhs)
    lhs_refs = refs[:n_lhs]
    rhs_refs = refs[n_lhs:n_lhs + n_rhs]
    n_in = n_lhs + n_rhs + n_extra
    extra_refs = refs[n_lhs + n_rhs:n_in]
    o_ref = refs[n_in + n_side_in]
    if side_body is not None:
        side_body(*refs[n_in:n_in + n_side_in], *refs[n_in + n_side_in + 1:])
    accs = []
    for li, r, tr in zip(pair_lhs, rhs_refs, rhs_transposed):
        dims = _NT_DIMS if tr else _NN_DIMS
        w = r[0] if len(r.shape) == 3 else r[...]
        accs.append(lax.dot_general(lhs_refs[li][...], w.astype(_BF16), dims,
                                    preferred_element_type=_F32))
    o_ref[...] = epilogue(accs, [e[...] for e in extra_refs]).astype(o_ref.dtype)


def _fused_matmul(lhs, pairs, extras, epilogue, n_out, out_dtype, tm, tn, name,
                  side_job=None):
    m = lhs[0].shape[0]
    grid = (m // tm, n_out // tn)
    in_specs = [pl.BlockSpec((tm, a.shape[1]), lambda i, j: (i, 0), pipeline_mode=pl.Buffered(1))
                for a in lhs]
    in_specs += [_rhs_spec(r, tn) for _, r in pairs]
    in_specs += [pl.BlockSpec((tm, tn), lambda i, j: (i, j)) for _ in extras]
    out_specs = [pl.BlockSpec((tm, tn), lambda i, j: (i, j))]
    out_shape = [jax.ShapeDtypeStruct((m, n_out), out_dtype)]
    operands = list(lhs) + [p[1].array for p in pairs] + list(extras)
    side = None if side_job is None else side_job(grid[0] * grid[1])
    if side is not None:
        def per_step(spec):
            return pl.BlockSpec(spec.block_shape,
                                lambda i, j: spec.index_map(i * grid[1] + j))
        in_specs += [per_step(s) for s in side.in_specs]
        out_specs += [per_step(s) for s in side.out_specs]
        out_shape += list(side.out_shape)
        operands += list(side.arrays)
    kern = functools.partial(_mm_kernel, n_lhs=len(lhs), pair_lhs=tuple(p[0] for p in pairs),
                             rhs_transposed=tuple(p[1].transposed for p in pairs),
                             n_extra=len(extras), epilogue=epilogue,
                             side_body=None if side is None else side.body,
                             n_side_in=0 if side is None else len(side.in_specs))
    outs = pl.pallas_call(
        kern,
        grid=grid,
        in_specs=in_specs,
        out_specs=out_specs,
        out_shape=out_shape,
        compiler_params=_compiler_params(2),
        name=name,
    )(*operands)
    return outs[0] if side is None else tuple(outs)


def _gating_kernel(za_ref, ng_ref, w_ref, bt_ref, o_ref):
    row = lax.broadcasted_iota(jnp.int32, (CHUNK, CHUNK), 0)
    col = lax.broadcasted_iota(jnp.int32, (CHUNK, CHUNK), 1)
    causal = col <= row
    for c in range(za_ref.shape[0] // CHUNK):
        rows = slice(c * CHUNK, (c + 1) * CHUNK)
        z = jax.nn.gelu(za_ref[rows, :])
        u = z[:, :A_WIDTH]
        v = z[:, A_WIDTH:]
        ms = jnp.mean(v * v, axis=-1, keepdims=True)
        vn = ((v * lax.rsqrt(ms + EPS)) * ng_ref[...]).astype(_BF16)
        for g in range(A_GROUPS):
            sl = slice(g * A_GROUP_DIM, (g + 1) * A_GROUP_DIM)
            w = jnp.where(causal, w_ref[g], 0.0).astype(_BF16)
            s = jnp.dot(w, vn[:, sl], preferred_element_type=_F32) + bt_ref[:, g:g + 1]
            o_ref[rows, sl] = (u[:, sl] * s).astype(o_ref.dtype)


def _spatial_gating(za, norm_g, w_s, b_s, chunks_per_step=2):
    m = za.shape[0]
    tm = chunks_per_step * CHUNK
    return pl.pallas_call(
        _gating_kernel,
        grid=(m // tm,),
        in_specs=[pl.BlockSpec((tm, 2 * A_WIDTH), lambda i: (i, 0)),
                  pl.BlockSpec((1, A_WIDTH), lambda i: (0, 0)),
                  pl.BlockSpec((A_GROUPS, CHUNK, CHUNK), lambda i: (0, 0, 0)),
                  pl.BlockSpec((CHUNK, A_GROUPS), lambda i: (0, 0))],
        out_specs=pl.BlockSpec((tm, A_WIDTH), lambda i: (i, 0)),
        out_shape=jax.ShapeDtypeStruct((m, A_WIDTH), _BF16),
        compiler_params=_compiler_params(1),
        name="spatial_gating",
    )(za, norm_g.reshape(1, A_WIDTH), w_s, b_s.T)


def _rope_tables(positions, rot_dim, period):
    half = rot_dim // 2
    inv_freq = ROPE_THETA ** (-jnp.arange(0, rot_dim, 2, dtype=_F32) / rot_dim)
    ang = positions.astype(_F32)[..., None] * inv_freq
    cos, sin = jnp.cos(ang), jnp.sin(ang)
    zeros_half = jnp.zeros_like(sin)
    rest = jnp.zeros(sin.shape[:-1] + (period - rot_dim,), _F32)

    def lanes(*pieces):
        return jnp.tile(jnp.concatenate(pieces, axis=-1), LANES // period)

    return (lanes(cos, cos, rest + 1.0), lanes(zeros_half, sin, rest),
            lanes(-sin, zeros_half, rest))


def _rope(x, c, s1, s2, half):
    return x * c + pltpu.roll(x, half, 1) * s1 + pltpu.roll(x, LANES - half, 1) * s2


def _prep_kernel(att_ref, cb_ref, s1b_ref, s2b_ref, ci_ref, s1i_ref, s2i_ref,
                 q_ref, k_ref, vt_ref, qi_ref, ki_ref, wi_ref):
    cb, s1b, s2b = cb_ref[0], s1b_ref[0], s2b_ref[0]
    ci, s1i, s2i = ci_ref[0], s1i_ref[0], s2i_ref[0]
    for h in range(B_HEADS):
        x = att_ref[0, :, ATT_Q0 + h * LANES:ATT_Q0 + (h + 1) * LANES]
        q_ref[0, :, h * LANES:(h + 1) * LANES] = _rope(x, cb, s1b, s2b, B_ROT // 2).astype(_BF16)
    for h in range(B_KV_HEADS):
        x = att_ref[0, :, ATT_K0 + h * LANES:ATT_K0 + (h + 1) * LANES]
        k_ref[0, :, h * LANES:(h + 1) * LANES] = _rope(x, cb, s1b, s2b, B_ROT // 2).astype(_BF16)
    vt_ref[0, 0] = att_ref[0, :, ATT_V0:ATT_V0 + B_KV_WIDTH].T.astype(_BF16)
    lane = lax.broadcasted_iota(jnp.int32, ci.shape, 1)
    is_ki = lane < IDX_DIM
    heads_per_group = LANES // IDX_DIM
    for j in range(IDX_WIDTH // LANES):
        x = att_ref[0, :, ATT_QI0 + j * LANES:ATT_QI0 + (j + 1) * LANES]
        y = _rope(x, ci, s1i, s2i, IDX_ROT // 2)
        for r in range(heads_per_group):
            h = j * heads_per_group + r
            yr = y if r == 0 else pltpu.roll(y, LANES - r * IDX_DIM, 1)
            qi_ref[0, :, h * LANES:(h + 1) * LANES] = jnp.where(is_ki, yr, 0.0).astype(_BF16)
    x = att_ref[0, :, ATT_KW0:ATT_KW0 + LANES]
    y = _rope(x, jnp.where(is_ki, ci, 1.0), jnp.where(is_ki, s1i, 0.0),
              jnp.where(is_ki, s2i, 0.0), IDX_ROT // 2)
    ki_ref[0] = jnp.where(is_ki, y, 0.0).astype(_BF16)
    wi_ref[0] = y[:, IDX_DIM:IDX_DIM + IDX_HEADS]


def _prep_side_job(att, tables_b, tables_i, n_steps):
    bsz, seq, _ = att.shape
    ts = bsz * seq // n_steps
    assert ts * n_steps == bsz * seq and seq % ts == 0 and KEY_CHUNK % ts == 0
    assert ts % (2 * SUBLANES) == 0
    per_seq = seq // ts
    per_chunk = KEY_CHUNK // ts

    def tokens(width):
        return pl.BlockSpec((1, ts, width), lambda s: (s // per_seq, s % per_seq, 0))

    def out(width, dtype):
        return tokens(width), jax.ShapeDtypeStruct((bsz, seq, width), dtype)

    vt_out = (pl.BlockSpec((1, 1, B_KV_WIDTH, ts),
                           lambda s: (s // per_seq, (s % per_seq) // per_chunk, 0,
                                      (s % per_seq) % per_chunk)),
              jax.ShapeDtypeStruct((bsz, seq // KEY_CHUNK, B_KV_WIDTH, KEY_CHUNK), _BF16))
    outs = [out(B_WIDTH, _BF16), out(B_KV_WIDTH, _BF16), vt_out,
            out(IDX_HEADS * LANES, _BF16), out(LANES, _BF16), out(IDX_HEADS, _F32)]
    return SideJob(_prep_kernel, (att, *tables_b, *tables_i),
                   (tokens(ATT_WIDTH),) + (tokens(LANES),) * 6,
                   tuple(o[0] for o in outs), tuple(o[1] for o in outs))


def _ordered_int_to_float(key):
    bits = jnp.where(key < 0, key ^ jnp.int32(0x7FFFFFFF), key)
    return lax.bitcast_convert_type(bits, _F32)


def _fold_rows(x, op):
    while x.shape[0] > SUBLANES:
        half = x.shape[0] // 2
        x = op(x[:half], x[half:])
    return x


def _dsa_kernel(qi_ref, wit_ref, ki_ref, q_ref, k_ref, vt_ref, o_ref,
                score_ref, bias_ref, logit_ref, acc_ref, *, n_sel):
    t = Q_BLOCK
    blk = pl.program_id(1)
    n_chunks = (blk * t + t + KEY_CHUNK - 1) // KEY_CHUNK
    qpos = blk * t + lax.broadcasted_iota(jnp.int32, (1, t), 1)
    kiota = lax.broadcasted_iota(jnp.int32, (KEY_CHUNK, 1), 0)
    idx_scale = (IDX_DIM ** -0.5) * (IDX_HEADS ** -0.5)

    def rows(c):
        return pl.ds(pl.multiple_of(c * KEY_CHUNK, KEY_CHUNK), KEY_CHUNK)

    def for_chunks(body, init):
        return lax.fori_loop(0, n_chunks, body, init)

    def indexer_chunk(c, carry):
        ki = ki_ref[0, rows(c), :]
        acc = jnp.zeros((KEY_CHUNK, t), _F32)
        for h in range(0, IDX_HEADS, 2):
            qpair = jnp.concatenate([qi_ref[0, :, h * LANES:(h + 1) * LANES],
                                     qi_ref[0, :, (h + 1) * LANES:(h + 2) * LANES]], axis=0)
            d = lax.dot_general(ki, qpair, _NT_DIMS, preferred_element_type=_F32)
            acc = acc + wit_ref[0, h:h + 1, :] * jnp.maximum(d[:, :t], 0.0)
            acc = acc + wit_ref[0, h + 1:h + 2, :] * jnp.maximum(d[:, t:], 0.0)
        kpos = c * KEY_CHUNK + kiota
        score_ref[rows(c), :] = jnp.where(kpos <= qpos, acc * idx_scale + 0.0, _NEG_INF)
        return carry

    for_chunks(indexer_chunk, 0)

    def count(indicator):
        def chunk(c, part):
            return part + _fold_rows(indicator(score_ref[rows(c), :], c), jnp.add)
        part = for_chunks(chunk, jnp.zeros((SUBLANES, t), _F32))
        return jnp.sum(part, axis=0, keepdims=True)

    def count_ge(cand_f):
        return count(lambda s, c: jnp.where(s >= cand_f, 1.0, 0.0))

    int_min = jnp.int32(-2 ** 31)
    base = jnp.where(count_ge(jnp.zeros((1, t), _F32)) >= n_sel, jnp.int32(0), int_min)

    def search_step(it, base):
        cand = base | lax.shift_left(jnp.int32(1), jnp.int32(30) - it)
        cnt = count_ge(_ordered_int_to_float(cand))
        return jnp.where(cnt >= n_sel, cand, base)

    base = lax.fori_loop(0, 31, search_step, base)
    thr = jnp.where(qpos + 1 < n_sel, _NEG_INF, _ordered_int_to_float(base))

    def bias_chunk(c, part):
        s = score_ref[rows(c), :]
        sel = jnp.where(c * KEY_CHUNK + kiota <= qpos, jnp.where(s >= thr, 1.0, 0.0), 0.0)
        bias_ref[rows(c), :] = jnp.where(sel > 0.0, 0.0, _NEG_INF)
        return part + _fold_rows(sel, jnp.add)

    n_ge = jnp.sum(for_chunks(bias_chunk, jnp.zeros((SUBLANES, t), _F32)), axis=0, keepdims=True)

    @pl.when(jnp.max(n_ge) > n_sel)
    def _():
        nxt = _ordered_int_to_float(base + 1)
        tied = n_ge > n_sel
        n_above = count(lambda s, c: jnp.where(s >= nxt, 1.0, 0.0))
        need = jnp.where(tied, n_sel - n_above, 0.0)
        no_index = float(k_ref.shape[1])
        front_v = jnp.where(tied, jnp.inf, _NEG_INF)
        front_i = jnp.where(tied, -1.0, no_index)

        def beyond(s, kposf, front_v, front_i):
            after = jnp.where(s < front_v, 1.0,
                              jnp.where(s == front_v, jnp.where(kposf > front_i, 1.0, 0.0), 0.0))
            return jnp.where(s >= thr, jnp.where(s < nxt, after, 0.0), 0.0)

        def kposf(c):
            return (c * KEY_CHUNK + kiota).astype(_F32)

        def advance(_, carry):
            front_v, front_i, need = carry

            def best_value(c, part):
                s = score_ref[rows(c), :]
                cand = jnp.where(beyond(s, kposf(c), front_v, front_i) > 0.0, s, _NEG_INF)
                return jnp.maximum(part, _fold_rows(cand, jnp.maximum))

            v = jnp.max(for_chunks(best_value, jnp.full((SUBLANES, t), _NEG_INF, _F32)),
                        axis=0, keepdims=True)

            def first_index(c, part):
                s = score_ref[rows(c), :]
                hit = jnp.where(s == v, beyond(s, kposf(c), front_v, front_i), 0.0)
                cand = jnp.where(hit > 0.0, kposf(c), no_index)
                return jnp.minimum(part, _fold_rows(cand, jnp.minimum))

            i = jnp.min(for_chunks(first_index, jnp.full((SUBLANES, t), no_index, _F32)),
                        axis=0, keepdims=True)
            active = need > 0.0
            return (jnp.where(active, v, front_v), jnp.where(active, i, front_i),
                    jnp.where(active, need - 1.0, need))

        front_v, front_i, _ = lax.fori_loop(0, jnp.max(need).astype(jnp.int32), advance,
                                            (front_v, front_i, need))

        def tie_chunk(c, carry):
            s = score_ref[rows(c), :]
            upto = jnp.where(s > front_v, 0.0,
                             jnp.where(s == front_v,
                                       jnp.where(kposf(c) <= front_i, 0.0, _NEG_INF), _NEG_INF))
            keep = jnp.where(s >= nxt, 0.0, jnp.where(s >= thr, upto, _NEG_INF))
            bias_ref[rows(c), :] = jnp.where(c * KEY_CHUNK + kiota <= qpos, keep, _NEG_INF)
            return carry

        for_chunks(tie_chunk, 0)

    grp = B_HEADS // B_KV_HEADS
    gw = grp * t
    exp2_scale = (B_HEAD_DIM ** -0.5) * 1.4426950408889634

    def logit_chunk(c, mx):
        bias = jnp.concatenate([bias_ref[rows(c), :]] * grp, axis=1)
        folded = []
        for g in range(B_KV_HEADS):
            qg = jnp.concatenate(
                [q_ref[0, :, (g * grp + hh) * B_HEAD_DIM:(g * grp + hh + 1) * B_HEAD_DIM]
                 for hh in range(grp)], axis=0)
            l = lax.dot_general(k_ref[0, rows(c), g * B_HEAD_DIM:(g + 1) * B_HEAD_DIM], qg,
                                _NT_DIMS, preferred_element_type=_F32) + bias
            logit_ref[rows(c), g * gw:(g + 1) * gw] = l
            folded.append(_fold_rows(l, jnp.maximum))
        return jnp.maximum(mx, jnp.concatenate(folded, axis=1))

    mx = for_chunks(logit_chunk, jnp.full((SUBLANES, B_HEADS * t), _NEG_INF, _F32))
    mx = jnp.max(mx, axis=0, keepdims=True)

    acc_ref[...] = jnp.zeros_like(acc_ref)
    ones_rows = jnp.ones((acc_ref.shape[0] - B_HEAD_DIM, KEY_CHUNK), _BF16)

    def pv_chunk(c, carry):
        for g in range(B_KV_HEADS):
            cols = slice(g * gw, (g + 1) * gw)
            p = jnp.exp2((logit_ref[rows(c), cols] - mx[:, cols]) * exp2_scale)
            vt_ext = jnp.concatenate(
                [vt_ref[0, c, g * B_HEAD_DIM:(g + 1) * B_HEAD_DIM, :], ones_rows], axis=0)
            acc_ref[:, cols] += jnp.dot(vt_ext, p.astype(_BF16), preferred_element_type=_F32)
        return carry

    for_chunks(pv_chunk, 0)
    out_t = acc_ref[:B_HEAD_DIM, :] / acc_ref[B_HEAD_DIM:B_HEAD_DIM + 1, :]
    for h in range(B_HEADS):
        o_ref[0, :, h * B_HEAD_DIM:(h + 1) * B_HEAD_DIM] = (
            out_t[:, h * t:(h + 1) * t].T.astype(o_ref.dtype))


def _dsa_attention(q, k, vt, qi, ki, wit):
    bsz, seq, _ = q.shape
    assert Q_BLOCK == LANES and seq % KEY_CHUNK == 0
    n_sel = min(TOPK_MAX, seq // 4)

    def q_spec(width):
        return pl.BlockSpec((1, Q_BLOCK, width), lambda b, i: (b, i, 0))

    def kv_spec(width):
        return pl.BlockSpec((1, seq, width), lambda b, i: (b, 0, 0))

    return pl.pallas_call(
        functools.partial(_dsa_kernel, n_sel=n_sel),
        grid=(bsz, seq // Q_BLOCK),
        in_specs=[q_spec(IDX_HEADS * LANES),
                  pl.BlockSpec((1, IDX_HEADS, Q_BLOCK), lambda b, i: (b, 0, i)),
                  kv_spec(LANES), q_spec(B_WIDTH), kv_spec(B_KV_WIDTH),
                  pl.BlockSpec((1, seq // KEY_CHUNK, B_KV_WIDTH, KEY_CHUNK),
                               lambda b, i: (b, 0, 0, 0))],
        out_specs=q_spec(B_WIDTH),
        out_shape=jax.ShapeDtypeStruct((bsz, seq, B_WIDTH), _BF16),
        scratch_shapes=[pltpu.VMEM((seq, Q_BLOCK), _F32), pltpu.VMEM((seq, Q_BLOCK), _F32),
                        pltpu.VMEM((seq, B_HEADS * Q_BLOCK), _F32),
                        pltpu.VMEM((B_HEAD_DIM + 2 * SUBLANES, B_HEADS * Q_BLOCK), _F32)],
        compiler_params=_compiler_params(2),
        name="dsa_attention",
    )(qi, wit, ki, q, k, vt)


def _xattn_kernel(q_ref, k_ref, v_ref, o_ref):
    scale = X_HEAD_DIM ** -0.5
    for h in range(X_HEADS):
        hs = slice(h * X_HEAD_DIM, (h + 1) * X_HEAD_DIM)
        logits = lax.dot_general(q_ref[0, :, hs], k_ref[0, :, hs], _NT_DIMS,
                                 preferred_element_type=_F32) * scale
        mx = jnp.max(logits, axis=-1, keepdims=True)
        p = jnp.exp(logits - mx)
        denom = jnp.sum(p, axis=-1, keepdims=True)
        o = jnp.dot(p.astype(_BF16), v_ref[0, :, hs], preferred_element_type=_F32)
        o_ref[0, :, hs] = (o / denom).astype(o_ref.dtype)


def _cross_attention(q, k, v, tq=512):
    bsz, seq, _ = q.shape
    m = k.shape[1]
    mem_spec = pl.BlockSpec((1, m, X_WIDTH), lambda b, i: (b, 0, 0))
    return pl.pallas_call(
        _xattn_kernel,
        grid=(bsz, seq // tq),
        in_specs=[pl.BlockSpec((1, tq, X_WIDTH), lambda b, i: (b, i, 0)), mem_spec, mem_spec],
        out_specs=pl.BlockSpec((1, tq, X_WIDTH), lambda b, i: (b, i, 0)),
        out_shape=jax.ShapeDtypeStruct((bsz, seq, X_WIDTH), _BF16),
        compiler_params=_compiler_params(2),
        name="cross_attention",
    )(q, k, v)


def _norm_proj_kernel(x_ref, g_ref, w_ref, o_ref, w_bf16_ref):
    @pl.when(pl.program_id(0) == 0)
    def _():
        w_bf16_ref[...] = w_ref[...].astype(_BF16)

    x = x_ref[...]
    ms = jnp.mean(x * x, axis=-1, keepdims=True)
    h = ((x * lax.rsqrt(ms + EPS)) * g_ref[...]).astype(_BF16)
    o_ref[...] = jnp.dot(h, w_bf16_ref[...], preferred_element_type=_F32).astype(o_ref.dtype)


def _norm_proj(x, g, w_stack, layer, out_dtype, name, tm=512):
    m, d = x.shape
    n = w_stack.shape[2]
    return pl.pallas_call(
        _norm_proj_kernel,
        grid=(m // tm,),
        in_specs=[pl.BlockSpec((tm, d), lambda i: (i, 0)),
                  pl.BlockSpec((1, d), lambda i: (0, 0)),
                  pl.BlockSpec((None, d, n), lambda i: (layer, 0, 0),
                               pipeline_mode=pl.Buffered(1))],
        out_specs=pl.BlockSpec((tm, n), lambda i: (i, 0)),
        out_shape=jax.ShapeDtypeStruct((m, n), out_dtype),
        scratch_shapes=[pltpu.VMEM((d, n), _BF16)],
        compiler_params=_compiler_params(1),
        name=name,
    )(x, g.reshape(1, d), w_stack)


def _xattn_out_kernel(o_ref, w_ref, x_ref, g_ref, x_out_ref, h_out_ref, w_bf16_ref):
    @pl.when(pl.program_id(0) == 0)
    def _():
        w_bf16_ref[...] = w_ref[...].astype(_BF16)

    x = x_ref[...] + jnp.dot(o_ref[...], w_bf16_ref[...], preferred_element_type=_F32)
    x_out_ref[...] = x
    ms = jnp.mean(x * x, axis=-1, keepdims=True)
    h_out_ref[...] = ((x * lax.rsqrt(ms + EPS)) * g_ref[...]).astype(h_out_ref.dtype)


def _xattn_out_and_norm(o, w_stack, layer, x, g, tm=256):
    m, k = o.shape
    d = x.shape[1]
    row_spec = pl.BlockSpec((tm, d), lambda i: (i, 0))
    return pl.pallas_call(
        _xattn_out_kernel,
        grid=(m // tm,),
        in_specs=[pl.BlockSpec((tm, k), lambda i: (i, 0)),
                  pl.BlockSpec((None, k, d), lambda i: (layer, 0, 0),
                               pipeline_mode=pl.Buffered(1)),
                  row_spec,
                  pl.BlockSpec((1, d), lambda i: (0, 0))],
        out_specs=[row_spec, row_spec],
        out_shape=[jax.ShapeDtypeStruct((m, d), _F32), jax.ShapeDtypeStruct((m, d), _BF16)],
        scratch_shapes=[pltpu.VMEM((k, d), _BF16)],
        compiler_params=_compiler_params(1),
        name="xattn_out_norm",
    )(o, w_stack, x, g.reshape(1, d))


def kernel(x, mem, positions, norm_mix_g, w_in, a_norm_g, a_spatial_w, a_spatial_b, p_a, p_b,
           w_out, norm_x_g, norm_mem_g, xq_w, xk_w, xv_w, xo_w, norm_ffn_g, ffn_w1, ffn_w3,
           ffn_w2, final_norm_g):
    bsz, seq, d = x.shape
    m = bsz * seq
    depth = w_in.shape[0]
    ffn_hidden = ffn_w1.shape[-1]
    tables_b = _rope_tables(positions, B_ROT, B_HEAD_DIM)
    tables_i = _rope_tables(positions, IDX_ROT, IDX_DIM)

    xf = x.reshape(m, d)
    mem_f = mem.reshape(bsz * MEM_LEN, d)
    w_in_t = jnp.transpose(w_in, (0, 2, 1))
    for l in range(depth):
        def w_in_cols(col0):
            return Rhs(w_in_t, l, col0, transposed=True)

        h = _rmsnorm(xf, norm_mix_g[l], _BF16)
        att = _fused_matmul([h], [(0, w_in_cols(ZA_END))], [], _ep_identity,
                            ATT_WIDTH, _F32, 2048, PROJ_TN, "proj_att")
        za, q, k, vt, qi, ki, wi = _fused_matmul(
            [h], [(0, w_in_cols(0))], [], _ep_identity, ZA_END, _F32, 2048, PROJ_TN, "proj_za",
            side_job=functools.partial(_prep_side_job, att.reshape(bsz, seq, ATT_WIDTH),
                                       tables_b, tables_i))
        y_a = _spatial_gating(za, a_norm_g[l], a_spatial_w[l], a_spatial_b[l])
        y_b = _dsa_attention(q, k, vt, qi, ki, wi.transpose(0, 2, 1)).reshape(m, B_WIDTH)
        merged = _fused_matmul(
            [y_a, y_b, h],
            [(0, Rhs(p_a, l)), (1, Rhs(p_b, l)), (2, w_in_cols(ATT_END)),
             (2, w_in_cols(ATT_END + d))],
            [], _ep_gated_merge, d, _BF16, 1024, 256, "gated_merge")
        x1 = _fused_matmul([merged], [(0, Rhs(w_out, l))], [xf], _ep_residual,
                           d, _F32, 2048, 256, "out_proj")

        qx = _norm_proj(x1, norm_x_g[l], xq_w, l, _BF16, "xattn_q")
        kx = _norm_proj(mem_f, norm_mem_g[l], xk_w, l, _BF16, "xattn_k")
        vx = _norm_proj(mem_f, norm_mem_g[l], xv_w, l, _BF16, "xattn_v")
        ox = _cross_attention(qx.reshape(bsz, seq, X_WIDTH),
                              kx.reshape(bsz, MEM_LEN, X_WIDTH),
                              vx.reshape(bsz, MEM_LEN, X_WIDTH)).reshape(m, X_WIDTH)
        x2, h2 = _xattn_out_and_norm(ox, xo_w, l, x1, norm_ffn_g[l])

        act, w2_bf16 = _fused_matmul([h2], [(0, Rhs(ffn_w1, l)), (0, Rhs(ffn_w3, l))],
                                     [], _ep_swiglu, ffn_hidden, _BF16, 2048, 256, "ffn_up",
                                     side_job=functools.partial(_cast_side_job, ffn_w2, l))
        xf = _fused_matmul([act], [(0, Rhs(w2_bf16))], [x2], _ep_residual,
                           d, _F32, 1024, 256, "ffn_down")

    return _rmsnorm(xf, final_norm_g, _F32).reshape(bsz, seq, d)
```

```python
import functools
from typing import Callable, NamedTuple, Optional

import jax
import jax.numpy as jnp
from jax import lax
from jax.experimental import pallas as pl
from jax.experimental.pallas import tpu as pltpu

D_MODEL = 4096
MEM_LEN = 256
EPS = 1e-6
ROPE_THETA = 500000.0
CHUNK = 128
A_GROUPS = 16
A_WIDTH = D_MODEL // 2
A_GROUP_DIM = A_WIDTH // A_GROUPS
B_HEADS = 16
B_HEAD_DIM = 128
B_KV_HEADS = 4
B_WIDTH = B_HEADS * B_HEAD_DIM
B_KV_WIDTH = B_KV_HEADS * B_HEAD_DIM
B_ROT = B_HEAD_DIM // 4
IDX_HEADS = 16
IDX_DIM = 64
IDX_WIDTH = IDX_HEADS * IDX_DIM
IDX_ROT = IDX_DIM // 4
TOPK_MAX = 256
Q_BLOCK = 128
X_HEADS = 4
X_HEAD_DIM = 256
X_WIDTH = X_HEADS * X_HEAD_DIM
IN_SIZES = (2 * A_WIDTH, B_WIDTH, B_KV_WIDTH, B_KV_WIDTH, IDX_WIDTH, IDX_DIM, IDX_HEADS,
            2 * D_MODEL)

LANES = 128
SUBLANES = 8
KEY_CHUNK = 512
PROJ_TN = 512
ZA_END = IN_SIZES[0]
ATT_END = ZA_END + sum(IN_SIZES[1:7])
ATT_Q0 = 0
ATT_K0 = ATT_Q0 + B_WIDTH
ATT_V0 = ATT_K0 + B_KV_WIDTH
ATT_QI0 = ATT_V0 + B_KV_WIDTH
ATT_KW0 = ATT_QI0 + IDX_WIDTH
ATT_WIDTH = -(-(ATT_END - ZA_END) // PROJ_TN) * PROJ_TN
assert ZA_END % PROJ_TN == 0 and ATT_KW0 + LANES <= ATT_WIDTH

VMEM_LIMIT_BYTES = 60 * 1024 * 1024

_BF16 = jnp.bfloat16
_F32 = jnp.float32
_NEG_INF = float("-inf")
_NN_DIMS = (((1,), (0,)), ((), ()))
_NT_DIMS = (((1,), (1,)), ((), ()))


def _compiler_params(n_axes):
    return pltpu.CompilerParams(
        dimension_semantics=("arbitrary",) * n_axes,
        vmem_limit_bytes=VMEM_LIMIT_BYTES,
    )


def _rmsnorm_kernel(x_ref, g_ref, o_ref):
    x = x_ref[...]
    ms = jnp.mean(x * x, axis=-1, keepdims=True)
    y = x * lax.rsqrt(ms + EPS)
    o_ref[...] = (y * g_ref[...]).astype(o_ref.dtype)


def _rmsnorm(x, g, out_dtype, tm=512):
    m, d = x.shape
    return pl.pallas_call(
        _rmsnorm_kernel,
        grid=(m // tm,),
        in_specs=[pl.BlockSpec((tm, d), lambda i: (i, 0)),
                  pl.BlockSpec((1, d), lambda i: (0, 0))],
        out_specs=pl.BlockSpec((tm, d), lambda i: (i, 0)),
        out_shape=jax.ShapeDtypeStruct((m, d), out_dtype),
        compiler_params=_compiler_params(1),
        name="rmsnorm",
    )(x, g.reshape(1, d))


def _ep_identity(accs, extras):
    return accs[0]


def _ep_residual(accs, extras):
    return extras[0] + accs[0]


def _ep_gated_merge(accs, extras):
    y_a, y_b, gate_a, gate_b = accs
    return jax.nn.sigmoid(gate_a) * y_a + jax.nn.sigmoid(gate_b) * y_b


def _ep_swiglu(accs, extras):
    return jax.nn.silu(accs[0]) * accs[1]


class Rhs(NamedTuple):
    array: jax.Array
    layer: Optional[int] = None
    col0: int = 0
    transposed: bool = False


def _rhs_spec(r, tn):
    if r.transposed:
        k = r.array.shape[2]
        if r.col0 % tn == 0:
            return pl.BlockSpec((None, tn, k), lambda i, j: (r.layer, j + r.col0 // tn, 0))
        assert r.col0 % SUBLANES == 0
        return pl.BlockSpec((pl.Element(1), pl.Element(tn), pl.Element(k)),
                            lambda i, j: (r.layer, (r.col0 // SUBLANES + j * (tn // SUBLANES))
                                          * SUBLANES, 0))
    assert r.col0 % tn == 0
    off = r.col0 // tn
    if r.layer is None:
        return pl.BlockSpec((r.array.shape[0], tn), lambda i, j: (0, j + off))
    return pl.BlockSpec((None, r.array.shape[1], tn), lambda i, j: (r.layer, 0, j + off))


class SideJob(NamedTuple):
    body: Callable
    arrays: tuple
    in_specs: tuple
    out_specs: tuple
    out_shape: tuple


def _cast_body(x_ref, o_ref):
    o_ref[...] = x_ref[...].astype(o_ref.dtype)


def _cast_side_job(w_stack, layer, n_steps):
    _, rows, cols = w_stack.shape
    slab = rows // n_steps
    assert slab * n_steps == rows and slab % (2 * SUBLANES) == 0
    return SideJob(_cast_body, (w_stack,),
                   (pl.BlockSpec((None, slab, cols), lambda s: (layer, s, 0)),),
                   (pl.BlockSpec((slab, cols), lambda s: (s, 0)),),
                   (jax.ShapeDtypeStruct((rows, cols), _BF16),))


def _mm_kernel(*refs, n_lhs, pair_lhs, rhs_transposed, n_extra, epilogue, side_body, n_side_in,
               row_splits):
    n_rhs = len(pair_lhs)
    lhs_refs = refs[:n_lhs]
    rhs_refs = refs[n_lhs:n_lhs + n_rhs]
    n_in = n_lhs + n_rhs + n_extra
    extra_refs = refs[n_lhs + n_rhs:n_in]
    o_ref = refs[n_in + n_side_in]
    if side_body is not None:
        side_body(*refs[n_in:n_in + n_side_in], *refs[n_in + n_side_in + 1:])
    weights = [(r[0] if len(r.shape) == 3 else r[...]).astype(_BF16) for r in rhs_refs]
    rows_per_split = o_ref.shape[0] // row_splits
    for s in range(row_splits):
        rows = slice(s * rows_per_split, (s + 1) * rows_per_split)
        accs = [lax.dot_general(lhs_refs[li][rows, :], w, _NT_DIMS if tr else _NN_DIMS,
                                preferred_element_type=_F32)
                for li, w, tr in zip(pair_lhs, weights, rhs_transposed)]
        o_ref[rows, :] = epilogue(accs, [e[rows, :] for e in extra_refs]).astype(o_ref.dtype)


def _fused_matmul(lhs, pairs, extras, epilogue, n_out, out_dtype, tm, tn, name,
                  side_job=None, row_splits=1):
    m = lhs[0].shape[0]
    grid = (m // tm, n_out // tn)
    in_specs = [pl.BlockSpec((tm, a.shape[1]), lambda i, j: (i, 0), pipeline_mode=pl.Buffered(1))
                for a in lhs]
    in_specs += [_rhs_spec(r, tn) for _, r in pairs]
    in_specs += [pl.BlockSpec((tm, tn), lambda i, j: (i, j)) for _ in extras]
    out_specs = [pl.BlockSpec((tm, tn), lambda i, j: (i, j))]
    out_shape = [jax.ShapeDtypeStruct((m, n_out), out_dtype)]
    operands = list(lhs) + [p[1].array for p in pairs] + list(extras)
    side = None if side_job is None else side_job(grid[0] * grid[1])
    if side is not None:
        def per_step(spec):
            return pl.BlockSpec(spec.block_shape,
                                lambda i, j: spec.index_map(i * grid[1] + j))
        in_specs += [per_step(s) for s in side.in_specs]
        out_specs += [per_step(s) for s in side.out_specs]
        out_shape += list(side.out_shape)
        operands += list(side.arrays)
    kern = functools.partial(_mm_kernel, n_lhs=len(lhs), pair_lhs=tuple(p[0] for p in pairs),
                             rhs_transposed=tuple(p[1].transposed for p in pairs),
                             n_extra=len(extras), epilogue=epilogue,
                             side_body=None if side is None else side.body,
                             n_side_in=0 if side is None else len(side.in_specs),
                             row_splits=row_splits)
    outs = pl.pallas_call(
        kern,
        grid=grid,
        in_specs=in_specs,
        out_specs=out_specs,
        out_shape=out_shape,
        compiler_params=_compiler_params(2),
        name=name,
    )(*operands)
    return outs[0] if side is None else tuple(outs)


def _gating_kernel(za_ref, ng_ref, w_ref, bt_ref, o_ref):
    row = lax.broadcasted_iota(jnp.int32, (CHUNK, CHUNK), 0)
    col = lax.broadcasted_iota(jnp.int32, (CHUNK, CHUNK), 1)
    causal = col <= row
    for c in range(za_ref.shape[0] // CHUNK):
        rows = slice(c * CHUNK, (c + 1) * CHUNK)
        z = jax.nn.gelu(za_ref[rows, :])
        u = z[:, :A_WIDTH]
        v = z[:, A_WIDTH:]
        ms = jnp.mean(v * v, axis=-1, keepdims=True)
        vn = ((v * lax.rsqrt(ms + EPS)) * ng_ref[...]).astype(_BF16)
        for g in range(A_GROUPS):
            sl = slice(g * A_GROUP_DIM, (g + 1) * A_GROUP_DIM)
            w = jnp.where(causal, w_ref[g], 0.0).astype(_BF16)
            s = jnp.dot(w, vn[:, sl], preferred_element_type=_F32) + bt_ref[:, g:g + 1]
            o_ref[rows, sl] = (u[:, sl] * s).astype(o_ref.dtype)


def _spatial_gating(za, norm_g, w_s, b_s, chunks_per_step=2):
    m = za.shape[0]
    tm = chunks_per_step * CHUNK
    return pl.pallas_call(
        _gating_kernel,
        grid=(m // tm,),
        in_specs=[pl.BlockSpec((tm, 2 * A_WIDTH), lambda i: (i, 0)),
                  pl.BlockSpec((1, A_WIDTH), lambda i: (0, 0)),
                  pl.BlockSpec((A_GROUPS, CHUNK, CHUNK), lambda i: (0, 0, 0)),
                  pl.BlockSpec((CHUNK, A_GROUPS), lambda i: (0, 0))],
        out_specs=pl.BlockSpec((tm, A_WIDTH), lambda i: (i, 0)),
        out_shape=jax.ShapeDtypeStruct((m, A_WIDTH), _BF16),
        compiler_params=_compiler_params(1),
        name="spatial_gating",
    )(za, norm_g.reshape(1, A_WIDTH), w_s, b_s.T)


def _rope_tables(positions, rot_dim, period):
    half = rot_dim // 2
    inv_freq = ROPE_THETA ** (-jnp.arange(0, rot_dim, 2, dtype=_F32) / rot_dim)
    ang = positions.astype(_F32)[..., None] * inv_freq
    cos, sin = jnp.cos(ang), jnp.sin(ang)
    zeros_half = jnp.zeros_like(sin)
    rest = jnp.zeros(sin.shape[:-1] + (period - rot_dim,), _F32)

    def lanes(*pieces):
        return jnp.tile(jnp.concatenate(pieces, axis=-1), LANES // period)

    return (lanes(cos, cos, rest + 1.0), lanes(zeros_half, sin, rest),
            lanes(-sin, zeros_half, rest))


def _rope(x, c, s1, s2, half):
    return x * c + pltpu.roll(x, half, 1) * s1 + pltpu.roll(x, LANES - half, 1) * s2


def _prep_kernel(att_ref, cb_ref, s1b_ref, s2b_ref, ci_ref, s1i_ref, s2i_ref,
                 q_ref, k_ref, vt_ref, qi_ref, ki_ref, wi_ref):
    cb, s1b, s2b = cb_ref[0], s1b_ref[0], s2b_ref[0]
    ci, s1i, s2i = ci_ref[0], s1i_ref[0], s2i_ref[0]
    for h in range(B_HEADS):
        x = att_ref[0, :, ATT_Q0 + h * LANES:ATT_Q0 + (h + 1) * LANES]
        q_ref[0, :, h * LANES:(h + 1) * LANES] = _rope(x, cb, s1b, s2b, B_ROT // 2).astype(_BF16)
    for h in range(B_KV_HEADS):
        x = att_ref[0, :, ATT_K0 + h * LANES:ATT_K0 + (h + 1) * LANES]
        k_ref[0, :, h * LANES:(h + 1) * LANES] = _rope(x, cb, s1b, s2b, B_ROT // 2).astype(_BF16)
    vt_ref[0, 0] = att_ref[0, :, ATT_V0:ATT_V0 + B_KV_WIDTH].T.astype(_BF16)
    lane = lax.broadcasted_iota(jnp.int32, ci.shape, 1)
    is_ki = lane < IDX_DIM
    heads_per_group = LANES // IDX_DIM
    for j in range(IDX_WIDTH // LANES):
        x = att_ref[0, :, ATT_QI0 + j * LANES:ATT_QI0 + (j + 1) * LANES]
        y = _rope(x, ci, s1i, s2i, IDX_ROT // 2)
        for r in range(heads_per_group):
            h = j * heads_per_group + r
            yr = y if r == 0 else pltpu.roll(y, LANES - r * IDX_DIM, 1)
            qi_ref[0, :, h * LANES:(h + 1) * LANES] = jnp.where(is_ki, yr, 0.0).astype(_BF16)
    x = att_ref[0, :, ATT_KW0:ATT_KW0 + LANES]
    y = _rope(x, jnp.where(is_ki, ci, 1.0), jnp.where(is_ki, s1i, 0.0),
              jnp.where(is_ki, s2i, 0.0), IDX_ROT // 2)
    ki_ref[0] = jnp.where(is_ki, y, 0.0).astype(_BF16)
    wi_ref[0] = y[:, IDX_DIM:IDX_DIM + IDX_HEADS]


def _prep_side_job(att, tables_b, tables_i, n_steps):
    bsz, seq, _ = att.shape
    ts = bsz * seq // n_steps
    assert ts * n_steps == bsz * seq and seq % ts == 0 and KEY_CHUNK % ts == 0
    assert ts % (2 * SUBLANES) == 0
    per_seq = seq // ts
    per_chunk = KEY_CHUNK // ts

    def tokens(width):
        return pl.BlockSpec((1, ts, width), lambda s: (s // per_seq, s % per_seq, 0))

    def out(width, dtype):
        return tokens(width), jax.ShapeDtypeStruct((bsz, seq, width), dtype)

    vt_out = (pl.BlockSpec((1, 1, B_KV_WIDTH, ts),
                           lambda s: (s // per_seq, (s % per_seq) // per_chunk, 0,
                                      (s % per_seq) % per_chunk)),
              jax.ShapeDtypeStruct((bsz, seq // KEY_CHUNK, B_KV_WIDTH, KEY_CHUNK), _BF16))
    outs = [out(B_WIDTH, _BF16), out(B_KV_WIDTH, _BF16), vt_out,
            out(IDX_HEADS * LANES, _BF16), out(LANES, _BF16), out(IDX_HEADS, _F32)]
    return SideJob(_prep_kernel, (att, *tables_b, *tables_i),
                   (tokens(ATT_WIDTH),) + (tokens(LANES),) * 6,
                   tuple(o[0] for o in outs), tuple(o[1] for o in outs))


def _ordered_int_to_float(key):
    bits = jnp.where(key < 0, key ^ jnp.int32(0x7FFFFFFF), key)
    return lax.bitcast_convert_type(bits, _F32)


def _fold_rows(x, op):
    while x.shape[0] > SUBLANES:
        half = x.shape[0] // 2
        x = op(x[:half], x[half:])
    return x


def _dsa_kernel(*refs, n_sel):
    seq = refs[4].shape[1]
    n_chunks = (pl.program_id(1) * Q_BLOCK + Q_BLOCK + KEY_CHUNK - 1) // KEY_CHUNK
    for n in range(1, seq // KEY_CHUNK + 1):
        pl.when(n_chunks == n)(functools.partial(_dsa_block, *refs, n_sel=n_sel, n_chunks=n))


def _dsa_block(qi_ref, wit_ref, ki_ref, q_ref, k_ref, vt_ref, o_ref,
               score_ref, bias_ref, logit_ref, acc_ref, *, n_sel, n_chunks):
    t = Q_BLOCK
    blk = pl.program_id(1)
    qpos = blk * t + lax.broadcasted_iota(jnp.int32, (1, t), 1)
    kiota = lax.broadcasted_iota(jnp.int32, (KEY_CHUNK, 1), 0)
    idx_scale = (IDX_DIM ** -0.5) * (IDX_HEADS ** -0.5)

    def rows(c):
        return slice(c * KEY_CHUNK, (c + 1) * KEY_CHUNK)

    def for_chunks(body, carry):
        for c in range(n_chunks):
            carry = body(c, carry)
        return carry

    def indexer_chunk(c, carry):
        ki = ki_ref[0, rows(c), :]
        acc = jnp.zeros((KEY_CHUNK, t), _F32)
        for h in range(0, IDX_HEADS, 2):
            qpair = jnp.concatenate([qi_ref[0, :, h * LANES:(h + 1) * LANES],
                                     qi_ref[0, :, (h + 1) * LANES:(h + 2) * LANES]], axis=0)
            d = lax.dot_general(ki, qpair, _NT_DIMS, preferred_element_type=_F32)
            acc = acc + wit_ref[0, h:h + 1, :] * jnp.maximum(d[:, :t], 0.0)
            acc = acc + wit_ref[0, h + 1:h + 2, :] * jnp.maximum(d[:, t:], 0.0)
        kpos = c * KEY_CHUNK + kiota
        score_ref[rows(c), :] = jnp.where(kpos <= qpos, acc * idx_scale + 0.0, _NEG_INF)
        return carry

    for_chunks(indexer_chunk, 0)

    def count(indicator):
        def chunk(c, part):
            return part + _fold_rows(indicator(score_ref[rows(c), :], c), jnp.add)
        part = for_chunks(chunk, jnp.zeros((SUBLANES, t), _F32))
        return jnp.sum(part, axis=0, keepdims=True)

    def count_ge(cand_f):
        return count(lambda s, c: jnp.where(s >= cand_f, 1.0, 0.0))

    int_min = jnp.int32(-2 ** 31)
    base = jnp.where(count_ge(jnp.zeros((1, t), _F32)) >= n_sel, jnp.int32(0), int_min)

    def search_step(it, base):
        cand = base | lax.shift_left(jnp.int32(1), jnp.int32(30) - it)
        cnt = count_ge(_ordered_int_to_float(cand))
        return jnp.where(cnt >= n_sel, cand, base)

    base = lax.fori_loop(0, 31, search_step, base)
    thr = jnp.where(qpos + 1 < n_sel, _NEG_INF, _ordered_int_to_float(base))

    def bias_chunk(c, part):
        s = score_ref[rows(c), :]
        sel = jnp.where(c * KEY_CHUNK + kiota <= qpos, jnp.where(s >= thr, 1.0, 0.0), 0.0)
        bias_ref[rows(c), :] = jnp.where(sel > 0.0, 0.0, _NEG_INF)
        return part + _fold_rows(sel, jnp.add)

    n_ge = jnp.sum(for_chunks(bias_chunk, jnp.zeros((SUBLANES, t), _F32)), axis=0, keepdims=True)

    @pl.when(jnp.max(n_ge) > n_sel)
    def _():
        nxt = _ordered_int_to_float(base + 1)
        tied = n_ge > n_sel
        n_above = count(lambda s, c: jnp.where(s >= nxt, 1.0, 0.0))
        need = jnp.where(tied, n_sel - n_above, 0.0)
        no_index = float(k_ref.shape[1])
        front_v = jnp.where(tied, jnp.inf, _NEG_INF)
        front_i = jnp.where(tied, -1.0, no_index)

        def beyond(s, kposf, front_v, front_i):
            after = jnp.where(s < front_v, 1.0,
                              jnp.where(s == front_v, jnp.where(kposf > front_i, 1.0, 0.0), 0.0))
            return jnp.where(s >= thr, jnp.where(s < nxt, after, 0.0), 0.0)

        def kposf(c):
            return (c * KEY_CHUNK + kiota).astype(_F32)

        def advance(_, carry):
            front_v, front_i, need = carry

            def best_value(c, part):
                s = score_ref[rows(c), :]
                cand = jnp.where(beyond(s, kposf(c), front_v, front_i) > 0.0, s, _NEG_INF)
                return jnp.maximum(part, _fold_rows(cand, jnp.maximum))

            v = jnp.max(for_chunks(best_value, jnp.full((SUBLANES, t), _NEG_INF, _F32)),
                        axis=0, keepdims=True)

            def first_index(c, part):
                s = score_ref[rows(c), :]
                hit = jnp.where(s == v, beyond(s, kposf(c), front_v, front_i), 0.0)
                cand = jnp.where(hit > 0.0, kposf(c), no_index)
                return jnp.minimum(part, _fold_rows(cand, jnp.minimum))

            i = jnp.min(for_chunks(first_index, jnp.full((SUBLANES, t), no_index, _F32)),
                        axis=0, keepdims=True)
            active = need > 0.0
            return (jnp.where(active, v, front_v), jnp.where(active, i, front_i),
                    jnp.where(active, need - 1.0, need))

        front_v, front_i, _ = lax.fori_loop(0, jnp.max(need).astype(jnp.int32), advance,
                                            (front_v, front_i, need))

        def tie_chunk(c, carry):
            s = score_ref[rows(c), :]
            upto = jnp.where(s > front_v, 0.0,
                             jnp.where(s == front_v,
                                       jnp.where(kposf(c) <= front_i, 0.0, _NEG_INF), _NEG_INF))
            keep = jnp.where(s >= nxt, 0.0, jnp.where(s >= thr, upto, _NEG_INF))
            bias_ref[rows(c), :] = jnp.where(c * KEY_CHUNK + kiota <= qpos, keep, _NEG_INF)
            return carry

        for_chunks(tie_chunk, 0)

    grp = B_HEADS // B_KV_HEADS
    gw = grp * t
    exp2_scale = (B_HEAD_DIM ** -0.5) * 1.4426950408889634

    def logit_chunk(c, mx):
        bias = jnp.concatenate([bias_ref[rows(c), :]] * grp, axis=1)
        folded = []
        for g in range(B_KV_HEADS):
            qg = jnp.concatenate(
                [q_ref[0, :, (g * grp + hh) * B_HEAD_DIM:(g * grp + hh + 1) * B_HEAD_DIM]
                 for hh in range(grp)], axis=0)
            l = lax.dot_general(k_ref[0, rows(c), g * B_HEAD_DIM:(g + 1) * B_HEAD_DIM], qg,
                                _NT_DIMS, preferred_element_type=_F32) + bias
            logit_ref[rows(c), g * gw:(g + 1) * gw] = l
            folded.append(_fold_rows(l, jnp.maximum))
        return jnp.maximum(mx, jnp.concatenate(folded, axis=1))

    mx = for_chunks(logit_chunk, jnp.full((SUBLANES, B_HEADS * t), _NEG_INF, _F32))
    mx = jnp.max(mx, axis=0, keepdims=True)

    acc_ref[...] = jnp.zeros_like(acc_ref)
    ones_rows = jnp.ones((acc_ref.shape[0] - B_HEAD_DIM, KEY_CHUNK), _BF16)

    def pv_chunk(c, carry):
        for g in range(B_KV_HEADS):
            cols = slice(g * gw, (g + 1) * gw)
            p = jnp.exp2((logit_ref[rows(c), cols] - mx[:, cols]) * exp2_scale)
            vt_ext = jnp.concatenate(
                [vt_ref[0, c, g * B_HEAD_DIM:(g + 1) * B_HEAD_DIM, :], ones_rows], axis=0)
            acc_ref[:, cols] += jnp.dot(vt_ext, p.astype(_BF16), preferred_element_type=_F32)
        return carry

    for_chunks(pv_chunk, 0)
    out_t = acc_ref[:B_HEAD_DIM, :] / acc_ref[B_HEAD_DIM:B_HEAD_DIM + 1, :]
    for h in range(B_HEADS):
        o_ref[0, :, h * B_HEAD_DIM:(h + 1) * B_HEAD_DIM] = (
            out_t[:, h * t:(h + 1) * t].T.astype(o_ref.dtype))


def _dsa_attention(q, k, vt, qi, ki, wit):
    bsz, seq, _ = q.shape
    assert Q_BLOCK == LANES and seq % KEY_CHUNK == 0
    n_sel = min(TOPK_MAX, seq // 4)

    def q_spec(width):
        return pl.BlockSpec((1, Q_BLOCK, width), lambda b, i: (b, i, 0))

    def kv_spec(width):
        return pl.BlockSpec((1, seq, width), lambda b, i: (b, 0, 0))

    return pl.pallas_call(
        functools.partial(_dsa_kernel, n_sel=n_sel),
        grid=(bsz, seq // Q_BLOCK),
        in_specs=[q_spec(IDX_HEADS * LANES),
                  pl.BlockSpec((1, IDX_HEADS, Q_BLOCK), lambda b, i: (b, 0, i)),
                  kv_spec(LANES), q_spec(B_WIDTH), kv_spec(B_KV_WIDTH),
                  pl.BlockSpec((1, seq // KEY_CHUNK, B_KV_WIDTH, KEY_CHUNK),
                               lambda b, i: (b, 0, 0, 0))],
        out_specs=q_spec(B_WIDTH),
        out_shape=jax.ShapeDtypeStruct((bsz, seq, B_WIDTH), _BF16),
        scratch_shapes=[pltpu.VMEM((seq, Q_BLOCK), _F32), pltpu.VMEM((seq, Q_BLOCK), _F32),
                        pltpu.VMEM((seq, B_HEADS * Q_BLOCK), _F32),
                        pltpu.VMEM((B_HEAD_DIM + 2 * SUBLANES, B_HEADS * Q_BLOCK), _F32)],
        compiler_params=_compiler_params(2),
        name="dsa_attention",
    )(qi, wit, ki, q, k, vt)


def _xattn_kernel(q_ref, k_ref, v_ref, o_ref):
    scale = X_HEAD_DIM ** -0.5
    for h in range(X_HEADS):
        hs = slice(h * X_HEAD_DIM, (h + 1) * X_HEAD_DIM)
        logits = lax.dot_general(q_ref[0, :, hs], k_ref[0, :, hs], _NT_DIMS,
                                 preferred_element_type=_F32) * scale
        mx = jnp.max(logits, axis=-1, keepdims=True)
        p = jnp.exp(logits - mx)
        denom = jnp.sum(p, axis=-1, keepdims=True)
        o = jnp.dot(p.astype(_BF16), v_ref[0, :, hs], preferred_element_type=_F32)
        o_ref[0, :, hs] = (o / denom).astype(o_ref.dtype)


def _cross_attention(q, k, v, tq=512):
    bsz, seq, _ = q.shape
    m = k.shape[1]
    mem_spec = pl.BlockSpec((1, m, X_WIDTH), lambda b, i: (b, 0, 0))
    return pl.pallas_call(
        _xattn_kernel,
        grid=(bsz, seq // tq),
        in_specs=[pl.BlockSpec((1, tq, X_WIDTH), lambda b, i: (b, i, 0)), mem_spec, mem_spec],
        out_specs=pl.BlockSpec((1, tq, X_WIDTH), lambda b, i: (b, i, 0)),
        out_shape=jax.ShapeDtypeStruct((bsz, seq, X_WIDTH), _BF16),
        compiler_params=_compiler_params(2),
        name="cross_attention",
    )(q, k, v)


def _norm_proj_kernel(x_ref, g_ref, w_ref, o_ref, w_bf16_ref):
    @pl.when(pl.program_id(0) == 0)
    def _():
        w_bf16_ref[...] = w_ref[...].astype(_BF16)

    x = x_ref[...]
    ms = jnp.mean(x * x, axis=-1, keepdims=True)
    h = ((x * lax.rsqrt(ms + EPS)) * g_ref[...]).astype(_BF16)
    o_ref[...] = jnp.dot(h, w_bf16_ref[...], preferred_element_type=_F32).astype(o_ref.dtype)


def _norm_proj(x, g, w_stack, layer, out_dtype, name, tm=512):
    m, d = x.shape
    n = w_stack.shape[2]
    return pl.pallas_call(
        _norm_proj_kernel,
        grid=(m // tm,),
        in_specs=[pl.BlockSpec((tm, d), lambda i: (i, 0)),
                  pl.BlockSpec((1, d), lambda i: (0, 0)),
                  pl.BlockSpec((None, d, n), lambda i: (layer, 0, 0),
                               pipeline_mode=pl.Buffered(1))],
        out_specs=pl.BlockSpec((tm, n), lambda i: (i, 0)),
        out_shape=jax.ShapeDtypeStruct((m, n), out_dtype),
        scratch_shapes=[pltpu.VMEM((d, n), _BF16)],
        compiler_params=_compiler_params(1),
        name=name,
    )(x, g.reshape(1, d), w_stack)


def _xattn_out_kernel(o_ref, w_ref, x_ref, g_ref, x_out_ref, h_out_ref, w_bf16_ref):
    @pl.when(pl.program_id(0) == 0)
    def _():
        w_bf16_ref[...] = w_ref[...].astype(_BF16)

    x = x_ref[...] + jnp.dot(o_ref[...], w_bf16_ref[...], preferred_element_type=_F32)
    x_out_ref[...] = x
    ms = jnp.mean(x * x, axis=-1, keepdims=True)
    h_out_ref[...] = ((x * lax.rsqrt(ms + EPS)) * g_ref[...]).astype(h_out_ref.dtype)


def _xattn_out_and_norm(o, w_stack, layer, x, g, tm=256):
    m, k = o.shape
    d = x.shape[1]
    row_spec = pl.BlockSpec((tm, d), lambda i: (i, 0))
    return pl.pallas_call(
        _xattn_out_kernel,
        grid=(m // tm,),
        in_specs=[pl.BlockSpec((tm, k), lambda i: (i, 0)),
                  pl.BlockSpec((None, k, d), lambda i: (layer, 0, 0),
                               pipeline_mode=pl.Buffered(1)),
                  row_spec,
                  pl.BlockSpec((1, d), lambda i: (0, 0))],
        out_specs=[row_spec, row_spec],
        out_shape=[jax.ShapeDtypeStruct((m, d), _F32), jax.ShapeDtypeStruct((m, d), _BF16)],
        scratch_shapes=[pltpu.VMEM((k, d), _BF16)],
        compiler_params=_compiler_params(1),
        name="xattn_out_norm",
    )(o, w_stack, x, g.reshape(1, d))


def kernel(x, mem, positions, norm_mix_g, w_in, a_norm_g, a_spatial_w, a_spatial_b, p_a, p_b,
           w_out, norm_x_g, norm_mem_g, xq_w, xk_w, xv_w, xo_w, norm_ffn_g, ffn_w1, ffn_w3,
           ffn_w2, final_norm_g):
    bsz, seq, d = x.shape
    m = bsz * seq
    depth = w_in.shape[0]
    ffn_hidden = ffn_w1.shape[-1]
    tables_b = _rope_tables(positions, B_ROT, B_HEAD_DIM)
    tables_i = _rope_tables(positions, IDX_ROT, IDX_DIM)

    xf = x.reshape(m, d)
    mem_f = mem.reshape(bsz * MEM_LEN, d)
    w_in_t = jnp.transpose(w_in, (0, 2, 1))
    for l in range(depth):
        def w_in_cols(col0):
            return Rhs(w_in_t, l, col0, transposed=True)

        h = _rmsnorm(xf, norm_mix_g[l], _BF16)
        att = _fused_matmul([h], [(0, w_in_cols(ZA_END))], [], _ep_identity,
                            ATT_WIDTH, _F32, 2048, PROJ_TN, "proj_att")
        za, q, k, vt, qi, ki, wi = _fused_matmul(
            [h], [(0, w_in_cols(0))], [], _ep_identity, ZA_END, _F32, 2048, 256, "proj_za",
            side_job=functools.partial(_prep_side_job, att.reshape(bsz, seq, ATT_WIDTH),
                                       tables_b, tables_i))
        y_a = _spatial_gating(za, a_norm_g[l], a_spatial_w[l], a_spatial_b[l])
        y_b = _dsa_attention(q, k, vt, qi, ki, wi.transpose(0, 2, 1)).reshape(m, B_WIDTH)
        merged = _fused_matmul(
            [y_a, y_b, h],
            [(0, Rhs(p_a, l)), (1, Rhs(p_b, l)), (2, w_in_cols(ATT_END)),
             (2, w_in_cols(ATT_END + d))],
            [], _ep_gated_merge, d, _BF16, 1024, 256, "gated_merge", row_splits=2)
        x1 = _fused_matmul([merged], [(0, Rhs(w_out, l))], [xf], _ep_residual,
                           d, _F32, 2048, 512, "out_proj")

        qx = _norm_proj(x1, norm_x_g[l], xq_w, l, _BF16, "xattn_q")
        kx = _norm_proj(mem_f, norm_mem_g[l], xk_w, l, _BF16, "xattn_k")
        vx = _norm_proj(mem_f, norm_mem_g[l], xv_w, l, _BF16, "xattn_v")
        ox = _cross_attention(qx.reshape(bsz, seq, X_WIDTH),
                              kx.reshape(bsz, MEM_LEN, X_WIDTH),
                              vx.reshape(bsz, MEM_LEN, X_WIDTH)).reshape(m, X_WIDTH)
        x2, h2 = _xattn_out_and_norm(ox, xo_w, l, x1, norm_ffn_g[l])

        act, w2_bf16 = _fused_matmul([h2], [(0, Rhs(ffn_w1, l)), (0, Rhs(ffn_w3, l))],
                                     [], _ep_swiglu, ffn_hidden, _BF16, 2048, 256, "ffn_up",
                                     side_job=functools.partial(_cast_side_job, ffn_w2, l))
        xf = _fused_matmul([act], [(0, Rhs(w2_bf16))], [x2], _ep_residual,
                           d, _F32, 1024, 512, "ffn_down")

    return _rmsnorm(xf, final_norm_g, _F32).reshape(bsz, seq, d)
```

```python
import functools
from typing import Callable, NamedTuple, Optional

import jax
import jax.numpy as jnp
from jax import lax
from jax.experimental import pallas as pl
from jax.experimental.pallas import tpu as pltpu

D_MODEL = 4096
MEM_LEN = 256
EPS = 1e-6
ROPE_THETA = 500000.0
CHUNK = 128
A_GROUPS = 16
A_WIDTH = D_MODEL // 2
A_GROUP_DIM = A_WIDTH // A_GROUPS
B_HEADS = 16
B_HEAD_DIM = 128
B_KV_HEADS = 4
B_WIDTH = B_HEADS * B_HEAD_DIM
B_KV_WIDTH = B_KV_HEADS * B_HEAD_DIM
B_ROT = B_HEAD_DIM // 4
IDX_HEADS = 16
IDX_DIM = 64
IDX_WIDTH = IDX_HEADS * IDX_DIM
IDX_ROT = IDX_DIM // 4
TOPK_MAX = 256
Q_BLOCK = 128
X_HEADS = 4
X_HEAD_DIM = 256
X_WIDTH = X_HEADS * X_HEAD_DIM
IN_SIZES = (2 * A_WIDTH, B_WIDTH, B_KV_WIDTH, B_KV_WIDTH, IDX_WIDTH, IDX_DIM, IDX_HEADS,
            2 * D_MODEL)

LANES = 128
SUBLANES = 8
KEY_CHUNK = 512
PROJ_TN = 512
ZA_END = IN_SIZES[0]
ATT_END = ZA_END + sum(IN_SIZES[1:7])
ATT_Q0 = 0
ATT_K0 = ATT_Q0 + B_WIDTH
ATT_V0 = ATT_K0 + B_KV_WIDTH
ATT_QI0 = ATT_V0 + B_KV_WIDTH
ATT_KW0 = ATT_QI0 + IDX_WIDTH
ATT_WIDTH = -(-(ATT_END - ZA_END) // PROJ_TN) * PROJ_TN
assert ZA_END % PROJ_TN == 0 and ATT_KW0 + LANES <= ATT_WIDTH

VMEM_LIMIT_BYTES = 60 * 1024 * 1024

_BF16 = jnp.bfloat16
_F32 = jnp.float32
_NEG_INF = float("-inf")
_NN_DIMS = (((1,), (0,)), ((), ()))
_NT_DIMS = (((1,), (1,)), ((), ()))


def _compiler_params(n_axes):
    return pltpu.CompilerParams(
        dimension_semantics=("arbitrary",) * n_axes,
        vmem_limit_bytes=VMEM_LIMIT_BYTES,
    )


def _rmsnorm_kernel(x_ref, g_ref, o_ref):
    x = x_ref[...]
    ms = jnp.mean(x * x, axis=-1, keepdims=True)
    y = x * lax.rsqrt(ms + EPS)
    o_ref[...] = (y * g_ref[...]).astype(o_ref.dtype)


def _rmsnorm(x, g, out_dtype, tm=512):
    m, d = x.shape
    return pl.pallas_call(
        _rmsnorm_kernel,
        grid=(m // tm,),
        in_specs=[pl.BlockSpec((tm, d), lambda i: (i, 0)),
                  pl.BlockSpec((1, d), lambda i: (0, 0))],
        out_specs=pl.BlockSpec((tm, d), lambda i: (i, 0)),
        out_shape=jax.ShapeDtypeStruct((m, d), out_dtype),
        compiler_params=_compiler_params(1),
        name="rmsnorm",
    )(x, g.reshape(1, d))


def _ep_identity(accs, extras):
    return accs[0]


def _ep_residual(accs, extras):
    return extras[0] + accs[0]


def _ep_gated_merge(accs, extras):
    y_a, y_b, gate_a, gate_b = accs
    return jax.nn.sigmoid(gate_a) * y_a + jax.nn.sigmoid(gate_b) * y_b


def _ep_swiglu(accs, extras):
    return jax.nn.silu(accs[0]) * accs[1]


class Rhs(NamedTuple):
    array: jax.Array
    layer: Optional[int] = None
    col0: int = 0
    transposed: bool = False


def _rhs_spec(r, tn):
    if r.transposed:
        k = r.array.shape[2]
        if r.col0 % tn == 0:
            return pl.BlockSpec((None, tn, k), lambda i, j: (r.layer, j + r.col0 // tn, 0))
        assert r.col0 % SUBLANES == 0
        return pl.BlockSpec((pl.Element(1), pl.Element(tn), pl.Element(k)),
                            lambda i, j: (r.layer, (r.col0 // SUBLANES + j * (tn // SUBLANES))
                                          * SUBLANES, 0))
    assert r.col0 % tn == 0
    off = r.col0 // tn
    if r.layer is None:
        return pl.BlockSpec((r.array.shape[0], tn), lambda i, j: (0, j + off))
    return pl.BlockSpec((None, r.array.shape[1], tn), lambda i, j: (r.layer, 0, j + off))


class SideJob(NamedTuple):
    body: Callable
    arrays: tuple
    in_specs: tuple
    out_specs: tuple
    out_shape: tuple


def _cast_body(x_ref, o_ref):
    o_ref[...] = x_ref[...].astype(o_ref.dtype)


def _cast_side_job(w_stack, layer, n_steps):
    _, rows, cols = w_stack.shape
    slab = rows // n_steps
    assert slab * n_steps == rows and slab % (2 * SUBLANES) == 0
    return SideJob(_cast_body, (w_stack,),
                   (pl.BlockSpec((None, slab, cols), lambda s: (layer, s, 0)),),
                   (pl.BlockSpec((slab, cols), lambda s: (s, 0)),),
                   (jax.ShapeDtypeStruct((rows, cols), _BF16),))


def _mm_kernel(*refs, n_lhs, pair_lhs, rhs_transposed, n_extra, epilogue, side_body, n_side_in,
               row_splits):
    n_rhs = len(pair_lhs)
    lhs_refs = refs[:n_lhs]
    rhs_refs = refs[n_lhs:n_lhs + n_rhs]
    n_in = n_lhs + n_rhs + n_extra
    extra_refs = refs[n_lhs + n_rhs:n_in]
    o_ref = refs[n_in + n_side_in]
    if side_body is not None:
        side_body(*refs[n_in:n_in + n_side_in], *refs[n_in + n_side_in + 1:])
    weights = [(r[0] if len(r.shape) == 3 else r[...]).astype(_BF16) for r in rhs_refs]
    rows_per_split = o_ref.shape[0] // row_splits
    for s in range(row_splits):
        rows = slice(s * rows_per_split, (s + 1) * rows_per_split)
        accs = [lax.dot_general(lhs_refs[li][rows, :], w, _NT_DIMS if tr else _NN_DIMS,
                                preferred_element_type=_F32)
                for li, w, tr in zip(pair_lhs, weights, rhs_transposed)]
        o_ref[rows, :] = epilogue(accs, [e[rows, :] for e in extra_refs]).astype(o_ref.dtype)


def _fused_matmul(lhs, pairs, extras, epilogue, n_out, out_dtype, tm, tn, name,
                  side_job=None, row_splits=1, prefetch_lhs=False):
    m = lhs[0].shape[0]
    grid = (m // tm, n_out // tn)
    lhs_mode = {} if prefetch_lhs else {"pipeline_mode": pl.Buffered(1)}
    in_specs = [pl.BlockSpec((tm, a.shape[1]), lambda i, j: (i, 0), **lhs_mode) for a in lhs]
    in_specs += [_rhs_spec(r, tn) for _, r in pairs]
    in_specs += [pl.BlockSpec((tm, tn), lambda i, j: (i, j)) for _ in extras]
    out_specs = [pl.BlockSpec((tm, tn), lambda i, j: (i, j))]
    out_shape = [jax.ShapeDtypeStruct((m, n_out), out_dtype)]
    operands = list(lhs) + [p[1].array for p in pairs] + list(extras)
    side = None if side_job is None else side_job(grid[0] * grid[1])
    if side is not None:
        def per_step(spec):
            return pl.BlockSpec(spec.block_shape,
                                lambda i, j: spec.index_map(i * grid[1] + j))
        in_specs += [per_step(s) for s in side.in_specs]
        out_specs += [per_step(s) for s in side.out_specs]
        out_shape += list(side.out_shape)
        operands += list(side.arrays)
    kern = functools.partial(_mm_kernel, n_lhs=len(lhs), pair_lhs=tuple(p[0] for p in pairs),
                             rhs_transposed=tuple(p[1].transposed for p in pairs),
                             n_extra=len(extras), epilogue=epilogue,
                             side_body=None if side is None else side.body,
                             n_side_in=0 if side is None else len(side.in_specs),
                             row_splits=row_splits)
    outs = pl.pallas_call(
        kern,
        grid=grid,
        in_specs=in_specs,
        out_specs=out_specs,
        out_shape=out_shape,
        compiler_params=_compiler_params(2),
        name=name,
    )(*operands)
    return outs[0] if side is None else tuple(outs)


def _gating_kernel(za_ref, ng_ref, w_ref, bt_ref, o_ref):
    row = lax.broadcasted_iota(jnp.int32, (CHUNK, CHUNK), 0)
    col = lax.broadcasted_iota(jnp.int32, (CHUNK, CHUNK), 1)
    causal = col <= row
    for c in range(za_ref.shape[0] // CHUNK):
        rows = slice(c * CHUNK, (c + 1) * CHUNK)
        z = jax.nn.gelu(za_ref[rows, :])
        u = z[:, :A_WIDTH]
        v = z[:, A_WIDTH:]
        ms = jnp.mean(v * v, axis=-1, keepdims=True)
        vn = ((v * lax.rsqrt(ms + EPS)) * ng_ref[...]).astype(_BF16)
        for g in range(A_GROUPS):
            sl = slice(g * A_GROUP_DIM, (g + 1) * A_GROUP_DIM)
            w = jnp.where(causal, w_ref[g], 0.0).astype(_BF16)
            s = jnp.dot(w, vn[:, sl], preferred_element_type=_F32) + bt_ref[:, g:g + 1]
            o_ref[rows, sl] = (u[:, sl] * s).astype(o_ref.dtype)


def _spatial_gating(za, norm_g, w_s, b_s, chunks_per_step=2):
    m = za.shape[0]
    tm = chunks_per_step * CHUNK
    return pl.pallas_call(
        _gating_kernel,
        grid=(m // tm,),
        in_specs=[pl.BlockSpec((tm, 2 * A_WIDTH), lambda i: (i, 0)),
                  pl.BlockSpec((1, A_WIDTH), lambda i: (0, 0)),
                  pl.BlockSpec((A_GROUPS, CHUNK, CHUNK), lambda i: (0, 0, 0)),
                  pl.BlockSpec((CHUNK, A_GROUPS), lambda i: (0, 0))],
        out_specs=pl.BlockSpec((tm, A_WIDTH), lambda i: (i, 0)),
        out_shape=jax.ShapeDtypeStruct((m, A_WIDTH), _BF16),
        compiler_params=_compiler_params(1),
        name="spatial_gating",
    )(za, norm_g.reshape(1, A_WIDTH), w_s, b_s.T)


def _rope_tables(positions, rot_dim, period):
    half = rot_dim // 2
    inv_freq = ROPE_THETA ** (-jnp.arange(0, rot_dim, 2, dtype=_F32) / rot_dim)
    ang = positions.astype(_F32)[..., None] * inv_freq
    cos, sin = jnp.cos(ang), jnp.sin(ang)
    zeros_half = jnp.zeros_like(sin)
    rest = jnp.zeros(sin.shape[:-1] + (period - rot_dim,), _F32)

    def lanes(*pieces):
        return jnp.tile(jnp.concatenate(pieces, axis=-1), LANES // period)

    return (lanes(cos, cos, rest + 1.0), lanes(zeros_half, sin, rest),
            lanes(-sin, zeros_half, rest))


def _rope(x, c, s1, s2, half):
    return x * c + pltpu.roll(x, half, 1) * s1 + pltpu.roll(x, LANES - half, 1) * s2


def _prep_kernel(att_ref, cb_ref, s1b_ref, s2b_ref, ci_ref, s1i_ref, s2i_ref,
                 q_ref, k_ref, vt_ref, qi_ref, ki_ref, wi_ref):
    cb, s1b, s2b = cb_ref[0], s1b_ref[0], s2b_ref[0]
    ci, s1i, s2i = ci_ref[0], s1i_ref[0], s2i_ref[0]
    for h in range(B_HEADS):
        x = att_ref[0, :, ATT_Q0 + h * LANES:ATT_Q0 + (h + 1) * LANES]
        q_ref[0, :, h * LANES:(h + 1) * LANES] = _rope(x, cb, s1b, s2b, B_ROT // 2).astype(_BF16)
    for h in range(B_KV_HEADS):
        x = att_ref[0, :, ATT_K0 + h * LANES:ATT_K0 + (h + 1) * LANES]
        k_ref[0, :, h * LANES:(h + 1) * LANES] = _rope(x, cb, s1b, s2b, B_ROT // 2).astype(_BF16)
    vt_ref[0, 0] = att_ref[0, :, ATT_V0:ATT_V0 + B_KV_WIDTH].T.astype(_BF16)
    lane = lax.broadcasted_iota(jnp.int32, ci.shape, 1)
    is_ki = lane < IDX_DIM
    heads_per_group = LANES // IDX_DIM
    for j in range(IDX_WIDTH // LANES):
        x = att_ref[0, :, ATT_QI0 + j * LANES:ATT_QI0 + (j + 1) * LANES]
        y = _rope(x, ci, s1i, s2i, IDX_ROT // 2)
        for r in range(heads_per_group):
            h = j * heads_per_group + r
            yr = y if r == 0 else pltpu.roll(y, LANES - r * IDX_DIM, 1)
            qi_ref[0, :, h * LANES:(h + 1) * LANES] = jnp.where(is_ki, yr, 0.0).astype(_BF16)
    x = att_ref[0, :, ATT_KW0:ATT_KW0 + LANES]
    y = _rope(x, jnp.where(is_ki, ci, 1.0), jnp.where(is_ki, s1i, 0.0),
              jnp.where(is_ki, s2i, 0.0), IDX_ROT // 2)
    ki_ref[0] = jnp.where(is_ki, y, 0.0).astype(_BF16)
    wi_ref[0] = y[:, IDX_DIM:IDX_DIM + IDX_HEADS]


def _prep_side_job(att, tables_b, tables_i, n_steps):
    bsz, seq, _ = att.shape
    ts = bsz * seq // n_steps
    assert ts * n_steps == bsz * seq and seq % ts == 0 and KEY_CHUNK % ts == 0
    assert ts % (2 * SUBLANES) == 0
    per_seq = seq // ts
    per_chunk = KEY_CHUNK // ts

    def tokens(width):
        return pl.BlockSpec((1, ts, width), lambda s: (s // per_seq, s % per_seq, 0))

    def out(width, dtype):
        return tokens(width), jax.ShapeDtypeStruct((bsz, seq, width), dtype)

    vt_out = (pl.BlockSpec((1, 1, B_KV_WIDTH, ts),
                           lambda s: (s // per_seq, (s % per_seq) // per_chunk, 0,
                                      (s % per_seq) % per_chunk)),
              jax.ShapeDtypeStruct((bsz, seq // KEY_CHUNK, B_KV_WIDTH, KEY_CHUNK), _BF16))
    outs = [out(B_WIDTH, _BF16), out(B_KV_WIDTH, _BF16), vt_out,
            out(IDX_HEADS * LANES, _BF16), out(LANES, _BF16), out(IDX_HEADS, _F32)]
    return SideJob(_prep_kernel, (att, *tables_b, *tables_i),
                   (tokens(ATT_WIDTH),) + (tokens(LANES),) * 6,
                   tuple(o[0] for o in outs), tuple(o[1] for o in outs))


def _ordered_int_to_float(key):
    bits = jnp.where(key < 0, key ^ jnp.int32(0x7FFFFFFF), key)
    return lax.bitcast_convert_type(bits, _F32)


def _fold_rows(x, op):
    while x.shape[0] > SUBLANES:
        half = x.shape[0] // 2
        x = op(x[:half], x[half:])
    return x


def _dsa_kernel(*refs, n_sel):
    seq = refs[4].shape[1]
    n_chunks = (pl.program_id(1) * Q_BLOCK + Q_BLOCK + KEY_CHUNK - 1) // KEY_CHUNK
    for n in range(1, seq // KEY_CHUNK + 1):
        pl.when(n_chunks == n)(functools.partial(_dsa_block, *refs, n_sel=n_sel, n_chunks=n))


def _dsa_block(qi_ref, wit_ref, ki_ref, q_ref, k_ref, vt_ref, o_ref,
               score_ref, bias_ref, logit_ref, acc_ref, *, n_sel, n_chunks):
    t = Q_BLOCK
    blk = pl.program_id(1)
    qpos = blk * t + lax.broadcasted_iota(jnp.int32, (1, t), 1)
    kiota = lax.broadcasted_iota(jnp.int32, (KEY_CHUNK, 1), 0)
    idx_scale = (IDX_DIM ** -0.5) * (IDX_HEADS ** -0.5)

    def rows(c):
        return slice(c * KEY_CHUNK, (c + 1) * KEY_CHUNK)

    def for_chunks(body, carry):
        for c in range(n_chunks):
            carry = body(c, carry)
        return carry

    def indexer_chunk(c, carry):
        ki = ki_ref[0, rows(c), :]
        acc = jnp.zeros((KEY_CHUNK, t), _F32)
        for h in range(0, IDX_HEADS, 2):
            qpair = jnp.concatenate([qi_ref[0, :, h * LANES:(h + 1) * LANES],
                                     qi_ref[0, :, (h + 1) * LANES:(h + 2) * LANES]], axis=0)
            d = lax.dot_general(ki, qpair, _NT_DIMS, preferred_element_type=_F32)
            acc = acc + wit_ref[0, h:h + 1, :] * jnp.maximum(d[:, :t], 0.0)
            acc = acc + wit_ref[0, h + 1:h + 2, :] * jnp.maximum(d[:, t:], 0.0)
        kpos = c * KEY_CHUNK + kiota
        score_ref[rows(c), :] = jnp.where(kpos <= qpos, acc * idx_scale + 0.0, _NEG_INF)
        return carry

    for_chunks(indexer_chunk, 0)

    def count(indicator):
        def chunk(c, part):
            return part + _fold_rows(indicator(score_ref[rows(c), :], c), jnp.add)
        part = for_chunks(chunk, jnp.zeros((SUBLANES, t), _F32))
        return jnp.sum(part, axis=0, keepdims=True)

    def count_ge(cand_f):
        return count(lambda s, c: jnp.where(s >= cand_f, 1.0, 0.0))

    int_min = jnp.int32(-2 ** 31)
    base = jnp.where(count_ge(jnp.zeros((1, t), _F32)) >= n_sel, jnp.int32(0), int_min)

    def search_step(it, base):
        cand = base | lax.shift_left(jnp.int32(1), jnp.int32(30) - it)
        cnt = count_ge(_ordered_int_to_float(cand))
        return jnp.where(cnt >= n_sel, cand, base)

    base = lax.fori_loop(0, 31, search_step, base)
    thr = jnp.where(qpos + 1 < n_sel, _NEG_INF, _ordered_int_to_float(base))

    def bias_chunk(c, part):
        s = score_ref[rows(c), :]
        sel = jnp.where(c * KEY_CHUNK + kiota <= qpos, jnp.where(s >= thr, 1.0, 0.0), 0.0)
        bias_ref[rows(c), :] = jnp.where(sel > 0.0, 0.0, _NEG_INF)
        return part + _fold_rows(sel, jnp.add)

    n_ge = jnp.sum(for_chunks(bias_chunk, jnp.zeros((SUBLANES, t), _F32)), axis=0, keepdims=True)

    @pl.when(jnp.max(n_ge) > n_sel)
    def _():
        nxt = _ordered_int_to_float(base + 1)
        tied = n_ge > n_sel
        n_above = count(lambda s, c: jnp.where(s >= nxt, 1.0, 0.0))
        need = jnp.where(tied, n_sel - n_above, 0.0)
        no_index = float(k_ref.shape[1])
        front_v = jnp.where(tied, jnp.inf, _NEG_INF)
        front_i = jnp.where(tied, -1.0, no_index)

        def beyond(s, kposf, front_v, front_i):
            after = jnp.where(s < front_v, 1.0,
                              jnp.where(s == front_v, jnp.where(kposf > front_i, 1.0, 0.0), 0.0))
            return jnp.where(s >= thr, jnp.where(s < nxt, after, 0.0), 0.0)

        def kposf(c):
            return (c * KEY_CHUNK + kiota).astype(_F32)

        def advance(_, carry):
            front_v, front_i, need = carry

            def best_value(c, part):
                s = score_ref[rows(c), :]
                cand = jnp.where(beyond(s, kposf(c), front_v, front_i) > 0.0, s, _NEG_INF)
                return jnp.maximum(part, _fold_rows(cand, jnp.maximum))

            v = jnp.max(for_chunks(best_value, jnp.full((SUBLANES, t), _NEG_INF, _F32)),
                        axis=0, keepdims=True)

            def first_index(c, part):
                s = score_ref[rows(c), :]
                hit = jnp.where(s == v, beyond(s, kposf(c), front_v, front_i), 0.0)
                cand = jnp.where(hit > 0.0, kposf(c), no_index)
                return jnp.minimum(part, _fold_rows(cand, jnp.minimum))

            i = jnp.min(for_chunks(first_index, jnp.full((SUBLANES, t), no_index, _F32)),
                        axis=0, keepdims=True)
            active = need > 0.0
            return (jnp.where(active, v, front_v), jnp.where(active, i, front_i),
                    jnp.where(active, need - 1.0, need))

        front_v, front_i, _ = lax.fori_loop(0, jnp.max(need).astype(jnp.int32), advance,
                                            (front_v, front_i, need))

        def tie_chunk(c, carry):
            s = score_ref[rows(c), :]
            upto = jnp.where(s > front_v, 0.0,
                             jnp.where(s == front_v,
                                       jnp.where(kposf(c) <= front_i, 0.0, _NEG_INF), _NEG_INF))
            keep = jnp.where(s >= nxt, 0.0, jnp.where(s >= thr, upto, _NEG_INF))
            bias_ref[rows(c), :] = jnp.where(c * KEY_CHUNK + kiota <= qpos, keep, _NEG_INF)
            return carry

        for_chunks(tie_chunk, 0)

    grp = B_HEADS // B_KV_HEADS
    gw = grp * t
    exp2_scale = (B_HEAD_DIM ** -0.5) * 1.4426950408889634

    def logit_chunk(c, mx):
        bias = jnp.concatenate([bias_ref[rows(c), :]] * grp, axis=1)
        folded = []
        for g in range(B_KV_HEADS):
            qg = jnp.concatenate(
                [q_ref[0, :, (g * grp + hh) * B_HEAD_DIM:(g * grp + hh + 1) * B_HEAD_DIM]
                 for hh in range(grp)], axis=0)
            l = lax.dot_general(k_ref[0, rows(c), g * B_HEAD_DIM:(g + 1) * B_HEAD_DIM], qg,
                                _NT_DIMS, preferred_element_type=_F32) + bias
            logit_ref[rows(c), g * gw:(g + 1) * gw] = l
            folded.append(_fold_rows(l, jnp.maximum))
        return jnp.maximum(mx, jnp.concatenate(folded, axis=1))

    mx = for_chunks(logit_chunk, jnp.full((SUBLANES, B_HEADS * t), _NEG_INF, _F32))
    mx = jnp.max(mx, axis=0, keepdims=True)

    acc_ref[...] = jnp.zeros_like(acc_ref)
    ones_rows = jnp.ones((acc_ref.shape[0] - B_HEAD_DIM, KEY_CHUNK), _BF16)

    def pv_chunk(c, carry):
        for g in range(B_KV_HEADS):
            cols = slice(g * gw, (g + 1) * gw)
            p = jnp.exp2((logit_ref[rows(c), cols] - mx[:, cols]) * exp2_scale)
            vt_ext = jnp.concatenate(
                [vt_ref[0, c, g * B_HEAD_DIM:(g + 1) * B_HEAD_DIM, :], ones_rows], axis=0)
            acc_ref[:, cols] += jnp.dot(vt_ext, p.astype(_BF16), preferred_element_type=_F32)
        return carry

    for_chunks(pv_chunk, 0)
    out_t = acc_ref[:B_HEAD_DIM, :] / acc_ref[B_HEAD_DIM:B_HEAD_DIM + 1, :]
    for h in range(B_HEADS):
        o_ref[0, :, h * B_HEAD_DIM:(h + 1) * B_HEAD_DIM] = (
            out_t[:, h * t:(h + 1) * t].T.astype(o_ref.dtype))


def _dsa_attention(q, k, vt, qi, ki, wit):
    bsz, seq, _ = q.shape
    assert Q_BLOCK == LANES and seq % KEY_CHUNK == 0
    n_sel = min(TOPK_MAX, seq // 4)

    def q_spec(width):
        return pl.BlockSpec((1, Q_BLOCK, width), lambda b, i: (b, i, 0))

    def kv_spec(width):
        return pl.BlockSpec((1, seq, width), lambda b, i: (b, 0, 0))

    return pl.pallas_call(
        functools.partial(_dsa_kernel, n_sel=n_sel),
        grid=(bsz, seq // Q_BLOCK),
        in_specs=[q_spec(IDX_HEADS * LANES),
                  pl.BlockSpec((1, IDX_HEADS, Q_BLOCK), lambda b, i: (b, 0, i)),
                  kv_spec(LANES), q_spec(B_WIDTH), kv_spec(B_KV_WIDTH),
                  pl.BlockSpec((1, seq // KEY_CHUNK, B_KV_WIDTH, KEY_CHUNK),
                               lambda b, i: (b, 0, 0, 0))],
        out_specs=q_spec(B_WIDTH),
        out_shape=jax.ShapeDtypeStruct((bsz, seq, B_WIDTH), _BF16),
        scratch_shapes=[pltpu.VMEM((seq, Q_BLOCK), _F32), pltpu.VMEM((seq, Q_BLOCK), _F32),
                        pltpu.VMEM((seq, B_HEADS * Q_BLOCK), _F32),
                        pltpu.VMEM((B_HEAD_DIM + 2 * SUBLANES, B_HEADS * Q_BLOCK), _F32)],
        compiler_params=_compiler_params(2),
        name="dsa_attention",
    )(qi, wit, ki, q, k, vt)


def _xattn_kernel(q_ref, k_ref, v_ref, o_ref):
    scale = X_HEAD_DIM ** -0.5
    for h in range(X_HEADS):
        hs = slice(h * X_HEAD_DIM, (h + 1) * X_HEAD_DIM)
        logits = lax.dot_general(q_ref[0, :, hs], k_ref[0, :, hs], _NT_DIMS,
                                 preferred_element_type=_F32) * scale
        mx = jnp.max(logits, axis=-1, keepdims=True)
        p = jnp.exp(logits - mx)
        denom = jnp.sum(p, axis=-1, keepdims=True)
        o = jnp.dot(p.astype(_BF16), v_ref[0, :, hs], preferred_element_type=_F32)
        o_ref[0, :, hs] = (o / denom).astype(o_ref.dtype)


def _cross_attention(q, k, v, tq=512):
    bsz, seq, _ = q.shape
    m = k.shape[1]
    mem_spec = pl.BlockSpec((1, m, X_WIDTH), lambda b, i: (b, 0, 0))
    return pl.pallas_call(
        _xattn_kernel,
        grid=(bsz, seq // tq),
        in_specs=[pl.BlockSpec((1, tq, X_WIDTH), lambda b, i: (b, i, 0)), mem_spec, mem_spec],
        out_specs=pl.BlockSpec((1, tq, X_WIDTH), lambda b, i: (b, i, 0)),
        out_shape=jax.ShapeDtypeStruct((bsz, seq, X_WIDTH), _BF16),
        compiler_params=_compiler_params(2),
        name="cross_attention",
    )(q, k, v)


def _norm_proj_kernel(x_ref, g_ref, w_ref, o_ref, w_bf16_ref):
    @pl.when(pl.program_id(0) == 0)
    def _():
        w_bf16_ref[...] = w_ref[...].astype(_BF16)

    x = x_ref[...]
    ms = jnp.mean(x * x, axis=-1, keepdims=True)
    h = ((x * lax.rsqrt(ms + EPS)) * g_ref[...]).astype(_BF16)
    o_ref[...] = jnp.dot(h, w_bf16_ref[...], preferred_element_type=_F32).astype(o_ref.dtype)


def _norm_proj(x, g, w_stack, layer, out_dtype, name, tm=512):
    m, d = x.shape
    n = w_stack.shape[2]
    return pl.pallas_call(
        _norm_proj_kernel,
        grid=(m // tm,),
        in_specs=[pl.BlockSpec((tm, d), lambda i: (i, 0)),
                  pl.BlockSpec((1, d), lambda i: (0, 0)),
                  pl.BlockSpec((None, d, n), lambda i: (layer, 0, 0),
                               pipeline_mode=pl.Buffered(1))],
        out_specs=pl.BlockSpec((tm, n), lambda i: (i, 0)),
        out_shape=jax.ShapeDtypeStruct((m, n), out_dtype),
        scratch_shapes=[pltpu.VMEM((d, n), _BF16)],
        compiler_params=_compiler_params(1),
        name=name,
    )(x, g.reshape(1, d), w_stack)


def _xattn_out_kernel(o_ref, w_ref, x_ref, g_ref, x_out_ref, h_out_ref, w_bf16_ref):
    @pl.when(pl.program_id(0) == 0)
    def _():
        w_bf16_ref[...] = w_ref[...].astype(_BF16)

    x = x_ref[...] + jnp.dot(o_ref[...], w_bf16_ref[...], preferred_element_type=_F32)
    x_out_ref[...] = x
    ms = jnp.mean(x * x, axis=-1, keepdims=True)
    h_out_ref[...] = ((x * lax.rsqrt(ms + EPS)) * g_ref[...]).astype(h_out_ref.dtype)


def _xattn_out_and_norm(o, w_stack, layer, x, g, tm=256):
    m, k = o.shape
    d = x.shape[1]
    row_spec = pl.BlockSpec((tm, d), lambda i: (i, 0))
    return pl.pallas_call(
        _xattn_out_kernel,
        grid=(m // tm,),
        in_specs=[pl.BlockSpec((tm, k), lambda i: (i, 0)),
                  pl.BlockSpec((None, k, d), lambda i: (layer, 0, 0),
                               pipeline_mode=pl.Buffered(1)),
                  row_spec,
                  pl.BlockSpec((1, d), lambda i: (0, 0))],
        out_specs=[row_spec, row_spec],
        out_shape=[jax.ShapeDtypeStruct((m, d), _F32), jax.ShapeDtypeStruct((m, d), _BF16)],
        scratch_shapes=[pltpu.VMEM((k, d), _BF16)],
        compiler_params=_compiler_params(1),
        name="xattn_out_norm",
    )(o, w_stack, x, g.reshape(1, d))


def kernel(x, mem, positions, norm_mix_g, w_in, a_norm_g, a_spatial_w, a_spatial_b, p_a, p_b,
           w_out, norm_x_g, norm_mem_g, xq_w, xk_w, xv_w, xo_w, norm_ffn_g, ffn_w1, ffn_w3,
           ffn_w2, final_norm_g):
    bsz, seq, d = x.shape
    m = bsz * seq
    depth = w_in.shape[0]
    ffn_hidden = ffn_w1.shape[-1]
    tables_b = _rope_tables(positions, B_ROT, B_HEAD_DIM)
    tables_i = _rope_tables(positions, IDX_ROT, IDX_DIM)

    xf = x.reshape(m, d)
    mem_f = mem.reshape(bsz * MEM_LEN, d)
    w_in_t = jnp.transpose(w_in, (0, 2, 1))
    for l in range(depth):
        def w_in_cols(col0):
            return Rhs(w_in_t, l, col0, transposed=True)

        h = _rmsnorm(xf, norm_mix_g[l], _BF16)
        att = _fused_matmul([h], [(0, w_in_cols(ZA_END))], [], _ep_identity,
                            ATT_WIDTH, _F32, 1024, PROJ_TN, "proj_att", prefetch_lhs=True)
        za, q, k, vt, qi, ki, wi = _fused_matmul(
            [h], [(0, w_in_cols(0))], [], _ep_identity, ZA_END, _F32, 1024, 512, "proj_za",
            side_job=functools.partial(_prep_side_job, att.reshape(bsz, seq, ATT_WIDTH),
                                       tables_b, tables_i), prefetch_lhs=True)
        y_a = _spatial_gating(za, a_norm_g[l], a_spatial_w[l], a_spatial_b[l])
        y_b = _dsa_attention(q, k, vt, qi, ki, wi.transpose(0, 2, 1)).reshape(m, B_WIDTH)
        merged = _fused_matmul(
            [y_a, y_b, h],
            [(0, Rhs(p_a, l)), (1, Rhs(p_b, l)), (2, w_in_cols(ATT_END)),
             (2, w_in_cols(ATT_END + d))],
            [], _ep_gated_merge, d, _BF16, 1024, 256, "gated_merge", row_splits=2)
        x1 = _fused_matmul([merged], [(0, Rhs(w_out, l))], [xf], _ep_residual,
                           d, _F32, 1024, 512, "out_proj", prefetch_lhs=True)

        qx = _norm_proj(x1, norm_x_g[l], xq_w, l, _BF16, "xattn_q")
        kx = _norm_proj(mem_f, norm_mem_g[l], xk_w, l, _BF16, "xattn_k")
        vx = _norm_proj(mem_f, norm_mem_g[l], xv_w, l, _BF16, "xattn_v")
        ox = _cross_attention(qx.reshape(bsz, seq, X_WIDTH),
                              kx.reshape(bsz, MEM_LEN, X_WIDTH),
                              vx.reshape(bsz, MEM_LEN, X_WIDTH)).reshape(m, X_WIDTH)
        x2, h2 = _xattn_out_and_norm(ox, xo_w, l, x1, norm_ffn_g[l])

        act, w2_bf16 = _fused_matmul([h2], [(0, Rhs(ffn_w1, l)), (0, Rhs(ffn_w3, l))],
                                     [], _ep_swiglu, ffn_hidden, _BF16, 1024, 256, "ffn_up",
                                     side_job=functools.partial(_cast_side_job, ffn_w2, l),
                                     prefetch_lhs=True)
        xf = _fused_matmul([act], [(0, Rhs(w2_bf16))], [x2], _ep_residual,
                           d, _F32, 512, 512, "ffn_down", prefetch_lhs=True)

    return _rmsnorm(xf, final_norm_g, _F32).reshape(bsz, seq, d)
```

```python
import functools
from typing import Callable, NamedTuple, Optional

import jax
import jax.numpy as jnp
from jax import lax
from jax.experimental import pallas as pl
from jax.experimental.pallas import tpu as pltpu

D_MODEL = 4096
MEM_LEN = 256
EPS = 1e-6
ROPE_THETA = 500000.0
CHUNK = 128
A_GROUPS = 16
A_WIDTH = D_MODEL // 2
A_GROUP_DIM = A_WIDTH // A_GROUPS
B_HEADS = 16
B_HEAD_DIM = 128
B_KV_HEADS = 4
B_WIDTH = B_HEADS * B_HEAD_DIM
B_KV_WIDTH = B_KV_HEADS * B_HEAD_DIM
B_ROT = B_HEAD_DIM // 4
IDX_HEADS = 16
IDX_DIM = 64
IDX_WIDTH = IDX_HEADS * IDX_DIM
IDX_ROT = IDX_DIM // 4
TOPK_MAX = 256
Q_BLOCK = 128
X_HEADS = 4
X_HEAD_DIM = 256
X_WIDTH = X_HEADS * X_HEAD_DIM
IN_SIZES = (2 * A_WIDTH, B_WIDTH, B_KV_WIDTH, B_KV_WIDTH, IDX_WIDTH, IDX_DIM, IDX_HEADS,
            2 * D_MODEL)

LANES = 128
SUBLANES = 8
KEY_CHUNK = 512
PROJ_TN = 512
ZA_END = IN_SIZES[0]
ATT_END = ZA_END + sum(IN_SIZES[1:7])
ATT_Q0 = 0
ATT_K0 = ATT_Q0 + B_WIDTH
ATT_V0 = ATT_K0 + B_KV_WIDTH
ATT_QI0 = ATT_V0 + B_KV_WIDTH
ATT_KW0 = ATT_QI0 + IDX_WIDTH
ATT_WIDTH = -(-(ATT_END - ZA_END) // PROJ_TN) * PROJ_TN
assert ZA_END % PROJ_TN == 0 and ATT_KW0 + LANES <= ATT_WIDTH

VMEM_LIMIT_BYTES = 60 * 1024 * 1024

_BF16 = jnp.bfloat16
_F32 = jnp.float32
_NEG_INF = float("-inf")
_NN_DIMS = (((1,), (0,)), ((), ()))
_NT_DIMS = (((1,), (1,)), ((), ()))


def _compiler_params(n_axes):
    return pltpu.CompilerParams(
        dimension_semantics=("arbitrary",) * n_axes,
        vmem_limit_bytes=VMEM_LIMIT_BYTES,
    )


def _rmsnorm_kernel(x_ref, g_ref, o_ref):
    x = x_ref[...]
    ms = jnp.mean(x * x, axis=-1, keepdims=True)
    y = x * lax.rsqrt(ms + EPS)
    o_ref[...] = (y * g_ref[...]).astype(o_ref.dtype)


def _rmsnorm(x, g, out_dtype, tm=512):
    m, d = x.shape
    return pl.pallas_call(
        _rmsnorm_kernel,
        grid=(m // tm,),
        in_specs=[pl.BlockSpec((tm, d), lambda i: (i, 0)),
                  pl.BlockSpec((1, d), lambda i: (0, 0))],
        out_specs=pl.BlockSpec((tm, d), lambda i: (i, 0)),
        out_shape=jax.ShapeDtypeStruct((m, d), out_dtype),
        compiler_params=_compiler_params(1),
        name="rmsnorm",
    )(x, g.reshape(1, d))


def _ep_identity(accs, extras):
    return accs[0]


def _ep_residual(accs, extras):
    return extras[0] + accs[0]


def _ep_gated_merge(accs, extras):
    y_a, y_b, gate_a, gate_b = accs
    return jax.nn.sigmoid(gate_a) * y_a + jax.nn.sigmoid(gate_b) * y_b


def _ep_swiglu(accs, extras):
    return jax.nn.silu(accs[0]) * accs[1]


class Rhs(NamedTuple):
    array: jax.Array
    layer: Optional[int] = None
    col0: int = 0
    transposed: bool = False


def _rhs_spec(r, tn):
    if r.transposed:
        k = r.array.shape[2]
        if r.col0 % tn == 0:
            return pl.BlockSpec((None, tn, k), lambda i, j: (r.layer, j + r.col0 // tn, 0))
        assert r.col0 % SUBLANES == 0
        return pl.BlockSpec((pl.Element(1), pl.Element(tn), pl.Element(k)),
                            lambda i, j: (r.layer, (r.col0 // SUBLANES + j * (tn // SUBLANES))
                                          * SUBLANES, 0))
    assert r.col0 % tn == 0
    off = r.col0 // tn
    if r.layer is None:
        return pl.BlockSpec((r.array.shape[0], tn), lambda i, j: (0, j + off))
    return pl.BlockSpec((None, r.array.shape[1], tn), lambda i, j: (r.layer, 0, j + off))


class SideJob(NamedTuple):
    body: Callable
    arrays: tuple
    in_specs: tuple
    out_specs: tuple
    out_shape: tuple


def _cast_body(x_ref, o_ref):
    o_ref[...] = x_ref[...].astype(o_ref.dtype)


def _cast_side_job(w_stack, layer, n_steps):
    _, rows, cols = w_stack.shape
    slab = rows // n_steps
    assert slab * n_steps == rows and slab % (2 * SUBLANES) == 0
    return SideJob(_cast_body, (w_stack,),
                   (pl.BlockSpec((None, slab, cols), lambda s: (layer, s, 0)),),
                   (pl.BlockSpec((slab, cols), lambda s: (s, 0)),),
                   (jax.ShapeDtypeStruct((rows, cols), _BF16),))


def _mm_kernel(*refs, n_lhs, pair_lhs, rhs_transposed, n_extra, epilogue, side_body, n_side_in,
               row_splits):
    n_rhs = len(pair_lhs)
    lhs_refs = refs[:n_lhs]
    rhs_refs = refs[n_lhs:n_lhs + n_rhs]
    n_in = n_lhs + n_rhs + n_extra
    extra_refs = refs[n_lhs + n_rhs:n_in]
    o_ref = refs[n_in + n_side_in]
    if side_body is not None:
        side_body(*refs[n_in:n_in + n_side_in], *refs[n_in + n_side_in + 1:])
    weights = [(r[0] if len(r.shape) == 3 else r[...]).astype(_BF16) for r in rhs_refs]
    rows_per_split = o_ref.shape[0] // row_splits
    for s in range(row_splits):
        rows = slice(s * rows_per_split, (s + 1) * rows_per_split)
        accs = [lax.dot_general(lhs_refs[li][rows, :], w, _NT_DIMS if tr else _NN_DIMS,
                                preferred_element_type=_F32)
                for li, w, tr in zip(pair_lhs, weights, rhs_transposed)]
        o_ref[rows, :] = epilogue(accs, [e[rows, :] for e in extra_refs]).astype(o_ref.dtype)


def _fused_matmul(lhs, pairs, extras, epilogue, n_out, out_dtype, tm, tn, name,
                  side_job=None, row_splits=1, prefetch_lhs=False):
    m = lhs[0].shape[0]
    grid = (m // tm, n_out // tn)
    lhs_mode = {} if prefetch_lhs else {"pipeline_mode": pl.Buffered(1)}
    in_specs = [pl.BlockSpec((tm, a.shape[1]), lambda i, j: (i, 0), **lhs_mode) for a in lhs]
    in_specs += [_rhs_spec(r, tn) for _, r in pairs]
    in_specs += [pl.BlockSpec((tm, tn), lambda i, j: (i, j)) for _ in extras]
    out_specs = [pl.BlockSpec((tm, tn), lambda i, j: (i, j))]
    out_shape = [jax.ShapeDtypeStruct((m, n_out), out_dtype)]
    operands = list(lhs) + [p[1].array for p in pairs] + list(extras)
    side = None if side_job is None else side_job(grid[0] * grid[1])
    if side is not None:
        def per_step(spec):
            return pl.BlockSpec(spec.block_shape,
                                lambda i, j: spec.index_map(i * grid[1] + j))
        in_specs += [per_step(s) for s in side.in_specs]
        out_specs += [per_step(s) for s in side.out_specs]
        out_shape += list(side.out_shape)
        operands += list(side.arrays)
    kern = functools.partial(_mm_kernel, n_lhs=len(lhs), pair_lhs=tuple(p[0] for p in pairs),
                             rhs_transposed=tuple(p[1].transposed for p in pairs),
                             n_extra=len(extras), epilogue=epilogue,
                             side_body=None if side is None else side.body,
                             n_side_in=0 if side is None else len(side.in_specs),
                             row_splits=row_splits)
    outs = pl.pallas_call(
        kern,
        grid=grid,
        in_specs=in_specs,
        out_specs=out_specs,
        out_shape=out_shape,
        compiler_params=_compiler_params(2),
        name=name,
    )(*operands)
    return outs[0] if side is None else tuple(outs)


def _gating_kernel(za_ref, ng_ref, w_ref, bt_ref, o_ref):
    row = lax.broadcasted_iota(jnp.int32, (CHUNK, CHUNK), 0)
    col = lax.broadcasted_iota(jnp.int32, (CHUNK, CHUNK), 1)
    causal = col <= row
    for c in range(za_ref.shape[0] // CHUNK):
        rows = slice(c * CHUNK, (c + 1) * CHUNK)
        z = jax.nn.gelu(za_ref[rows, :])
        u = z[:, :A_WIDTH]
        v = z[:, A_WIDTH:]
        ms = jnp.mean(v * v, axis=-1, keepdims=True)
        vn = ((v * lax.rsqrt(ms + EPS)) * ng_ref[...]).astype(_BF16)
        for g in range(A_GROUPS):
            sl = slice(g * A_GROUP_DIM, (g + 1) * A_GROUP_DIM)
            w = jnp.where(causal, w_ref[g], 0.0).astype(_BF16)
            s = jnp.dot(w, vn[:, sl], preferred_element_type=_F32) + bt_ref[:, g:g + 1]
            o_ref[rows, sl] = (u[:, sl] * s).astype(o_ref.dtype)


def _spatial_gating(za, norm_g, w_s, b_s, chunks_per_step=2):
    m = za.shape[0]
    tm = chunks_per_step * CHUNK
    return pl.pallas_call(
        _gating_kernel,
        grid=(m // tm,),
        in_specs=[pl.BlockSpec((tm, 2 * A_WIDTH), lambda i: (i, 0)),
                  pl.BlockSpec((1, A_WIDTH), lambda i: (0, 0)),
                  pl.BlockSpec((A_GROUPS, CHUNK, CHUNK), lambda i: (0, 0, 0)),
                  pl.BlockSpec((CHUNK, A_GROUPS), lambda i: (0, 0))],
        out_specs=pl.BlockSpec((tm, A_WIDTH), lambda i: (i, 0)),
        out_shape=jax.ShapeDtypeStruct((m, A_WIDTH), _BF16),
        compiler_params=_compiler_params(1),
        name="spatial_gating",
    )(za, norm_g.reshape(1, A_WIDTH), w_s, b_s.T)


def _rope_tables(positions, rot_dim, period):
    half = rot_dim // 2
    inv_freq = ROPE_THETA ** (-jnp.arange(0, rot_dim, 2, dtype=_F32) / rot_dim)
    ang = positions.astype(_F32)[..., None] * inv_freq
    cos, sin = jnp.cos(ang), jnp.sin(ang)
    zeros_half = jnp.zeros_like(sin)
    rest = jnp.zeros(sin.shape[:-1] + (period - rot_dim,), _F32)

    def lanes(*pieces):
        return jnp.tile(jnp.concatenate(pieces, axis=-1), LANES // period)

    return (lanes(cos, cos, rest + 1.0), lanes(zeros_half, sin, rest),
            lanes(-sin, zeros_half, rest))


def _rope(x, c, s1, s2, half):
    return x * c + pltpu.roll(x, half, 1) * s1 + pltpu.roll(x, LANES - half, 1) * s2


def _prep_kernel(att_ref, cb_ref, s1b_ref, s2b_ref, ci_ref, s1i_ref, s2i_ref,
                 q_ref, k_ref, vt_ref, qi_ref, ki_ref, wi_ref):
    cb, s1b, s2b = cb_ref[0], s1b_ref[0], s2b_ref[0]
    ci, s1i, s2i = ci_ref[0], s1i_ref[0], s2i_ref[0]
    for h in range(B_HEADS):
        x = att_ref[0, :, ATT_Q0 + h * LANES:ATT_Q0 + (h + 1) * LANES]
        q_ref[0, :, h * LANES:(h + 1) * LANES] = _rope(x, cb, s1b, s2b, B_ROT // 2).astype(_BF16)
    for h in range(B_KV_HEADS):
        x = att_ref[0, :, ATT_K0 + h * LANES:ATT_K0 + (h + 1) * LANES]
        k_ref[0, :, h * LANES:(h + 1) * LANES] = _rope(x, cb, s1b, s2b, B_ROT // 2).astype(_BF16)
    vt_ref[0, 0] = att_ref[0, :, ATT_V0:ATT_V0 + B_KV_WIDTH].T.astype(_BF16)
    lane = lax.broadcasted_iota(jnp.int32, ci.shape, 1)
    is_ki = lane < IDX_DIM
    heads_per_group = LANES // IDX_DIM
    for j in range(IDX_WIDTH // LANES):
        x = att_ref[0, :, ATT_QI0 + j * LANES:ATT_QI0 + (j + 1) * LANES]
        y = _rope(x, ci, s1i, s2i, IDX_ROT // 2)
        for r in range(heads_per_group):
            h = j * heads_per_group + r
            yr = y if r == 0 else pltpu.roll(y, LANES - r * IDX_DIM, 1)
            qi_ref[0, :, h * LANES:(h + 1) * LANES] = jnp.where(is_ki, yr, 0.0).astype(_BF16)
    x = att_ref[0, :, ATT_KW0:ATT_KW0 + LANES]
    y = _rope(x, jnp.where(is_ki, ci, 1.0), jnp.where(is_ki, s1i, 0.0),
              jnp.where(is_ki, s2i, 0.0), IDX_ROT // 2)
    ki_ref[0] = jnp.where(is_ki, y, 0.0).astype(_BF16)
    wi_ref[0] = y[:, IDX_DIM:IDX_DIM + IDX_HEADS]


def _prep_side_job(att, tables_b, tables_i, n_steps):
    bsz, seq, _ = att.shape
    ts = bsz * seq // n_steps
    assert ts * n_steps == bsz * seq and seq % ts == 0 and KEY_CHUNK % ts == 0
    assert ts % (2 * SUBLANES) == 0
    per_seq = seq // ts
    per_chunk = KEY_CHUNK // ts

    def tokens(width):
        return pl.BlockSpec((1, ts, width), lambda s: (s // per_seq, s % per_seq, 0))

    def out(width, dtype):
        return tokens(width), jax.ShapeDtypeStruct((bsz, seq, width), dtype)

    vt_out = (pl.BlockSpec((1, 1, B_KV_WIDTH, ts),
                           lambda s: (s // per_seq, (s % per_seq) // per_chunk, 0,
                                      (s % per_seq) % per_chunk)),
              jax.ShapeDtypeStruct((bsz, seq // KEY_CHUNK, B_KV_WIDTH, KEY_CHUNK), _BF16))
    outs = [out(B_WIDTH, _BF16), out(B_KV_WIDTH, _BF16), vt_out,
            out(IDX_HEADS * LANES, _BF16), out(LANES, _BF16), out(IDX_HEADS, _F32)]
    return SideJob(_prep_kernel, (att, *tables_b, *tables_i),
                   (tokens(ATT_WIDTH),) + (tokens(LANES),) * 6,
                   tuple(o[0] for o in outs), tuple(o[1] for o in outs))


def _ordered_int_to_float(key):
    bits = jnp.where(key < 0, key ^ jnp.int32(0x7FFFFFFF), key)
    return lax.bitcast_convert_type(bits, _F32)


def _fold_rows(x, op):
    while x.shape[0] > SUBLANES:
        half = x.shape[0] // 2
        x = op(x[:half], x[half:])
    return x


def _dsa_kernel(*refs, n_sel):
    seq = refs[4].shape[1]
    n_chunks = (pl.program_id(1) * Q_BLOCK + Q_BLOCK + KEY_CHUNK - 1) // KEY_CHUNK
    for n in range(1, seq // KEY_CHUNK + 1):
        pl.when(n_chunks == n)(functools.partial(_dsa_block, *refs, n_sel=n_sel, n_chunks=n))


def _dsa_block(qi_ref, wit_ref, ki_ref, q_ref, k_ref, vt_ref, o_ref,
               score_ref, bias_ref, acc_ref, *, n_sel, n_chunks):
    t = Q_BLOCK
    blk = pl.program_id(1)
    qpos = blk * t + lax.broadcasted_iota(jnp.int32, (1, t), 1)
    kiota = lax.broadcasted_iota(jnp.int32, (KEY_CHUNK, 1), 0)
    idx_scale = (IDX_DIM ** -0.5) * (IDX_HEADS ** -0.5)

    def rows(c):
        return slice(c * KEY_CHUNK, (c + 1) * KEY_CHUNK)

    def for_chunks(body, carry):
        for c in range(n_chunks):
            carry = body(c, carry)
        return carry

    def indexer_chunk(c, carry):
        ki = ki_ref[0, rows(c), :]
        acc = jnp.zeros((KEY_CHUNK, t), _F32)
        for h in range(0, IDX_HEADS, 2):
            qpair = jnp.concatenate([qi_ref[0, :, h * LANES:(h + 1) * LANES],
                                     qi_ref[0, :, (h + 1) * LANES:(h + 2) * LANES]], axis=0)
            d = lax.dot_general(ki, qpair, _NT_DIMS, preferred_element_type=_F32)
            acc = acc + wit_ref[0, h:h + 1, :] * jnp.maximum(d[:, :t], 0.0)
            acc = acc + wit_ref[0, h + 1:h + 2, :] * jnp.maximum(d[:, t:], 0.0)
        kpos = c * KEY_CHUNK + kiota
        score_ref[rows(c), :] = jnp.where(kpos <= qpos, acc * idx_scale + 0.0, _NEG_INF)
        return carry

    for_chunks(indexer_chunk, 0)

    def count(indicator):
        def chunk(c, part):
            return part + _fold_rows(indicator(score_ref[rows(c), :], c), jnp.add)
        part = for_chunks(chunk, jnp.zeros((SUBLANES, t), _F32))
        return jnp.sum(part, axis=0, keepdims=True)

    def count_ge(cand_f):
        return count(lambda s, c: jnp.where(s >= cand_f, 1.0, 0.0))

    int_min = jnp.int32(-2 ** 31)
    base = jnp.where(count_ge(jnp.zeros((1, t), _F32)) >= n_sel, jnp.int32(0), int_min)

    def search_step(it, base):
        cand = base | lax.shift_left(jnp.int32(1), jnp.int32(30) - it)
        cnt = count_ge(_ordered_int_to_float(cand))
        return jnp.where(cnt >= n_sel, cand, base)

    base = lax.fori_loop(0, 31, search_step, base)
    thr = jnp.where(qpos + 1 < n_sel, _NEG_INF, _ordered_int_to_float(base))

    def bias_chunk(c, part):
        s = score_ref[rows(c), :]
        sel = jnp.where(c * KEY_CHUNK + kiota <= qpos, jnp.where(s >= thr, 1.0, 0.0), 0.0)
        bias_ref[rows(c), :] = jnp.where(sel > 0.0, 0.0, _NEG_INF)
        return part + _fold_rows(sel, jnp.add)

    n_ge = jnp.sum(for_chunks(bias_chunk, jnp.zeros((SUBLANES, t), _F32)), axis=0, keepdims=True)

    @pl.when(jnp.max(n_ge) > n_sel)
    def _():
        nxt = _ordered_int_to_float(base + 1)
        tied = n_ge > n_sel
        n_above = count(lambda s, c: jnp.where(s >= nxt, 1.0, 0.0))
        need = jnp.where(tied, n_sel - n_above, 0.0)
        no_index = float(k_ref.shape[1])
        front_v = jnp.where(tied, jnp.inf, _NEG_INF)
        front_i = jnp.where(tied, -1.0, no_index)

        def beyond(s, kposf, front_v, front_i):
            after = jnp.where(s < front_v, 1.0,
                              jnp.where(s == front_v, jnp.where(kposf > front_i, 1.0, 0.0), 0.0))
            return jnp.where(s >= thr, jnp.where(s < nxt, after, 0.0), 0.0)

        def kposf(c):
            return (c * KEY_CHUNK + kiota).astype(_F32)

        def advance(_, carry):
            front_v, front_i, need = carry

            def best_value(c, part):
                s = score_ref[rows(c), :]
                cand = jnp.where(beyond(s, kposf(c), front_v, front_i) > 0.0, s, _NEG_INF)
                return jnp.maximum(part, _fold_rows(cand, jnp.maximum))

            v = jnp.max(for_chunks(best_value, jnp.full((SUBLANES, t), _NEG_INF, _F32)),
                        axis=0, keepdims=True)

            def first_index(c, part):
                s = score_ref[rows(c), :]
                hit = jnp.where(s == v, beyond(s, kposf(c), front_v, front_i), 0.0)
                cand = jnp.where(hit > 0.0, kposf(c), no_index)
                return jnp.minimum(part, _fold_rows(cand, jnp.minimum))

            i = jnp.min(for_chunks(first_index, jnp.full((SUBLANES, t), no_index, _F32)),
                        axis=0, keepdims=True)
            active = need > 0.0
            return (jnp.where(active, v, front_v), jnp.where(active, i, front_i),
                    jnp.where(active, need - 1.0, need))

        front_v, front_i, _ = lax.fori_loop(0, jnp.max(need).astype(jnp.int32), advance,
                                            (front_v, front_i, need))

        def tie_chunk(c, carry):
            s = score_ref[rows(c), :]
            upto = jnp.where(s > front_v, 0.0,
                             jnp.where(s == front_v,
                                       jnp.where(kposf(c) <= front_i, 0.0, _NEG_INF), _NEG_INF))
            keep = jnp.where(s >= nxt, 0.0, jnp.where(s >= thr, upto, _NEG_INF))
            bias_ref[rows(c), :] = jnp.where(c * KEY_CHUNK + kiota <= qpos, keep, _NEG_INF)
            return carry

        for_chunks(tie_chunk, 0)

    grp = B_HEADS // B_KV_HEADS
    gw = grp * t
    exp2_scale = (B_HEAD_DIM ** -0.5) * 1.4426950408889634
    ones_rows = jnp.ones((acc_ref.shape[0] - B_HEAD_DIM, KEY_CHUNK), _BF16)
    run_max = [None] * B_KV_HEADS
    for c in range(n_chunks):
        bias = jnp.concatenate([bias_ref[rows(c), :]] * grp, axis=1)
        for g in range(B_KV_HEADS):
            cols = slice(g * gw, (g + 1) * gw)
            qg = jnp.concatenate(
                [q_ref[0, :, (g * grp + hh) * B_HEAD_DIM:(g * grp + hh + 1) * B_HEAD_DIM]
                 for hh in range(grp)], axis=0)
            l = lax.dot_general(k_ref[0, rows(c), g * B_HEAD_DIM:(g + 1) * B_HEAD_DIM], qg,
                                _NT_DIMS, preferred_element_type=_F32) + bias
            chunk_max = jnp.max(_fold_rows(l, jnp.maximum), axis=0, keepdims=True)
            new_max = chunk_max if c == 0 else jnp.maximum(run_max[g], chunk_max)
            shift = jnp.where(new_max == _NEG_INF, 0.0, new_max)
            p = jnp.exp2((l - shift) * exp2_scale)
            vt_ext = jnp.concatenate(
                [vt_ref[0, c, g * B_HEAD_DIM:(g + 1) * B_HEAD_DIM, :], ones_rows], axis=0)
            pv = jnp.dot(vt_ext, p.astype(_BF16), preferred_element_type=_F32)
            if c == 0:
                acc_ref[:, cols] = pv
            else:
                rescale = jnp.where(run_max[g] == _NEG_INF, 0.0,
                                    jnp.exp2((run_max[g] - shift) * exp2_scale))
                acc_ref[:, cols] = acc_ref[:, cols] * rescale + pv
            run_max[g] = new_max
    out_t = acc_ref[:B_HEAD_DIM, :] / acc_ref[B_HEAD_DIM:B_HEAD_DIM + 1, :]
    for h in range(B_HEADS):
        o_ref[0, :, h * B_HEAD_DIM:(h + 1) * B_HEAD_DIM] = (
            out_t[:, h * t:(h + 1) * t].T.astype(o_ref.dtype))


def _dsa_attention(q, k, vt, qi, ki, wit):
    bsz, seq, _ = q.shape
    assert Q_BLOCK == LANES and seq % KEY_CHUNK == 0
    n_sel = min(TOPK_MAX, seq // 4)

    def q_spec(width):
        return pl.BlockSpec((1, Q_BLOCK, width), lambda b, i: (b, i, 0))

    def kv_spec(width):
        return pl.BlockSpec((1, seq, width), lambda b, i: (b, 0, 0))

    return pl.pallas_call(
        functools.partial(_dsa_kernel, n_sel=n_sel),
        grid=(bsz, seq // Q_BLOCK),
        in_specs=[q_spec(IDX_HEADS * LANES),
                  pl.BlockSpec((1, IDX_HEADS, Q_BLOCK), lambda b, i: (b, 0, i)),
                  kv_spec(LANES), q_spec(B_WIDTH), kv_spec(B_KV_WIDTH),
                  pl.BlockSpec((1, seq // KEY_CHUNK, B_KV_WIDTH, KEY_CHUNK),
                               lambda b, i: (b, 0, 0, 0))],
        out_specs=q_spec(B_WIDTH),
        out_shape=jax.ShapeDtypeStruct((bsz, seq, B_WIDTH), _BF16),
        scratch_shapes=[pltpu.VMEM((seq, Q_BLOCK), _F32), pltpu.VMEM((seq, Q_BLOCK), _F32),
                        pltpu.VMEM((B_HEAD_DIM + 2 * SUBLANES, B_HEADS * Q_BLOCK), _F32)],
        compiler_params=_compiler_params(2),
        name="dsa_attention",
    )(qi, wit, ki, q, k, vt)


def _xattn_kernel(q_ref, k_ref, v_ref, o_ref):
    scale = X_HEAD_DIM ** -0.5
    for h in range(X_HEADS):
        hs = slice(h * X_HEAD_DIM, (h + 1) * X_HEAD_DIM)
        logits = lax.dot_general(q_ref[0, :, hs], k_ref[0, :, hs], _NT_DIMS,
                                 preferred_element_type=_F32) * scale
        mx = jnp.max(logits, axis=-1, keepdims=True)
        p = jnp.exp(logits - mx)
        denom = jnp.sum(p, axis=-1, keepdims=True)
        o = jnp.dot(p.astype(_BF16), v_ref[0, :, hs], preferred_element_type=_F32)
        o_ref[0, :, hs] = (o / denom).astype(o_ref.dtype)


def _cross_attention(q, k, v, tq=512):
    bsz, seq, _ = q.shape
    m = k.shape[1]
    mem_spec = pl.BlockSpec((1, m, X_WIDTH), lambda b, i: (b, 0, 0))
    return pl.pallas_call(
        _xattn_kernel,
        grid=(bsz, seq // tq),
        in_specs=[pl.BlockSpec((1, tq, X_WIDTH), lambda b, i: (b, i, 0)), mem_spec, mem_spec],
        out_specs=pl.BlockSpec((1, tq, X_WIDTH), lambda b, i: (b, i, 0)),
        out_shape=jax.ShapeDtypeStruct((bsz, seq, X_WIDTH), _BF16),
        compiler_params=_compiler_params(2),
        name="cross_attention",
    )(q, k, v)


def _norm_proj_kernel(x_ref, g_ref, w_ref, o_ref, w_bf16_ref):
    @pl.when(pl.program_id(0) == 0)
    def _():
        w_bf16_ref[...] = w_ref[...].astype(_BF16)

    x = x_ref[...]
    ms = jnp.mean(x * x, axis=-1, keepdims=True)
    h = ((x * lax.rsqrt(ms + EPS)) * g_ref[...]).astype(_BF16)
    o_ref[...] = jnp.dot(h, w_bf16_ref[...], preferred_element_type=_F32).astype(o_ref.dtype)


def _norm_proj(x, g, w_stack, layer, out_dtype, name, tm=512):
    m, d = x.shape
    n = w_stack.shape[2]
    return pl.pallas_call(
        _norm_proj_kernel,
        grid=(m // tm,),
        in_specs=[pl.BlockSpec((tm, d), lambda i: (i, 0)),
                  pl.BlockSpec((1, d), lambda i: (0, 0)),
                  pl.BlockSpec((None, d, n), lambda i: (layer, 0, 0),
                               pipeline_mode=pl.Buffered(1))],
        out_specs=pl.BlockSpec((tm, n), lambda i: (i, 0)),
        out_shape=jax.ShapeDtypeStruct((m, n), out_dtype),
        scratch_shapes=[pltpu.VMEM((d, n), _BF16)],
        compiler_params=_compiler_params(1),
        name=name,
    )(x, g.reshape(1, d), w_stack)


def _xattn_out_kernel(o_ref, w_ref, x_ref, g_ref, x_out_ref, h_out_ref, w_bf16_ref):
    @pl.when(pl.program_id(0) == 0)
    def _():
        w_bf16_ref[...] = w_ref[...].astype(_BF16)

    x = x_ref[...] + jnp.dot(o_ref[...], w_bf16_ref[...], preferred_element_type=_F32)
    x_out_ref[...] = x
    ms = jnp.mean(x * x, axis=-1, keepdims=True)
    h_out_ref[...] = ((x * lax.rsqrt(ms + EPS)) * g_ref[...]).astype(h_out_ref.dtype)


def _xattn_out_and_norm(o, w_stack, layer, x, g, tm=256):
    m, k = o.shape
    d = x.shape[1]
    row_spec = pl.BlockSpec((tm, d), lambda i: (i, 0))
    return pl.pallas_call(
        _xattn_out_kernel,
        grid=(m // tm,),
        in_specs=[pl.BlockSpec((tm, k), lambda i: (i, 0)),
                  pl.BlockSpec((None, k, d), lambda i: (layer, 0, 0),
                               pipeline_mode=pl.Buffered(1)),
                  row_spec,
                  pl.BlockSpec((1, d), lambda i: (0, 0))],
        out_specs=[row_spec, row_spec],
        out_shape=[jax.ShapeDtypeStruct((m, d), _F32), jax.ShapeDtypeStruct((m, d), _BF16)],
        scratch_shapes=[pltpu.VMEM((k, d), _BF16)],
        compiler_params=_compiler_params(1),
        name="xattn_out_norm",
    )(o, w_stack, x, g.reshape(1, d))


def kernel(x, mem, positions, norm_mix_g, w_in, a_norm_g, a_spatial_w, a_spatial_b, p_a, p_b,
           w_out, norm_x_g, norm_mem_g, xq_w, xk_w, xv_w, xo_w, norm_ffn_g, ffn_w1, ffn_w3,
           ffn_w2, final_norm_g):
    bsz, seq, d = x.shape
    m = bsz * seq
    depth = w_in.shape[0]
    ffn_hidden = ffn_w1.shape[-1]
    tables_b = _rope_tables(positions, B_ROT, B_HEAD_DIM)
    tables_i = _rope_tables(positions, IDX_ROT, IDX_DIM)

    xf = x.reshape(m, d)
    mem_f = mem.reshape(bsz * MEM_LEN, d)
    w_in_t = jnp.transpose(w_in, (0, 2, 1))
    for l in range(depth):
        def w_in_cols(col0):
            return Rhs(w_in_t, l, col0, transposed=True)

        h = _rmsnorm(xf, norm_mix_g[l], _BF16)
        att = _fused_matmul([h], [(0, w_in_cols(ZA_END))], [], _ep_identity,
                            ATT_WIDTH, _F32, 1024, PROJ_TN, "proj_att", prefetch_lhs=True)
        za, q, k, vt, qi, ki, wi = _fused_matmul(
            [h], [(0, w_in_cols(0))], [], _ep_identity, ZA_END, _F32, 1024, 512, "proj_za",
            side_job=functools.partial(_prep_side_job, att.reshape(bsz, seq, ATT_WIDTH),
                                       tables_b, tables_i), prefetch_lhs=True)
        y_a = _spatial_gating(za, a_norm_g[l], a_spatial_w[l], a_spatial_b[l])
        y_b = _dsa_attention(q, k, vt, qi, ki, wi.transpose(0, 2, 1)).reshape(m, B_WIDTH)
        merged = _fused_matmul(
            [y_a, y_b, h],
            [(0, Rhs(p_a, l)), (1, Rhs(p_b, l)), (2, w_in_cols(ATT_END)),
             (2, w_in_cols(ATT_END + d))],
            [], _ep_gated_merge, d, _BF16, 1024, 256, "gated_merge", row_splits=2)
        x1 = _fused_matmul([merged], [(0, Rhs(w_out, l))], [xf], _ep_residual,
                           d, _F32, 1024, 512, "out_proj", prefetch_lhs=True)

        qx = _norm_proj(x1, norm_x_g[l], xq_w, l, _BF16, "xattn_q")
        kx = _norm_proj(mem_f, norm_mem_g[l], xk_w, l, _BF16, "xattn_k")
        vx = _norm_proj(mem_f, norm_mem_g[l], xv_w, l, _BF16, "xattn_v")
        ox = _cross_attention(qx.reshape(bsz, seq, X_WIDTH),
                              kx.reshape(bsz, MEM_LEN, X_WIDTH),
                              vx.reshape(bsz, MEM_LEN, X_WIDTH)).reshape(m, X_WIDTH)
        x2, h2 = _xattn_out_and_norm(ox, xo_w, l, x1, norm_ffn_g[l])

        act, w2_bf16 = _fused_matmul([h2], [(0, Rhs(ffn_w1, l)), (0, Rhs(ffn_w3, l))],
                                     [], _ep_swiglu, ffn_hidden, _BF16, 2048, 256, "ffn_up",
                                     side_job=functools.partial(_cast_side_job, ffn_w2, l))
        xf = _fused_matmul([act], [(0, Rhs(w2_bf16))], [x2], _ep_residual,
                           d, _F32, 512, 512, "ffn_down", prefetch_lhs=True)

    return _rmsnorm(xf, final_norm_g, _F32).reshape(bsz, seq, d)
```

```python
import functools
from typing import Callable, NamedTuple, Optional

import jax
import jax.numpy as jnp
from jax import lax
from jax.experimental import pallas as pl
from jax.experimental.pallas import tpu as pltpu

D_MODEL = 4096
MEM_LEN = 256
EPS = 1e-6
ROPE_THETA = 500000.0
CHUNK = 128
A_GROUPS = 16
A_WIDTH = D_MODEL // 2
A_GROUP_DIM = A_WIDTH // A_GROUPS
B_HEADS = 16
B_HEAD_DIM = 128
B_KV_HEADS = 4
B_WIDTH = B_HEADS * B_HEAD_DIM
B_KV_WIDTH = B_KV_HEADS * B_HEAD_DIM
B_ROT = B_HEAD_DIM // 4
IDX_HEADS = 16
IDX_DIM = 64
IDX_WIDTH = IDX_HEADS * IDX_DIM
IDX_ROT = IDX_DIM // 4
TOPK_MAX = 256
Q_BLOCK = 128
X_HEADS = 4
X_HEAD_DIM = 256
X_WIDTH = X_HEADS * X_HEAD_DIM
IN_SIZES = (2 * A_WIDTH, B_WIDTH, B_KV_WIDTH, B_KV_WIDTH, IDX_WIDTH, IDX_DIM, IDX_HEADS,
            2 * D_MODEL)

LANES = 128
SUBLANES = 8
KEY_CHUNK = 512
PROJ_TN = 512
ZA_END = IN_SIZES[0]
ATT_END = ZA_END + sum(IN_SIZES[1:7])
ATT_Q0 = 0
ATT_K0 = ATT_Q0 + B_WIDTH
ATT_V0 = ATT_K0 + B_KV_WIDTH
ATT_QI0 = ATT_V0 + B_KV_WIDTH
ATT_KW0 = ATT_QI0 + IDX_WIDTH
ATT_WIDTH = -(-(ATT_END - ZA_END) // PROJ_TN) * PROJ_TN
assert ZA_END % PROJ_TN == 0 and ATT_KW0 + LANES <= ATT_WIDTH

VMEM_LIMIT_BYTES = 60 * 1024 * 1024


class Tile(NamedTuple):
    tm: int
    tn: int
    prefetch_lhs: bool = False
    row_splits: int = 1


TILES = {
    "proj_att": Tile(1024, PROJ_TN, prefetch_lhs=True),
    "proj_za": Tile(1024, 512, prefetch_lhs=True),
    "gated_merge": Tile(1024, 256, row_splits=2),
    "out_proj": Tile(1024, 512, prefetch_lhs=True),
    "ffn_up": Tile(2048, 256),
    "ffn_down": Tile(512, 512, prefetch_lhs=True),
}
ROW_BLOCK = {
    "rmsnorm": 512,
    "spatial_gating": 2 * CHUNK,
    "cross_attention": 512,
    "norm_proj": 512,
    "xattn_out_norm": 256,
}

_BF16 = jnp.bfloat16
_F32 = jnp.float32
_NEG_INF = float("-inf")
_NN_DIMS = (((1,), (0,)), ((), ()))
_NT_DIMS = (((1,), (1,)), ((), ()))


def _compiler_params(n_axes):
    return pltpu.CompilerParams(
        dimension_semantics=("arbitrary",) * n_axes,
        vmem_limit_bytes=VMEM_LIMIT_BYTES,
    )


def _rmsnorm_kernel(x_ref, g_ref, o_ref):
    x = x_ref[...]
    ms = jnp.mean(x * x, axis=-1, keepdims=True)
    y = x * lax.rsqrt(ms + EPS)
    o_ref[...] = (y * g_ref[...]).astype(o_ref.dtype)


def _rmsnorm(x, g, out_dtype):
    m, d = x.shape
    tm = ROW_BLOCK["rmsnorm"]
    return pl.pallas_call(
        _rmsnorm_kernel,
        grid=(m // tm,),
        in_specs=[pl.BlockSpec((tm, d), lambda i: (i, 0)),
                  pl.BlockSpec((1, d), lambda i: (0, 0))],
        out_specs=pl.BlockSpec((tm, d), lambda i: (i, 0)),
        out_shape=jax.ShapeDtypeStruct((m, d), out_dtype),
        compiler_params=_compiler_params(1),
        name="rmsnorm",
    )(x, g.reshape(1, d))


def _ep_identity(accs, extras):
    return accs[0]


def _ep_residual(accs, extras):
    return extras[0] + accs[0]


def _ep_gated_merge(accs, extras):
    y_a, y_b, gate_a, gate_b = accs
    return jax.nn.sigmoid(gate_a) * y_a + jax.nn.sigmoid(gate_b) * y_b


def _ep_swiglu(accs, extras):
    return jax.nn.silu(accs[0]) * accs[1]


class Rhs(NamedTuple):
    array: jax.Array
    layer: Optional[int] = None
    col0: int = 0
    transposed: bool = False


def _rhs_spec(r, tn):
    if r.transposed:
        k = r.array.shape[2]
        if r.col0 % tn == 0:
            return pl.BlockSpec((None, tn, k), lambda i, j: (r.layer, j + r.col0 // tn, 0))
        assert r.col0 % SUBLANES == 0
        return pl.BlockSpec((pl.Element(1), pl.Element(tn), pl.Element(k)),
                            lambda i, j: (r.layer, (r.col0 // SUBLANES + j * (tn // SUBLANES))
                                          * SUBLANES, 0))
    assert r.col0 % tn == 0
    off = r.col0 // tn
    if r.layer is None:
        return pl.BlockSpec((r.array.shape[0], tn), lambda i, j: (0, j + off))
    return pl.BlockSpec((None, r.array.shape[1], tn), lambda i, j: (r.layer, 0, j + off))


class SideJob(NamedTuple):
    body: Callable
    arrays: tuple
    in_specs: tuple
    out_specs: tuple
    out_shape: tuple


def _cast_body(x_ref, o_ref):
    o_ref[...] = x_ref[...].astype(o_ref.dtype)


def _cast_side_job(w_stack, layer, n_steps):
    _, rows, cols = w_stack.shape
    slab = rows // n_steps
    assert slab * n_steps == rows and slab % (2 * SUBLANES) == 0
    return SideJob(_cast_body, (w_stack,),
                   (pl.BlockSpec((None, slab, cols), lambda s: (layer, s, 0)),),
                   (pl.BlockSpec((slab, cols), lambda s: (s, 0)),),
                   (jax.ShapeDtypeStruct((rows, cols), _BF16),))


def _mm_kernel(*refs, n_lhs, pair_lhs, rhs_transposed, n_extra, epilogue, side_body, n_side_in,
               row_splits):
    n_rhs = len(pair_lhs)
    lhs_refs = refs[:n_lhs]
    rhs_refs = refs[n_lhs:n_lhs + n_rhs]
    n_in = n_lhs + n_rhs + n_extra
    extra_refs = refs[n_lhs + n_rhs:n_in]
    o_ref = refs[n_in + n_side_in]
    if side_body is not None:
        side_body(*refs[n_in:n_in + n_side_in], *refs[n_in + n_side_in + 1:])
    weights = [(r[0] if len(r.shape) == 3 else r[...]).astype(_BF16) for r in rhs_refs]
    rows_per_split = o_ref.shape[0] // row_splits
    for s in range(row_splits):
        rows = slice(s * rows_per_split, (s + 1) * rows_per_split)
        accs = [lax.dot_general(lhs_refs[li][rows, :], w, _NT_DIMS if tr else _NN_DIMS,
                                preferred_element_type=_F32)
                for li, w, tr in zip(pair_lhs, weights, rhs_transposed)]
        o_ref[rows, :] = epilogue(accs, [e[rows, :] for e in extra_refs]).astype(o_ref.dtype)


def _fused_matmul(name, lhs, pairs, extras, epilogue, n_out, out_dtype, side_job=None):
    tm, tn, prefetch_lhs, row_splits = TILES[name]
    m = lhs[0].shape[0]
    assert m % tm == 0 and n_out % tn == 0
    grid = (m // tm, n_out // tn)
    lhs_mode = {} if prefetch_lhs else {"pipeline_mode": pl.Buffered(1)}
    in_specs = [pl.BlockSpec((tm, a.shape[1]), lambda i, j: (i, 0), **lhs_mode) for a in lhs]
    in_specs += [_rhs_spec(r, tn) for _, r in pairs]
    in_specs += [pl.BlockSpec((tm, tn), lambda i, j: (i, j)) for _ in extras]
    out_specs = [pl.BlockSpec((tm, tn), lambda i, j: (i, j))]
    out_shape = [jax.ShapeDtypeStruct((m, n_out), out_dtype)]
    operands = list(lhs) + [p[1].array for p in pairs] + list(extras)
    side = None if side_job is None else side_job(grid[0] * grid[1])
    if side is not None:
        def per_step(spec):
            return pl.BlockSpec(spec.block_shape,
                                lambda i, j: spec.index_map(i * grid[1] + j))
        in_specs += [per_step(s) for s in side.in_specs]
        out_specs += [per_step(s) for s in side.out_specs]
        out_shape += list(side.out_shape)
        operands += list(side.arrays)
    kern = functools.partial(_mm_kernel, n_lhs=len(lhs), pair_lhs=tuple(p[0] for p in pairs),
                             rhs_transposed=tuple(p[1].transposed for p in pairs),
                             n_extra=len(extras), epilogue=epilogue,
                             side_body=None if side is None else side.body,
                             n_side_in=0 if side is None else len(side.in_specs),
                             row_splits=row_splits)
    outs = pl.pallas_call(
        kern,
        grid=grid,
        in_specs=in_specs,
        out_specs=out_specs,
        out_shape=out_shape,
        compiler_params=_compiler_params(2),
        name=name,
    )(*operands)
    return outs[0] if side is None else tuple(outs)


def _gating_kernel(za_ref, ng_ref, w_ref, bt_ref, o_ref):
    row = lax.broadcasted_iota(jnp.int32, (CHUNK, CHUNK), 0)
    col = lax.broadcasted_iota(jnp.int32, (CHUNK, CHUNK), 1)
    causal = col <= row
    for c in range(za_ref.shape[0] // CHUNK):
        rows = slice(c * CHUNK, (c + 1) * CHUNK)
        z = jax.nn.gelu(za_ref[rows, :])
        u = z[:, :A_WIDTH]
        v = z[:, A_WIDTH:]
        ms = jnp.mean(v * v, axis=-1, keepdims=True)
        vn = ((v * lax.rsqrt(ms + EPS)) * ng_ref[...]).astype(_BF16)
        for g in range(A_GROUPS):
            sl = slice(g * A_GROUP_DIM, (g + 1) * A_GROUP_DIM)
            w = jnp.where(causal, w_ref[g], 0.0).astype(_BF16)
            s = jnp.dot(w, vn[:, sl], preferred_element_type=_F32) + bt_ref[:, g:g + 1]
            o_ref[rows, sl] = (u[:, sl] * s).astype(o_ref.dtype)


def _spatial_gating(za, norm_g, w_s, b_s):
    m = za.shape[0]
    tm = ROW_BLOCK["spatial_gating"]
    return pl.pallas_call(
        _gating_kernel,
        grid=(m // tm,),
        in_specs=[pl.BlockSpec((tm, 2 * A_WIDTH), lambda i: (i, 0)),
                  pl.BlockSpec((1, A_WIDTH), lambda i: (0, 0)),
                  pl.BlockSpec((A_GROUPS, CHUNK, CHUNK), lambda i: (0, 0, 0)),
                  pl.BlockSpec((CHUNK, A_GROUPS), lambda i: (0, 0))],
        out_specs=pl.BlockSpec((tm, A_WIDTH), lambda i: (i, 0)),
        out_shape=jax.ShapeDtypeStruct((m, A_WIDTH), _BF16),
        compiler_params=_compiler_params(1),
        name="spatial_gating",
    )(za, norm_g.reshape(1, A_WIDTH), w_s, b_s.T)


def _rope_tables(positions, rot_dim, period):
    half = rot_dim // 2
    inv_freq = ROPE_THETA ** (-jnp.arange(0, rot_dim, 2, dtype=_F32) / rot_dim)
    ang = positions.astype(_F32)[..., None] * inv_freq
    cos, sin = jnp.cos(ang), jnp.sin(ang)
    zeros_half = jnp.zeros_like(sin)
    rest = jnp.zeros(sin.shape[:-1] + (period - rot_dim,), _F32)

    def lanes(*pieces):
        return jnp.tile(jnp.concatenate(pieces, axis=-1), LANES // period)

    return (lanes(cos, cos, rest + 1.0), lanes(zeros_half, sin, rest),
            lanes(-sin, zeros_half, rest))


def _rope(x, c, s1, s2, half):
    return x * c + pltpu.roll(x, half, 1) * s1 + pltpu.roll(x, LANES - half, 1) * s2


def _prep_kernel(att_ref, cb_ref, s1b_ref, s2b_ref, ci_ref, s1i_ref, s2i_ref,
                 q_ref, k_ref, vt_ref, qi_ref, ki_ref, wi_ref):
    cb, s1b, s2b = cb_ref[0], s1b_ref[0], s2b_ref[0]
    ci, s1i, s2i = ci_ref[0], s1i_ref[0], s2i_ref[0]
    for h in range(B_HEADS):
        x = att_ref[0, :, ATT_Q0 + h * LANES:ATT_Q0 + (h + 1) * LANES]
        q_ref[0, :, h * LANES:(h + 1) * LANES] = _rope(x, cb, s1b, s2b, B_ROT // 2).astype(_BF16)
    for h in range(B_KV_HEADS):
        x = att_ref[0, :, ATT_K0 + h * LANES:ATT_K0 + (h + 1) * LANES]
        k_ref[0, :, h * LANES:(h + 1) * LANES] = _rope(x, cb, s1b, s2b, B_ROT // 2).astype(_BF16)
    vt_ref[0, 0] = att_ref[0, :, ATT_V0:ATT_V0 + B_KV_WIDTH].T.astype(_BF16)
    lane = lax.broadcasted_iota(jnp.int32, ci.shape, 1)
    is_ki = lane < IDX_DIM
    heads_per_group = LANES // IDX_DIM
    for j in range(IDX_WIDTH // LANES):
        x = att_ref[0, :, ATT_QI0 + j * LANES:ATT_QI0 + (j + 1) * LANES]
        y = _rope(x, ci, s1i, s2i, IDX_ROT // 2)
        for r in range(heads_per_group):
            h = j * heads_per_group + r
            yr = y if r == 0 else pltpu.roll(y, LANES - r * IDX_DIM, 1)
            qi_ref[0, :, h * LANES:(h + 1) * LANES] = jnp.where(is_ki, yr, 0.0).astype(_BF16)
    x = att_ref[0, :, ATT_KW0:ATT_KW0 + LANES]
    y = _rope(x, jnp.where(is_ki, ci, 1.0), jnp.where(is_ki, s1i, 0.0),
              jnp.where(is_ki, s2i, 0.0), IDX_ROT // 2)
    ki_ref[0] = jnp.where(is_ki, y, 0.0).astype(_BF16)
    wi_ref[0] = y[:, IDX_DIM:IDX_DIM + IDX_HEADS]


def _prep_side_job(att, tables_b, tables_i, n_steps):
    bsz, seq, _ = att.shape
    ts = bsz * seq // n_steps
    assert ts * n_steps == bsz * seq and seq % ts == 0 and KEY_CHUNK % ts == 0
    assert ts % (2 * SUBLANES) == 0
    per_seq = seq // ts
    per_chunk = KEY_CHUNK // ts

    def tokens(width):
        return pl.BlockSpec((1, ts, width), lambda s: (s // per_seq, s % per_seq, 0))

    def out(width, dtype):
        return tokens(width), jax.ShapeDtypeStruct((bsz, seq, width), dtype)

    vt_out = (pl.BlockSpec((1, 1, B_KV_WIDTH, ts),
                           lambda s: (s // per_seq, (s % per_seq) // per_chunk, 0,
                                      (s % per_seq) % per_chunk)),
              jax.ShapeDtypeStruct((bsz, seq // KEY_CHUNK, B_KV_WIDTH, KEY_CHUNK), _BF16))
    outs = [out(B_WIDTH, _BF16), out(B_KV_WIDTH, _BF16), vt_out,
            out(IDX_HEADS * LANES, _BF16), out(LANES, _BF16), out(IDX_HEADS, _F32)]
    return SideJob(_prep_kernel, (att, *tables_b, *tables_i),
                   (tokens(ATT_WIDTH),) + (tokens(LANES),) * 6,
                   tuple(o[0] for o in outs), tuple(o[1] for o in outs))


def _ordered_int_to_float(key):
    bits = jnp.where(key < 0, key ^ jnp.int32(0x7FFFFFFF), key)
    return lax.bitcast_convert_type(bits, _F32)


def _fold_rows(x, op):
    while x.shape[0] > SUBLANES:
        half = x.shape[0] // 2
        x = op(x[:half], x[half:])
    return x


def _dsa_kernel(*refs, n_sel):
    seq = refs[4].shape[1]
    n_chunks = (pl.program_id(1) * Q_BLOCK + Q_BLOCK + KEY_CHUNK - 1) // KEY_CHUNK
    for n in range(1, seq // KEY_CHUNK + 1):
        pl.when(n_chunks == n)(functools.partial(_dsa_block, *refs, n_sel=n_sel, n_chunks=n))


def _dsa_block(qi_ref, wit_ref, ki_ref, q_ref, k_ref, vt_ref, o_ref,
               score_ref, bias_ref, logit_ref, acc_ref, *, n_sel, n_chunks):
    t = Q_BLOCK
    blk = pl.program_id(1)
    qpos = blk * t + lax.broadcasted_iota(jnp.int32, (1, t), 1)
    kiota = lax.broadcasted_iota(jnp.int32, (KEY_CHUNK, 1), 0)
    idx_scale = (IDX_DIM ** -0.5) * (IDX_HEADS ** -0.5)

    def rows(c):
        return slice(c * KEY_CHUNK, (c + 1) * KEY_CHUNK)

    def for_chunks(body, carry):
        for c in range(n_chunks):
            carry = body(c, carry)
        return carry

    def indexer_chunk(c, carry):
        ki = ki_ref[0, rows(c), :]
        acc = jnp.zeros((KEY_CHUNK, t), _F32)
        for h in range(0, IDX_HEADS, 2):
            qpair = jnp.concatenate([qi_ref[0, :, h * LANES:(h + 1) * LANES],
                                     qi_ref[0, :, (h + 1) * LANES:(h + 2) * LANES]], axis=0)
            d = lax.dot_general(ki, qpair, _NT_DIMS, preferred_element_type=_F32)
            acc = acc + wit_ref[0, h:h + 1, :] * jnp.maximum(d[:, :t], 0.0)
            acc = acc + wit_ref[0, h + 1:h + 2, :] * jnp.maximum(d[:, t:], 0.0)
        kpos = c * KEY_CHUNK + kiota
        score_ref[rows(c), :] = jnp.where(kpos <= qpos, acc * idx_scale + 0.0, _NEG_INF)
        return carry

    for_chunks(indexer_chunk, 0)

    def count(indicator):
        def chunk(c, part):
            return part + _fold_rows(indicator(score_ref[rows(c), :], c), jnp.add)
        part = for_chunks(chunk, jnp.zeros((SUBLANES, t), _F32))
        return jnp.sum(part, axis=0, keepdims=True)

    def count_ge(cand_f):
        return count(lambda s, c: jnp.where(s >= cand_f, 1.0, 0.0))

    int_min = jnp.int32(-2 ** 31)
    base = jnp.where(count_ge(jnp.zeros((1, t), _F32)) >= n_sel, jnp.int32(0), int_min)

    def search_step(it, base):
        cand = base | lax.shift_left(jnp.int32(1), jnp.int32(30) - it)
        cnt = count_ge(_ordered_int_to_float(cand))
        return jnp.where(cnt >= n_sel, cand, base)

    base = lax.fori_loop(0, 31, search_step, base)
    thr = jnp.where(qpos + 1 < n_sel, _NEG_INF, _ordered_int_to_float(base))

    def bias_chunk(c, part):
        s = score_ref[rows(c), :]
        sel = jnp.where(c * KEY_CHUNK + kiota <= qpos, jnp.where(s >= thr, 1.0, 0.0), 0.0)
        bias_ref[rows(c), :] = jnp.where(sel > 0.0, 0.0, _NEG_INF)
        return part + _fold_rows(sel, jnp.add)

    n_ge = jnp.sum(for_chunks(bias_chunk, jnp.zeros((SUBLANES, t), _F32)), axis=0, keepdims=True)

    @pl.when(jnp.max(n_ge) > n_sel)
    def _():
        nxt = _ordered_int_to_float(base + 1)
        tied = n_ge > n_sel
        n_above = count(lambda s, c: jnp.where(s >= nxt, 1.0, 0.0))
        need = jnp.where(tied, n_sel - n_above, 0.0)
        no_index = float(k_ref.shape[1])
        front_v = jnp.where(tied, jnp.inf, _NEG_INF)
        front_i = jnp.where(tied, -1.0, no_index)

        def beyond(s, kposf, front_v, front_i):
            after = jnp.where(s < front_v, 1.0,
                              jnp.where(s == front_v, jnp.where(kposf > front_i, 1.0, 0.0), 0.0))
            return jnp.where(s >= thr, jnp.where(s < nxt, after, 0.0), 0.0)

        def kposf(c):
            return (c * KEY_CHUNK + kiota).astype(_F32)

        def advance(_, carry):
            front_v, front_i, need = carry

            def best_value(c, part):
                s = score_ref[rows(c), :]
                cand = jnp.where(beyond(s, kposf(c), front_v, front_i) > 0.0, s, _NEG_INF)
                return jnp.maximum(part, _fold_rows(cand, jnp.maximum))

            v = jnp.max(for_chunks(best_value, jnp.full((SUBLANES, t), _NEG_INF, _F32)),
                        axis=0, keepdims=True)

            def first_index(c, part):
                s = score_ref[rows(c), :]
                hit = jnp.where(s == v, beyond(s, kposf(c), front_v, front_i), 0.0)
                cand = jnp.where(hit > 0.0, kposf(c), no_index)
                return jnp.minimum(part, _fold_rows(cand, jnp.minimum))

            i = jnp.min(for_chunks(first_index, jnp.full((SUBLANES, t), no_index, _F32)),
                        axis=0, keepdims=True)
            active = need > 0.0
            return (jnp.where(active, v, front_v), jnp.where(active, i, front_i),
                    jnp.where(active, need - 1.0, need))

        front_v, front_i, _ = lax.fori_loop(0, jnp.max(need).astype(jnp.int32), advance,
                                            (front_v, front_i, need))

        def tie_chunk(c, carry):
            s = score_ref[rows(c), :]
            upto = jnp.where(s > front_v, 0.0,
                             jnp.where(s == front_v,
                                       jnp.where(kposf(c) <= front_i, 0.0, _NEG_INF), _NEG_INF))
            keep = jnp.where(s >= nxt, 0.0, jnp.where(s >= thr, upto, _NEG_INF))
            bias_ref[rows(c), :] = jnp.where(c * KEY_CHUNK + kiota <= qpos, keep, _NEG_INF)
            return carry

        for_chunks(tie_chunk, 0)

    grp = B_HEADS // B_KV_HEADS
    gw = grp * t
    exp2_scale = (B_HEAD_DIM ** -0.5) * 1.4426950408889634

    def logit_chunk(c, mx):
        bias = jnp.concatenate([bias_ref[rows(c), :]] * grp, axis=1)
        folded = []
        for g in range(B_KV_HEADS):
            qg = jnp.concatenate(
                [q_ref[0, :, (g * grp + hh) * B_HEAD_DIM:(g * grp + hh + 1) * B_HEAD_DIM]
                 for hh in range(grp)], axis=0)
            l = lax.dot_general(k_ref[0, rows(c), g * B_HEAD_DIM:(g + 1) * B_HEAD_DIM], qg,
                                _NT_DIMS, preferred_element_type=_F32) + bias
            logit_ref[rows(c), g * gw:(g + 1) * gw] = l
            folded.append(_fold_rows(l, jnp.maximum))
        return jnp.maximum(mx, jnp.concatenate(folded, axis=1))

    mx = for_chunks(logit_chunk, jnp.full((SUBLANES, B_HEADS * t), _NEG_INF, _F32))
    mx = jnp.max(mx, axis=0, keepdims=True)

    acc_ref[...] = jnp.zeros_like(acc_ref)
    ones_rows = jnp.ones((acc_ref.shape[0] - B_HEAD_DIM, KEY_CHUNK), _BF16)

    def pv_chunk(c, carry):
        for g in range(B_KV_HEADS):
            cols = slice(g * gw, (g + 1) * gw)
            p = jnp.exp2((logit_ref[rows(c), cols] - mx[:, cols]) * exp2_scale)
            vt_ext = jnp.concatenate(
                [vt_ref[0, c, g * B_HEAD_DIM:(g + 1) * B_HEAD_DIM, :], ones_rows], axis=0)
            acc_ref[:, cols] += jnp.dot(vt_ext, p.astype(_BF16), preferred_element_type=_F32)
        return carry

    for_chunks(pv_chunk, 0)
    out_t = acc_ref[:B_HEAD_DIM, :] / acc_ref[B_HEAD_DIM:B_HEAD_DIM + 1, :]
    for h in range(B_HEADS):
        o_ref[0, :, h * B_HEAD_DIM:(h + 1) * B_HEAD_DIM] = (
            out_t[:, h * t:(h + 1) * t].T.astype(o_ref.dtype))


def _dsa_attention(q, k, vt, qi, ki, wit):
    bsz, seq, _ = q.shape
    assert Q_BLOCK == LANES and seq % KEY_CHUNK == 0
    n_sel = min(TOPK_MAX, seq // 4)

    def q_spec(width):
        return pl.BlockSpec((1, Q_BLOCK, width), lambda b, i: (b, i, 0))

    def kv_spec(width):
        return pl.BlockSpec((1, seq, width), lambda b, i: (b, 0, 0))

    return pl.pallas_call(
        functools.partial(_dsa_kernel, n_sel=n_sel),
        grid=(bsz, seq // Q_BLOCK),
        in_specs=[q_spec(IDX_HEADS * LANES),
                  pl.BlockSpec((1, IDX_HEADS, Q_BLOCK), lambda b, i: (b, 0, i)),
                  kv_spec(LANES), q_spec(B_WIDTH), kv_spec(B_KV_WIDTH),
                  pl.BlockSpec((1, seq // KEY_CHUNK, B_KV_WIDTH, KEY_CHUNK),
                               lambda b, i: (b, 0, 0, 0))],
        out_specs=q_spec(B_WIDTH),
        out_shape=jax.ShapeDtypeStruct((bsz, seq, B_WIDTH), _BF16),
        scratch_shapes=[pltpu.VMEM((seq, Q_BLOCK), _F32), pltpu.VMEM((seq, Q_BLOCK), _F32),
                        pltpu.VMEM((seq, B_HEADS * Q_BLOCK), _F32),
                        pltpu.VMEM((B_HEAD_DIM + 2 * SUBLANES, B_HEADS * Q_BLOCK), _F32)],
        compiler_params=_compiler_params(2),
        name="dsa_attention",
    )(qi, wit, ki, q, k, vt)


def _xattn_kernel(q_ref, k_ref, v_ref, o_ref):
    scale = X_HEAD_DIM ** -0.5
    for h in range(X_HEADS):
        hs = slice(h * X_HEAD_DIM, (h + 1) * X_HEAD_DIM)
        logits = lax.dot_general(q_ref[0, :, hs], k_ref[0, :, hs], _NT_DIMS,
                                 preferred_element_type=_F32) * scale
        mx = jnp.max(logits, axis=-1, keepdims=True)
        p = jnp.exp(logits - mx)
        denom = jnp.sum(p, axis=-1, keepdims=True)
        o = jnp.dot(p.astype(_BF16), v_ref[0, :, hs], preferred_element_type=_F32)
        o_ref[0, :, hs] = (o / denom).astype(o_ref.dtype)


def _cross_attention(q, k, v):
    bsz, seq, _ = q.shape
    tq = ROW_BLOCK["cross_attention"]
    m = k.shape[1]
    mem_spec = pl.BlockSpec((1, m, X_WIDTH), lambda b, i: (b, 0, 0))
    return pl.pallas_call(
        _xattn_kernel,
        grid=(bsz, seq // tq),
        in_specs=[pl.BlockSpec((1, tq, X_WIDTH), lambda b, i: (b, i, 0)), mem_spec, mem_spec],
        out_specs=pl.BlockSpec((1, tq, X_WIDTH), lambda b, i: (b, i, 0)),
        out_shape=jax.ShapeDtypeStruct((bsz, seq, X_WIDTH), _BF16),
        compiler_params=_compiler_params(2),
        name="cross_attention",
    )(q, k, v)


def _norm_proj_kernel(x_ref, g_ref, w_ref, o_ref, w_bf16_ref):
    @pl.when(pl.program_id(0) == 0)
    def _():
        w_bf16_ref[...] = w_ref[...].astype(_BF16)

    x = x_ref[...]
    ms = jnp.mean(x * x, axis=-1, keepdims=True)
    h = ((x * lax.rsqrt(ms + EPS)) * g_ref[...]).astype(_BF16)
    o_ref[...] = jnp.dot(h, w_bf16_ref[...], preferred_element_type=_F32).astype(o_ref.dtype)


def _norm_proj(x, g, w_stack, layer, out_dtype, name):
    m, d = x.shape
    tm = ROW_BLOCK["norm_proj"]
    n = w_stack.shape[2]
    return pl.pallas_call(
        _norm_proj_kernel,
        grid=(m // tm,),
        in_specs=[pl.BlockSpec((tm, d), lambda i: (i, 0)),
                  pl.BlockSpec((1, d), lambda i: (0, 0)),
                  pl.BlockSpec((None, d, n), lambda i: (layer, 0, 0),
                               pipeline_mode=pl.Buffered(1))],
        out_specs=pl.BlockSpec((tm, n), lambda i: (i, 0)),
        out_shape=jax.ShapeDtypeStruct((m, n), out_dtype),
        scratch_shapes=[pltpu.VMEM((d, n), _BF16)],
        compiler_params=_compiler_params(1),
        name=name,
    )(x, g.reshape(1, d), w_stack)


def _xattn_out_kernel(o_ref, w_ref, x_ref, g_ref, x_out_ref, h_out_ref, w_bf16_ref):
    @pl.when(pl.program_id(0) == 0)
    def _():
        w_bf16_ref[...] = w_ref[...].astype(_BF16)

    x = x_ref[...] + jnp.dot(o_ref[...], w_bf16_ref[...], preferred_element_type=_F32)
    x_out_ref[...] = x
    ms = jnp.mean(x * x, axis=-1, keepdims=True)
    h_out_ref[...] = ((x * lax.rsqrt(ms + EPS)) * g_ref[...]).astype(h_out_ref.dtype)


def _xattn_out_and_norm(o, w_stack, layer, x, g):
    m, k = o.shape
    tm = ROW_BLOCK["xattn_out_norm"]
    d = x.shape[1]
    row_spec = pl.BlockSpec((tm, d), lambda i: (i, 0))
    return pl.pallas_call(
        _xattn_out_kernel,
        grid=(m // tm,),
        in_specs=[pl.BlockSpec((tm, k), lambda i: (i, 0)),
                  pl.BlockSpec((None, k, d), lambda i: (layer, 0, 0),
                               pipeline_mode=pl.Buffered(1)),
                  row_spec,
                  pl.BlockSpec((1, d), lambda i: (0, 0))],
        out_specs=[row_spec, row_spec],
        out_shape=[jax.ShapeDtypeStruct((m, d), _F32), jax.ShapeDtypeStruct((m, d), _BF16)],
        scratch_shapes=[pltpu.VMEM((k, d), _BF16)],
        compiler_params=_compiler_params(1),
        name="xattn_out_norm",
    )(o, w_stack, x, g.reshape(1, d))


def kernel(x, mem, positions, norm_mix_g, w_in, a_norm_g, a_spatial_w, a_spatial_b, p_a, p_b,
           w_out, norm_x_g, norm_mem_g, xq_w, xk_w, xv_w, xo_w, norm_ffn_g, ffn_w1, ffn_w3,
           ffn_w2, final_norm_g):
    bsz, seq, d = x.shape
    m = bsz * seq
    depth = w_in.shape[0]
    ffn_hidden = ffn_w1.shape[-1]
    tables_b = _rope_tables(positions, B_ROT, B_HEAD_DIM)
    tables_i = _rope_tables(positions, IDX_ROT, IDX_DIM)

    xf = x.reshape(m, d)
    mem_f = mem.reshape(bsz * MEM_LEN, d)
    w_in_t = jnp.transpose(w_in, (0, 2, 1))
    for l in range(depth):
        def w_in_cols(col0):
            return Rhs(w_in_t, l, col0, transposed=True)

        h = _rmsnorm(xf, norm_mix_g[l], _BF16)
        att = _fused_matmul("proj_att", [h], [(0, w_in_cols(ZA_END))], [], _ep_identity,
                            ATT_WIDTH, _F32)
        za, q, k, vt, qi, ki, wi = _fused_matmul(
            "proj_za", [h], [(0, w_in_cols(0))], [], _ep_identity, ZA_END, _F32,
            side_job=functools.partial(_prep_side_job, att.reshape(bsz, seq, ATT_WIDTH),
                                       tables_b, tables_i))
        y_a = _spatial_gating(za, a_norm_g[l], a_spatial_w[l], a_spatial_b[l])
        y_b = _dsa_attention(q, k, vt, qi, ki, wi.transpose(0, 2, 1)).reshape(m, B_WIDTH)
        merged = _fused_matmul(
            "gated_merge", [y_a, y_b, h],
            [(0, Rhs(p_a, l)), (1, Rhs(p_b, l)), (2, w_in_cols(ATT_END)),
             (2, w_in_cols(ATT_END + d))],
            [], _ep_gated_merge, d, _BF16)
        x1 = _fused_matmul("out_proj", [merged], [(0, Rhs(w_out, l))], [xf], _ep_residual,
                           d, _F32)

        qx = _norm_proj(x1, norm_x_g[l], xq_w, l, _BF16, "xattn_q")
        kx = _norm_proj(mem_f, norm_mem_g[l], xk_w, l, _BF16, "xattn_k")
        vx = _norm_proj(mem_f, norm_mem_g[l], xv_w, l, _BF16, "xattn_v")
        ox = _cross_attention(qx.reshape(bsz, seq, X_WIDTH),
                              kx.reshape(bsz, MEM_LEN, X_WIDTH),
                              vx.reshape(bsz, MEM_LEN, X_WIDTH)).reshape(m, X_WIDTH)
        x2, h2 = _xattn_out_and_norm(ox, xo_w, l, x1, norm_ffn_g[l])

        act, w2_bf16 = _fused_matmul("ffn_up", [h2],
                                     [(0, Rhs(ffn_w1, l)), (0, Rhs(ffn_w3, l))],
                                     [], _ep_swiglu, ffn_hidden, _BF16,
                                     side_job=functools.partial(_cast_side_job, ffn_w2, l))
        xf = _fused_matmul("ffn_down", [act], [(0, Rhs(w2_bf16))], [x2], _ep_residual, d, _F32)

    return _rmsnorm(xf, final_norm_g, _F32).reshape(bsz, seq, d)
```

```python
import functools
from typing import Callable, NamedTuple, Optional

import jax
import jax.numpy as jnp
from jax import lax
from jax.experimental import pallas as pl
from jax.experimental.pallas import tpu as pltpu

D_MODEL = 4096
MEM_LEN = 256
EPS = 1e-6
ROPE_THETA = 500000.0
CHUNK = 128
A_GROUPS = 16
A_WIDTH = D_MODEL // 2
A_GROUP_DIM = A_WIDTH // A_GROUPS
B_HEADS = 16
B_HEAD_DIM = 128
B_KV_HEADS = 4
B_WIDTH = B_HEADS * B_HEAD_DIM
B_KV_WIDTH = B_KV_HEADS * B_HEAD_DIM
B_ROT = B_HEAD_DIM // 4
IDX_HEADS = 16
IDX_DIM = 64
IDX_WIDTH = IDX_HEADS * IDX_DIM
IDX_ROT = IDX_DIM // 4
TOPK_MAX = 256
Q_BLOCK = 128
X_HEADS = 4
X_HEAD_DIM = 256
X_WIDTH = X_HEADS * X_HEAD_DIM
IN_SIZES = (2 * A_WIDTH, B_WIDTH, B_KV_WIDTH, B_KV_WIDTH, IDX_WIDTH, IDX_DIM, IDX_HEADS,
            2 * D_MODEL)

LANES = 128
SUBLANES = 8
KEY_CHUNK = 512
PROJ_TN = 512
ZA_END = IN_SIZES[0]
ATT_END = ZA_END + sum(IN_SIZES[1:7])
ATT_Q0 = 0
ATT_K0 = ATT_Q0 + B_WIDTH
ATT_V0 = ATT_K0 + B_KV_WIDTH
ATT_QI0 = ATT_V0 + B_KV_WIDTH
ATT_KW0 = ATT_QI0 + IDX_WIDTH
ATT_WIDTH = -(-(ATT_END - ZA_END) // PROJ_TN) * PROJ_TN
assert ZA_END % PROJ_TN == 0 and ATT_KW0 + LANES <= ATT_WIDTH

VMEM_LIMIT_BYTES = 60 * 1024 * 1024


class Tile(NamedTuple):
    tm: int
    tn: int
    prefetch_lhs: bool = False
    row_splits: int = 1


TILES = {
    "proj_att": Tile(1024, PROJ_TN, prefetch_lhs=True),
    "proj_za": Tile(1024, 512, prefetch_lhs=True),
    "gated_merge": Tile(1024, 256, row_splits=4),
    "out_proj": Tile(1024, 512, prefetch_lhs=True),
    "ffn_up": Tile(2048, 256, row_splits=2),
    "ffn_down": Tile(512, 512, prefetch_lhs=True),
}
ROW_BLOCK = {
    "rmsnorm": 512,
    "spatial_gating": 2 * CHUNK,
    "cross_attention": 512,
    "norm_proj": 512,
    "xattn_out_norm": 256,
}

_BF16 = jnp.bfloat16
_F32 = jnp.float32
_NEG_INF = float("-inf")
_NN_DIMS = (((1,), (0,)), ((), ()))
_NT_DIMS = (((1,), (1,)), ((), ()))


def _compiler_params(n_axes):
    return pltpu.CompilerParams(
        dimension_semantics=("arbitrary",) * n_axes,
        vmem_limit_bytes=VMEM_LIMIT_BYTES,
    )


def _rmsnorm_kernel(x_ref, g_ref, o_ref):
    x = x_ref[...]
    ms = jnp.mean(x * x, axis=-1, keepdims=True)
    y = x * lax.rsqrt(ms + EPS)
    o_ref[...] = (y * g_ref[...]).astype(o_ref.dtype)


def _rmsnorm(x, g, out_dtype):
    m, d = x.shape
    tm = ROW_BLOCK["rmsnorm"]
    return pl.pallas_call(
        _rmsnorm_kernel,
        grid=(m // tm,),
        in_specs=[pl.BlockSpec((tm, d), lambda i: (i, 0)),
                  pl.BlockSpec((1, d), lambda i: (0, 0))],
        out_specs=pl.BlockSpec((tm, d), lambda i: (i, 0)),
        out_shape=jax.ShapeDtypeStruct((m, d), out_dtype),
        compiler_params=_compiler_params(1),
        name="rmsnorm",
    )(x, g.reshape(1, d))


def _ep_identity(accs, extras):
    return accs[0]


def _ep_residual(accs, extras):
    return extras[0] + accs[0]


def _ep_gated_merge(accs, extras):
    y_a, y_b, gate_a, gate_b = accs
    return jax.nn.sigmoid(gate_a) * y_a + jax.nn.sigmoid(gate_b) * y_b


def _ep_swiglu(accs, extras):
    return jax.nn.silu(accs[0]) * accs[1]


class Rhs(NamedTuple):
    array: jax.Array
    layer: Optional[int] = None
    col0: int = 0
    transposed: bool = False


def _rhs_spec(r, tn):
    if r.transposed:
        k = r.array.shape[2]
        if r.col0 % tn == 0:
            return pl.BlockSpec((None, tn, k), lambda i, j: (r.layer, j + r.col0 // tn, 0))
        assert r.col0 % SUBLANES == 0
        return pl.BlockSpec((pl.Element(1), pl.Element(tn), pl.Element(k)),
                            lambda i, j: (r.layer, (r.col0 // SUBLANES + j * (tn // SUBLANES))
                                          * SUBLANES, 0))
    assert r.col0 % tn == 0
    off = r.col0 // tn
    if r.layer is None:
        return pl.BlockSpec((r.array.shape[0], tn), lambda i, j: (0, j + off))
    return pl.BlockSpec((None, r.array.shape[1], tn), lambda i, j: (r.layer, 0, j + off))


class SideJob(NamedTuple):
    body: Callable
    arrays: tuple
    in_specs: tuple
    out_specs: tuple
    out_shape: tuple


def _cast_body(x_ref, o_ref):
    o_ref[...] = x_ref[...].astype(o_ref.dtype)


def _cast_side_job(w_stack, layer, n_steps):
    _, rows, cols = w_stack.shape
    slab = rows // n_steps
    assert slab * n_steps == rows and slab % (2 * SUBLANES) == 0
    return SideJob(_cast_body, (w_stack,),
                   (pl.BlockSpec((None, slab, cols), lambda s: (layer, s, 0)),),
                   (pl.BlockSpec((slab, cols), lambda s: (s, 0)),),
                   (jax.ShapeDtypeStruct((rows, cols), _BF16),))


def _mm_kernel(*refs, n_lhs, pair_lhs, rhs_transposed, n_extra, epilogue, side_body, n_side_in,
               row_splits):
    n_rhs = len(pair_lhs)
    lhs_refs = refs[:n_lhs]
    rhs_refs = refs[n_lhs:n_lhs + n_rhs]
    n_in = n_lhs + n_rhs + n_extra
    extra_refs = refs[n_lhs + n_rhs:n_in]
    o_ref = refs[n_in + n_side_in]
    if side_body is not None:
        side_body(*refs[n_in:n_in + n_side_in], *refs[n_in + n_side_in + 1:])
    weights = [(r[0] if len(r.shape) == 3 else r[...]).astype(_BF16) for r in rhs_refs]
    rows_per_split = o_ref.shape[0] // row_splits
    for s in range(row_splits):
        rows = slice(s * rows_per_split, (s + 1) * rows_per_split)
        accs = [lax.dot_general(lhs_refs[li][rows, :], w, _NT_DIMS if tr else _NN_DIMS,
                                preferred_element_type=_F32)
                for li, w, tr in zip(pair_lhs, weights, rhs_transposed)]
        o_ref[rows, :] = epilogue(accs, [e[rows, :] for e in extra_refs]).astype(o_ref.dtype)


def _fused_matmul(name, lhs, pairs, extras, epilogue, n_out, out_dtype, side_job=None):
    tm, tn, prefetch_lhs, row_splits = TILES[name]
    m = lhs[0].shape[0]
    assert m % tm == 0 and n_out % tn == 0
    grid = (m // tm, n_out // tn)
    lhs_mode = {} if prefetch_lhs else {"pipeline_mode": pl.Buffered(1)}
    in_specs = [pl.BlockSpec((tm, a.shape[1]), lambda i, j: (i, 0), **lhs_mode) for a in lhs]
    in_specs += [_rhs_spec(r, tn) for _, r in pairs]
    in_specs += [pl.BlockSpec((tm, tn), lambda i, j: (i, j)) for _ in extras]
    out_specs = [pl.BlockSpec((tm, tn), lambda i, j: (i, j))]
    out_shape = [jax.ShapeDtypeStruct((m, n_out), out_dtype)]
    operands = list(lhs) + [p[1].array for p in pairs] + list(extras)
    side = None if side_job is None else side_job(grid[0] * grid[1])
    if side is not None:
        def per_step(spec):
            return pl.BlockSpec(spec.block_shape,
                                lambda i, j: spec.index_map(i * grid[1] + j))
        in_specs += [per_step(s) for s in side.in_specs]
        out_specs += [per_step(s) for s in side.out_specs]
        out_shape += list(side.out_shape)
        operands += list(side.arrays)
    kern = functools.partial(_mm_kernel, n_lhs=len(lhs), pair_lhs=tuple(p[0] for p in pairs),
                             rhs_transposed=tuple(p[1].transposed for p in pairs),
                             n_extra=len(extras), epilogue=epilogue,
                             side_body=None if side is None else side.body,
                             n_side_in=0 if side is None else len(side.in_specs),
                             row_splits=row_splits)
    outs = pl.pallas_call(
        kern,
        grid=grid,
        in_specs=in_specs,
        out_specs=out_specs,
        out_shape=out_shape,
        compiler_params=_compiler_params(2),
        name=name,
    )(*operands)
    return outs[0] if side is None else tuple(outs)


def _gating_kernel(za_ref, ng_ref, w_ref, bt_ref, o_ref):
    row = lax.broadcasted_iota(jnp.int32, (CHUNK, CHUNK), 0)
    col = lax.broadcasted_iota(jnp.int32, (CHUNK, CHUNK), 1)
    causal = col <= row
    for c in range(za_ref.shape[0] // CHUNK):
        rows = slice(c * CHUNK, (c + 1) * CHUNK)
        z = jax.nn.gelu(za_ref[rows, :])
        u = z[:, :A_WIDTH]
        v = z[:, A_WIDTH:]
        ms = jnp.mean(v * v, axis=-1, keepdims=True)
        vn = ((v * lax.rsqrt(ms + EPS)) * ng_ref[...]).astype(_BF16)
        for g in range(A_GROUPS):
            sl = slice(g * A_GROUP_DIM, (g + 1) * A_GROUP_DIM)
            w = jnp.where(causal, w_ref[g], 0.0).astype(_BF16)
            s = jnp.dot(w, vn[:, sl], preferred_element_type=_F32) + bt_ref[:, g:g + 1]
            o_ref[rows, sl] = (u[:, sl] * s).astype(o_ref.dtype)


def _spatial_gating(za, norm_g, w_s, b_s):
    m = za.shape[0]
    tm = ROW_BLOCK["spatial_gating"]
    return pl.pallas_call(
        _gating_kernel,
        grid=(m // tm,),
        in_specs=[pl.BlockSpec((tm, 2 * A_WIDTH), lambda i: (i, 0)),
                  pl.BlockSpec((1, A_WIDTH), lambda i: (0, 0)),
                  pl.BlockSpec((A_GROUPS, CHUNK, CHUNK), lambda i: (0, 0, 0)),
                  pl.BlockSpec((CHUNK, A_GROUPS), lambda i: (0, 0))],
        out_specs=pl.BlockSpec((tm, A_WIDTH), lambda i: (i, 0)),
        out_shape=jax.ShapeDtypeStruct((m, A_WIDTH), _BF16),
        compiler_params=_compiler_params(1),
        name="spatial_gating",
    )(za, norm_g.reshape(1, A_WIDTH), w_s, b_s.T)


def _rope_tables(positions, rot_dim, period):
    half = rot_dim // 2
    inv_freq = ROPE_THETA ** (-jnp.arange(0, rot_dim, 2, dtype=_F32) / rot_dim)
    ang = positions.astype(_F32)[..., None] * inv_freq
    cos, sin = jnp.cos(ang), jnp.sin(ang)
    zeros_half = jnp.zeros_like(sin)
    rest = jnp.zeros(sin.shape[:-1] + (period - rot_dim,), _F32)

    def lanes(*pieces):
        return jnp.tile(jnp.concatenate(pieces, axis=-1), LANES // period)

    return (lanes(cos, cos, rest + 1.0), lanes(zeros_half, sin, rest),
            lanes(-sin, zeros_half, rest))


def _rope(x, c, s1, s2, half):
    return x * c + pltpu.roll(x, half, 1) * s1 + pltpu.roll(x, LANES - half, 1) * s2


def _prep_kernel(att_ref, cb_ref, s1b_ref, s2b_ref, ci_ref, s1i_ref, s2i_ref,
                 q_ref, k_ref, vt_ref, qi_ref, ki_ref, wi_ref):
    cb, s1b, s2b = cb_ref[0], s1b_ref[0], s2b_ref[0]
    ci, s1i, s2i = ci_ref[0], s1i_ref[0], s2i_ref[0]
    for h in range(B_HEADS):
        x = att_ref[0, :, ATT_Q0 + h * LANES:ATT_Q0 + (h + 1) * LANES]
        q_ref[0, :, h * LANES:(h + 1) * LANES] = _rope(x, cb, s1b, s2b, B_ROT // 2).astype(_BF16)
    for h in range(B_KV_HEADS):
        x = att_ref[0, :, ATT_K0 + h * LANES:ATT_K0 + (h + 1) * LANES]
        k_ref[0, :, h * LANES:(h + 1) * LANES] = _rope(x, cb, s1b, s2b, B_ROT // 2).astype(_BF16)
    vt_ref[0, 0] = att_ref[0, :, ATT_V0:ATT_V0 + B_KV_WIDTH].T.astype(_BF16)
    lane = lax.broadcasted_iota(jnp.int32, ci.shape, 1)
    is_ki = lane < IDX_DIM
    heads_per_group = LANES // IDX_DIM
    for j in range(IDX_WIDTH // LANES):
        x = att_ref[0, :, ATT_QI0 + j * LANES:ATT_QI0 + (j + 1) * LANES]
        y = _rope(x, ci, s1i, s2i, IDX_ROT // 2)
        for r in range(heads_per_group):
            h = j * heads_per_group + r
            yr = y if r == 0 else pltpu.roll(y, LANES - r * IDX_DIM, 1)
            qi_ref[0, :, h * LANES:(h + 1) * LANES] = jnp.where(is_ki, yr, 0.0).astype(_BF16)
    x = att_ref[0, :, ATT_KW0:ATT_KW0 + LANES]
    y = _rope(x, jnp.where(is_ki, ci, 1.0), jnp.where(is_ki, s1i, 0.0),
              jnp.where(is_ki, s2i, 0.0), IDX_ROT // 2)
    ki_ref[0] = jnp.where(is_ki, y, 0.0).astype(_BF16)
    wi_ref[0] = y[:, IDX_DIM:IDX_DIM + IDX_HEADS]


def _prep_side_job(att, tables_b, tables_i, n_steps):
    bsz, seq, _ = att.shape
    ts = bsz * seq // n_steps
    assert ts * n_steps == bsz * seq and seq % ts == 0 and KEY_CHUNK % ts == 0
    assert ts % (2 * SUBLANES) == 0
    per_seq = seq // ts
    per_chunk = KEY_CHUNK // ts

    def tokens(width):
        return pl.BlockSpec((1, ts, width), lambda s: (s // per_seq, s % per_seq, 0))

    def out(width, dtype):
        return tokens(width), jax.ShapeDtypeStruct((bsz, seq, width), dtype)

    vt_out = (pl.BlockSpec((1, 1, B_KV_WIDTH, ts),
                           lambda s: (s // per_seq, (s % per_seq) // per_chunk, 0,
                                      (s % per_seq) % per_chunk)),
              jax.ShapeDtypeStruct((bsz, seq // KEY_CHUNK, B_KV_WIDTH, KEY_CHUNK), _BF16))
    outs = [out(B_WIDTH, _BF16), out(B_KV_WIDTH, _BF16), vt_out,
            out(IDX_HEADS * LANES, _BF16), out(LANES, _BF16), out(IDX_HEADS, _F32)]
    return SideJob(_prep_kernel, (att, *tables_b, *tables_i),
                   (tokens(ATT_WIDTH),) + (tokens(LANES),) * 6,
                   tuple(o[0] for o in outs), tuple(o[1] for o in outs))


def _ordered_int_to_float(key):
    bits = jnp.where(key < 0, key ^ jnp.int32(0x7FFFFFFF), key)
    return lax.bitcast_convert_type(bits, _F32)


def _fold_rows(x, op):
    while x.shape[0] > SUBLANES:
        half = x.shape[0] // 2
        x = op(x[:half], x[half:])
    return x


def _dsa_kernel(*refs, n_sel):
    seq = refs[4].shape[1]
    n_chunks = (pl.program_id(1) * Q_BLOCK + Q_BLOCK + KEY_CHUNK - 1) // KEY_CHUNK
    for n in range(1, seq // KEY_CHUNK + 1):
        pl.when(n_chunks == n)(functools.partial(_dsa_block, *refs, n_sel=n_sel, n_chunks=n))


def _dsa_block(qi_ref, wit_ref, ki_ref, q_ref, k_ref, vt_ref, o_ref,
               score_ref, bias_ref, logit_ref, acc_ref, *, n_sel, n_chunks):
    t = Q_BLOCK
    blk = pl.program_id(1)
    qpos = blk * t + lax.broadcasted_iota(jnp.int32, (1, t), 1)
    kiota = lax.broadcasted_iota(jnp.int32, (KEY_CHUNK, 1), 0)
    idx_scale = (IDX_DIM ** -0.5) * (IDX_HEADS ** -0.5)

    def rows(c):
        return slice(c * KEY_CHUNK, (c + 1) * KEY_CHUNK)

    def for_chunks(body, carry):
        for c in range(n_chunks):
            carry = body(c, carry)
        return carry

    def indexer_chunk(c, carry):
        ki = ki_ref[0, rows(c), :]
        acc = jnp.zeros((KEY_CHUNK, t), _F32)
        for h in range(0, IDX_HEADS, 2):
            qpair = jnp.concatenate([qi_ref[0, :, h * LANES:(h + 1) * LANES],
                                     qi_ref[0, :, (h + 1) * LANES:(h + 2) * LANES]], axis=0)
            d = lax.dot_general(ki, qpair, _NT_DIMS, preferred_element_type=_F32)
            acc = acc + wit_ref[0, h:h + 1, :] * jnp.maximum(d[:, :t], 0.0)
            acc = acc + wit_ref[0, h + 1:h + 2, :] * jnp.maximum(d[:, t:], 0.0)
        kpos = c * KEY_CHUNK + kiota
        score_ref[rows(c), :] = jnp.where(kpos <= qpos, acc * idx_scale + 0.0, _NEG_INF)
        return carry

    for_chunks(indexer_chunk, 0)

    def count(indicator):
        def chunk(c, part):
            return part + _fold_rows(indicator(score_ref[rows(c), :], c), jnp.add)
        part = for_chunks(chunk, jnp.zeros((SUBLANES, t), _F32))
        return jnp.sum(part, axis=0, keepdims=True)

    def count_ge(cand_f):
        return count(lambda s, c: jnp.where(s >= cand_f, 1.0, 0.0))

    def search():
        int_min = jnp.int32(-2 ** 31)
        base = jnp.where(count_ge(jnp.zeros((1, t), _F32)) >= n_sel, jnp.int32(0), int_min)

        def search_step(it, base):
            cand = base | lax.shift_left(jnp.int32(1), jnp.int32(30) - it)
            cnt = count_ge(_ordered_int_to_float(cand))
            return jnp.where(cnt >= n_sel, cand, base)

        return lax.fori_loop(0, 31, search_step, base)

    if (n_chunks - 1) * KEY_CHUNK >= n_sel:
        base = search()
    else:
        base = lax.cond((blk + 1) * t <= n_sel, lambda: jnp.zeros((1, t), jnp.int32), search)
    thr = jnp.where(qpos + 1 <= n_sel, _NEG_INF, _ordered_int_to_float(base))

    def bias_chunk(c, part):
        s = score_ref[rows(c), :]
        sel = jnp.where(c * KEY_CHUNK + kiota <= qpos, jnp.where(s >= thr, 1.0, 0.0), 0.0)
        bias_ref[rows(c), :] = jnp.where(sel > 0.0, 0.0, _NEG_INF)
        return part + _fold_rows(sel, jnp.add)

    n_ge = jnp.sum(for_chunks(bias_chunk, jnp.zeros((SUBLANES, t), _F32)), axis=0, keepdims=True)

    @pl.when(jnp.max(n_ge) > n_sel)
    def _():
        nxt = _ordered_int_to_float(base + 1)
        tied = n_ge > n_sel
        n_above = count(lambda s, c: jnp.where(s >= nxt, 1.0, 0.0))
        need = jnp.where(tied, n_sel - n_above, 0.0)
        no_index = float(k_ref.shape[1])
        front_v = jnp.where(tied, jnp.inf, _NEG_INF)
        front_i = jnp.where(tied, -1.0, no_index)

        def beyond(s, kposf, front_v, front_i):
            after = jnp.where(s < front_v, 1.0,
                              jnp.where(s == front_v, jnp.where(kposf > front_i, 1.0, 0.0), 0.0))
            return jnp.where(s >= thr, jnp.where(s < nxt, after, 0.0), 0.0)

        def kposf(c):
            return (c * KEY_CHUNK + kiota).astype(_F32)

        def advance(_, carry):
            front_v, front_i, need = carry

            def best_value(c, part):
                s = score_ref[rows(c), :]
                cand = jnp.where(beyond(s, kposf(c), front_v, front_i) > 0.0, s, _NEG_INF)
                return jnp.maximum(part, _fold_rows(cand, jnp.maximum))

            v = jnp.max(for_chunks(best_value, jnp.full((SUBLANES, t), _NEG_INF, _F32)),
                        axis=0, keepdims=True)

            def first_index(c, part):
                s = score_ref[rows(c), :]
                hit = jnp.where(s == v, beyond(s, kposf(c), front_v, front_i), 0.0)
                cand = jnp.where(hit > 0.0, kposf(c), no_index)
                return jnp.minimum(part, _fold_rows(cand, jnp.minimum))

            i = jnp.min(for_chunks(first_index, jnp.full((SUBLANES, t), no_index, _F32)),
                        axis=0, keepdims=True)
            active = need > 0.0
            return (jnp.where(active, v, front_v), jnp.where(active, i, front_i),
                    jnp.where(active, need - 1.0, need))

        front_v, front_i, _ = lax.fori_loop(0, jnp.max(need).astype(jnp.int32), advance,
                                            (front_v, front_i, need))

        def tie_chunk(c, carry):
            s = score_ref[rows(c), :]
            upto = jnp.where(s > front_v, 0.0,
                             jnp.where(s == front_v,
                                       jnp.where(kposf(c) <= front_i, 0.0, _NEG_INF), _NEG_INF))
            keep = jnp.where(s >= nxt, 0.0, jnp.where(s >= thr, upto, _NEG_INF))
            bias_ref[rows(c), :] = jnp.where(c * KEY_CHUNK + kiota <= qpos, keep, _NEG_INF)
            return carry

        for_chunks(tie_chunk, 0)

    grp = B_HEADS // B_KV_HEADS
    gw = grp * t
    exp2_scale = (B_HEAD_DIM ** -0.5) * 1.4426950408889634

    def logit_chunk(c, mx):
        bias = jnp.concatenate([bias_ref[rows(c), :]] * grp, axis=1)
        folded = []
        for g in range(B_KV_HEADS):
            qg = jnp.concatenate(
                [q_ref[0, :, (g * grp + hh) * B_HEAD_DIM:(g * grp + hh + 1) * B_HEAD_DIM]
                 for hh in range(grp)], axis=0)
            l = lax.dot_general(k_ref[0, rows(c), g * B_HEAD_DIM:(g + 1) * B_HEAD_DIM], qg,
                                _NT_DIMS, preferred_element_type=_F32) + bias
            logit_ref[rows(c), g * gw:(g + 1) * gw] = l
            folded.append(_fold_rows(l, jnp.maximum))
        return jnp.maximum(mx, jnp.concatenate(folded, axis=1))

    mx = for_chunks(logit_chunk, jnp.full((SUBLANES, B_HEADS * t), _NEG_INF, _F32))
    mx = jnp.max(mx, axis=0, keepdims=True)

    acc_ref[...] = jnp.zeros_like(acc_ref)
    ones_rows = jnp.ones((acc_ref.shape[0] - B_HEAD_DIM, KEY_CHUNK), _BF16)

    def pv_chunk(c, carry):
        for g in range(B_KV_HEADS):
            cols = slice(g * gw, (g + 1) * gw)
            p = jnp.exp2((logit_ref[rows(c), cols] - mx[:, cols]) * exp2_scale)
            vt_ext = jnp.concatenate(
                [vt_ref[0, c, g * B_HEAD_DIM:(g + 1) * B_HEAD_DIM, :], ones_rows], axis=0)
            acc_ref[:, cols] += jnp.dot(vt_ext, p.astype(_BF16), preferred_element_type=_F32)
        return carry

    for_chunks(pv_chunk, 0)
    out_t = acc_ref[:B_HEAD_DIM, :] / acc_ref[B_HEAD_DIM:B_HEAD_DIM + 1, :]
    for h in range(B_HEADS):
        o_ref[0, :, h * B_HEAD_DIM:(h + 1) * B_HEAD_DIM] = (
            out_t[:, h * t:(h + 1) * t].T.astype(o_ref.dtype))


def _dsa_attention(q, k, vt, qi, ki, wit):
    bsz, seq, _ = q.shape
    assert Q_BLOCK == LANES and seq % KEY_CHUNK == 0
    n_sel = min(TOPK_MAX, seq // 4)

    def q_spec(width):
        return pl.BlockSpec((1, Q_BLOCK, width), lambda b, i: (b, i, 0))

    def kv_spec(width):
        return pl.BlockSpec((1, seq, width), lambda b, i: (b, 0, 0))

    return pl.pallas_call(
        functools.partial(_dsa_kernel, n_sel=n_sel),
        grid=(bsz, seq // Q_BLOCK),
        in_specs=[q_spec(IDX_HEADS * LANES),
                  pl.BlockSpec((1, IDX_HEADS, Q_BLOCK), lambda b, i: (b, 0, i)),
                  kv_spec(LANES), q_spec(B_WIDTH), kv_spec(B_KV_WIDTH),
                  pl.BlockSpec((1, seq // KEY_CHUNK, B_KV_WIDTH, KEY_CHUNK),
                               lambda b, i: (b, 0, 0, 0))],
        out_specs=q_spec(B_WIDTH),
        out_shape=jax.ShapeDtypeStruct((bsz, seq, B_WIDTH), _BF16),
        scratch_shapes=[pltpu.VMEM((seq, Q_BLOCK), _F32), pltpu.VMEM((seq, Q_BLOCK), _F32),
                        pltpu.VMEM((seq, B_HEADS * Q_BLOCK), _F32),
                        pltpu.VMEM((B_HEAD_DIM + 2 * SUBLANES, B_HEADS * Q_BLOCK), _F32)],
        compiler_params=_compiler_params(2),
        name="dsa_attention",
    )(qi, wit, ki, q, k, vt)


def _xattn_kernel(q_ref, k_ref, v_ref, o_ref):
    scale = X_HEAD_DIM ** -0.5
    for h in range(X_HEADS):
        hs = slice(h * X_HEAD_DIM, (h + 1) * X_HEAD_DIM)
        logits = lax.dot_general(q_ref[0, :, hs], k_ref[0, :, hs], _NT_DIMS,
                                 preferred_element_type=_F32) * scale
        mx = jnp.max(logits, axis=-1, keepdims=True)
        p = jnp.exp(logits - mx)
        denom = jnp.sum(p, axis=-1, keepdims=True)
        o = jnp.dot(p.astype(_BF16), v_ref[0, :, hs], preferred_element_type=_F32)
        o_ref[0, :, hs] = (o / denom).astype(o_ref.dtype)


def _cross_attention(q, k, v):
    bsz, seq, _ = q.shape
    tq = ROW_BLOCK["cross_attention"]
    m = k.shape[1]
    mem_spec = pl.BlockSpec((1, m, X_WIDTH), lambda b, i: (b, 0, 0))
    return pl.pallas_call(
        _xattn_kernel,
        grid=(bsz, seq // tq),
        in_specs=[pl.BlockSpec((1, tq, X_WIDTH), lambda b, i: (b, i, 0)), mem_spec, mem_spec],
        out_specs=pl.BlockSpec((1, tq, X_WIDTH), lambda b, i: (b, i, 0)),
        out_shape=jax.ShapeDtypeStruct((bsz, seq, X_WIDTH), _BF16),
        compiler_params=_compiler_params(2),
        name="cross_attention",
    )(q, k, v)


def _norm_proj_kernel(x_ref, g_ref, w_ref, o_ref, w_bf16_ref):
    @pl.when(pl.program_id(0) == 0)
    def _():
        w_bf16_ref[...] = w_ref[...].astype(_BF16)

    x = x_ref[...]
    ms = jnp.mean(x * x, axis=-1, keepdims=True)
    h = ((x * lax.rsqrt(ms + EPS)) * g_ref[...]).astype(_BF16)
    o_ref[...] = jnp.dot(h, w_bf16_ref[...], preferred_element_type=_F32).astype(o_ref.dtype)


def _norm_proj(x, g, w_stack, layer, out_dtype, name):
    m, d = x.shape
    tm = ROW_BLOCK["norm_proj"]
    n = w_stack.shape[2]
    return pl.pallas_call(
        _norm_proj_kernel,
        grid=(m // tm,),
        in_specs=[pl.BlockSpec((tm, d), lambda i: (i, 0)),
                  pl.BlockSpec((1, d), lambda i: (0, 0)),
                  pl.BlockSpec((None, d, n), lambda i: (layer, 0, 0),
                               pipeline_mode=pl.Buffered(1))],
        out_specs=pl.BlockSpec((tm, n), lambda i: (i, 0)),
        out_shape=jax.ShapeDtypeStruct((m, n), out_dtype),
        scratch_shapes=[pltpu.VMEM((d, n), _BF16)],
        compiler_params=_compiler_params(1),
        name=name,
    )(x, g.reshape(1, d), w_stack)


def _xattn_out_kernel(o_ref, w_ref, x_ref, g_ref, x_out_ref, h_out_ref, w_bf16_ref):
    @pl.when(pl.program_id(0) == 0)
    def _():
        w_bf16_ref[...] = w_ref[...].astype(_BF16)

    x = x_ref[...] + jnp.dot(o_ref[...], w_bf16_ref[...], preferred_element_type=_F32)
    x_out_ref[...] = x
    ms = jnp.mean(x * x, axis=-1, keepdims=True)
    h_out_ref[...] = ((x * lax.rsqrt(ms + EPS)) * g_ref[...]).astype(h_out_ref.dtype)


def _xattn_out_and_norm(o, w_stack, layer, x, g):
    m, k = o.shape
    tm = ROW_BLOCK["xattn_out_norm"]
    d = x.shape[1]
    row_spec = pl.BlockSpec((tm, d), lambda i: (i, 0))
    return pl.pallas_call(
        _xattn_out_kernel,
        grid=(m // tm,),
        in_specs=[pl.BlockSpec((tm, k), lambda i: (i, 0)),
                  pl.BlockSpec((None, k, d), lambda i: (layer, 0, 0),
                               pipeline_mode=pl.Buffered(1)),
                  row_spec,
                  pl.BlockSpec((1, d), lambda i: (0, 0))],
        out_specs=[row_spec, row_spec],
        out_shape=[jax.ShapeDtypeStruct((m, d), _F32), jax.ShapeDtypeStruct((m, d), _BF16)],
        scratch_shapes=[pltpu.VMEM((k, d), _BF16)],
        compiler_params=_compiler_params(1),
        name="xattn_out_norm",
    )(o, w_stack, x, g.reshape(1, d))


def kernel(x, mem, positions, norm_mix_g, w_in, a_norm_g, a_spatial_w, a_spatial_b, p_a, p_b,
           w_out, norm_x_g, norm_mem_g, xq_w, xk_w, xv_w, xo_w, norm_ffn_g, ffn_w1, ffn_w3,
           ffn_w2, final_norm_g):
    bsz, seq, d = x.shape
    m = bsz * seq
    depth = w_in.shape[0]
    ffn_hidden = ffn_w1.shape[-1]
    tables_b = _rope_tables(positions, B_ROT, B_HEAD_DIM)
    tables_i = _rope_tables(positions, IDX_ROT, IDX_DIM)

    xf = x.reshape(m, d)
    mem_f = mem.reshape(bsz * MEM_LEN, d)
    w_in_t = jnp.transpose(w_in, (0, 2, 1))
    for l in range(depth):
        def w_in_cols(col0):
            return Rhs(w_in_t, l, col0, transposed=True)

        h = _rmsnorm(xf, norm_mix_g[l], _BF16)
        att = _fused_matmul("proj_att", [h], [(0, w_in_cols(ZA_END))], [], _ep_identity,
                            ATT_WIDTH, _F32)
        za, q, k, vt, qi, ki, wi = _fused_matmul(
            "proj_za", [h], [(0, w_in_cols(0))], [], _ep_identity, ZA_END, _F32,
            side_job=functools.partial(_prep_side_job, att.reshape(bsz, seq, ATT_WIDTH),
                                       tables_b, tables_i))
        y_a = _spatial_gating(za, a_norm_g[l], a_spatial_w[l], a_spatial_b[l])
        y_b = _dsa_attention(q, k, vt, qi, ki, wi.transpose(0, 2, 1)).reshape(m, B_WIDTH)
        merged = _fused_matmul(
            "gated_merge", [y_a, y_b, h],
            [(0, Rhs(p_a, l)), (1, Rhs(p_b, l)), (2, w_in_cols(ATT_END)),
             (2, w_in_cols(ATT_END + d))],
            [], _ep_gated_merge, d, _BF16)
        x1 = _fused_matmul("out_proj", [merged], [(0, Rhs(w_out, l))], [xf], _ep_residual,
                           d, _F32)

        qx = _norm_proj(x1, norm_x_g[l], xq_w, l, _BF16, "xattn_q")
        kx = _norm_proj(mem_f, norm_mem_g[l], xk_w, l, _BF16, "xattn_k")
        vx = _norm_proj(mem_f, norm_mem_g[l], xv_w, l, _BF16, "xattn_v")
        ox = _cross_attention(qx.reshape(bsz, seq, X_WIDTH),
                              kx.reshape(bsz, MEM_LEN, X_WIDTH),
                              vx.reshape(bsz, MEM_LEN, X_WIDTH)).reshape(m, X_WIDTH)
        x2, h2 = _xattn_out_and_norm(ox, xo_w, l, x1, norm_ffn_g[l])

        act, w2_bf16 = _fused_matmul("ffn_up", [h2],
                                     [(0, Rhs(ffn_w1, l)), (0, Rhs(ffn_w3, l))],
                                     [], _ep_swiglu, ffn_hidden, _BF16,
                                     side_job=functools.partial(_cast_side_job, ffn_w2, l))
        xf = _fused_matmul("ffn_down", [act], [(0, Rhs(w2_bf16))], [x2], _ep_residual, d, _F32)

    return _rmsnorm(xf, final_norm_g, _F32).reshape(bsz, seq, d)
```

```python
import functools
from typing import Callable, NamedTuple, Optional

import jax
import jax.numpy as jnp
from jax import lax
from jax.experimental import pallas as pl
from jax.experimental.pallas import tpu as pltpu

D_MODEL = 4096
MEM_LEN = 256
EPS = 1e-6
ROPE_THETA = 500000.0
CHUNK = 128
A_GROUPS = 16
A_WIDTH = D_MODEL // 2
A_GROUP_DIM = A_WIDTH // A_GROUPS
B_HEADS = 16
B_HEAD_DIM = 128
B_KV_HEADS = 4
B_WIDTH = B_HEADS * B_HEAD_DIM
B_KV_WIDTH = B_KV_HEADS * B_HEAD_DIM
B_ROT = B_HEAD_DIM // 4
IDX_HEADS = 16
IDX_DIM = 64
IDX_WIDTH = IDX_HEADS * IDX_DIM
IDX_ROT = IDX_DIM // 4
TOPK_MAX = 256
Q_BLOCK = 128
X_HEADS = 4
X_HEAD_DIM = 256
X_WIDTH = X_HEADS * X_HEAD_DIM
IN_SIZES = (2 * A_WIDTH, B_WIDTH, B_KV_WIDTH, B_KV_WIDTH, IDX_WIDTH, IDX_DIM, IDX_HEADS,
            2 * D_MODEL)

LANES = 128
SUBLANES = 8
KEY_CHUNK = 512
PROJ_TN = 512
ZA_END = IN_SIZES[0]
ATT_END = ZA_END + sum(IN_SIZES[1:7])
ATT_Q0 = 0
ATT_K0 = ATT_Q0 + B_WIDTH
ATT_V0 = ATT_K0 + B_KV_WIDTH
ATT_QI0 = ATT_V0 + B_KV_WIDTH
ATT_KW0 = ATT_QI0 + IDX_WIDTH
ATT_WIDTH = -(-(ATT_END - ZA_END) // PROJ_TN) * PROJ_TN
assert ZA_END % PROJ_TN == 0 and ATT_KW0 + LANES <= ATT_WIDTH

VMEM_LIMIT_BYTES = 60 * 1024 * 1024


class Tile(NamedTuple):
    tm: int
    tn: int
    prefetch_lhs: bool = False
    row_splits: int = 1


TILES = {
    "proj_att": Tile(1024, PROJ_TN, prefetch_lhs=True, row_splits=2),
    "proj_za": Tile(1024, 512, prefetch_lhs=True),
    "gated_merge": Tile(1024, 256, row_splits=8),
    "out_proj": Tile(1024, 512, prefetch_lhs=True, row_splits=2),
    "ffn_up": Tile(2048, 256, row_splits=4),
    "ffn_down": Tile(512, 512, prefetch_lhs=True, row_splits=2),
}
ROW_BLOCK = {
    "rmsnorm": 512,
    "spatial_gating": 2 * CHUNK,
    "cross_attention": 512,
    "norm_proj": 512,
    "xattn_out_norm": 256,
}

_BF16 = jnp.bfloat16
_F32 = jnp.float32
_NEG_INF = float("-inf")
_NN_DIMS = (((1,), (0,)), ((), ()))
_NT_DIMS = (((1,), (1,)), ((), ()))


def _compiler_params(n_axes):
    return pltpu.CompilerParams(
        dimension_semantics=("arbitrary",) * n_axes,
        vmem_limit_bytes=VMEM_LIMIT_BYTES,
    )


def _rmsnorm_kernel(x_ref, g_ref, o_ref):
    x = x_ref[...]
    ms = jnp.mean(x * x, axis=-1, keepdims=True)
    y = x * lax.rsqrt(ms + EPS)
    o_ref[...] = (y * g_ref[...]).astype(o_ref.dtype)


def _rmsnorm(x, g, out_dtype):
    m, d = x.shape
    tm = ROW_BLOCK["rmsnorm"]
    return pl.pallas_call(
        _rmsnorm_kernel,
        grid=(m // tm,),
        in_specs=[pl.BlockSpec((tm, d), lambda i: (i, 0)),
                  pl.BlockSpec((1, d), lambda i: (0, 0))],
        out_specs=pl.BlockSpec((tm, d), lambda i: (i, 0)),
        out_shape=jax.ShapeDtypeStruct((m, d), out_dtype),
        compiler_params=_compiler_params(1),
        name="rmsnorm",
    )(x, g.reshape(1, d))


def _ep_identity(accs, extras):
    return accs[0]


def _ep_residual(accs, extras):
    return extras[0] + accs[0]


def _ep_gated_merge(accs, extras):
    y_a, y_b, gate_a, gate_b = accs
    return jax.nn.sigmoid(gate_a) * y_a + jax.nn.sigmoid(gate_b) * y_b


def _ep_swiglu(accs, extras):
    return jax.nn.silu(accs[0]) * accs[1]


class Rhs(NamedTuple):
    array: jax.Array
    layer: Optional[int] = None
    col0: int = 0
    transposed: bool = False


def _rhs_spec(r, tn):
    if r.transposed:
        k = r.array.shape[2]
        if r.col0 % tn == 0:
            return pl.BlockSpec((None, tn, k), lambda i, j: (r.layer, j + r.col0 // tn, 0))
        assert r.col0 % SUBLANES == 0
        return pl.BlockSpec((pl.Element(1), pl.Element(tn), pl.Element(k)),
                            lambda i, j: (r.layer, (r.col0 // SUBLANES + j * (tn // SUBLANES))
                                          * SUBLANES, 0))
    assert r.col0 % tn == 0
    off = r.col0 // tn
    if r.layer is None:
        return pl.BlockSpec((r.array.shape[0], tn), lambda i, j: (0, j + off))
    return pl.BlockSpec((None, r.array.shape[1], tn), lambda i, j: (r.layer, 0, j + off))


class SideJob(NamedTuple):
    body: Callable
    arrays: tuple
    in_specs: tuple
    out_specs: tuple
    out_shape: tuple


def _cast_body(x_ref, o_ref):
    o_ref[...] = x_ref[...].astype(o_ref.dtype)


def _cast_side_job(w_stack, layer, n_steps):
    _, rows, cols = w_stack.shape
    slab = rows // n_steps
    assert slab * n_steps == rows and slab % (2 * SUBLANES) == 0
    return SideJob(_cast_body, (w_stack,),
                   (pl.BlockSpec((None, slab, cols), lambda s: (layer, s, 0)),),
                   (pl.BlockSpec((slab, cols), lambda s: (s, 0)),),
                   (jax.ShapeDtypeStruct((rows, cols), _BF16),))


def _mm_kernel(*refs, n_lhs, pair_lhs, rhs_transposed, n_extra, epilogue, side_body, n_side_in,
               row_splits):
    n_rhs = len(pair_lhs)
    lhs_refs = refs[:n_lhs]
    rhs_refs = refs[n_lhs:n_lhs + n_rhs]
    n_in = n_lhs + n_rhs + n_extra
    extra_refs = refs[n_lhs + n_rhs:n_in]
    o_ref = refs[n_in + n_side_in]
    if side_body is not None:
        side_body(*refs[n_in:n_in + n_side_in], *refs[n_in + n_side_in + 1:])
    weights = [(r[0] if len(r.shape) == 3 else r[...]).astype(_BF16) for r in rhs_refs]
    rows_per_split = o_ref.shape[0] // row_splits
    for s in range(row_splits):
        rows = slice(s * rows_per_split, (s + 1) * rows_per_split)
        accs = [lax.dot_general(lhs_refs[li][rows, :], w, _NT_DIMS if tr else _NN_DIMS,
                                preferred_element_type=_F32)
                for li, w, tr in zip(pair_lhs, weights, rhs_transposed)]
        o_ref[rows, :] = epilogue(accs, [e[rows, :] for e in extra_refs]).astype(o_ref.dtype)


def _fused_matmul(name, lhs, pairs, extras, epilogue, n_out, out_dtype, side_job=None):
    tm, tn, prefetch_lhs, row_splits = TILES[name]
    m = lhs[0].shape[0]
    assert m % tm == 0 and n_out % tn == 0
    grid = (m // tm, n_out // tn)
    lhs_mode = {} if prefetch_lhs else {"pipeline_mode": pl.Buffered(1)}
    in_specs = [pl.BlockSpec((tm, a.shape[1]), lambda i, j: (i, 0), **lhs_mode) for a in lhs]
    in_specs += [_rhs_spec(r, tn) for _, r in pairs]
    in_specs += [pl.BlockSpec((tm, tn), lambda i, j: (i, j)) for _ in extras]
    out_specs = [pl.BlockSpec((tm, tn), lambda i, j: (i, j))]
    out_shape = [jax.ShapeDtypeStruct((m, n_out), out_dtype)]
    operands = list(lhs) + [p[1].array for p in pairs] + list(extras)
    side = None if side_job is None else side_job(grid[0] * grid[1])
    if side is not None:
        def per_step(spec):
            return pl.BlockSpec(spec.block_shape,
                                lambda i, j: spec.index_map(i * grid[1] + j))
        in_specs += [per_step(s) for s in side.in_specs]
        out_specs += [per_step(s) for s in side.out_specs]
        out_shape += list(side.out_shape)
        operands += list(side.arrays)
    kern = functools.partial(_mm_kernel, n_lhs=len(lhs), pair_lhs=tuple(p[0] for p in pairs),
                             rhs_transposed=tuple(p[1].transposed for p in pairs),
                             n_extra=len(extras), epilogue=epilogue,
                             side_body=None if side is None else side.body,
                             n_side_in=0 if side is None else len(side.in_specs),
                             row_splits=row_splits)
    outs = pl.pallas_call(
        kern,
        grid=grid,
        in_specs=in_specs,
        out_specs=out_specs,
        out_shape=out_shape,
        compiler_params=_compiler_params(2),
        name=name,
    )(*operands)
    return outs[0] if side is None else tuple(outs)


def _gating_kernel(za_ref, ng_ref, w_ref, bt_ref, o_ref):
    row = lax.broadcasted_iota(jnp.int32, (CHUNK, CHUNK), 0)
    col = lax.broadcasted_iota(jnp.int32, (CHUNK, CHUNK), 1)
    causal = col <= row
    for c in range(za_ref.shape[0] // CHUNK):
        rows = slice(c * CHUNK, (c + 1) * CHUNK)
        z = jax.nn.gelu(za_ref[rows, :])
        u = z[:, :A_WIDTH]
        v = z[:, A_WIDTH:]
        ms = jnp.mean(v * v, axis=-1, keepdims=True)
        vn = ((v * lax.rsqrt(ms + EPS)) * ng_ref[...]).astype(_BF16)
        for g in range(A_GROUPS):
            sl = slice(g * A_GROUP_DIM, (g + 1) * A_GROUP_DIM)
            w = jnp.where(causal, w_ref[g], 0.0).astype(_BF16)
            s = jnp.dot(w, vn[:, sl], preferred_element_type=_F32) + bt_ref[:, g:g + 1]
            o_ref[rows, sl] = (u[:, sl] * s).astype(o_ref.dtype)


def _spatial_gating(za, norm_g, w_s, b_s):
    m = za.shape[0]
    tm = ROW_BLOCK["spatial_gating"]
    return pl.pallas_call(
        _gating_kernel,
        grid=(m // tm,),
        in_specs=[pl.BlockSpec((tm, 2 * A_WIDTH), lambda i: (i, 0)),
                  pl.BlockSpec((1, A_WIDTH), lambda i: (0, 0)),
                  pl.BlockSpec((A_GROUPS, CHUNK, CHUNK), lambda i: (0, 0, 0)),
                  pl.BlockSpec((CHUNK, A_GROUPS), lambda i: (0, 0))],
        out_specs=pl.BlockSpec((tm, A_WIDTH), lambda i: (i, 0)),
        out_shape=jax.ShapeDtypeStruct((m, A_WIDTH), _BF16),
        compiler_params=_compiler_params(1),
        name="spatial_gating",
    )(za, norm_g.reshape(1, A_WIDTH), w_s, b_s.T)


def _rope_tables(positions, rot_dim, period):
    half = rot_dim // 2
    inv_freq = ROPE_THETA ** (-jnp.arange(0, rot_dim, 2, dtype=_F32) / rot_dim)
    ang = positions.astype(_F32)[..., None] * inv_freq
    cos, sin = jnp.cos(ang), jnp.sin(ang)
    zeros_half = jnp.zeros_like(sin)
    rest = jnp.zeros(sin.shape[:-1] + (period - rot_dim,), _F32)

    def lanes(*pieces):
        return jnp.tile(jnp.concatenate(pieces, axis=-1), LANES // period)

    return (lanes(cos, cos, rest + 1.0), lanes(zeros_half, sin, rest),
            lanes(-sin, zeros_half, rest))


def _rope(x, c, s1, s2, half):
    return x * c + pltpu.roll(x, half, 1) * s1 + pltpu.roll(x, LANES - half, 1) * s2


def _prep_kernel(att_ref, cb_ref, s1b_ref, s2b_ref, ci_ref, s1i_ref, s2i_ref,
                 q_ref, k_ref, vt_ref, qi_ref, ki_ref, wi_ref):
    cb, s1b, s2b = cb_ref[0], s1b_ref[0], s2b_ref[0]
    ci, s1i, s2i = ci_ref[0], s1i_ref[0], s2i_ref[0]
    for h in range(B_HEADS):
        x = att_ref[0, :, ATT_Q0 + h * LANES:ATT_Q0 + (h + 1) * LANES]
        q_ref[0, :, h * LANES:(h + 1) * LANES] = _rope(x, cb, s1b, s2b, B_ROT // 2).astype(_BF16)
    for h in range(B_KV_HEADS):
        x = att_ref[0, :, ATT_K0 + h * LANES:ATT_K0 + (h + 1) * LANES]
        k_ref[0, :, h * LANES:(h + 1) * LANES] = _rope(x, cb, s1b, s2b, B_ROT // 2).astype(_BF16)
    vt_ref[0, 0] = att_ref[0, :, ATT_V0:ATT_V0 + B_KV_WIDTH].T.astype(_BF16)
    lane = lax.broadcasted_iota(jnp.int32, ci.shape, 1)
    is_ki = lane < IDX_DIM
    heads_per_group = LANES // IDX_DIM
    for j in range(IDX_WIDTH // LANES):
        x = att_ref[0, :, ATT_QI0 + j * LANES:ATT_QI0 + (j + 1) * LANES]
        y = _rope(x, ci, s1i, s2i, IDX_ROT // 2)
        for r in range(heads_per_group):
            h = j * heads_per_group + r
            yr = y if r == 0 else pltpu.roll(y, LANES - r * IDX_DIM, 1)
            qi_ref[0, :, h * LANES:(h + 1) * LANES] = jnp.where(is_ki, yr, 0.0).astype(_BF16)
    x = att_ref[0, :, ATT_KW0:ATT_KW0 + LANES]
    y = _rope(x, jnp.where(is_ki, ci, 1.0), jnp.where(is_ki, s1i, 0.0),
              jnp.where(is_ki, s2i, 0.0), IDX_ROT // 2)
    ki_ref[0] = jnp.where(is_ki, y, 0.0).astype(_BF16)
    wi_ref[0] = y[:, IDX_DIM:IDX_DIM + IDX_HEADS]


def _prep_side_job(att, tables_b, tables_i, n_steps):
    bsz, seq, _ = att.shape
    ts = bsz * seq // n_steps
    assert ts * n_steps == bsz * seq and seq % ts == 0 and KEY_CHUNK % ts == 0
    assert ts % (2 * SUBLANES) == 0
    per_seq = seq // ts
    per_chunk = KEY_CHUNK // ts

    def tokens(width):
        return pl.BlockSpec((1, ts, width), lambda s: (s // per_seq, s % per_seq, 0))

    def out(width, dtype):
        return tokens(width), jax.ShapeDtypeStruct((bsz, seq, width), dtype)

    vt_out = (pl.BlockSpec((1, 1, B_KV_WIDTH, ts),
                           lambda s: (s // per_seq, (s % per_seq) // per_chunk, 0,
                                      (s % per_seq) % per_chunk)),
              jax.ShapeDtypeStruct((bsz, seq // KEY_CHUNK, B_KV_WIDTH, KEY_CHUNK), _BF16))
    outs = [out(B_WIDTH, _BF16), out(B_KV_WIDTH, _BF16), vt_out,
            out(IDX_HEADS * LANES, _BF16), out(LANES, _BF16), out(IDX_HEADS, _F32)]
    return SideJob(_prep_kernel, (att, *tables_b, *tables_i),
                   (tokens(ATT_WIDTH),) + (tokens(LANES),) * 6,
                   tuple(o[0] for o in outs), tuple(o[1] for o in outs))


def _ordered_int_to_float(key):
    bits = jnp.where(key < 0, key ^ jnp.int32(0x7FFFFFFF), key)
    return lax.bitcast_convert_type(bits, _F32)


def _fold_rows(x, op):
    while x.shape[0] > SUBLANES:
        half = x.shape[0] // 2
        x = op(x[:half], x[half:])
    return x


def _dsa_kernel(*refs, n_sel):
    seq = refs[4].shape[1]
    n_chunks = (pl.program_id(1) * Q_BLOCK + Q_BLOCK + KEY_CHUNK - 1) // KEY_CHUNK
    for n in range(1, seq // KEY_CHUNK + 1):
        pl.when(n_chunks == n)(functools.partial(_dsa_block, *refs, n_sel=n_sel, n_chunks=n))


def _dsa_block(qi_ref, wit_ref, ki_ref, q_ref, k_ref, vt_ref, o_ref,
               score_ref, bias_ref, logit_ref, acc_ref, *, n_sel, n_chunks):
    t = Q_BLOCK
    blk = pl.program_id(1)
    qpos = blk * t + lax.broadcasted_iota(jnp.int32, (1, t), 1)
    kiota = lax.broadcasted_iota(jnp.int32, (KEY_CHUNK, 1), 0)
    idx_scale = (IDX_DIM ** -0.5) * (IDX_HEADS ** -0.5)

    def rows(c):
        return slice(c * KEY_CHUNK, (c + 1) * KEY_CHUNK)

    def for_chunks(body, carry):
        for c in range(n_chunks):
            carry = body(c, carry)
        return carry

    def indexer_chunk(c, carry):
        ki = ki_ref[0, rows(c), :]
        acc = jnp.zeros((KEY_CHUNK, t), _F32)
        for h in range(0, IDX_HEADS, 2):
            qpair = jnp.concatenate([qi_ref[0, :, h * LANES:(h + 1) * LANES],
                                     qi_ref[0, :, (h + 1) * LANES:(h + 2) * LANES]], axis=0)
            d = lax.dot_general(ki, qpair, _NT_DIMS, preferred_element_type=_F32)
            acc = acc + wit_ref[0, h:h + 1, :] * jnp.maximum(d[:, :t], 0.0)
            acc = acc + wit_ref[0, h + 1:h + 2, :] * jnp.maximum(d[:, t:], 0.0)
        kpos = c * KEY_CHUNK + kiota
        score_ref[rows(c), :] = jnp.where(kpos <= qpos, acc * idx_scale + 0.0, _NEG_INF)
        return carry

    for_chunks(indexer_chunk, 0)

    def count(indicator):
        def chunk(c, part):
            return part + _fold_rows(indicator(score_ref[rows(c), :], c), jnp.add)
        part = for_chunks(chunk, jnp.zeros((SUBLANES, t), _F32))
        return jnp.sum(part, axis=0, keepdims=True)

    def count_ge(cand_f):
        return count(lambda s, c: jnp.where(s >= cand_f, 1.0, 0.0))

    def search():
        int_min = jnp.int32(-2 ** 31)
        base = jnp.where(count_ge(jnp.zeros((1, t), _F32)) >= n_sel, jnp.int32(0), int_min)

        def search_step(it, base):
            cand = base | lax.shift_left(jnp.int32(1), jnp.int32(30) - it)
            cnt = count_ge(_ordered_int_to_float(cand))
            return jnp.where(cnt >= n_sel, cand, base)

        return lax.fori_loop(0, 31, search_step, base)

    if (n_chunks - 1) * KEY_CHUNK >= n_sel:
        base = search()
    else:
        base = lax.cond((blk + 1) * t <= n_sel, lambda: jnp.zeros((1, t), jnp.int32), search)
    thr = jnp.where(qpos + 1 <= n_sel, _NEG_INF, _ordered_int_to_float(base))

    def bias_chunk(c, part):
        s = score_ref[rows(c), :]
        sel = jnp.where(c * KEY_CHUNK + kiota <= qpos, jnp.where(s >= thr, 1.0, 0.0), 0.0)
        bias_ref[rows(c), :] = jnp.where(sel > 0.0, 0.0, _NEG_INF)
        return part + _fold_rows(sel, jnp.add)

    n_ge = jnp.sum(for_chunks(bias_chunk, jnp.zeros((SUBLANES, t), _F32)), axis=0, keepdims=True)

    @pl.when(jnp.max(n_ge) > n_sel)
    def _():
        nxt = _ordered_int_to_float(base + 1)
        tied = n_ge > n_sel
        n_above = count(lambda s, c: jnp.where(s >= nxt, 1.0, 0.0))
        need = jnp.where(tied, n_sel - n_above, 0.0)
        no_index = float(k_ref.shape[1])
        front_v = jnp.where(tied, jnp.inf, _NEG_INF)
        front_i = jnp.where(tied, -1.0, no_index)

        def beyond(s, kposf, front_v, front_i):
            after = jnp.where(s < front_v, 1.0,
                              jnp.where(s == front_v, jnp.where(kposf > front_i, 1.0, 0.0), 0.0))
            return jnp.where(s >= thr, jnp.where(s < nxt, after, 0.0), 0.0)

        def kposf(c):
            return (c * KEY_CHUNK + kiota).astype(_F32)

        def advance(_, carry):
            front_v, front_i, need = carry

            def best_value(c, part):
                s = score_ref[rows(c), :]
                cand = jnp.where(beyond(s, kposf(c), front_v, front_i) > 0.0, s, _NEG_INF)
                return jnp.maximum(part, _fold_rows(cand, jnp.maximum))

            v = jnp.max(for_chunks(best_value, jnp.full((SUBLANES, t), _NEG_INF, _F32)),
                        axis=0, keepdims=True)

            def first_index(c, part):
                s = score_ref[rows(c), :]
                hit = jnp.where(s == v, beyond(s, kposf(c), front_v, front_i), 0.0)
                cand = jnp.where(hit > 0.0, kposf(c), no_index)
                return jnp.minimum(part, _fold_rows(cand, jnp.minimum))

            i = jnp.min(for_chunks(first_index, jnp.full((SUBLANES, t), no_index, _F32)),
                        axis=0, keepdims=True)
            active = need > 0.0
            return (jnp.where(active, v, front_v), jnp.where(active, i, front_i),
                    jnp.where(active, need - 1.0, need))

        front_v, front_i, _ = lax.fori_loop(0, jnp.max(need).astype(jnp.int32), advance,
                                            (front_v, front_i, need))

        def tie_chunk(c, carry):
            s = score_ref[rows(c), :]
            upto = jnp.where(s > front_v, 0.0,
                             jnp.where(s == front_v,
                                       jnp.where(kposf(c) <= front_i, 0.0, _NEG_INF), _NEG_INF))
            keep = jnp.where(s >= nxt, 0.0, jnp.where(s >= thr, upto, _NEG_INF))
            bias_ref[rows(c), :] = jnp.where(c * KEY_CHUNK + kiota <= qpos, keep, _NEG_INF)
            return carry

        for_chunks(tie_chunk, 0)

    grp = B_HEADS // B_KV_HEADS
    gw = grp * t
    exp2_scale = (B_HEAD_DIM ** -0.5) * 1.4426950408889634

    def logit_chunk(c, mx):
        bias = jnp.concatenate([bias_ref[rows(c), :]] * grp, axis=1)
        folded = []
        for g in range(B_KV_HEADS):
            qg = jnp.concatenate(
                [q_ref[0, :, (g * grp + hh) * B_HEAD_DIM:(g * grp + hh + 1) * B_HEAD_DIM]
                 for hh in range(grp)], axis=0)
            l = lax.dot_general(k_ref[0, rows(c), g * B_HEAD_DIM:(g + 1) * B_HEAD_DIM], qg,
                                _NT_DIMS, preferred_element_type=_F32) + bias
            logit_ref[rows(c), g * gw:(g + 1) * gw] = l
            folded.append(_fold_rows(l, jnp.maximum))
        return jnp.maximum(mx, jnp.concatenate(folded, axis=1))

    mx = for_chunks(logit_chunk, jnp.full((SUBLANES, B_HEADS * t), _NEG_INF, _F32))
    mx = jnp.max(mx, axis=0, keepdims=True)

    acc_ref[...] = jnp.zeros_like(acc_ref)
    ones_rows = jnp.ones((acc_ref.shape[0] - B_HEAD_DIM, KEY_CHUNK), _BF16)

    def pv_chunk(c, carry):
        for g in range(B_KV_HEADS):
            cols = slice(g * gw, (g + 1) * gw)
            p = jnp.exp2((logit_ref[rows(c), cols] - mx[:, cols]) * exp2_scale)
            vt_ext = jnp.concatenate(
                [vt_ref[0, c, g * B_HEAD_DIM:(g + 1) * B_HEAD_DIM, :], ones_rows], axis=0)
            acc_ref[:, cols] += jnp.dot(vt_ext, p.astype(_BF16), preferred_element_type=_F32)
        return carry

    for_chunks(pv_chunk, 0)
    out_t = acc_ref[:B_HEAD_DIM, :] / acc_ref[B_HEAD_DIM:B_HEAD_DIM + 1, :]
    for h in range(B_HEADS):
        o_ref[0, :, h * B_HEAD_DIM:(h + 1) * B_HEAD_DIM] = (
            out_t[:, h * t:(h + 1) * t].T.astype(o_ref.dtype))


def _dsa_attention(q, k, vt, qi, ki, wit):
    bsz, seq, _ = q.shape
    assert Q_BLOCK == LANES and seq % KEY_CHUNK == 0
    n_sel = min(TOPK_MAX, seq // 4)

    def q_spec(width):
        return pl.BlockSpec((1, Q_BLOCK, width), lambda b, i: (b, i, 0))

    def kv_spec(width):
        return pl.BlockSpec((1, seq, width), lambda b, i: (b, 0, 0))

    return pl.pallas_call(
        functools.partial(_dsa_kernel, n_sel=n_sel),
        grid=(bsz, seq // Q_BLOCK),
        in_specs=[q_spec(IDX_HEADS * LANES),
                  pl.BlockSpec((1, IDX_HEADS, Q_BLOCK), lambda b, i: (b, 0, i)),
                  kv_spec(LANES), q_spec(B_WIDTH), kv_spec(B_KV_WIDTH),
                  pl.BlockSpec((1, seq // KEY_CHUNK, B_KV_WIDTH, KEY_CHUNK),
                               lambda b, i: (b, 0, 0, 0))],
        out_specs=q_spec(B_WIDTH),
        out_shape=jax.ShapeDtypeStruct((bsz, seq, B_WIDTH), _BF16),
        scratch_shapes=[pltpu.VMEM((seq, Q_BLOCK), _F32), pltpu.VMEM((seq, Q_BLOCK), _F32),
                        pltpu.VMEM((seq, B_HEADS * Q_BLOCK), _F32),
                        pltpu.VMEM((B_HEAD_DIM + 2 * SUBLANES, B_HEADS * Q_BLOCK), _F32)],
        compiler_params=_compiler_params(2),
        name="dsa_attention",
    )(qi, wit, ki, q, k, vt)


def _xattn_kernel(q_ref, k_ref, v_ref, o_ref):
    scale = X_HEAD_DIM ** -0.5
    for h in range(X_HEADS):
        hs = slice(h * X_HEAD_DIM, (h + 1) * X_HEAD_DIM)
        logits = lax.dot_general(q_ref[0, :, hs], k_ref[0, :, hs], _NT_DIMS,
                                 preferred_element_type=_F32) * scale
        mx = jnp.max(logits, axis=-1, keepdims=True)
        p = jnp.exp(logits - mx)
        denom = jnp.sum(p, axis=-1, keepdims=True)
        o = jnp.dot(p.astype(_BF16), v_ref[0, :, hs], preferred_element_type=_F32)
        o_ref[0, :, hs] = (o / denom).astype(o_ref.dtype)


def _cross_attention(q, k, v):
    bsz, seq, _ = q.shape
    tq = ROW_BLOCK["cross_attention"]
    m = k.shape[1]
    mem_spec = pl.BlockSpec((1, m, X_WIDTH), lambda b, i: (b, 0, 0))
    return pl.pallas_call(
        _xattn_kernel,
        grid=(bsz, seq // tq),
        in_specs=[pl.BlockSpec((1, tq, X_WIDTH), lambda b, i: (b, i, 0)), mem_spec, mem_spec],
        out_specs=pl.BlockSpec((1, tq, X_WIDTH), lambda b, i: (b, i, 0)),
        out_shape=jax.ShapeDtypeStruct((bsz, seq, X_WIDTH), _BF16),
        compiler_params=_compiler_params(2),
        name="cross_attention",
    )(q, k, v)


def _norm_proj_kernel(x_ref, g_ref, w_ref, o_ref, w_bf16_ref):
    @pl.when(pl.program_id(0) == 0)
    def _():
        w_bf16_ref[...] = w_ref[...].astype(_BF16)

    x = x_ref[...]
    ms = jnp.mean(x * x, axis=-1, keepdims=True)
    h = ((x * lax.rsqrt(ms + EPS)) * g_ref[...]).astype(_BF16)
    o_ref[...] = jnp.dot(h, w_bf16_ref[...], preferred_element_type=_F32).astype(o_ref.dtype)


def _norm_proj(x, g, w_stack, layer, out_dtype, name):
    m, d = x.shape
    tm = ROW_BLOCK["norm_proj"]
    n = w_stack.shape[2]
    return pl.pallas_call(
        _norm_proj_kernel,
        grid=(m // tm,),
        in_specs=[pl.BlockSpec((tm, d), lambda i: (i, 0)),
                  pl.BlockSpec((1, d), lambda i: (0, 0)),
                  pl.BlockSpec((None, d, n), lambda i: (layer, 0, 0),
                               pipeline_mode=pl.Buffered(1))],
        out_specs=pl.BlockSpec((tm, n), lambda i: (i, 0)),
        out_shape=jax.ShapeDtypeStruct((m, n), out_dtype),
        scratch_shapes=[pltpu.VMEM((d, n), _BF16)],
        compiler_params=_compiler_params(1),
        name=name,
    )(x, g.reshape(1, d), w_stack)


def _xattn_out_kernel(o_ref, w_ref, x_ref, g_ref, x_out_ref, h_out_ref, w_bf16_ref):
    @pl.when(pl.program_id(0) == 0)
    def _():
        w_bf16_ref[...] = w_ref[...].astype(_BF16)

    x = x_ref[...] + jnp.dot(o_ref[...], w_bf16_ref[...], preferred_element_type=_F32)
    x_out_ref[...] = x
    ms = jnp.mean(x * x, axis=-1, keepdims=True)
    h_out_ref[...] = ((x * lax.rsqrt(ms + EPS)) * g_ref[...]).astype(h_out_ref.dtype)


def _xattn_out_and_norm(o, w_stack, layer, x, g):
    m, k = o.shape
    tm = ROW_BLOCK["xattn_out_norm"]
    d = x.shape[1]
    row_spec = pl.BlockSpec((tm, d), lambda i: (i, 0))
    return pl.pallas_call(
        _xattn_out_kernel,
        grid=(m // tm,),
        in_specs=[pl.BlockSpec((tm, k), lambda i: (i, 0)),
                  pl.BlockSpec((None, k, d), lambda i: (layer, 0, 0),
                               pipeline_mode=pl.Buffered(1)),
                  row_spec,
                  pl.BlockSpec((1, d), lambda i: (0, 0))],
        out_specs=[row_spec, row_spec],
        out_shape=[jax.ShapeDtypeStruct((m, d), _F32), jax.ShapeDtypeStruct((m, d), _BF16)],
        scratch_shapes=[pltpu.VMEM((k, d), _BF16)],
        compiler_params=_compiler_params(1),
        name="xattn_out_norm",
    )(o, w_stack, x, g.reshape(1, d))


def kernel(x, mem, positions, norm_mix_g, w_in, a_norm_g, a_spatial_w, a_spatial_b, p_a, p_b,
           w_out, norm_x_g, norm_mem_g, xq_w, xk_w, xv_w, xo_w, norm_ffn_g, ffn_w1, ffn_w3,
           ffn_w2, final_norm_g):
    bsz, seq, d = x.shape
    m = bsz * seq
    depth = w_in.shape[0]
    ffn_hidden = ffn_w1.shape[-1]
    tables_b = _rope_tables(positions, B_ROT, B_HEAD_DIM)
    tables_i = _rope_tables(positions, IDX_ROT, IDX_DIM)

    xf = x.reshape(m, d)
    mem_f = mem.reshape(bsz * MEM_LEN, d)
    w_in_t = jnp.transpose(w_in, (0, 2, 1))
    for l in range(depth):
        def w_in_cols(col0):
            return Rhs(w_in_t, l, col0, transposed=True)

        h = _rmsnorm(xf, norm_mix_g[l], _BF16)
        att = _fused_matmul("proj_att", [h], [(0, w_in_cols(ZA_END))], [], _ep_identity,
                            ATT_WIDTH, _F32)
        za, q, k, vt, qi, ki, wi = _fused_matmul(
            "proj_za", [h], [(0, w_in_cols(0))], [], _ep_identity, ZA_END, _F32,
            side_job=functools.partial(_prep_side_job, att.reshape(bsz, seq, ATT_WIDTH),
                                       tables_b, tables_i))
        y_a = _spatial_gating(za, a_norm_g[l], a_spatial_w[l], a_spatial_b[l])
        y_b = _dsa_attention(q, k, vt, qi, ki, wi.transpose(0, 2, 1)).reshape(m, B_WIDTH)
        merged = _fused_matmul(
            "gated_merge", [y_a, y_b, h],
            [(0, Rhs(p_a, l)), (1, Rhs(p_b, l)), (2, w_in_cols(ATT_END)),
             (2, w_in_cols(ATT_END + d))],
            [], _ep_gated_merge, d, _BF16)
        x1 = _fused_matmul("out_proj", [merged], [(0, Rhs(w_out, l))], [xf], _ep_residual,
                           d, _F32)

        qx = _norm_proj(x1, norm_x_g[l], xq_w, l, _BF16, "xattn_q")
        kx = _norm_proj(mem_f, norm_mem_g[l], xk_w, l, _BF16, "xattn_k")
        vx = _norm_proj(mem_f, norm_mem_g[l], xv_w, l, _BF16, "xattn_v")
        ox = _cross_attention(qx.reshape(bsz, seq, X_WIDTH),
                              kx.reshape(bsz, MEM_LEN, X_WIDTH),
                              vx.reshape(bsz, MEM_LEN, X_WIDTH)).reshape(m, X_WIDTH)
        x2, h2 = _xattn_out_and_norm(ox, xo_w, l, x1, norm_ffn_g[l])

        act, w2_bf16 = _fused_matmul("ffn_up", [h2],
                                     [(0, Rhs(ffn_w1, l)), (0, Rhs(ffn_w3, l))],
                                     [], _ep_swiglu, ffn_hidden, _BF16,
                                     side_job=functools.partial(_cast_side_job, ffn_w2, l))
        xf = _fused_matmul("ffn_down", [act], [(0, Rhs(w2_bf16))], [x2], _ep_residual, d, _F32)

    return _rmsnorm(xf, final_norm_g, _F32).reshape(bsz, seq, d)
```

```python
import functools
from typing import Callable, NamedTuple, Optional

import jax
import jax.numpy as jnp
from jax import lax
from jax.experimental import pallas as pl
from jax.experimental.pallas import tpu as pltpu

D_MODEL = 4096
MEM_LEN = 256
EPS = 1e-6
ROPE_THETA = 500000.0
CHUNK = 128
A_GROUPS = 16
A_WIDTH = D_MODEL // 2
A_GROUP_DIM = A_WIDTH // A_GROUPS
B_HEADS = 16
B_HEAD_DIM = 128
B_KV_HEADS = 4
B_WIDTH = B_HEADS * B_HEAD_DIM
B_KV_WIDTH = B_KV_HEADS * B_HEAD_DIM
B_ROT = B_HEAD_DIM // 4
IDX_HEADS = 16
IDX_DIM = 64
IDX_WIDTH = IDX_HEADS * IDX_DIM
IDX_ROT = IDX_DIM // 4
TOPK_MAX = 256
Q_BLOCK = 128
X_HEADS = 4
X_HEAD_DIM = 256
X_WIDTH = X_HEADS * X_HEAD_DIM
IN_SIZES = (2 * A_WIDTH, B_WIDTH, B_KV_WIDTH, B_KV_WIDTH, IDX_WIDTH, IDX_DIM, IDX_HEADS,
            2 * D_MODEL)

LANES = 128
SUBLANES = 8
KEY_CHUNK = 256
PROJ_TN = 512
ZA_END = IN_SIZES[0]
ATT_END = ZA_END + sum(IN_SIZES[1:7])
ATT_Q0 = 0
ATT_K0 = ATT_Q0 + B_WIDTH
ATT_V0 = ATT_K0 + B_KV_WIDTH
ATT_QI0 = ATT_V0 + B_KV_WIDTH
ATT_KW0 = ATT_QI0 + IDX_WIDTH
ATT_WIDTH = -(-(ATT_END - ZA_END) // PROJ_TN) * PROJ_TN
assert ZA_END % PROJ_TN == 0 and ATT_KW0 + LANES <= ATT_WIDTH

VMEM_LIMIT_BYTES = 60 * 1024 * 1024


class Tile(NamedTuple):
    tm: int
    tn: int
    prefetch_lhs: bool = False
    row_splits: int = 1


TILES = {
    "proj_att": Tile(1024, PROJ_TN, prefetch_lhs=True),
    "proj_za": Tile(1024, 512, prefetch_lhs=True),
    "gated_merge": Tile(1024, 256, row_splits=4),
    "out_proj": Tile(1024, 512, prefetch_lhs=True),
    "ffn_up": Tile(2048, 256, row_splits=4),
    "ffn_down": Tile(512, 512, prefetch_lhs=True),
}
ROW_BLOCK = {
    "rmsnorm": 512,
    "spatial_gating": 2 * CHUNK,
    "cross_attention": 512,
    "norm_proj": 512,
    "xattn_out_norm": 256,
}

_BF16 = jnp.bfloat16
_F32 = jnp.float32
_NEG_INF = float("-inf")
_NN_DIMS = (((1,), (0,)), ((), ()))
_NT_DIMS = (((1,), (1,)), ((), ()))


def _compiler_params(n_axes):
    return pltpu.CompilerParams(
        dimension_semantics=("arbitrary",) * n_axes,
        vmem_limit_bytes=VMEM_LIMIT_BYTES,
    )


def _rmsnorm_kernel(x_ref, g_ref, o_ref):
    x = x_ref[...]
    ms = jnp.mean(x * x, axis=-1, keepdims=True)
    y = x * lax.rsqrt(ms + EPS)
    o_ref[...] = (y * g_ref[...]).astype(o_ref.dtype)


def _rmsnorm(x, g, out_dtype):
    m, d = x.shape
    tm = ROW_BLOCK["rmsnorm"]
    return pl.pallas_call(
        _rmsnorm_kernel,
        grid=(m // tm,),
        in_specs=[pl.BlockSpec((tm, d), lambda i: (i, 0)),
                  pl.BlockSpec((1, d), lambda i: (0, 0))],
        out_specs=pl.BlockSpec((tm, d), lambda i: (i, 0)),
        out_shape=jax.ShapeDtypeStruct((m, d), out_dtype),
        compiler_params=_compiler_params(1),
        name="rmsnorm",
    )(x, g.reshape(1, d))


def _ep_identity(accs, extras):
    return accs[0]


def _ep_residual(accs, extras):
    return extras[0] + accs[0]


def _ep_gated_merge(accs, extras):
    y_a, y_b, gate_a, gate_b = accs
    return jax.nn.sigmoid(gate_a) * y_a + jax.nn.sigmoid(gate_b) * y_b


def _ep_swiglu(accs, extras):
    return jax.nn.silu(accs[0]) * accs[1]


class Rhs(NamedTuple):
    array: jax.Array
    layer: Optional[int] = None
    col0: int = 0
    transposed: bool = False


def _rhs_spec(r, tn):
    if r.transposed:
        k = r.array.shape[2]
        if r.col0 % tn == 0:
            return pl.BlockSpec((None, tn, k), lambda i, j: (r.layer, j + r.col0 // tn, 0))
        assert r.col0 % SUBLANES == 0
        return pl.BlockSpec((pl.Element(1), pl.Element(tn), pl.Element(k)),
                            lambda i, j: (r.layer, (r.col0 // SUBLANES + j * (tn // SUBLANES))
                                          * SUBLANES, 0))
    assert r.col0 % tn == 0
    off = r.col0 // tn
    if r.layer is None:
        return pl.BlockSpec((r.array.shape[0], tn), lambda i, j: (0, j + off))
    return pl.BlockSpec((None, r.array.shape[1], tn), lambda i, j: (r.layer, 0, j + off))


class SideJob(NamedTuple):
    body: Callable
    arrays: tuple
    in_specs: tuple
    out_specs: tuple
    out_shape: tuple


def _cast_body(x_ref, o_ref):
    o_ref[...] = x_ref[...].astype(o_ref.dtype)


def _cast_side_job(w_stack, layer, n_steps):
    _, rows, cols = w_stack.shape
    slab = rows // n_steps
    assert slab * n_steps == rows and slab % (2 * SUBLANES) == 0
    return SideJob(_cast_body, (w_stack,),
                   (pl.BlockSpec((None, slab, cols), lambda s: (layer, s, 0)),),
                   (pl.BlockSpec((slab, cols), lambda s: (s, 0)),),
                   (jax.ShapeDtypeStruct((rows, cols), _BF16),))


def _mm_kernel(*refs, n_lhs, pair_lhs, rhs_transposed, n_extra, epilogue, side_body, n_side_in,
               row_splits):
    n_rhs = len(pair_lhs)
    lhs_refs = refs[:n_lhs]
    rhs_refs = refs[n_lhs:n_lhs + n_rhs]
    n_in = n_lhs + n_rhs + n_extra
    extra_refs = refs[n_lhs + n_rhs:n_in]
    o_ref = refs[n_in + n_side_in]
    if side_body is not None:
        side_body(*refs[n_in:n_in + n_side_in], *refs[n_in + n_side_in + 1:])
    weights = [(r[0] if len(r.shape) == 3 else r[...]).astype(_BF16) for r in rhs_refs]
    rows_per_split = o_ref.shape[0] // row_splits
    for s in range(row_splits):
        rows = slice(s * rows_per_split, (s + 1) * rows_per_split)
        accs = [lax.dot_general(lhs_refs[li][rows, :], w, _NT_DIMS if tr else _NN_DIMS,
                                preferred_element_type=_F32)
                for li, w, tr in zip(pair_lhs, weights, rhs_transposed)]
        o_ref[rows, :] = epilogue(accs, [e[rows, :] for e in extra_refs]).astype(o_ref.dtype)


def _fused_matmul(name, lhs, pairs, extras, epilogue, n_out, out_dtype, side_job=None):
    tm, tn, prefetch_lhs, row_splits = TILES[name]
    m = lhs[0].shape[0]
    assert m % tm == 0 and n_out % tn == 0
    grid = (m // tm, n_out // tn)
    lhs_mode = {} if prefetch_lhs else {"pipeline_mode": pl.Buffered(1)}
    in_specs = [pl.BlockSpec((tm, a.shape[1]), lambda i, j: (i, 0), **lhs_mode) for a in lhs]
    in_specs += [_rhs_spec(r, tn) for _, r in pairs]
    in_specs += [pl.BlockSpec((tm, tn), lambda i, j: (i, j)) for _ in extras]
    out_specs = [pl.BlockSpec((tm, tn), lambda i, j: (i, j))]
    out_shape = [jax.ShapeDtypeStruct((m, n_out), out_dtype)]
    operands = list(lhs) + [p[1].array for p in pairs] + list(extras)
    side = None if side_job is None else side_job(grid[0] * grid[1])
    if side is not None:
        def per_step(spec):
            return pl.BlockSpec(spec.block_shape,
                                lambda i, j: spec.index_map(i * grid[1] + j))
        in_specs += [per_step(s) for s in side.in_specs]
        out_specs += [per_step(s) for s in side.out_specs]
        out_shape += list(side.out_shape)
        operands += list(side.arrays)
    kern = functools.partial(_mm_kernel, n_lhs=len(lhs), pair_lhs=tuple(p[0] for p in pairs),
                             rhs_transposed=tuple(p[1].transposed for p in pairs),
                             n_extra=len(extras), epilogue=epilogue,
                             side_body=None if side is None else side.body,
                             n_side_in=0 if side is None else len(side.in_specs),
                             row_splits=row_splits)
    outs = pl.pallas_call(
        kern,
        grid=grid,
        in_specs=in_specs,
        out_specs=out_specs,
        out_shape=out_shape,
        compiler_params=_compiler_params(2),
        name=name,
    )(*operands)
    return outs[0] if side is None else tuple(outs)


def _gating_kernel(za_ref, ng_ref, w_ref, bt_ref, o_ref):
    row = lax.broadcasted_iota(jnp.int32, (CHUNK, CHUNK), 0)
    col = lax.broadcasted_iota(jnp.int32, (CHUNK, CHUNK), 1)
    causal = col <= row
    for c in range(za_ref.shape[0] // CHUNK):
        rows = slice(c * CHUNK, (c + 1) * CHUNK)
        z = jax.nn.gelu(za_ref[rows, :])
        u = z[:, :A_WIDTH]
        v = z[:, A_WIDTH:]
        ms = jnp.mean(v * v, axis=-1, keepdims=True)
        vn = ((v * lax.rsqrt(ms + EPS)) * ng_ref[...]).astype(_BF16)
        for g in range(A_GROUPS):
            sl = slice(g * A_GROUP_DIM, (g + 1) * A_GROUP_DIM)
            w = jnp.where(causal, w_ref[g], 0.0).astype(_BF16)
            s = jnp.dot(w, vn[:, sl], preferred_element_type=_F32) + bt_ref[:, g:g + 1]
            o_ref[rows, sl] = (u[:, sl] * s).astype(o_ref.dtype)


def _spatial_gating(za, norm_g, w_s, b_s):
    m = za.shape[0]
    tm = ROW_BLOCK["spatial_gating"]
    return pl.pallas_call(
        _gating_kernel,
        grid=(m // tm,),
        in_specs=[pl.BlockSpec((tm, 2 * A_WIDTH), lambda i: (i, 0)),
                  pl.BlockSpec((1, A_WIDTH), lambda i: (0, 0)),
                  pl.BlockSpec((A_GROUPS, CHUNK, CHUNK), lambda i: (0, 0, 0)),
                  pl.BlockSpec((CHUNK, A_GROUPS), lambda i: (0, 0))],
        out_specs=pl.BlockSpec((tm, A_WIDTH), lambda i: (i, 0)),
        out_shape=jax.ShapeDtypeStruct((m, A_WIDTH), _BF16),
        compiler_params=_compiler_params(1),
        name="spatial_gating",
    )(za, norm_g.reshape(1, A_WIDTH), w_s, b_s.T)


def _rope_tables(positions, rot_dim, period):
    half = rot_dim // 2
    inv_freq = ROPE_THETA ** (-jnp.arange(0, rot_dim, 2, dtype=_F32) / rot_dim)
    ang = positions.astype(_F32)[..., None] * inv_freq
    cos, sin = jnp.cos(ang), jnp.sin(ang)
    zeros_half = jnp.zeros_like(sin)
    rest = jnp.zeros(sin.shape[:-1] + (period - rot_dim,), _F32)

    def lanes(*pieces):
        return jnp.tile(jnp.concatenate(pieces, axis=-1), LANES // period)

    return (lanes(cos, cos, rest + 1.0), lanes(zeros_half, sin, rest),
            lanes(-sin, zeros_half, rest))


def _rope(x, c, s1, s2, half):
    return x * c + pltpu.roll(x, half, 1) * s1 + pltpu.roll(x, LANES - half, 1) * s2


def _prep_kernel(att_ref, cb_ref, s1b_ref, s2b_ref, ci_ref, s1i_ref, s2i_ref,
                 q_ref, k_ref, vt_ref, qi_ref, ki_ref, wi_ref):
    cb, s1b, s2b = cb_ref[0], s1b_ref[0], s2b_ref[0]
    ci, s1i, s2i = ci_ref[0], s1i_ref[0], s2i_ref[0]
    for h in range(B_HEADS):
        x = att_ref[0, :, ATT_Q0 + h * LANES:ATT_Q0 + (h + 1) * LANES]
        q_ref[0, :, h * LANES:(h + 1) * LANES] = _rope(x, cb, s1b, s2b, B_ROT // 2).astype(_BF16)
    for h in range(B_KV_HEADS):
        x = att_ref[0, :, ATT_K0 + h * LANES:ATT_K0 + (h + 1) * LANES]
        k_ref[0, :, h * LANES:(h + 1) * LANES] = _rope(x, cb, s1b, s2b, B_ROT // 2).astype(_BF16)
    vt_ref[0, 0] = att_ref[0, :, ATT_V0:ATT_V0 + B_KV_WIDTH].T.astype(_BF16)
    lane = lax.broadcasted_iota(jnp.int32, ci.shape, 1)
    is_ki = lane < IDX_DIM
    heads_per_group = LANES // IDX_DIM
    for j in range(IDX_WIDTH // LANES):
        x = att_ref[0, :, ATT_QI0 + j * LANES:ATT_QI0 + (j + 1) * LANES]
        y = _rope(x, ci, s1i, s2i, IDX_ROT // 2)
        for r in range(heads_per_group):
            h = j * heads_per_group + r
            yr = y if r == 0 else pltpu.roll(y, LANES - r * IDX_DIM, 1)
            qi_ref[0, :, h * LANES:(h + 1) * LANES] = jnp.where(is_ki, yr, 0.0).astype(_BF16)
    x = att_ref[0, :, ATT_KW0:ATT_KW0 + LANES]
    y = _rope(x, jnp.where(is_ki, ci, 1.0), jnp.where(is_ki, s1i, 0.0),
              jnp.where(is_ki, s2i, 0.0), IDX_ROT // 2)
    ki_ref[0] = jnp.where(is_ki, y, 0.0).astype(_BF16)
    wi_ref[0] = y[:, IDX_DIM:IDX_DIM + IDX_HEADS]


def _prep_side_job(att, tables_b, tables_i, n_steps):
    bsz, seq, _ = att.shape
    ts = bsz * seq // n_steps
    assert ts * n_steps == bsz * seq and seq % ts == 0 and KEY_CHUNK % ts == 0
    assert ts % (2 * SUBLANES) == 0
    per_seq = seq // ts
    per_chunk = KEY_CHUNK // ts

    def tokens(width):
        return pl.BlockSpec((1, ts, width), lambda s: (s // per_seq, s % per_seq, 0))

    def out(width, dtype):
        return tokens(width), jax.ShapeDtypeStruct((bsz, seq, width), dtype)

    vt_out = (pl.BlockSpec((1, 1, B_KV_WIDTH, ts),
                           lambda s: (s // per_seq, (s % per_seq) // per_chunk, 0,
                                      (s % per_seq) % per_chunk)),
              jax.ShapeDtypeStruct((bsz, seq // KEY_CHUNK, B_KV_WIDTH, KEY_CHUNK), _BF16))
    outs = [out(B_WIDTH, _BF16), out(B_KV_WIDTH, _BF16), vt_out,
            out(IDX_HEADS * LANES, _BF16), out(LANES, _BF16), out(IDX_HEADS, _F32)]
    return SideJob(_prep_kernel, (att, *tables_b, *tables_i),
                   (tokens(ATT_WIDTH),) + (tokens(LANES),) * 6,
                   tuple(o[0] for o in outs), tuple(o[1] for o in outs))


def _ordered_int_to_float(key):
    bits = jnp.where(key < 0, key ^ jnp.int32(0x7FFFFFFF), key)
    return lax.bitcast_convert_type(bits, _F32)


def _fold_rows(x, op):
    while x.shape[0] > SUBLANES:
        half = x.shape[0] // 2
        x = op(x[:half], x[half:])
    return x


def _dsa_kernel(*refs, n_sel):
    seq = refs[4].shape[1]
    n_chunks = (pl.program_id(1) * Q_BLOCK + Q_BLOCK + KEY_CHUNK - 1) // KEY_CHUNK
    for n in range(1, seq // KEY_CHUNK + 1):
        pl.when(n_chunks == n)(functools.partial(_dsa_block, *refs, n_sel=n_sel, n_chunks=n))


def _dsa_block(qi_ref, wit_ref, ki_ref, q_ref, k_ref, vt_ref, o_ref,
               score_ref, bias_ref, logit_ref, acc_ref, *, n_sel, n_chunks):
    t = Q_BLOCK
    blk = pl.program_id(1)
    qpos = blk * t + lax.broadcasted_iota(jnp.int32, (1, t), 1)
    kiota = lax.broadcasted_iota(jnp.int32, (KEY_CHUNK, 1), 0)
    idx_scale = (IDX_DIM ** -0.5) * (IDX_HEADS ** -0.5)

    def rows(c):
        return slice(c * KEY_CHUNK, (c + 1) * KEY_CHUNK)

    def for_chunks(body, carry):
        for c in range(n_chunks):
            carry = body(c, carry)
        return carry

    def indexer_chunk(c, carry):
        ki = ki_ref[0, rows(c), :]
        acc = jnp.zeros((KEY_CHUNK, t), _F32)
        for h in range(0, IDX_HEADS, 2):
            qpair = jnp.concatenate([qi_ref[0, :, h * LANES:(h + 1) * LANES],
                                     qi_ref[0, :, (h + 1) * LANES:(h + 2) * LANES]], axis=0)
            d = lax.dot_general(ki, qpair, _NT_DIMS, preferred_element_type=_F32)
            acc = acc + wit_ref[0, h:h + 1, :] * jnp.maximum(d[:, :t], 0.0)
            acc = acc + wit_ref[0, h + 1:h + 2, :] * jnp.maximum(d[:, t:], 0.0)
        kpos = c * KEY_CHUNK + kiota
        score_ref[rows(c), :] = jnp.where(kpos <= qpos, acc * idx_scale + 0.0, _NEG_INF)
        return carry

    for_chunks(indexer_chunk, 0)

    def count(indicator):
        def chunk(c, part):
            return part + _fold_rows(indicator(score_ref[rows(c), :], c), jnp.add)
        part = for_chunks(chunk, jnp.zeros((SUBLANES, t), _F32))
        return jnp.sum(part, axis=0, keepdims=True)

    def count_ge(cand_f):
        return count(lambda s, c: jnp.where(s >= cand_f, 1.0, 0.0))

    def search():
        int_min = jnp.int32(-2 ** 31)
        base = jnp.where(count_ge(jnp.zeros((1, t), _F32)) >= n_sel, jnp.int32(0), int_min)

        def search_step(it, base):
            cand = base | lax.shift_left(jnp.int32(1), jnp.int32(30) - it)
            cnt = count_ge(_ordered_int_to_float(cand))
            return jnp.where(cnt >= n_sel, cand, base)

        return lax.fori_loop(0, 31, search_step, base)

    if (n_chunks - 1) * KEY_CHUNK >= n_sel:
        base = search()
    else:
        base = lax.cond((blk + 1) * t <= n_sel, lambda: jnp.zeros((1, t), jnp.int32), search)
    thr = jnp.where(qpos + 1 <= n_sel, _NEG_INF, _ordered_int_to_float(base))

    def bias_chunk(c, part):
        s = score_ref[rows(c), :]
        sel = jnp.where(c * KEY_CHUNK + kiota <= qpos, jnp.where(s >= thr, 1.0, 0.0), 0.0)
        bias_ref[rows(c), :] = jnp.where(sel > 0.0, 0.0, _NEG_INF)
        return part + _fold_rows(sel, jnp.add)

    n_ge = jnp.sum(for_chunks(bias_chunk, jnp.zeros((SUBLANES, t), _F32)), axis=0, keepdims=True)

    @pl.when(jnp.max(n_ge) > n_sel)
    def _():
        nxt = _ordered_int_to_float(base + 1)
        tied = n_ge > n_sel
        n_above = count(lambda s, c: jnp.where(s >= nxt, 1.0, 0.0))
        need = jnp.where(tied, n_sel - n_above, 0.0)
        no_index = float(k_ref.shape[1])
        front_v = jnp.where(tied, jnp.inf, _NEG_INF)
        front_i = jnp.where(tied, -1.0, no_index)

        def beyond(s, kposf, front_v, front_i):
            after = jnp.where(s < front_v, 1.0,
                              jnp.where(s == front_v, jnp.where(kposf > front_i, 1.0, 0.0), 0.0))
            return jnp.where(s >= thr, jnp.where(s < nxt, after, 0.0), 0.0)

        def kposf(c):
            return (c * KEY_CHUNK + kiota).astype(_F32)

        def advance(_, carry):
            front_v, front_i, need = carry

            def best_value(c, part):
                s = score_ref[rows(c), :]
                cand = jnp.where(beyond(s, kposf(c), front_v, front_i) > 0.0, s, _NEG_INF)
                return jnp.maximum(part, _fold_rows(cand, jnp.maximum))

            v = jnp.max(for_chunks(best_value, jnp.full((SUBLANES, t), _NEG_INF, _F32)),
                        axis=0, keepdims=True)

            def first_index(c, part):
                s = score_ref[rows(c), :]
                hit = jnp.where(s == v, beyond(s, kposf(c), front_v, front_i), 0.0)
                cand = jnp.where(hit > 0.0, kposf(c), no_index)
                return jnp.minimum(part, _fold_rows(cand, jnp.minimum))

            i = jnp.min(for_chunks(first_index, jnp.full((SUBLANES, t), no_index, _F32)),
                        axis=0, keepdims=True)
            active = need > 0.0
            return (jnp.where(active, v, front_v), jnp.where(active, i, front_i),
                    jnp.where(active, need - 1.0, need))

        front_v, front_i, _ = lax.fori_loop(0, jnp.max(need).astype(jnp.int32), advance,
                                            (front_v, front_i, need))

        def tie_chunk(c, carry):
            s = score_ref[rows(c), :]
            upto = jnp.where(s > front_v, 0.0,
                             jnp.where(s == front_v,
                                       jnp.where(kposf(c) <= front_i, 0.0, _NEG_INF), _NEG_INF))
            keep = jnp.where(s >= nxt, 0.0, jnp.where(s >= thr, upto, _NEG_INF))
            bias_ref[rows(c), :] = jnp.where(c * KEY_CHUNK + kiota <= qpos, keep, _NEG_INF)
            return carry

        for_chunks(tie_chunk, 0)

    grp = B_HEADS // B_KV_HEADS
    gw = grp * t
    exp2_scale = (B_HEAD_DIM ** -0.5) * 1.4426950408889634

    def logit_chunk(c, mx):
        bias = jnp.concatenate([bias_ref[rows(c), :]] * grp, axis=1)
        folded = []
        for g in range(B_KV_HEADS):
            qg = jnp.concatenate(
                [q_ref[0, :, (g * grp + hh) * B_HEAD_DIM:(g * grp + hh + 1) * B_HEAD_DIM]
                 for hh in range(grp)], axis=0)
            l = lax.dot_general(k_ref[0, rows(c), g * B_HEAD_DIM:(g + 1) * B_HEAD_DIM], qg,
                                _NT_DIMS, preferred_element_type=_F32) + bias
            logit_ref[rows(c), g * gw:(g + 1) * gw] = l
            folded.append(_fold_rows(l, jnp.maximum))
        return jnp.maximum(mx, jnp.concatenate(folded, axis=1))

    mx = for_chunks(logit_chunk, jnp.full((SUBLANES, B_HEADS * t), _NEG_INF, _F32))
    mx = jnp.max(mx, axis=0, keepdims=True)

    acc_ref[...] = jnp.zeros_like(acc_ref)
    ones_rows = jnp.ones((acc_ref.shape[0] - B_HEAD_DIM, KEY_CHUNK), _BF16)

    def pv_chunk(c, carry):
        for g in range(B_KV_HEADS):
            cols = slice(g * gw, (g + 1) * gw)
            p = jnp.exp2((logit_ref[rows(c), cols] - mx[:, cols]) * exp2_scale)
            vt_ext = jnp.concatenate(
                [vt_ref[0, c, g * B_HEAD_DIM:(g + 1) * B_HEAD_DIM, :], ones_rows], axis=0)
            acc_ref[:, cols] += jnp.dot(vt_ext, p.astype(_BF16), preferred_element_type=_F32)
        return carry

    for_chunks(pv_chunk, 0)
    out_t = acc_ref[:B_HEAD_DIM, :] / acc_ref[B_HEAD_DIM:B_HEAD_DIM + 1, :]
    for h in range(B_HEADS):
        o_ref[0, :, h * B_HEAD_DIM:(h + 1) * B_HEAD_DIM] = (
            out_t[:, h * t:(h + 1) * t].T.astype(o_ref.dtype))


def _dsa_attention(q, k, vt, qi, ki, wit):
    bsz, seq, _ = q.shape
    assert Q_BLOCK == LANES and seq % KEY_CHUNK == 0
    n_sel = min(TOPK_MAX, seq // 4)

    def q_spec(width):
        return pl.BlockSpec((1, Q_BLOCK, width), lambda b, i: (b, i, 0))

    def kv_spec(width):
        return pl.BlockSpec((1, seq, width), lambda b, i: (b, 0, 0))

    return pl.pallas_call(
        functools.partial(_dsa_kernel, n_sel=n_sel),
        grid=(bsz, seq // Q_BLOCK),
        in_specs=[q_spec(IDX_HEADS * LANES),
                  pl.BlockSpec((1, IDX_HEADS, Q_BLOCK), lambda b, i: (b, 0, i)),
                  kv_spec(LANES), q_spec(B_WIDTH), kv_spec(B_KV_WIDTH),
                  pl.BlockSpec((1, seq // KEY_CHUNK, B_KV_WIDTH, KEY_CHUNK),
                               lambda b, i: (b, 0, 0, 0))],
        out_specs=q_spec(B_WIDTH),
        out_shape=jax.ShapeDtypeStruct((bsz, seq, B_WIDTH), _BF16),
        scratch_shapes=[pltpu.VMEM((seq, Q_BLOCK), _F32), pltpu.VMEM((seq, Q_BLOCK), _F32),
                        pltpu.VMEM((seq, B_HEADS * Q_BLOCK), _F32),
                        pltpu.VMEM((B_HEAD_DIM + 2 * SUBLANES, B_HEADS * Q_BLOCK), _F32)],
        compiler_params=_compiler_params(2),
        name="dsa_attention",
    )(qi, wit, ki, q, k, vt)


def _xattn_kernel(q_ref, k_ref, v_ref, o_ref):
    scale = X_HEAD_DIM ** -0.5
    for h in range(X_HEADS):
        hs = slice(h * X_HEAD_DIM, (h + 1) * X_HEAD_DIM)
        logits = lax.dot_general(q_ref[0, :, hs], k_ref[0, :, hs], _NT_DIMS,
                                 preferred_element_type=_F32) * scale
        mx = jnp.max(logits, axis=-1, keepdims=True)
        p = jnp.exp(logits - mx)
        denom = jnp.sum(p, axis=-1, keepdims=True)
        o = jnp.dot(p.astype(_BF16), v_ref[0, :, hs], preferred_element_type=_F32)
        o_ref[0, :, hs] = (o / denom).astype(o_ref.dtype)


def _cross_attention(q, k, v):
    bsz, seq, _ = q.shape
    tq = ROW_BLOCK["cross_attention"]
    m = k.shape[1]
    mem_spec = pl.BlockSpec((1, m, X_WIDTH), lambda b, i: (b, 0, 0))
    return pl.pallas_call(
        _xattn_kernel,
        grid=(bsz, seq // tq),
        in_specs=[pl.BlockSpec((1, tq, X_WIDTH), lambda b, i: (b, i, 0)), mem_spec, mem_spec],
        out_specs=pl.BlockSpec((1, tq, X_WIDTH), lambda b, i: (b, i, 0)),
        out_shape=jax.ShapeDtypeStruct((bsz, seq, X_WIDTH), _BF16),
        compiler_params=_compiler_params(2),
        name="cross_attention",
    )(q, k, v)


def _norm_proj_kernel(x_ref, g_ref, w_ref, o_ref, w_bf16_ref):
    @pl.when(pl.program_id(0) == 0)
    def _():
        w_bf16_ref[...] = w_ref[...].astype(_BF16)

    x = x_ref[...]
    ms = jnp.mean(x * x, axis=-1, keepdims=True)
    h = ((x * lax.rsqrt(ms + EPS)) * g_ref[...]).astype(_BF16)
    o_ref[...] = jnp.dot(h, w_bf16_ref[...], preferred_element_type=_F32).astype(o_ref.dtype)


def _norm_proj(x, g, w_stack, layer, out_dtype, name):
    m, d = x.shape
    tm = ROW_BLOCK["norm_proj"]
    n = w_stack.shape[2]
    return pl.pallas_call(
        _norm_proj_kernel,
        grid=(m // tm,),
        in_specs=[pl.BlockSpec((tm, d), lambda i: (i, 0)),
                  pl.BlockSpec((1, d), lambda i: (0, 0)),
                  pl.BlockSpec((None, d, n), lambda i: (layer, 0, 0),
                               pipeline_mode=pl.Buffered(1))],
        out_specs=pl.BlockSpec((tm, n), lambda i: (i, 0)),
        out_shape=jax.ShapeDtypeStruct((m, n), out_dtype),
        scratch_shapes=[pltpu.VMEM((d, n), _BF16)],
        compiler_params=_compiler_params(1),
        name=name,
    )(x, g.reshape(1, d), w_stack)


def _xattn_out_kernel(o_ref, w_ref, x_ref, g_ref, x_out_ref, h_out_ref, w_bf16_ref):
    @pl.when(pl.program_id(0) == 0)
    def _():
        w_bf16_ref[...] = w_ref[...].astype(_BF16)

    x = x_ref[...] + jnp.dot(o_ref[...], w_bf16_ref[...], preferred_element_type=_F32)
    x_out_ref[...] = x
    ms = jnp.mean(x * x, axis=-1, keepdims=True)
    h_out_ref[...] = ((x * lax.rsqrt(ms + EPS)) * g_ref[...]).astype(h_out_ref.dtype)


def _xattn_out_and_norm(o, w_stack, layer, x, g):
    m, k = o.shape
    tm = ROW_BLOCK["xattn_out_norm"]
    d = x.shape[1]
    row_spec = pl.BlockSpec((tm, d), lambda i: (i, 0))
    return pl.pallas_call(
        _xattn_out_kernel,
        grid=(m // tm,),
        in_specs=[pl.BlockSpec((tm, k), lambda i: (i, 0)),
                  pl.BlockSpec((None, k, d), lambda i: (layer, 0, 0),
                               pipeline_mode=pl.Buffered(1)),
                  row_spec,
                  pl.BlockSpec((1, d), lambda i: (0, 0))],
        out_specs=[row_spec, row_spec],
        out_shape=[jax.ShapeDtypeStruct((m, d), _F32), jax.ShapeDtypeStruct((m, d), _BF16)],
        scratch_shapes=[pltpu.VMEM((k, d), _BF16)],
        compiler_params=_compiler_params(1),
        name="xattn_out_norm",
    )(o, w_stack, x, g.reshape(1, d))


def kernel(x, mem, positions, norm_mix_g, w_in, a_norm_g, a_spatial_w, a_spatial_b, p_a, p_b,
           w_out, norm_x_g, norm_mem_g, xq_w, xk_w, xv_w, xo_w, norm_ffn_g, ffn_w1, ffn_w3,
           ffn_w2, final_norm_g):
    bsz, seq, d = x.shape
    m = bsz * seq
    depth = w_in.shape[0]
    ffn_hidden = ffn_w1.shape[-1]
    tables_b = _rope_tables(positions, B_ROT, B_HEAD_DIM)
    tables_i = _rope_tables(positions, IDX_ROT, IDX_DIM)

    xf = x.reshape(m, d)
    mem_f = mem.reshape(bsz * MEM_LEN, d)
    w_in_t = jnp.transpose(w_in, (0, 2, 1))
    for l in range(depth):
        def w_in_cols(col0):
            return Rhs(w_in_t, l, col0, transposed=True)

        h = _rmsnorm(xf, norm_mix_g[l], _BF16)
        att = _fused_matmul("proj_att", [h], [(0, w_in_cols(ZA_END))], [], _ep_identity,
                            ATT_WIDTH, _F32)
        za, q, k, vt, qi, ki, wi = _fused_matmul(
            "proj_za", [h], [(0, w_in_cols(0))], [], _ep_identity, ZA_END, _F32,
            side_job=functools.partial(_prep_side_job, att.reshape(bsz, seq, ATT_WIDTH),
                                       tables_b, tables_i))
        y_a = _spatial_gating(za, a_norm_g[l], a_spatial_w[l], a_spatial_b[l])
        y_b = _dsa_attention(q, k, vt, qi, ki, wi.transpose(0, 2, 1)).reshape(m, B_WIDTH)
        merged = _fused_matmul(
            "gated_merge", [y_a, y_b, h],
            [(0, Rhs(p_a, l)), (1, Rhs(p_b, l)), (2, w_in_cols(ATT_END)),
             (2, w_in_cols(ATT_END + d))],
            [], _ep_gated_merge, d, _BF16)
        x1 = _fused_matmul("out_proj", [merged], [(0, Rhs(w_out, l))], [xf], _ep_residual,
                           d, _F32)

        qx = _norm_proj(x1, norm_x_g[l], xq_w, l, _BF16, "xattn_q")
        kx = _norm_proj(mem_f, norm_mem_g[l], xk_w, l, _BF16, "xattn_k")
        vx = _norm_proj(mem_f, norm_mem_g[l], xv_w, l, _BF16, "xattn_v")
        ox = _cross_attention(qx.reshape(bsz, seq, X_WIDTH),
                              kx.reshape(bsz, MEM_LEN, X_WIDTH),
                              vx.reshape(bsz, MEM_LEN, X_WIDTH)).reshape(m, X_WIDTH)
        x2, h2 = _xattn_out_and_norm(ox, xo_w, l, x1, norm_ffn_g[l])

        act, w2_bf16 = _fused_matmul("ffn_up", [h2],
                                     [(0, Rhs(ffn_w1, l)), (0, Rhs(ffn_w3, l))],
                                     [], _ep_swiglu, ffn_hidden, _BF16,
                                     side_job=functools.partial(_cast_side_job, ffn_w2, l))
        xf = _fused_matmul("ffn_down", [act], [(0, Rhs(w2_bf16))], [x2], _ep_residual, d, _F32)

    return _rmsnorm(xf, final_norm_g, _F32).reshape(bsz, seq, d)
```

```python
import functools
from typing import Callable, NamedTuple, Optional

import jax
import jax.numpy as jnp
from jax import lax
from jax.experimental import pallas as pl
from jax.experimental.pallas import tpu as pltpu

D_MODEL = 4096
MEM_LEN = 256
EPS = 1e-6
ROPE_THETA = 500000.0
CHUNK = 128
A_GROUPS = 16
A_WIDTH = D_MODEL // 2
A_GROUP_DIM = A_WIDTH // A_GROUPS
B_HEADS = 16
B_HEAD_DIM = 128
B_KV_HEADS = 4
B_WIDTH = B_HEADS * B_HEAD_DIM
B_KV_WIDTH = B_KV_HEADS * B_HEAD_DIM
B_ROT = B_HEAD_DIM // 4
IDX_HEADS = 16
IDX_DIM = 64
IDX_WIDTH = IDX_HEADS * IDX_DIM
IDX_ROT = IDX_DIM // 4
TOPK_MAX = 256
Q_BLOCK = 128
X_HEADS = 4
X_HEAD_DIM = 256
X_WIDTH = X_HEADS * X_HEAD_DIM
IN_SIZES = (2 * A_WIDTH, B_WIDTH, B_KV_WIDTH, B_KV_WIDTH, IDX_WIDTH, IDX_DIM, IDX_HEADS,
            2 * D_MODEL)

LANES = 128
SUBLANES = 8
KEY_CHUNK = 512
PROJ_TN = 512
ZA_END = IN_SIZES[0]
ATT_END = ZA_END + sum(IN_SIZES[1:7])
ATT_Q0 = 0
ATT_K0 = ATT_Q0 + B_WIDTH
ATT_V0 = ATT_K0 + B_KV_WIDTH
ATT_QI0 = ATT_V0 + B_KV_WIDTH
ATT_KW0 = ATT_QI0 + IDX_WIDTH
ATT_WIDTH = -(-(ATT_END - ZA_END) // PROJ_TN) * PROJ_TN
assert ZA_END % PROJ_TN == 0 and ATT_KW0 + LANES <= ATT_WIDTH

VMEM_LIMIT_BYTES = 60 * 1024 * 1024


class Tile(NamedTuple):
    tm: int
    tn: int
    prefetch_lhs: bool = False
    row_splits: int = 1


TILES = {
    "proj_att": Tile(1024, PROJ_TN, prefetch_lhs=True),
    "proj_za": Tile(1024, 512, prefetch_lhs=True),
    "gated_merge": Tile(1024, 256, row_splits=4),
    "out_proj": Tile(1024, 512, prefetch_lhs=True),
    "ffn_up": Tile(2048, 256, row_splits=4),
    "ffn_down": Tile(512, 512, prefetch_lhs=True),
}
ROW_BLOCK = {
    "rmsnorm": 512,
    "spatial_gating": 2 * CHUNK,
    "cross_attention": 512,
    "norm_proj": 512,
    "xattn_out_norm": 256,
}

_BF16 = jnp.bfloat16
_F32 = jnp.float32
_NEG_INF = float("-inf")
_NN_DIMS = (((1,), (0,)), ((), ()))
_NT_DIMS = (((1,), (1,)), ((), ()))


def _compiler_params(n_axes):
    return pltpu.CompilerParams(
        dimension_semantics=("arbitrary",) * n_axes,
        vmem_limit_bytes=VMEM_LIMIT_BYTES,
    )


def _rmsnorm_kernel(x_ref, g_ref, o_ref):
    x = x_ref[...]
    ms = jnp.mean(x * x, axis=-1, keepdims=True)
    y = x * lax.rsqrt(ms + EPS)
    o_ref[...] = (y * g_ref[...]).astype(o_ref.dtype)


def _rmsnorm(x, g, out_dtype):
    m, d = x.shape
    tm = ROW_BLOCK["rmsnorm"]
    return pl.pallas_call(
        _rmsnorm_kernel,
        grid=(m // tm,),
        in_specs=[pl.BlockSpec((tm, d), lambda i: (i, 0)),
                  pl.BlockSpec((1, d), lambda i: (0, 0))],
        out_specs=pl.BlockSpec((tm, d), lambda i: (i, 0)),
        out_shape=jax.ShapeDtypeStruct((m, d), out_dtype),
        compiler_params=_compiler_params(1),
        name="rmsnorm",
    )(x, g.reshape(1, d))


def _ep_identity(accs, extras):
    return accs[0]


def _ep_residual(accs, extras):
    return extras[0] + accs[0]


def _ep_gated_merge(accs, extras):
    y_a, y_b, gate_a, gate_b = accs
    return jax.nn.sigmoid(gate_a) * y_a + jax.nn.sigmoid(gate_b) * y_b


def _ep_swiglu(accs, extras):
    return jax.nn.silu(accs[0]) * accs[1]


class Rhs(NamedTuple):
    array: jax.Array
    layer: Optional[int] = None
    col0: int = 0
    transposed: bool = False


def _rhs_spec(r, tn):
    if r.transposed:
        k = r.array.shape[2]
        if r.col0 % tn == 0:
            return pl.BlockSpec((None, tn, k), lambda i, j: (r.layer, j + r.col0 // tn, 0))
        assert r.col0 % SUBLANES == 0
        return pl.BlockSpec((pl.Element(1), pl.Element(tn), pl.Element(k)),
                            lambda i, j: (r.layer, (r.col0 // SUBLANES + j * (tn // SUBLANES))
                                          * SUBLANES, 0))
    assert r.col0 % tn == 0
    off = r.col0 // tn
    if r.layer is None:
        return pl.BlockSpec((r.array.shape[0], tn), lambda i, j: (0, j + off))
    return pl.BlockSpec((None, r.array.shape[1], tn), lambda i, j: (r.layer, 0, j + off))


class SideJob(NamedTuple):
    body: Callable
    arrays: tuple
    in_specs: tuple
    out_specs: tuple
    out_shape: tuple


def _cast_body(x_ref, o_ref):
    o_ref[...] = x_ref[...].astype(o_ref.dtype)


def _cast_side_job(w_stack, layer, n_steps):
    _, rows, cols = w_stack.shape
    slab = rows // n_steps
    assert slab * n_steps == rows and slab % (2 * SUBLANES) == 0
    return SideJob(_cast_body, (w_stack,),
                   (pl.BlockSpec((None, slab, cols), lambda s: (layer, s, 0)),),
                   (pl.BlockSpec((slab, cols), lambda s: (s, 0)),),
                   (jax.ShapeDtypeStruct((rows, cols), _BF16),))


def _mm_kernel(*refs, n_lhs, pair_lhs, rhs_transposed, n_extra, epilogue, side_body, n_side_in,
               row_splits):
    n_rhs = len(pair_lhs)
    lhs_refs = refs[:n_lhs]
    rhs_refs = refs[n_lhs:n_lhs + n_rhs]
    n_in = n_lhs + n_rhs + n_extra
    extra_refs = refs[n_lhs + n_rhs:n_in]
    o_ref = refs[n_in + n_side_in]
    if side_body is not None:
        side_body(*refs[n_in:n_in + n_side_in], *refs[n_in + n_side_in + 1:])
    weights = [(r[0] if len(r.shape) == 3 else r[...]).astype(_BF16) for r in rhs_refs]
    rows_per_split = o_ref.shape[0] // row_splits
    for s in range(row_splits):
        rows = slice(s * rows_per_split, (s + 1) * rows_per_split)
        accs = [lax.dot_general(lhs_refs[li][rows, :], w, _NT_DIMS if tr else _NN_DIMS,
                                preferred_element_type=_F32)
                for li, w, tr in zip(pair_lhs, weights, rhs_transposed)]
        o_ref[rows, :] = epilogue(accs, [e[rows, :] for e in extra_refs]).astype(o_ref.dtype)


def _fused_matmul(name, lhs, pairs, extras, epilogue, n_out, out_dtype, side_job=None):
    tm, tn, prefetch_lhs, row_splits = TILES[name]
    m = lhs[0].shape[0]
    assert m % tm == 0 and n_out % tn == 0
    grid = (m // tm, n_out // tn)
    lhs_mode = {} if prefetch_lhs else {"pipeline_mode": pl.Buffered(1)}
    in_specs = [pl.BlockSpec((tm, a.shape[1]), lambda i, j: (i, 0), **lhs_mode) for a in lhs]
    in_specs += [_rhs_spec(r, tn) for _, r in pairs]
    in_specs += [pl.BlockSpec((tm, tn), lambda i, j: (i, j)) for _ in extras]
    out_specs = [pl.BlockSpec((tm, tn), lambda i, j: (i, j))]
    out_shape = [jax.ShapeDtypeStruct((m, n_out), out_dtype)]
    operands = list(lhs) + [p[1].array for p in pairs] + list(extras)
    side = None if side_job is None else side_job(grid[0] * grid[1])
    if side is not None:
        def per_step(spec):
            return pl.BlockSpec(spec.block_shape,
                                lambda i, j: spec.index_map(i * grid[1] + j))
        in_specs += [per_step(s) for s in side.in_specs]
        out_specs += [per_step(s) for s in side.out_specs]
        out_shape += list(side.out_shape)
        operands += list(side.arrays)
    kern = functools.partial(_mm_kernel, n_lhs=len(lhs), pair_lhs=tuple(p[0] for p in pairs),
                             rhs_transposed=tuple(p[1].transposed for p in pairs),
                             n_extra=len(extras), epilogue=epilogue,
                             side_body=None if side is None else side.body,
                             n_side_in=0 if side is None else len(side.in_specs),
                             row_splits=row_splits)
    outs = pl.pallas_call(
        kern,
        grid=grid,
        in_specs=in_specs,
        out_specs=out_specs,
        out_shape=out_shape,
        compiler_params=_compiler_params(2),
        name=name,
    )(*operands)
    return outs[0] if side is None else tuple(outs)


def _gating_kernel(za_ref, ng_ref, w_ref, bt_ref, o_ref):
    row = lax.broadcasted_iota(jnp.int32, (CHUNK, CHUNK), 0)
    col = lax.broadcasted_iota(jnp.int32, (CHUNK, CHUNK), 1)
    causal = col <= row
    for c in range(za_ref.shape[0] // CHUNK):
        rows = slice(c * CHUNK, (c + 1) * CHUNK)
        z = jax.nn.gelu(za_ref[rows, :])
        u = z[:, :A_WIDTH]
        v = z[:, A_WIDTH:]
        ms = jnp.mean(v * v, axis=-1, keepdims=True)
        vn = ((v * lax.rsqrt(ms + EPS)) * ng_ref[...]).astype(_BF16)
        for g in range(A_GROUPS):
            sl = slice(g * A_GROUP_DIM, (g + 1) * A_GROUP_DIM)
            w = jnp.where(causal, w_ref[g], 0.0).astype(_BF16)
            s = jnp.dot(w, vn[:, sl], preferred_element_type=_F32) + bt_ref[:, g:g + 1]
            o_ref[rows, sl] = (u[:, sl] * s).astype(o_ref.dtype)


def _spatial_gating(za, norm_g, w_s, b_s):
    m = za.shape[0]
    tm = ROW_BLOCK["spatial_gating"]
    return pl.pallas_call(
        _gating_kernel,
        grid=(m // tm,),
        in_specs=[pl.BlockSpec((tm, 2 * A_WIDTH), lambda i: (i, 0)),
                  pl.BlockSpec((1, A_WIDTH), lambda i: (0, 0)),
                  pl.BlockSpec((A_GROUPS, CHUNK, CHUNK), lambda i: (0, 0, 0)),
                  pl.BlockSpec((CHUNK, A_GROUPS), lambda i: (0, 0))],
        out_specs=pl.BlockSpec((tm, A_WIDTH), lambda i: (i, 0)),
        out_shape=jax.ShapeDtypeStruct((m, A_WIDTH), _BF16),
        compiler_params=_compiler_params(1),
        name="spatial_gating",
    )(za, norm_g.reshape(1, A_WIDTH), w_s, b_s.T)


def _rope_tables(positions, rot_dim, period):
    half = rot_dim // 2
    inv_freq = ROPE_THETA ** (-jnp.arange(0, rot_dim, 2, dtype=_F32) / rot_dim)
    ang = positions.astype(_F32)[..., None] * inv_freq
    cos, sin = jnp.cos(ang), jnp.sin(ang)
    zeros_half = jnp.zeros_like(sin)
    rest = jnp.zeros(sin.shape[:-1] + (period - rot_dim,), _F32)

    def lanes(*pieces):
        return jnp.tile(jnp.concatenate(pieces, axis=-1), LANES // period)

    return (lanes(cos, cos, rest + 1.0), lanes(zeros_half, sin, rest),
            lanes(-sin, zeros_half, rest))


def _rope(x, c, s1, s2, half):
    return x * c + pltpu.roll(x, half, 1) * s1 + pltpu.roll(x, LANES - half, 1) * s2


def _prep_kernel(att_ref, cb_ref, s1b_ref, s2b_ref, ci_ref, s1i_ref, s2i_ref,
                 q_ref, k_ref, vt_ref, qi_ref, ki_ref, wi_ref):
    cb, s1b, s2b = cb_ref[0], s1b_ref[0], s2b_ref[0]
    ci, s1i, s2i = ci_ref[0], s1i_ref[0], s2i_ref[0]
    for h in range(B_HEADS):
        x = att_ref[0, :, ATT_Q0 + h * LANES:ATT_Q0 + (h + 1) * LANES]
        q_ref[0, :, h * LANES:(h + 1) * LANES] = _rope(x, cb, s1b, s2b, B_ROT // 2).astype(_BF16)
    for h in range(B_KV_HEADS):
        x = att_ref[0, :, ATT_K0 + h * LANES:ATT_K0 + (h + 1) * LANES]
        k_ref[0, :, h * LANES:(h + 1) * LANES] = _rope(x, cb, s1b, s2b, B_ROT // 2).astype(_BF16)
    vt_ref[0, 0] = att_ref[0, :, ATT_V0:ATT_V0 + B_KV_WIDTH].T.astype(_BF16)
    lane = lax.broadcasted_iota(jnp.int32, ci.shape, 1)
    is_ki = lane < IDX_DIM
    heads_per_group = LANES // IDX_DIM
    for j in range(IDX_WIDTH // LANES):
        x = att_ref[0, :, ATT_QI0 + j * LANES:ATT_QI0 + (j + 1) * LANES]
        y = _rope(x, ci, s1i, s2i, IDX_ROT // 2)
        for r in range(heads_per_group):
            h = j * heads_per_group + r
            yr = y if r == 0 else pltpu.roll(y, LANES - r * IDX_DIM, 1)
            qi_ref[0, :, h * LANES:(h + 1) * LANES] = jnp.where(is_ki, yr, 0.0).astype(_BF16)
    x = att_ref[0, :, ATT_KW0:ATT_KW0 + LANES]
    y = _rope(x, jnp.where(is_ki, ci, 1.0), jnp.where(is_ki, s1i, 0.0),
              jnp.where(is_ki, s2i, 0.0), IDX_ROT // 2)
    ki_ref[0] = jnp.where(is_ki, y, 0.0).astype(_BF16)
    wi_ref[0] = y[:, IDX_DIM:IDX_DIM + IDX_HEADS]


def _prep_side_job(att, tables_b, tables_i, n_steps):
    bsz, seq, _ = att.shape
    ts = bsz * seq // n_steps
    assert ts * n_steps == bsz * seq and seq % ts == 0 and KEY_CHUNK % ts == 0
    assert ts % (2 * SUBLANES) == 0
    per_seq = seq // ts
    per_chunk = KEY_CHUNK // ts

    def tokens(width):
        return pl.BlockSpec((1, ts, width), lambda s: (s // per_seq, s % per_seq, 0))

    def out(width, dtype):
        return tokens(width), jax.ShapeDtypeStruct((bsz, seq, width), dtype)

    vt_out = (pl.BlockSpec((1, 1, B_KV_WIDTH, ts),
                           lambda s: (s // per_seq, (s % per_seq) // per_chunk, 0,
                                      (s % per_seq) % per_chunk)),
              jax.ShapeDtypeStruct((bsz, seq // KEY_CHUNK, B_KV_WIDTH, KEY_CHUNK), _BF16))
    outs = [out(B_WIDTH, _BF16), out(B_KV_WIDTH, _BF16), vt_out,
            out(IDX_HEADS * LANES, _BF16), out(LANES, _BF16), out(IDX_HEADS, _F32)]
    return SideJob(_prep_kernel, (att, *tables_b, *tables_i),
                   (tokens(ATT_WIDTH),) + (tokens(LANES),) * 6,
                   tuple(o[0] for o in outs), tuple(o[1] for o in outs))


def _ordered_int_to_float(key):
    bits = jnp.where(key < 0, key ^ jnp.int32(0x7FFFFFFF), key)
    return lax.bitcast_convert_type(bits, _F32)


def _fold_rows(x, op):
    while x.shape[0] > SUBLANES:
        half = x.shape[0] // 2
        x = op(x[:half], x[half:])
    return x


def _dsa_kernel(*refs, n_sel):
    seq = refs[4].shape[1]
    n_chunks = (pl.program_id(1) * Q_BLOCK + Q_BLOCK + KEY_CHUNK - 1) // KEY_CHUNK
    for n in range(1, seq // KEY_CHUNK + 1):
        pl.when(n_chunks == n)(functools.partial(_dsa_block, *refs, n_sel=n_sel, n_chunks=n))


def _dsa_block(qi_ref, wit_ref, ki_ref, q_ref, k_ref, vt_ref, o_ref,
               score_ref, bias_ref, logit_ref, acc_ref, *, n_sel, n_chunks):
    t = Q_BLOCK
    blk = pl.program_id(1)
    qpos = blk * t + lax.broadcasted_iota(jnp.int32, (1, t), 1)
    kiota = lax.broadcasted_iota(jnp.int32, (KEY_CHUNK, 1), 0)
    idx_scale = (IDX_DIM ** -0.5) * (IDX_HEADS ** -0.5)

    def rows(c):
        return slice(c * KEY_CHUNK, (c + 1) * KEY_CHUNK)

    def for_chunks(body, carry):
        for c in range(n_chunks):
            carry = body(c, carry)
        return carry

    def indexer_chunk(c, carry):
        ki = ki_ref[0, rows(c), :]
        acc = jnp.zeros((KEY_CHUNK, t), _F32)
        for h in range(0, IDX_HEADS, 2):
            qpair = jnp.concatenate([qi_ref[0, :, h * LANES:(h + 1) * LANES],
                                     qi_ref[0, :, (h + 1) * LANES:(h + 2) * LANES]], axis=0)
            d = lax.dot_general(ki, qpair, _NT_DIMS, preferred_element_type=_F32)
            acc = acc + wit_ref[0, h:h + 1, :] * jnp.maximum(d[:, :t], 0.0)
            acc = acc + wit_ref[0, h + 1:h + 2, :] * jnp.maximum(d[:, t:], 0.0)
        kpos = c * KEY_CHUNK + kiota
        score_ref[rows(c), :] = jnp.where(kpos <= qpos, acc * idx_scale + 0.0, _NEG_INF)
        return carry

    for_chunks(indexer_chunk, 0)

    def count(indicator):
        def chunk(c, part):
            return part + _fold_rows(indicator(score_ref[rows(c), :], c), jnp.add)
        part = for_chunks(chunk, jnp.zeros((SUBLANES, t), _F32))
        return jnp.sum(part, axis=0, keepdims=True)

    def count_ge(cand_f):
        return count(lambda s, c: jnp.where(s >= cand_f, 1.0, 0.0))

    def search():
        int_min = jnp.int32(-2 ** 31)
        base = jnp.where(count_ge(jnp.zeros((1, t), _F32)) >= n_sel, jnp.int32(0), int_min)

        def search_step(it, base):
            cand = base | lax.shift_left(jnp.int32(1), jnp.int32(30) - it)
            cnt = count_ge(_ordered_int_to_float(cand))
            return jnp.where(cnt >= n_sel, cand, base)

        return lax.fori_loop(0, 31, search_step, base)

    if (n_chunks - 1) * KEY_CHUNK >= n_sel:
        base = search()
    else:
        base = lax.cond((blk + 1) * t <= n_sel, lambda: jnp.zeros((1, t), jnp.int32), search)
    thr = jnp.where(qpos + 1 <= n_sel, _NEG_INF, _ordered_int_to_float(base))

    def bias_chunk(c, part):
        s = score_ref[rows(c), :]
        sel = jnp.where(c * KEY_CHUNK + kiota <= qpos, jnp.where(s >= thr, 1.0, 0.0), 0.0)
        bias_ref[rows(c), :] = jnp.where(sel > 0.0, 0.0, _NEG_INF)
        return part + _fold_rows(sel, jnp.add)

    n_ge = jnp.sum(for_chunks(bias_chunk, jnp.zeros((SUBLANES, t), _F32)), axis=0, keepdims=True)

    @pl.when(jnp.max(n_ge) > n_sel)
    def _():
        nxt = _ordered_int_to_float(base + 1)
        tied = n_ge > n_sel
        n_above = count(lambda s, c: jnp.where(s >= nxt, 1.0, 0.0))
        need = jnp.where(tied, n_sel - n_above, 0.0)
        no_index = float(k_ref.shape[1])
        front_v = jnp.where(tied, jnp.inf, _NEG_INF)
        front_i = jnp.where(tied, -1.0, no_index)

        def beyond(s, kposf, front_v, front_i):
            after = jnp.where(s < front_v, 1.0,
                              jnp.where(s == front_v, jnp.where(kposf > front_i, 1.0, 0.0), 0.0))
            return jnp.where(s >= thr, jnp.where(s < nxt, after, 0.0), 0.0)

        def kposf(c):
            return (c * KEY_CHUNK + kiota).astype(_F32)

        def advance(_, carry):
            front_v, front_i, need = carry

            def best_value(c, part):
                s = score_ref[rows(c), :]
                cand = jnp.where(beyond(s, kposf(c), front_v, front_i) > 0.0, s, _NEG_INF)
                return jnp.maximum(part, _fold_rows(cand, jnp.maximum))

            v = jnp.max(for_chunks(best_value, jnp.full((SUBLANES, t), _NEG_INF, _F32)),
                        axis=0, keepdims=True)

            def first_index(c, part):
                s = score_ref[rows(c), :]
                hit = jnp.where(s == v, beyond(s, kposf(c), front_v, front_i), 0.0)
                cand = jnp.where(hit > 0.0, kposf(c), no_index)
                return jnp.minimum(part, _fold_rows(cand, jnp.minimum))

            i = jnp.min(for_chunks(first_index, jnp.full((SUBLANES, t), no_index, _F32)),
                        axis=0, keepdims=True)
            active = need > 0.0
            return (jnp.where(active, v, front_v), jnp.where(active, i, front_i),
                    jnp.where(active, need - 1.0, need))

        front_v, front_i, _ = lax.fori_loop(0, jnp.max(need).astype(jnp.int32), advance,
                                            (front_v, front_i, need))

        def tie_chunk(c, carry):
            s = score_ref[rows(c), :]
            upto = jnp.where(s > front_v, 0.0,
                             jnp.where(s == front_v,
                                       jnp.where(kposf(c) <= front_i, 0.0, _NEG_INF), _NEG_INF))
            keep = jnp.where(s >= nxt, 0.0, jnp.where(s >= thr, upto, _NEG_INF))
            bias_ref[rows(c), :] = jnp.where(c * KEY_CHUNK + kiota <= qpos, keep, _NEG_INF)
            return carry

        for_chunks(tie_chunk, 0)

    grp = B_HEADS // B_KV_HEADS
    gw = grp * t
    exp2_scale = (B_HEAD_DIM ** -0.5) * 1.4426950408889634

    def logit_chunk(c, mx):
        bias = jnp.concatenate([bias_ref[rows(c), :]] * grp, axis=1)
        folded = []
        for g in range(B_KV_HEADS):
            qg = jnp.concatenate(
                [q_ref[0, :, (g * grp + hh) * B_HEAD_DIM:(g * grp + hh + 1) * B_HEAD_DIM]
                 for hh in range(grp)], axis=0)
            l = lax.dot_general(k_ref[0, rows(c), g * B_HEAD_DIM:(g + 1) * B_HEAD_DIM], qg,
                                _NT_DIMS, preferred_element_type=_F32) + bias
            logit_ref[rows(c), g * gw:(g + 1) * gw] = l
            folded.append(_fold_rows(l, jnp.maximum))
        return jnp.maximum(mx, jnp.concatenate(folded, axis=1))

    mx = for_chunks(logit_chunk, jnp.full((SUBLANES, B_HEADS * t), _NEG_INF, _F32))
    mx = jnp.max(mx, axis=0, keepdims=True)

    acc_ref[...] = jnp.zeros_like(acc_ref)
    ones_rows = jnp.ones((acc_ref.shape[0] - B_HEAD_DIM, KEY_CHUNK), _BF16)

    def pv_chunk(c, carry):
        for g in range(B_KV_HEADS):
            cols = slice(g * gw, (g + 1) * gw)
            p = jnp.exp2((logit_ref[rows(c), cols] - mx[:, cols]) * exp2_scale)
            vt_ext = jnp.concatenate(
                [vt_ref[0, c, g * B_HEAD_DIM:(g + 1) * B_HEAD_DIM, :], ones_rows], axis=0)
            acc_ref[:, cols] += jnp.dot(vt_ext, p.astype(_BF16), preferred_element_type=_F32)
        return carry

    for_chunks(pv_chunk, 0)
    out_t = acc_ref[:B_HEAD_DIM, :] / acc_ref[B_HEAD_DIM:B_HEAD_DIM + 1, :]
    for h in range(B_HEADS):
        o_ref[0, :, h * B_HEAD_DIM:(h + 1) * B_HEAD_DIM] = (
            out_t[:, h * t:(h + 1) * t].T.astype(o_ref.dtype))


def _dsa_attention(q, k, vt, qi, ki, wit):
    bsz, seq, _ = q.shape
    assert Q_BLOCK == LANES and seq % KEY_CHUNK == 0
    n_sel = min(TOPK_MAX, seq // 4)

    def q_spec(width):
        return pl.BlockSpec((1, Q_BLOCK, width), lambda b, i: (b, i, 0))

    def kv_spec(width):
        return pl.BlockSpec((1, seq, width), lambda b, i: (b, 0, 0))

    return pl.pallas_call(
        functools.partial(_dsa_kernel, n_sel=n_sel),
        grid=(bsz, seq // Q_BLOCK),
        in_specs=[q_spec(IDX_HEADS * LANES),
                  pl.BlockSpec((1, IDX_HEADS, Q_BLOCK), lambda b, i: (b, 0, i)),
                  kv_spec(LANES), q_spec(B_WIDTH), kv_spec(B_KV_WIDTH),
                  pl.BlockSpec((1, seq // KEY_CHUNK, B_KV_WIDTH, KEY_CHUNK),
                               lambda b, i: (b, 0, 0, 0))],
        out_specs=q_spec(B_WIDTH),
        out_shape=jax.ShapeDtypeStruct((bsz, seq, B_WIDTH), _BF16),
        scratch_shapes=[pltpu.VMEM((seq, Q_BLOCK), _F32), pltpu.VMEM((seq, Q_BLOCK), _F32),
                        pltpu.VMEM((seq, B_HEADS * Q_BLOCK), _F32),
                        pltpu.VMEM((B_HEAD_DIM + 2 * SUBLANES, B_HEADS * Q_BLOCK), _F32)],
        compiler_params=_compiler_params(2),
        name="dsa_attention",
    )(qi, wit, ki, q, k, vt)


def _xattn_kernel(q_ref, k_ref, v_ref, o_ref):
    scale = X_HEAD_DIM ** -0.5
    for h in range(X_HEADS):
        hs = slice(h * X_HEAD_DIM, (h + 1) * X_HEAD_DIM)
        logits = lax.dot_general(q_ref[0, :, hs], k_ref[0, :, hs], _NT_DIMS,
                                 preferred_element_type=_F32) * scale
        mx = jnp.max(logits, axis=-1, keepdims=True)
        p = jnp.exp(logits - mx)
        denom = jnp.sum(p, axis=-1, keepdims=True)
        o = jnp.dot(p.astype(_BF16), v_ref[0, :, hs], preferred_element_type=_F32)
        o_ref[0, :, hs] = (o / denom).astype(o_ref.dtype)


def _cross_attention(q, k, v):
    bsz, seq, _ = q.shape
    tq = ROW_BLOCK["cross_attention"]
    m = k.shape[1]
    mem_spec = pl.BlockSpec((1, m, X_WIDTH), lambda b, i: (b, 0, 0))
    return pl.pallas_call(
        _xattn_kernel,
        grid=(bsz, seq // tq),
        in_specs=[pl.BlockSpec((1, tq, X_WIDTH), lambda b, i: (b, i, 0)), mem_spec, mem_spec],
        out_specs=pl.BlockSpec((1, tq, X_WIDTH), lambda b, i: (b, i, 0)),
        out_shape=jax.ShapeDtypeStruct((bsz, seq, X_WIDTH), _BF16),
        compiler_params=_compiler_params(2),
        name="cross_attention",
    )(q, k, v)


def _norm_proj_kernel(x_ref, g_ref, w_ref, o_ref, w_bf16_ref):
    @pl.when(pl.program_id(0) == 0)
    def _():
        w_bf16_ref[...] = w_ref[...].astype(_BF16)

    x = x_ref[...]
    ms = jnp.mean(x * x, axis=-1, keepdims=True)
    h = ((x * lax.rsqrt(ms + EPS)) * g_ref[...]).astype(_BF16)
    o_ref[...] = jnp.dot(h, w_bf16_ref[...], preferred_element_type=_F32).astype(o_ref.dtype)


def _norm_proj(x, g, w_stack, layer, out_dtype, name):
    m, d = x.shape
    tm = ROW_BLOCK["norm_proj"]
    n = w_stack.shape[2]
    return pl.pallas_call(
        _norm_proj_kernel,
        grid=(m // tm,),
        in_specs=[pl.BlockSpec((tm, d), lambda i: (i, 0)),
                  pl.BlockSpec((1, d), lambda i: (0, 0)),
                  pl.BlockSpec((None, d, n), lambda i: (layer, 0, 0),
                               pipeline_mode=pl.Buffered(1))],
        out_specs=pl.BlockSpec((tm, n), lambda i: (i, 0)),
        out_shape=jax.ShapeDtypeStruct((m, n), out_dtype),
        scratch_shapes=[pltpu.VMEM((d, n), _BF16)],
        compiler_params=_compiler_params(1),
        name=name,
    )(x, g.reshape(1, d), w_stack)


def _xattn_out_kernel(o_ref, w_ref, x_ref, g_ref, x_out_ref, h_out_ref, w_bf16_ref):
    @pl.when(pl.program_id(0) == 0)
    def _():
        w_bf16_ref[...] = w_ref[...].astype(_BF16)

    x = x_ref[...] + jnp.dot(o_ref[...], w_bf16_ref[...], preferred_element_type=_F32)
    x_out_ref[...] = x
    ms = jnp.mean(x * x, axis=-1, keepdims=True)
    h_out_ref[...] = ((x * lax.rsqrt(ms + EPS)) * g_ref[...]).astype(h_out_ref.dtype)


def _xattn_out_and_norm(o, w_stack, layer, x, g):
    m, k = o.shape
    tm = ROW_BLOCK["xattn_out_norm"]
    d = x.shape[1]
    row_spec = pl.BlockSpec((tm, d), lambda i: (i, 0))
    return pl.pallas_call(
        _xattn_out_kernel,
        grid=(m // tm,),
        in_specs=[pl.BlockSpec((tm, k), lambda i: (i, 0)),
                  pl.BlockSpec((None, k, d), lambda i: (layer, 0, 0),
                               pipeline_mode=pl.Buffered(1)),
                  row_spec,
                  pl.BlockSpec((1, d), lambda i: (0, 0))],
        out_specs=[row_spec, row_spec],
        out_shape=[jax.ShapeDtypeStruct((m, d), _F32), jax.ShapeDtypeStruct((m, d), _BF16)],
        scratch_shapes=[pltpu.VMEM((k, d), _BF16)],
        compiler_params=_compiler_params(1),
        name="xattn_out_norm",
    )(o, w_stack, x, g.reshape(1, d))


def kernel(x, mem, positions, norm_mix_g, w_in, a_norm_g, a_spatial_w, a_spatial_b, p_a, p_b,
           w_out, norm_x_g, norm_mem_g, xq_w, xk_w, xv_w, xo_w, norm_ffn_g, ffn_w1, ffn_w3,
           ffn_w2, final_norm_g):
    bsz, seq, d = x.shape
    m = bsz * seq
    depth = w_in.shape[0]
    ffn_hidden = ffn_w1.shape[-1]
    tables_b = _rope_tables(positions, B_ROT, B_HEAD_DIM)
    tables_i = _rope_tables(positions, IDX_ROT, IDX_DIM)

    xf = x.reshape(m, d)
    mem_f = mem.reshape(bsz * MEM_LEN, d)
    w_in_t = jnp.transpose(w_in, (0, 2, 1))
    for l in range(depth):
        def w_in_cols(col0):
            return Rhs(w_in_t, l, col0, transposed=True)

        h = _rmsnorm(xf, norm_mix_g[l], _BF16)
        att = _fused_matmul("proj_att", [h], [(0, w_in_cols(ZA_END))], [], _ep_identity,
                            ATT_WIDTH, _F32)
        za, q, k, vt, qi, ki, wi = _fused_matmul(
            "proj_za", [h], [(0, w_in_cols(0))], [], _ep_identity, ZA_END, _F32,
            side_job=functools.partial(_prep_side_job, att.reshape(bsz, seq, ATT_WIDTH),
                                       tables_b, tables_i))
        y_a = _spatial_gating(za, a_norm_g[l], a_spatial_w[l], a_spatial_b[l])
        y_b = _dsa_attention(q, k, vt, qi, ki, wi.transpose(0, 2, 1)).reshape(m, B_WIDTH)
        merged = _fused_matmul(
            "gated_merge", [y_a, y_b, h],
            [(0, Rhs(p_a, l)), (1, Rhs(p_b, l)), (2, w_in_cols(ATT_END)),
             (2, w_in_cols(ATT_END + d))],
            [], _ep_gated_merge, d, _BF16)
        x1 = _fused_matmul("out_proj", [merged], [(0, Rhs(w_out, l))], [xf], _ep_residual,
                           d, _F32)

        qx = _norm_proj(x1, norm_x_g[l], xq_w, l, _BF16, "xattn_q")
        kx = _norm_proj(mem_f, norm_mem_g[l], xk_w, l, _BF16, "xattn_k")
        vx = _norm_proj(mem_f, norm_mem_g[l], xv_w, l, _BF16, "xattn_v")
        ox = _cross_attention(qx.reshape(bsz, seq, X_WIDTH),
                              kx.reshape(bsz, MEM_LEN, X_WIDTH),
                              vx.reshape(bsz, MEM_LEN, X_WIDTH)).reshape(m, X_WIDTH)
        x2, h2 = _xattn_out_and_norm(ox, xo_w, l, x1, norm_ffn_g[l])

        act, w2_bf16 = _fused_matmul("ffn_up", [h2],
                                     [(0, Rhs(ffn_w1, l)), (0, Rhs(ffn_w3, l))],
                                     [], _ep_swiglu, ffn_hidden, _BF16,
                                     side_job=functools.partial(_cast_side_job, ffn_w2, l))
        xf = _fused_matmul("ffn_down", [act], [(0, Rhs(w2_bf16))], [x2], _ep_residual, d, _F32)

    return _rmsnorm(xf, final_norm_g, _F32).reshape(bsz, seq, d)
```

```python
import functools
from typing import Callable, NamedTuple, Optional

import jax
import jax.numpy as jnp
from jax import lax
from jax.experimental import pallas as pl
from jax.experimental.pallas import tpu as pltpu

D_MODEL = 4096
MEM_LEN = 256
EPS = 1e-6
ROPE_THETA = 500000.0
CHUNK = 128
A_GROUPS = 16
A_WIDTH = D_MODEL // 2
A_GROUP_DIM = A_WIDTH // A_GROUPS
B_HEADS = 16
B_HEAD_DIM = 128
B_KV_HEADS = 4
B_WIDTH = B_HEADS * B_HEAD_DIM
B_KV_WIDTH = B_KV_HEADS * B_HEAD_DIM
B_ROT = B_HEAD_DIM // 4
IDX_HEADS = 16
IDX_DIM = 64
IDX_WIDTH = IDX_HEADS * IDX_DIM
IDX_ROT = IDX_DIM // 4
TOPK_MAX = 256
Q_BLOCK = 128
X_HEADS = 4
X_HEAD_DIM = 256
X_WIDTH = X_HEADS * X_HEAD_DIM
IN_SIZES = (2 * A_WIDTH, B_WIDTH, B_KV_WIDTH, B_KV_WIDTH, IDX_WIDTH, IDX_DIM, IDX_HEADS,
            2 * D_MODEL)

LANES = 128
SUBLANES = 8
KEY_CHUNK = 512
PROJ_TN = 512
ZA_END = IN_SIZES[0]
ATT_END = ZA_END + sum(IN_SIZES[1:7])
ATT_Q0 = 0
ATT_K0 = ATT_Q0 + B_WIDTH
ATT_V0 = ATT_K0 + B_KV_WIDTH
ATT_QI0 = ATT_V0 + B_KV_WIDTH
ATT_KW0 = ATT_QI0 + IDX_WIDTH
ATT_WIDTH = -(-(ATT_END - ZA_END) // PROJ_TN) * PROJ_TN
assert ZA_END % PROJ_TN == 0 and ATT_KW0 + LANES <= ATT_WIDTH

VMEM_LIMIT_BYTES = 60 * 1024 * 1024


class Tile(NamedTuple):
    tm: int
    tn: int
    prefetch_lhs: bool = False
    row_splits: int = 1


TILES = {
    "proj_att": Tile(1024, PROJ_TN, prefetch_lhs=True),
    "proj_za": Tile(1024, 512, prefetch_lhs=True),
    "gated_merge": Tile(1024, 256, row_splits=4),
    "out_proj": Tile(1024, 512, prefetch_lhs=True),
    "ffn_up": Tile(2048, 256, row_splits=4),
    "ffn_down": Tile(512, 512, prefetch_lhs=True),
}
ROW_BLOCK = {
    "rmsnorm": 512,
    "spatial_gating": 4 * CHUNK,
    "cross_attention": 1024,
    "norm_proj": 512,
    "xattn_out_norm": 256,
}

_BF16 = jnp.bfloat16
_F32 = jnp.float32
_NEG_INF = float("-inf")
_NN_DIMS = (((1,), (0,)), ((), ()))
_NT_DIMS = (((1,), (1,)), ((), ()))


def _compiler_params(n_axes):
    return pltpu.CompilerParams(
        dimension_semantics=("arbitrary",) * n_axes,
        vmem_limit_bytes=VMEM_LIMIT_BYTES,
    )


def _rmsnorm_kernel(x_ref, g_ref, o_ref):
    x = x_ref[...]
    ms = jnp.mean(x * x, axis=-1, keepdims=True)
    y = x * lax.rsqrt(ms + EPS)
    o_ref[...] = (y * g_ref[...]).astype(o_ref.dtype)


def _rmsnorm(x, g, out_dtype):
    m, d = x.shape
    tm = ROW_BLOCK["rmsnorm"]
    return pl.pallas_call(
        _rmsnorm_kernel,
        grid=(m // tm,),
        in_specs=[pl.BlockSpec((tm, d), lambda i: (i, 0)),
                  pl.BlockSpec((1, d), lambda i: (0, 0))],
        out_specs=pl.BlockSpec((tm, d), lambda i: (i, 0)),
        out_shape=jax.ShapeDtypeStruct((m, d), out_dtype),
        compiler_params=_compiler_params(1),
        name="rmsnorm",
    )(x, g.reshape(1, d))


def _ep_identity(accs, extras):
    return accs[0]


def _ep_residual(accs, extras):
    return extras[0] + accs[0]


def _ep_gated_merge(accs, extras):
    y_a, y_b, gate_a, gate_b = accs
    return jax.nn.sigmoid(gate_a) * y_a + jax.nn.sigmoid(gate_b) * y_b


def _ep_swiglu(accs, extras):
    return jax.nn.silu(accs[0]) * accs[1]


class Rhs(NamedTuple):
    array: jax.Array
    layer: Optional[int] = None
    col0: int = 0
    transposed: bool = False


def _rhs_spec(r, tn):
    if r.transposed:
        k = r.array.shape[2]
        if r.col0 % tn == 0:
            return pl.BlockSpec((None, tn, k), lambda i, j: (r.layer, j + r.col0 // tn, 0))
        assert r.col0 % SUBLANES == 0
        return pl.BlockSpec((pl.Element(1), pl.Element(tn), pl.Element(k)),
                            lambda i, j: (r.layer, (r.col0 // SUBLANES + j * (tn // SUBLANES))
                                          * SUBLANES, 0))
    assert r.col0 % tn == 0
    off = r.col0 // tn
    if r.layer is None:
        return pl.BlockSpec((r.array.shape[0], tn), lambda i, j: (0, j + off))
    return pl.BlockSpec((None, r.array.shape[1], tn), lambda i, j: (r.layer, 0, j + off))


class SideJob(NamedTuple):
    body: Callable
    arrays: tuple
    in_specs: tuple
    out_specs: tuple
    out_shape: tuple


def _cast_body(x_ref, o_ref):
    o_ref[...] = x_ref[...].astype(o_ref.dtype)


def _cast_side_job(w_stack, layer, n_steps):
    _, rows, cols = w_stack.shape
    slab = rows // n_steps
    assert slab * n_steps == rows and slab % (2 * SUBLANES) == 0
    return SideJob(_cast_body, (w_stack,),
                   (pl.BlockSpec((None, slab, cols), lambda s: (layer, s, 0)),),
                   (pl.BlockSpec((slab, cols), lambda s: (s, 0)),),
                   (jax.ShapeDtypeStruct((rows, cols), _BF16),))


def _mm_kernel(*refs, n_lhs, pair_lhs, rhs_transposed, n_extra, epilogue, side_body, n_side_in,
               row_splits):
    n_rhs = len(pair_lhs)
    lhs_refs = refs[:n_lhs]
    rhs_refs = refs[n_lhs:n_lhs + n_rhs]
    n_in = n_lhs + n_rhs + n_extra
    extra_refs = refs[n_lhs + n_rhs:n_in]
    o_ref = refs[n_in + n_side_in]
    if side_body is not None:
        side_body(*refs[n_in:n_in + n_side_in], *refs[n_in + n_side_in + 1:])
    weights = [(r[0] if len(r.shape) == 3 else r[...]).astype(_BF16) for r in rhs_refs]
    rows_per_split = o_ref.shape[0] // row_splits
    for s in range(row_splits):
        rows = slice(s * rows_per_split, (s + 1) * rows_per_split)
        accs = [lax.dot_general(lhs_refs[li][rows, :], w, _NT_DIMS if tr else _NN_DIMS,
                                preferred_element_type=_F32)
                for li, w, tr in zip(pair_lhs, weights, rhs_transposed)]
        o_ref[rows, :] = epilogue(accs, [e[rows, :] for e in extra_refs]).astype(o_ref.dtype)


def _fused_matmul(name, lhs, pairs, extras, epilogue, n_out, out_dtype, side_job=None):
    tm, tn, prefetch_lhs, row_splits = TILES[name]
    m = lhs[0].shape[0]
    assert m % tm == 0 and n_out % tn == 0
    grid = (m // tm, n_out // tn)
    lhs_mode = {} if prefetch_lhs else {"pipeline_mode": pl.Buffered(1)}
    in_specs = [pl.BlockSpec((tm, a.shape[1]), lambda i, j: (i, 0), **lhs_mode) for a in lhs]
    in_specs += [_rhs_spec(r, tn) for _, r in pairs]
    in_specs += [pl.BlockSpec((tm, tn), lambda i, j: (i, j)) for _ in extras]
    out_specs = [pl.BlockSpec((tm, tn), lambda i, j: (i, j))]
    out_shape = [jax.ShapeDtypeStruct((m, n_out), out_dtype)]
    operands = list(lhs) + [p[1].array for p in pairs] + list(extras)
    side = None if side_job is None else side_job(grid[0] * grid[1])
    if side is not None:
        def per_step(spec):
            return pl.BlockSpec(spec.block_shape,
                                lambda i, j: spec.index_map(i * grid[1] + j))
        in_specs += [per_step(s) for s in side.in_specs]
        out_specs += [per_step(s) for s in side.out_specs]
        out_shape += list(side.out_shape)
        operands += list(side.arrays)
    kern = functools.partial(_mm_kernel, n_lhs=len(lhs), pair_lhs=tuple(p[0] for p in pairs),
                             rhs_transposed=tuple(p[1].transposed for p in pairs),
                             n_extra=len(extras), epilogue=epilogue,
                             side_body=None if side is None else side.body,
                             n_side_in=0 if side is None else len(side.in_specs),
                             row_splits=row_splits)
    outs = pl.pallas_call(
        kern,
        grid=grid,
        in_specs=in_specs,
        out_specs=out_specs,
        out_shape=out_shape,
        compiler_params=_compiler_params(2),
        name=name,
    )(*operands)
    return outs[0] if side is None else tuple(outs)


def _gating_kernel(za_ref, ng_ref, w_ref, bt_ref, o_ref):
    row = lax.broadcasted_iota(jnp.int32, (CHUNK, CHUNK), 0)
    col = lax.broadcasted_iota(jnp.int32, (CHUNK, CHUNK), 1)
    causal = col <= row
    for c in range(za_ref.shape[0] // CHUNK):
        rows = slice(c * CHUNK, (c + 1) * CHUNK)
        z = jax.nn.gelu(za_ref[rows, :])
        u = z[:, :A_WIDTH]
        v = z[:, A_WIDTH:]
        ms = jnp.mean(v * v, axis=-1, keepdims=True)
        vn = ((v * lax.rsqrt(ms + EPS)) * ng_ref[...]).astype(_BF16)
        for g in range(A_GROUPS):
            sl = slice(g * A_GROUP_DIM, (g + 1) * A_GROUP_DIM)
            w = jnp.where(causal, w_ref[g], 0.0).astype(_BF16)
            s = jnp.dot(w, vn[:, sl], preferred_element_type=_F32) + bt_ref[:, g:g + 1]
            o_ref[rows, sl] = (u[:, sl] * s).astype(o_ref.dtype)


def _spatial_gating(za, norm_g, w_s, b_s):
    m = za.shape[0]
    tm = ROW_BLOCK["spatial_gating"]
    return pl.pallas_call(
        _gating_kernel,
        grid=(m // tm,),
        in_specs=[pl.BlockSpec((tm, 2 * A_WIDTH), lambda i: (i, 0)),
                  pl.BlockSpec((1, A_WIDTH), lambda i: (0, 0)),
                  pl.BlockSpec((A_GROUPS, CHUNK, CHUNK), lambda i: (0, 0, 0)),
                  pl.BlockSpec((CHUNK, A_GROUPS), lambda i: (0, 0))],
        out_specs=pl.BlockSpec((tm, A_WIDTH), lambda i: (i, 0)),
        out_shape=jax.ShapeDtypeStruct((m, A_WIDTH), _BF16),
        compiler_params=_compiler_params(1),
        name="spatial_gating",
    )(za, norm_g.reshape(1, A_WIDTH), w_s, b_s.T)


def _rope_tables(positions, rot_dim, period):
    half = rot_dim // 2
    inv_freq = ROPE_THETA ** (-jnp.arange(0, rot_dim, 2, dtype=_F32) / rot_dim)
    ang = positions.astype(_F32)[..., None] * inv_freq
    cos, sin = jnp.cos(ang), jnp.sin(ang)
    zeros_half = jnp.zeros_like(sin)
    rest = jnp.zeros(sin.shape[:-1] + (period - rot_dim,), _F32)

    def lanes(*pieces):
        return jnp.tile(jnp.concatenate(pieces, axis=-1), LANES // period)

    return (lanes(cos, cos, rest + 1.0), lanes(zeros_half, sin, rest),
            lanes(-sin, zeros_half, rest))


def _rope(x, c, s1, s2, half):
    return x * c + pltpu.roll(x, half, 1) * s1 + pltpu.roll(x, LANES - half, 1) * s2


def _prep_kernel(att_ref, cb_ref, s1b_ref, s2b_ref, ci_ref, s1i_ref, s2i_ref,
                 q_ref, k_ref, vt_ref, qi_ref, ki_ref, wit_ref):
    cb, s1b, s2b = cb_ref[0], s1b_ref[0], s2b_ref[0]
    ci, s1i, s2i = ci_ref[0], s1i_ref[0], s2i_ref[0]
    for h in range(B_HEADS):
        x = att_ref[0, :, ATT_Q0 + h * LANES:ATT_Q0 + (h + 1) * LANES]
        q_ref[0, :, h * LANES:(h + 1) * LANES] = _rope(x, cb, s1b, s2b, B_ROT // 2).astype(_BF16)
    for h in range(B_KV_HEADS):
        x = att_ref[0, :, ATT_K0 + h * LANES:ATT_K0 + (h + 1) * LANES]
        k_ref[0, :, h * LANES:(h + 1) * LANES] = _rope(x, cb, s1b, s2b, B_ROT // 2).astype(_BF16)
    vt_ref[0, 0] = att_ref[0, :, ATT_V0:ATT_V0 + B_KV_WIDTH].T.astype(_BF16)
    lane = lax.broadcasted_iota(jnp.int32, ci.shape, 1)
    is_ki = lane < IDX_DIM
    heads_per_group = LANES // IDX_DIM
    for j in range(IDX_WIDTH // LANES):
        x = att_ref[0, :, ATT_QI0 + j * LANES:ATT_QI0 + (j + 1) * LANES]
        y = _rope(x, ci, s1i, s2i, IDX_ROT // 2)
        for r in range(heads_per_group):
            h = j * heads_per_group + r
            yr = y if r == 0 else pltpu.roll(y, LANES - r * IDX_DIM, 1)
            qi_ref[0, :, h * LANES:(h + 1) * LANES] = jnp.where(is_ki, yr, 0.0).astype(_BF16)
    x = att_ref[0, :, ATT_KW0:ATT_KW0 + LANES]
    y = _rope(x, jnp.where(is_ki, ci, 1.0), jnp.where(is_ki, s1i, 0.0),
              jnp.where(is_ki, s2i, 0.0), IDX_ROT // 2)
    ki_ref[0] = jnp.where(is_ki, y, 0.0).astype(_BF16)
    wit_ref[0] = y.T[IDX_DIM:IDX_DIM + IDX_HEADS, :]


def _prep_side_job(att, tables_b, tables_i, n_steps):
    bsz, seq, _ = att.shape
    ts = bsz * seq // n_steps
    assert ts * n_steps == bsz * seq and seq % ts == 0 and KEY_CHUNK % ts == 0
    assert ts % (2 * SUBLANES) == 0
    per_seq = seq // ts
    per_chunk = KEY_CHUNK // ts

    def tokens(width):
        return pl.BlockSpec((1, ts, width), lambda s: (s // per_seq, s % per_seq, 0))

    def out(width, dtype):
        return tokens(width), jax.ShapeDtypeStruct((bsz, seq, width), dtype)

    vt_out = (pl.BlockSpec((1, 1, B_KV_WIDTH, ts),
                           lambda s: (s // per_seq, (s % per_seq) // per_chunk, 0,
                                      (s % per_seq) % per_chunk)),
              jax.ShapeDtypeStruct((bsz, seq // KEY_CHUNK, B_KV_WIDTH, KEY_CHUNK), _BF16))
    wit_out = (pl.BlockSpec((1, IDX_HEADS, ts), lambda s: (s // per_seq, 0, s % per_seq)),
               jax.ShapeDtypeStruct((bsz, IDX_HEADS, seq), _F32))
    outs = [out(B_WIDTH, _BF16), out(B_KV_WIDTH, _BF16), vt_out,
            out(IDX_HEADS * LANES, _BF16), out(LANES, _BF16), wit_out]
    return SideJob(_prep_kernel, (att, *tables_b, *tables_i),
                   (tokens(ATT_WIDTH),) + (tokens(LANES),) * 6,
                   tuple(o[0] for o in outs), tuple(o[1] for o in outs))


def _ordered_int_to_float(key):
    bits = jnp.where(key < 0, key ^ jnp.int32(0x7FFFFFFF), key)
    return lax.bitcast_convert_type(bits, _F32)


def _fold_rows(x, op):
    while x.shape[0] > SUBLANES:
        half = x.shape[0] // 2
        x = op(x[:half], x[half:])
    return x


def _dsa_kernel(*refs, n_sel):
    seq = refs[4].shape[1]
    n_chunks = (pl.program_id(1) * Q_BLOCK + Q_BLOCK + KEY_CHUNK - 1) // KEY_CHUNK
    for n in range(1, seq // KEY_CHUNK + 1):
        pl.when(n_chunks == n)(functools.partial(_dsa_block, *refs, n_sel=n_sel, n_chunks=n))


def _dsa_block(qi_ref, wit_ref, ki_ref, q_ref, k_ref, vt_ref, o_ref,
               score_ref, bias_ref, logit_ref, acc_ref, *, n_sel, n_chunks):
    t = Q_BLOCK
    blk = pl.program_id(1)
    qpos = blk * t + lax.broadcasted_iota(jnp.int32, (1, t), 1)
    kiota = lax.broadcasted_iota(jnp.int32, (KEY_CHUNK, 1), 0)
    idx_scale = (IDX_DIM ** -0.5) * (IDX_HEADS ** -0.5)

    def rows(c):
        return slice(c * KEY_CHUNK, (c + 1) * KEY_CHUNK)

    def for_chunks(body, carry):
        for c in range(n_chunks):
            carry = body(c, carry)
        return carry

    def indexer_chunk(c, carry):
        ki = ki_ref[0, rows(c), :]
        acc = jnp.zeros((KEY_CHUNK, t), _F32)
        for h in range(0, IDX_HEADS, 2):
            qpair = jnp.concatenate([qi_ref[0, :, h * LANES:(h + 1) * LANES],
                                     qi_ref[0, :, (h + 1) * LANES:(h + 2) * LANES]], axis=0)
            d = lax.dot_general(ki, qpair, _NT_DIMS, preferred_element_type=_F32)
            acc = acc + wit_ref[0, h:h + 1, :] * jnp.maximum(d[:, :t], 0.0)
            acc = acc + wit_ref[0, h + 1:h + 2, :] * jnp.maximum(d[:, t:], 0.0)
        kpos = c * KEY_CHUNK + kiota
        score_ref[rows(c), :] = jnp.where(kpos <= qpos, acc * idx_scale + 0.0, _NEG_INF)
        return carry

    for_chunks(indexer_chunk, 0)

    def count(indicator):
        def chunk(c, part):
            return part + _fold_rows(indicator(score_ref[rows(c), :], c), jnp.add)
        part = for_chunks(chunk, jnp.zeros((SUBLANES, t), _F32))
        return jnp.sum(part, axis=0, keepdims=True)

    def count_ge(cand_f):
        return count(lambda s, c: jnp.where(s >= cand_f, 1.0, 0.0))

    def search():
        int_min = jnp.int32(-2 ** 31)
        base = jnp.where(count_ge(jnp.zeros((1, t), _F32)) >= n_sel, jnp.int32(0), int_min)

        def search_step(it, base):
            cand = base | lax.shift_left(jnp.int32(1), jnp.int32(30) - it)
            cnt = count_ge(_ordered_int_to_float(cand))
            return jnp.where(cnt >= n_sel, cand, base)

        return lax.fori_loop(0, 31, search_step, base)

    if (n_chunks - 1) * KEY_CHUNK >= n_sel:
        base = search()
    else:
        base = lax.cond((blk + 1) * t <= n_sel, lambda: jnp.zeros((1, t), jnp.int32), search)
    thr = jnp.where(qpos + 1 <= n_sel, _NEG_INF, _ordered_int_to_float(base))

    def bias_chunk(c, part):
        s = score_ref[rows(c), :]
        sel = jnp.where(c * KEY_CHUNK + kiota <= qpos, jnp.where(s >= thr, 1.0, 0.0), 0.0)
        bias_ref[rows(c), :] = jnp.where(sel > 0.0, 0.0, _NEG_INF)
        return part + _fold_rows(sel, jnp.add)

    n_ge = jnp.sum(for_chunks(bias_chunk, jnp.zeros((SUBLANES, t), _F32)), axis=0, keepdims=True)

    @pl.when(jnp.max(n_ge) > n_sel)
    def _():
        nxt = _ordered_int_to_float(base + 1)
        tied = n_ge > n_sel
        n_above = count(lambda s, c: jnp.where(s >= nxt, 1.0, 0.0))
        need = jnp.where(tied, n_sel - n_above, 0.0)
        no_index = float(k_ref.shape[1])
        front_v = jnp.where(tied, jnp.inf, _NEG_INF)
        front_i = jnp.where(tied, -1.0, no_index)

        def beyond(s, kposf, front_v, front_i):
            after = jnp.where(s < front_v, 1.0,
                              jnp.where(s == front_v, jnp.where(kposf > front_i, 1.0, 0.0), 0.0))
            return jnp.where(s >= thr, jnp.where(s < nxt, after, 0.0), 0.0)

        def kposf(c):
            return (c * KEY_CHUNK + kiota).astype(_F32)

        def advance(_, carry):
            front_v, front_i, need = carry

            def best_value(c, part):
                s = score_ref[rows(c), :]
                cand = jnp.where(beyond(s, kposf(c), front_v, front_i) > 0.0, s, _NEG_INF)
                return jnp.maximum(part, _fold_rows(cand, jnp.maximum))

            v = jnp.max(for_chunks(best_value, jnp.full((SUBLANES, t), _NEG_INF, _F32)),
                        axis=0, keepdims=True)

            def first_index(c, part):
                s = score_ref[rows(c), :]
                hit = jnp.where(s == v, beyond(s, kposf(c), front_v, front_i), 0.0)
                cand = jnp.where(hit > 0.0, kposf(c), no_index)
                return jnp.minimum(part, _fold_rows(cand, jnp.minimum))

            i = jnp.min(for_chunks(first_index, jnp.full((SUBLANES, t), no_index, _F32)),
                        axis=0, keepdims=True)
            active = need > 0.0
            return (jnp.where(active, v, front_v), jnp.where(active, i, front_i),
                    jnp.where(active, need - 1.0, need))

        front_v, front_i, _ = lax.fori_loop(0, jnp.max(need).astype(jnp.int32), advance,
                                            (front_v, front_i, need))

        def tie_chunk(c, carry):
            s = score_ref[rows(c), :]
            upto = jnp.where(s > front_v, 0.0,
                             jnp.where(s == front_v,
                                       jnp.where(kposf(c) <= front_i, 0.0, _NEG_INF), _NEG_INF))
            keep = jnp.where(s >= nxt, 0.0, jnp.where(s >= thr, upto, _NEG_INF))
            bias_ref[rows(c), :] = jnp.where(c * KEY_CHUNK + kiota <= qpos, keep, _NEG_INF)
            return carry

        for_chunks(tie_chunk, 0)

    grp = B_HEADS // B_KV_HEADS
    gw = grp * t
    exp2_scale = (B_HEAD_DIM ** -0.5) * 1.4426950408889634

    def logit_chunk(c, mx):
        bias = jnp.concatenate([bias_ref[rows(c), :]] * grp, axis=1)
        folded = []
        for g in range(B_KV_HEADS):
            qg = jnp.concatenate(
                [q_ref[0, :, (g * grp + hh) * B_HEAD_DIM:(g * grp + hh + 1) * B_HEAD_DIM]
                 for hh in range(grp)], axis=0)
            l = lax.dot_general(k_ref[0, rows(c), g * B_HEAD_DIM:(g + 1) * B_HEAD_DIM], qg,
                                _NT_DIMS, preferred_element_type=_F32) + bias
            logit_ref[rows(c), g * gw:(g + 1) * gw] = l
            folded.append(_fold_rows(l, jnp.maximum))
        return jnp.maximum(mx, jnp.concatenate(folded, axis=1))

    mx = for_chunks(logit_chunk, jnp.full((SUBLANES, B_HEADS * t), _NEG_INF, _F32))
    mx = jnp.max(mx, axis=0, keepdims=True)

    acc_ref[...] = jnp.zeros_like(acc_ref)
    ones_rows = jnp.ones((acc_ref.shape[0] - B_HEAD_DIM, KEY_CHUNK), _BF16)

    def pv_chunk(c, carry):
        for g in range(B_KV_HEADS):
            cols = slice(g * gw, (g + 1) * gw)
            p = jnp.exp2((logit_ref[rows(c), cols] - mx[:, cols]) * exp2_scale)
            vt_ext = jnp.concatenate(
                [vt_ref[0, c, g * B_HEAD_DIM:(g + 1) * B_HEAD_DIM, :], ones_rows], axis=0)
            acc_ref[:, cols] += jnp.dot(vt_ext, p.astype(_BF16), preferred_element_type=_F32)
        return carry

    for_chunks(pv_chunk, 0)
    out_t = acc_ref[:B_HEAD_DIM, :] / acc_ref[B_HEAD_DIM:B_HEAD_DIM + 1, :]
    for h in range(B_HEADS):
        o_ref[0, :, h * B_HEAD_DIM:(h + 1) * B_HEAD_DIM] = (
            out_t[:, h * t:(h + 1) * t].T.astype(o_ref.dtype))


def _dsa_attention(q, k, vt, qi, ki, wit):
    bsz, seq, _ = q.shape
    assert Q_BLOCK == LANES and seq % KEY_CHUNK == 0
    n_sel = min(TOPK_MAX, seq // 4)

    def q_spec(width):
        return pl.BlockSpec((1, Q_BLOCK, width), lambda b, i: (b, i, 0))

    def kv_spec(width):
        return pl.BlockSpec((1, seq, width), lambda b, i: (b, 0, 0))

    return pl.pallas_call(
        functools.partial(_dsa_kernel, n_sel=n_sel),
        grid=(bsz, seq // Q_BLOCK),
        in_specs=[q_spec(IDX_HEADS * LANES),
                  pl.BlockSpec((1, IDX_HEADS, Q_BLOCK), lambda b, i: (b, 0, i)),
                  kv_spec(LANES), q_spec(B_WIDTH), kv_spec(B_KV_WIDTH),
                  pl.BlockSpec((1, seq // KEY_CHUNK, B_KV_WIDTH, KEY_CHUNK),
                               lambda b, i: (b, 0, 0, 0))],
        out_specs=q_spec(B_WIDTH),
        out_shape=jax.ShapeDtypeStruct((bsz, seq, B_WIDTH), _BF16),
        scratch_shapes=[pltpu.VMEM((seq, Q_BLOCK), _F32), pltpu.VMEM((seq, Q_BLOCK), _F32),
                        pltpu.VMEM((seq, B_HEADS * Q_BLOCK), _F32),
                        pltpu.VMEM((B_HEAD_DIM + 2 * SUBLANES, B_HEADS * Q_BLOCK), _F32)],
        compiler_params=_compiler_params(2),
        name="dsa_attention",
    )(qi, wit, ki, q, k, vt)


def _xattn_kernel(q_ref, k_ref, v_ref, o_ref):
    scale = X_HEAD_DIM ** -0.5
    for h in range(X_HEADS):
        hs = slice(h * X_HEAD_DIM, (h + 1) * X_HEAD_DIM)
        logits = lax.dot_general(q_ref[0, :, hs], k_ref[0, :, hs], _NT_DIMS,
                                 preferred_element_type=_F32) * scale
        mx = jnp.max(logits, axis=-1, keepdims=True)
        p = jnp.exp(logits - mx)
        denom = jnp.sum(p, axis=-1, keepdims=True)
        o = jnp.dot(p.astype(_BF16), v_ref[0, :, hs], preferred_element_type=_F32)
        o_ref[0, :, hs] = (o / denom).astype(o_ref.dtype)


def _cross_attention(q, k, v):
    bsz, seq, _ = q.shape
    tq = ROW_BLOCK["cross_attention"]
    m = k.shape[1]
    mem_spec = pl.BlockSpec((1, m, X_WIDTH), lambda b, i: (b, 0, 0))
    return pl.pallas_call(
        _xattn_kernel,
        grid=(bsz, seq // tq),
        in_specs=[pl.BlockSpec((1, tq, X_WIDTH), lambda b, i: (b, i, 0)), mem_spec, mem_spec],
        out_specs=pl.BlockSpec((1, tq, X_WIDTH), lambda b, i: (b, i, 0)),
        out_shape=jax.ShapeDtypeStruct((bsz, seq, X_WIDTH), _BF16),
        compiler_params=_compiler_params(2),
        name="cross_attention",
    )(q, k, v)


def _norm_proj_kernel(x_ref, g_ref, w_ref, o_ref, w_bf16_ref):
    @pl.when(pl.program_id(0) == 0)
    def _():
        w_bf16_ref[...] = w_ref[...].astype(_BF16)

    x = x_ref[...]
    ms = jnp.mean(x * x, axis=-1, keepdims=True)
    h = ((x * lax.rsqrt(ms + EPS)) * g_ref[...]).astype(_BF16)
    o_ref[...] = jnp.dot(h, w_bf16_ref[...], preferred_element_type=_F32).astype(o_ref.dtype)


def _norm_proj(x, g, w_stack, layer, out_dtype, name):
    m, d = x.shape
    tm = ROW_BLOCK["norm_proj"]
    n = w_stack.shape[2]
    return pl.pallas_call(
        _norm_proj_kernel,
        grid=(m // tm,),
        in_specs=[pl.BlockSpec((tm, d), lambda i: (i, 0)),
                  pl.BlockSpec((1, d), lambda i: (0, 0)),
                  pl.BlockSpec((None, d, n), lambda i: (layer, 0, 0),
                               pipeline_mode=pl.Buffered(1))],
        out_specs=pl.BlockSpec((tm, n), lambda i: (i, 0)),
        out_shape=jax.ShapeDtypeStruct((m, n), out_dtype),
        scratch_shapes=[pltpu.VMEM((d, n), _BF16)],
        compiler_params=_compiler_params(1),
        name=name,
    )(x, g.reshape(1, d), w_stack)


def _xattn_out_kernel(o_ref, w_ref, x_ref, g_ref, x_out_ref, h_out_ref, w_bf16_ref):
    @pl.when(pl.program_id(0) == 0)
    def _():
        w_bf16_ref[...] = w_ref[...].astype(_BF16)

    x = x_ref[...] + jnp.dot(o_ref[...], w_bf16_ref[...], preferred_element_type=_F32)
    x_out_ref[...] = x
    ms = jnp.mean(x * x, axis=-1, keepdims=True)
    h_out_ref[...] = ((x * lax.rsqrt(ms + EPS)) * g_ref[...]).astype(h_out_ref.dtype)


def _xattn_out_and_norm(o, w_stack, layer, x, g):
    m, k = o.shape
    tm = ROW_BLOCK["xattn_out_norm"]
    d = x.shape[1]
    row_spec = pl.BlockSpec((tm, d), lambda i: (i, 0))
    return pl.pallas_call(
        _xattn_out_kernel,
        grid=(m // tm,),
        in_specs=[pl.BlockSpec((tm, k), lambda i: (i, 0)),
                  pl.BlockSpec((None, k, d), lambda i: (layer, 0, 0),
                               pipeline_mode=pl.Buffered(1)),
                  row_spec,
                  pl.BlockSpec((1, d), lambda i: (0, 0))],
        out_specs=[row_spec, row_spec],
        out_shape=[jax.ShapeDtypeStruct((m, d), _F32), jax.ShapeDtypeStruct((m, d), _BF16)],
        scratch_shapes=[pltpu.VMEM((k, d), _BF16)],
        compiler_params=_compiler_params(1),
        name="xattn_out_norm",
    )(o, w_stack, x, g.reshape(1, d))


def kernel(x, mem, positions, norm_mix_g, w_in, a_norm_g, a_spatial_w, a_spatial_b, p_a, p_b,
           w_out, norm_x_g, norm_mem_g, xq_w, xk_w, xv_w, xo_w, norm_ffn_g, ffn_w1, ffn_w3,
           ffn_w2, final_norm_g):
    bsz, seq, d = x.shape
    m = bsz * seq
    depth = w_in.shape[0]
    ffn_hidden = ffn_w1.shape[-1]
    tables_b = _rope_tables(positions, B_ROT, B_HEAD_DIM)
    tables_i = _rope_tables(positions, IDX_ROT, IDX_DIM)

    xf = x.reshape(m, d)
    mem_f = mem.reshape(bsz * MEM_LEN, d)
    w_in_t = jnp.transpose(w_in, (0, 2, 1))
    for l in range(depth):
        def w_in_cols(col0):
            return Rhs(w_in_t, l, col0, transposed=True)

        h = _rmsnorm(xf, norm_mix_g[l], _BF16)
        att = _fused_matmul("proj_att", [h], [(0, w_in_cols(ZA_END))], [], _ep_identity,
                            ATT_WIDTH, _F32)
        za, q, k, vt, qi, ki, wit = _fused_matmul(
            "proj_za", [h], [(0, w_in_cols(0))], [], _ep_identity, ZA_END, _F32,
            side_job=functools.partial(_prep_side_job, att.reshape(bsz, seq, ATT_WIDTH),
                                       tables_b, tables_i))
        y_a = _spatial_gating(za, a_norm_g[l], a_spatial_w[l], a_spatial_b[l])
        y_b = _dsa_attention(q, k, vt, qi, ki, wit).reshape(m, B_WIDTH)
        merged = _fused_matmul(
            "gated_merge", [y_a, y_b, h],
            [(0, Rhs(p_a, l)), (1, Rhs(p_b, l)), (2, w_in_cols(ATT_END)),
             (2, w_in_cols(ATT_END + d))],
            [], _ep_gated_merge, d, _BF16)
        x1 = _fused_matmul("out_proj", [merged], [(0, Rhs(w_out, l))], [xf], _ep_residual,
                           d, _F32)

        qx = _norm_proj(x1, norm_x_g[l], xq_w, l, _BF16, "xattn_q")
        kx = _norm_proj(mem_f, norm_mem_g[l], xk_w, l, _BF16, "xattn_k")
        vx = _norm_proj(mem_f, norm_mem_g[l], xv_w, l, _BF16, "xattn_v")
        ox = _cross_attention(qx.reshape(bsz, seq, X_WIDTH),
                              kx.reshape(bsz, MEM_LEN, X_WIDTH),
                              vx.reshape(bsz, MEM_LEN, X_WIDTH)).reshape(m, X_WIDTH)
        x2, h2 = _xattn_out_and_norm(ox, xo_w, l, x1, norm_ffn_g[l])

        act, w2_bf16 = _fused_matmul("ffn_up", [h2],
                                     [(0, Rhs(ffn_w1, l)), (0, Rhs(ffn_w3, l))],
                                     [], _ep_swiglu, ffn_hidden, _BF16,
                                     side_job=functools.partial(_cast_side_job, ffn_w2, l))
        xf = _fused_matmul("ffn_down", [act], [(0, Rhs(w2_bf16))], [x2], _ep_residual, d, _F32)

    return _rmsnorm(xf, final_norm_g, _F32).reshape(bsz, seq, d)
```

```python
import functools
from typing import Callable, NamedTuple, Optional

import jax
import jax.numpy as jnp
from jax import lax
from jax.experimental import pallas as pl
from jax.experimental.pallas import tpu as pltpu

D_MODEL = 4096
MEM_LEN = 256
EPS = 1e-6
ROPE_THETA = 500000.0
CHUNK = 128
A_GROUPS = 16
A_WIDTH = D_MODEL // 2
A_GROUP_DIM = A_WIDTH // A_GROUPS
B_HEADS = 16
B_HEAD_DIM = 128
B_KV_HEADS = 4
B_WIDTH = B_HEADS * B_HEAD_DIM
B_KV_WIDTH = B_KV_HEADS * B_HEAD_DIM
B_ROT = B_HEAD_DIM // 4
IDX_HEADS = 16
IDX_DIM = 64
IDX_WIDTH = IDX_HEADS * IDX_DIM
IDX_ROT = IDX_DIM // 4
TOPK_MAX = 256
Q_BLOCK = 128
X_HEADS = 4
X_HEAD_DIM = 256
X_WIDTH = X_HEADS * X_HEAD_DIM
IN_SIZES = (2 * A_WIDTH, B_WIDTH, B_KV_WIDTH, B_KV_WIDTH, IDX_WIDTH, IDX_DIM, IDX_HEADS,
            2 * D_MODEL)

LANES = 128
SUBLANES = 8
KEY_CHUNK = 512
PROJ_TN = 512
ZA_END = IN_SIZES[0]
ATT_END = ZA_END + sum(IN_SIZES[1:7])
ATT_Q0 = 0
ATT_K0 = ATT_Q0 + B_WIDTH
ATT_V0 = ATT_K0 + B_KV_WIDTH
ATT_QI0 = ATT_V0 + B_KV_WIDTH
ATT_KW0 = ATT_QI0 + IDX_WIDTH
ATT_WIDTH = -(-(ATT_END - ZA_END) // PROJ_TN) * PROJ_TN
assert ZA_END % PROJ_TN == 0 and ATT_KW0 + LANES <= ATT_WIDTH

VMEM_LIMIT_BYTES = 60 * 1024 * 1024


class Tile(NamedTuple):
    tm: int
    tn: int
    prefetch_lhs: bool = False
    row_splits: int = 1


TILES = {
    "proj_att": Tile(1024, PROJ_TN, prefetch_lhs=True),
    "proj_za": Tile(1024, 512, prefetch_lhs=True),
    "gated_merge": Tile(1024, 256, row_splits=4),
    "out_proj": Tile(1024, 512, prefetch_lhs=True),
    "ffn_up": Tile(2048, 256, row_splits=4),
    "ffn_down": Tile(512, 512, prefetch_lhs=True),
}
ROW_BLOCK = {
    "rmsnorm": 512,
    "spatial_gating": 4 * CHUNK,
    "cross_attention": 1024,
    "norm_proj": 512,
    "xattn_out_norm": 256,
}

_BF16 = jnp.bfloat16
_F32 = jnp.float32
_NEG_INF = float("-inf")
_NN_DIMS = (((1,), (0,)), ((), ()))
_NT_DIMS = (((1,), (1,)), ((), ()))


def _compiler_params(n_axes):
    return pltpu.CompilerParams(
        dimension_semantics=("arbitrary",) * n_axes,
        vmem_limit_bytes=VMEM_LIMIT_BYTES,
    )


def _rmsnorm_kernel(x_ref, g_ref, o_ref):
    x = x_ref[...]
    ms = jnp.mean(x * x, axis=-1, keepdims=True)
    y = x * lax.rsqrt(ms + EPS)
    o_ref[...] = (y * g_ref[...]).astype(o_ref.dtype)


def _rmsnorm(x, g, out_dtype):
    m, d = x.shape
    tm = ROW_BLOCK["rmsnorm"]
    return pl.pallas_call(
        _rmsnorm_kernel,
        grid=(m // tm,),
        in_specs=[pl.BlockSpec((tm, d), lambda i: (i, 0)),
                  pl.BlockSpec((1, d), lambda i: (0, 0))],
        out_specs=pl.BlockSpec((tm, d), lambda i: (i, 0)),
        out_shape=jax.ShapeDtypeStruct((m, d), out_dtype),
        compiler_params=_compiler_params(1),
        name="rmsnorm",
    )(x, g.reshape(1, d))


def _ep_identity(accs, extras):
    return accs[0]


def _ep_residual(accs, extras):
    return extras[0] + accs[0]


def _ep_gated_merge(accs, extras):
    y_a, y_b, gate_a, gate_b = accs
    return jax.nn.sigmoid(gate_a) * y_a + jax.nn.sigmoid(gate_b) * y_b


def _ep_swiglu(accs, extras):
    return jax.nn.silu(accs[0]) * accs[1]


class Rhs(NamedTuple):
    array: jax.Array
    layer: Optional[int] = None
    col0: int = 0
    transposed: bool = False


def _rhs_spec(r, tn):
    if r.transposed:
        k = r.array.shape[2]
        if r.col0 % tn == 0:
            return pl.BlockSpec((None, tn, k), lambda i, j: (r.layer, j + r.col0 // tn, 0))
        assert r.col0 % SUBLANES == 0
        return pl.BlockSpec((pl.Element(1), pl.Element(tn), pl.Element(k)),
                            lambda i, j: (r.layer, (r.col0 // SUBLANES + j * (tn // SUBLANES))
                                          * SUBLANES, 0))
    assert r.col0 % tn == 0
    off = r.col0 // tn
    if r.layer is None:
        return pl.BlockSpec((r.array.shape[0], tn), lambda i, j: (0, j + off))
    return pl.BlockSpec((None, r.array.shape[1], tn), lambda i, j: (r.layer, 0, j + off))


class SideJob(NamedTuple):
    body: Callable
    arrays: tuple
    in_specs: tuple
    out_specs: tuple
    out_shape: tuple


def _cast_body(x_ref, o_ref):
    o_ref[...] = x_ref[...].astype(o_ref.dtype)


def _cast_side_job(w_stack, layer, n_steps):
    _, rows, cols = w_stack.shape
    slab = rows // n_steps
    assert slab * n_steps == rows and slab % (2 * SUBLANES) == 0
    return SideJob(_cast_body, (w_stack,),
                   (pl.BlockSpec((None, slab, cols), lambda s: (layer, s, 0)),),
                   (pl.BlockSpec((slab, cols), lambda s: (s, 0)),),
                   (jax.ShapeDtypeStruct((rows, cols), _BF16),))


def _mm_kernel(*refs, n_lhs, pair_lhs, rhs_transposed, n_extra, epilogue, side_body, n_side_in,
               row_splits):
    n_rhs = len(pair_lhs)
    lhs_refs = refs[:n_lhs]
    rhs_refs = refs[n_lhs:n_lhs + n_rhs]
    n_in = n_lhs + n_rhs + n_extra
    extra_refs = refs[n_lhs + n_rhs:n_in]
    o_ref = refs[n_in + n_side_in]
    if side_body is not None:
        side_body(*refs[n_in:n_in + n_side_in], *refs[n_in + n_side_in + 1:])
    weights = [(r[0] if len(r.shape) == 3 else r[...]).astype(_BF16) for r in rhs_refs]
    rows_per_split = o_ref.shape[0] // row_splits
    for s in range(row_splits):
        rows = slice(s * rows_per_split, (s + 1) * rows_per_split)
        accs = [lax.dot_general(lhs_refs[li][rows, :], w, _NT_DIMS if tr else _NN_DIMS,
                                preferred_element_type=_F32)
                for li, w, tr in zip(pair_lhs, weights, rhs_transposed)]
        o_ref[rows, :] = epilogue(accs, [e[rows, :] for e in extra_refs]).astype(o_ref.dtype)


def _fused_matmul(name, lhs, pairs, extras, epilogue, n_out, out_dtype, side_job=None):
    tm, tn, prefetch_lhs, row_splits = TILES[name]
    m = lhs[0].shape[0]
    assert m % tm == 0 and n_out % tn == 0
    grid = (m // tm, n_out // tn)
    lhs_mode = {} if prefetch_lhs else {"pipeline_mode": pl.Buffered(1)}
    in_specs = [pl.BlockSpec((tm, a.shape[1]), lambda i, j: (i, 0), **lhs_mode) for a in lhs]
    in_specs += [_rhs_spec(r, tn) for _, r in pairs]
    in_specs += [pl.BlockSpec((tm, tn), lambda i, j: (i, j)) for _ in extras]
    out_specs = [pl.BlockSpec((tm, tn), lambda i, j: (i, j))]
    out_shape = [jax.ShapeDtypeStruct((m, n_out), out_dtype)]
    operands = list(lhs) + [p[1].array for p in pairs] + list(extras)
    side = None if side_job is None else side_job(grid[0] * grid[1])
    if side is not None:
        def per_step(spec):
            return pl.BlockSpec(spec.block_shape,
                                lambda i, j: spec.index_map(i * grid[1] + j))
        in_specs += [per_step(s) for s in side.in_specs]
        out_specs += [per_step(s) for s in side.out_specs]
        out_shape += list(side.out_shape)
        operands += list(side.arrays)
    kern = functools.partial(_mm_kernel, n_lhs=len(lhs), pair_lhs=tuple(p[0] for p in pairs),
                             rhs_transposed=tuple(p[1].transposed for p in pairs),
                             n_extra=len(extras), epilogue=epilogue,
                             side_body=None if side is None else side.body,
                             n_side_in=0 if side is None else len(side.in_specs),
                             row_splits=row_splits)
    outs = pl.pallas_call(
        kern,
        grid=grid,
        in_specs=in_specs,
        out_specs=out_specs,
        out_shape=out_shape,
        compiler_params=_compiler_params(2),
        name=name,
    )(*operands)
    return outs[0] if side is None else tuple(outs)


def _gating_kernel(za_ref, ng_ref, w_ref, bt_ref, o_ref):
    row = lax.broadcasted_iota(jnp.int32, (CHUNK, CHUNK), 0)
    col = lax.broadcasted_iota(jnp.int32, (CHUNK, CHUNK), 1)
    causal = col <= row
    for c in range(za_ref.shape[0] // CHUNK):
        rows = slice(c * CHUNK, (c + 1) * CHUNK)
        z = jax.nn.gelu(za_ref[rows, :])
        u = z[:, :A_WIDTH]
        v = z[:, A_WIDTH:]
        ms = jnp.mean(v * v, axis=-1, keepdims=True)
        vn = ((v * lax.rsqrt(ms + EPS)) * ng_ref[...]).astype(_BF16)
        for g in range(A_GROUPS):
            sl = slice(g * A_GROUP_DIM, (g + 1) * A_GROUP_DIM)
            w = jnp.where(causal, w_ref[g], 0.0).astype(_BF16)
            s = jnp.dot(w, vn[:, sl], preferred_element_type=_F32) + bt_ref[:, g:g + 1]
            o_ref[rows, sl] = (u[:, sl] * s).astype(o_ref.dtype)


def _spatial_gating(za, norm_g, w_s, b_s):
    m = za.shape[0]
    tm = ROW_BLOCK["spatial_gating"]
    return pl.pallas_call(
        _gating_kernel,
        grid=(m // tm,),
        in_specs=[pl.BlockSpec((tm, 2 * A_WIDTH), lambda i: (i, 0)),
                  pl.BlockSpec((1, A_WIDTH), lambda i: (0, 0)),
                  pl.BlockSpec((A_GROUPS, CHUNK, CHUNK), lambda i: (0, 0, 0)),
                  pl.BlockSpec((CHUNK, A_GROUPS), lambda i: (0, 0))],
        out_specs=pl.BlockSpec((tm, A_WIDTH), lambda i: (i, 0)),
        out_shape=jax.ShapeDtypeStruct((m, A_WIDTH), _BF16),
        compiler_params=_compiler_params(1),
        name="spatial_gating",
    )(za, norm_g.reshape(1, A_WIDTH), w_s, b_s.T)


def _lane_inv_freq(rot_dim, period):
    inv_freq = ROPE_THETA ** (-jnp.arange(0, rot_dim, 2, dtype=_F32) / rot_dim)
    lane = jnp.arange(LANES) % period
    return jnp.take(inv_freq, lane % (rot_dim // 2)).reshape(1, LANES)


def _rope_tables(pos, inv_freq_lanes, rot_dim, period):
    half = rot_dim // 2
    ang = pos * inv_freq_lanes
    cos, sin = jnp.cos(ang), jnp.sin(ang)
    lane = lax.broadcasted_iota(jnp.int32, ang.shape, 1) & (period - 1)
    c = jnp.where(lane < rot_dim, cos, 1.0)
    s1 = jnp.where(lane >= half, jnp.where(lane < rot_dim, sin, 0.0), 0.0)
    s2 = jnp.where(lane < half, -sin, 0.0)
    return c, s1, s2


def _rope(x, c, s1, s2, half):
    return x * c + pltpu.roll(x, half, 1) * s1 + pltpu.roll(x, LANES - half, 1) * s2


def _prep_kernel(att_ref, pos_ref, fb_ref, fi_ref,
                 q_ref, k_ref, vt_ref, qi_ref, ki_ref, wit_ref):
    pos = pos_ref[0]
    cb, s1b, s2b = _rope_tables(pos, fb_ref[...], B_ROT, B_HEAD_DIM)
    ci, s1i, s2i = _rope_tables(pos, fi_ref[...], IDX_ROT, IDX_DIM)
    for h in range(B_HEADS):
        x = att_ref[0, :, ATT_Q0 + h * LANES:ATT_Q0 + (h + 1) * LANES]
        q_ref[0, :, h * LANES:(h + 1) * LANES] = _rope(x, cb, s1b, s2b, B_ROT // 2).astype(_BF16)
    for h in range(B_KV_HEADS):
        x = att_ref[0, :, ATT_K0 + h * LANES:ATT_K0 + (h + 1) * LANES]
        k_ref[0, :, h * LANES:(h + 1) * LANES] = _rope(x, cb, s1b, s2b, B_ROT // 2).astype(_BF16)
    vt_ref[0, 0] = att_ref[0, :, ATT_V0:ATT_V0 + B_KV_WIDTH].T.astype(_BF16)
    lane = lax.broadcasted_iota(jnp.int32, ci.shape, 1)
    is_ki = lane < IDX_DIM
    heads_per_group = LANES // IDX_DIM
    for j in range(IDX_WIDTH // LANES):
        x = att_ref[0, :, ATT_QI0 + j * LANES:ATT_QI0 + (j + 1) * LANES]
        y = _rope(x, ci, s1i, s2i, IDX_ROT // 2)
        for r in range(heads_per_group):
            h = j * heads_per_group + r
            yr = y if r == 0 else pltpu.roll(y, LANES - r * IDX_DIM, 1)
            qi_ref[0, :, h * LANES:(h + 1) * LANES] = jnp.where(is_ki, yr, 0.0).astype(_BF16)
    x = att_ref[0, :, ATT_KW0:ATT_KW0 + LANES]
    y = _rope(x, jnp.where(is_ki, ci, 1.0), jnp.where(is_ki, s1i, 0.0),
              jnp.where(is_ki, s2i, 0.0), IDX_ROT // 2)
    ki_ref[0] = jnp.where(is_ki, y, 0.0).astype(_BF16)
    wit_ref[0] = y.T[IDX_DIM:IDX_DIM + IDX_HEADS, :]


def _prep_side_job(att, positions, n_steps):
    bsz, seq, _ = att.shape
    ts = bsz * seq // n_steps
    assert ts * n_steps == bsz * seq and seq % ts == 0 and KEY_CHUNK % ts == 0
    assert ts % (2 * SUBLANES) == 0
    per_seq = seq // ts
    per_chunk = KEY_CHUNK // ts

    def tokens(width):
        return pl.BlockSpec((1, ts, width), lambda s: (s // per_seq, s % per_seq, 0))

    def out(width, dtype):
        return tokens(width), jax.ShapeDtypeStruct((bsz, seq, width), dtype)

    vt_out = (pl.BlockSpec((1, 1, B_KV_WIDTH, ts),
                           lambda s: (s // per_seq, (s % per_seq) // per_chunk, 0,
                                      (s % per_seq) % per_chunk)),
              jax.ShapeDtypeStruct((bsz, seq // KEY_CHUNK, B_KV_WIDTH, KEY_CHUNK), _BF16))
    wit_out = (pl.BlockSpec((1, IDX_HEADS, ts), lambda s: (s // per_seq, 0, s % per_seq)),
               jax.ShapeDtypeStruct((bsz, IDX_HEADS, seq), _F32))
    outs = [out(B_WIDTH, _BF16), out(B_KV_WIDTH, _BF16), vt_out,
            out(IDX_HEADS * LANES, _BF16), out(LANES, _BF16), wit_out]
    freq_spec = pl.BlockSpec((1, LANES), lambda s: (0, 0))
    return SideJob(_prep_kernel,
                   (att, positions.astype(_F32).reshape(bsz, seq, 1),
                    _lane_inv_freq(B_ROT, B_HEAD_DIM), _lane_inv_freq(IDX_ROT, IDX_DIM)),
                   (tokens(ATT_WIDTH), tokens(1), freq_spec, freq_spec),
                   tuple(o[0] for o in outs), tuple(o[1] for o in outs))


def _ordered_int_to_float(key):
    bits = jnp.where(key < 0, key ^ jnp.int32(0x7FFFFFFF), key)
    return lax.bitcast_convert_type(bits, _F32)


def _fold_rows(x, op):
    while x.shape[0] > SUBLANES:
        half = x.shape[0] // 2
        x = op(x[:half], x[half:])
    return x


def _dsa_kernel(*refs, n_sel):
    seq = refs[4].shape[1]
    n_chunks = (pl.program_id(1) * Q_BLOCK + Q_BLOCK + KEY_CHUNK - 1) // KEY_CHUNK
    for n in range(1, seq // KEY_CHUNK + 1):
        pl.when(n_chunks == n)(functools.partial(_dsa_block, *refs, n_sel=n_sel, n_chunks=n))


def _dsa_block(qi_ref, wit_ref, ki_ref, q_ref, k_ref, vt_ref, o_ref,
               score_ref, bias_ref, logit_ref, acc_ref, *, n_sel, n_chunks):
    t = Q_BLOCK
    blk = pl.program_id(1)
    qpos = blk * t + lax.broadcasted_iota(jnp.int32, (1, t), 1)
    kiota = lax.broadcasted_iota(jnp.int32, (KEY_CHUNK, 1), 0)
    idx_scale = (IDX_DIM ** -0.5) * (IDX_HEADS ** -0.5)

    def rows(c):
        return slice(c * KEY_CHUNK, (c + 1) * KEY_CHUNK)

    def for_chunks(body, carry):
        for c in range(n_chunks):
            carry = body(c, carry)
        return carry

    def indexer_chunk(c, carry):
        ki = ki_ref[0, rows(c), :]
        acc = jnp.zeros((KEY_CHUNK, t), _F32)
        for h in range(0, IDX_HEADS, 2):
            qpair = jnp.concatenate([qi_ref[0, :, h * LANES:(h + 1) * LANES],
                                     qi_ref[0, :, (h + 1) * LANES:(h + 2) * LANES]], axis=0)
            d = lax.dot_general(ki, qpair, _NT_DIMS, preferred_element_type=_F32)
            acc = acc + wit_ref[0, h:h + 1, :] * jnp.maximum(d[:, :t], 0.0)
            acc = acc + wit_ref[0, h + 1:h + 2, :] * jnp.maximum(d[:, t:], 0.0)
        kpos = c * KEY_CHUNK + kiota
        score_ref[rows(c), :] = jnp.where(kpos <= qpos, acc * idx_scale + 0.0, _NEG_INF)
        return carry

    for_chunks(indexer_chunk, 0)

    def count(indicator):
        def chunk(c, part):
            return part + _fold_rows(indicator(score_ref[rows(c), :], c), jnp.add)
        part = for_chunks(chunk, jnp.zeros((SUBLANES, t), _F32))
        return jnp.sum(part, axis=0, keepdims=True)

    def count_ge(cand_f):
        return count(lambda s, c: jnp.where(s >= cand_f, 1.0, 0.0))

    def search():
        int_min = jnp.int32(-2 ** 31)
        base = jnp.where(count_ge(jnp.zeros((1, t), _F32)) >= n_sel, jnp.int32(0), int_min)

        def search_step(it, base):
            cand = base | lax.shift_left(jnp.int32(1), jnp.int32(30) - it)
            cnt = count_ge(_ordered_int_to_float(cand))
            return jnp.where(cnt >= n_sel, cand, base)

        return lax.fori_loop(0, 31, search_step, base)

    if (n_chunks - 1) * KEY_CHUNK >= n_sel:
        base = search()
    else:
        base = lax.cond((blk + 1) * t <= n_sel, lambda: jnp.zeros((1, t), jnp.int32), search)
    thr = jnp.where(qpos + 1 <= n_sel, _NEG_INF, _ordered_int_to_float(base))

    def bias_chunk(c, part):
        s = score_ref[rows(c), :]
        sel = jnp.where(c * KEY_CHUNK + kiota <= qpos, jnp.where(s >= thr, 1.0, 0.0), 0.0)
        bias_ref[rows(c), :] = jnp.where(sel > 0.0, 0.0, _NEG_INF)
        return part + _fold_rows(sel, jnp.add)

    n_ge = jnp.sum(for_chunks(bias_chunk, jnp.zeros((SUBLANES, t), _F32)), axis=0, keepdims=True)

    @pl.when(jnp.max(n_ge) > n_sel)
    def _():
        nxt = _ordered_int_to_float(base + 1)
        tied = n_ge > n_sel
        n_above = count(lambda s, c: jnp.where(s >= nxt, 1.0, 0.0))
        need = jnp.where(tied, n_sel - n_above, 0.0)
        no_index = float(k_ref.shape[1])
        front_v = jnp.where(tied, jnp.inf, _NEG_INF)
        front_i = jnp.where(tied, -1.0, no_index)

        def beyond(s, kposf, front_v, front_i):
            after = jnp.where(s < front_v, 1.0,
                              jnp.where(s == front_v, jnp.where(kposf > front_i, 1.0, 0.0), 0.0))
            return jnp.where(s >= thr, jnp.where(s < nxt, after, 0.0), 0.0)

        def kposf(c):
            return (c * KEY_CHUNK + kiota).astype(_F32)

        def advance(_, carry):
            front_v, front_i, need = carry

            def best_value(c, part):
                s = score_ref[rows(c), :]
                cand = jnp.where(beyond(s, kposf(c), front_v, front_i) > 0.0, s, _NEG_INF)
                return jnp.maximum(part, _fold_rows(cand, jnp.maximum))

            v = jnp.max(for_chunks(best_value, jnp.full((SUBLANES, t), _NEG_INF, _F32)),
                        axis=0, keepdims=True)

            def first_index(c, part):
                s = score_ref[rows(c), :]
                hit = jnp.where(s == v, beyond(s, kposf(c), front_v, front_i), 0.0)
                cand = jnp.where(hit > 0.0, kposf(c), no_index)
                return jnp.minimum(part, _fold_rows(cand, jnp.minimum))

            i = jnp.min(for_chunks(first_index, jnp.full((SUBLANES, t), no_index, _F32)),
                        axis=0, keepdims=True)
            active = need > 0.0
            return (jnp.where(active, v, front_v), jnp.where(active, i, front_i),
                    jnp.where(active, need - 1.0, need))

        front_v, front_i, _ = lax.fori_loop(0, jnp.max(need).astype(jnp.int32), advance,
                                            (front_v, front_i, need))

        def tie_chunk(c, carry):
            s = score_ref[rows(c), :]
            upto = jnp.where(s > front_v, 0.0,
                             jnp.where(s == front_v,
                                       jnp.where(kposf(c) <= front_i, 0.0, _NEG_INF), _NEG_INF))
            keep = jnp.where(s >= nxt, 0.0, jnp.where(s >= thr, upto, _NEG_INF))
            bias_ref[rows(c), :] = jnp.where(c * KEY_CHUNK + kiota <= qpos, keep, _NEG_INF)
            return carry

        for_chunks(tie_chunk, 0)

    grp = B_HEADS // B_KV_HEADS
    gw = grp * t
    exp2_scale = (B_HEAD_DIM ** -0.5) * 1.4426950408889634

    def logit_chunk(c, mx):
        bias = jnp.concatenate([bias_ref[rows(c), :]] * grp, axis=1)
        folded = []
        for g in range(B_KV_HEADS):
            qg = jnp.concatenate(
                [q_ref[0, :, (g * grp + hh) * B_HEAD_DIM:(g * grp + hh + 1) * B_HEAD_DIM]
                 for hh in range(grp)], axis=0)
            l = lax.dot_general(k_ref[0, rows(c), g * B_HEAD_DIM:(g + 1) * B_HEAD_DIM], qg,
                                _NT_DIMS, preferred_element_type=_F32) + bias
            logit_ref[rows(c), g * gw:(g + 1) * gw] = l
            folded.append(_fold_rows(l, jnp.maximum))
        return jnp.maximum(mx, jnp.concatenate(folded, axis=1))

    mx = for_chunks(logit_chunk, jnp.full((SUBLANES, B_HEADS * t), _NEG_INF, _F32))
    mx = jnp.max(mx, axis=0, keepdims=True)

    acc_ref[...] = jnp.zeros_like(acc_ref)
    ones_rows = jnp.ones((acc_ref.shape[0] - B_HEAD_DIM, KEY_CHUNK), _BF16)

    def pv_chunk(c, carry):
        for g in range(B_KV_HEADS):
            cols = slice(g * gw, (g + 1) * gw)
            p = jnp.exp2((logit_ref[rows(c), cols] - mx[:, cols]) * exp2_scale)
            vt_ext = jnp.concatenate(
                [vt_ref[0, c, g * B_HEAD_DIM:(g + 1) * B_HEAD_DIM, :], ones_rows], axis=0)
            acc_ref[:, cols] += jnp.dot(vt_ext, p.astype(_BF16), preferred_element_type=_F32)
        return carry

    for_chunks(pv_chunk, 0)
    out_t = acc_ref[:B_HEAD_DIM, :] / acc_ref[B_HEAD_DIM:B_HEAD_DIM + 1, :]
    for h in range(B_HEADS):
        o_ref[0, :, h * B_HEAD_DIM:(h + 1) * B_HEAD_DIM] = (
            out_t[:, h * t:(h + 1) * t].T.astype(o_ref.dtype))


def _dsa_attention(q, k, vt, qi, ki, wit):
    bsz, seq, _ = q.shape
    assert Q_BLOCK == LANES and seq % KEY_CHUNK == 0
    n_sel = min(TOPK_MAX, seq // 4)

    def q_spec(width):
        return pl.BlockSpec((1, Q_BLOCK, width), lambda b, i: (b, i, 0))

    def kv_spec(width):
        return pl.BlockSpec((1, seq, width), lambda b, i: (b, 0, 0))

    return pl.pallas_call(
        functools.partial(_dsa_kernel, n_sel=n_sel),
        grid=(bsz, seq // Q_BLOCK),
        in_specs=[q_spec(IDX_HEADS * LANES),
                  pl.BlockSpec((1, IDX_HEADS, Q_BLOCK), lambda b, i: (b, 0, i)),
                  kv_spec(LANES), q_spec(B_WIDTH), kv_spec(B_KV_WIDTH),
                  pl.BlockSpec((1, seq // KEY_CHUNK, B_KV_WIDTH, KEY_CHUNK),
                               lambda b, i: (b, 0, 0, 0))],
        out_specs=q_spec(B_WIDTH),
        out_shape=jax.ShapeDtypeStruct((bsz, seq, B_WIDTH), _BF16),
        scratch_shapes=[pltpu.VMEM((seq, Q_BLOCK), _F32), pltpu.VMEM((seq, Q_BLOCK), _F32),
                        pltpu.VMEM((seq, B_HEADS * Q_BLOCK), _F32),
                        pltpu.VMEM((B_HEAD_DIM + 2 * SUBLANES, B_HEADS * Q_BLOCK), _F32)],
        compiler_params=_compiler_params(2),
        name="dsa_attention",
    )(qi, wit, ki, q, k, vt)


def _xattn_kernel(q_ref, k_ref, v_ref, o_ref):
    scale = X_HEAD_DIM ** -0.5
    for h in range(X_HEADS):
        hs = slice(h * X_HEAD_DIM, (h + 1) * X_HEAD_DIM)
        logits = lax.dot_general(q_ref[0, :, hs], k_ref[0, :, hs], _NT_DIMS,
                                 preferred_element_type=_F32) * scale
        mx = jnp.max(logits, axis=-1, keepdims=True)
        p = jnp.exp(logits - mx)
        denom = jnp.sum(p, axis=-1, keepdims=True)
        o = jnp.dot(p.astype(_BF16), v_ref[0, :, hs], preferred_element_type=_F32)
        o_ref[0, :, hs] = (o / denom).astype(o_ref.dtype)


def _cross_attention(q, k, v):
    bsz, seq, _ = q.shape
    tq = ROW_BLOCK["cross_attention"]
    m = k.shape[1]
    mem_spec = pl.BlockSpec((1, m, X_WIDTH), lambda b, i: (b, 0, 0))
    return pl.pallas_call(
        _xattn_kernel,
        grid=(bsz, seq // tq),
        in_specs=[pl.BlockSpec((1, tq, X_WIDTH), lambda b, i: (b, i, 0)), mem_spec, mem_spec],
        out_specs=pl.BlockSpec((1, tq, X_WIDTH), lambda b, i: (b, i, 0)),
        out_shape=jax.ShapeDtypeStruct((bsz, seq, X_WIDTH), _BF16),
        compiler_params=_compiler_params(2),
        name="cross_attention",
    )(q, k, v)


def _norm_proj_kernel(x_ref, g_ref, w_ref, o_ref, w_bf16_ref):
    @pl.when(pl.program_id(0) == 0)
    def _():
        w_bf16_ref[...] = w_ref[...].astype(_BF16)

    x = x_ref[...]
    ms = jnp.mean(x * x, axis=-1, keepdims=True)
    h = ((x * lax.rsqrt(ms + EPS)) * g_ref[...]).astype(_BF16)
    o_ref[...] = jnp.dot(h, w_bf16_ref[...], preferred_element_type=_F32).astype(o_ref.dtype)


def _norm_proj(x, g, w_stack, layer, out_dtype, name):
    m, d = x.shape
    tm = ROW_BLOCK["norm_proj"]
    n = w_stack.shape[2]
    return pl.pallas_call(
        _norm_proj_kernel,
        grid=(m // tm,),
        in_specs=[pl.BlockSpec((tm, d), lambda i: (i, 0)),
                  pl.BlockSpec((1, d), lambda i: (0, 0)),
                  pl.BlockSpec((None, d, n), lambda i: (layer, 0, 0),
                               pipeline_mode=pl.Buffered(1))],
        out_specs=pl.BlockSpec((tm, n), lambda i: (i, 0)),
        out_shape=jax.ShapeDtypeStruct((m, n), out_dtype),
        scratch_shapes=[pltpu.VMEM((d, n), _BF16)],
        compiler_params=_compiler_params(1),
        name=name,
    )(x, g.reshape(1, d), w_stack)


def _xattn_out_kernel(o_ref, w_ref, x_ref, g_ref, x_out_ref, h_out_ref, w_bf16_ref):
    @pl.when(pl.program_id(0) == 0)
    def _():
        w_bf16_ref[...] = w_ref[...].astype(_BF16)

    x = x_ref[...] + jnp.dot(o_ref[...], w_bf16_ref[...], preferred_element_type=_F32)
    x_out_ref[...] = x
    ms = jnp.mean(x * x, axis=-1, keepdims=True)
    h_out_ref[...] = ((x * lax.rsqrt(ms + EPS)) * g_ref[...]).astype(h_out_ref.dtype)


def _xattn_out_and_norm(o, w_stack, layer, x, g):
    m, k = o.shape
    tm = ROW_BLOCK["xattn_out_norm"]
    d = x.shape[1]
    row_spec = pl.BlockSpec((tm, d), lambda i: (i, 0))
    return pl.pallas_call(
        _xattn_out_kernel,
        grid=(m // tm,),
        in_specs=[pl.BlockSpec((tm, k), lambda i: (i, 0)),
                  pl.BlockSpec((None, k, d), lambda i: (layer, 0, 0),
                               pipeline_mode=pl.Buffered(1)),
                  row_spec,
                  pl.BlockSpec((1, d), lambda i: (0, 0))],
        out_specs=[row_spec, row_spec],
        out_shape=[jax.ShapeDtypeStruct((m, d), _F32), jax.ShapeDtypeStruct((m, d), _BF16)],
        scratch_shapes=[pltpu.VMEM((k, d), _BF16)],
        compiler_params=_compiler_params(1),
        name="xattn_out_norm",
    )(o, w_stack, x, g.reshape(1, d))


def kernel(x, mem, positions, norm_mix_g, w_in, a_norm_g, a_spatial_w, a_spatial_b, p_a, p_b,
           w_out, norm_x_g, norm_mem_g, xq_w, xk_w, xv_w, xo_w, norm_ffn_g, ffn_w1, ffn_w3,
           ffn_w2, final_norm_g):
    bsz, seq, d = x.shape
    m = bsz * seq
    depth = w_in.shape[0]
    ffn_hidden = ffn_w1.shape[-1]

    xf = x.reshape(m, d)
    mem_f = mem.reshape(bsz * MEM_LEN, d)
    w_in_t = jnp.transpose(w_in, (0, 2, 1))
    for l in range(depth):
        def w_in_cols(col0):
            return Rhs(w_in_t, l, col0, transposed=True)

        h = _rmsnorm(xf, norm_mix_g[l], _BF16)
        att = _fused_matmul("proj_att", [h], [(0, w_in_cols(ZA_END))], [], _ep_identity,
                            ATT_WIDTH, _F32)
        za, q, k, vt, qi, ki, wit = _fused_matmul(
            "proj_za", [h], [(0, w_in_cols(0))], [], _ep_identity, ZA_END, _F32,
            side_job=functools.partial(_prep_side_job, att.reshape(bsz, seq, ATT_WIDTH),
                                       positions))
        y_a = _spatial_gating(za, a_norm_g[l], a_spatial_w[l], a_spatial_b[l])
        y_b = _dsa_attention(q, k, vt, qi, ki, wit).reshape(m, B_WIDTH)
        merged = _fused_matmul(
            "gated_merge", [y_a, y_b, h],
            [(0, Rhs(p_a, l)), (1, Rhs(p_b, l)), (2, w_in_cols(ATT_END)),
             (2, w_in_cols(ATT_END + d))],
            [], _ep_gated_merge, d, _BF16)
        x1 = _fused_matmul("out_proj", [merged], [(0, Rhs(w_out, l))], [xf], _ep_residual,
                           d, _F32)

        qx = _norm_proj(x1, norm_x_g[l], xq_w, l, _BF16, "xattn_q")
        kx = _norm_proj(mem_f, norm_mem_g[l], xk_w, l, _BF16, "xattn_k")
        vx = _norm_proj(mem_f, norm_mem_g[l], xv_w, l, _BF16, "xattn_v")
        ox = _cross_attention(qx.reshape(bsz, seq, X_WIDTH),
                              kx.reshape(bsz, MEM_LEN, X_WIDTH),
                              vx.reshape(bsz, MEM_LEN, X_WIDTH)).reshape(m, X_WIDTH)
        x2, h2 = _xattn_out_and_norm(ox, xo_w, l, x1, norm_ffn_g[l])

        act, w2_bf16 = _fused_matmul("ffn_up", [h2],
                                     [(0, Rhs(ffn_w1, l)), (0, Rhs(ffn_w3, l))],
                                     [], _ep_swiglu, ffn_hidden, _BF16,
                                     side_job=functools.partial(_cast_side_job, ffn_w2, l))
        xf = _fused_matmul("ffn_down", [act], [(0, Rhs(w2_bf16))], [x2], _ep_residual, d, _F32)

    return _rmsnorm(xf, final_norm_g, _F32).reshape(bsz, seq, d)
```

```python
import functools
from typing import Callable, NamedTuple, Optional

import jax
import jax.numpy as jnp
from jax import lax
from jax.experimental import pallas as pl
from jax.experimental.pallas import tpu as pltpu

D_MODEL = 4096
MEM_LEN = 256
EPS = 1e-6
ROPE_THETA = 500000.0
CHUNK = 128
A_GROUPS = 16
A_WIDTH = D_MODEL // 2
A_GROUP_DIM = A_WIDTH // A_GROUPS
B_HEADS = 16
B_HEAD_DIM = 128
B_KV_HEADS = 4
B_WIDTH = B_HEADS * B_HEAD_DIM
B_KV_WIDTH = B_KV_HEADS * B_HEAD_DIM
B_ROT = B_HEAD_DIM // 4
IDX_HEADS = 16
IDX_DIM = 64
IDX_WIDTH = IDX_HEADS * IDX_DIM
IDX_ROT = IDX_DIM // 4
TOPK_MAX = 256
Q_BLOCK = 128
X_HEADS = 4
X_HEAD_DIM = 256
X_WIDTH = X_HEADS * X_HEAD_DIM
IN_SIZES = (2 * A_WIDTH, B_WIDTH, B_KV_WIDTH, B_KV_WIDTH, IDX_WIDTH, IDX_DIM, IDX_HEADS,
            2 * D_MODEL)

LANES = 128
SUBLANES = 8
KEY_CHUNK = 512
PROJ_TN = 512
ZA_END = IN_SIZES[0]
ATT_END = ZA_END + sum(IN_SIZES[1:7])
ATT_Q0 = 0
ATT_K0 = ATT_Q0 + B_WIDTH
ATT_V0 = ATT_K0 + B_KV_WIDTH
ATT_QI0 = ATT_V0 + B_KV_WIDTH
ATT_KW0 = ATT_QI0 + IDX_WIDTH
ATT_WIDTH = -(-(ATT_END - ZA_END) // PROJ_TN) * PROJ_TN
assert ZA_END % PROJ_TN == 0 and ATT_KW0 + LANES <= ATT_WIDTH

VMEM_LIMIT_BYTES = 60 * 1024 * 1024


class Tile(NamedTuple):
    tm: int
    tn: int
    prefetch_lhs: bool = False
    row_splits: int = 1


TILES = {
    "proj_att": Tile(1024, PROJ_TN, prefetch_lhs=True),
    "proj_za": Tile(1024, 512, prefetch_lhs=True),
    "gated_merge": Tile(1024, 256, row_splits=4),
    "out_proj": Tile(1024, 512, prefetch_lhs=True),
    "ffn_up": Tile(2048, 256, row_splits=4),
    "ffn_down": Tile(512, 512, prefetch_lhs=True),
}
ROW_BLOCK = {
    "rmsnorm": 512,
    "spatial_gating": 4 * CHUNK,
    "cross_attention": 1024,
    "norm_proj": 512,
    "xattn_out_norm": 256,
}

_BF16 = jnp.bfloat16
_F32 = jnp.float32
_NEG_INF = float("-inf")
_NN_DIMS = (((1,), (0,)), ((), ()))
_NT_DIMS = (((1,), (1,)), ((), ()))


def _compiler_params(n_axes):
    return pltpu.CompilerParams(
        dimension_semantics=("arbitrary",) * n_axes,
        vmem_limit_bytes=VMEM_LIMIT_BYTES,
    )


def _rmsnorm_kernel(x_ref, g_ref, o_ref):
    x = x_ref[...]
    ms = jnp.mean(x * x, axis=-1, keepdims=True)
    y = x * lax.rsqrt(ms + EPS)
    o_ref[...] = (y * g_ref[...]).astype(o_ref.dtype)


def _rmsnorm(x, g, out_dtype):
    m, d = x.shape
    tm = ROW_BLOCK["rmsnorm"]
    return pl.pallas_call(
        _rmsnorm_kernel,
        grid=(m // tm,),
        in_specs=[pl.BlockSpec((tm, d), lambda i: (i, 0)),
                  pl.BlockSpec((1, d), lambda i: (0, 0))],
        out_specs=pl.BlockSpec((tm, d), lambda i: (i, 0)),
        out_shape=jax.ShapeDtypeStruct((m, d), out_dtype),
        compiler_params=_compiler_params(1),
        name="rmsnorm",
    )(x, g.reshape(1, d))


def _ep_identity(accs, extras):
    return accs[0]


def _ep_residual(accs, extras):
    return extras[0] + accs[0]


def _ep_gated_merge(accs, extras):
    y_a, y_b, gate_a, gate_b = accs
    return jax.nn.sigmoid(gate_a) * y_a + jax.nn.sigmoid(gate_b) * y_b


def _ep_swiglu(accs, extras):
    return jax.nn.silu(accs[0]) * accs[1]


class Rhs(NamedTuple):
    array: jax.Array
    layer: Optional[int] = None
    col0: int = 0
    transposed: bool = False


def _rhs_spec(r, tn):
    if r.transposed:
        k = r.array.shape[2]
        if r.col0 % tn == 0:
            return pl.BlockSpec((None, tn, k), lambda i, j: (r.layer, j + r.col0 // tn, 0))
        assert r.col0 % SUBLANES == 0
        return pl.BlockSpec((pl.Element(1), pl.Element(tn), pl.Element(k)),
                            lambda i, j: (r.layer, (r.col0 // SUBLANES + j * (tn // SUBLANES))
                                          * SUBLANES, 0))
    assert r.col0 % tn == 0
    off = r.col0 // tn
    if r.layer is None:
        return pl.BlockSpec((r.array.shape[0], tn), lambda i, j: (0, j + off))
    return pl.BlockSpec((None, r.array.shape[1], tn), lambda i, j: (r.layer, 0, j + off))


class SideJob(NamedTuple):
    body: Callable
    arrays: tuple
    in_specs: tuple
    out_specs: tuple
    out_shape: tuple


def _cast_body(x_ref, o_ref):
    o_ref[...] = x_ref[...].astype(o_ref.dtype)


def _cast_side_job(w_stack, layer, n_steps):
    _, rows, cols = w_stack.shape
    slab = rows // n_steps
    assert slab * n_steps == rows and slab % (2 * SUBLANES) == 0
    return SideJob(_cast_body, (w_stack,),
                   (pl.BlockSpec((None, slab, cols), lambda s: (layer, s, 0)),),
                   (pl.BlockSpec((slab, cols), lambda s: (s, 0)),),
                   (jax.ShapeDtypeStruct((rows, cols), _BF16),))


def _mm_kernel(*refs, n_lhs, pair_lhs, rhs_transposed, n_extra, epilogue, side_body, n_side_in,
               row_splits):
    n_rhs = len(pair_lhs)
    lhs_refs = refs[:n_lhs]
    rhs_refs = refs[n_lhs:n_lhs + n_rhs]
    n_in = n_lhs + n_rhs + n_extra
    extra_refs = refs[n_lhs + n_rhs:n_in]
    o_ref = refs[n_in + n_side_in]
    if side_body is not None:
        side_body(*refs[n_in:n_in + n_side_in], *refs[n_in + n_side_in + 1:])
    weights = [(r[0] if len(r.shape) == 3 else r[...]).astype(_BF16) for r in rhs_refs]
    rows_per_split = o_ref.shape[0] // row_splits
    for s in range(row_splits):
        rows = slice(s * rows_per_split, (s + 1) * rows_per_split)
        accs = [lax.dot_general(lhs_refs[li][rows, :], w, _NT_DIMS if tr else _NN_DIMS,
                                preferred_element_type=_F32)
                for li, w, tr in zip(pair_lhs, weights, rhs_transposed)]
        o_ref[rows, :] = epilogue(accs, [e[rows, :] for e in extra_refs]).astype(o_ref.dtype)


def _fused_matmul(name, lhs, pairs, extras, epilogue, n_out, out_dtype, side_job=None):
    tm, tn, prefetch_lhs, row_splits = TILES[name]
    m = lhs[0].shape[0]
    assert m % tm == 0 and n_out % tn == 0
    grid = (m // tm, n_out // tn)
    lhs_mode = {} if prefetch_lhs else {"pipeline_mode": pl.Buffered(1)}
    in_specs = [pl.BlockSpec((tm, a.shape[1]), lambda i, j: (i, 0), **lhs_mode) for a in lhs]
    in_specs += [_rhs_spec(r, tn) for _, r in pairs]
    in_specs += [pl.BlockSpec((tm, tn), lambda i, j: (i, j)) for _ in extras]
    out_specs = [pl.BlockSpec((tm, tn), lambda i, j: (i, j))]
    out_shape = [jax.ShapeDtypeStruct((m, n_out), out_dtype)]
    operands = list(lhs) + [p[1].array for p in pairs] + list(extras)
    side = None if side_job is None else side_job(grid[0] * grid[1])
    if side is not None:
        def per_step(spec):
            return pl.BlockSpec(spec.block_shape,
                                lambda i, j: spec.index_map(i * grid[1] + j))
        in_specs += [per_step(s) for s in side.in_specs]
        out_specs += [per_step(s) for s in side.out_specs]
        out_shape += list(side.out_shape)
        operands += list(side.arrays)
    kern = functools.partial(_mm_kernel, n_lhs=len(lhs), pair_lhs=tuple(p[0] for p in pairs),
                             rhs_transposed=tuple(p[1].transposed for p in pairs),
                             n_extra=len(extras), epilogue=epilogue,
                             side_body=None if side is None else side.body,
                             n_side_in=0 if side is None else len(side.in_specs),
                             row_splits=row_splits)
    outs = pl.pallas_call(
        kern,
        grid=grid,
        in_specs=in_specs,
        out_specs=out_specs,
        out_shape=out_shape,
        compiler_params=_compiler_params(2),
        name=name,
    )(*operands)
    return outs[0] if side is None else tuple(outs)


def _gating_kernel(za_ref, ng_ref, w_ref, bt_ref, o_ref):
    row = lax.broadcasted_iota(jnp.int32, (CHUNK, CHUNK), 0)
    col = lax.broadcasted_iota(jnp.int32, (CHUNK, CHUNK), 1)
    causal = col <= row
    for c in range(za_ref.shape[0] // CHUNK):
        rows = slice(c * CHUNK, (c + 1) * CHUNK)
        z = jax.nn.gelu(za_ref[rows, :])
        u = z[:, :A_WIDTH]
        v = z[:, A_WIDTH:]
        ms = jnp.mean(v * v, axis=-1, keepdims=True)
        vn = ((v * lax.rsqrt(ms + EPS)) * ng_ref[...]).astype(_BF16)
        for g in range(A_GROUPS):
            sl = slice(g * A_GROUP_DIM, (g + 1) * A_GROUP_DIM)
            w = jnp.where(causal, w_ref[g], 0.0).astype(_BF16)
            s = jnp.dot(w, vn[:, sl], preferred_element_type=_F32) + bt_ref[:, g:g + 1]
            o_ref[rows, sl] = (u[:, sl] * s).astype(o_ref.dtype)


def _spatial_gating(za, norm_g, w_s, b_s):
    m = za.shape[0]
    tm = ROW_BLOCK["spatial_gating"]
    return pl.pallas_call(
        _gating_kernel,
        grid=(m // tm,),
        in_specs=[pl.BlockSpec((tm, 2 * A_WIDTH), lambda i: (i, 0)),
                  pl.BlockSpec((1, A_WIDTH), lambda i: (0, 0)),
                  pl.BlockSpec((A_GROUPS, CHUNK, CHUNK), lambda i: (0, 0, 0)),
                  pl.BlockSpec((CHUNK, A_GROUPS), lambda i: (0, 0))],
        out_specs=pl.BlockSpec((tm, A_WIDTH), lambda i: (i, 0)),
        out_shape=jax.ShapeDtypeStruct((m, A_WIDTH), _BF16),
        compiler_params=_compiler_params(1),
        name="spatial_gating",
    )(za, norm_g.reshape(1, A_WIDTH), w_s, b_s.T)


def _lane_inv_freq(rot_dim, period):
    inv_freq = ROPE_THETA ** (-jnp.arange(0, rot_dim, 2, dtype=_F32) / rot_dim)
    head = jnp.concatenate([inv_freq, inv_freq, jnp.zeros((period - rot_dim,), _F32)])
    return jnp.tile(head, LANES // period).reshape(1, LANES)


def _rope_tables(pos, inv_freq_lanes, rot_dim, period):
    half = rot_dim // 2
    ang = pos * inv_freq_lanes
    cos, sin = jnp.cos(ang), jnp.sin(ang)
    lane = lax.broadcasted_iota(jnp.int32, ang.shape, 1) & (period - 1)
    c = jnp.where(lane < rot_dim, cos, 1.0)
    s1 = jnp.where(lane >= half, jnp.where(lane < rot_dim, sin, 0.0), 0.0)
    s2 = jnp.where(lane < half, -sin, 0.0)
    return c, s1, s2


def _rope(x, c, s1, s2, half):
    return x * c + pltpu.roll(x, half, 1) * s1 + pltpu.roll(x, LANES - half, 1) * s2


def _prep_kernel(att_ref, pos_ref, fb_ref, fi_ref,
                 q_ref, k_ref, vt_ref, qi_ref, ki_ref, wit_ref):
    pos = pos_ref[0]
    cb, s1b, s2b = _rope_tables(pos, fb_ref[...], B_ROT, B_HEAD_DIM)
    ci, s1i, s2i = _rope_tables(pos, fi_ref[...], IDX_ROT, IDX_DIM)
    for h in range(B_HEADS):
        x = att_ref[0, :, ATT_Q0 + h * LANES:ATT_Q0 + (h + 1) * LANES]
        q_ref[0, :, h * LANES:(h + 1) * LANES] = _rope(x, cb, s1b, s2b, B_ROT // 2).astype(_BF16)
    for h in range(B_KV_HEADS):
        x = att_ref[0, :, ATT_K0 + h * LANES:ATT_K0 + (h + 1) * LANES]
        k_ref[0, :, h * LANES:(h + 1) * LANES] = _rope(x, cb, s1b, s2b, B_ROT // 2).astype(_BF16)
    vt_ref[0, 0] = att_ref[0, :, ATT_V0:ATT_V0 + B_KV_WIDTH].T.astype(_BF16)
    lane = lax.broadcasted_iota(jnp.int32, ci.shape, 1)
    is_ki = lane < IDX_DIM
    heads_per_group = LANES // IDX_DIM
    for j in range(IDX_WIDTH // LANES):
        x = att_ref[0, :, ATT_QI0 + j * LANES:ATT_QI0 + (j + 1) * LANES]
        y = _rope(x, ci, s1i, s2i, IDX_ROT // 2)
        for r in range(heads_per_group):
            h = j * heads_per_group + r
            yr = y if r == 0 else pltpu.roll(y, LANES - r * IDX_DIM, 1)
            qi_ref[0, :, h * LANES:(h + 1) * LANES] = jnp.where(is_ki, yr, 0.0).astype(_BF16)
    x = att_ref[0, :, ATT_KW0:ATT_KW0 + LANES]
    y = _rope(x, jnp.where(is_ki, ci, 1.0), jnp.where(is_ki, s1i, 0.0),
              jnp.where(is_ki, s2i, 0.0), IDX_ROT // 2)
    ki_ref[0] = jnp.where(is_ki, y, 0.0).astype(_BF16)
    wit_ref[0] = y.T[IDX_DIM:IDX_DIM + IDX_HEADS, :]


def _prep_side_job(att, positions, n_steps):
    bsz, seq, _ = att.shape
    ts = bsz * seq // n_steps
    assert ts * n_steps == bsz * seq and seq % ts == 0 and KEY_CHUNK % ts == 0
    assert ts % (2 * SUBLANES) == 0
    per_seq = seq // ts
    per_chunk = KEY_CHUNK // ts

    def tokens(width):
        return pl.BlockSpec((1, ts, width), lambda s: (s // per_seq, s % per_seq, 0))

    def out(width, dtype):
        return tokens(width), jax.ShapeDtypeStruct((bsz, seq, width), dtype)

    vt_out = (pl.BlockSpec((1, 1, B_KV_WIDTH, ts),
                           lambda s: (s // per_seq, (s % per_seq) // per_chunk, 0,
                                      (s % per_seq) % per_chunk)),
              jax.ShapeDtypeStruct((bsz, seq // KEY_CHUNK, B_KV_WIDTH, KEY_CHUNK), _BF16))
    wit_out = (pl.BlockSpec((1, IDX_HEADS, ts), lambda s: (s // per_seq, 0, s % per_seq)),
               jax.ShapeDtypeStruct((bsz, IDX_HEADS, seq), _F32))
    outs = [out(B_WIDTH, _BF16), out(B_KV_WIDTH, _BF16), vt_out,
            out(IDX_HEADS * LANES, _BF16), out(LANES, _BF16), wit_out]
    freq_spec = pl.BlockSpec((1, LANES), lambda s: (0, 0))
    return SideJob(_prep_kernel,
                   (att, positions.astype(_F32).reshape(bsz, seq, 1),
                    _lane_inv_freq(B_ROT, B_HEAD_DIM), _lane_inv_freq(IDX_ROT, IDX_DIM)),
                   (tokens(ATT_WIDTH), tokens(1), freq_spec, freq_spec),
                   tuple(o[0] for o in outs), tuple(o[1] for o in outs))


def _ordered_int_to_float(key):
    bits = jnp.where(key < 0, key ^ jnp.int32(0x7FFFFFFF), key)
    return lax.bitcast_convert_type(bits, _F32)


def _fold_rows(x, op):
    while x.shape[0] > SUBLANES:
        half = x.shape[0] // 2
        x = op(x[:half], x[half:])
    return x


def _dsa_kernel(*refs, n_sel):
    seq = refs[4].shape[1]
    n_chunks = (pl.program_id(1) * Q_BLOCK + Q_BLOCK + KEY_CHUNK - 1) // KEY_CHUNK
    for n in range(1, seq // KEY_CHUNK + 1):
        pl.when(n_chunks == n)(functools.partial(_dsa_block, *refs, n_sel=n_sel, n_chunks=n))


def _dsa_block(qi_ref, wit_ref, ki_ref, q_ref, k_ref, vt_ref, o_ref,
               score_ref, bias_ref, logit_ref, acc_ref, *, n_sel, n_chunks):
    t = Q_BLOCK
    blk = pl.program_id(1)
    qpos = blk * t + lax.broadcasted_iota(jnp.int32, (1, t), 1)
    kiota = lax.broadcasted_iota(jnp.int32, (KEY_CHUNK, 1), 0)
    idx_scale = (IDX_DIM ** -0.5) * (IDX_HEADS ** -0.5)

    def rows(c):
        return slice(c * KEY_CHUNK, (c + 1) * KEY_CHUNK)

    def for_chunks(body, carry):
        for c in range(n_chunks):
            carry = body(c, carry)
        return carry

    def indexer_chunk(c, carry):
        ki = ki_ref[0, rows(c), :]
        acc = jnp.zeros((KEY_CHUNK, t), _F32)
        for h in range(0, IDX_HEADS, 2):
            qpair = jnp.concatenate([qi_ref[0, :, h * LANES:(h + 1) * LANES],
                                     qi_ref[0, :, (h + 1) * LANES:(h + 2) * LANES]], axis=0)
            d = lax.dot_general(ki, qpair, _NT_DIMS, preferred_element_type=_F32)
            acc = acc + wit_ref[0, h:h + 1, :] * jnp.maximum(d[:, :t], 0.0)
            acc = acc + wit_ref[0, h + 1:h + 2, :] * jnp.maximum(d[:, t:], 0.0)
        kpos = c * KEY_CHUNK + kiota
        score_ref[rows(c), :] = jnp.where(kpos <= qpos, acc * idx_scale + 0.0, _NEG_INF)
        return carry

    for_chunks(indexer_chunk, 0)

    def count(indicator):
        def chunk(c, part):
            return part + _fold_rows(indicator(score_ref[rows(c), :], c), jnp.add)
        part = for_chunks(chunk, jnp.zeros((SUBLANES, t), _F32))
        return jnp.sum(part, axis=0, keepdims=True)

    def count_ge(cand_f):
        return count(lambda s, c: jnp.where(s >= cand_f, 1.0, 0.0))

    def search():
        int_min = jnp.int32(-2 ** 31)
        base = jnp.where(count_ge(jnp.zeros((1, t), _F32)) >= n_sel, jnp.int32(0), int_min)

        def search_step(it, base):
            cand = base | lax.shift_left(jnp.int32(1), jnp.int32(30) - it)
            cnt = count_ge(_ordered_int_to_float(cand))
            return jnp.where(cnt >= n_sel, cand, base)

        return lax.fori_loop(0, 31, search_step, base)

    if (n_chunks - 1) * KEY_CHUNK >= n_sel:
        base = search()
    else:
        base = lax.cond((blk + 1) * t <= n_sel, lambda: jnp.zeros((1, t), jnp.int32), search)
    thr = jnp.where(qpos + 1 <= n_sel, _NEG_INF, _ordered_int_to_float(base))

    def bias_chunk(c, part):
        s = score_ref[rows(c), :]
        sel = jnp.where(c * KEY_CHUNK + kiota <= qpos, jnp.where(s >= thr, 1.0, 0.0), 0.0)
        bias_ref[rows(c), :] = jnp.where(sel > 0.0, 0.0, _NEG_INF)
        return part + _fold_rows(sel, jnp.add)

    n_ge = jnp.sum(for_chunks(bias_chunk, jnp.zeros((SUBLANES, t), _F32)), axis=0, keepdims=True)

    @pl.when(jnp.max(n_ge) > n_sel)
    def _():
        nxt = _ordered_int_to_float(base + 1)
        tied = n_ge > n_sel
        n_above = count(lambda s, c: jnp.where(s >= nxt, 1.0, 0.0))
        need = jnp.where(tied, n_sel - n_above, 0.0)
        no_index = float(k_ref.shape[1])
        front_v = jnp.where(tied, jnp.inf, _NEG_INF)
        front_i = jnp.where(tied, -1.0, no_index)

        def beyond(s, kposf, front_v, front_i):
            after = jnp.where(s < front_v, 1.0,
                              jnp.where(s == front_v, jnp.where(kposf > front_i, 1.0, 0.0), 0.0))
            return jnp.where(s >= thr, jnp.where(s < nxt, after, 0.0), 0.0)

        def kposf(c):
            return (c * KEY_CHUNK + kiota).astype(_F32)

        def advance(_, carry):
            front_v, front_i, need = carry

            def best_value(c, part):
                s = score_ref[rows(c), :]
                cand = jnp.where(beyond(s, kposf(c), front_v, front_i) > 0.0, s, _NEG_INF)
                return jnp.maximum(part, _fold_rows(cand, jnp.maximum))

            v = jnp.max(for_chunks(best_value, jnp.full((SUBLANES, t), _NEG_INF, _F32)),
                        axis=0, keepdims=True)

            def first_index(c, part):
                s = score_ref[rows(c), :]
                hit = jnp.where(s == v, beyond(s, kposf(c), front_v, front_i), 0.0)
                cand = jnp.where(hit > 0.0, kposf(c), no_index)
                return jnp.minimum(part, _fold_rows(cand, jnp.minimum))

            i = jnp.min(for_chunks(first_index, jnp.full((SUBLANES, t), no_index, _F32)),
                        axis=0, keepdims=True)
            active = need > 0.0
            return (jnp.where(active, v, front_v), jnp.where(active, i, front_i),
                    jnp.where(active, need - 1.0, need))

        front_v, front_i, _ = lax.fori_loop(0, jnp.max(need).astype(jnp.int32), advance,
                                            (front_v, front_i, need))

        def tie_chunk(c, carry):
            s = score_ref[rows(c), :]
            upto = jnp.where(s > front_v, 0.0,
                             jnp.where(s == front_v,
                                       jnp.where(kposf(c) <= front_i, 0.0, _NEG_INF), _NEG_INF))
            keep = jnp.where(s >= nxt, 0.0, jnp.where(s >= thr, upto, _NEG_INF))
            bias_ref[rows(c), :] = jnp.where(c * KEY_CHUNK + kiota <= qpos, keep, _NEG_INF)
            return carry

        for_chunks(tie_chunk, 0)

    grp = B_HEADS // B_KV_HEADS
    gw = grp * t
    exp2_scale = (B_HEAD_DIM ** -0.5) * 1.4426950408889634

    def logit_chunk(c, mx):
        bias = jnp.concatenate([bias_ref[rows(c), :]] * grp, axis=1)
        folded = []
        for g in range(B_KV_HEADS):
            qg = jnp.concatenate(
                [q_ref[0, :, (g * grp + hh) * B_HEAD_DIM:(g * grp + hh + 1) * B_HEAD_DIM]
                 for hh in range(grp)], axis=0)
            l = lax.dot_general(k_ref[0, rows(c), g * B_HEAD_DIM:(g + 1) * B_HEAD_DIM], qg,
                                _NT_DIMS, preferred_element_type=_F32) + bias
            logit_ref[rows(c), g * gw:(g + 1) * gw] = l
            folded.append(_fold_rows(l, jnp.maximum))
        return jnp.maximum(mx, jnp.concatenate(folded, axis=1))

    mx = for_chunks(logit_chunk, jnp.full((SUBLANES, B_HEADS * t), _NEG_INF, _F32))
    mx = jnp.max(mx, axis=0, keepdims=True)

    acc_ref[...] = jnp.zeros_like(acc_ref)
    ones_rows = jnp.ones((acc_ref.shape[0] - B_HEAD_DIM, KEY_CHUNK), _BF16)

    def pv_chunk(c, carry):
        for g in range(B_KV_HEADS):
            cols = slice(g * gw, (g + 1) * gw)
            p = jnp.exp2((logit_ref[rows(c), cols] - mx[:, cols]) * exp2_scale)
            vt_ext = jnp.concatenate(
                [vt_ref[0, c, g * B_HEAD_DIM:(g + 1) * B_HEAD_DIM, :], ones_rows], axis=0)
            acc_ref[:, cols] += jnp.dot(vt_ext, p.astype(_BF16), preferred_element_type=_F32)
        return carry

    for_chunks(pv_chunk, 0)
    out_t = acc_ref[:B_HEAD_DIM, :] / acc_ref[B_HEAD_DIM:B_HEAD_DIM + 1, :]
    for h in range(B_HEADS):
        o_ref[0, :, h * B_HEAD_DIM:(h + 1) * B_HEAD_DIM] = (
            out_t[:, h * t:(h + 1) * t].T.astype(o_ref.dtype))


def _dsa_attention(q, k, vt, qi, ki, wit):
    bsz, seq, _ = q.shape
    assert Q_BLOCK == LANES and seq % KEY_CHUNK == 0
    n_sel = min(TOPK_MAX, seq // 4)

    def q_spec(width):
        return pl.BlockSpec((1, Q_BLOCK, width), lambda b, i: (b, i, 0))

    def kv_spec(width):
        return pl.BlockSpec((1, seq, width), lambda b, i: (b, 0, 0))

    return pl.pallas_call(
        functools.partial(_dsa_kernel, n_sel=n_sel),
        grid=(bsz, seq // Q_BLOCK),
        in_specs=[q_spec(IDX_HEADS * LANES),
                  pl.BlockSpec((1, IDX_HEADS, Q_BLOCK), lambda b, i: (b, 0, i)),
                  kv_spec(LANES), q_spec(B_WIDTH), kv_spec(B_KV_WIDTH),
                  pl.BlockSpec((1, seq // KEY_CHUNK, B_KV_WIDTH, KEY_CHUNK),
                               lambda b, i: (b, 0, 0, 0))],
        out_specs=q_spec(B_WIDTH),
        out_shape=jax.ShapeDtypeStruct((bsz, seq, B_WIDTH), _BF16),
        scratch_shapes=[pltpu.VMEM((seq, Q_BLOCK), _F32), pltpu.VMEM((seq, Q_BLOCK), _F32),
                        pltpu.VMEM((seq, B_HEADS * Q_BLOCK), _F32),
                        pltpu.VMEM((B_HEAD_DIM + 2 * SUBLANES, B_HEADS * Q_BLOCK), _F32)],
        compiler_params=_compiler_params(2),
        name="dsa_attention",
    )(qi, wit, ki, q, k, vt)


def _xattn_kernel(q_ref, k_ref, v_ref, o_ref):
    scale = X_HEAD_DIM ** -0.5
    for h in range(X_HEADS):
        hs = slice(h * X_HEAD_DIM, (h + 1) * X_HEAD_DIM)
        logits = lax.dot_general(q_ref[0, :, hs], k_ref[0, :, hs], _NT_DIMS,
                                 preferred_element_type=_F32) * scale
        mx = jnp.max(logits, axis=-1, keepdims=True)
        p = jnp.exp(logits - mx)
        denom = jnp.sum(p, axis=-1, keepdims=True)
        o = jnp.dot(p.astype(_BF16), v_ref[0, :, hs], preferred_element_type=_F32)
        o_ref[0, :, hs] = (o / denom).astype(o_ref.dtype)


def _cross_attention(q, k, v):
    bsz, seq, _ = q.shape
    tq = ROW_BLOCK["cross_attention"]
    m = k.shape[1]
    mem_spec = pl.BlockSpec((1, m, X_WIDTH), lambda b, i: (b, 0, 0))
    return pl.pallas_call(
        _xattn_kernel,
        grid=(bsz, seq // tq),
        in_specs=[pl.BlockSpec((1, tq, X_WIDTH), lambda b, i: (b, i, 0)), mem_spec, mem_spec],
        out_specs=pl.BlockSpec((1, tq, X_WIDTH), lambda b, i: (b, i, 0)),
        out_shape=jax.ShapeDtypeStruct((bsz, seq, X_WIDTH), _BF16),
        compiler_params=_compiler_params(2),
        name="cross_attention",
    )(q, k, v)


def _norm_proj_kernel(x_ref, g_ref, w_ref, o_ref, w_bf16_ref):
    @pl.when(pl.program_id(0) == 0)
    def _():
        w_bf16_ref[...] = w_ref[...].astype(_BF16)

    x = x_ref[...]
    ms = jnp.mean(x * x, axis=-1, keepdims=True)
    h = ((x * lax.rsqrt(ms + EPS)) * g_ref[...]).astype(_BF16)
    o_ref[...] = jnp.dot(h, w_bf16_ref[...], preferred_element_type=_F32).astype(o_ref.dtype)


def _norm_proj(x, g, w_stack, layer, out_dtype, name):
    m, d = x.shape
    tm = ROW_BLOCK["norm_proj"]
    n = w_stack.shape[2]
    return pl.pallas_call(
        _norm_proj_kernel,
        grid=(m // tm,),
        in_specs=[pl.BlockSpec((tm, d), lambda i: (i, 0)),
                  pl.BlockSpec((1, d), lambda i: (0, 0)),
                  pl.BlockSpec((None, d, n), lambda i: (layer, 0, 0),
                               pipeline_mode=pl.Buffered(1))],
        out_specs=pl.BlockSpec((tm, n), lambda i: (i, 0)),
        out_shape=jax.ShapeDtypeStruct((m, n), out_dtype),
        scratch_shapes=[pltpu.VMEM((d, n), _BF16)],
        compiler_params=_compiler_params(1),
        name=name,
    )(x, g.reshape(1, d), w_stack)


def _xattn_out_kernel(o_ref, w_ref, x_ref, g_ref, x_out_ref, h_out_ref, w_bf16_ref):
    @pl.when(pl.program_id(0) == 0)
    def _():
        w_bf16_ref[...] = w_ref[...].astype(_BF16)

    x = x_ref[...] + jnp.dot(o_ref[...], w_bf16_ref[...], preferred_element_type=_F32)
    x_out_ref[...] = x
    ms = jnp.mean(x * x, axis=-1, keepdims=True)
    h_out_ref[...] = ((x * lax.rsqrt(ms + EPS)) * g_ref[...]).astype(h_out_ref.dtype)


def _xattn_out_and_norm(o, w_stack, layer, x, g):
    m, k = o.shape
    tm = ROW_BLOCK["xattn_out_norm"]
    d = x.shape[1]
    row_spec = pl.BlockSpec((tm, d), lambda i: (i, 0))
    return pl.pallas_call(
        _xattn_out_kernel,
        grid=(m // tm,),
        in_specs=[pl.BlockSpec((tm, k), lambda i: (i, 0)),
                  pl.BlockSpec((None, k, d), lambda i: (layer, 0, 0),
                               pipeline_mode=pl.Buffered(1)),
                  row_spec,
                  pl.BlockSpec((1, d), lambda i: (0, 0))],
        out_specs=[row_spec, row_spec],
        out_shape=[jax.ShapeDtypeStruct((m, d), _F32), jax.ShapeDtypeStruct((m, d), _BF16)],
        scratch_shapes=[pltpu.VMEM((k, d), _BF16)],
        compiler_params=_compiler_params(1),
        name="xattn_out_norm",
    )(o, w_stack, x, g.reshape(1, d))


def kernel(x, mem, positions, norm_mix_g, w_in, a_norm_g, a_spatial_w, a_spatial_b, p_a, p_b,
           w_out, norm_x_g, norm_mem_g, xq_w, xk_w, xv_w, xo_w, norm_ffn_g, ffn_w1, ffn_w3,
           ffn_w2, final_norm_g):
    bsz, seq, d = x.shape
    m = bsz * seq
    depth = w_in.shape[0]
    ffn_hidden = ffn_w1.shape[-1]

    xf = x.reshape(m, d)
    mem_f = mem.reshape(bsz * MEM_LEN, d)
    w_in_t = jnp.transpose(w_in, (0, 2, 1))
    for l in range(depth):
        def w_in_cols(col0):
            return Rhs(w_in_t, l, col0, transposed=True)

        h = _rmsnorm(xf, norm_mix_g[l], _BF16)
        att = _fused_matmul("proj_att", [h], [(0, w_in_cols(ZA_END))], [], _ep_identity,
                            ATT_WIDTH, _F32)
        za, q, k, vt, qi, ki, wit = _fused_matmul(
            "proj_za", [h], [(0, w_in_cols(0))], [], _ep_identity, ZA_END, _F32,
            side_job=functools.partial(_prep_side_job, att.reshape(bsz, seq, ATT_WIDTH),
                                       positions))
        y_a = _spatial_gating(za, a_norm_g[l], a_spatial_w[l], a_spatial_b[l])
        y_b = _dsa_attention(q, k, vt, qi, ki, wit).reshape(m, B_WIDTH)
        merged = _fused_matmul(
            "gated_merge", [y_a, y_b, h],
            [(0, Rhs(p_a, l)), (1, Rhs(p_b, l)), (2, w_in_cols(ATT_END)),
             (2, w_in_cols(ATT_END + d))],
            [], _ep_gated_merge, d, _BF16)
        x1 = _fused_matmul("out_proj", [merged], [(0, Rhs(w_out, l))], [xf], _ep_residual,
                           d, _F32)

        qx = _norm_proj(x1, norm_x_g[l], xq_w, l, _BF16, "xattn_q")
        kx = _norm_proj(mem_f, norm_mem_g[l], xk_w, l, _BF16, "xattn_k")
        vx = _norm_proj(mem_f, norm_mem_g[l], xv_w, l, _BF16, "xattn_v")
        ox = _cross_attention(qx.reshape(bsz, seq, X_WIDTH),
                              kx.reshape(bsz, MEM_LEN, X_WIDTH),
                              vx.reshape(bsz, MEM_LEN, X_WIDTH)).reshape(m, X_WIDTH)
        x2, h2 = _xattn_out_and_norm(ox, xo_w, l, x1, norm_ffn_g[l])

        act, w2_bf16 = _fused_matmul("ffn_up", [h2],
                                     [(0, Rhs(ffn_w1, l)), (0, Rhs(ffn_w3, l))],
                                     [], _ep_swiglu, ffn_hidden, _BF16,
                                     side_job=functools.partial(_cast_side_job, ffn_w2, l))
        xf = _fused_matmul("ffn_down", [act], [(0, Rhs(w2_bf16))], [x2], _ep_residual, d, _F32)

    return _rmsnorm(xf, final_norm_g, _F32).reshape(bsz, seq, d)
```

```python
import functools
from typing import Callable, NamedTuple, Optional

import jax
import jax.numpy as jnp
from jax import lax
from jax.experimental import pallas as pl
from jax.experimental.pallas import tpu as pltpu

D_MODEL = 4096
MEM_LEN = 256
EPS = 1e-6
ROPE_THETA = 500000.0
CHUNK = 128
A_GROUPS = 16
A_WIDTH = D_MODEL // 2
A_GROUP_DIM = A_WIDTH // A_GROUPS
B_HEADS = 16
B_HEAD_DIM = 128
B_KV_HEADS = 4
B_WIDTH = B_HEADS * B_HEAD_DIM
B_KV_WIDTH = B_KV_HEADS * B_HEAD_DIM
B_ROT = B_HEAD_DIM // 4
IDX_HEADS = 16
IDX_DIM = 64
IDX_WIDTH = IDX_HEADS * IDX_DIM
IDX_ROT = IDX_DIM // 4
TOPK_MAX = 256
Q_BLOCK = 128
X_HEADS = 4
X_HEAD_DIM = 256
X_WIDTH = X_HEADS * X_HEAD_DIM
IN_SIZES = (2 * A_WIDTH, B_WIDTH, B_KV_WIDTH, B_KV_WIDTH, IDX_WIDTH, IDX_DIM, IDX_HEADS,
            2 * D_MODEL)

LANES = 128
SUBLANES = 8
KEY_CHUNK = 512
PROJ_TN = 512
ZA_END = IN_SIZES[0]
ATT_END = ZA_END + sum(IN_SIZES[1:7])
ATT_Q0 = 0
ATT_K0 = ATT_Q0 + B_WIDTH
ATT_V0 = ATT_K0 + B_KV_WIDTH
ATT_QI0 = ATT_V0 + B_KV_WIDTH
ATT_KW0 = ATT_QI0 + IDX_WIDTH
ATT_WIDTH = -(-(ATT_END - ZA_END) // PROJ_TN) * PROJ_TN
assert ZA_END % PROJ_TN == 0 and ATT_KW0 + LANES <= ATT_WIDTH

VMEM_LIMIT_BYTES = 60 * 1024 * 1024


class Tile(NamedTuple):
    tm: int
    tn: int
    prefetch_lhs: bool = False
    row_splits: int = 1


TILES = {
    "proj_att": Tile(1024, PROJ_TN, prefetch_lhs=True),
    "proj_za": Tile(1024, 512, prefetch_lhs=True),
    "gated_merge": Tile(1024, 256, row_splits=4),
    "out_proj": Tile(1024, 512, prefetch_lhs=True),
    "ffn_up": Tile(2048, 256, row_splits=4),
    "ffn_down": Tile(512, 512, prefetch_lhs=True),
}
ROW_BLOCK = {
    "rmsnorm": 512,
    "spatial_gating": 4 * CHUNK,
    "cross_attention": 1024,
    "norm_proj": 512,
    "xattn_out_norm": 512,
}

_BF16 = jnp.bfloat16
_F32 = jnp.float32
_NEG_INF = float("-inf")
_NN_DIMS = (((1,), (0,)), ((), ()))
_NT_DIMS = (((1,), (1,)), ((), ()))


def _compiler_params(n_axes):
    return pltpu.CompilerParams(
        dimension_semantics=("arbitrary",) * n_axes,
        vmem_limit_bytes=VMEM_LIMIT_BYTES,
    )


def _rmsnorm_kernel(x_ref, g_ref, o_ref):
    x = x_ref[...]
    ms = jnp.mean(x * x, axis=-1, keepdims=True)
    y = x * lax.rsqrt(ms + EPS)
    o_ref[...] = (y * g_ref[...]).astype(o_ref.dtype)


def _rmsnorm(x, g, out_dtype):
    m, d = x.shape
    tm = ROW_BLOCK["rmsnorm"]
    return pl.pallas_call(
        _rmsnorm_kernel,
        grid=(m // tm,),
        in_specs=[pl.BlockSpec((tm, d), lambda i: (i, 0)),
                  pl.BlockSpec((1, d), lambda i: (0, 0))],
        out_specs=pl.BlockSpec((tm, d), lambda i: (i, 0)),
        out_shape=jax.ShapeDtypeStruct((m, d), out_dtype),
        compiler_params=_compiler_params(1),
        name="rmsnorm",
    )(x, g.reshape(1, d))


def _ep_identity(accs, extras):
    return accs[0]


def _ep_residual(accs, extras):
    return extras[0] + accs[0]


def _ep_gated_merge(accs, extras):
    y_a, y_b, gate_a, gate_b = accs
    return jax.nn.sigmoid(gate_a) * y_a + jax.nn.sigmoid(gate_b) * y_b


def _ep_swiglu(accs, extras):
    return jax.nn.silu(accs[0]) * accs[1]


class Rhs(NamedTuple):
    array: jax.Array
    layer: Optional[int] = None
    col0: int = 0
    transposed: bool = False


def _rhs_spec(r, tn):
    if r.transposed:
        k = r.array.shape[2]
        if r.col0 % tn == 0:
            return pl.BlockSpec((None, tn, k), lambda i, j: (r.layer, j + r.col0 // tn, 0))
        assert r.col0 % SUBLANES == 0
        return pl.BlockSpec((pl.Element(1), pl.Element(tn), pl.Element(k)),
                            lambda i, j: (r.layer, (r.col0 // SUBLANES + j * (tn // SUBLANES))
                                          * SUBLANES, 0))
    assert r.col0 % tn == 0
    off = r.col0 // tn
    if r.layer is None:
        return pl.BlockSpec((r.array.shape[0], tn), lambda i, j: (0, j + off))
    return pl.BlockSpec((None, r.array.shape[1], tn), lambda i, j: (r.layer, 0, j + off))


class SideJob(NamedTuple):
    body: Callable
    arrays: tuple
    in_specs: tuple
    out_specs: tuple
    out_shape: tuple


def _cast_body(x_ref, o_ref):
    o_ref[...] = x_ref[...].astype(o_ref.dtype)


def _cast_side_job(w_stack, layer, n_steps):
    _, rows, cols = w_stack.shape
    slab = rows // n_steps
    assert slab * n_steps == rows and slab % (2 * SUBLANES) == 0
    return SideJob(_cast_body, (w_stack,),
                   (pl.BlockSpec((None, slab, cols), lambda s: (layer, s, 0)),),
                   (pl.BlockSpec((slab, cols), lambda s: (s, 0)),),
                   (jax.ShapeDtypeStruct((rows, cols), _BF16),))


def _mm_kernel(*refs, n_lhs, pair_lhs, rhs_transposed, n_extra, epilogue, side_body, n_side_in,
               row_splits):
    n_rhs = len(pair_lhs)
    lhs_refs = refs[:n_lhs]
    rhs_refs = refs[n_lhs:n_lhs + n_rhs]
    n_in = n_lhs + n_rhs + n_extra
    extra_refs = refs[n_lhs + n_rhs:n_in]
    o_ref = refs[n_in + n_side_in]
    if side_body is not None:
        side_body(*refs[n_in:n_in + n_side_in], *refs[n_in + n_side_in + 1:])
    weights = [(r[0] if len(r.shape) == 3 else r[...]).astype(_BF16) for r in rhs_refs]
    rows_per_split = o_ref.shape[0] // row_splits
    for s in range(row_splits):
        rows = slice(s * rows_per_split, (s + 1) * rows_per_split)
        accs = [lax.dot_general(lhs_refs[li][rows, :], w, _NT_DIMS if tr else _NN_DIMS,
                                preferred_element_type=_F32)
                for li, w, tr in zip(pair_lhs, weights, rhs_transposed)]
        o_ref[rows, :] = epilogue(accs, [e[rows, :] for e in extra_refs]).astype(o_ref.dtype)


def _fused_matmul(name, lhs, pairs, extras, epilogue, n_out, out_dtype, side_job=None):
    tm, tn, prefetch_lhs, row_splits = TILES[name]
    m = lhs[0].shape[0]
    assert m % tm == 0 and n_out % tn == 0
    grid = (m // tm, n_out // tn)
    lhs_mode = {} if prefetch_lhs else {"pipeline_mode": pl.Buffered(1)}
    in_specs = [pl.BlockSpec((tm, a.shape[1]), lambda i, j: (i, 0), **lhs_mode) for a in lhs]
    in_specs += [_rhs_spec(r, tn) for _, r in pairs]
    in_specs += [pl.BlockSpec((tm, tn), lambda i, j: (i, j)) for _ in extras]
    out_specs = [pl.BlockSpec((tm, tn), lambda i, j: (i, j))]
    out_shape = [jax.ShapeDtypeStruct((m, n_out), out_dtype)]
    operands = list(lhs) + [p[1].array for p in pairs] + list(extras)
    side = None if side_job is None else side_job(grid[0] * grid[1])
    if side is not None:
        def per_step(spec):
            return pl.BlockSpec(spec.block_shape,
                                lambda i, j: spec.index_map(i * grid[1] + j))
        in_specs += [per_step(s) for s in side.in_specs]
        out_specs += [per_step(s) for s in side.out_specs]
        out_shape += list(side.out_shape)
        operands += list(side.arrays)
    kern = functools.partial(_mm_kernel, n_lhs=len(lhs), pair_lhs=tuple(p[0] for p in pairs),
                             rhs_transposed=tuple(p[1].transposed for p in pairs),
                             n_extra=len(extras), epilogue=epilogue,
                             side_body=None if side is None else side.body,
                             n_side_in=0 if side is None else len(side.in_specs),
                             row_splits=row_splits)
    outs = pl.pallas_call(
        kern,
        grid=grid,
        in_specs=in_specs,
        out_specs=out_specs,
        out_shape=out_shape,
        compiler_params=_compiler_params(2),
        name=name,
    )(*operands)
    return outs[0] if side is None else tuple(outs)


def _gating_kernel(za_ref, ng_ref, w_ref, bt_ref, o_ref):
    row = lax.broadcasted_iota(jnp.int32, (CHUNK, CHUNK), 0)
    col = lax.broadcasted_iota(jnp.int32, (CHUNK, CHUNK), 1)
    causal = col <= row
    for c in range(za_ref.shape[0] // CHUNK):
        rows = slice(c * CHUNK, (c + 1) * CHUNK)
        z = jax.nn.gelu(za_ref[rows, :])
        u = z[:, :A_WIDTH]
        v = z[:, A_WIDTH:]
        ms = jnp.mean(v * v, axis=-1, keepdims=True)
        vn = ((v * lax.rsqrt(ms + EPS)) * ng_ref[...]).astype(_BF16)
        for g in range(A_GROUPS):
            sl = slice(g * A_GROUP_DIM, (g + 1) * A_GROUP_DIM)
            w = jnp.where(causal, w_ref[g], 0.0).astype(_BF16)
            s = jnp.dot(w, vn[:, sl], preferred_element_type=_F32) + bt_ref[:, g:g + 1]
            o_ref[rows, sl] = (u[:, sl] * s).astype(o_ref.dtype)


def _spatial_gating(za, norm_g, w_s, b_s):
    m = za.shape[0]
    tm = ROW_BLOCK["spatial_gating"]
    return pl.pallas_call(
        _gating_kernel,
        grid=(m // tm,),
        in_specs=[pl.BlockSpec((tm, 2 * A_WIDTH), lambda i: (i, 0)),
                  pl.BlockSpec((1, A_WIDTH), lambda i: (0, 0)),
                  pl.BlockSpec((A_GROUPS, CHUNK, CHUNK), lambda i: (0, 0, 0)),
                  pl.BlockSpec((CHUNK, A_GROUPS), lambda i: (0, 0))],
        out_specs=pl.BlockSpec((tm, A_WIDTH), lambda i: (i, 0)),
        out_shape=jax.ShapeDtypeStruct((m, A_WIDTH), _BF16),
        compiler_params=_compiler_params(1),
        name="spatial_gating",
    )(za, norm_g.reshape(1, A_WIDTH), w_s, b_s.T)


def _lane_inv_freq(rot_dim, period):
    inv_freq = ROPE_THETA ** (-jnp.arange(0, rot_dim, 2, dtype=_F32) / rot_dim)
    head = jnp.concatenate([inv_freq, inv_freq, jnp.zeros((period - rot_dim,), _F32)])
    return jnp.tile(head, LANES // period).reshape(1, LANES)


def _rope_tables(pos, inv_freq_lanes, rot_dim, period):
    half = rot_dim // 2
    ang = pos * inv_freq_lanes
    cos, sin = jnp.cos(ang), jnp.sin(ang)
    lane = lax.broadcasted_iota(jnp.int32, ang.shape, 1) & (period - 1)
    c = jnp.where(lane < rot_dim, cos, 1.0)
    s1 = jnp.where(lane >= half, jnp.where(lane < rot_dim, sin, 0.0), 0.0)
    s2 = jnp.where(lane < half, -sin, 0.0)
    return c, s1, s2


def _rope(x, c, s1, s2, half):
    return x * c + pltpu.roll(x, half, 1) * s1 + pltpu.roll(x, LANES - half, 1) * s2


def _prep_kernel(att_ref, pos_ref, fb_ref, fi_ref,
                 q_ref, k_ref, vt_ref, qi_ref, ki_ref, wit_ref):
    pos = pos_ref[0]
    cb, s1b, s2b = _rope_tables(pos, fb_ref[...], B_ROT, B_HEAD_DIM)
    ci, s1i, s2i = _rope_tables(pos, fi_ref[...], IDX_ROT, IDX_DIM)
    for h in range(B_HEADS):
        x = att_ref[0, :, ATT_Q0 + h * LANES:ATT_Q0 + (h + 1) * LANES]
        q_ref[0, :, h * LANES:(h + 1) * LANES] = _rope(x, cb, s1b, s2b, B_ROT // 2).astype(_BF16)
    for h in range(B_KV_HEADS):
        x = att_ref[0, :, ATT_K0 + h * LANES:ATT_K0 + (h + 1) * LANES]
        k_ref[0, :, h * LANES:(h + 1) * LANES] = _rope(x, cb, s1b, s2b, B_ROT // 2).astype(_BF16)
    vt_ref[0, 0] = att_ref[0, :, ATT_V0:ATT_V0 + B_KV_WIDTH].T.astype(_BF16)
    lane = lax.broadcasted_iota(jnp.int32, ci.shape, 1)
    is_ki = lane < IDX_DIM
    heads_per_group = LANES // IDX_DIM
    for j in range(IDX_WIDTH // LANES):
        x = att_ref[0, :, ATT_QI0 + j * LANES:ATT_QI0 + (j + 1) * LANES]
        y = _rope(x, ci, s1i, s2i, IDX_ROT // 2)
        for r in range(heads_per_group):
            h = j * heads_per_group + r
            yr = y if r == 0 else pltpu.roll(y, LANES - r * IDX_DIM, 1)
            qi_ref[0, :, h * LANES:(h + 1) * LANES] = jnp.where(is_ki, yr, 0.0).astype(_BF16)
    x = att_ref[0, :, ATT_KW0:ATT_KW0 + LANES]
    y = _rope(x, jnp.where(is_ki, ci, 1.0), jnp.where(is_ki, s1i, 0.0),
              jnp.where(is_ki, s2i, 0.0), IDX_ROT // 2)
    ki_ref[0] = jnp.where(is_ki, y, 0.0).astype(_BF16)
    wit_ref[0] = y.T[IDX_DIM:IDX_DIM + IDX_HEADS, :]


def _prep_side_job(att, positions, n_steps):
    bsz, seq, _ = att.shape
    ts = bsz * seq // n_steps
    assert ts * n_steps == bsz * seq and seq % ts == 0 and KEY_CHUNK % ts == 0
    assert ts % (2 * SUBLANES) == 0
    per_seq = seq // ts
    per_chunk = KEY_CHUNK // ts

    def tokens(width):
        return pl.BlockSpec((1, ts, width), lambda s: (s // per_seq, s % per_seq, 0))

    def out(width, dtype):
        return tokens(width), jax.ShapeDtypeStruct((bsz, seq, width), dtype)

    vt_out = (pl.BlockSpec((1, 1, B_KV_WIDTH, ts),
                           lambda s: (s // per_seq, (s % per_seq) // per_chunk, 0,
                                      (s % per_seq) % per_chunk)),
              jax.ShapeDtypeStruct((bsz, seq // KEY_CHUNK, B_KV_WIDTH, KEY_CHUNK), _BF16))
    wit_out = (pl.BlockSpec((1, IDX_HEADS, ts), lambda s: (s // per_seq, 0, s % per_seq)),
               jax.ShapeDtypeStruct((bsz, IDX_HEADS, seq), _F32))
    outs = [out(B_WIDTH, _BF16), out(B_KV_WIDTH, _BF16), vt_out,
            out(IDX_HEADS * LANES, _BF16), out(LANES, _BF16), wit_out]
    freq_spec = pl.BlockSpec((1, LANES), lambda s: (0, 0))
    return SideJob(_prep_kernel,
                   (att, positions.astype(_F32).reshape(bsz, seq, 1),
                    _lane_inv_freq(B_ROT, B_HEAD_DIM), _lane_inv_freq(IDX_ROT, IDX_DIM)),
                   (tokens(ATT_WIDTH), tokens(1), freq_spec, freq_spec),
                   tuple(o[0] for o in outs), tuple(o[1] for o in outs))


def _ordered_int_to_float(key):
    bits = jnp.where(key < 0, key ^ jnp.int32(0x7FFFFFFF), key)
    return lax.bitcast_convert_type(bits, _F32)


def _fold_rows(x, op):
    while x.shape[0] > SUBLANES:
        half = x.shape[0] // 2
        x = op(x[:half], x[half:])
    return x


def _dsa_kernel(*refs, n_sel):
    seq = refs[4].shape[1]
    n_chunks = (pl.program_id(1) * Q_BLOCK + Q_BLOCK + KEY_CHUNK - 1) // KEY_CHUNK
    for n in range(1, seq // KEY_CHUNK + 1):
        pl.when(n_chunks == n)(functools.partial(_dsa_block, *refs, n_sel=n_sel, n_chunks=n))


def _dsa_block(qi_ref, wit_ref, ki_ref, q_ref, k_ref, vt_ref, o_ref,
               score_ref, bias_ref, logit_ref, acc_ref, *, n_sel, n_chunks):
    t = Q_BLOCK
    blk = pl.program_id(1)
    qpos = blk * t + lax.broadcasted_iota(jnp.int32, (1, t), 1)
    kiota = lax.broadcasted_iota(jnp.int32, (KEY_CHUNK, 1), 0)
    idx_scale = (IDX_DIM ** -0.5) * (IDX_HEADS ** -0.5)

    def rows(c):
        return slice(c * KEY_CHUNK, (c + 1) * KEY_CHUNK)

    def for_chunks(body, carry):
        for c in range(n_chunks):
            carry = body(c, carry)
        return carry

    def indexer_chunk(c, carry):
        ki = ki_ref[0, rows(c), :]
        acc = jnp.zeros((KEY_CHUNK, t), _F32)
        for h in range(0, IDX_HEADS, 2):
            qpair = jnp.concatenate([qi_ref[0, :, h * LANES:(h + 1) * LANES],
                                     qi_ref[0, :, (h + 1) * LANES:(h + 2) * LANES]], axis=0)
            d = lax.dot_general(ki, qpair, _NT_DIMS, preferred_element_type=_F32)
            acc = acc + wit_ref[0, h:h + 1, :] * jnp.maximum(d[:, :t], 0.0)
            acc = acc + wit_ref[0, h + 1:h + 2, :] * jnp.maximum(d[:, t:], 0.0)
        kpos = c * KEY_CHUNK + kiota
        score_ref[rows(c), :] = jnp.where(kpos <= qpos, acc * idx_scale + 0.0, _NEG_INF)
        return carry

    for_chunks(indexer_chunk, 0)

    def count(indicator):
        def chunk(c, part):
            return part + _fold_rows(indicator(score_ref[rows(c), :], c), jnp.add)
        part = for_chunks(chunk, jnp.zeros((SUBLANES, t), _F32))
        return jnp.sum(part, axis=0, keepdims=True)

    def count_ge(cand_f):
        return count(lambda s, c: jnp.where(s >= cand_f, 1.0, 0.0))

    def search():
        int_min = jnp.int32(-2 ** 31)
        base = jnp.where(count_ge(jnp.zeros((1, t), _F32)) >= n_sel, jnp.int32(0), int_min)

        def search_step(it, base):
            cand = base | lax.shift_left(jnp.int32(1), jnp.int32(30) - it)
            cnt = count_ge(_ordered_int_to_float(cand))
            return jnp.where(cnt >= n_sel, cand, base)

        return lax.fori_loop(0, 31, search_step, base)

    if (n_chunks - 1) * KEY_CHUNK >= n_sel:
        base = search()
    else:
        base = lax.cond((blk + 1) * t <= n_sel, lambda: jnp.zeros((1, t), jnp.int32), search)
    thr = jnp.where(qpos + 1 <= n_sel, _NEG_INF, _ordered_int_to_float(base))

    def bias_chunk(c, part):
        s = score_ref[rows(c), :]
        sel = jnp.where(c * KEY_CHUNK + kiota <= qpos, jnp.where(s >= thr, 1.0, 0.0), 0.0)
        bias_ref[rows(c), :] = jnp.where(sel > 0.0, 0.0, _NEG_INF)
        return part + _fold_rows(sel, jnp.add)

    n_ge = jnp.sum(for_chunks(bias_chunk, jnp.zeros((SUBLANES, t), _F32)), axis=0, keepdims=True)

    @pl.when(jnp.max(n_ge) > n_sel)
    def _():
        nxt = _ordered_int_to_float(base + 1)
        tied = n_ge > n_sel
        n_above = count(lambda s, c: jnp.where(s >= nxt, 1.0, 0.0))
        need = jnp.where(tied, n_sel - n_above, 0.0)
        no_index = float(k_ref.shape[1])
        front_v = jnp.where(tied, jnp.inf, _NEG_INF)
        front_i = jnp.where(tied, -1.0, no_index)

        def beyond(s, kposf, front_v, front_i):
            after = jnp.where(s < front_v, 1.0,
                              jnp.where(s == front_v, jnp.where(kposf > front_i, 1.0, 0.0), 0.0))
            return jnp.where(s >= thr, jnp.where(s < nxt, after, 0.0), 0.0)

        def kposf(c):
            return (c * KEY_CHUNK + kiota).astype(_F32)

        def advance(_, carry):
            front_v, front_i, need = carry

            def best_value(c, part):
                s = score_ref[rows(c), :]
                cand = jnp.where(beyond(s, kposf(c), front_v, front_i) > 0.0, s, _NEG_INF)
                return jnp.maximum(part, _fold_rows(cand, jnp.maximum))

            v = jnp.max(for_chunks(best_value, jnp.full((SUBLANES, t), _NEG_INF, _F32)),
                        axis=0, keepdims=True)

            def first_index(c, part):
                s = score_ref[rows(c), :]
                hit = jnp.where(s == v, beyond(s, kposf(c), front_v, front_i), 0.0)
                cand = jnp.where(hit > 0.0, kposf(c), no_index)
                return jnp.minimum(part, _fold_rows(cand, jnp.minimum))

            i = jnp.min(for_chunks(first_index, jnp.full((SUBLANES, t), no_index, _F32)),
                        axis=0, keepdims=True)
            active = need > 0.0
            return (jnp.where(active, v, front_v), jnp.where(active, i, front_i),
                    jnp.where(active, need - 1.0, need))

        front_v, front_i, _ = lax.fori_loop(0, jnp.max(need).astype(jnp.int32), advance,
                                            (front_v, front_i, need))

        def tie_chunk(c, carry):
            s = score_ref[rows(c), :]
            upto = jnp.where(s > front_v, 0.0,
                             jnp.where(s == front_v,
                                       jnp.where(kposf(c) <= front_i, 0.0, _NEG_INF), _NEG_INF))
            keep = jnp.where(s >= nxt, 0.0, jnp.where(s >= thr, upto, _NEG_INF))
            bias_ref[rows(c), :] = jnp.where(c * KEY_CHUNK + kiota <= qpos, keep, _NEG_INF)
            return carry

        for_chunks(tie_chunk, 0)

    grp = B_HEADS // B_KV_HEADS
    gw = grp * t
    exp2_scale = (B_HEAD_DIM ** -0.5) * 1.4426950408889634

    def logit_chunk(c, mx):
        bias = jnp.concatenate([bias_ref[rows(c), :]] * grp, axis=1)
        folded = []
        for g in range(B_KV_HEADS):
            qg = jnp.concatenate(
                [q_ref[0, :, (g * grp + hh) * B_HEAD_DIM:(g * grp + hh + 1) * B_HEAD_DIM]
                 for hh in range(grp)], axis=0)
            l = lax.dot_general(k_ref[0, rows(c), g * B_HEAD_DIM:(g + 1) * B_HEAD_DIM], qg,
                                _NT_DIMS, preferred_element_type=_F32) + bias
            logit_ref[rows(c), g * gw:(g + 1) * gw] = l
            folded.append(_fold_rows(l, jnp.maximum))
        return jnp.maximum(mx, jnp.concatenate(folded, axis=1))

    mx = for_chunks(logit_chunk, jnp.full((SUBLANES, B_HEADS * t), _NEG_INF, _F32))
    mx = jnp.max(mx, axis=0, keepdims=True)

    acc_ref[...] = jnp.zeros_like(acc_ref)
    ones_rows = jnp.ones((acc_ref.shape[0] - B_HEAD_DIM, KEY_CHUNK), _BF16)

    def pv_chunk(c, carry):
        for g in range(B_KV_HEADS):
            cols = slice(g * gw, (g + 1) * gw)
            p = jnp.exp2((logit_ref[rows(c), cols] - mx[:, cols]) * exp2_scale)
            vt_ext = jnp.concatenate(
                [vt_ref[0, c, g * B_HEAD_DIM:(g + 1) * B_HEAD_DIM, :], ones_rows], axis=0)
            acc_ref[:, cols] += jnp.dot(vt_ext, p.astype(_BF16), preferred_element_type=_F32)
        return carry

    for_chunks(pv_chunk, 0)
    out_t = acc_ref[:B_HEAD_DIM, :] / acc_ref[B_HEAD_DIM:B_HEAD_DIM + 1, :]
    for h in range(B_HEADS):
        o_ref[0, :, h * B_HEAD_DIM:(h + 1) * B_HEAD_DIM] = (
            out_t[:, h * t:(h + 1) * t].T.astype(o_ref.dtype))


def _dsa_attention(q, k, vt, qi, ki, wit):
    bsz, seq, _ = q.shape
    assert Q_BLOCK == LANES and seq % KEY_CHUNK == 0
    n_sel = min(TOPK_MAX, seq // 4)

    def q_spec(width):
        return pl.BlockSpec((1, Q_BLOCK, width), lambda b, i: (b, i, 0))

    def kv_spec(width):
        return pl.BlockSpec((1, seq, width), lambda b, i: (b, 0, 0))

    return pl.pallas_call(
        functools.partial(_dsa_kernel, n_sel=n_sel),
        grid=(bsz, seq // Q_BLOCK),
        in_specs=[q_spec(IDX_HEADS * LANES),
                  pl.BlockSpec((1, IDX_HEADS, Q_BLOCK), lambda b, i: (b, 0, i)),
                  kv_spec(LANES), q_spec(B_WIDTH), kv_spec(B_KV_WIDTH),
                  pl.BlockSpec((1, seq // KEY_CHUNK, B_KV_WIDTH, KEY_CHUNK),
                               lambda b, i: (b, 0, 0, 0))],
        out_specs=q_spec(B_WIDTH),
        out_shape=jax.ShapeDtypeStruct((bsz, seq, B_WIDTH), _BF16),
        scratch_shapes=[pltpu.VMEM((seq, Q_BLOCK), _F32), pltpu.VMEM((seq, Q_BLOCK), _F32),
                        pltpu.VMEM((seq, B_HEADS * Q_BLOCK), _F32),
                        pltpu.VMEM((B_HEAD_DIM + 2 * SUBLANES, B_HEADS * Q_BLOCK), _F32)],
        compiler_params=_compiler_params(2),
        name="dsa_attention",
    )(qi, wit, ki, q, k, vt)


def _xattn_kernel(q_ref, k_ref, v_ref, o_ref):
    scale = X_HEAD_DIM ** -0.5
    for h in range(X_HEADS):
        hs = slice(h * X_HEAD_DIM, (h + 1) * X_HEAD_DIM)
        logits = lax.dot_general(q_ref[0, :, hs], k_ref[0, :, hs], _NT_DIMS,
                                 preferred_element_type=_F32) * scale
        mx = jnp.max(logits, axis=-1, keepdims=True)
        p = jnp.exp(logits - mx)
        denom = jnp.sum(p, axis=-1, keepdims=True)
        o = jnp.dot(p.astype(_BF16), v_ref[0, :, hs], preferred_element_type=_F32)
        o_ref[0, :, hs] = (o / denom).astype(o_ref.dtype)


def _cross_attention(q, k, v):
    bsz, seq, _ = q.shape
    tq = ROW_BLOCK["cross_attention"]
    m = k.shape[1]
    mem_spec = pl.BlockSpec((1, m, X_WIDTH), lambda b, i: (b, 0, 0))
    return pl.pallas_call(
        _xattn_kernel,
        grid=(bsz, seq // tq),
        in_specs=[pl.BlockSpec((1, tq, X_WIDTH), lambda b, i: (b, i, 0)), mem_spec, mem_spec],
        out_specs=pl.BlockSpec((1, tq, X_WIDTH), lambda b, i: (b, i, 0)),
        out_shape=jax.ShapeDtypeStruct((bsz, seq, X_WIDTH), _BF16),
        compiler_params=_compiler_params(2),
        name="cross_attention",
    )(q, k, v)


def _norm_proj_kernel(x_ref, g_ref, w_ref, o_ref, w_bf16_ref):
    @pl.when(pl.program_id(0) == 0)
    def _():
        w_bf16_ref[...] = w_ref[...].astype(_BF16)

    x = x_ref[...]
    ms = jnp.mean(x * x, axis=-1, keepdims=True)
    h = ((x * lax.rsqrt(ms + EPS)) * g_ref[...]).astype(_BF16)
    o_ref[...] = jnp.dot(h, w_bf16_ref[...], preferred_element_type=_F32).astype(o_ref.dtype)


def _norm_proj(x, g, w_stack, layer, out_dtype, name):
    m, d = x.shape
    tm = ROW_BLOCK["norm_proj"]
    n = w_stack.shape[2]
    return pl.pallas_call(
        _norm_proj_kernel,
        grid=(m // tm,),
        in_specs=[pl.BlockSpec((tm, d), lambda i: (i, 0)),
                  pl.BlockSpec((1, d), lambda i: (0, 0)),
                  pl.BlockSpec((None, d, n), lambda i: (layer, 0, 0),
                               pipeline_mode=pl.Buffered(1))],
        out_specs=pl.BlockSpec((tm, n), lambda i: (i, 0)),
        out_shape=jax.ShapeDtypeStruct((m, n), out_dtype),
        scratch_shapes=[pltpu.VMEM((d, n), _BF16)],
        compiler_params=_compiler_params(1),
        name=name,
    )(x, g.reshape(1, d), w_stack)


def _xattn_out_kernel(o_ref, w_ref, x_ref, g_ref, x_out_ref, h_out_ref):
    x = x_ref[...] + jnp.dot(o_ref[...], w_ref[...], preferred_element_type=_F32)
    x_out_ref[...] = x
    ms = jnp.mean(x * x, axis=-1, keepdims=True)
    h_out_ref[...] = ((x * lax.rsqrt(ms + EPS)) * g_ref[...]).astype(h_out_ref.dtype)


def _xattn_out_and_norm(o, w_bf16, x, g):
    m, k = o.shape
    tm = ROW_BLOCK["xattn_out_norm"]
    d = x.shape[1]
    row_spec = pl.BlockSpec((tm, d), lambda i: (i, 0))
    return pl.pallas_call(
        _xattn_out_kernel,
        grid=(m // tm,),
        in_specs=[pl.BlockSpec((tm, k), lambda i: (i, 0)),
                  pl.BlockSpec((k, d), lambda i: (0, 0), pipeline_mode=pl.Buffered(1)),
                  row_spec,
                  pl.BlockSpec((1, d), lambda i: (0, 0))],
        out_specs=[row_spec, row_spec],
        out_shape=[jax.ShapeDtypeStruct((m, d), _F32), jax.ShapeDtypeStruct((m, d), _BF16)],
        compiler_params=_compiler_params(1),
        name="xattn_out_norm",
    )(o, w_bf16, x, g.reshape(1, d))


def kernel(x, mem, positions, norm_mix_g, w_in, a_norm_g, a_spatial_w, a_spatial_b, p_a, p_b,
           w_out, norm_x_g, norm_mem_g, xq_w, xk_w, xv_w, xo_w, norm_ffn_g, ffn_w1, ffn_w3,
           ffn_w2, final_norm_g):
    bsz, seq, d = x.shape
    m = bsz * seq
    depth = w_in.shape[0]
    ffn_hidden = ffn_w1.shape[-1]

    xf = x.reshape(m, d)
    mem_f = mem.reshape(bsz * MEM_LEN, d)
    w_in_t = jnp.transpose(w_in, (0, 2, 1))
    for l in range(depth):
        def w_in_cols(col0):
            return Rhs(w_in_t, l, col0, transposed=True)

        h = _rmsnorm(xf, norm_mix_g[l], _BF16)
        att = _fused_matmul("proj_att", [h], [(0, w_in_cols(ZA_END))], [], _ep_identity,
                            ATT_WIDTH, _F32)
        za, q, k, vt, qi, ki, wit = _fused_matmul(
            "proj_za", [h], [(0, w_in_cols(0))], [], _ep_identity, ZA_END, _F32,
            side_job=functools.partial(_prep_side_job, att.reshape(bsz, seq, ATT_WIDTH),
                                       positions))
        y_a = _spatial_gating(za, a_norm_g[l], a_spatial_w[l], a_spatial_b[l])
        y_b = _dsa_attention(q, k, vt, qi, ki, wit).reshape(m, B_WIDTH)
        merged = _fused_matmul(
            "gated_merge", [y_a, y_b, h],
            [(0, Rhs(p_a, l)), (1, Rhs(p_b, l)), (2, w_in_cols(ATT_END)),
             (2, w_in_cols(ATT_END + d))],
            [], _ep_gated_merge, d, _BF16)
        x1, xo_bf16 = _fused_matmul("out_proj", [merged], [(0, Rhs(w_out, l))], [xf],
                                    _ep_residual, d, _F32,
                                    side_job=functools.partial(_cast_side_job, xo_w, l))

        qx = _norm_proj(x1, norm_x_g[l], xq_w, l, _BF16, "xattn_q")
        kx = _norm_proj(mem_f, norm_mem_g[l], xk_w, l, _BF16, "xattn_k")
        vx = _norm_proj(mem_f, norm_mem_g[l], xv_w, l, _BF16, "xattn_v")
        ox = _cross_attention(qx.reshape(bsz, seq, X_WIDTH),
                              kx.reshape(bsz, MEM_LEN, X_WIDTH),
                              vx.reshape(bsz, MEM_LEN, X_WIDTH)).reshape(m, X_WIDTH)
        x2, h2 = _xattn_out_and_norm(ox, xo_bf16, x1, norm_ffn_g[l])

        act, w2_bf16 = _fused_matmul("ffn_up", [h2],
                                     [(0, Rhs(ffn_w1, l)), (0, Rhs(ffn_w3, l))],
                                     [], _ep_swiglu, ffn_hidden, _BF16,
                                     side_job=functools.partial(_cast_side_job, ffn_w2, l))
        xf = _fused_matmul("ffn_down", [act], [(0, Rhs(w2_bf16))], [x2], _ep_residual, d, _F32)

    return _rmsnorm(xf, final_norm_g, _F32).reshape(bsz, seq, d)
```

```python
import functools
from typing import Callable, NamedTuple, Optional

import jax
import jax.numpy as jnp
from jax import lax
from jax.experimental import pallas as pl
from jax.experimental.pallas import tpu as pltpu

D_MODEL = 4096
MEM_LEN = 256
EPS = 1e-6
ROPE_THETA = 500000.0
CHUNK = 128
A_GROUPS = 16
A_WIDTH = D_MODEL // 2
A_GROUP_DIM = A_WIDTH // A_GROUPS
B_HEADS = 16
B_HEAD_DIM = 128
B_KV_HEADS = 4
B_WIDTH = B_HEADS * B_HEAD_DIM
B_KV_WIDTH = B_KV_HEADS * B_HEAD_DIM
B_ROT = B_HEAD_DIM // 4
IDX_HEADS = 16
IDX_DIM = 64
IDX_WIDTH = IDX_HEADS * IDX_DIM
IDX_ROT = IDX_DIM // 4
TOPK_MAX = 256
Q_BLOCK = 128
X_HEADS = 4
X_HEAD_DIM = 256
X_WIDTH = X_HEADS * X_HEAD_DIM
IN_SIZES = (2 * A_WIDTH, B_WIDTH, B_KV_WIDTH, B_KV_WIDTH, IDX_WIDTH, IDX_DIM, IDX_HEADS,
            2 * D_MODEL)

LANES = 128
SUBLANES = 8
KEY_CHUNK = 512
PROJ_TN = 512
ZA_END = IN_SIZES[0]
ATT_END = ZA_END + sum(IN_SIZES[1:7])
ATT_Q0 = 0
ATT_K0 = ATT_Q0 + B_WIDTH
ATT_V0 = ATT_K0 + B_KV_WIDTH
ATT_QI0 = ATT_V0 + B_KV_WIDTH
ATT_KW0 = ATT_QI0 + IDX_WIDTH
ATT_WIDTH = -(-(ATT_END - ZA_END) // PROJ_TN) * PROJ_TN
assert ZA_END % PROJ_TN == 0 and ATT_KW0 + LANES <= ATT_WIDTH

VMEM_LIMIT_BYTES = 60 * 1024 * 1024


class Tile(NamedTuple):
    tm: int
    tn: int
    prefetch_lhs: bool = False
    row_splits: int = 1


TILES = {
    "proj_att": Tile(1024, PROJ_TN, prefetch_lhs=True),
    "proj_za": Tile(1024, 512, prefetch_lhs=True),
    "gated_merge": Tile(1024, 512, row_splits=4),
    "out_proj": Tile(1024, 512, prefetch_lhs=True),
    "ffn_up": Tile(2048, 256, row_splits=4),
    "ffn_down": Tile(512, 512, prefetch_lhs=True),
}
ROW_BLOCK = {
    "rmsnorm": 512,
    "spatial_gating": 4 * CHUNK,
    "cross_attention": 1024,
    "norm_proj": 512,
    "xattn_out_norm": 512,
}

_BF16 = jnp.bfloat16
_F32 = jnp.float32
_NEG_INF = float("-inf")
_NN_DIMS = (((1,), (0,)), ((), ()))
_NT_DIMS = (((1,), (1,)), ((), ()))


def _compiler_params(n_axes):
    return pltpu.CompilerParams(
        dimension_semantics=("arbitrary",) * n_axes,
        vmem_limit_bytes=VMEM_LIMIT_BYTES,
    )


def _rmsnorm_kernel(x_ref, g_ref, o_ref):
    x = x_ref[...]
    ms = jnp.mean(x * x, axis=-1, keepdims=True)
    y = x * lax.rsqrt(ms + EPS)
    o_ref[...] = (y * g_ref[...]).astype(o_ref.dtype)


def _rmsnorm(x, g, out_dtype):
    m, d = x.shape
    tm = ROW_BLOCK["rmsnorm"]
    return pl.pallas_call(
        _rmsnorm_kernel,
        grid=(m // tm,),
        in_specs=[pl.BlockSpec((tm, d), lambda i: (i, 0)),
                  pl.BlockSpec((1, d), lambda i: (0, 0))],
        out_specs=pl.BlockSpec((tm, d), lambda i: (i, 0)),
        out_shape=jax.ShapeDtypeStruct((m, d), out_dtype),
        compiler_params=_compiler_params(1),
        name="rmsnorm",
    )(x, g.reshape(1, d))


def _ep_identity(accs, extras):
    return accs[0]


def _ep_residual(accs, extras):
    return extras[0] + accs[0]


def _ep_gated_merge(accs, extras):
    y_a, y_b, gate_a, gate_b = accs
    return jax.nn.sigmoid(gate_a) * y_a + jax.nn.sigmoid(gate_b) * y_b


def _ep_swiglu(accs, extras):
    return jax.nn.silu(accs[0]) * accs[1]


class Rhs(NamedTuple):
    array: jax.Array
    layer: Optional[int] = None
    col0: int = 0
    transposed: bool = False


def _rhs_spec(r, tn):
    if r.transposed and r.layer is None:
        assert r.col0 % tn == 0
        off = r.col0 // tn
        return pl.BlockSpec((tn, r.array.shape[1]), lambda i, j: (j + off, 0))
    if r.transposed:
        k = r.array.shape[2]
        if r.col0 % tn == 0:
            return pl.BlockSpec((None, tn, k), lambda i, j: (r.layer, j + r.col0 // tn, 0))
        assert r.col0 % SUBLANES == 0
        return pl.BlockSpec((pl.Element(1), pl.Element(tn), pl.Element(k)),
                            lambda i, j: (r.layer, (r.col0 // SUBLANES + j * (tn // SUBLANES))
                                          * SUBLANES, 0))
    assert r.col0 % tn == 0
    off = r.col0 // tn
    if r.layer is None:
        return pl.BlockSpec((r.array.shape[0], tn), lambda i, j: (0, j + off))
    return pl.BlockSpec((None, r.array.shape[1], tn), lambda i, j: (r.layer, 0, j + off))


class SideJob(NamedTuple):
    body: Callable
    arrays: tuple
    in_specs: tuple
    out_specs: tuple
    out_shape: tuple


def _cast_body(x_ref, o_ref):
    x = x_ref[0] if len(x_ref.shape) == 3 else x_ref[...]
    o_ref[...] = x.astype(o_ref.dtype)


def _cast_side_job(w_stack, layer, n_steps, row0=0, rows=None):
    rows = w_stack.shape[1] if rows is None else rows
    cols = w_stack.shape[2]
    slab = rows // n_steps
    assert slab * n_steps == rows and slab % (2 * SUBLANES) == 0
    if row0 % slab == 0:
        in_spec = pl.BlockSpec((None, slab, cols), lambda s: (layer, s + row0 // slab, 0))
    else:
        assert row0 % SUBLANES == 0
        in_spec = pl.BlockSpec(
            (pl.Element(1), pl.Element(slab), pl.Element(cols)),
            lambda s: (layer, (row0 // SUBLANES + s * (slab // SUBLANES)) * SUBLANES, 0))
    return SideJob(_cast_body, (w_stack,), (in_spec,),
                   (pl.BlockSpec((slab, cols), lambda s: (s, 0)),),
                   (jax.ShapeDtypeStruct((rows, cols), _BF16),))


def _merge_side_jobs(*jobs):
    n_in = [len(j.in_specs) for j in jobs]
    n_out = [len(j.out_specs) for j in jobs]

    def body(*refs):
        ins, outs = refs[:sum(n_in)], refs[sum(n_in):]
        i = o = 0
        for j, ni, no in zip(jobs, n_in, n_out):
            j.body(*ins[i:i + ni], *outs[o:o + no])
            i, o = i + ni, o + no

    return SideJob(body, sum((j.arrays for j in jobs), ()), sum((j.in_specs for j in jobs), ()),
                   sum((j.out_specs for j in jobs), ()), sum((j.out_shape for j in jobs), ()))


def _mm_kernel(*refs, n_lhs, pair_lhs, rhs_transposed, n_extra, epilogue, side_body, n_side_in,
               row_splits):
    n_rhs = len(pair_lhs)
    lhs_refs = refs[:n_lhs]
    rhs_refs = refs[n_lhs:n_lhs + n_rhs]
    n_in = n_lhs + n_rhs + n_extra
    extra_refs = refs[n_lhs + n_rhs:n_in]
    o_ref = refs[n_in + n_side_in]
    if side_body is not None:
        side_body(*refs[n_in:n_in + n_side_in], *refs[n_in + n_side_in + 1:])
    weights = [(r[0] if len(r.shape) == 3 else r[...]).astype(_BF16) for r in rhs_refs]
    rows_per_split = o_ref.shape[0] // row_splits
    for s in range(row_splits):
        rows = slice(s * rows_per_split, (s + 1) * rows_per_split)
        accs = [lax.dot_general(lhs_refs[li][rows, :], w, _NT_DIMS if tr else _NN_DIMS,
                                preferred_element_type=_F32)
                for li, w, tr in zip(pair_lhs, weights, rhs_transposed)]
        o_ref[rows, :] = epilogue(accs, [e[rows, :] for e in extra_refs]).astype(o_ref.dtype)


def _fused_matmul(name, lhs, pairs, extras, epilogue, n_out, out_dtype, side_job=None):
    tm, tn, prefetch_lhs, row_splits = TILES[name]
    m = lhs[0].shape[0]
    assert m % tm == 0 and n_out % tn == 0
    grid = (m // tm, n_out // tn)
    lhs_mode = {} if prefetch_lhs else {"pipeline_mode": pl.Buffered(1)}
    in_specs = [pl.BlockSpec((tm, a.shape[1]), lambda i, j: (i, 0), **lhs_mode) for a in lhs]
    in_specs += [_rhs_spec(r, tn) for _, r in pairs]
    in_specs += [pl.BlockSpec((tm, tn), lambda i, j: (i, j)) for _ in extras]
    out_specs = [pl.BlockSpec((tm, tn), lambda i, j: (i, j))]
    out_shape = [jax.ShapeDtypeStruct((m, n_out), out_dtype)]
    operands = list(lhs) + [p[1].array for p in pairs] + list(extras)
    side = None if side_job is None else side_job(grid[0] * grid[1])
    if side is not None:
        def per_step(spec):
            return pl.BlockSpec(spec.block_shape,
                                lambda i, j: spec.index_map(i * grid[1] + j))
        in_specs += [per_step(s) for s in side.in_specs]
        out_specs += [per_step(s) for s in side.out_specs]
        out_shape += list(side.out_shape)
        operands += list(side.arrays)
    kern = functools.partial(_mm_kernel, n_lhs=len(lhs), pair_lhs=tuple(p[0] for p in pairs),
                             rhs_transposed=tuple(p[1].transposed for p in pairs),
                             n_extra=len(extras), epilogue=epilogue,
                             side_body=None if side is None else side.body,
                             n_side_in=0 if side is None else len(side.in_specs),
                             row_splits=row_splits)
    outs = pl.pallas_call(
        kern,
        grid=grid,
        in_specs=in_specs,
        out_specs=out_specs,
        out_shape=out_shape,
        compiler_params=_compiler_params(2),
        name=name,
    )(*operands)
    return outs[0] if side is None else tuple(outs)


def _gating_kernel(za_ref, ng_ref, w_ref, bt_ref, o_ref):
    row = lax.broadcasted_iota(jnp.int32, (CHUNK, CHUNK), 0)
    col = lax.broadcasted_iota(jnp.int32, (CHUNK, CHUNK), 1)
    causal = col <= row
    for c in range(za_ref.shape[0] // CHUNK):
        rows = slice(c * CHUNK, (c + 1) * CHUNK)
        z = jax.nn.gelu(za_ref[rows, :])
        u = z[:, :A_WIDTH]
        v = z[:, A_WIDTH:]
        ms = jnp.mean(v * v, axis=-1, keepdims=True)
        vn = ((v * lax.rsqrt(ms + EPS)) * ng_ref[...]).astype(_BF16)
        for g in range(A_GROUPS):
            sl = slice(g * A_GROUP_DIM, (g + 1) * A_GROUP_DIM)
            w = jnp.where(causal, w_ref[g], 0.0).astype(_BF16)
            s = jnp.dot(w, vn[:, sl], preferred_element_type=_F32) + bt_ref[:, g:g + 1]
            o_ref[rows, sl] = (u[:, sl] * s).astype(o_ref.dtype)


def _spatial_gating(za, norm_g, w_s, b_s):
    m = za.shape[0]
    tm = ROW_BLOCK["spatial_gating"]
    return pl.pallas_call(
        _gating_kernel,
        grid=(m // tm,),
        in_specs=[pl.BlockSpec((tm, 2 * A_WIDTH), lambda i: (i, 0)),
                  pl.BlockSpec((1, A_WIDTH), lambda i: (0, 0)),
                  pl.BlockSpec((A_GROUPS, CHUNK, CHUNK), lambda i: (0, 0, 0)),
                  pl.BlockSpec((CHUNK, A_GROUPS), lambda i: (0, 0))],
        out_specs=pl.BlockSpec((tm, A_WIDTH), lambda i: (i, 0)),
        out_shape=jax.ShapeDtypeStruct((m, A_WIDTH), _BF16),
        compiler_params=_compiler_params(1),
        name="spatial_gating",
    )(za, norm_g.reshape(1, A_WIDTH), w_s, b_s.T)


def _lane_inv_freq(rot_dim, period):
    inv_freq = ROPE_THETA ** (-jnp.arange(0, rot_dim, 2, dtype=_F32) / rot_dim)
    head = jnp.concatenate([inv_freq, inv_freq, jnp.zeros((period - rot_dim,), _F32)])
    return jnp.tile(head, LANES // period).reshape(1, LANES)


def _rope_tables(pos, inv_freq_lanes, rot_dim, period):
    half = rot_dim // 2
    ang = pos * inv_freq_lanes
    cos, sin = jnp.cos(ang), jnp.sin(ang)
    lane = lax.broadcasted_iota(jnp.int32, ang.shape, 1) & (period - 1)
    c = jnp.where(lane < rot_dim, cos, 1.0)
    s1 = jnp.where(lane >= half, jnp.where(lane < rot_dim, sin, 0.0), 0.0)
    s2 = jnp.where(lane < half, -sin, 0.0)
    return c, s1, s2


def _rope(x, c, s1, s2, half):
    return x * c + pltpu.roll(x, half, 1) * s1 + pltpu.roll(x, LANES - half, 1) * s2


def _prep_kernel(att_ref, pos_ref, fb_ref, fi_ref,
                 q_ref, k_ref, vt_ref, qi_ref, ki_ref, wit_ref):
    pos = pos_ref[0]
    cb, s1b, s2b = _rope_tables(pos, fb_ref[...], B_ROT, B_HEAD_DIM)
    ci, s1i, s2i = _rope_tables(pos, fi_ref[...], IDX_ROT, IDX_DIM)
    for h in range(B_HEADS):
        x = att_ref[0, :, ATT_Q0 + h * LANES:ATT_Q0 + (h + 1) * LANES]
        q_ref[0, :, h * LANES:(h + 1) * LANES] = _rope(x, cb, s1b, s2b, B_ROT // 2).astype(_BF16)
    for h in range(B_KV_HEADS):
        x = att_ref[0, :, ATT_K0 + h * LANES:ATT_K0 + (h + 1) * LANES]
        k_ref[0, :, h * LANES:(h + 1) * LANES] = _rope(x, cb, s1b, s2b, B_ROT // 2).astype(_BF16)
    vt_ref[0, 0] = att_ref[0, :, ATT_V0:ATT_V0 + B_KV_WIDTH].T.astype(_BF16)
    lane = lax.broadcasted_iota(jnp.int32, ci.shape, 1)
    is_ki = lane < IDX_DIM
    heads_per_group = LANES // IDX_DIM
    for j in range(IDX_WIDTH // LANES):
        x = att_ref[0, :, ATT_QI0 + j * LANES:ATT_QI0 + (j + 1) * LANES]
        y = _rope(x, ci, s1i, s2i, IDX_ROT // 2)
        for r in range(heads_per_group):
            h = j * heads_per_group + r
            yr = y if r == 0 else pltpu.roll(y, LANES - r * IDX_DIM, 1)
            qi_ref[0, :, h * LANES:(h + 1) * LANES] = jnp.where(is_ki, yr, 0.0).astype(_BF16)
    x = att_ref[0, :, ATT_KW0:ATT_KW0 + LANES]
    y = _rope(x, jnp.where(is_ki, ci, 1.0), jnp.where(is_ki, s1i, 0.0),
              jnp.where(is_ki, s2i, 0.0), IDX_ROT // 2)
    ki_ref[0] = jnp.where(is_ki, y, 0.0).astype(_BF16)
    wit_ref[0] = y.T[IDX_DIM:IDX_DIM + IDX_HEADS, :]


def _prep_side_job(att, positions, n_steps):
    bsz, seq, _ = att.shape
    ts = bsz * seq // n_steps
    assert ts * n_steps == bsz * seq and seq % ts == 0 and KEY_CHUNK % ts == 0
    assert ts % (2 * SUBLANES) == 0
    per_seq = seq // ts
    per_chunk = KEY_CHUNK // ts

    def tokens(width):
        return pl.BlockSpec((1, ts, width), lambda s: (s // per_seq, s % per_seq, 0))

    def out(width, dtype):
        return tokens(width), jax.ShapeDtypeStruct((bsz, seq, width), dtype)

    vt_out = (pl.BlockSpec((1, 1, B_KV_WIDTH, ts),
                           lambda s: (s // per_seq, (s % per_seq) // per_chunk, 0,
                                      (s % per_seq) % per_chunk)),
              jax.ShapeDtypeStruct((bsz, seq // KEY_CHUNK, B_KV_WIDTH, KEY_CHUNK), _BF16))
    wit_out = (pl.BlockSpec((1, IDX_HEADS, ts), lambda s: (s // per_seq, 0, s % per_seq)),
               jax.ShapeDtypeStruct((bsz, IDX_HEADS, seq), _F32))
    outs = [out(B_WIDTH, _BF16), out(B_KV_WIDTH, _BF16), vt_out,
            out(IDX_HEADS * LANES, _BF16), out(LANES, _BF16), wit_out]
    freq_spec = pl.BlockSpec((1, LANES), lambda s: (0, 0))
    return SideJob(_prep_kernel,
                   (att, positions.astype(_F32).reshape(bsz, seq, 1),
                    _lane_inv_freq(B_ROT, B_HEAD_DIM), _lane_inv_freq(IDX_ROT, IDX_DIM)),
                   (tokens(ATT_WIDTH), tokens(1), freq_spec, freq_spec),
                   tuple(o[0] for o in outs), tuple(o[1] for o in outs))


def _ordered_int_to_float(key):
    bits = jnp.where(key < 0, key ^ jnp.int32(0x7FFFFFFF), key)
    return lax.bitcast_convert_type(bits, _F32)


def _fold_rows(x, op):
    while x.shape[0] > SUBLANES:
        half = x.shape[0] // 2
        x = op(x[:half], x[half:])
    return x


_DSA_N_IN = 6


def _dsa_kernel(*refs, n_sel, side_body, n_side_in, n_side_out):
    o_at = _DSA_N_IN + n_side_in
    if side_body is not None:
        side_body(*refs[_DSA_N_IN:o_at], *refs[o_at + 1:o_at + 1 + n_side_out])
    own = refs[:_DSA_N_IN] + (refs[o_at],) + refs[o_at + 1 + n_side_out:]
    seq = own[4].shape[1]
    n_chunks = (pl.program_id(1) * Q_BLOCK + Q_BLOCK + KEY_CHUNK - 1) // KEY_CHUNK
    for n in range(1, seq // KEY_CHUNK + 1):
        pl.when(n_chunks == n)(functools.partial(_dsa_block, *own, n_sel=n_sel, n_chunks=n))


def _dsa_block(qi_ref, wit_ref, ki_ref, q_ref, k_ref, vt_ref, o_ref,
               score_ref, bias_ref, logit_ref, acc_ref, *, n_sel, n_chunks):
    t = Q_BLOCK
    blk = pl.program_id(1)
    qpos = blk * t + lax.broadcasted_iota(jnp.int32, (1, t), 1)
    kiota = lax.broadcasted_iota(jnp.int32, (KEY_CHUNK, 1), 0)
    idx_scale = (IDX_DIM ** -0.5) * (IDX_HEADS ** -0.5)

    def rows(c):
        return slice(c * KEY_CHUNK, (c + 1) * KEY_CHUNK)

    def for_chunks(body, carry):
        for c in range(n_chunks):
            carry = body(c, carry)
        return carry

    def indexer_chunk(c, carry):
        ki = ki_ref[0, rows(c), :]
        acc = jnp.zeros((KEY_CHUNK, t), _F32)
        for h in range(0, IDX_HEADS, 2):
            qpair = jnp.concatenate([qi_ref[0, :, h * LANES:(h + 1) * LANES],
                                     qi_ref[0, :, (h + 1) * LANES:(h + 2) * LANES]], axis=0)
            d = lax.dot_general(ki, qpair, _NT_DIMS, preferred_element_type=_F32)
            acc = acc + wit_ref[0, h:h + 1, :] * jnp.maximum(d[:, :t], 0.0)
            acc = acc + wit_ref[0, h + 1:h + 2, :] * jnp.maximum(d[:, t:], 0.0)
        kpos = c * KEY_CHUNK + kiota
        score_ref[rows(c), :] = jnp.where(kpos <= qpos, acc * idx_scale + 0.0, _NEG_INF)
        return carry

    for_chunks(indexer_chunk, 0)

    def count(indicator):
        def chunk(c, part):
            return part + _fold_rows(indicator(score_ref[rows(c), :], c), jnp.add)
        part = for_chunks(chunk, jnp.zeros((SUBLANES, t), _F32))
        return jnp.sum(part, axis=0, keepdims=True)

    def count_ge(cand_f):
        return count(lambda s, c: jnp.where(s >= cand_f, 1.0, 0.0))

    def search():
        int_min = jnp.int32(-2 ** 31)
        base = jnp.where(count_ge(jnp.zeros((1, t), _F32)) >= n_sel, jnp.int32(0), int_min)

        def search_step(it, base):
            cand = base | lax.shift_left(jnp.int32(1), jnp.int32(30) - it)
            cnt = count_ge(_ordered_int_to_float(cand))
            return jnp.where(cnt >= n_sel, cand, base)

        return lax.fori_loop(0, 31, search_step, base)

    if (n_chunks - 1) * KEY_CHUNK >= n_sel:
        base = search()
    else:
        base = lax.cond((blk + 1) * t <= n_sel, lambda: jnp.zeros((1, t), jnp.int32), search)
    thr = jnp.where(qpos + 1 <= n_sel, _NEG_INF, _ordered_int_to_float(base))

    def bias_chunk(c, part):
        s = score_ref[rows(c), :]
        sel = jnp.where(c * KEY_CHUNK + kiota <= qpos, jnp.where(s >= thr, 1.0, 0.0), 0.0)
        bias_ref[rows(c), :] = jnp.where(sel > 0.0, 0.0, _NEG_INF)
        return part + _fold_rows(sel, jnp.add)

    n_ge = jnp.sum(for_chunks(bias_chunk, jnp.zeros((SUBLANES, t), _F32)), axis=0, keepdims=True)

    @pl.when(jnp.max(n_ge) > n_sel)
    def _():
        nxt = _ordered_int_to_float(base + 1)
        tied = n_ge > n_sel
        n_above = count(lambda s, c: jnp.where(s >= nxt, 1.0, 0.0))
        need = jnp.where(tied, n_sel - n_above, 0.0)
        no_index = float(k_ref.shape[1])
        front_v = jnp.where(tied, jnp.inf, _NEG_INF)
        front_i = jnp.where(tied, -1.0, no_index)

        def beyond(s, kposf, front_v, front_i):
            after = jnp.where(s < front_v, 1.0,
                              jnp.where(s == front_v, jnp.where(kposf > front_i, 1.0, 0.0), 0.0))
            return jnp.where(s >= thr, jnp.where(s < nxt, after, 0.0), 0.0)

        def kposf(c):
            return (c * KEY_CHUNK + kiota).astype(_F32)

        def advance(_, carry):
            front_v, front_i, need = carry

            def best_value(c, part):
                s = score_ref[rows(c), :]
                cand = jnp.where(beyond(s, kposf(c), front_v, front_i) > 0.0, s, _NEG_INF)
                return jnp.maximum(part, _fold_rows(cand, jnp.maximum))

            v = jnp.max(for_chunks(best_value, jnp.full((SUBLANES, t), _NEG_INF, _F32)),
                        axis=0, keepdims=True)

            def first_index(c, part):
                s = score_ref[rows(c), :]
                hit = jnp.where(s == v, beyond(s, kposf(c), front_v, front_i), 0.0)
                cand = jnp.where(hit > 0.0, kposf(c), no_index)
                return jnp.minimum(part, _fold_rows(cand, jnp.minimum))

            i = jnp.min(for_chunks(first_index, jnp.full((SUBLANES, t), no_index, _F32)),
                        axis=0, keepdims=True)
            active = need > 0.0
            return (jnp.where(active, v, front_v), jnp.where(active, i, front_i),
                    jnp.where(active, need - 1.0, need))

        front_v, front_i, _ = lax.fori_loop(0, jnp.max(need).astype(jnp.int32), advance,
                                            (front_v, front_i, need))

        def tie_chunk(c, carry):
            s = score_ref[rows(c), :]
            upto = jnp.where(s > front_v, 0.0,
                             jnp.where(s == front_v,
                                       jnp.where(kposf(c) <= front_i, 0.0, _NEG_INF), _NEG_INF))
            keep = jnp.where(s >= nxt, 0.0, jnp.where(s >= thr, upto, _NEG_INF))
            bias_ref[rows(c), :] = jnp.where(c * KEY_CHUNK + kiota <= qpos, keep, _NEG_INF)
            return carry

        for_chunks(tie_chunk, 0)

    grp = B_HEADS // B_KV_HEADS
    gw = grp * t
    exp2_scale = (B_HEAD_DIM ** -0.5) * 1.4426950408889634

    def logit_chunk(c, mx):
        bias = jnp.concatenate([bias_ref[rows(c), :]] * grp, axis=1)
        folded = []
        for g in range(B_KV_HEADS):
            qg = jnp.concatenate(
                [q_ref[0, :, (g * grp + hh) * B_HEAD_DIM:(g * grp + hh + 1) * B_HEAD_DIM]
                 for hh in range(grp)], axis=0)
            l = lax.dot_general(k_ref[0, rows(c), g * B_HEAD_DIM:(g + 1) * B_HEAD_DIM], qg,
                                _NT_DIMS, preferred_element_type=_F32) + bias
            logit_ref[rows(c), g * gw:(g + 1) * gw] = l
            folded.append(_fold_rows(l, jnp.maximum))
        return jnp.maximum(mx, jnp.concatenate(folded, axis=1))

    mx = for_chunks(logit_chunk, jnp.full((SUBLANES, B_HEADS * t), _NEG_INF, _F32))
    mx = jnp.max(mx, axis=0, keepdims=True)

    acc_ref[...] = jnp.zeros_like(acc_ref)
    ones_rows = jnp.ones((acc_ref.shape[0] - B_HEAD_DIM, KEY_CHUNK), _BF16)

    def pv_chunk(c, carry):
        for g in range(B_KV_HEADS):
            cols = slice(g * gw, (g + 1) * gw)
            p = jnp.exp2((logit_ref[rows(c), cols] - mx[:, cols]) * exp2_scale)
            vt_ext = jnp.concatenate(
                [vt_ref[0, c, g * B_HEAD_DIM:(g + 1) * B_HEAD_DIM, :], ones_rows], axis=0)
            acc_ref[:, cols] += jnp.dot(vt_ext, p.astype(_BF16), preferred_element_type=_F32)
        return carry

    for_chunks(pv_chunk, 0)
    out_t = acc_ref[:B_HEAD_DIM, :] / acc_ref[B_HEAD_DIM:B_HEAD_DIM + 1, :]
    for h in range(B_HEADS):
        o_ref[0, :, h * B_HEAD_DIM:(h + 1) * B_HEAD_DIM] = (
            out_t[:, h * t:(h + 1) * t].T.astype(o_ref.dtype))


def _dsa_attention(q, k, vt, qi, ki, wit, side_job=None):
    bsz, seq, _ = q.shape
    assert Q_BLOCK == LANES and seq % KEY_CHUNK == 0
    n_sel = min(TOPK_MAX, seq // 4)
    n_blk = seq // Q_BLOCK

    def q_spec(width):
        return pl.BlockSpec((1, Q_BLOCK, width), lambda b, i: (b, i, 0))

    def kv_spec(width):
        return pl.BlockSpec((1, seq, width), lambda b, i: (b, 0, 0))

    def per_step(spec):
        return pl.BlockSpec(spec.block_shape, lambda b, i: spec.index_map(b * n_blk + i))

    side = None if side_job is None else side_job(bsz * n_blk)
    side_in = [] if side is None else [per_step(s) for s in side.in_specs]
    side_out = [] if side is None else [per_step(s) for s in side.out_specs]
    outs = pl.pallas_call(
        functools.partial(_dsa_kernel, n_sel=n_sel,
                          side_body=None if side is None else side.body,
                          n_side_in=len(side_in), n_side_out=len(side_out)),
        grid=(bsz, n_blk),
        in_specs=[q_spec(IDX_HEADS * LANES),
                  pl.BlockSpec((1, IDX_HEADS, Q_BLOCK), lambda b, i: (b, 0, i)),
                  kv_spec(LANES), q_spec(B_WIDTH), kv_spec(B_KV_WIDTH),
                  pl.BlockSpec((1, seq // KEY_CHUNK, B_KV_WIDTH, KEY_CHUNK),
                               lambda b, i: (b, 0, 0, 0))] + side_in,
        out_specs=[q_spec(B_WIDTH)] + side_out,
        out_shape=[jax.ShapeDtypeStruct((bsz, seq, B_WIDTH), _BF16)]
        + ([] if side is None else list(side.out_shape)),
        scratch_shapes=[pltpu.VMEM((seq, Q_BLOCK), _F32), pltpu.VMEM((seq, Q_BLOCK), _F32),
                        pltpu.VMEM((seq, B_HEADS * Q_BLOCK), _F32),
                        pltpu.VMEM((B_HEAD_DIM + 2 * SUBLANES, B_HEADS * Q_BLOCK), _F32)],
        compiler_params=_compiler_params(2),
        name="dsa_attention",
    )(qi, wit, ki, q, k, vt, *(() if side is None else side.arrays))
    return outs[0] if side is None else tuple(outs)


def _xattn_kernel(q_ref, k_ref, v_ref, o_ref):
    scale = X_HEAD_DIM ** -0.5
    for h in range(X_HEADS):
        hs = slice(h * X_HEAD_DIM, (h + 1) * X_HEAD_DIM)
        logits = lax.dot_general(q_ref[0, :, hs], k_ref[0, :, hs], _NT_DIMS,
                                 preferred_element_type=_F32) * scale
        mx = jnp.max(logits, axis=-1, keepdims=True)
        p = jnp.exp(logits - mx)
        denom = jnp.sum(p, axis=-1, keepdims=True)
        o = jnp.dot(p.astype(_BF16), v_ref[0, :, hs], preferred_element_type=_F32)
        o_ref[0, :, hs] = (o / denom).astype(o_ref.dtype)


def _cross_attention(q, k, v):
    bsz, seq, _ = q.shape
    tq = ROW_BLOCK["cross_attention"]
    m = k.shape[1]
    mem_spec = pl.BlockSpec((1, m, X_WIDTH), lambda b, i: (b, 0, 0))
    return pl.pallas_call(
        _xattn_kernel,
        grid=(bsz, seq // tq),
        in_specs=[pl.BlockSpec((1, tq, X_WIDTH), lambda b, i: (b, i, 0)), mem_spec, mem_spec],
        out_specs=pl.BlockSpec((1, tq, X_WIDTH), lambda b, i: (b, i, 0)),
        out_shape=jax.ShapeDtypeStruct((bsz, seq, X_WIDTH), _BF16),
        compiler_params=_compiler_params(2),
        name="cross_attention",
    )(q, k, v)


def _norm_proj_kernel(x_ref, g_ref, w_ref, o_ref, w_bf16_ref):
    @pl.when(pl.program_id(0) == 0)
    def _():
        w_bf16_ref[...] = w_ref[...].astype(_BF16)

    x = x_ref[...]
    ms = jnp.mean(x * x, axis=-1, keepdims=True)
    h = ((x * lax.rsqrt(ms + EPS)) * g_ref[...]).astype(_BF16)
    o_ref[...] = jnp.dot(h, w_bf16_ref[...], preferred_element_type=_F32).astype(o_ref.dtype)


def _norm_proj(x, g, w_stack, layer, out_dtype, name):
    m, d = x.shape
    tm = ROW_BLOCK["norm_proj"]
    n = w_stack.shape[2]
    return pl.pallas_call(
        _norm_proj_kernel,
        grid=(m // tm,),
        in_specs=[pl.BlockSpec((tm, d), lambda i: (i, 0)),
                  pl.BlockSpec((1, d), lambda i: (0, 0)),
                  pl.BlockSpec((None, d, n), lambda i: (layer, 0, 0),
                               pipeline_mode=pl.Buffered(1))],
        out_specs=pl.BlockSpec((tm, n), lambda i: (i, 0)),
        out_shape=jax.ShapeDtypeStruct((m, n), out_dtype),
        scratch_shapes=[pltpu.VMEM((d, n), _BF16)],
        compiler_params=_compiler_params(1),
        name=name,
    )(x, g.reshape(1, d), w_stack)


def _xattn_out_kernel(o_ref, w_ref, x_ref, g_ref, x_out_ref, h_out_ref):
    x = x_ref[...] + jnp.dot(o_ref[...], w_ref[...], preferred_element_type=_F32)
    x_out_ref[...] = x
    ms = jnp.mean(x * x, axis=-1, keepdims=True)
    h_out_ref[...] = ((x * lax.rsqrt(ms + EPS)) * g_ref[...]).astype(h_out_ref.dtype)


def _xattn_out_and_norm(o, w_bf16, x, g):
    m, k = o.shape
    tm = ROW_BLOCK["xattn_out_norm"]
    d = x.shape[1]
    row_spec = pl.BlockSpec((tm, d), lambda i: (i, 0))
    return pl.pallas_call(
        _xattn_out_kernel,
        grid=(m // tm,),
        in_specs=[pl.BlockSpec((tm, k), lambda i: (i, 0)),
                  pl.BlockSpec((k, d), lambda i: (0, 0), pipeline_mode=pl.Buffered(1)),
                  row_spec,
                  pl.BlockSpec((1, d), lambda i: (0, 0))],
        out_specs=[row_spec, row_spec],
        out_shape=[jax.ShapeDtypeStruct((m, d), _F32), jax.ShapeDtypeStruct((m, d), _BF16)],
        compiler_params=_compiler_params(1),
        name="xattn_out_norm",
    )(o, w_bf16, x, g.reshape(1, d))


def kernel(x, mem, positions, norm_mix_g, w_in, a_norm_g, a_spatial_w, a_spatial_b, p_a, p_b,
           w_out, norm_x_g, norm_mem_g, xq_w, xk_w, xv_w, xo_w, norm_ffn_g, ffn_w1, ffn_w3,
           ffn_w2, final_norm_g):
    bsz, seq, d = x.shape
    m = bsz * seq
    depth = w_in.shape[0]
    ffn_hidden = ffn_w1.shape[-1]

    xf = x.reshape(m, d)
    mem_f = mem.reshape(bsz * MEM_LEN, d)
    w_in_t = jnp.transpose(w_in, (0, 2, 1))
    for l in range(depth):
        def w_in_cols(col0):
            return Rhs(w_in_t, l, col0, transposed=True)

        h = _rmsnorm(xf, norm_mix_g[l], _BF16)
        att = _fused_matmul("proj_att", [h], [(0, w_in_cols(ZA_END))], [], _ep_identity,
                            ATT_WIDTH, _F32)
        za, q, k, vt, qi, ki, wit = _fused_matmul(
            "proj_za", [h], [(0, w_in_cols(0))], [], _ep_identity, ZA_END, _F32,
            side_job=functools.partial(_prep_side_job, att.reshape(bsz, seq, ATT_WIDTH),
                                       positions))
        y_a = _spatial_gating(za, a_norm_g[l], a_spatial_w[l], a_spatial_b[l])
        def merge_weight_casts(n_steps):
            return _merge_side_jobs(
                _cast_side_job(p_a, l, n_steps), _cast_side_job(p_b, l, n_steps),
                _cast_side_job(w_in_t, l, n_steps, row0=ATT_END, rows=2 * d))

        y_b, pa_bf16, pb_bf16, gate_wt_bf16 = _dsa_attention(q, k, vt, qi, ki, wit,
                                                              side_job=merge_weight_casts)
        merged = _fused_matmul(
            "gated_merge", [y_a, y_b.reshape(m, B_WIDTH), h],
            [(0, Rhs(pa_bf16)), (1, Rhs(pb_bf16)), (2, Rhs(gate_wt_bf16, transposed=True)),
             (2, Rhs(gate_wt_bf16, col0=d, transposed=True))],
            [], _ep_gated_merge, d, _BF16)
        x1, xo_bf16 = _fused_matmul("out_proj", [merged], [(0, Rhs(w_out, l))], [xf],
                                    _ep_residual, d, _F32,
                                    side_job=functools.partial(_cast_side_job, xo_w, l))

        qx = _norm_proj(x1, norm_x_g[l], xq_w, l, _BF16, "xattn_q")
        kx = _norm_proj(mem_f, norm_mem_g[l], xk_w, l, _BF16, "xattn_k")
        vx = _norm_proj(mem_f, norm_mem_g[l], xv_w, l, _BF16, "xattn_v")
        ox = _cross_attention(qx.reshape(bsz, seq, X_WIDTH),
                              kx.reshape(bsz, MEM_LEN, X_WIDTH),
                              vx.reshape(bsz, MEM_LEN, X_WIDTH)).reshape(m, X_WIDTH)
        x2, h2 = _xattn_out_and_norm(ox, xo_bf16, x1, norm_ffn_g[l])

        act, w2_bf16 = _fused_matmul("ffn_up", [h2],
                                     [(0, Rhs(ffn_w1, l)), (0, Rhs(ffn_w3, l))],
                                     [], _ep_swiglu, ffn_hidden, _BF16,
                                     side_job=functools.partial(_cast_side_job, ffn_w2, l))
        xf = _fused_matmul("ffn_down", [act], [(0, Rhs(w2_bf16))], [x2], _ep_residual, d, _F32)

    return _rmsnorm(xf, final_norm_g, _F32).reshape(bsz, seq, d)
```

```python
import functools
from typing import Callable, NamedTuple, Optional

import jax
import jax.numpy as jnp
from jax import lax
from jax.experimental import pallas as pl
from jax.experimental.pallas import tpu as pltpu

D_MODEL = 4096
MEM_LEN = 256
EPS = 1e-6
ROPE_THETA = 500000.0
CHUNK = 128
A_GROUPS = 16
A_WIDTH = D_MODEL // 2
A_GROUP_DIM = A_WIDTH // A_GROUPS
B_HEADS = 16
B_HEAD_DIM = 128
B_KV_HEADS = 4
B_WIDTH = B_HEADS * B_HEAD_DIM
B_KV_WIDTH = B_KV_HEADS * B_HEAD_DIM
B_ROT = B_HEAD_DIM // 4
IDX_HEADS = 16
IDX_DIM = 64
IDX_WIDTH = IDX_HEADS * IDX_DIM
IDX_ROT = IDX_DIM // 4
TOPK_MAX = 256
Q_BLOCK = 128
X_HEADS = 4
X_HEAD_DIM = 256
X_WIDTH = X_HEADS * X_HEAD_DIM
IN_SIZES = (2 * A_WIDTH, B_WIDTH, B_KV_WIDTH, B_KV_WIDTH, IDX_WIDTH, IDX_DIM, IDX_HEADS,
            2 * D_MODEL)

LANES = 128
SUBLANES = 8
KEY_CHUNK = 512
PROJ_TN = 512
ZA_END = IN_SIZES[0]
ATT_END = ZA_END + sum(IN_SIZES[1:7])
ATT_Q0 = 0
ATT_K0 = ATT_Q0 + B_WIDTH
ATT_V0 = ATT_K0 + B_KV_WIDTH
ATT_QI0 = ATT_V0 + B_KV_WIDTH
ATT_KW0 = ATT_QI0 + IDX_WIDTH
ATT_WIDTH = -(-(ATT_END - ZA_END) // PROJ_TN) * PROJ_TN
assert ZA_END % PROJ_TN == 0 and ATT_KW0 + LANES <= ATT_WIDTH

VMEM_LIMIT_BYTES = 60 * 1024 * 1024


class Tile(NamedTuple):
    tm: int
    tn: int
    prefetch_lhs: bool = False
    row_splits: int = 1


TILES = {
    "proj_att": Tile(1024, PROJ_TN, prefetch_lhs=True),
    "proj_za": Tile(1024, 512, prefetch_lhs=True),
    "gated_merge": Tile(1024, 256, prefetch_lhs=True, row_splits=4),
    "out_proj": Tile(1024, 512, prefetch_lhs=True),
    "ffn_up": Tile(2048, 256, row_splits=4),
    "ffn_down": Tile(512, 512, prefetch_lhs=True),
}
ROW_BLOCK = {
    "rmsnorm": 512,
    "spatial_gating": 4 * CHUNK,
    "cross_attention": 1024,
    "norm_proj": 512,
    "xattn_out_norm": 512,
}

_BF16 = jnp.bfloat16
_F32 = jnp.float32
_NEG_INF = float("-inf")
_NN_DIMS = (((1,), (0,)), ((), ()))
_NT_DIMS = (((1,), (1,)), ((), ()))


def _compiler_params(n_axes):
    return pltpu.CompilerParams(
        dimension_semantics=("arbitrary",) * n_axes,
        vmem_limit_bytes=VMEM_LIMIT_BYTES,
    )


def _rmsnorm_kernel(x_ref, g_ref, o_ref):
    x = x_ref[...]
    ms = jnp.mean(x * x, axis=-1, keepdims=True)
    y = x * lax.rsqrt(ms + EPS)
    o_ref[...] = (y * g_ref[...]).astype(o_ref.dtype)


def _rmsnorm(x, g, out_dtype):
    m, d = x.shape
    tm = ROW_BLOCK["rmsnorm"]
    return pl.pallas_call(
        _rmsnorm_kernel,
        grid=(m // tm,),
        in_specs=[pl.BlockSpec((tm, d), lambda i: (i, 0)),
                  pl.BlockSpec((1, d), lambda i: (0, 0))],
        out_specs=pl.BlockSpec((tm, d), lambda i: (i, 0)),
        out_shape=jax.ShapeDtypeStruct((m, d), out_dtype),
        compiler_params=_compiler_params(1),
        name="rmsnorm",
    )(x, g.reshape(1, d))


def _ep_identity(accs, extras):
    return accs[0]


def _ep_residual(accs, extras):
    return extras[0] + accs[0]


def _ep_gated_merge(accs, extras):
    y_a, y_b, gate_a, gate_b = accs
    return jax.nn.sigmoid(gate_a) * y_a + jax.nn.sigmoid(gate_b) * y_b


def _ep_swiglu(accs, extras):
    return jax.nn.silu(accs[0]) * accs[1]


class Rhs(NamedTuple):
    array: jax.Array
    layer: Optional[int] = None
    col0: int = 0
    transposed: bool = False


def _rhs_spec(r, tn):
    if r.transposed and r.layer is None:
        assert r.col0 % tn == 0
        off = r.col0 // tn
        return pl.BlockSpec((tn, r.array.shape[1]), lambda i, j: (j + off, 0))
    if r.transposed:
        k = r.array.shape[2]
        if r.col0 % tn == 0:
            return pl.BlockSpec((None, tn, k), lambda i, j: (r.layer, j + r.col0 // tn, 0))
        assert r.col0 % SUBLANES == 0
        return pl.BlockSpec((pl.Element(1), pl.Element(tn), pl.Element(k)),
                            lambda i, j: (r.layer, (r.col0 // SUBLANES + j * (tn // SUBLANES))
                                          * SUBLANES, 0))
    assert r.col0 % tn == 0
    off = r.col0 // tn
    if r.layer is None:
        return pl.BlockSpec((r.array.shape[0], tn), lambda i, j: (0, j + off))
    return pl.BlockSpec((None, r.array.shape[1], tn), lambda i, j: (r.layer, 0, j + off))


class SideJob(NamedTuple):
    body: Callable
    arrays: tuple
    in_specs: tuple
    out_specs: tuple
    out_shape: tuple


def _cast_body(x_ref, o_ref):
    x = x_ref[0] if len(x_ref.shape) == 3 else x_ref[...]
    o_ref[...] = x.astype(o_ref.dtype)


def _cast_side_job(w_stack, layer, n_steps, row0=0, rows=None):
    rows = w_stack.shape[1] if rows is None else rows
    cols = w_stack.shape[2]
    slab = rows // n_steps
    assert slab * n_steps == rows and slab % (2 * SUBLANES) == 0
    if row0 % slab == 0:
        in_spec = pl.BlockSpec((None, slab, cols), lambda s: (layer, s + row0 // slab, 0))
    else:
        assert row0 % SUBLANES == 0
        in_spec = pl.BlockSpec(
            (pl.Element(1), pl.Element(slab), pl.Element(cols)),
            lambda s: (layer, (row0 // SUBLANES + s * (slab // SUBLANES)) * SUBLANES, 0))
    return SideJob(_cast_body, (w_stack,), (in_spec,),
                   (pl.BlockSpec((slab, cols), lambda s: (s, 0)),),
                   (jax.ShapeDtypeStruct((rows, cols), _BF16),))


def _merge_side_jobs(*jobs):
    n_in = [len(j.in_specs) for j in jobs]
    n_out = [len(j.out_specs) for j in jobs]

    def body(*refs):
        ins, outs = refs[:sum(n_in)], refs[sum(n_in):]
        i = o = 0
        for j, ni, no in zip(jobs, n_in, n_out):
            j.body(*ins[i:i + ni], *outs[o:o + no])
            i, o = i + ni, o + no

    return SideJob(body, sum((j.arrays for j in jobs), ()), sum((j.in_specs for j in jobs), ()),
                   sum((j.out_specs for j in jobs), ()), sum((j.out_shape for j in jobs), ()))


def _mm_kernel(*refs, n_lhs, pair_lhs, rhs_transposed, n_extra, epilogue, side_body, n_side_in,
               row_splits):
    n_rhs = len(pair_lhs)
    lhs_refs = refs[:n_lhs]
    rhs_refs = refs[n_lhs:n_lhs + n_rhs]
    n_in = n_lhs + n_rhs + n_extra
    extra_refs = refs[n_lhs + n_rhs:n_in]
    o_ref = refs[n_in + n_side_in]
    if side_body is not None:
        side_body(*refs[n_in:n_in + n_side_in], *refs[n_in + n_side_in + 1:])
    weights = [(r[0] if len(r.shape) == 3 else r[...]).astype(_BF16) for r in rhs_refs]
    rows_per_split = o_ref.shape[0] // row_splits
    for s in range(row_splits):
        rows = slice(s * rows_per_split, (s + 1) * rows_per_split)
        accs = [lax.dot_general(lhs_refs[li][rows, :], w, _NT_DIMS if tr else _NN_DIMS,
                                preferred_element_type=_F32)
                for li, w, tr in zip(pair_lhs, weights, rhs_transposed)]
        o_ref[rows, :] = epilogue(accs, [e[rows, :] for e in extra_refs]).astype(o_ref.dtype)


def _fused_matmul(name, lhs, pairs, extras, epilogue, n_out, out_dtype, side_job=None):
    tm, tn, prefetch_lhs, row_splits = TILES[name]
    m = lhs[0].shape[0]
    assert m % tm == 0 and n_out % tn == 0
    grid = (m // tm, n_out // tn)
    lhs_mode = {} if prefetch_lhs else {"pipeline_mode": pl.Buffered(1)}
    in_specs = [pl.BlockSpec((tm, a.shape[1]), lambda i, j: (i, 0), **lhs_mode) for a in lhs]
    in_specs += [_rhs_spec(r, tn) for _, r in pairs]
    in_specs += [pl.BlockSpec((tm, tn), lambda i, j: (i, j)) for _ in extras]
    out_specs = [pl.BlockSpec((tm, tn), lambda i, j: (i, j))]
    out_shape = [jax.ShapeDtypeStruct((m, n_out), out_dtype)]
    operands = list(lhs) + [p[1].array for p in pairs] + list(extras)
    side = None if side_job is None else side_job(grid[0] * grid[1])
    if side is not None:
        def per_step(spec):
            return pl.BlockSpec(spec.block_shape,
                                lambda i, j: spec.index_map(i * grid[1] + j))
        in_specs += [per_step(s) for s in side.in_specs]
        out_specs += [per_step(s) for s in side.out_specs]
        out_shape += list(side.out_shape)
        operands += list(side.arrays)
    kern = functools.partial(_mm_kernel, n_lhs=len(lhs), pair_lhs=tuple(p[0] for p in pairs),
                             rhs_transposed=tuple(p[1].transposed for p in pairs),
                             n_extra=len(extras), epilogue=epilogue,
                             side_body=None if side is None else side.body,
                             n_side_in=0 if side is None else len(side.in_specs),
                             row_splits=row_splits)
    outs = pl.pallas_call(
        kern,
        grid=grid,
        in_specs=in_specs,
        out_specs=out_specs,
        out_shape=out_shape,
        compiler_params=_compiler_params(2),
        name=name,
    )(*operands)
    return outs[0] if side is None else tuple(outs)


def _gating_kernel(za_ref, ng_ref, w_ref, bt_ref, o_ref):
    row = lax.broadcasted_iota(jnp.int32, (CHUNK, CHUNK), 0)
    col = lax.broadcasted_iota(jnp.int32, (CHUNK, CHUNK), 1)
    causal = col <= row
    for c in range(za_ref.shape[0] // CHUNK):
        rows = slice(c * CHUNK, (c + 1) * CHUNK)
        z = jax.nn.gelu(za_ref[rows, :])
        u = z[:, :A_WIDTH]
        v = z[:, A_WIDTH:]
        ms = jnp.mean(v * v, axis=-1, keepdims=True)
        vn = ((v * lax.rsqrt(ms + EPS)) * ng_ref[...]).astype(_BF16)
        for g in range(A_GROUPS):
            sl = slice(g * A_GROUP_DIM, (g + 1) * A_GROUP_DIM)
            w = jnp.where(causal, w_ref[g], 0.0).astype(_BF16)
            s = jnp.dot(w, vn[:, sl], preferred_element_type=_F32) + bt_ref[:, g:g + 1]
            o_ref[rows, sl] = (u[:, sl] * s).astype(o_ref.dtype)


def _spatial_gating(za, norm_g, w_s, b_s):
    m = za.shape[0]
    tm = ROW_BLOCK["spatial_gating"]
    return pl.pallas_call(
        _gating_kernel,
        grid=(m // tm,),
        in_specs=[pl.BlockSpec((tm, 2 * A_WIDTH), lambda i: (i, 0)),
                  pl.BlockSpec((1, A_WIDTH), lambda i: (0, 0)),
                  pl.BlockSpec((A_GROUPS, CHUNK, CHUNK), lambda i: (0, 0, 0)),
                  pl.BlockSpec((CHUNK, A_GROUPS), lambda i: (0, 0))],
        out_specs=pl.BlockSpec((tm, A_WIDTH), lambda i: (i, 0)),
        out_shape=jax.ShapeDtypeStruct((m, A_WIDTH), _BF16),
        compiler_params=_compiler_params(1),
        name="spatial_gating",
    )(za, norm_g.reshape(1, A_WIDTH), w_s, b_s.T)


def _lane_inv_freq(rot_dim, period):
    inv_freq = ROPE_THETA ** (-jnp.arange(0, rot_dim, 2, dtype=_F32) / rot_dim)
    head = jnp.concatenate([inv_freq, inv_freq, jnp.zeros((period - rot_dim,), _F32)])
    return jnp.tile(head, LANES // period).reshape(1, LANES)


def _rope_tables(pos, inv_freq_lanes, rot_dim, period):
    half = rot_dim // 2
    ang = pos * inv_freq_lanes
    cos, sin = jnp.cos(ang), jnp.sin(ang)
    lane = lax.broadcasted_iota(jnp.int32, ang.shape, 1) & (period - 1)
    c = jnp.where(lane < rot_dim, cos, 1.0)
    s1 = jnp.where(lane >= half, jnp.where(lane < rot_dim, sin, 0.0), 0.0)
    s2 = jnp.where(lane < half, -sin, 0.0)
    return c, s1, s2


def _rope(x, c, s1, s2, half):
    return x * c + pltpu.roll(x, half, 1) * s1 + pltpu.roll(x, LANES - half, 1) * s2


def _prep_kernel(att_ref, pos_ref, fb_ref, fi_ref,
                 q_ref, k_ref, vt_ref, qi_ref, ki_ref, wit_ref):
    pos = pos_ref[0]
    cb, s1b, s2b = _rope_tables(pos, fb_ref[...], B_ROT, B_HEAD_DIM)
    ci, s1i, s2i = _rope_tables(pos, fi_ref[...], IDX_ROT, IDX_DIM)
    for h in range(B_HEADS):
        x = att_ref[0, :, ATT_Q0 + h * LANES:ATT_Q0 + (h + 1) * LANES]
        q_ref[0, :, h * LANES:(h + 1) * LANES] = _rope(x, cb, s1b, s2b, B_ROT // 2).astype(_BF16)
    for h in range(B_KV_HEADS):
        x = att_ref[0, :, ATT_K0 + h * LANES:ATT_K0 + (h + 1) * LANES]
        k_ref[0, :, h * LANES:(h + 1) * LANES] = _rope(x, cb, s1b, s2b, B_ROT // 2).astype(_BF16)
    vt_ref[0, 0] = att_ref[0, :, ATT_V0:ATT_V0 + B_KV_WIDTH].T.astype(_BF16)
    lane = lax.broadcasted_iota(jnp.int32, ci.shape, 1)
    is_ki = lane < IDX_DIM
    heads_per_group = LANES // IDX_DIM
    for j in range(IDX_WIDTH // LANES):
        x = att_ref[0, :, ATT_QI0 + j * LANES:ATT_QI0 + (j + 1) * LANES]
        y = _rope(x, ci, s1i, s2i, IDX_ROT // 2)
        for r in range(heads_per_group):
            h = j * heads_per_group + r
            yr = y if r == 0 else pltpu.roll(y, LANES - r * IDX_DIM, 1)
            qi_ref[0, :, h * LANES:(h + 1) * LANES] = jnp.where(is_ki, yr, 0.0).astype(_BF16)
    x = att_ref[0, :, ATT_KW0:ATT_KW0 + LANES]
    y = _rope(x, jnp.where(is_ki, ci, 1.0), jnp.where(is_ki, s1i, 0.0),
              jnp.where(is_ki, s2i, 0.0), IDX_ROT // 2)
    ki_ref[0] = jnp.where(is_ki, y, 0.0).astype(_BF16)
    wit_ref[0] = y.T[IDX_DIM:IDX_DIM + IDX_HEADS, :]


def _prep_side_job(att, positions, n_steps):
    bsz, seq, _ = att.shape
    ts = bsz * seq // n_steps
    assert ts * n_steps == bsz * seq and seq % ts == 0 and KEY_CHUNK % ts == 0
    assert ts % (2 * SUBLANES) == 0
    per_seq = seq // ts
    per_chunk = KEY_CHUNK // ts

    def tokens(width):
        return pl.BlockSpec((1, ts, width), lambda s: (s // per_seq, s % per_seq, 0))

    def out(width, dtype):
        return tokens(width), jax.ShapeDtypeStruct((bsz, seq, width), dtype)

    vt_out = (pl.BlockSpec((1, 1, B_KV_WIDTH, ts),
                           lambda s: (s // per_seq, (s % per_seq) // per_chunk, 0,
                                      (s % per_seq) % per_chunk)),
              jax.ShapeDtypeStruct((bsz, seq // KEY_CHUNK, B_KV_WIDTH, KEY_CHUNK), _BF16))
    wit_out = (pl.BlockSpec((1, IDX_HEADS, ts), lambda s: (s // per_seq, 0, s % per_seq)),
               jax.ShapeDtypeStruct((bsz, IDX_HEADS, seq), _F32))
    outs = [out(B_WIDTH, _BF16), out(B_KV_WIDTH, _BF16), vt_out,
            out(IDX_HEADS * LANES, _BF16), out(LANES, _BF16), wit_out]
    freq_spec = pl.BlockSpec((1, LANES), lambda s: (0, 0))
    return SideJob(_prep_kernel,
                   (att, positions.astype(_F32).reshape(bsz, seq, 1),
                    _lane_inv_freq(B_ROT, B_HEAD_DIM), _lane_inv_freq(IDX_ROT, IDX_DIM)),
                   (tokens(ATT_WIDTH), tokens(1), freq_spec, freq_spec),
                   tuple(o[0] for o in outs), tuple(o[1] for o in outs))


def _ordered_int_to_float(key):
    bits = jnp.where(key < 0, key ^ jnp.int32(0x7FFFFFFF), key)
    return lax.bitcast_convert_type(bits, _F32)


def _fold_rows(x, op):
    while x.shape[0] > SUBLANES:
        half = x.shape[0] // 2
        x = op(x[:half], x[half:])
    return x


_DSA_N_IN = 6


def _dsa_kernel(*refs, n_sel, side_body, n_side_in, n_side_out):
    o_at = _DSA_N_IN + n_side_in
    if side_body is not None:
        side_body(*refs[_DSA_N_IN:o_at], *refs[o_at + 1:o_at + 1 + n_side_out])
    own = refs[:_DSA_N_IN] + (refs[o_at],) + refs[o_at + 1 + n_side_out:]
    seq = own[4].shape[1]
    n_chunks = (pl.program_id(1) * Q_BLOCK + Q_BLOCK + KEY_CHUNK - 1) // KEY_CHUNK
    for n in range(1, seq // KEY_CHUNK + 1):
        pl.when(n_chunks == n)(functools.partial(_dsa_block, *own, n_sel=n_sel, n_chunks=n))


def _dsa_block(qi_ref, wit_ref, ki_ref, q_ref, k_ref, vt_ref, o_ref,
               score_ref, bias_ref, logit_ref, acc_ref, *, n_sel, n_chunks):
    t = Q_BLOCK
    blk = pl.program_id(1)
    qpos = blk * t + lax.broadcasted_iota(jnp.int32, (1, t), 1)
    kiota = lax.broadcasted_iota(jnp.int32, (KEY_CHUNK, 1), 0)
    idx_scale = (IDX_DIM ** -0.5) * (IDX_HEADS ** -0.5)

    def rows(c):
        return slice(c * KEY_CHUNK, (c + 1) * KEY_CHUNK)

    def for_chunks(body, carry):
        for c in range(n_chunks):
            carry = body(c, carry)
        return carry

    def indexer_chunk(c, carry):
        ki = ki_ref[0, rows(c), :]
        acc = jnp.zeros((KEY_CHUNK, t), _F32)
        for h in range(0, IDX_HEADS, 2):
            qpair = jnp.concatenate([qi_ref[0, :, h * LANES:(h + 1) * LANES],
                                     qi_ref[0, :, (h + 1) * LANES:(h + 2) * LANES]], axis=0)
            d = lax.dot_general(ki, qpair, _NT_DIMS, preferred_element_type=_F32)
            acc = acc + wit_ref[0, h:h + 1, :] * jnp.maximum(d[:, :t], 0.0)
            acc = acc + wit_ref[0, h + 1:h + 2, :] * jnp.maximum(d[:, t:], 0.0)
        kpos = c * KEY_CHUNK + kiota
        score_ref[rows(c), :] = jnp.where(kpos <= qpos, acc * idx_scale + 0.0, _NEG_INF)
        return carry

    for_chunks(indexer_chunk, 0)

    def count(indicator):
        def chunk(c, part):
            return part + _fold_rows(indicator(score_ref[rows(c), :], c), jnp.add)
        part = for_chunks(chunk, jnp.zeros((SUBLANES, t), _F32))
        return jnp.sum(part, axis=0, keepdims=True)

    def count_ge(cand_f):
        return count(lambda s, c: jnp.where(s >= cand_f, 1.0, 0.0))

    def search():
        int_min = jnp.int32(-2 ** 31)
        base = jnp.where(count_ge(jnp.zeros((1, t), _F32)) >= n_sel, jnp.int32(0), int_min)

        def search_step(it, base):
            cand = base | lax.shift_left(jnp.int32(1), jnp.int32(30) - it)
            cnt = count_ge(_ordered_int_to_float(cand))
            return jnp.where(cnt >= n_sel, cand, base)

        return lax.fori_loop(0, 31, search_step, base)

    if (n_chunks - 1) * KEY_CHUNK >= n_sel:
        base = search()
    else:
        base = lax.cond((blk + 1) * t <= n_sel, lambda: jnp.zeros((1, t), jnp.int32), search)
    thr = jnp.where(qpos + 1 <= n_sel, _NEG_INF, _ordered_int_to_float(base))

    def bias_chunk(c, part):
        s = score_ref[rows(c), :]
        sel = jnp.where(c * KEY_CHUNK + kiota <= qpos, jnp.where(s >= thr, 1.0, 0.0), 0.0)
        bias_ref[rows(c), :] = jnp.where(sel > 0.0, 0.0, _NEG_INF)
        return part + _fold_rows(sel, jnp.add)

    n_ge = jnp.sum(for_chunks(bias_chunk, jnp.zeros((SUBLANES, t), _F32)), axis=0, keepdims=True)

    @pl.when(jnp.max(n_ge) > n_sel)
    def _():
        nxt = _ordered_int_to_float(base + 1)
        tied = n_ge > n_sel
        n_above = count(lambda s, c: jnp.where(s >= nxt, 1.0, 0.0))
        need = jnp.where(tied, n_sel - n_above, 0.0)
        no_index = float(k_ref.shape[1])
        front_v = jnp.where(tied, jnp.inf, _NEG_INF)
        front_i = jnp.where(tied, -1.0, no_index)

        def beyond(s, kposf, front_v, front_i):
            after = jnp.where(s < front_v, 1.0,
                              jnp.where(s == front_v, jnp.where(kposf > front_i, 1.0, 0.0), 0.0))
            return jnp.where(s >= thr, jnp.where(s < nxt, after, 0.0), 0.0)

        def kposf(c):
            return (c * KEY_CHUNK + kiota).astype(_F32)

        def advance(_, carry):
            front_v, front_i, need = carry

            def best_value(c, part):
                s = score_ref[rows(c), :]
                cand = jnp.where(beyond(s, kposf(c), front_v, front_i) > 0.0, s, _NEG_INF)
                return jnp.maximum(part, _fold_rows(cand, jnp.maximum))

            v = jnp.max(for_chunks(best_value, jnp.full((SUBLANES, t), _NEG_INF, _F32)),
                        axis=0, keepdims=True)

            def first_index(c, part):
                s = score_ref[rows(c), :]
                hit = jnp.where(s == v, beyond(s, kposf(c), front_v, front_i), 0.0)
                cand = jnp.where(hit > 0.0, kposf(c), no_index)
                return jnp.minimum(part, _fold_rows(cand, jnp.minimum))

            i = jnp.min(for_chunks(first_index, jnp.full((SUBLANES, t), no_index, _F32)),
                        axis=0, keepdims=True)
            active = need > 0.0
            return (jnp.where(active, v, front_v), jnp.where(active, i, front_i),
                    jnp.where(active, need - 1.0, need))

        front_v, front_i, _ = lax.fori_loop(0, jnp.max(need).astype(jnp.int32), advance,
                                            (front_v, front_i, need))

        def tie_chunk(c, carry):
            s = score_ref[rows(c), :]
            upto = jnp.where(s > front_v, 0.0,
                             jnp.where(s == front_v,
                                       jnp.where(kposf(c) <= front_i, 0.0, _NEG_INF), _NEG_INF))
            keep = jnp.where(s >= nxt, 0.0, jnp.where(s >= thr, upto, _NEG_INF))
            bias_ref[rows(c), :] = jnp.where(c * KEY_CHUNK + kiota <= qpos, keep, _NEG_INF)
            return carry

        for_chunks(tie_chunk, 0)

    grp = B_HEADS // B_KV_HEADS
    gw = grp * t
    exp2_scale = (B_HEAD_DIM ** -0.5) * 1.4426950408889634

    def logit_chunk(c, mx):
        bias = jnp.concatenate([bias_ref[rows(c), :]] * grp, axis=1)
        folded = []
        for g in range(B_KV_HEADS):
            qg = jnp.concatenate(
                [q_ref[0, :, (g * grp + hh) * B_HEAD_DIM:(g * grp + hh + 1) * B_HEAD_DIM]
                 for hh in range(grp)], axis=0)
            l = lax.dot_general(k_ref[0, rows(c), g * B_HEAD_DIM:(g + 1) * B_HEAD_DIM], qg,
                                _NT_DIMS, preferred_element_type=_F32) + bias
            logit_ref[rows(c), g * gw:(g + 1) * gw] = l
            folded.append(_fold_rows(l, jnp.maximum))
        return jnp.maximum(mx, jnp.concatenate(folded, axis=1))

    mx = for_chunks(logit_chunk, jnp.full((SUBLANES, B_HEADS * t), _NEG_INF, _F32))
    mx = jnp.max(mx, axis=0, keepdims=True)

    acc_ref[...] = jnp.zeros_like(acc_ref)
    ones_rows = jnp.ones((acc_ref.shape[0] - B_HEAD_DIM, KEY_CHUNK), _BF16)

    def pv_chunk(c, carry):
        for g in range(B_KV_HEADS):
            cols = slice(g * gw, (g + 1) * gw)
            p = jnp.exp2((logit_ref[rows(c), cols] - mx[:, cols]) * exp2_scale)
            vt_ext = jnp.concatenate(
                [vt_ref[0, c, g * B_HEAD_DIM:(g + 1) * B_HEAD_DIM, :], ones_rows], axis=0)
            acc_ref[:, cols] += jnp.dot(vt_ext, p.astype(_BF16), preferred_element_type=_F32)
        return carry

    for_chunks(pv_chunk, 0)
    out_t = acc_ref[:B_HEAD_DIM, :] / acc_ref[B_HEAD_DIM:B_HEAD_DIM + 1, :]
    for h in range(B_HEADS):
        o_ref[0, :, h * B_HEAD_DIM:(h + 1) * B_HEAD_DIM] = (
            out_t[:, h * t:(h + 1) * t].T.astype(o_ref.dtype))


def _dsa_attention(q, k, vt, qi, ki, wit, side_job=None):
    bsz, seq, _ = q.shape
    assert Q_BLOCK == LANES and seq % KEY_CHUNK == 0
    n_sel = min(TOPK_MAX, seq // 4)
    n_blk = seq // Q_BLOCK

    def q_spec(width):
        return pl.BlockSpec((1, Q_BLOCK, width), lambda b, i: (b, i, 0))

    def kv_spec(width):
        return pl.BlockSpec((1, seq, width), lambda b, i: (b, 0, 0))

    def per_step(spec):
        return pl.BlockSpec(spec.block_shape, lambda b, i: spec.index_map(b * n_blk + i))

    side = None if side_job is None else side_job(bsz * n_blk)
    side_in = [] if side is None else [per_step(s) for s in side.in_specs]
    side_out = [] if side is None else [per_step(s) for s in side.out_specs]
    outs = pl.pallas_call(
        functools.partial(_dsa_kernel, n_sel=n_sel,
                          side_body=None if side is None else side.body,
                          n_side_in=len(side_in), n_side_out=len(side_out)),
        grid=(bsz, n_blk),
        in_specs=[q_spec(IDX_HEADS * LANES),
                  pl.BlockSpec((1, IDX_HEADS, Q_BLOCK), lambda b, i: (b, 0, i)),
                  kv_spec(LANES), q_spec(B_WIDTH), kv_spec(B_KV_WIDTH),
                  pl.BlockSpec((1, seq // KEY_CHUNK, B_KV_WIDTH, KEY_CHUNK),
                               lambda b, i: (b, 0, 0, 0))] + side_in,
        out_specs=[q_spec(B_WIDTH)] + side_out,
        out_shape=[jax.ShapeDtypeStruct((bsz, seq, B_WIDTH), _BF16)]
        + ([] if side is None else list(side.out_shape)),
        scratch_shapes=[pltpu.VMEM((seq, Q_BLOCK), _F32), pltpu.VMEM((seq, Q_BLOCK), _F32),
                        pltpu.VMEM((seq, B_HEADS * Q_BLOCK), _F32),
                        pltpu.VMEM((B_HEAD_DIM + 2 * SUBLANES, B_HEADS * Q_BLOCK), _F32)],
        compiler_params=_compiler_params(2),
        name="dsa_attention",
    )(qi, wit, ki, q, k, vt, *(() if side is None else side.arrays))
    return outs[0] if side is None else tuple(outs)


def _xattn_kernel(q_ref, k_ref, v_ref, o_ref):
    scale = X_HEAD_DIM ** -0.5
    for h in range(X_HEADS):
        hs = slice(h * X_HEAD_DIM, (h + 1) * X_HEAD_DIM)
        logits = lax.dot_general(q_ref[0, :, hs], k_ref[0, :, hs], _NT_DIMS,
                                 preferred_element_type=_F32) * scale
        mx = jnp.max(logits, axis=-1, keepdims=True)
        p = jnp.exp(logits - mx)
        denom = jnp.sum(p, axis=-1, keepdims=True)
        o = jnp.dot(p.astype(_BF16), v_ref[0, :, hs], preferred_element_type=_F32)
        o_ref[0, :, hs] = (o / denom).astype(o_ref.dtype)


def _cross_attention(q, k, v):
    bsz, seq, _ = q.shape
    tq = ROW_BLOCK["cross_attention"]
    m = k.shape[1]
    mem_spec = pl.BlockSpec((1, m, X_WIDTH), lambda b, i: (b, 0, 0))
    return pl.pallas_call(
        _xattn_kernel,
        grid=(bsz, seq // tq),
        in_specs=[pl.BlockSpec((1, tq, X_WIDTH), lambda b, i: (b, i, 0)), mem_spec, mem_spec],
        out_specs=pl.BlockSpec((1, tq, X_WIDTH), lambda b, i: (b, i, 0)),
        out_shape=jax.ShapeDtypeStruct((bsz, seq, X_WIDTH), _BF16),
        compiler_params=_compiler_params(2),
        name="cross_attention",
    )(q, k, v)


def _norm_proj_kernel(x_ref, g_ref, w_ref, o_ref, w_bf16_ref):
    @pl.when(pl.program_id(0) == 0)
    def _():
        w_bf16_ref[...] = w_ref[...].astype(_BF16)

    x = x_ref[...]
    ms = jnp.mean(x * x, axis=-1, keepdims=True)
    h = ((x * lax.rsqrt(ms + EPS)) * g_ref[...]).astype(_BF16)
    o_ref[...] = jnp.dot(h, w_bf16_ref[...], preferred_element_type=_F32).astype(o_ref.dtype)


def _norm_proj(x, g, w_stack, layer, out_dtype, name):
    m, d = x.shape
    tm = ROW_BLOCK["norm_proj"]
    n = w_stack.shape[2]
    return pl.pallas_call(
        _norm_proj_kernel,
        grid=(m // tm,),
        in_specs=[pl.BlockSpec((tm, d), lambda i: (i, 0)),
                  pl.BlockSpec((1, d), lambda i: (0, 0)),
                  pl.BlockSpec((None, d, n), lambda i: (layer, 0, 0),
                               pipeline_mode=pl.Buffered(1))],
        out_specs=pl.BlockSpec((tm, n), lambda i: (i, 0)),
        out_shape=jax.ShapeDtypeStruct((m, n), out_dtype),
        scratch_shapes=[pltpu.VMEM((d, n), _BF16)],
        compiler_params=_compiler_params(1),
        name=name,
    )(x, g.reshape(1, d), w_stack)


def _xattn_out_kernel(o_ref, w_ref, x_ref, g_ref, x_out_ref, h_out_ref):
    x = x_ref[...] + jnp.dot(o_ref[...], w_ref[...], preferred_element_type=_F32)
    x_out_ref[...] = x
    ms = jnp.mean(x * x, axis=-1, keepdims=True)
    h_out_ref[...] = ((x * lax.rsqrt(ms + EPS)) * g_ref[...]).astype(h_out_ref.dtype)


def _xattn_out_and_norm(o, w_bf16, x, g):
    m, k = o.shape
    tm = ROW_BLOCK["xattn_out_norm"]
    d = x.shape[1]
    row_spec = pl.BlockSpec((tm, d), lambda i: (i, 0))
    return pl.pallas_call(
        _xattn_out_kernel,
        grid=(m // tm,),
        in_specs=[pl.BlockSpec((tm, k), lambda i: (i, 0)),
                  pl.BlockSpec((k, d), lambda i: (0, 0), pipeline_mode=pl.Buffered(1)),
                  row_spec,
                  pl.BlockSpec((1, d), lambda i: (0, 0))],
        out_specs=[row_spec, row_spec],
        out_shape=[jax.ShapeDtypeStruct((m, d), _F32), jax.ShapeDtypeStruct((m, d), _BF16)],
        compiler_params=_compiler_params(1),
        name="xattn_out_norm",
    )(o, w_bf16, x, g.reshape(1, d))


def kernel(x, mem, positions, norm_mix_g, w_in, a_norm_g, a_spatial_w, a_spatial_b, p_a, p_b,
           w_out, norm_x_g, norm_mem_g, xq_w, xk_w, xv_w, xo_w, norm_ffn_g, ffn_w1, ffn_w3,
           ffn_w2, final_norm_g):
    bsz, seq, d = x.shape
    m = bsz * seq
    depth = w_in.shape[0]
    ffn_hidden = ffn_w1.shape[-1]

    xf = x.reshape(m, d)
    mem_f = mem.reshape(bsz * MEM_LEN, d)
    w_in_t = jnp.transpose(w_in, (0, 2, 1))
    for l in range(depth):
        def w_in_cols(col0):
            return Rhs(w_in_t, l, col0, transposed=True)

        h = _rmsnorm(xf, norm_mix_g[l], _BF16)
        att = _fused_matmul("proj_att", [h], [(0, w_in_cols(ZA_END))], [], _ep_identity,
                            ATT_WIDTH, _F32)
        za, q, k, vt, qi, ki, wit = _fused_matmul(
            "proj_za", [h], [(0, w_in_cols(0))], [], _ep_identity, ZA_END, _F32,
            side_job=functools.partial(_prep_side_job, att.reshape(bsz, seq, ATT_WIDTH),
                                       positions))
        y_a = _spatial_gating(za, a_norm_g[l], a_spatial_w[l], a_spatial_b[l])
        def merge_weight_casts(n_steps):
            return _merge_side_jobs(
                _cast_side_job(p_a, l, n_steps), _cast_side_job(p_b, l, n_steps),
                _cast_side_job(w_in_t, l, n_steps, row0=ATT_END, rows=2 * d))

        y_b, pa_bf16, pb_bf16, gate_wt_bf16 = _dsa_attention(q, k, vt, qi, ki, wit,
                                                              side_job=merge_weight_casts)
        merged = _fused_matmul(
            "gated_merge", [y_a, y_b.reshape(m, B_WIDTH), h],
            [(0, Rhs(pa_bf16)), (1, Rhs(pb_bf16)), (2, Rhs(gate_wt_bf16, transposed=True)),
             (2, Rhs(gate_wt_bf16, col0=d, transposed=True))],
            [], _ep_gated_merge, d, _BF16)
        x1, xo_bf16 = _fused_matmul("out_proj", [merged], [(0, Rhs(w_out, l))], [xf],
                                    _ep_residual, d, _F32,
                                    side_job=functools.partial(_cast_side_job, xo_w, l))

        qx = _norm_proj(x1, norm_x_g[l], xq_w, l, _BF16, "xattn_q")
        kx = _norm_proj(mem_f, norm_mem_g[l], xk_w, l, _BF16, "xattn_k")
        vx = _norm_proj(mem_f, norm_mem_g[l], xv_w, l, _BF16, "xattn_v")
        ox = _cross_attention(qx.reshape(bsz, seq, X_WIDTH),
                              kx.reshape(bsz, MEM_LEN, X_WIDTH),
                              vx.reshape(bsz, MEM_LEN, X_WIDTH)).reshape(m, X_WIDTH)
        x2, h2 = _xattn_out_and_norm(ox, xo_bf16, x1, norm_ffn_g[l])

        act, w2_bf16 = _fused_matmul("ffn_up", [h2],
                                     [(0, Rhs(ffn_w1, l)), (0, Rhs(ffn_w3, l))],
                                     [], _ep_swiglu, ffn_hidden, _BF16,
                                     side_job=functools.partial(_cast_side_job, ffn_w2, l))
        xf = _fused_matmul("ffn_down", [act], [(0, Rhs(w2_bf16))], [x2], _ep_residual, d, _F32)

    return _rmsnorm(xf, final_norm_g, _F32).reshape(bsz, seq, d)
```

```python
import functools
from typing import Callable, NamedTuple, Optional

import jax
import jax.numpy as jnp
from jax import lax
from jax.experimental import pallas as pl
from jax.experimental.pallas import tpu as pltpu

D_MODEL = 4096
MEM_LEN = 256
EPS = 1e-6
ROPE_THETA = 500000.0
CHUNK = 128
A_GROUPS = 16
A_WIDTH = D_MODEL // 2
A_GROUP_DIM = A_WIDTH // A_GROUPS
B_HEADS = 16
B_HEAD_DIM = 128
B_KV_HEADS = 4
B_WIDTH = B_HEADS * B_HEAD_DIM
B_KV_WIDTH = B_KV_HEADS * B_HEAD_DIM
B_ROT = B_HEAD_DIM // 4
IDX_HEADS = 16
IDX_DIM = 64
IDX_WIDTH = IDX_HEADS * IDX_DIM
IDX_ROT = IDX_DIM // 4
TOPK_MAX = 256
Q_BLOCK = 128
X_HEADS = 4
X_HEAD_DIM = 256
X_WIDTH = X_HEADS * X_HEAD_DIM
IN_SIZES = (2 * A_WIDTH, B_WIDTH, B_KV_WIDTH, B_KV_WIDTH, IDX_WIDTH, IDX_DIM, IDX_HEADS,
            2 * D_MODEL)

LANES = 128
SUBLANES = 8
KEY_CHUNK = 512
PROJ_TN = 512
ZA_END = IN_SIZES[0]
ATT_END = ZA_END + sum(IN_SIZES[1:7])
ATT_Q0 = 0
ATT_K0 = ATT_Q0 + B_WIDTH
ATT_V0 = ATT_K0 + B_KV_WIDTH
ATT_QI0 = ATT_V0 + B_KV_WIDTH
ATT_KW0 = ATT_QI0 + IDX_WIDTH
ATT_WIDTH = -(-(ATT_END - ZA_END) // PROJ_TN) * PROJ_TN
assert ZA_END % PROJ_TN == 0 and ATT_KW0 + LANES <= ATT_WIDTH

VMEM_LIMIT_BYTES = 60 * 1024 * 1024


class Tile(NamedTuple):
    tm: int
    tn: int
    prefetch_lhs: bool = False
    row_splits: int = 1


TILES = {
    "proj_att": Tile(1024, PROJ_TN, prefetch_lhs=True),
    "proj_za": Tile(1024, 512, prefetch_lhs=True),
    "gated_merge": Tile(1024, 256, prefetch_lhs=True, row_splits=4),
    "out_proj": Tile(1024, 512, prefetch_lhs=True),
    "ffn_up": Tile(2048, 256, prefetch_lhs=True, row_splits=4),
    "ffn_down": Tile(512, 512, prefetch_lhs=True),
}
ROW_BLOCK = {
    "rmsnorm": 512,
    "spatial_gating": 4 * CHUNK,
    "cross_attention": 1024,
    "norm_proj": 512,
    "xattn_out_norm": 512,
}

_BF16 = jnp.bfloat16
_F32 = jnp.float32
_NEG_INF = float("-inf")
_NN_DIMS = (((1,), (0,)), ((), ()))
_NT_DIMS = (((1,), (1,)), ((), ()))


def _compiler_params(n_axes):
    return pltpu.CompilerParams(
        dimension_semantics=("arbitrary",) * n_axes,
        vmem_limit_bytes=VMEM_LIMIT_BYTES,
    )


def _rmsnorm_kernel(x_ref, g_ref, o_ref):
    x = x_ref[...]
    ms = jnp.mean(x * x, axis=-1, keepdims=True)
    y = x * lax.rsqrt(ms + EPS)
    o_ref[...] = (y * g_ref[...]).astype(o_ref.dtype)


def _rmsnorm(x, g, out_dtype):
    m, d = x.shape
    tm = ROW_BLOCK["rmsnorm"]
    return pl.pallas_call(
        _rmsnorm_kernel,
        grid=(m // tm,),
        in_specs=[pl.BlockSpec((tm, d), lambda i: (i, 0)),
                  pl.BlockSpec((1, d), lambda i: (0, 0))],
        out_specs=pl.BlockSpec((tm, d), lambda i: (i, 0)),
        out_shape=jax.ShapeDtypeStruct((m, d), out_dtype),
        compiler_params=_compiler_params(1),
        name="rmsnorm",
    )(x, g.reshape(1, d))


def _ep_identity(accs, extras):
    return accs[0]


def _ep_residual(accs, extras):
    return extras[0] + accs[0]


def _ep_gated_merge(accs, extras):
    y_a, y_b, gate_a, gate_b = accs
    return jax.nn.sigmoid(gate_a) * y_a + jax.nn.sigmoid(gate_b) * y_b


def _ep_swiglu(accs, extras):
    return jax.nn.silu(accs[0]) * accs[1]


class Rhs(NamedTuple):
    array: jax.Array
    layer: Optional[int] = None
    col0: int = 0
    transposed: bool = False


def _rhs_spec(r, tn):
    if r.transposed and r.layer is None:
        assert r.col0 % tn == 0
        off = r.col0 // tn
        return pl.BlockSpec((tn, r.array.shape[1]), lambda i, j: (j + off, 0))
    if r.transposed:
        k = r.array.shape[2]
        if r.col0 % tn == 0:
            return pl.BlockSpec((None, tn, k), lambda i, j: (r.layer, j + r.col0 // tn, 0))
        assert r.col0 % SUBLANES == 0
        return pl.BlockSpec((pl.Element(1), pl.Element(tn), pl.Element(k)),
                            lambda i, j: (r.layer, (r.col0 // SUBLANES + j * (tn // SUBLANES))
                                          * SUBLANES, 0))
    assert r.col0 % tn == 0
    off = r.col0 // tn
    if r.layer is None:
        return pl.BlockSpec((r.array.shape[0], tn), lambda i, j: (0, j + off))
    return pl.BlockSpec((None, r.array.shape[1], tn), lambda i, j: (r.layer, 0, j + off))


class SideJob(NamedTuple):
    body: Callable
    arrays: tuple
    in_specs: tuple
    out_specs: tuple
    out_shape: tuple


def _cast_body(x_ref, o_ref):
    x = x_ref[0] if len(x_ref.shape) == 3 else x_ref[...]
    o_ref[...] = x.astype(o_ref.dtype)


def _cast_side_job(w_stack, layer, n_steps, row0=0, rows=None):
    rows = w_stack.shape[1] if rows is None else rows
    cols = w_stack.shape[2]
    slab = rows // n_steps
    assert slab * n_steps == rows and slab % (2 * SUBLANES) == 0
    if row0 % slab == 0:
        in_spec = pl.BlockSpec((None, slab, cols), lambda s: (layer, s + row0 // slab, 0))
    else:
        assert row0 % SUBLANES == 0
        in_spec = pl.BlockSpec(
            (pl.Element(1), pl.Element(slab), pl.Element(cols)),
            lambda s: (layer, (row0 // SUBLANES + s * (slab // SUBLANES)) * SUBLANES, 0))
    return SideJob(_cast_body, (w_stack,), (in_spec,),
                   (pl.BlockSpec((slab, cols), lambda s: (s, 0)),),
                   (jax.ShapeDtypeStruct((rows, cols), _BF16),))


def _merge_side_jobs(*jobs):
    n_in = [len(j.in_specs) for j in jobs]
    n_out = [len(j.out_specs) for j in jobs]

    def body(*refs):
        ins, outs = refs[:sum(n_in)], refs[sum(n_in):]
        i = o = 0
        for j, ni, no in zip(jobs, n_in, n_out):
            j.body(*ins[i:i + ni], *outs[o:o + no])
            i, o = i + ni, o + no

    return SideJob(body, sum((j.arrays for j in jobs), ()), sum((j.in_specs for j in jobs), ()),
                   sum((j.out_specs for j in jobs), ()), sum((j.out_shape for j in jobs), ()))


def _mm_kernel(*refs, n_lhs, pair_lhs, rhs_transposed, n_extra, epilogue, side_body, n_side_in,
               row_splits):
    n_rhs = len(pair_lhs)
    lhs_refs = refs[:n_lhs]
    rhs_refs = refs[n_lhs:n_lhs + n_rhs]
    n_in = n_lhs + n_rhs + n_extra
    extra_refs = refs[n_lhs + n_rhs:n_in]
    o_ref = refs[n_in + n_side_in]
    if side_body is not None:
        side_body(*refs[n_in:n_in + n_side_in], *refs[n_in + n_side_in + 1:])
    weights = [(r[0] if len(r.shape) == 3 else r[...]).astype(_BF16) for r in rhs_refs]
    rows_per_split = o_ref.shape[0] // row_splits
    for s in range(row_splits):
        rows = slice(s * rows_per_split, (s + 1) * rows_per_split)
        accs = [lax.dot_general(lhs_refs[li][rows, :], w, _NT_DIMS if tr else _NN_DIMS,
                                preferred_element_type=_F32)
                for li, w, tr in zip(pair_lhs, weights, rhs_transposed)]
        o_ref[rows, :] = epilogue(accs, [e[rows, :] for e in extra_refs]).astype(o_ref.dtype)


def _fused_matmul(name, lhs, pairs, extras, epilogue, n_out, out_dtype, side_job=None):
    tm, tn, prefetch_lhs, row_splits = TILES[name]
    m = lhs[0].shape[0]
    assert m % tm == 0 and n_out % tn == 0
    grid = (m // tm, n_out // tn)
    lhs_mode = {} if prefetch_lhs else {"pipeline_mode": pl.Buffered(1)}
    in_specs = [pl.BlockSpec((tm, a.shape[1]), lambda i, j: (i, 0), **lhs_mode) for a in lhs]
    in_specs += [_rhs_spec(r, tn) for _, r in pairs]
    in_specs += [pl.BlockSpec((tm, tn), lambda i, j: (i, j)) for _ in extras]
    out_specs = [pl.BlockSpec((tm, tn), lambda i, j: (i, j))]
    out_shape = [jax.ShapeDtypeStruct((m, n_out), out_dtype)]
    operands = list(lhs) + [p[1].array for p in pairs] + list(extras)
    side = None if side_job is None else side_job(grid[0] * grid[1])
    if side is not None:
        def per_step(spec):
            return pl.BlockSpec(spec.block_shape,
                                lambda i, j: spec.index_map(i * grid[1] + j))
        in_specs += [per_step(s) for s in side.in_specs]
        out_specs += [per_step(s) for s in side.out_specs]
        out_shape += list(side.out_shape)
        operands += list(side.arrays)
    kern = functools.partial(_mm_kernel, n_lhs=len(lhs), pair_lhs=tuple(p[0] for p in pairs),
                             rhs_transposed=tuple(p[1].transposed for p in pairs),
                             n_extra=len(extras), epilogue=epilogue,
                             side_body=None if side is None else side.body,
                             n_side_in=0 if side is None else len(side.in_specs),
                             row_splits=row_splits)
    outs = pl.pallas_call(
        kern,
        grid=grid,
        in_specs=in_specs,
        out_specs=out_specs,
        out_shape=out_shape,
        compiler_params=_compiler_params(2),
        name=name,
    )(*operands)
    return outs[0] if side is None else tuple(outs)


def _gating_kernel(za_ref, ng_ref, w_ref, bt_ref, o_ref):
    row = lax.broadcasted_iota(jnp.int32, (CHUNK, CHUNK), 0)
    col = lax.broadcasted_iota(jnp.int32, (CHUNK, CHUNK), 1)
    causal = col <= row
    for c in range(za_ref.shape[0] // CHUNK):
        rows = slice(c * CHUNK, (c + 1) * CHUNK)
        z = jax.nn.gelu(za_ref[rows, :])
        u = z[:, :A_WIDTH]
        v = z[:, A_WIDTH:]
        ms = jnp.mean(v * v, axis=-1, keepdims=True)
        vn = ((v * lax.rsqrt(ms + EPS)) * ng_ref[...]).astype(_BF16)
        for g in range(A_GROUPS):
            sl = slice(g * A_GROUP_DIM, (g + 1) * A_GROUP_DIM)
            w = jnp.where(causal, w_ref[g], 0.0).astype(_BF16)
            s = jnp.dot(w, vn[:, sl], preferred_element_type=_F32) + bt_ref[:, g:g + 1]
            o_ref[rows, sl] = (u[:, sl] * s).astype(o_ref.dtype)


def _spatial_gating(za, norm_g, w_s, b_s):
    m = za.shape[0]
    tm = ROW_BLOCK["spatial_gating"]
    return pl.pallas_call(
        _gating_kernel,
        grid=(m // tm,),
        in_specs=[pl.BlockSpec((tm, 2 * A_WIDTH), lambda i: (i, 0)),
                  pl.BlockSpec((1, A_WIDTH), lambda i: (0, 0)),
                  pl.BlockSpec((A_GROUPS, CHUNK, CHUNK), lambda i: (0, 0, 0)),
                  pl.BlockSpec((CHUNK, A_GROUPS), lambda i: (0, 0))],
        out_specs=pl.BlockSpec((tm, A_WIDTH), lambda i: (i, 0)),
        out_shape=jax.ShapeDtypeStruct((m, A_WIDTH), _BF16),
        compiler_params=_compiler_params(1),
        name="spatial_gating",
    )(za, norm_g.reshape(1, A_WIDTH), w_s, b_s.T)


def _lane_inv_freq(rot_dim, period):
    inv_freq = ROPE_THETA ** (-jnp.arange(0, rot_dim, 2, dtype=_F32) / rot_dim)
    head = jnp.concatenate([inv_freq, inv_freq, jnp.zeros((period - rot_dim,), _F32)])
    return jnp.tile(head, LANES // period).reshape(1, LANES)


def _rope_tables(pos, inv_freq_lanes, rot_dim, period):
    half = rot_dim // 2
    ang = pos * inv_freq_lanes
    cos, sin = jnp.cos(ang), jnp.sin(ang)
    lane = lax.broadcasted_iota(jnp.int32, ang.shape, 1) & (period - 1)
    c = jnp.where(lane < rot_dim, cos, 1.0)
    s1 = jnp.where(lane >= half, jnp.where(lane < rot_dim, sin, 0.0), 0.0)
    s2 = jnp.where(lane < half, -sin, 0.0)
    return c, s1, s2


def _rope(x, c, s1, s2, half):
    return x * c + pltpu.roll(x, half, 1) * s1 + pltpu.roll(x, LANES - half, 1) * s2


def _prep_kernel(att_ref, pos_ref, fb_ref, fi_ref,
                 q_ref, k_ref, vt_ref, qi_ref, ki_ref, wit_ref):
    pos = pos_ref[0]
    cb, s1b, s2b = _rope_tables(pos, fb_ref[...], B_ROT, B_HEAD_DIM)
    ci, s1i, s2i = _rope_tables(pos, fi_ref[...], IDX_ROT, IDX_DIM)
    for h in range(B_HEADS):
        x = att_ref[0, :, ATT_Q0 + h * LANES:ATT_Q0 + (h + 1) * LANES]
        q_ref[0, :, h * LANES:(h + 1) * LANES] = _rope(x, cb, s1b, s2b, B_ROT // 2).astype(_BF16)
    for h in range(B_KV_HEADS):
        x = att_ref[0, :, ATT_K0 + h * LANES:ATT_K0 + (h + 1) * LANES]
        k_ref[0, :, h * LANES:(h + 1) * LANES] = _rope(x, cb, s1b, s2b, B_ROT // 2).astype(_BF16)
    vt_ref[0, 0] = att_ref[0, :, ATT_V0:ATT_V0 + B_KV_WIDTH].T.astype(_BF16)
    lane = lax.broadcasted_iota(jnp.int32, ci.shape, 1)
    is_ki = lane < IDX_DIM
    heads_per_group = LANES // IDX_DIM
    for j in range(IDX_WIDTH // LANES):
        x = att_ref[0, :, ATT_QI0 + j * LANES:ATT_QI0 + (j + 1) * LANES]
        y = _rope(x, ci, s1i, s2i, IDX_ROT // 2)
        for r in range(heads_per_group):
            h = j * heads_per_group + r
            yr = y if r == 0 else pltpu.roll(y, LANES - r * IDX_DIM, 1)
            qi_ref[0, :, h * LANES:(h + 1) * LANES] = jnp.where(is_ki, yr, 0.0).astype(_BF16)
    x = att_ref[0, :, ATT_KW0:ATT_KW0 + LANES]
    y = _rope(x, jnp.where(is_ki, ci, 1.0), jnp.where(is_ki, s1i, 0.0),
              jnp.where(is_ki, s2i, 0.0), IDX_ROT // 2)
    ki_ref[0] = jnp.where(is_ki, y, 0.0).astype(_BF16)
    wit_ref[0] = y.T[IDX_DIM:IDX_DIM + IDX_HEADS, :]


def _prep_side_job(att, positions, n_steps):
    bsz, seq, _ = att.shape
    ts = bsz * seq // n_steps
    assert ts * n_steps == bsz * seq and seq % ts == 0 and KEY_CHUNK % ts == 0
    assert ts % (2 * SUBLANES) == 0
    per_seq = seq // ts
    per_chunk = KEY_CHUNK // ts

    def tokens(width):
        return pl.BlockSpec((1, ts, width), lambda s: (s // per_seq, s % per_seq, 0))

    def out(width, dtype):
        return tokens(width), jax.ShapeDtypeStruct((bsz, seq, width), dtype)

    vt_out = (pl.BlockSpec((1, 1, B_KV_WIDTH, ts),
                           lambda s: (s // per_seq, (s % per_seq) // per_chunk, 0,
                                      (s % per_seq) % per_chunk)),
              jax.ShapeDtypeStruct((bsz, seq // KEY_CHUNK, B_KV_WIDTH, KEY_CHUNK), _BF16))
    wit_out = (pl.BlockSpec((1, IDX_HEADS, ts), lambda s: (s // per_seq, 0, s % per_seq)),
               jax.ShapeDtypeStruct((bsz, IDX_HEADS, seq), _F32))
    outs = [out(B_WIDTH, _BF16), out(B_KV_WIDTH, _BF16), vt_out,
            out(IDX_HEADS * LANES, _BF16), out(LANES, _BF16), wit_out]
    freq_spec = pl.BlockSpec((1, LANES), lambda s: (0, 0))
    return SideJob(_prep_kernel,
                   (att, positions.astype(_F32).reshape(bsz, seq, 1),
                    _lane_inv_freq(B_ROT, B_HEAD_DIM), _lane_inv_freq(IDX_ROT, IDX_DIM)),
                   (tokens(ATT_WIDTH), tokens(1), freq_spec, freq_spec),
                   tuple(o[0] for o in outs), tuple(o[1] for o in outs))


def _ordered_int_to_float(key):
    bits = jnp.where(key < 0, key ^ jnp.int32(0x7FFFFFFF), key)
    return lax.bitcast_convert_type(bits, _F32)


def _fold_rows(x, op):
    while x.shape[0] > SUBLANES:
        half = x.shape[0] // 2
        x = op(x[:half], x[half:])
    return x


_DSA_N_IN = 6


def _dsa_kernel(*refs, n_sel, side_body, n_side_in, n_side_out):
    o_at = _DSA_N_IN + n_side_in
    if side_body is not None:
        side_body(*refs[_DSA_N_IN:o_at], *refs[o_at + 1:o_at + 1 + n_side_out])
    own = refs[:_DSA_N_IN] + (refs[o_at],) + refs[o_at + 1 + n_side_out:]
    seq = own[4].shape[1]
    n_chunks = (pl.program_id(1) * Q_BLOCK + Q_BLOCK + KEY_CHUNK - 1) // KEY_CHUNK
    for n in range(1, seq // KEY_CHUNK + 1):
        pl.when(n_chunks == n)(functools.partial(_dsa_block, *own, n_sel=n_sel, n_chunks=n))


def _dsa_block(qi_ref, wit_ref, ki_ref, q_ref, k_ref, vt_ref, o_ref,
               score_ref, bias_ref, logit_ref, acc_ref, *, n_sel, n_chunks):
    t = Q_BLOCK
    blk = pl.program_id(1)
    qpos = blk * t + lax.broadcasted_iota(jnp.int32, (1, t), 1)
    kiota = lax.broadcasted_iota(jnp.int32, (KEY_CHUNK, 1), 0)
    idx_scale = (IDX_DIM ** -0.5) * (IDX_HEADS ** -0.5)

    def rows(c):
        return slice(c * KEY_CHUNK, (c + 1) * KEY_CHUNK)

    def for_chunks(body, carry):
        for c in range(n_chunks):
            carry = body(c, carry)
        return carry

    def indexer_chunk(c, carry):
        ki = ki_ref[0, rows(c), :]
        acc = jnp.zeros((KEY_CHUNK, t), _F32)
        for h in range(0, IDX_HEADS, 2):
            qpair = jnp.concatenate([qi_ref[0, :, h * LANES:(h + 1) * LANES],
                                     qi_ref[0, :, (h + 1) * LANES:(h + 2) * LANES]], axis=0)
            d = lax.dot_general(ki, qpair, _NT_DIMS, preferred_element_type=_F32)
            acc = acc + wit_ref[0, h:h + 1, :] * jnp.maximum(d[:, :t], 0.0)
            acc = acc + wit_ref[0, h + 1:h + 2, :] * jnp.maximum(d[:, t:], 0.0)
        kpos = c * KEY_CHUNK + kiota
        score_ref[rows(c), :] = jnp.where(kpos <= qpos, acc * idx_scale + 0.0, _NEG_INF)
        return carry

    for_chunks(indexer_chunk, 0)

    def count(indicator):
        def chunk(c, part):
            return part + _fold_rows(indicator(score_ref[rows(c), :], c), jnp.add)
        part = for_chunks(chunk, jnp.zeros((SUBLANES, t), _F32))
        return jnp.sum(part, axis=0, keepdims=True)

    def count_ge(cand_f):
        return count(lambda s, c: jnp.where(s >= cand_f, 1.0, 0.0))

    def search():
        int_min = jnp.int32(-2 ** 31)
        base = jnp.where(count_ge(jnp.zeros((1, t), _F32)) >= n_sel, jnp.int32(0), int_min)

        def search_step(it, base):
            cand = base | lax.shift_left(jnp.int32(1), jnp.int32(30) - it)
            cnt = count_ge(_ordered_int_to_float(cand))
            return jnp.where(cnt >= n_sel, cand, base)

        return lax.fori_loop(0, 31, search_step, base)

    if (n_chunks - 1) * KEY_CHUNK >= n_sel:
        base = search()
    else:
        base = lax.cond((blk + 1) * t <= n_sel, lambda: jnp.zeros((1, t), jnp.int32), search)
    thr = jnp.where(qpos + 1 <= n_sel, _NEG_INF, _ordered_int_to_float(base))

    def bias_chunk(c, part):
        s = score_ref[rows(c), :]
        sel = jnp.where(c * KEY_CHUNK + kiota <= qpos, jnp.where(s >= thr, 1.0, 0.0), 0.0)
        bias_ref[rows(c), :] = jnp.where(sel > 0.0, 0.0, _NEG_INF)
        return part + _fold_rows(sel, jnp.add)

    n_ge = jnp.sum(for_chunks(bias_chunk, jnp.zeros((SUBLANES, t), _F32)), axis=0, keepdims=True)

    @pl.when(jnp.max(n_ge) > n_sel)
    def _():
        nxt = _ordered_int_to_float(base + 1)
        tied = n_ge > n_sel
        n_above = count(lambda s, c: jnp.where(s >= nxt, 1.0, 0.0))
        need = jnp.where(tied, n_sel - n_above, 0.0)
        no_index = float(k_ref.shape[1])
        front_v = jnp.where(tied, jnp.inf, _NEG_INF)
        front_i = jnp.where(tied, -1.0, no_index)

        def beyond(s, kposf, front_v, front_i):
            after = jnp.where(s < front_v, 1.0,
                              jnp.where(s == front_v, jnp.where(kposf > front_i, 1.0, 0.0), 0.0))
            return jnp.where(s >= thr, jnp.where(s < nxt, after, 0.0), 0.0)

        def kposf(c):
            return (c * KEY_CHUNK + kiota).astype(_F32)

        def advance(_, carry):
            front_v, front_i, need = carry

            def best_value(c, part):
                s = score_ref[rows(c), :]
                cand = jnp.where(beyond(s, kposf(c), front_v, front_i) > 0.0, s, _NEG_INF)
                return jnp.maximum(part, _fold_rows(cand, jnp.maximum))

            v = jnp.max(for_chunks(best_value, jnp.full((SUBLANES, t), _NEG_INF, _F32)),
                        axis=0, keepdims=True)

            def first_index(c, part):
                s = score_ref[rows(c), :]
                hit = jnp.where(s == v, beyond(s, kposf(c), front_v, front_i), 0.0)
                cand = jnp.where(hit > 0.0, kposf(c), no_index)
                return jnp.minimum(part, _fold_rows(cand, jnp.minimum))

            i = jnp.min(for_chunks(first_index, jnp.full((SUBLANES, t), no_index, _F32)),
                        axis=0, keepdims=True)
            active = need > 0.0
            return (jnp.where(active, v, front_v), jnp.where(active, i, front_i),
                    jnp.where(active, need - 1.0, need))

        front_v, front_i, _ = lax.fori_loop(0, jnp.max(need).astype(jnp.int32), advance,
                                            (front_v, front_i, need))

        def tie_chunk(c, carry):
            s = score_ref[rows(c), :]
            upto = jnp.where(s > front_v, 0.0,
                             jnp.where(s == front_v,
                                       jnp.where(kposf(c) <= front_i, 0.0, _NEG_INF), _NEG_INF))
            keep = jnp.where(s >= nxt, 0.0, jnp.where(s >= thr, upto, _NEG_INF))
            bias_ref[rows(c), :] = jnp.where(c * KEY_CHUNK + kiota <= qpos, keep, _NEG_INF)
            return carry

        for_chunks(tie_chunk, 0)

    grp = B_HEADS // B_KV_HEADS
    gw = grp * t
    exp2_scale = (B_HEAD_DIM ** -0.5) * 1.4426950408889634

    def logit_chunk(c, mx):
        bias = jnp.concatenate([bias_ref[rows(c), :]] * grp, axis=1)
        folded = []
        for g in range(B_KV_HEADS):
            qg = jnp.concatenate(
                [q_ref[0, :, (g * grp + hh) * B_HEAD_DIM:(g * grp + hh + 1) * B_HEAD_DIM]
                 for hh in range(grp)], axis=0)
            l = lax.dot_general(k_ref[0, rows(c), g * B_HEAD_DIM:(g + 1) * B_HEAD_DIM], qg,
                                _NT_DIMS, preferred_element_type=_F32) + bias
            logit_ref[rows(c), g * gw:(g + 1) * gw] = l
            folded.append(_fold_rows(l, jnp.maximum))
        return jnp.maximum(mx, jnp.concatenate(folded, axis=1))

    mx = for_chunks(logit_chunk, jnp.full((SUBLANES, B_HEADS * t), _NEG_INF, _F32))
    mx = jnp.max(mx, axis=0, keepdims=True)

    acc_ref[...] = jnp.zeros_like(acc_ref)
    ones_rows = jnp.ones((acc_ref.shape[0] - B_HEAD_DIM, KEY_CHUNK), _BF16)

    def pv_chunk(c, carry):
        for g in range(B_KV_HEADS):
            cols = slice(g * gw, (g + 1) * gw)
            p = jnp.exp2((logit_ref[rows(c), cols] - mx[:, cols]) * exp2_scale)
            vt_ext = jnp.concatenate(
                [vt_ref[0, c, g * B_HEAD_DIM:(g + 1) * B_HEAD_DIM, :], ones_rows], axis=0)
            acc_ref[:, cols] += jnp.dot(vt_ext, p.astype(_BF16), preferred_element_type=_F32)
        return carry

    for_chunks(pv_chunk, 0)
    out_t = acc_ref[:B_HEAD_DIM, :] / acc_ref[B_HEAD_DIM:B_HEAD_DIM + 1, :]
    for h in range(B_HEADS):
        o_ref[0, :, h * B_HEAD_DIM:(h + 1) * B_HEAD_DIM] = (
            out_t[:, h * t:(h + 1) * t].T.astype(o_ref.dtype))


def _dsa_attention(q, k, vt, qi, ki, wit, side_job=None):
    bsz, seq, _ = q.shape
    assert Q_BLOCK == LANES and seq % KEY_CHUNK == 0
    n_sel = min(TOPK_MAX, seq // 4)
    n_blk = seq // Q_BLOCK

    def q_spec(width):
        return pl.BlockSpec((1, Q_BLOCK, width), lambda b, i: (b, i, 0))

    def kv_spec(width):
        return pl.BlockSpec((1, seq, width), lambda b, i: (b, 0, 0))

    def per_step(spec):
        return pl.BlockSpec(spec.block_shape, lambda b, i: spec.index_map(b * n_blk + i))

    side = None if side_job is None else side_job(bsz * n_blk)
    side_in = [] if side is None else [per_step(s) for s in side.in_specs]
    side_out = [] if side is None else [per_step(s) for s in side.out_specs]
    outs = pl.pallas_call(
        functools.partial(_dsa_kernel, n_sel=n_sel,
                          side_body=None if side is None else side.body,
                          n_side_in=len(side_in), n_side_out=len(side_out)),
        grid=(bsz, n_blk),
        in_specs=[q_spec(IDX_HEADS * LANES),
                  pl.BlockSpec((1, IDX_HEADS, Q_BLOCK), lambda b, i: (b, 0, i)),
                  kv_spec(LANES), q_spec(B_WIDTH), kv_spec(B_KV_WIDTH),
                  pl.BlockSpec((1, seq // KEY_CHUNK, B_KV_WIDTH, KEY_CHUNK),
                               lambda b, i: (b, 0, 0, 0))] + side_in,
        out_specs=[q_spec(B_WIDTH)] + side_out,
        out_shape=[jax.ShapeDtypeStruct((bsz, seq, B_WIDTH), _BF16)]
        + ([] if side is None else list(side.out_shape)),
        scratch_shapes=[pltpu.VMEM((seq, Q_BLOCK), _F32), pltpu.VMEM((seq, Q_BLOCK), _F32),
                        pltpu.VMEM((seq, B_HEADS * Q_BLOCK), _F32),
                        pltpu.VMEM((B_HEAD_DIM + 2 * SUBLANES, B_HEADS * Q_BLOCK), _F32)],
        compiler_params=_compiler_params(2),
        name="dsa_attention",
    )(qi, wit, ki, q, k, vt, *(() if side is None else side.arrays))
    return outs[0] if side is None else tuple(outs)


def _xattn_kernel(q_ref, k_ref, v_ref, o_ref):
    scale = X_HEAD_DIM ** -0.5
    for h in range(X_HEADS):
        hs = slice(h * X_HEAD_DIM, (h + 1) * X_HEAD_DIM)
        logits = lax.dot_general(q_ref[0, :, hs], k_ref[0, :, hs], _NT_DIMS,
                                 preferred_element_type=_F32) * scale
        mx = jnp.max(logits, axis=-1, keepdims=True)
        p = jnp.exp(logits - mx)
        denom = jnp.sum(p, axis=-1, keepdims=True)
        o = jnp.dot(p.astype(_BF16), v_ref[0, :, hs], preferred_element_type=_F32)
        o_ref[0, :, hs] = (o / denom).astype(o_ref.dtype)


def _cross_attention(q, k, v):
    bsz, seq, _ = q.shape
    tq = ROW_BLOCK["cross_attention"]
    m = k.shape[1]
    mem_spec = pl.BlockSpec((1, m, X_WIDTH), lambda b, i: (b, 0, 0))
    return pl.pallas_call(
        _xattn_kernel,
        grid=(bsz, seq // tq),
        in_specs=[pl.BlockSpec((1, tq, X_WIDTH), lambda b, i: (b, i, 0)), mem_spec, mem_spec],
        out_specs=pl.BlockSpec((1, tq, X_WIDTH), lambda b, i: (b, i, 0)),
        out_shape=jax.ShapeDtypeStruct((bsz, seq, X_WIDTH), _BF16),
        compiler_params=_compiler_params(2),
        name="cross_attention",
    )(q, k, v)


def _norm_proj_kernel(x_ref, g_ref, w_ref, o_ref, w_bf16_ref):
    @pl.when(pl.program_id(0) == 0)
    def _():
        w_bf16_ref[...] = w_ref[...].astype(_BF16)

    x = x_ref[...]
    ms = jnp.mean(x * x, axis=-1, keepdims=True)
    h = ((x * lax.rsqrt(ms + EPS)) * g_ref[...]).astype(_BF16)
    o_ref[...] = jnp.dot(h, w_bf16_ref[...], preferred_element_type=_F32).astype(o_ref.dtype)


def _norm_proj(x, g, w_stack, layer, out_dtype, name):
    m, d = x.shape
    tm = ROW_BLOCK["norm_proj"]
    n = w_stack.shape[2]
    return pl.pallas_call(
        _norm_proj_kernel,
        grid=(m // tm,),
        in_specs=[pl.BlockSpec((tm, d), lambda i: (i, 0)),
                  pl.BlockSpec((1, d), lambda i: (0, 0)),
                  pl.BlockSpec((None, d, n), lambda i: (layer, 0, 0),
                               pipeline_mode=pl.Buffered(1))],
        out_specs=pl.BlockSpec((tm, n), lambda i: (i, 0)),
        out_shape=jax.ShapeDtypeStruct((m, n), out_dtype),
        scratch_shapes=[pltpu.VMEM((d, n), _BF16)],
        compiler_params=_compiler_params(1),
        name=name,
    )(x, g.reshape(1, d), w_stack)


def _xattn_out_kernel(o_ref, w_ref, x_ref, g_ref, x_out_ref, h_out_ref):
    x = x_ref[...] + jnp.dot(o_ref[...], w_ref[...], preferred_element_type=_F32)
    x_out_ref[...] = x
    ms = jnp.mean(x * x, axis=-1, keepdims=True)
    h_out_ref[...] = ((x * lax.rsqrt(ms + EPS)) * g_ref[...]).astype(h_out_ref.dtype)


def _xattn_out_and_norm(o, w_bf16, x, g):
    m, k = o.shape
    tm = ROW_BLOCK["xattn_out_norm"]
    d = x.shape[1]
    row_spec = pl.BlockSpec((tm, d), lambda i: (i, 0))
    return pl.pallas_call(
        _xattn_out_kernel,
        grid=(m // tm,),
        in_specs=[pl.BlockSpec((tm, k), lambda i: (i, 0)),
                  pl.BlockSpec((k, d), lambda i: (0, 0), pipeline_mode=pl.Buffered(1)),
                  row_spec,
                  pl.BlockSpec((1, d), lambda i: (0, 0))],
        out_specs=[row_spec, row_spec],
        out_shape=[jax.ShapeDtypeStruct((m, d), _F32), jax.ShapeDtypeStruct((m, d), _BF16)],
        compiler_params=_compiler_params(1),
        name="xattn_out_norm",
    )(o, w_bf16, x, g.reshape(1, d))


def kernel(x, mem, positions, norm_mix_g, w_in, a_norm_g, a_spatial_w, a_spatial_b, p_a, p_b,
           w_out, norm_x_g, norm_mem_g, xq_w, xk_w, xv_w, xo_w, norm_ffn_g, ffn_w1, ffn_w3,
           ffn_w2, final_norm_g):
    bsz, seq, d = x.shape
    m = bsz * seq
    depth = w_in.shape[0]
    ffn_hidden = ffn_w1.shape[-1]

    xf = x.reshape(m, d)
    mem_f = mem.reshape(bsz * MEM_LEN, d)
    w_in_t = jnp.transpose(w_in, (0, 2, 1))
    for l in range(depth):
        def w_in_cols(col0):
            return Rhs(w_in_t, l, col0, transposed=True)

        h = _rmsnorm(xf, norm_mix_g[l], _BF16)
        att = _fused_matmul("proj_att", [h], [(0, w_in_cols(ZA_END))], [], _ep_identity,
                            ATT_WIDTH, _F32)
        za, q, k, vt, qi, ki, wit = _fused_matmul(
            "proj_za", [h], [(0, w_in_cols(0))], [], _ep_identity, ZA_END, _F32,
            side_job=functools.partial(_prep_side_job, att.reshape(bsz, seq, ATT_WIDTH),
                                       positions))
        y_a = _spatial_gating(za, a_norm_g[l], a_spatial_w[l], a_spatial_b[l])
        def weight_casts(n_steps):
            return _merge_side_jobs(
                _cast_side_job(p_a, l, n_steps), _cast_side_job(p_b, l, n_steps),
                _cast_side_job(w_in_t, l, n_steps, row0=ATT_END, rows=2 * d),
                _cast_side_job(w_out, l, n_steps))

        y_b, pa_bf16, pb_bf16, gate_wt_bf16, wo_bf16 = _dsa_attention(
            q, k, vt, qi, ki, wit, side_job=weight_casts)

        def ffn_up_weight_casts(n_steps):
            return _merge_side_jobs(_cast_side_job(ffn_w1, l, n_steps),
                                    _cast_side_job(ffn_w3, l, n_steps))

        merged, w1_bf16, w3_bf16 = _fused_matmul(
            "gated_merge", [y_a, y_b.reshape(m, B_WIDTH), h],
            [(0, Rhs(pa_bf16)), (1, Rhs(pb_bf16)), (2, Rhs(gate_wt_bf16, transposed=True)),
             (2, Rhs(gate_wt_bf16, col0=d, transposed=True))],
            [], _ep_gated_merge, d, _BF16, side_job=ffn_up_weight_casts)
        x1, xo_bf16 = _fused_matmul("out_proj", [merged], [(0, Rhs(wo_bf16))], [xf],
                                    _ep_residual, d, _F32,
                                    side_job=functools.partial(_cast_side_job, xo_w, l))

        qx = _norm_proj(x1, norm_x_g[l], xq_w, l, _BF16, "xattn_q")
        kx = _norm_proj(mem_f, norm_mem_g[l], xk_w, l, _BF16, "xattn_k")
        vx = _norm_proj(mem_f, norm_mem_g[l], xv_w, l, _BF16, "xattn_v")
        ox = _cross_attention(qx.reshape(bsz, seq, X_WIDTH),
                              kx.reshape(bsz, MEM_LEN, X_WIDTH),
                              vx.reshape(bsz, MEM_LEN, X_WIDTH)).reshape(m, X_WIDTH)
        x2, h2 = _xattn_out_and_norm(ox, xo_bf16, x1, norm_ffn_g[l])

        act, w2_bf16 = _fused_matmul("ffn_up", [h2],
                                     [(0, Rhs(w1_bf16)), (0, Rhs(w3_bf16))],
                                     [], _ep_swiglu, ffn_hidden, _BF16,
                                     side_job=functools.partial(_cast_side_job, ffn_w2, l))
        xf = _fused_matmul("ffn_down", [act], [(0, Rhs(w2_bf16))], [x2], _ep_residual, d, _F32)

    return _rmsnorm(xf, final_norm_g, _F32).reshape(bsz, seq, d)
```

```python
import functools
from typing import Callable, NamedTuple, Optional

import jax
import jax.numpy as jnp
from jax import lax
from jax.experimental import pallas as pl
from jax.experimental.pallas import tpu as pltpu

D_MODEL = 4096
MEM_LEN = 256
EPS = 1e-6
ROPE_THETA = 500000.0
CHUNK = 128
A_GROUPS = 16
A_WIDTH = D_MODEL // 2
A_GROUP_DIM = A_WIDTH // A_GROUPS
B_HEADS = 16
B_HEAD_DIM = 128
B_KV_HEADS = 4
B_WIDTH = B_HEADS * B_HEAD_DIM
B_KV_WIDTH = B_KV_HEADS * B_HEAD_DIM
B_ROT = B_HEAD_DIM // 4
IDX_HEADS = 16
IDX_DIM = 64
IDX_WIDTH = IDX_HEADS * IDX_DIM
IDX_ROT = IDX_DIM // 4
TOPK_MAX = 256
Q_BLOCK = 128
X_HEADS = 4
X_HEAD_DIM = 256
X_WIDTH = X_HEADS * X_HEAD_DIM
IN_SIZES = (2 * A_WIDTH, B_WIDTH, B_KV_WIDTH, B_KV_WIDTH, IDX_WIDTH, IDX_DIM, IDX_HEADS,
            2 * D_MODEL)

LANES = 128
SUBLANES = 8
KEY_CHUNK = 512
PROJ_TN = 512
ZA_END = IN_SIZES[0]
ATT_END = ZA_END + sum(IN_SIZES[1:7])
ATT_Q0 = 0
ATT_K0 = ATT_Q0 + B_WIDTH
ATT_V0 = ATT_K0 + B_KV_WIDTH
ATT_QI0 = ATT_V0 + B_KV_WIDTH
ATT_KW0 = ATT_QI0 + IDX_WIDTH
ATT_WIDTH = -(-(ATT_END - ZA_END) // PROJ_TN) * PROJ_TN
assert ZA_END % PROJ_TN == 0 and ATT_KW0 + LANES <= ATT_WIDTH

VMEM_LIMIT_BYTES = 60 * 1024 * 1024


class Tile(NamedTuple):
    tm: int
    tn: int
    prefetch_lhs: bool = False
    row_splits: int = 1


TILES = {
    "proj_att": Tile(1024, PROJ_TN, prefetch_lhs=True),
    "proj_za": Tile(1024, 512, prefetch_lhs=True),
    "gated_merge": Tile(1024, 256, prefetch_lhs=True, row_splits=4),
    "out_proj": Tile(1024, 512, prefetch_lhs=True),
    "ffn_up": Tile(2048, 256, prefetch_lhs=True, row_splits=4),
    "ffn_down": Tile(512, 512, prefetch_lhs=True),
}
ROW_BLOCK = {
    "rmsnorm": 512,
    "spatial_gating": 4 * CHUNK,
    "cross_attention": 1024,
    "norm_proj": 512,
    "xattn_out_norm": 512,
}

_BF16 = jnp.bfloat16
_F32 = jnp.float32
_NEG_INF = float("-inf")
_NN_DIMS = (((1,), (0,)), ((), ()))
_NT_DIMS = (((1,), (1,)), ((), ()))


def _compiler_params(n_axes):
    return pltpu.CompilerParams(
        dimension_semantics=("arbitrary",) * n_axes,
        vmem_limit_bytes=VMEM_LIMIT_BYTES,
    )


def _rmsnorm_kernel(x_ref, g_ref, o_ref):
    x = x_ref[...]
    ms = jnp.mean(x * x, axis=-1, keepdims=True)
    y = x * lax.rsqrt(ms + EPS)
    o_ref[...] = (y * g_ref[...]).astype(o_ref.dtype)


def _rmsnorm(x, g, out_dtype):
    m, d = x.shape
    tm = ROW_BLOCK["rmsnorm"]
    return pl.pallas_call(
        _rmsnorm_kernel,
        grid=(m // tm,),
        in_specs=[pl.BlockSpec((tm, d), lambda i: (i, 0)),
                  pl.BlockSpec((1, d), lambda i: (0, 0))],
        out_specs=pl.BlockSpec((tm, d), lambda i: (i, 0)),
        out_shape=jax.ShapeDtypeStruct((m, d), out_dtype),
        compiler_params=_compiler_params(1),
        name="rmsnorm",
    )(x, g.reshape(1, d))


def _ep_identity(accs, extras):
    return accs[0]


def _ep_residual(accs, extras):
    return extras[0] + accs[0]


def _ep_gated_merge(accs, extras):
    y_a, y_b, gate_a, gate_b = accs
    return jax.nn.sigmoid(gate_a) * y_a + jax.nn.sigmoid(gate_b) * y_b


def _ep_swiglu(accs, extras):
    return jax.nn.silu(accs[0]) * accs[1]


class Rhs(NamedTuple):
    array: jax.Array
    layer: Optional[int] = None
    col0: int = 0
    transposed: bool = False


def _rhs_spec(r, tn):
    if r.transposed and r.layer is None:
        assert r.col0 % tn == 0
        off = r.col0 // tn
        return pl.BlockSpec((tn, r.array.shape[1]), lambda i, j: (j + off, 0))
    if r.transposed:
        k = r.array.shape[2]
        if r.col0 % tn == 0:
            return pl.BlockSpec((None, tn, k), lambda i, j: (r.layer, j + r.col0 // tn, 0))
        assert r.col0 % SUBLANES == 0
        return pl.BlockSpec((pl.Element(1), pl.Element(tn), pl.Element(k)),
                            lambda i, j: (r.layer, (r.col0 // SUBLANES + j * (tn // SUBLANES))
                                          * SUBLANES, 0))
    assert r.col0 % tn == 0
    off = r.col0 // tn
    if r.layer is None:
        return pl.BlockSpec((r.array.shape[0], tn), lambda i, j: (0, j + off))
    return pl.BlockSpec((None, r.array.shape[1], tn), lambda i, j: (r.layer, 0, j + off))


class SideJob(NamedTuple):
    body: Callable
    arrays: tuple
    in_specs: tuple
    out_specs: tuple
    out_shape: tuple


def _cast_body(x_ref, o_ref):
    x = x_ref[0] if len(x_ref.shape) == 3 else x_ref[...]
    o_ref[...] = x.astype(o_ref.dtype)


def _cast_side_job(w_stack, layer, n_steps, row0=0, rows=None):
    rows = w_stack.shape[1] if rows is None else rows
    cols = w_stack.shape[2]
    slab = rows // n_steps
    assert slab * n_steps == rows and slab % (2 * SUBLANES) == 0
    if row0 % slab == 0:
        in_spec = pl.BlockSpec((None, slab, cols), lambda s: (layer, s + row0 // slab, 0))
    else:
        assert row0 % SUBLANES == 0
        in_spec = pl.BlockSpec(
            (pl.Element(1), pl.Element(slab), pl.Element(cols)),
            lambda s: (layer, (row0 // SUBLANES + s * (slab // SUBLANES)) * SUBLANES, 0))
    return SideJob(_cast_body, (w_stack,), (in_spec,),
                   (pl.BlockSpec((slab, cols), lambda s: (s, 0)),),
                   (jax.ShapeDtypeStruct((rows, cols), _BF16),))


def _merge_side_jobs(*jobs):
    n_in = [len(j.in_specs) for j in jobs]
    n_out = [len(j.out_specs) for j in jobs]

    def body(*refs):
        ins, outs = refs[:sum(n_in)], refs[sum(n_in):]
        i = o = 0
        for j, ni, no in zip(jobs, n_in, n_out):
            j.body(*ins[i:i + ni], *outs[o:o + no])
            i, o = i + ni, o + no

    return SideJob(body, sum((j.arrays for j in jobs), ()), sum((j.in_specs for j in jobs), ()),
                   sum((j.out_specs for j in jobs), ()), sum((j.out_shape for j in jobs), ()))


def _mm_kernel(*refs, n_lhs, pair_lhs, rhs_transposed, n_extra, epilogue, side_body, n_side_in,
               row_splits):
    n_rhs = len(pair_lhs)
    lhs_refs = refs[:n_lhs]
    rhs_refs = refs[n_lhs:n_lhs + n_rhs]
    n_in = n_lhs + n_rhs + n_extra
    extra_refs = refs[n_lhs + n_rhs:n_in]
    o_ref = refs[n_in + n_side_in]
    if side_body is not None:
        side_body(*refs[n_in:n_in + n_side_in], *refs[n_in + n_side_in + 1:])
    weights = [(r[0] if len(r.shape) == 3 else r[...]).astype(_BF16) for r in rhs_refs]
    rows_per_split = o_ref.shape[0] // row_splits
    for s in range(row_splits):
        rows = slice(s * rows_per_split, (s + 1) * rows_per_split)
        accs = [lax.dot_general(lhs_refs[li][rows, :], w, _NT_DIMS if tr else _NN_DIMS,
                                preferred_element_type=_F32)
                for li, w, tr in zip(pair_lhs, weights, rhs_transposed)]
        o_ref[rows, :] = epilogue(accs, [e[rows, :] for e in extra_refs]).astype(o_ref.dtype)


def _fused_matmul(name, lhs, pairs, extras, epilogue, n_out, out_dtype, side_job=None):
    tm, tn, prefetch_lhs, row_splits = TILES[name]
    m = lhs[0].shape[0]
    assert m % tm == 0 and n_out % tn == 0
    grid = (m // tm, n_out // tn)
    lhs_mode = {} if prefetch_lhs else {"pipeline_mode": pl.Buffered(1)}
    in_specs = [pl.BlockSpec((tm, a.shape[1]), lambda i, j: (i, 0), **lhs_mode) for a in lhs]
    in_specs += [_rhs_spec(r, tn) for _, r in pairs]
    in_specs += [pl.BlockSpec((tm, tn), lambda i, j: (i, j)) for _ in extras]
    out_specs = [pl.BlockSpec((tm, tn), lambda i, j: (i, j))]
    out_shape = [jax.ShapeDtypeStruct((m, n_out), out_dtype)]
    operands = list(lhs) + [p[1].array for p in pairs] + list(extras)
    side = None if side_job is None else side_job(grid[0] * grid[1])
    if side is not None:
        def per_step(spec):
            return pl.BlockSpec(spec.block_shape,
                                lambda i, j: spec.index_map(i * grid[1] + j))
        in_specs += [per_step(s) for s in side.in_specs]
        out_specs += [per_step(s) for s in side.out_specs]
        out_shape += list(side.out_shape)
        operands += list(side.arrays)
    kern = functools.partial(_mm_kernel, n_lhs=len(lhs), pair_lhs=tuple(p[0] for p in pairs),
                             rhs_transposed=tuple(p[1].transposed for p in pairs),
                             n_extra=len(extras), epilogue=epilogue,
                             side_body=None if side is None else side.body,
                             n_side_in=0 if side is None else len(side.in_specs),
                             row_splits=row_splits)
    outs = pl.pallas_call(
        kern,
        grid=grid,
        in_specs=in_specs,
        out_specs=out_specs,
        out_shape=out_shape,
        compiler_params=_compiler_params(2),
        name=name,
    )(*operands)
    return outs[0] if side is None else tuple(outs)


def _gating_kernel(za_ref, ng_ref, w_ref, bt_ref, o_ref):
    row = lax.broadcasted_iota(jnp.int32, (CHUNK, CHUNK), 0)
    col = lax.broadcasted_iota(jnp.int32, (CHUNK, CHUNK), 1)
    causal = col <= row
    for c in range(za_ref.shape[0] // CHUNK):
        rows = slice(c * CHUNK, (c + 1) * CHUNK)
        z = jax.nn.gelu(za_ref[rows, :])
        u = z[:, :A_WIDTH]
        v = z[:, A_WIDTH:]
        ms = jnp.mean(v * v, axis=-1, keepdims=True)
        vn = ((v * lax.rsqrt(ms + EPS)) * ng_ref[...]).astype(_BF16)
        for g in range(A_GROUPS):
            sl = slice(g * A_GROUP_DIM, (g + 1) * A_GROUP_DIM)
            w = jnp.where(causal, w_ref[g], 0.0).astype(_BF16)
            s = jnp.dot(w, vn[:, sl], preferred_element_type=_F32) + bt_ref[:, g:g + 1]
            o_ref[rows, sl] = (u[:, sl] * s).astype(o_ref.dtype)


def _spatial_gating(za, norm_g, w_s, b_s):
    m = za.shape[0]
    tm = ROW_BLOCK["spatial_gating"]
    return pl.pallas_call(
        _gating_kernel,
        grid=(m // tm,),
        in_specs=[pl.BlockSpec((tm, 2 * A_WIDTH), lambda i: (i, 0)),
                  pl.BlockSpec((1, A_WIDTH), lambda i: (0, 0)),
                  pl.BlockSpec((A_GROUPS, CHUNK, CHUNK), lambda i: (0, 0, 0)),
                  pl.BlockSpec((CHUNK, A_GROUPS), lambda i: (0, 0))],
        out_specs=pl.BlockSpec((tm, A_WIDTH), lambda i: (i, 0)),
        out_shape=jax.ShapeDtypeStruct((m, A_WIDTH), _BF16),
        compiler_params=_compiler_params(1),
        name="spatial_gating",
    )(za, norm_g.reshape(1, A_WIDTH), w_s, b_s.T)


def _lane_inv_freq(rot_dim, period):
    inv_freq = ROPE_THETA ** (-jnp.arange(0, rot_dim, 2, dtype=_F32) / rot_dim)
    head = jnp.concatenate([inv_freq, inv_freq, jnp.zeros((period - rot_dim,), _F32)])
    return jnp.tile(head, LANES // period).reshape(1, LANES)


def _rope_tables(pos, inv_freq_lanes, rot_dim, period):
    half = rot_dim // 2
    ang = pos * inv_freq_lanes
    cos, sin = jnp.cos(ang), jnp.sin(ang)
    lane = lax.broadcasted_iota(jnp.int32, ang.shape, 1) & (period - 1)
    c = jnp.where(lane < rot_dim, cos, 1.0)
    s1 = jnp.where(lane >= half, jnp.where(lane < rot_dim, sin, 0.0), 0.0)
    s2 = jnp.where(lane < half, -sin, 0.0)
    return c, s1, s2


def _rope(x, c, s1, s2, half):
    return x * c + pltpu.roll(x, half, 1) * s1 + pltpu.roll(x, LANES - half, 1) * s2


def _prep_kernel(att_ref, pos_ref, fb_ref, fi_ref,
                 q_ref, k_ref, vt_ref, qi_ref, ki_ref, wit_ref):
    pos = pos_ref[0]
    cb, s1b, s2b = _rope_tables(pos, fb_ref[...], B_ROT, B_HEAD_DIM)
    ci, s1i, s2i = _rope_tables(pos, fi_ref[...], IDX_ROT, IDX_DIM)
    for h in range(B_HEADS):
        x = att_ref[0, :, ATT_Q0 + h * LANES:ATT_Q0 + (h + 1) * LANES]
        q_ref[0, :, h * LANES:(h + 1) * LANES] = _rope(x, cb, s1b, s2b, B_ROT // 2).astype(_BF16)
    for h in range(B_KV_HEADS):
        x = att_ref[0, :, ATT_K0 + h * LANES:ATT_K0 + (h + 1) * LANES]
        k_ref[0, :, h * LANES:(h + 1) * LANES] = _rope(x, cb, s1b, s2b, B_ROT // 2).astype(_BF16)
    vt_ref[0, 0] = att_ref[0, :, ATT_V0:ATT_V0 + B_KV_WIDTH].T.astype(_BF16)
    lane = lax.broadcasted_iota(jnp.int32, ci.shape, 1)
    is_ki = lane < IDX_DIM
    heads_per_group = LANES // IDX_DIM
    for j in range(IDX_WIDTH // LANES):
        x = att_ref[0, :, ATT_QI0 + j * LANES:ATT_QI0 + (j + 1) * LANES]
        y = _rope(x, ci, s1i, s2i, IDX_ROT // 2)
        for r in range(heads_per_group):
            h = j * heads_per_group + r
            yr = y if r == 0 else pltpu.roll(y, LANES - r * IDX_DIM, 1)
            qi_ref[0, :, h * LANES:(h + 1) * LANES] = jnp.where(is_ki, yr, 0.0).astype(_BF16)
    x = att_ref[0, :, ATT_KW0:ATT_KW0 + LANES]
    y = _rope(x, jnp.where(is_ki, ci, 1.0), jnp.where(is_ki, s1i, 0.0),
              jnp.where(is_ki, s2i, 0.0), IDX_ROT // 2)
    ki_ref[0] = jnp.where(is_ki, y, 0.0).astype(_BF16)
    wit_ref[0] = y.T[IDX_DIM:IDX_DIM + IDX_HEADS, :]


def _prep_side_job(att, positions, n_steps):
    bsz, seq, _ = att.shape
    ts = bsz * seq // n_steps
    assert ts * n_steps == bsz * seq and seq % ts == 0 and KEY_CHUNK % ts == 0
    assert ts % (2 * SUBLANES) == 0
    per_seq = seq // ts
    per_chunk = KEY_CHUNK // ts

    def tokens(width):
        return pl.BlockSpec((1, ts, width), lambda s: (s // per_seq, s % per_seq, 0))

    def out(width, dtype):
        return tokens(width), jax.ShapeDtypeStruct((bsz, seq, width), dtype)

    vt_out = (pl.BlockSpec((1, 1, B_KV_WIDTH, ts),
                           lambda s: (s // per_seq, (s % per_seq) // per_chunk, 0,
                                      (s % per_seq) % per_chunk)),
              jax.ShapeDtypeStruct((bsz, seq // KEY_CHUNK, B_KV_WIDTH, KEY_CHUNK), _BF16))
    wit_out = (pl.BlockSpec((1, IDX_HEADS, ts), lambda s: (s // per_seq, 0, s % per_seq)),
               jax.ShapeDtypeStruct((bsz, IDX_HEADS, seq), _F32))
    outs = [out(B_WIDTH, _BF16), out(B_KV_WIDTH, _BF16), vt_out,
            out(IDX_HEADS * LANES, _BF16), out(LANES, _BF16), wit_out]
    freq_spec = pl.BlockSpec((1, LANES), lambda s: (0, 0))
    return SideJob(_prep_kernel,
                   (att, positions.astype(_F32).reshape(bsz, seq, 1),
                    _lane_inv_freq(B_ROT, B_HEAD_DIM), _lane_inv_freq(IDX_ROT, IDX_DIM)),
                   (tokens(ATT_WIDTH), tokens(1), freq_spec, freq_spec),
                   tuple(o[0] for o in outs), tuple(o[1] for o in outs))


def _ordered_int_to_float(key):
    bits = jnp.where(key < 0, key ^ jnp.int32(0x7FFFFFFF), key)
    return lax.bitcast_convert_type(bits, _F32)


def _fold_rows(x, op):
    while x.shape[0] > SUBLANES:
        half = x.shape[0] // 2
        x = op(x[:half], x[half:])
    return x


_DSA_N_IN = 6


def _dsa_kernel(*refs, n_sel, side_body, n_side_in, n_side_out):
    o_at = _DSA_N_IN + n_side_in
    if side_body is not None:
        side_body(*refs[_DSA_N_IN:o_at], *refs[o_at + 1:o_at + 1 + n_side_out])
    own = refs[:_DSA_N_IN] + (refs[o_at],) + refs[o_at + 1 + n_side_out:]
    seq = own[4].shape[1]
    n_chunks = (pl.program_id(1) * Q_BLOCK + Q_BLOCK + KEY_CHUNK - 1) // KEY_CHUNK
    for n in range(1, seq // KEY_CHUNK + 1):
        pl.when(n_chunks == n)(functools.partial(_dsa_block, *own, n_sel=n_sel, n_chunks=n))


def _dsa_block(qi_ref, wit_ref, ki_ref, q_ref, k_ref, vt_ref, o_ref,
               score_ref, bias_ref, logit_ref, acc_ref, *, n_sel, n_chunks):
    t = Q_BLOCK
    blk = pl.program_id(1)
    qpos = blk * t + lax.broadcasted_iota(jnp.int32, (1, t), 1)
    kiota = lax.broadcasted_iota(jnp.int32, (KEY_CHUNK, 1), 0)
    idx_scale = (IDX_DIM ** -0.5) * (IDX_HEADS ** -0.5)

    def rows(c):
        return slice(c * KEY_CHUNK, (c + 1) * KEY_CHUNK)

    def for_chunks(body, carry):
        for c in range(n_chunks):
            carry = body(c, carry)
        return carry

    def indexer_chunk(c, carry):
        ki = ki_ref[0, rows(c), :]
        acc = jnp.zeros((KEY_CHUNK, t), _F32)
        for h in range(0, IDX_HEADS, 2):
            qpair = jnp.concatenate([qi_ref[0, :, h * LANES:(h + 1) * LANES],
                                     qi_ref[0, :, (h + 1) * LANES:(h + 2) * LANES]], axis=0)
            d = lax.dot_general(ki, qpair, _NT_DIMS, preferred_element_type=_F32)
            acc = acc + wit_ref[0, h:h + 1, :] * jnp.maximum(d[:, :t], 0.0)
            acc = acc + wit_ref[0, h + 1:h + 2, :] * jnp.maximum(d[:, t:], 0.0)
        kpos = c * KEY_CHUNK + kiota
        score_ref[rows(c), :] = jnp.where(kpos <= qpos, acc * idx_scale + 0.0, _NEG_INF)
        return carry

    for_chunks(indexer_chunk, 0)

    def count(indicator):
        def chunk(c, part):
            return part + _fold_rows(indicator(score_ref[rows(c), :], c), jnp.add)
        part = for_chunks(chunk, jnp.zeros((SUBLANES, t), _F32))
        return jnp.sum(part, axis=0, keepdims=True)

    def count_ge(cand_f):
        return count(lambda s, c: jnp.where(s >= cand_f, 1.0, 0.0))

    def search():
        int_min = jnp.int32(-2 ** 31)
        base = jnp.where(count_ge(jnp.zeros((1, t), _F32)) >= n_sel, jnp.int32(0), int_min)

        def search_step(it, base):
            cand = base | lax.shift_left(jnp.int32(1), jnp.int32(30) - it)
            cnt = count_ge(_ordered_int_to_float(cand))
            return jnp.where(cnt >= n_sel, cand, base)

        return lax.fori_loop(0, 31, search_step, base)

    if (n_chunks - 1) * KEY_CHUNK >= n_sel:
        base = search()
    else:
        base = lax.cond((blk + 1) * t <= n_sel, lambda: jnp.zeros((1, t), jnp.int32), search)
    thr = jnp.where(qpos + 1 <= n_sel, _NEG_INF, _ordered_int_to_float(base))

    def bias_chunk(c, part):
        s = score_ref[rows(c), :]
        sel = jnp.where(c * KEY_CHUNK + kiota <= qpos, jnp.where(s >= thr, 1.0, 0.0), 0.0)
        bias_ref[rows(c), :] = jnp.where(sel > 0.0, 0.0, _NEG_INF)
        return part + _fold_rows(sel, jnp.add)

    n_ge = jnp.sum(for_chunks(bias_chunk, jnp.zeros((SUBLANES, t), _F32)), axis=0, keepdims=True)

    @pl.when(jnp.max(n_ge) > n_sel)
    def _():
        nxt = _ordered_int_to_float(base + 1)
        tied = n_ge > n_sel
        n_above = count(lambda s, c: jnp.where(s >= nxt, 1.0, 0.0))
        need = jnp.where(tied, n_sel - n_above, 0.0)
        no_index = float(k_ref.shape[1])
        front_v = jnp.where(tied, jnp.inf, _NEG_INF)
        front_i = jnp.where(tied, -1.0, no_index)

        def beyond(s, kposf, front_v, front_i):
            after = jnp.where(s < front_v, 1.0,
                              jnp.where(s == front_v, jnp.where(kposf > front_i, 1.0, 0.0), 0.0))
            return jnp.where(s >= thr, jnp.where(s < nxt, after, 0.0), 0.0)

        def kposf(c):
            return (c * KEY_CHUNK + kiota).astype(_F32)

        def advance(_, carry):
            front_v, front_i, need = carry

            def best_value(c, part):
                s = score_ref[rows(c), :]
                cand = jnp.where(beyond(s, kposf(c), front_v, front_i) > 0.0, s, _NEG_INF)
                return jnp.maximum(part, _fold_rows(cand, jnp.maximum))

            v = jnp.max(for_chunks(best_value, jnp.full((SUBLANES, t), _NEG_INF, _F32)),
                        axis=0, keepdims=True)

            def first_index(c, part):
                s = score_ref[rows(c), :]
                hit = jnp.where(s == v, beyond(s, kposf(c), front_v, front_i), 0.0)
                cand = jnp.where(hit > 0.0, kposf(c), no_index)
                return jnp.minimum(part, _fold_rows(cand, jnp.minimum))

            i = jnp.min(for_chunks(first_index, jnp.full((SUBLANES, t), no_index, _F32)),
                        axis=0, keepdims=True)
            active = need > 0.0
            return (jnp.where(active, v, front_v), jnp.where(active, i, front_i),
                    jnp.where(active, need - 1.0, need))

        front_v, front_i, _ = lax.fori_loop(0, jnp.max(need).astype(jnp.int32), advance,
                                            (front_v, front_i, need))

        def tie_chunk(c, carry):
            s = score_ref[rows(c), :]
            upto = jnp.where(s > front_v, 0.0,
                             jnp.where(s == front_v,
                                       jnp.where(kposf(c) <= front_i, 0.0, _NEG_INF), _NEG_INF))
            keep = jnp.where(s >= nxt, 0.0, jnp.where(s >= thr, upto, _NEG_INF))
            bias_ref[rows(c), :] = jnp.where(c * KEY_CHUNK + kiota <= qpos, keep, _NEG_INF)
            return carry

        for_chunks(tie_chunk, 0)

    grp = B_HEADS // B_KV_HEADS
    gw = grp * t
    exp2_scale = (B_HEAD_DIM ** -0.5) * 1.4426950408889634

    def logit_chunk(c, mx):
        bias = jnp.concatenate([bias_ref[rows(c), :]] * grp, axis=1)
        folded = []
        for g in range(B_KV_HEADS):
            qg = jnp.concatenate(
                [q_ref[0, :, (g * grp + hh) * B_HEAD_DIM:(g * grp + hh + 1) * B_HEAD_DIM]
                 for hh in range(grp)], axis=0)
            l = lax.dot_general(k_ref[0, rows(c), g * B_HEAD_DIM:(g + 1) * B_HEAD_DIM], qg,
                                _NT_DIMS, preferred_element_type=_F32) + bias
            logit_ref[rows(c), g * gw:(g + 1) * gw] = l
            folded.append(_fold_rows(l, jnp.maximum))
        return jnp.maximum(mx, jnp.concatenate(folded, axis=1))

    mx = for_chunks(logit_chunk, jnp.full((SUBLANES, B_HEADS * t), _NEG_INF, _F32))
    mx = jnp.max(mx, axis=0, keepdims=True)

    acc_ref[...] = jnp.zeros_like(acc_ref)
    ones_rows = jnp.ones((acc_ref.shape[0] - B_HEAD_DIM, KEY_CHUNK), _BF16)

    def pv_chunk(c, carry):
        for g in range(B_KV_HEADS):
            cols = slice(g * gw, (g + 1) * gw)
            p = jnp.exp2((logit_ref[rows(c), cols] - mx[:, cols]) * exp2_scale)
            vt_ext = jnp.concatenate(
                [vt_ref[0, c, g * B_HEAD_DIM:(g + 1) * B_HEAD_DIM, :], ones_rows], axis=0)
            acc_ref[:, cols] += jnp.dot(vt_ext, p.astype(_BF16), preferred_element_type=_F32)
        return carry

    for_chunks(pv_chunk, 0)
    out_t = acc_ref[:B_HEAD_DIM, :] / acc_ref[B_HEAD_DIM:B_HEAD_DIM + 1, :]
    for h in range(B_HEADS):
        o_ref[0, :, h * B_HEAD_DIM:(h + 1) * B_HEAD_DIM] = (
            out_t[:, h * t:(h + 1) * t].T.astype(o_ref.dtype))


def _dsa_attention(q, k, vt, qi, ki, wit, side_job=None):
    bsz, seq, _ = q.shape
    assert Q_BLOCK == LANES and seq % KEY_CHUNK == 0
    n_sel = min(TOPK_MAX, seq // 4)
    n_blk = seq // Q_BLOCK

    def q_spec(width):
        return pl.BlockSpec((1, Q_BLOCK, width), lambda b, i: (b, i, 0))

    def kv_spec(width):
        return pl.BlockSpec((1, seq, width), lambda b, i: (b, 0, 0))

    def per_step(spec):
        return pl.BlockSpec(spec.block_shape, lambda b, i: spec.index_map(b * n_blk + i))

    side = None if side_job is None else side_job(bsz * n_blk)
    side_in = [] if side is None else [per_step(s) for s in side.in_specs]
    side_out = [] if side is None else [per_step(s) for s in side.out_specs]
    outs = pl.pallas_call(
        functools.partial(_dsa_kernel, n_sel=n_sel,
                          side_body=None if side is None else side.body,
                          n_side_in=len(side_in), n_side_out=len(side_out)),
        grid=(bsz, n_blk),
        in_specs=[q_spec(IDX_HEADS * LANES),
                  pl.BlockSpec((1, IDX_HEADS, Q_BLOCK), lambda b, i: (b, 0, i)),
                  kv_spec(LANES), q_spec(B_WIDTH), kv_spec(B_KV_WIDTH),
                  pl.BlockSpec((1, seq // KEY_CHUNK, B_KV_WIDTH, KEY_CHUNK),
                               lambda b, i: (b, 0, 0, 0))] + side_in,
        out_specs=[q_spec(B_WIDTH)] + side_out,
        out_shape=[jax.ShapeDtypeStruct((bsz, seq, B_WIDTH), _BF16)]
        + ([] if side is None else list(side.out_shape)),
        scratch_shapes=[pltpu.VMEM((seq, Q_BLOCK), _F32), pltpu.VMEM((seq, Q_BLOCK), _F32),
                        pltpu.VMEM((seq, B_HEADS * Q_BLOCK), _F32),
                        pltpu.VMEM((B_HEAD_DIM + 2 * SUBLANES, B_HEADS * Q_BLOCK), _F32)],
        compiler_params=_compiler_params(2),
        name="dsa_attention",
    )(qi, wit, ki, q, k, vt, *(() if side is None else side.arrays))
    return outs[0] if side is None else tuple(outs)


def _xattn_kernel(q_ref, k_ref, v_ref, o_ref):
    scale = X_HEAD_DIM ** -0.5
    for h in range(X_HEADS):
        hs = slice(h * X_HEAD_DIM, (h + 1) * X_HEAD_DIM)
        logits = lax.dot_general(q_ref[0, :, hs], k_ref[0, :, hs], _NT_DIMS,
                                 preferred_element_type=_F32) * scale
        mx = jnp.max(logits, axis=-1, keepdims=True)
        p = jnp.exp(logits - mx)
        denom = jnp.sum(p, axis=-1, keepdims=True)
        o = jnp.dot(p.astype(_BF16), v_ref[0, :, hs], preferred_element_type=_F32)
        o_ref[0, :, hs] = (o / denom).astype(o_ref.dtype)


def _cross_attention(q, k, v):
    bsz, seq, _ = q.shape
    tq = ROW_BLOCK["cross_attention"]
    m = k.shape[1]
    mem_spec = pl.BlockSpec((1, m, X_WIDTH), lambda b, i: (b, 0, 0))
    return pl.pallas_call(
        _xattn_kernel,
        grid=(bsz, seq // tq),
        in_specs=[pl.BlockSpec((1, tq, X_WIDTH), lambda b, i: (b, i, 0)), mem_spec, mem_spec],
        out_specs=pl.BlockSpec((1, tq, X_WIDTH), lambda b, i: (b, i, 0)),
        out_shape=jax.ShapeDtypeStruct((bsz, seq, X_WIDTH), _BF16),
        compiler_params=_compiler_params(2),
        name="cross_attention",
    )(q, k, v)


def _norm_proj_kernel(x_ref, g_ref, w_ref, o_ref):
    x = x_ref[...]
    ms = jnp.mean(x * x, axis=-1, keepdims=True)
    h = ((x * lax.rsqrt(ms + EPS)) * g_ref[...]).astype(_BF16)
    o_ref[...] = jnp.dot(h, w_ref[...], preferred_element_type=_F32).astype(o_ref.dtype)


def _norm_proj(x, g, w_bf16, out_dtype, name):
    m, d = x.shape
    tm = ROW_BLOCK["norm_proj"]
    n = w_bf16.shape[1]
    return pl.pallas_call(
        _norm_proj_kernel,
        grid=(m // tm,),
        in_specs=[pl.BlockSpec((tm, d), lambda i: (i, 0)),
                  pl.BlockSpec((1, d), lambda i: (0, 0)),
                  pl.BlockSpec((d, n), lambda i: (0, 0), pipeline_mode=pl.Buffered(1))],
        out_specs=pl.BlockSpec((tm, n), lambda i: (i, 0)),
        out_shape=jax.ShapeDtypeStruct((m, n), out_dtype),
        compiler_params=_compiler_params(1),
        name=name,
    )(x, g.reshape(1, d), w_bf16)


def _xattn_out_kernel(o_ref, w_ref, x_ref, g_ref, x_out_ref, h_out_ref):
    x = x_ref[...] + jnp.dot(o_ref[...], w_ref[...], preferred_element_type=_F32)
    x_out_ref[...] = x
    ms = jnp.mean(x * x, axis=-1, keepdims=True)
    h_out_ref[...] = ((x * lax.rsqrt(ms + EPS)) * g_ref[...]).astype(h_out_ref.dtype)


def _xattn_out_and_norm(o, w_bf16, x, g):
    m, k = o.shape
    tm = ROW_BLOCK["xattn_out_norm"]
    d = x.shape[1]
    row_spec = pl.BlockSpec((tm, d), lambda i: (i, 0))
    return pl.pallas_call(
        _xattn_out_kernel,
        grid=(m // tm,),
        in_specs=[pl.BlockSpec((tm, k), lambda i: (i, 0)),
                  pl.BlockSpec((k, d), lambda i: (0, 0), pipeline_mode=pl.Buffered(1)),
                  row_spec,
                  pl.BlockSpec((1, d), lambda i: (0, 0))],
        out_specs=[row_spec, row_spec],
        out_shape=[jax.ShapeDtypeStruct((m, d), _F32), jax.ShapeDtypeStruct((m, d), _BF16)],
        compiler_params=_compiler_params(1),
        name="xattn_out_norm",
    )(o, w_bf16, x, g.reshape(1, d))


def kernel(x, mem, positions, norm_mix_g, w_in, a_norm_g, a_spatial_w, a_spatial_b, p_a, p_b,
           w_out, norm_x_g, norm_mem_g, xq_w, xk_w, xv_w, xo_w, norm_ffn_g, ffn_w1, ffn_w3,
           ffn_w2, final_norm_g):
    bsz, seq, d = x.shape
    m = bsz * seq
    depth = w_in.shape[0]
    ffn_hidden = ffn_w1.shape[-1]

    xf = x.reshape(m, d)
    mem_f = mem.reshape(bsz * MEM_LEN, d)
    w_in_t = jnp.transpose(w_in, (0, 2, 1))
    for l in range(depth):
        def w_in_cols(col0):
            return Rhs(w_in_t, l, col0, transposed=True)

        h = _rmsnorm(xf, norm_mix_g[l], _BF16)
        att = _fused_matmul("proj_att", [h], [(0, w_in_cols(ZA_END))], [], _ep_identity,
                            ATT_WIDTH, _F32)
        za, q, k, vt, qi, ki, wit = _fused_matmul(
            "proj_za", [h], [(0, w_in_cols(0))], [], _ep_identity, ZA_END, _F32,
            side_job=functools.partial(_prep_side_job, att.reshape(bsz, seq, ATT_WIDTH),
                                       positions))
        y_a = _spatial_gating(za, a_norm_g[l], a_spatial_w[l], a_spatial_b[l])
        def weight_casts(n_steps):
            return _merge_side_jobs(
                _cast_side_job(p_a, l, n_steps), _cast_side_job(p_b, l, n_steps),
                _cast_side_job(w_in_t, l, n_steps, row0=ATT_END, rows=2 * d),
                _cast_side_job(w_out, l, n_steps), _cast_side_job(xq_w, l, n_steps),
                _cast_side_job(xk_w, l, n_steps), _cast_side_job(xv_w, l, n_steps))

        y_b, pa_bf16, pb_bf16, gate_wt_bf16, wo_bf16, xq_bf16, xk_bf16, xv_bf16 = _dsa_attention(
            q, k, vt, qi, ki, wit, side_job=weight_casts)

        def ffn_up_weight_casts(n_steps):
            return _merge_side_jobs(_cast_side_job(ffn_w1, l, n_steps),
                                    _cast_side_job(ffn_w3, l, n_steps))

        merged, w1_bf16, w3_bf16 = _fused_matmul(
            "gated_merge", [y_a, y_b.reshape(m, B_WIDTH), h],
            [(0, Rhs(pa_bf16)), (1, Rhs(pb_bf16)), (2, Rhs(gate_wt_bf16, transposed=True)),
             (2, Rhs(gate_wt_bf16, col0=d, transposed=True))],
            [], _ep_gated_merge, d, _BF16, side_job=ffn_up_weight_casts)
        x1, xo_bf16 = _fused_matmul("out_proj", [merged], [(0, Rhs(wo_bf16))], [xf],
                                    _ep_residual, d, _F32,
                                    side_job=functools.partial(_cast_side_job, xo_w, l))

        qx = _norm_proj(x1, norm_x_g[l], xq_bf16, _BF16, "xattn_q")
        kx = _norm_proj(mem_f, norm_mem_g[l], xk_bf16, _BF16, "xattn_k")
        vx = _norm_proj(mem_f, norm_mem_g[l], xv_bf16, _BF16, "xattn_v")
        ox = _cross_attention(qx.reshape(bsz, seq, X_WIDTH),
                              kx.reshape(bsz, MEM_LEN, X_WIDTH),
                              vx.reshape(bsz, MEM_LEN, X_WIDTH)).reshape(m, X_WIDTH)
        x2, h2 = _xattn_out_and_norm(ox, xo_bf16, x1, norm_ffn_g[l])

        act, w2_bf16 = _fused_matmul("ffn_up", [h2],
                                     [(0, Rhs(w1_bf16)), (0, Rhs(w3_bf16))],
                                     [], _ep_swiglu, ffn_hidden, _BF16,
                                     side_job=functools.partial(_cast_side_job, ffn_w2, l))
        xf = _fused_matmul("ffn_down", [act], [(0, Rhs(w2_bf16))], [x2], _ep_residual, d, _F32)

    return _rmsnorm(xf, final_norm_g, _F32).reshape(bsz, seq, d)
```

```python
import functools
from typing import Callable, NamedTuple, Optional

import jax
import jax.numpy as jnp
from jax import lax
from jax.experimental import pallas as pl
from jax.experimental.pallas import tpu as pltpu

D_MODEL = 4096
MEM_LEN = 256
EPS = 1e-6
ROPE_THETA = 500000.0
CHUNK = 128
A_GROUPS = 16
A_WIDTH = D_MODEL // 2
A_GROUP_DIM = A_WIDTH // A_GROUPS
B_HEADS = 16
B_HEAD_DIM = 128
B_KV_HEADS = 4
B_WIDTH = B_HEADS * B_HEAD_DIM
B_KV_WIDTH = B_KV_HEADS * B_HEAD_DIM
B_ROT = B_HEAD_DIM // 4
IDX_HEADS = 16
IDX_DIM = 64
IDX_WIDTH = IDX_HEADS * IDX_DIM
IDX_ROT = IDX_DIM // 4
TOPK_MAX = 256
Q_BLOCK = 128
X_HEADS = 4
X_HEAD_DIM = 256
X_WIDTH = X_HEADS * X_HEAD_DIM
IN_SIZES = (2 * A_WIDTH, B_WIDTH, B_KV_WIDTH, B_KV_WIDTH, IDX_WIDTH, IDX_DIM, IDX_HEADS,
            2 * D_MODEL)

LANES = 128
SUBLANES = 8
KEY_CHUNK = 512
PROJ_TN = 512
ZA_END = IN_SIZES[0]
ATT_END = ZA_END + sum(IN_SIZES[1:7])
ATT_Q0 = 0
ATT_K0 = ATT_Q0 + B_WIDTH
ATT_V0 = ATT_K0 + B_KV_WIDTH
ATT_QI0 = ATT_V0 + B_KV_WIDTH
ATT_KW0 = ATT_QI0 + IDX_WIDTH
ATT_WIDTH = -(-(ATT_END - ZA_END) // PROJ_TN) * PROJ_TN
assert ZA_END % PROJ_TN == 0 and ATT_KW0 + LANES <= ATT_WIDTH

VMEM_LIMIT_BYTES = 60 * 1024 * 1024


class Tile(NamedTuple):
    tm: int
    tn: int
    prefetch_lhs: bool = False
    row_splits: int = 1


TILES = {
    "proj_att": Tile(1024, PROJ_TN, prefetch_lhs=True),
    "proj_za": Tile(1024, 512, prefetch_lhs=True),
    "gated_merge": Tile(1024, 256, prefetch_lhs=True, row_splits=4),
    "out_proj": Tile(1024, 512, prefetch_lhs=True),
    "ffn_up": Tile(2048, 256, prefetch_lhs=True, row_splits=4),
    "ffn_down": Tile(512, 512, prefetch_lhs=True),
}
ROW_BLOCK = {
    "rmsnorm": 512,
    "spatial_gating": 4 * CHUNK,
    "cross_attention": 1024,
    "norm_proj": 512,
    "xattn_out_norm": 512,
}

_BF16 = jnp.bfloat16
_F32 = jnp.float32
_NEG_INF = float("-inf")
_NN_DIMS = (((1,), (0,)), ((), ()))
_NT_DIMS = (((1,), (1,)), ((), ()))


def _compiler_params(n_axes):
    return pltpu.CompilerParams(
        dimension_semantics=("arbitrary",) * n_axes,
        vmem_limit_bytes=VMEM_LIMIT_BYTES,
    )


def _rmsnorm_kernel(x_ref, g_ref, o_ref):
    x = x_ref[...]
    ms = jnp.mean(x * x, axis=-1, keepdims=True)
    y = x * lax.rsqrt(ms + EPS)
    o_ref[...] = (y * g_ref[...]).astype(o_ref.dtype)


def _rmsnorm(x, g, out_dtype):
    m, d = x.shape
    tm = ROW_BLOCK["rmsnorm"]
    return pl.pallas_call(
        _rmsnorm_kernel,
        grid=(m // tm,),
        in_specs=[pl.BlockSpec((tm, d), lambda i: (i, 0)),
                  pl.BlockSpec((1, d), lambda i: (0, 0))],
        out_specs=pl.BlockSpec((tm, d), lambda i: (i, 0)),
        out_shape=jax.ShapeDtypeStruct((m, d), out_dtype),
        compiler_params=_compiler_params(1),
        name="rmsnorm",
    )(x, g.reshape(1, d))


def _ep_identity(accs, extras):
    return accs[0]


def _ep_residual(accs, extras):
    return extras[0] + accs[0]


def _ep_gated_merge(accs, extras):
    y_a, y_b, gate_a, gate_b = accs
    return jax.nn.sigmoid(gate_a) * y_a + jax.nn.sigmoid(gate_b) * y_b


def _ep_swiglu(accs, extras):
    return jax.nn.silu(accs[0]) * accs[1]


class Rhs(NamedTuple):
    array: jax.Array
    layer: Optional[int] = None
    col0: int = 0
    transposed: bool = False


def _rhs_spec(r, tn):
    if r.transposed and r.layer is None:
        assert r.col0 % tn == 0
        off = r.col0 // tn
        return pl.BlockSpec((tn, r.array.shape[1]), lambda i, j: (j + off, 0))
    if r.transposed:
        k = r.array.shape[2]
        if r.col0 % tn == 0:
            return pl.BlockSpec((None, tn, k), lambda i, j: (r.layer, j + r.col0 // tn, 0))
        assert r.col0 % SUBLANES == 0
        return pl.BlockSpec((pl.Element(1), pl.Element(tn), pl.Element(k)),
                            lambda i, j: (r.layer, (r.col0 // SUBLANES + j * (tn // SUBLANES))
                                          * SUBLANES, 0))
    assert r.col0 % tn == 0
    off = r.col0 // tn
    if r.layer is None:
        return pl.BlockSpec((r.array.shape[0], tn), lambda i, j: (0, j + off))
    return pl.BlockSpec((None, r.array.shape[1], tn), lambda i, j: (r.layer, 0, j + off))


class SideJob(NamedTuple):
    body: Callable
    arrays: tuple
    in_specs: tuple
    out_specs: tuple
    out_shape: tuple


def _cast_body(x_ref, o_ref):
    x = x_ref[0] if len(x_ref.shape) == 3 else x_ref[...]
    o_ref[...] = x.astype(o_ref.dtype)


def _cast_side_job(w_stack, layer, n_steps, row0=0, rows=None):
    rows = w_stack.shape[1] if rows is None else rows
    cols = w_stack.shape[2]
    n_slabs = n_steps if rows % n_steps == 0 else 1 << (n_steps.bit_length() - 1)
    slab = rows // n_slabs
    assert slab * n_slabs == rows and slab % (2 * SUBLANES) == 0

    def slab_index(s):
        return s if n_slabs == n_steps else jnp.minimum(s, n_slabs - 1)

    if row0 % slab == 0:
        in_spec = pl.BlockSpec((None, slab, cols),
                               lambda s: (layer, slab_index(s) + row0 // slab, 0))
    else:
        assert row0 % SUBLANES == 0
        in_spec = pl.BlockSpec(
            (pl.Element(1), pl.Element(slab), pl.Element(cols)),
            lambda s: (layer, (row0 // SUBLANES + slab_index(s) * (slab // SUBLANES))
                       * SUBLANES, 0))
    return SideJob(_cast_body, (w_stack,), (in_spec,),
                   (pl.BlockSpec((slab, cols), lambda s: (slab_index(s), 0)),),
                   (jax.ShapeDtypeStruct((rows, cols), _BF16),))


def _merge_side_jobs(*jobs):
    n_in = [len(j.in_specs) for j in jobs]
    n_out = [len(j.out_specs) for j in jobs]

    def body(*refs):
        ins, outs = refs[:sum(n_in)], refs[sum(n_in):]
        i = o = 0
        for j, ni, no in zip(jobs, n_in, n_out):
            j.body(*ins[i:i + ni], *outs[o:o + no])
            i, o = i + ni, o + no

    return SideJob(body, sum((j.arrays for j in jobs), ()), sum((j.in_specs for j in jobs), ()),
                   sum((j.out_specs for j in jobs), ()), sum((j.out_shape for j in jobs), ()))


def _mm_kernel(*refs, n_lhs, pair_lhs, rhs_transposed, n_extra, epilogue, side_body, n_side_in,
               row_splits):
    n_rhs = len(pair_lhs)
    lhs_refs = refs[:n_lhs]
    rhs_refs = refs[n_lhs:n_lhs + n_rhs]
    n_in = n_lhs + n_rhs + n_extra
    extra_refs = refs[n_lhs + n_rhs:n_in]
    o_ref = refs[n_in + n_side_in]
    if side_body is not None:
        side_body(*refs[n_in:n_in + n_side_in], *refs[n_in + n_side_in + 1:])
    weights = [(r[0] if len(r.shape) == 3 else r[...]).astype(_BF16) for r in rhs_refs]
    rows_per_split = o_ref.shape[0] // row_splits
    for s in range(row_splits):
        rows = slice(s * rows_per_split, (s + 1) * rows_per_split)
        accs = [lax.dot_general(lhs_refs[li][rows, :], w, _NT_DIMS if tr else _NN_DIMS,
                                preferred_element_type=_F32)
                for li, w, tr in zip(pair_lhs, weights, rhs_transposed)]
        o_ref[rows, :] = epilogue(accs, [e[rows, :] for e in extra_refs]).astype(o_ref.dtype)


def _fused_matmul(name, lhs, pairs, extras, epilogue, n_out, out_dtype, side_job=None):
    tm, tn, prefetch_lhs, row_splits = TILES[name]
    m = lhs[0].shape[0]
    assert m % tm == 0 and n_out % tn == 0
    grid = (m // tm, n_out // tn)
    lhs_mode = {} if prefetch_lhs else {"pipeline_mode": pl.Buffered(1)}
    in_specs = [pl.BlockSpec((tm, a.shape[1]), lambda i, j: (i, 0), **lhs_mode) for a in lhs]
    in_specs += [_rhs_spec(r, tn) for _, r in pairs]
    in_specs += [pl.BlockSpec((tm, tn), lambda i, j: (i, j)) for _ in extras]
    out_specs = [pl.BlockSpec((tm, tn), lambda i, j: (i, j))]
    out_shape = [jax.ShapeDtypeStruct((m, n_out), out_dtype)]
    operands = list(lhs) + [p[1].array for p in pairs] + list(extras)
    side = None if side_job is None else side_job(grid[0] * grid[1])
    if side is not None:
        def per_step(spec):
            return pl.BlockSpec(spec.block_shape,
                                lambda i, j: spec.index_map(i * grid[1] + j))
        in_specs += [per_step(s) for s in side.in_specs]
        out_specs += [per_step(s) for s in side.out_specs]
        out_shape += list(side.out_shape)
        operands += list(side.arrays)
    kern = functools.partial(_mm_kernel, n_lhs=len(lhs), pair_lhs=tuple(p[0] for p in pairs),
                             rhs_transposed=tuple(p[1].transposed for p in pairs),
                             n_extra=len(extras), epilogue=epilogue,
                             side_body=None if side is None else side.body,
                             n_side_in=0 if side is None else len(side.in_specs),
                             row_splits=row_splits)
    outs = pl.pallas_call(
        kern,
        grid=grid,
        in_specs=in_specs,
        out_specs=out_specs,
        out_shape=out_shape,
        compiler_params=_compiler_params(2),
        name=name,
    )(*operands)
    return outs[0] if side is None else tuple(outs)


def _gating_kernel(za_ref, ng_ref, w_ref, bt_ref, o_ref):
    row = lax.broadcasted_iota(jnp.int32, (CHUNK, CHUNK), 0)
    col = lax.broadcasted_iota(jnp.int32, (CHUNK, CHUNK), 1)
    causal = col <= row
    for c in range(za_ref.shape[0] // CHUNK):
        rows = slice(c * CHUNK, (c + 1) * CHUNK)
        z = jax.nn.gelu(za_ref[rows, :])
        u = z[:, :A_WIDTH]
        v = z[:, A_WIDTH:]
        ms = jnp.mean(v * v, axis=-1, keepdims=True)
        vn = ((v * lax.rsqrt(ms + EPS)) * ng_ref[...]).astype(_BF16)
        for g in range(A_GROUPS):
            sl = slice(g * A_GROUP_DIM, (g + 1) * A_GROUP_DIM)
            w = jnp.where(causal, w_ref[g], 0.0).astype(_BF16)
            s = jnp.dot(w, vn[:, sl], preferred_element_type=_F32) + bt_ref[:, g:g + 1]
            o_ref[rows, sl] = (u[:, sl] * s).astype(o_ref.dtype)


def _spatial_gating(za, norm_g, w_s, b_s):
    m = za.shape[0]
    tm = ROW_BLOCK["spatial_gating"]
    return pl.pallas_call(
        _gating_kernel,
        grid=(m // tm,),
        in_specs=[pl.BlockSpec((tm, 2 * A_WIDTH), lambda i: (i, 0)),
                  pl.BlockSpec((1, A_WIDTH), lambda i: (0, 0)),
                  pl.BlockSpec((A_GROUPS, CHUNK, CHUNK), lambda i: (0, 0, 0)),
                  pl.BlockSpec((CHUNK, A_GROUPS), lambda i: (0, 0))],
        out_specs=pl.BlockSpec((tm, A_WIDTH), lambda i: (i, 0)),
        out_shape=jax.ShapeDtypeStruct((m, A_WIDTH), _BF16),
        compiler_params=_compiler_params(1),
        name="spatial_gating",
    )(za, norm_g.reshape(1, A_WIDTH), w_s, b_s.T)


def _lane_inv_freq(rot_dim, period):
    inv_freq = ROPE_THETA ** (-jnp.arange(0, rot_dim, 2, dtype=_F32) / rot_dim)
    head = jnp.concatenate([inv_freq, inv_freq, jnp.zeros((period - rot_dim,), _F32)])
    return jnp.tile(head, LANES // period).reshape(1, LANES)


def _rope_tables(pos, inv_freq_lanes, rot_dim, period):
    half = rot_dim // 2
    ang = pos * inv_freq_lanes
    cos, sin = jnp.cos(ang), jnp.sin(ang)
    lane = lax.broadcasted_iota(jnp.int32, ang.shape, 1) & (period - 1)
    c = jnp.where(lane < rot_dim, cos, 1.0)
    s1 = jnp.where(lane >= half, jnp.where(lane < rot_dim, sin, 0.0), 0.0)
    s2 = jnp.where(lane < half, -sin, 0.0)
    return c, s1, s2


def _rope(x, c, s1, s2, half):
    return x * c + pltpu.roll(x, half, 1) * s1 + pltpu.roll(x, LANES - half, 1) * s2


def _prep_kernel(att_ref, pos_ref, fb_ref, fi_ref,
                 q_ref, k_ref, vt_ref, qi_ref, ki_ref, wit_ref):
    pos = pos_ref[0]
    cb, s1b, s2b = _rope_tables(pos, fb_ref[...], B_ROT, B_HEAD_DIM)
    ci, s1i, s2i = _rope_tables(pos, fi_ref[...], IDX_ROT, IDX_DIM)
    for h in range(B_HEADS):
        x = att_ref[0, :, ATT_Q0 + h * LANES:ATT_Q0 + (h + 1) * LANES]
        q_ref[0, :, h * LANES:(h + 1) * LANES] = _rope(x, cb, s1b, s2b, B_ROT // 2).astype(_BF16)
    for h in range(B_KV_HEADS):
        x = att_ref[0, :, ATT_K0 + h * LANES:ATT_K0 + (h + 1) * LANES]
        k_ref[0, :, h * LANES:(h + 1) * LANES] = _rope(x, cb, s1b, s2b, B_ROT // 2).astype(_BF16)
    vt_ref[0, 0] = att_ref[0, :, ATT_V0:ATT_V0 + B_KV_WIDTH].T.astype(_BF16)
    lane = lax.broadcasted_iota(jnp.int32, ci.shape, 1)
    is_ki = lane < IDX_DIM
    heads_per_group = LANES // IDX_DIM
    for j in range(IDX_WIDTH // LANES):
        x = att_ref[0, :, ATT_QI0 + j * LANES:ATT_QI0 + (j + 1) * LANES]
        y = _rope(x, ci, s1i, s2i, IDX_ROT // 2)
        for r in range(heads_per_group):
            h = j * heads_per_group + r
            yr = y if r == 0 else pltpu.roll(y, LANES - r * IDX_DIM, 1)
            qi_ref[0, :, h * LANES:(h + 1) * LANES] = jnp.where(is_ki, yr, 0.0).astype(_BF16)
    x = att_ref[0, :, ATT_KW0:ATT_KW0 + LANES]
    y = _rope(x, jnp.where(is_ki, ci, 1.0), jnp.where(is_ki, s1i, 0.0),
              jnp.where(is_ki, s2i, 0.0), IDX_ROT // 2)
    ki_ref[0] = jnp.where(is_ki, y, 0.0).astype(_BF16)
    wit_ref[0] = y.T[IDX_DIM:IDX_DIM + IDX_HEADS, :]


def _prep_side_job(att, positions, n_steps):
    bsz, seq, _ = att.shape
    ts = bsz * seq // n_steps
    assert ts * n_steps == bsz * seq and seq % ts == 0 and KEY_CHUNK % ts == 0
    assert ts % (2 * SUBLANES) == 0
    per_seq = seq // ts
    per_chunk = KEY_CHUNK // ts

    def tokens(width):
        return pl.BlockSpec((1, ts, width), lambda s: (s // per_seq, s % per_seq, 0))

    def out(width, dtype):
        return tokens(width), jax.ShapeDtypeStruct((bsz, seq, width), dtype)

    vt_out = (pl.BlockSpec((1, 1, B_KV_WIDTH, ts),
                           lambda s: (s // per_seq, (s % per_seq) // per_chunk, 0,
                                      (s % per_seq) % per_chunk)),
              jax.ShapeDtypeStruct((bsz, seq // KEY_CHUNK, B_KV_WIDTH, KEY_CHUNK), _BF16))
    wit_out = (pl.BlockSpec((1, IDX_HEADS, ts), lambda s: (s // per_seq, 0, s % per_seq)),
               jax.ShapeDtypeStruct((bsz, IDX_HEADS, seq), _F32))
    outs = [out(B_WIDTH, _BF16), out(B_KV_WIDTH, _BF16), vt_out,
            out(IDX_HEADS * LANES, _BF16), out(LANES, _BF16), wit_out]
    freq_spec = pl.BlockSpec((1, LANES), lambda s: (0, 0))
    return SideJob(_prep_kernel,
                   (att, positions.astype(_F32).reshape(bsz, seq, 1),
                    _lane_inv_freq(B_ROT, B_HEAD_DIM), _lane_inv_freq(IDX_ROT, IDX_DIM)),
                   (tokens(ATT_WIDTH), tokens(1), freq_spec, freq_spec),
                   tuple(o[0] for o in outs), tuple(o[1] for o in outs))


def _ordered_int_to_float(key):
    bits = jnp.where(key < 0, key ^ jnp.int32(0x7FFFFFFF), key)
    return lax.bitcast_convert_type(bits, _F32)


def _fold_rows(x, op):
    while x.shape[0] > SUBLANES:
        half = x.shape[0] // 2
        x = op(x[:half], x[half:])
    return x


_DSA_N_IN = 6


def _dsa_kernel(*refs, n_sel, side_body, n_side_in, n_side_out):
    o_at = _DSA_N_IN + n_side_in
    if side_body is not None:
        side_body(*refs[_DSA_N_IN:o_at], *refs[o_at + 1:o_at + 1 + n_side_out])
    own = refs[:_DSA_N_IN] + (refs[o_at],) + refs[o_at + 1 + n_side_out:]
    seq = own[4].shape[1]
    n_chunks = (pl.program_id(1) * Q_BLOCK + Q_BLOCK + KEY_CHUNK - 1) // KEY_CHUNK
    for n in range(1, seq // KEY_CHUNK + 1):
        pl.when(n_chunks == n)(functools.partial(_dsa_block, *own, n_sel=n_sel, n_chunks=n))


def _dsa_block(qi_ref, wit_ref, ki_ref, q_ref, k_ref, vt_ref, o_ref,
               score_ref, bias_ref, logit_ref, acc_ref, *, n_sel, n_chunks):
    t = Q_BLOCK
    blk = pl.program_id(1)
    qpos = blk * t + lax.broadcasted_iota(jnp.int32, (1, t), 1)
    kiota = lax.broadcasted_iota(jnp.int32, (KEY_CHUNK, 1), 0)
    idx_scale = (IDX_DIM ** -0.5) * (IDX_HEADS ** -0.5)

    def rows(c):
        return slice(c * KEY_CHUNK, (c + 1) * KEY_CHUNK)

    def for_chunks(body, carry):
        for c in range(n_chunks):
            carry = body(c, carry)
        return carry

    def indexer_chunk(c, carry):
        ki = ki_ref[0, rows(c), :]
        acc = jnp.zeros((KEY_CHUNK, t), _F32)
        for h in range(0, IDX_HEADS, 2):
            qpair = jnp.concatenate([qi_ref[0, :, h * LANES:(h + 1) * LANES],
                                     qi_ref[0, :, (h + 1) * LANES:(h + 2) * LANES]], axis=0)
            d = lax.dot_general(ki, qpair, _NT_DIMS, preferred_element_type=_F32)
            acc = acc + wit_ref[0, h:h + 1, :] * jnp.maximum(d[:, :t], 0.0)
            acc = acc + wit_ref[0, h + 1:h + 2, :] * jnp.maximum(d[:, t:], 0.0)
        kpos = c * KEY_CHUNK + kiota
        score_ref[rows(c), :] = jnp.where(kpos <= qpos, acc * idx_scale + 0.0, _NEG_INF)
        return carry

    for_chunks(indexer_chunk, 0)

    def count(indicator):
        def chunk(c, part):
            return part + _fold_rows(indicator(score_ref[rows(c), :], c), jnp.add)
        part = for_chunks(chunk, jnp.zeros((SUBLANES, t), _F32))
        return jnp.sum(part, axis=0, keepdims=True)

    def count_ge(cand_f):
        return count(lambda s, c: jnp.where(s >= cand_f, 1.0, 0.0))

    def search():
        int_min = jnp.int32(-2 ** 31)
        base = jnp.where(count_ge(jnp.zeros((1, t), _F32)) >= n_sel, jnp.int32(0), int_min)

        def search_step(it, base):
            cand = base | lax.shift_left(jnp.int32(1), jnp.int32(30) - it)
            cnt = count_ge(_ordered_int_to_float(cand))
            return jnp.where(cnt >= n_sel, cand, base)

        return lax.fori_loop(0, 31, search_step, base)

    if (n_chunks - 1) * KEY_CHUNK >= n_sel:
        base = search()
    else:
        base = lax.cond((blk + 1) * t <= n_sel, lambda: jnp.zeros((1, t), jnp.int32), search)
    thr = jnp.where(qpos + 1 <= n_sel, _NEG_INF, _ordered_int_to_float(base))

    def bias_chunk(c, part):
        s = score_ref[rows(c), :]
        sel = jnp.where(c * KEY_CHUNK + kiota <= qpos, jnp.where(s >= thr, 1.0, 0.0), 0.0)
        bias_ref[rows(c), :] = jnp.where(sel > 0.0, 0.0, _NEG_INF)
        return part + _fold_rows(sel, jnp.add)

    n_ge = jnp.sum(for_chunks(bias_chunk, jnp.zeros((SUBLANES, t), _F32)), axis=0, keepdims=True)

    @pl.when(jnp.max(n_ge) > n_sel)
    def _():
        nxt = _ordered_int_to_float(base + 1)
        tied = n_ge > n_sel
        n_above = count(lambda s, c: jnp.where(s >= nxt, 1.0, 0.0))
        need = jnp.where(tied, n_sel - n_above, 0.0)
        no_index = float(k_ref.shape[1])
        front_v = jnp.where(tied, jnp.inf, _NEG_INF)
        front_i = jnp.where(tied, -1.0, no_index)

        def beyond(s, kposf, front_v, front_i):
            after = jnp.where(s < front_v, 1.0,
                              jnp.where(s == front_v, jnp.where(kposf > front_i, 1.0, 0.0), 0.0))
            return jnp.where(s >= thr, jnp.where(s < nxt, after, 0.0), 0.0)

        def kposf(c):
            return (c * KEY_CHUNK + kiota).astype(_F32)

        def advance(_, carry):
            front_v, front_i, need = carry

            def best_value(c, part):
                s = score_ref[rows(c), :]
                cand = jnp.where(beyond(s, kposf(c), front_v, front_i) > 0.0, s, _NEG_INF)
                return jnp.maximum(part, _fold_rows(cand, jnp.maximum))

            v = jnp.max(for_chunks(best_value, jnp.full((SUBLANES, t), _NEG_INF, _F32)),
                        axis=0, keepdims=True)

            def first_index(c, part):
                s = score_ref[rows(c), :]
                hit = jnp.where(s == v, beyond(s, kposf(c), front_v, front_i), 0.0)
                cand = jnp.where(hit > 0.0, kposf(c), no_index)
                return jnp.minimum(part, _fold_rows(cand, jnp.minimum))

            i = jnp.min(for_chunks(first_index, jnp.full((SUBLANES, t), no_index, _F32)),
                        axis=0, keepdims=True)
            active = need > 0.0
            return (jnp.where(active, v, front_v), jnp.where(active, i, front_i),
                    jnp.where(active, need - 1.0, need))

        front_v, front_i, _ = lax.fori_loop(0, jnp.max(need).astype(jnp.int32), advance,
                                            (front_v, front_i, need))

        def tie_chunk(c, carry):
            s = score_ref[rows(c), :]
            upto = jnp.where(s > front_v, 0.0,
                             jnp.where(s == front_v,
                                       jnp.where(kposf(c) <= front_i, 0.0, _NEG_INF), _NEG_INF))
            keep = jnp.where(s >= nxt, 0.0, jnp.where(s >= thr, upto, _NEG_INF))
            bias_ref[rows(c), :] = jnp.where(c * KEY_CHUNK + kiota <= qpos, keep, _NEG_INF)
            return carry

        for_chunks(tie_chunk, 0)

    grp = B_HEADS // B_KV_HEADS
    gw = grp * t
    exp2_scale = (B_HEAD_DIM ** -0.5) * 1.4426950408889634

    def logit_chunk(c, mx):
        bias = jnp.concatenate([bias_ref[rows(c), :]] * grp, axis=1)
        folded = []
        for g in range(B_KV_HEADS):
            qg = jnp.concatenate(
                [q_ref[0, :, (g * grp + hh) * B_HEAD_DIM:(g * grp + hh + 1) * B_HEAD_DIM]
                 for hh in range(grp)], axis=0)
            l = lax.dot_general(k_ref[0, rows(c), g * B_HEAD_DIM:(g + 1) * B_HEAD_DIM], qg,
                                _NT_DIMS, preferred_element_type=_F32) + bias
            logit_ref[rows(c), g * gw:(g + 1) * gw] = l
            folded.append(_fold_rows(l, jnp.maximum))
        return jnp.maximum(mx, jnp.concatenate(folded, axis=1))

    mx = for_chunks(logit_chunk, jnp.full((SUBLANES, B_HEADS * t), _NEG_INF, _F32))
    mx = jnp.max(mx, axis=0, keepdims=True)

    acc_ref[...] = jnp.zeros_like(acc_ref)
    ones_rows = jnp.ones((acc_ref.shape[0] - B_HEAD_DIM, KEY_CHUNK), _BF16)

    def pv_chunk(c, carry):
        for g in range(B_KV_HEADS):
            cols = slice(g * gw, (g + 1) * gw)
            p = jnp.exp2((logit_ref[rows(c), cols] - mx[:, cols]) * exp2_scale)
            vt_ext = jnp.concatenate(
                [vt_ref[0, c, g * B_HEAD_DIM:(g + 1) * B_HEAD_DIM, :], ones_rows], axis=0)
            acc_ref[:, cols] += jnp.dot(vt_ext, p.astype(_BF16), preferred_element_type=_F32)
        return carry

    for_chunks(pv_chunk, 0)
    out_t = acc_ref[:B_HEAD_DIM, :] / acc_ref[B_HEAD_DIM:B_HEAD_DIM + 1, :]
    for h in range(B_HEADS):
        o_ref[0, :, h * B_HEAD_DIM:(h + 1) * B_HEAD_DIM] = (
            out_t[:, h * t:(h + 1) * t].T.astype(o_ref.dtype))


def _dsa_attention(q, k, vt, qi, ki, wit, side_job=None):
    bsz, seq, _ = q.shape
    assert Q_BLOCK == LANES and seq % KEY_CHUNK == 0
    n_sel = min(TOPK_MAX, seq // 4)
    n_blk = seq // Q_BLOCK

    def q_spec(width):
        return pl.BlockSpec((1, Q_BLOCK, width), lambda b, i: (b, i, 0))

    def kv_spec(width):
        return pl.BlockSpec((1, seq, width), lambda b, i: (b, 0, 0))

    def per_step(spec):
        return pl.BlockSpec(spec.block_shape, lambda b, i: spec.index_map(b * n_blk + i))

    side = None if side_job is None else side_job(bsz * n_blk)
    side_in = [] if side is None else [per_step(s) for s in side.in_specs]
    side_out = [] if side is None else [per_step(s) for s in side.out_specs]
    outs = pl.pallas_call(
        functools.partial(_dsa_kernel, n_sel=n_sel,
                          side_body=None if side is None else side.body,
                          n_side_in=len(side_in), n_side_out=len(side_out)),
        grid=(bsz, n_blk),
        in_specs=[q_spec(IDX_HEADS * LANES),
                  pl.BlockSpec((1, IDX_HEADS, Q_BLOCK), lambda b, i: (b, 0, i)),
                  kv_spec(LANES), q_spec(B_WIDTH), kv_spec(B_KV_WIDTH),
                  pl.BlockSpec((1, seq // KEY_CHUNK, B_KV_WIDTH, KEY_CHUNK),
                               lambda b, i: (b, 0, 0, 0))] + side_in,
        out_specs=[q_spec(B_WIDTH)] + side_out,
        out_shape=[jax.ShapeDtypeStruct((bsz, seq, B_WIDTH), _BF16)]
        + ([] if side is None else list(side.out_shape)),
        scratch_shapes=[pltpu.VMEM((seq, Q_BLOCK), _F32), pltpu.VMEM((seq, Q_BLOCK), _F32),
                        pltpu.VMEM((seq, B_HEADS * Q_BLOCK), _F32),
                        pltpu.VMEM((B_HEAD_DIM + 2 * SUBLANES, B_HEADS * Q_BLOCK), _F32)],
        compiler_params=_compiler_params(2),
        name="dsa_attention",
    )(qi, wit, ki, q, k, vt, *(() if side is None else side.arrays))
    return outs[0] if side is None else tuple(outs)


def _xattn_kernel(q_ref, k_ref, v_ref, o_ref):
    scale = X_HEAD_DIM ** -0.5
    for h in range(X_HEADS):
        hs = slice(h * X_HEAD_DIM, (h + 1) * X_HEAD_DIM)
        logits = lax.dot_general(q_ref[0, :, hs], k_ref[0, :, hs], _NT_DIMS,
                                 preferred_element_type=_F32) * scale
        mx = jnp.max(logits, axis=-1, keepdims=True)
        p = jnp.exp(logits - mx)
        denom = jnp.sum(p, axis=-1, keepdims=True)
        o = jnp.dot(p.astype(_BF16), v_ref[0, :, hs], preferred_element_type=_F32)
        o_ref[0, :, hs] = (o / denom).astype(o_ref.dtype)


def _cross_attention(q, k, v):
    bsz, seq, _ = q.shape
    tq = ROW_BLOCK["cross_attention"]
    m = k.shape[1]
    mem_spec = pl.BlockSpec((1, m, X_WIDTH), lambda b, i: (b, 0, 0))
    return pl.pallas_call(
        _xattn_kernel,
        grid=(bsz, seq // tq),
        in_specs=[pl.BlockSpec((1, tq, X_WIDTH), lambda b, i: (b, i, 0)), mem_spec, mem_spec],
        out_specs=pl.BlockSpec((1, tq, X_WIDTH), lambda b, i: (b, i, 0)),
        out_shape=jax.ShapeDtypeStruct((bsz, seq, X_WIDTH), _BF16),
        compiler_params=_compiler_params(2),
        name="cross_attention",
    )(q, k, v)


def _norm_proj_kernel(x_ref, g_ref, w_ref, o_ref):
    x = x_ref[...]
    ms = jnp.mean(x * x, axis=-1, keepdims=True)
    h = ((x * lax.rsqrt(ms + EPS)) * g_ref[...]).astype(_BF16)
    o_ref[...] = jnp.dot(h, w_ref[...], preferred_element_type=_F32).astype(o_ref.dtype)


def _norm_proj(x, g, w_bf16, out_dtype, name):
    m, d = x.shape
    tm = ROW_BLOCK["norm_proj"]
    n = w_bf16.shape[1]
    return pl.pallas_call(
        _norm_proj_kernel,
        grid=(m // tm,),
        in_specs=[pl.BlockSpec((tm, d), lambda i: (i, 0)),
                  pl.BlockSpec((1, d), lambda i: (0, 0)),
                  pl.BlockSpec((d, n), lambda i: (0, 0), pipeline_mode=pl.Buffered(1))],
        out_specs=pl.BlockSpec((tm, n), lambda i: (i, 0)),
        out_shape=jax.ShapeDtypeStruct((m, n), out_dtype),
        compiler_params=_compiler_params(1),
        name=name,
    )(x, g.reshape(1, d), w_bf16)


def _xattn_out_kernel(o_ref, w_ref, x_ref, g_ref, x_out_ref, h_out_ref):
    x = x_ref[...] + jnp.dot(o_ref[...], w_ref[...], preferred_element_type=_F32)
    x_out_ref[...] = x
    ms = jnp.mean(x * x, axis=-1, keepdims=True)
    h_out_ref[...] = ((x * lax.rsqrt(ms + EPS)) * g_ref[...]).astype(h_out_ref.dtype)


def _xattn_out_and_norm(o, w_bf16, x, g):
    m, k = o.shape
    tm = ROW_BLOCK["xattn_out_norm"]
    d = x.shape[1]
    row_spec = pl.BlockSpec((tm, d), lambda i: (i, 0))
    return pl.pallas_call(
        _xattn_out_kernel,
        grid=(m // tm,),
        in_specs=[pl.BlockSpec((tm, k), lambda i: (i, 0)),
                  pl.BlockSpec((k, d), lambda i: (0, 0), pipeline_mode=pl.Buffered(1)),
                  row_spec,
                  pl.BlockSpec((1, d), lambda i: (0, 0))],
        out_specs=[row_spec, row_spec],
        out_shape=[jax.ShapeDtypeStruct((m, d), _F32), jax.ShapeDtypeStruct((m, d), _BF16)],
        compiler_params=_compiler_params(1),
        name="xattn_out_norm",
    )(o, w_bf16, x, g.reshape(1, d))


def kernel(x, mem, positions, norm_mix_g, w_in, a_norm_g, a_spatial_w, a_spatial_b, p_a, p_b,
           w_out, norm_x_g, norm_mem_g, xq_w, xk_w, xv_w, xo_w, norm_ffn_g, ffn_w1, ffn_w3,
           ffn_w2, final_norm_g):
    bsz, seq, d = x.shape
    m = bsz * seq
    depth = w_in.shape[0]
    ffn_hidden = ffn_w1.shape[-1]

    xf = x.reshape(m, d)
    mem_f = mem.reshape(bsz * MEM_LEN, d)
    w_in_t = jnp.transpose(w_in, (0, 2, 1))
    for l in range(depth):
        def w_in_cols(col0):
            return Rhs(w_in_t, l, col0, transposed=True)

        h = _rmsnorm(xf, norm_mix_g[l], _BF16)
        att, za_wt_bf16 = _fused_matmul(
            "proj_att", [h], [(0, w_in_cols(ZA_END))], [], _ep_identity, ATT_WIDTH, _F32,
            side_job=lambda n_steps: _cast_side_job(w_in_t, l, n_steps, rows=ZA_END))
        za, q, k, vt, qi, ki, wit = _fused_matmul(
            "proj_za", [h], [(0, Rhs(za_wt_bf16, transposed=True))], [], _ep_identity, ZA_END,
            _F32,
            side_job=functools.partial(_prep_side_job, att.reshape(bsz, seq, ATT_WIDTH),
                                       positions))
        y_a = _spatial_gating(za, a_norm_g[l], a_spatial_w[l], a_spatial_b[l])
        def weight_casts(n_steps):
            return _merge_side_jobs(
                _cast_side_job(p_a, l, n_steps), _cast_side_job(p_b, l, n_steps),
                _cast_side_job(w_in_t, l, n_steps, row0=ATT_END, rows=2 * d),
                _cast_side_job(w_out, l, n_steps), _cast_side_job(xq_w, l, n_steps),
                _cast_side_job(xk_w, l, n_steps), _cast_side_job(xv_w, l, n_steps))

        y_b, pa_bf16, pb_bf16, gate_wt_bf16, wo_bf16, xq_bf16, xk_bf16, xv_bf16 = _dsa_attention(
            q, k, vt, qi, ki, wit, side_job=weight_casts)

        def ffn_up_weight_casts(n_steps):
            return _merge_side_jobs(_cast_side_job(ffn_w1, l, n_steps),
                                    _cast_side_job(ffn_w3, l, n_steps))

        merged, w1_bf16, w3_bf16 = _fused_matmul(
            "gated_merge", [y_a, y_b.reshape(m, B_WIDTH), h],
            [(0, Rhs(pa_bf16)), (1, Rhs(pb_bf16)), (2, Rhs(gate_wt_bf16, transposed=True)),
             (2, Rhs(gate_wt_bf16, col0=d, transposed=True))],
            [], _ep_gated_merge, d, _BF16, side_job=ffn_up_weight_casts)
        x1, xo_bf16 = _fused_matmul("out_proj", [merged], [(0, Rhs(wo_bf16))], [xf],
                                    _ep_residual, d, _F32,
                                    side_job=functools.partial(_cast_side_job, xo_w, l))

        qx = _norm_proj(x1, norm_x_g[l], xq_bf16, _BF16, "xattn_q")
        kx = _norm_proj(mem_f, norm_mem_g[l], xk_bf16, _BF16, "xattn_k")
        vx = _norm_proj(mem_f, norm_mem_g[l], xv_bf16, _BF16, "xattn_v")
        ox = _cross_attention(qx.reshape(bsz, seq, X_WIDTH),
                              kx.reshape(bsz, MEM_LEN, X_WIDTH),
                              vx.reshape(bsz, MEM_LEN, X_WIDTH)).reshape(m, X_WIDTH)
        x2, h2 = _xattn_out_and_norm(ox, xo_bf16, x1, norm_ffn_g[l])

        act, w2_bf16 = _fused_matmul("ffn_up", [h2],
                                     [(0, Rhs(w1_bf16)), (0, Rhs(w3_bf16))],
                                     [], _ep_swiglu, ffn_hidden, _BF16,
                                     side_job=functools.partial(_cast_side_job, ffn_w2, l))
        xf = _fused_matmul("ffn_down", [act], [(0, Rhs(w2_bf16))], [x2], _ep_residual, d, _F32)

    return _rmsnorm(xf, final_norm_g, _F32).reshape(bsz, seq, d)
```

```python
import functools
from typing import Callable, NamedTuple, Optional

import jax
import jax.numpy as jnp
from jax import lax
from jax.experimental import pallas as pl
from jax.experimental.pallas import tpu as pltpu

D_MODEL = 4096
MEM_LEN = 256
EPS = 1e-6
ROPE_THETA = 500000.0
CHUNK = 128
A_GROUPS = 16
A_WIDTH = D_MODEL // 2
A_GROUP_DIM = A_WIDTH // A_GROUPS
B_HEADS = 16
B_HEAD_DIM = 128
B_KV_HEADS = 4
B_WIDTH = B_HEADS * B_HEAD_DIM
B_KV_WIDTH = B_KV_HEADS * B_HEAD_DIM
B_ROT = B_HEAD_DIM // 4
IDX_HEADS = 16
IDX_DIM = 64
IDX_WIDTH = IDX_HEADS * IDX_DIM
IDX_ROT = IDX_DIM // 4
TOPK_MAX = 256
Q_BLOCK = 128
X_HEADS = 4
X_HEAD_DIM = 256
X_WIDTH = X_HEADS * X_HEAD_DIM
IN_SIZES = (2 * A_WIDTH, B_WIDTH, B_KV_WIDTH, B_KV_WIDTH, IDX_WIDTH, IDX_DIM, IDX_HEADS,
            2 * D_MODEL)

LANES = 128
SUBLANES = 8
KEY_CHUNK = 512
PROJ_TN = 512
ZA_END = IN_SIZES[0]
ATT_END = ZA_END + sum(IN_SIZES[1:7])
ATT_Q0 = 0
ATT_K0 = ATT_Q0 + B_WIDTH
ATT_V0 = ATT_K0 + B_KV_WIDTH
ATT_QI0 = ATT_V0 + B_KV_WIDTH
ATT_KW0 = ATT_QI0 + IDX_WIDTH
ATT_WIDTH = -(-(ATT_END - ZA_END) // PROJ_TN) * PROJ_TN
assert ZA_END % PROJ_TN == 0 and ATT_KW0 + LANES <= ATT_WIDTH

VMEM_LIMIT_BYTES = 60 * 1024 * 1024


class Tile(NamedTuple):
    tm: int
    tn: int
    prefetch_lhs: bool = False
    row_splits: int = 1


TILES = {
    "proj_att": Tile(1024, PROJ_TN, prefetch_lhs=True),
    "proj_za": Tile(1024, 512, prefetch_lhs=True),
    "gated_merge": Tile(1024, 256, prefetch_lhs=True, row_splits=4),
    "out_proj": Tile(1024, 512, prefetch_lhs=True),
    "ffn_up": Tile(2048, 256, prefetch_lhs=True, row_splits=4),
    "ffn_down": Tile(512, 512, prefetch_lhs=True),
}
ROW_BLOCK = {
    "rmsnorm": 512,
    "spatial_gating": 4 * CHUNK,
    "cross_attention": 1024,
    "norm_proj": 512,
    "xattn_out_norm": 512,
}

_BF16 = jnp.bfloat16
_F32 = jnp.float32
_NEG_INF = float("-inf")
_NN_DIMS = (((1,), (0,)), ((), ()))
_NT_DIMS = (((1,), (1,)), ((), ()))


def _compiler_params(n_axes):
    return pltpu.CompilerParams(
        dimension_semantics=("arbitrary",) * n_axes,
        vmem_limit_bytes=VMEM_LIMIT_BYTES,
    )


def _rmsnorm_kernel(x_ref, g_ref, o_ref):
    x = x_ref[...]
    ms = jnp.mean(x * x, axis=-1, keepdims=True)
    y = x * lax.rsqrt(ms + EPS)
    o_ref[...] = (y * g_ref[...]).astype(o_ref.dtype)


def _rmsnorm(x, g, out_dtype):
    m, d = x.shape
    tm = ROW_BLOCK["rmsnorm"]
    return pl.pallas_call(
        _rmsnorm_kernel,
        grid=(m // tm,),
        in_specs=[pl.BlockSpec((tm, d), lambda i: (i, 0)),
                  pl.BlockSpec((1, d), lambda i: (0, 0))],
        out_specs=pl.BlockSpec((tm, d), lambda i: (i, 0)),
        out_shape=jax.ShapeDtypeStruct((m, d), out_dtype),
        compiler_params=_compiler_params(1),
        name="rmsnorm",
    )(x, g.reshape(1, d))


def _ep_identity(accs, extras):
    return accs[0]


def _ep_residual(accs, extras):
    return extras[0] + accs[0]


def _ep_gated_merge(accs, extras):
    y_a, y_b, gate_a, gate_b = accs
    return jax.nn.sigmoid(gate_a) * y_a + jax.nn.sigmoid(gate_b) * y_b


def _ep_swiglu(accs, extras):
    return jax.nn.silu(accs[0]) * accs[1]


class Rhs(NamedTuple):
    array: jax.Array
    layer: Optional[int] = None
    col0: int = 0
    transposed: bool = False


def _rhs_spec(r, tn):
    if r.transposed and r.layer is None:
        assert r.col0 % tn == 0
        off = r.col0 // tn
        return pl.BlockSpec((tn, r.array.shape[1]), lambda i, j: (j + off, 0))
    if r.transposed:
        k = r.array.shape[2]
        if r.col0 % tn == 0:
            return pl.BlockSpec((None, tn, k), lambda i, j: (r.layer, j + r.col0 // tn, 0))
        assert r.col0 % SUBLANES == 0
        return pl.BlockSpec((pl.Element(1), pl.Element(tn), pl.Element(k)),
                            lambda i, j: (r.layer, (r.col0 // SUBLANES + j * (tn // SUBLANES))
                                          * SUBLANES, 0))
    assert r.col0 % tn == 0
    off = r.col0 // tn
    if r.layer is None:
        return pl.BlockSpec((r.array.shape[0], tn), lambda i, j: (0, j + off))
    return pl.BlockSpec((None, r.array.shape[1], tn), lambda i, j: (r.layer, 0, j + off))


class SideJob(NamedTuple):
    body: Callable
    arrays: tuple
    in_specs: tuple
    out_specs: tuple
    out_shape: tuple


def _cast_body(x_ref, o_ref):
    x = x_ref[0] if len(x_ref.shape) == 3 else x_ref[...]
    o_ref[...] = x.astype(o_ref.dtype)


def _cast_side_job(w_stack, layer, n_steps, row0=0, rows=None):
    rows = w_stack.shape[1] if rows is None else rows
    cols = w_stack.shape[2]
    slab = rows // n_steps
    assert slab * n_steps == rows and slab % (2 * SUBLANES) == 0
    if row0 % slab == 0:
        in_spec = pl.BlockSpec((None, slab, cols), lambda s: (layer, s + row0 // slab, 0))
    else:
        assert row0 % SUBLANES == 0
        in_spec = pl.BlockSpec(
            (pl.Element(1), pl.Element(slab), pl.Element(cols)),
            lambda s: (layer, (row0 // SUBLANES + s * (slab // SUBLANES)) * SUBLANES, 0))
    return SideJob(_cast_body, (w_stack,), (in_spec,),
                   (pl.BlockSpec((slab, cols), lambda s: (s, 0)),),
                   (jax.ShapeDtypeStruct((rows, cols), _BF16),))


def _merge_side_jobs(*jobs):
    n_in = [len(j.in_specs) for j in jobs]
    n_out = [len(j.out_specs) for j in jobs]

    def body(*refs):
        ins, outs = refs[:sum(n_in)], refs[sum(n_in):]
        i = o = 0
        for j, ni, no in zip(jobs, n_in, n_out):
            j.body(*ins[i:i + ni], *outs[o:o + no])
            i, o = i + ni, o + no

    return SideJob(body, sum((j.arrays for j in jobs), ()), sum((j.in_specs for j in jobs), ()),
                   sum((j.out_specs for j in jobs), ()), sum((j.out_shape for j in jobs), ()))


def _mm_kernel(*refs, n_lhs, pair_lhs, rhs_transposed, n_extra, epilogue, side_body, n_side_in,
               row_splits):
    n_rhs = len(pair_lhs)
    lhs_refs = refs[:n_lhs]
    rhs_refs = refs[n_lhs:n_lhs + n_rhs]
    n_in = n_lhs + n_rhs + n_extra
    extra_refs = refs[n_lhs + n_rhs:n_in]
    o_ref = refs[n_in + n_side_in]
    if side_body is not None:
        side_body(*refs[n_in:n_in + n_side_in], *refs[n_in + n_side_in + 1:])
    weights = [(r[0] if len(r.shape) == 3 else r[...]).astype(_BF16) for r in rhs_refs]
    rows_per_split = o_ref.shape[0] // row_splits
    for s in range(row_splits):
        rows = slice(s * rows_per_split, (s + 1) * rows_per_split)
        accs = [lax.dot_general(lhs_refs[li][rows, :], w, _NT_DIMS if tr else _NN_DIMS,
                                preferred_element_type=_F32)
                for li, w, tr in zip(pair_lhs, weights, rhs_transposed)]
        o_ref[rows, :] = epilogue(accs, [e[rows, :] for e in extra_refs]).astype(o_ref.dtype)


def _fused_matmul(name, lhs, pairs, extras, epilogue, n_out, out_dtype, side_job=None):
    tm, tn, prefetch_lhs, row_splits = TILES[name]
    m = lhs[0].shape[0]
    assert m % tm == 0 and n_out % tn == 0
    grid = (m // tm, n_out // tn)
    lhs_mode = {} if prefetch_lhs else {"pipeline_mode": pl.Buffered(1)}
    in_specs = [pl.BlockSpec((tm, a.shape[1]), lambda i, j: (i, 0), **lhs_mode) for a in lhs]
    in_specs += [_rhs_spec(r, tn) for _, r in pairs]
    in_specs += [pl.BlockSpec((tm, tn), lambda i, j: (i, j)) for _ in extras]
    out_specs = [pl.BlockSpec((tm, tn), lambda i, j: (i, j))]
    out_shape = [jax.ShapeDtypeStruct((m, n_out), out_dtype)]
    operands = list(lhs) + [p[1].array for p in pairs] + list(extras)
    side = None if side_job is None else side_job(grid[0] * grid[1])
    if side is not None:
        def per_step(spec):
            return pl.BlockSpec(spec.block_shape,
                                lambda i, j: spec.index_map(i * grid[1] + j))
        in_specs += [per_step(s) for s in side.in_specs]
        out_specs += [per_step(s) for s in side.out_specs]
        out_shape += list(side.out_shape)
        operands += list(side.arrays)
    kern = functools.partial(_mm_kernel, n_lhs=len(lhs), pair_lhs=tuple(p[0] for p in pairs),
                             rhs_transposed=tuple(p[1].transposed for p in pairs),
                             n_extra=len(extras), epilogue=epilogue,
                             side_body=None if side is None else side.body,
                             n_side_in=0 if side is None else len(side.in_specs),
                             row_splits=row_splits)
    outs = pl.pallas_call(
        kern,
        grid=grid,
        in_specs=in_specs,
        out_specs=out_specs,
        out_shape=out_shape,
        compiler_params=_compiler_params(2),
        name=name,
    )(*operands)
    return outs[0] if side is None else tuple(outs)


def _gating_kernel(za_ref, ng_ref, w_ref, bt_ref, o_ref):
    row = lax.broadcasted_iota(jnp.int32, (CHUNK, CHUNK), 0)
    col = lax.broadcasted_iota(jnp.int32, (CHUNK, CHUNK), 1)
    causal = col <= row
    for c in range(za_ref.shape[0] // CHUNK):
        rows = slice(c * CHUNK, (c + 1) * CHUNK)
        z = jax.nn.gelu(za_ref[rows, :])
        u = z[:, :A_WIDTH]
        v = z[:, A_WIDTH:]
        ms = jnp.mean(v * v, axis=-1, keepdims=True)
        vn = ((v * lax.rsqrt(ms + EPS)) * ng_ref[...]).astype(_BF16)
        for g in range(A_GROUPS):
            sl = slice(g * A_GROUP_DIM, (g + 1) * A_GROUP_DIM)
            w = jnp.where(causal, w_ref[g], 0.0).astype(_BF16)
            s = jnp.dot(w, vn[:, sl], preferred_element_type=_F32) + bt_ref[:, g:g + 1]
            o_ref[rows, sl] = (u[:, sl] * s).astype(o_ref.dtype)


def _spatial_gating(za, norm_g, w_s, b_s):
    m = za.shape[0]
    tm = ROW_BLOCK["spatial_gating"]
    return pl.pallas_call(
        _gating_kernel,
        grid=(m // tm,),
        in_specs=[pl.BlockSpec((tm, 2 * A_WIDTH), lambda i: (i, 0)),
                  pl.BlockSpec((1, A_WIDTH), lambda i: (0, 0)),
                  pl.BlockSpec((A_GROUPS, CHUNK, CHUNK), lambda i: (0, 0, 0)),
                  pl.BlockSpec((CHUNK, A_GROUPS), lambda i: (0, 0))],
        out_specs=pl.BlockSpec((tm, A_WIDTH), lambda i: (i, 0)),
        out_shape=jax.ShapeDtypeStruct((m, A_WIDTH), _BF16),
        compiler_params=_compiler_params(1),
        name="spatial_gating",
    )(za, norm_g.reshape(1, A_WIDTH), w_s, b_s.T)


def _lane_inv_freq(rot_dim, period):
    inv_freq = ROPE_THETA ** (-jnp.arange(0, rot_dim, 2, dtype=_F32) / rot_dim)
    head = jnp.concatenate([inv_freq, inv_freq, jnp.zeros((period - rot_dim,), _F32)])
    return jnp.tile(head, LANES // period).reshape(1, LANES)


def _rope_tables(pos, inv_freq_lanes, rot_dim, period):
    half = rot_dim // 2
    ang = pos * inv_freq_lanes
    cos, sin = jnp.cos(ang), jnp.sin(ang)
    lane = lax.broadcasted_iota(jnp.int32, ang.shape, 1) & (period - 1)
    c = jnp.where(lane < rot_dim, cos, 1.0)
    s1 = jnp.where(lane >= half, jnp.where(lane < rot_dim, sin, 0.0), 0.0)
    s2 = jnp.where(lane < half, -sin, 0.0)
    return c, s1, s2


def _rope(x, c, s1, s2, half):
    return x * c + pltpu.roll(x, half, 1) * s1 + pltpu.roll(x, LANES - half, 1) * s2


def _prep_kernel(att_ref, pos_ref, fb_ref, fi_ref,
                 q_ref, k_ref, vt_ref, qi_ref, ki_ref, wit_ref):
    pos = pos_ref[0]
    cb, s1b, s2b = _rope_tables(pos, fb_ref[...], B_ROT, B_HEAD_DIM)
    ci, s1i, s2i = _rope_tables(pos, fi_ref[...], IDX_ROT, IDX_DIM)
    for h in range(B_HEADS):
        x = att_ref[0, :, ATT_Q0 + h * LANES:ATT_Q0 + (h + 1) * LANES]
        q_ref[0, :, h * LANES:(h + 1) * LANES] = _rope(x, cb, s1b, s2b, B_ROT // 2).astype(_BF16)
    for h in range(B_KV_HEADS):
        x = att_ref[0, :, ATT_K0 + h * LANES:ATT_K0 + (h + 1) * LANES]
        k_ref[0, :, h * LANES:(h + 1) * LANES] = _rope(x, cb, s1b, s2b, B_ROT // 2).astype(_BF16)
    vt_ref[0, 0] = att_ref[0, :, ATT_V0:ATT_V0 + B_KV_WIDTH].T.astype(_BF16)
    lane = lax.broadcasted_iota(jnp.int32, ci.shape, 1)
    is_ki = lane < IDX_DIM
    heads_per_group = LANES // IDX_DIM
    for j in range(IDX_WIDTH // LANES):
        x = att_ref[0, :, ATT_QI0 + j * LANES:ATT_QI0 + (j + 1) * LANES]
        y = _rope(x, ci, s1i, s2i, IDX_ROT // 2)
        for r in range(heads_per_group):
            h = j * heads_per_group + r
            yr = y if r == 0 else pltpu.roll(y, LANES - r * IDX_DIM, 1)
            qi_ref[0, :, h * LANES:(h + 1) * LANES] = jnp.where(is_ki, yr, 0.0).astype(_BF16)
    x = att_ref[0, :, ATT_KW0:ATT_KW0 + LANES]
    y = _rope(x, jnp.where(is_ki, ci, 1.0), jnp.where(is_ki, s1i, 0.0),
              jnp.where(is_ki, s2i, 0.0), IDX_ROT // 2)
    ki_ref[0] = jnp.where(is_ki, y, 0.0).astype(_BF16)
    wit_ref[0] = y.T[IDX_DIM:IDX_DIM + IDX_HEADS, :]


def _prep_side_job(att, positions, n_steps):
    bsz, seq, _ = att.shape
    ts = bsz * seq // n_steps
    assert ts * n_steps == bsz * seq and seq % ts == 0 and KEY_CHUNK % ts == 0
    assert ts % (2 * SUBLANES) == 0
    per_seq = seq // ts
    per_chunk = KEY_CHUNK // ts

    def tokens(width):
        return pl.BlockSpec((1, ts, width), lambda s: (s // per_seq, s % per_seq, 0))

    def out(width, dtype):
        return tokens(width), jax.ShapeDtypeStruct((bsz, seq, width), dtype)

    vt_out = (pl.BlockSpec((1, 1, B_KV_WIDTH, ts),
                           lambda s: (s // per_seq, (s % per_seq) // per_chunk, 0,
                                      (s % per_seq) % per_chunk)),
              jax.ShapeDtypeStruct((bsz, seq // KEY_CHUNK, B_KV_WIDTH, KEY_CHUNK), _BF16))
    wit_out = (pl.BlockSpec((1, IDX_HEADS, ts), lambda s: (s // per_seq, 0, s % per_seq)),
               jax.ShapeDtypeStruct((bsz, IDX_HEADS, seq), _F32))
    outs = [out(B_WIDTH, _BF16), out(B_KV_WIDTH, _BF16), vt_out,
            out(IDX_HEADS * LANES, _BF16), out(LANES, _BF16), wit_out]
    freq_spec = pl.BlockSpec((1, LANES), lambda s: (0, 0))
    return SideJob(_prep_kernel,
                   (att, positions.astype(_F32).reshape(bsz, seq, 1),
                    _lane_inv_freq(B_ROT, B_HEAD_DIM), _lane_inv_freq(IDX_ROT, IDX_DIM)),
                   (tokens(ATT_WIDTH), tokens(1), freq_spec, freq_spec),
                   tuple(o[0] for o in outs), tuple(o[1] for o in outs))


def _ordered_int_to_float(key):
    bits = jnp.where(key < 0, key ^ jnp.int32(0x7FFFFFFF), key)
    return lax.bitcast_convert_type(bits, _F32)


def _fold_rows(x, op):
    while x.shape[0] > SUBLANES:
        half = x.shape[0] // 2
        x = op(x[:half], x[half:])
    return x


_DSA_N_IN = 6


def _dsa_kernel(*refs, n_sel, side_body, n_side_in, n_side_out):
    o_at = _DSA_N_IN + n_side_in
    side = None
    if side_body is not None:
        side = functools.partial(side_body, *refs[_DSA_N_IN:o_at],
                                 *refs[o_at + 1:o_at + 1 + n_side_out])
    own = refs[:_DSA_N_IN] + (refs[o_at],) + refs[o_at + 1 + n_side_out:]
    seq = own[4].shape[1]
    n_chunks = (pl.program_id(1) * Q_BLOCK + Q_BLOCK + KEY_CHUNK - 1) // KEY_CHUNK
    for n in range(1, seq // KEY_CHUNK + 1):
        pl.when(n_chunks == n)(functools.partial(_dsa_block, *own, n_sel=n_sel, n_chunks=n,
                                                 side=side))


def _dsa_block(qi_ref, wit_ref, ki_ref, q_ref, k_ref, vt_ref, o_ref,
               score_ref, bias_ref, logit_ref, acc_ref, *, n_sel, n_chunks, side):
    if side is not None:
        side()
    t = Q_BLOCK
    blk = pl.program_id(1)
    qpos = blk * t + lax.broadcasted_iota(jnp.int32, (1, t), 1)
    kiota = lax.broadcasted_iota(jnp.int32, (KEY_CHUNK, 1), 0)
    idx_scale = (IDX_DIM ** -0.5) * (IDX_HEADS ** -0.5)

    def rows(c):
        return slice(c * KEY_CHUNK, (c + 1) * KEY_CHUNK)

    def for_chunks(body, carry):
        for c in range(n_chunks):
            carry = body(c, carry)
        return carry

    def indexer_chunk(c, carry):
        ki = ki_ref[0, rows(c), :]
        acc = jnp.zeros((KEY_CHUNK, t), _F32)
        for h in range(0, IDX_HEADS, 2):
            qpair = jnp.concatenate([qi_ref[0, :, h * LANES:(h + 1) * LANES],
                                     qi_ref[0, :, (h + 1) * LANES:(h + 2) * LANES]], axis=0)
            d = lax.dot_general(ki, qpair, _NT_DIMS, preferred_element_type=_F32)
            acc = acc + wit_ref[0, h:h + 1, :] * jnp.maximum(d[:, :t], 0.0)
            acc = acc + wit_ref[0, h + 1:h + 2, :] * jnp.maximum(d[:, t:], 0.0)
        kpos = c * KEY_CHUNK + kiota
        score_ref[rows(c), :] = jnp.where(kpos <= qpos, acc * idx_scale + 0.0, _NEG_INF)
        return carry

    for_chunks(indexer_chunk, 0)

    def count(indicator):
        def chunk(c, part):
            return part + _fold_rows(indicator(score_ref[rows(c), :], c), jnp.add)
        part = for_chunks(chunk, jnp.zeros((SUBLANES, t), _F32))
        return jnp.sum(part, axis=0, keepdims=True)

    def count_ge(cand_f):
        return count(lambda s, c: jnp.where(s >= cand_f, 1.0, 0.0))

    def search():
        int_min = jnp.int32(-2 ** 31)
        base = jnp.where(count_ge(jnp.zeros((1, t), _F32)) >= n_sel, jnp.int32(0), int_min)

        def search_step(it, base):
            cand = base | lax.shift_left(jnp.int32(1), jnp.int32(30) - it)
            cnt = count_ge(_ordered_int_to_float(cand))
            return jnp.where(cnt >= n_sel, cand, base)

        return lax.fori_loop(0, 31, search_step, base)

    if (n_chunks - 1) * KEY_CHUNK >= n_sel:
        base = search()
    else:
        base = lax.cond((blk + 1) * t <= n_sel, lambda: jnp.zeros((1, t), jnp.int32), search)
    thr = jnp.where(qpos + 1 <= n_sel, _NEG_INF, _ordered_int_to_float(base))

    def bias_chunk(c, part):
        s = score_ref[rows(c), :]
        sel = jnp.where(c * KEY_CHUNK + kiota <= qpos, jnp.where(s >= thr, 1.0, 0.0), 0.0)
        bias_ref[rows(c), :] = jnp.where(sel > 0.0, 0.0, _NEG_INF)
        return part + _fold_rows(sel, jnp.add)

    n_ge = jnp.sum(for_chunks(bias_chunk, jnp.zeros((SUBLANES, t), _F32)), axis=0, keepdims=True)

    @pl.when(jnp.max(n_ge) > n_sel)
    def _():
        nxt = _ordered_int_to_float(base + 1)
        tied = n_ge > n_sel
        n_above = count(lambda s, c: jnp.where(s >= nxt, 1.0, 0.0))
        need = jnp.where(tied, n_sel - n_above, 0.0)
        no_index = float(k_ref.shape[1])
        front_v = jnp.where(tied, jnp.inf, _NEG_INF)
        front_i = jnp.where(tied, -1.0, no_index)

        def beyond(s, kposf, front_v, front_i):
            after = jnp.where(s < front_v, 1.0,
                              jnp.where(s == front_v, jnp.where(kposf > front_i, 1.0, 0.0), 0.0))
            return jnp.where(s >= thr, jnp.where(s < nxt, after, 0.0), 0.0)

        def kposf(c):
            return (c * KEY_CHUNK + kiota).astype(_F32)

        def advance(_, carry):
            front_v, front_i, need = carry

            def best_value(c, part):
                s = score_ref[rows(c), :]
                cand = jnp.where(beyond(s, kposf(c), front_v, front_i) > 0.0, s, _NEG_INF)
                return jnp.maximum(part, _fold_rows(cand, jnp.maximum))

            v = jnp.max(for_chunks(best_value, jnp.full((SUBLANES, t), _NEG_INF, _F32)),
                        axis=0, keepdims=True)

            def first_index(c, part):
                s = score_ref[rows(c), :]
                hit = jnp.where(s == v, beyond(s, kposf(c), front_v, front_i), 0.0)
                cand = jnp.where(hit > 0.0, kposf(c), no_index)
                return jnp.minimum(part, _fold_rows(cand, jnp.minimum))

            i = jnp.min(for_chunks(first_index, jnp.full((SUBLANES, t), no_index, _F32)),
                        axis=0, keepdims=True)
            active = need > 0.0
            return (jnp.where(active, v, front_v), jnp.where(active, i, front_i),
                    jnp.where(active, need - 1.0, need))

        front_v, front_i, _ = lax.fori_loop(0, jnp.max(need).astype(jnp.int32), advance,
                                            (front_v, front_i, need))

        def tie_chunk(c, carry):
            s = score_ref[rows(c), :]
            upto = jnp.where(s > front_v, 0.0,
                             jnp.where(s == front_v,
                                       jnp.where(kposf(c) <= front_i, 0.0, _NEG_INF), _NEG_INF))
            keep = jnp.where(s >= nxt, 0.0, jnp.where(s >= thr, upto, _NEG_INF))
            bias_ref[rows(c), :] = jnp.where(c * KEY_CHUNK + kiota <= qpos, keep, _NEG_INF)
            return carry

        for_chunks(tie_chunk, 0)

    grp = B_HEADS // B_KV_HEADS
    gw = grp * t
    exp2_scale = (B_HEAD_DIM ** -0.5) * 1.4426950408889634

    def logit_chunk(c, mx):
        bias = jnp.concatenate([bias_ref[rows(c), :]] * grp, axis=1)
        folded = []
        for g in range(B_KV_HEADS):
            qg = jnp.concatenate(
                [q_ref[0, :, (g * grp + hh) * B_HEAD_DIM:(g * grp + hh + 1) * B_HEAD_DIM]
                 for hh in range(grp)], axis=0)
            l = lax.dot_general(k_ref[0, rows(c), g * B_HEAD_DIM:(g + 1) * B_HEAD_DIM], qg,
                                _NT_DIMS, preferred_element_type=_F32) + bias
            logit_ref[rows(c), g * gw:(g + 1) * gw] = l
            folded.append(_fold_rows(l, jnp.maximum))
        return jnp.maximum(mx, jnp.concatenate(folded, axis=1))

    mx = for_chunks(logit_chunk, jnp.full((SUBLANES, B_HEADS * t), _NEG_INF, _F32))
    mx = jnp.max(mx, axis=0, keepdims=True)

    acc_ref[...] = jnp.zeros_like(acc_ref)
    ones_rows = jnp.ones((acc_ref.shape[0] - B_HEAD_DIM, KEY_CHUNK), _BF16)

    def pv_chunk(c, carry):
        for g in range(B_KV_HEADS):
            cols = slice(g * gw, (g + 1) * gw)
            p = jnp.exp2((logit_ref[rows(c), cols] - mx[:, cols]) * exp2_scale)
            vt_ext = jnp.concatenate(
                [vt_ref[0, c, g * B_HEAD_DIM:(g + 1) * B_HEAD_DIM, :], ones_rows], axis=0)
            acc_ref[:, cols] += jnp.dot(vt_ext, p.astype(_BF16), preferred_element_type=_F32)
        return carry

    for_chunks(pv_chunk, 0)
    out_t = acc_ref[:B_HEAD_DIM, :] / acc_ref[B_HEAD_DIM:B_HEAD_DIM + 1, :]
    for h in range(B_HEADS):
        o_ref[0, :, h * B_HEAD_DIM:(h + 1) * B_HEAD_DIM] = (
            out_t[:, h * t:(h + 1) * t].T.astype(o_ref.dtype))


def _dsa_attention(q, k, vt, qi, ki, wit, side_job=None):
    bsz, seq, _ = q.shape
    assert Q_BLOCK == LANES and seq % KEY_CHUNK == 0
    n_sel = min(TOPK_MAX, seq // 4)
    n_blk = seq // Q_BLOCK

    def q_spec(width):
        return pl.BlockSpec((1, Q_BLOCK, width), lambda b, i: (b, i, 0))

    def kv_spec(width):
        return pl.BlockSpec((1, seq, width), lambda b, i: (b, 0, 0))

    def per_step(spec):
        return pl.BlockSpec(spec.block_shape, lambda b, i: spec.index_map(b * n_blk + i))

    side = None if side_job is None else side_job(bsz * n_blk)
    side_in = [] if side is None else [per_step(s) for s in side.in_specs]
    side_out = [] if side is None else [per_step(s) for s in side.out_specs]
    outs = pl.pallas_call(
        functools.partial(_dsa_kernel, n_sel=n_sel,
                          side_body=None if side is None else side.body,
                          n_side_in=len(side_in), n_side_out=len(side_out)),
        grid=(bsz, n_blk),
        in_specs=[q_spec(IDX_HEADS * LANES),
                  pl.BlockSpec((1, IDX_HEADS, Q_BLOCK), lambda b, i: (b, 0, i)),
                  kv_spec(LANES), q_spec(B_WIDTH), kv_spec(B_KV_WIDTH),
                  pl.BlockSpec((1, seq // KEY_CHUNK, B_KV_WIDTH, KEY_CHUNK),
                               lambda b, i: (b, 0, 0, 0))] + side_in,
        out_specs=[q_spec(B_WIDTH)] + side_out,
        out_shape=[jax.ShapeDtypeStruct((bsz, seq, B_WIDTH), _BF16)]
        + ([] if side is None else list(side.out_shape)),
        scratch_shapes=[pltpu.VMEM((seq, Q_BLOCK), _F32), pltpu.VMEM((seq, Q_BLOCK), _F32),
                        pltpu.VMEM((seq, B_HEADS * Q_BLOCK), _F32),
                        pltpu.VMEM((B_HEAD_DIM + 2 * SUBLANES, B_HEADS * Q_BLOCK), _F32)],
        compiler_params=_compiler_params(2),
        name="dsa_attention",
    )(qi, wit, ki, q, k, vt, *(() if side is None else side.arrays))
    return outs[0] if side is None else tuple(outs)


def _xattn_kernel(q_ref, k_ref, v_ref, o_ref):
    scale = X_HEAD_DIM ** -0.5
    for h in range(X_HEADS):
        hs = slice(h * X_HEAD_DIM, (h + 1) * X_HEAD_DIM)
        logits = lax.dot_general(q_ref[0, :, hs], k_ref[0, :, hs], _NT_DIMS,
                                 preferred_element_type=_F32) * scale
        mx = jnp.max(logits, axis=-1, keepdims=True)
        p = jnp.exp(logits - mx)
        denom = jnp.sum(p, axis=-1, keepdims=True)
        o = jnp.dot(p.astype(_BF16), v_ref[0, :, hs], preferred_element_type=_F32)
        o_ref[0, :, hs] = (o / denom).astype(o_ref.dtype)


def _cross_attention(q, k, v):
    bsz, seq, _ = q.shape
    tq = ROW_BLOCK["cross_attention"]
    m = k.shape[1]
    mem_spec = pl.BlockSpec((1, m, X_WIDTH), lambda b, i: (b, 0, 0))
    return pl.pallas_call(
        _xattn_kernel,
        grid=(bsz, seq // tq),
        in_specs=[pl.BlockSpec((1, tq, X_WIDTH), lambda b, i: (b, i, 0)), mem_spec, mem_spec],
        out_specs=pl.BlockSpec((1, tq, X_WIDTH), lambda b, i: (b, i, 0)),
        out_shape=jax.ShapeDtypeStruct((bsz, seq, X_WIDTH), _BF16),
        compiler_params=_compiler_params(2),
        name="cross_attention",
    )(q, k, v)


def _norm_proj_kernel(x_ref, g_ref, w_ref, o_ref):
    x = x_ref[...]
    ms = jnp.mean(x * x, axis=-1, keepdims=True)
    h = ((x * lax.rsqrt(ms + EPS)) * g_ref[...]).astype(_BF16)
    o_ref[...] = jnp.dot(h, w_ref[...], preferred_element_type=_F32).astype(o_ref.dtype)


def _norm_proj(x, g, w_bf16, out_dtype, name):
    m, d = x.shape
    tm = ROW_BLOCK["norm_proj"]
    n = w_bf16.shape[1]
    return pl.pallas_call(
        _norm_proj_kernel,
        grid=(m // tm,),
        in_specs=[pl.BlockSpec((tm, d), lambda i: (i, 0)),
                  pl.BlockSpec((1, d), lambda i: (0, 0)),
                  pl.BlockSpec((d, n), lambda i: (0, 0), pipeline_mode=pl.Buffered(1))],
        out_specs=pl.BlockSpec((tm, n), lambda i: (i, 0)),
        out_shape=jax.ShapeDtypeStruct((m, n), out_dtype),
        compiler_params=_compiler_params(1),
        name=name,
    )(x, g.reshape(1, d), w_bf16)


def _xattn_out_kernel(o_ref, w_ref, x_ref, g_ref, x_out_ref, h_out_ref):
    x = x_ref[...] + jnp.dot(o_ref[...], w_ref[...], preferred_element_type=_F32)
    x_out_ref[...] = x
    ms = jnp.mean(x * x, axis=-1, keepdims=True)
    h_out_ref[...] = ((x * lax.rsqrt(ms + EPS)) * g_ref[...]).astype(h_out_ref.dtype)


def _xattn_out_and_norm(o, w_bf16, x, g):
    m, k = o.shape
    tm = ROW_BLOCK["xattn_out_norm"]
    d = x.shape[1]
    row_spec = pl.BlockSpec((tm, d), lambda i: (i, 0))
    return pl.pallas_call(
        _xattn_out_kernel,
        grid=(m // tm,),
        in_specs=[pl.BlockSpec((tm, k), lambda i: (i, 0)),
                  pl.BlockSpec((k, d), lambda i: (0, 0), pipeline_mode=pl.Buffered(1)),
                  row_spec,
                  pl.BlockSpec((1, d), lambda i: (0, 0))],
        out_specs=[row_spec, row_spec],
        out_shape=[jax.ShapeDtypeStruct((m, d), _F32), jax.ShapeDtypeStruct((m, d), _BF16)],
        compiler_params=_compiler_params(1),
        name="xattn_out_norm",
    )(o, w_bf16, x, g.reshape(1, d))


def kernel(x, mem, positions, norm_mix_g, w_in, a_norm_g, a_spatial_w, a_spatial_b, p_a, p_b,
           w_out, norm_x_g, norm_mem_g, xq_w, xk_w, xv_w, xo_w, norm_ffn_g, ffn_w1, ffn_w3,
           ffn_w2, final_norm_g):
    bsz, seq, d = x.shape
    m = bsz * seq
    depth = w_in.shape[0]
    ffn_hidden = ffn_w1.shape[-1]

    xf = x.reshape(m, d)
    mem_f = mem.reshape(bsz * MEM_LEN, d)
    w_in_t = jnp.transpose(w_in, (0, 2, 1))
    for l in range(depth):
        def w_in_cols(col0):
            return Rhs(w_in_t, l, col0, transposed=True)

        h = _rmsnorm(xf, norm_mix_g[l], _BF16)
        att = _fused_matmul("proj_att", [h], [(0, w_in_cols(ZA_END))], [], _ep_identity,
                            ATT_WIDTH, _F32)
        za, q, k, vt, qi, ki, wit = _fused_matmul(
            "proj_za", [h], [(0, w_in_cols(0))], [], _ep_identity, ZA_END, _F32,
            side_job=functools.partial(_prep_side_job, att.reshape(bsz, seq, ATT_WIDTH),
                                       positions))
        y_a = _spatial_gating(za, a_norm_g[l], a_spatial_w[l], a_spatial_b[l])
        def weight_casts(n_steps):
            return _merge_side_jobs(
                _cast_side_job(p_a, l, n_steps), _cast_side_job(p_b, l, n_steps),
                _cast_side_job(w_in_t, l, n_steps, row0=ATT_END, rows=2 * d),
                _cast_side_job(w_out, l, n_steps), _cast_side_job(xq_w, l, n_steps),
                _cast_side_job(xk_w, l, n_steps), _cast_side_job(xv_w, l, n_steps))

        y_b, pa_bf16, pb_bf16, gate_wt_bf16, wo_bf16, xq_bf16, xk_bf16, xv_bf16 = _dsa_attention(
            q, k, vt, qi, ki, wit, side_job=weight_casts)

        def ffn_up_weight_casts(n_steps):
            return _merge_side_jobs(_cast_side_job(ffn_w1, l, n_steps),
                                    _cast_side_job(ffn_w3, l, n_steps))

        merged, w1_bf16, w3_bf16 = _fused_matmul(
            "gated_merge", [y_a, y_b.reshape(m, B_WIDTH), h],
            [(0, Rhs(pa_bf16)), (1, Rhs(pb_bf16)), (2, Rhs(gate_wt_bf16, transposed=True)),
             (2, Rhs(gate_wt_bf16, col0=d, transposed=True))],
            [], _ep_gated_merge, d, _BF16, side_job=ffn_up_weight_casts)
        x1, xo_bf16 = _fused_matmul("out_proj", [merged], [(0, Rhs(wo_bf16))], [xf],
                                    _ep_residual, d, _F32,
                                    side_job=functools.partial(_cast_side_job, xo_w, l))

        qx = _norm_proj(x1, norm_x_g[l], xq_bf16, _BF16, "xattn_q")
        kx = _norm_proj(mem_f, norm_mem_g[l], xk_bf16, _BF16, "xattn_k")
        vx = _norm_proj(mem_f, norm_mem_g[l], xv_bf16, _BF16, "xattn_v")
        ox = _cross_attention(qx.reshape(bsz, seq, X_WIDTH),
                              kx.reshape(bsz, MEM_LEN, X_WIDTH),
                              vx.reshape(bsz, MEM_LEN, X_WIDTH)).reshape(m, X_WIDTH)
        x2, h2 = _xattn_out_and_norm(ox, xo_bf16, x1, norm_ffn_g[l])

        act, w2_bf16 = _fused_matmul("ffn_up", [h2],
                                     [(0, Rhs(w1_bf16)), (0, Rhs(w3_bf16))],
                                     [], _ep_swiglu, ffn_hidden, _BF16,
                                     side_job=functools.partial(_cast_side_job, ffn_w2, l))
        xf = _fused_matmul("ffn_down", [act], [(0, Rhs(w2_bf16))], [x2], _ep_residual, d, _F32)

    return _rmsnorm(xf, final_norm_g, _F32).reshape(bsz, seq, d)
```

```python
import functools
from typing import Callable, NamedTuple, Optional

import jax
import jax.numpy as jnp
from jax import lax
from jax.experimental import pallas as pl
from jax.experimental.pallas import tpu as pltpu

D_MODEL = 4096
MEM_LEN = 256
EPS = 1e-6
ROPE_THETA = 500000.0
CHUNK = 128
A_GROUPS = 16
A_WIDTH = D_MODEL // 2
A_GROUP_DIM = A_WIDTH // A_GROUPS
B_HEADS = 16
B_HEAD_DIM = 128
B_KV_HEADS = 4
B_WIDTH = B_HEADS * B_HEAD_DIM
B_KV_WIDTH = B_KV_HEADS * B_HEAD_DIM
B_ROT = B_HEAD_DIM // 4
IDX_HEADS = 16
IDX_DIM = 64
IDX_WIDTH = IDX_HEADS * IDX_DIM
IDX_ROT = IDX_DIM // 4
TOPK_MAX = 256
Q_BLOCK = 128
X_HEADS = 4
X_HEAD_DIM = 256
X_WIDTH = X_HEADS * X_HEAD_DIM
IN_SIZES = (2 * A_WIDTH, B_WIDTH, B_KV_WIDTH, B_KV_WIDTH, IDX_WIDTH, IDX_DIM, IDX_HEADS,
            2 * D_MODEL)

LANES = 128
SUBLANES = 8
KEY_CHUNK = 512
PROJ_TN = 512
ZA_END = IN_SIZES[0]
ATT_END = ZA_END + sum(IN_SIZES[1:7])
ATT_Q0 = 0
ATT_K0 = ATT_Q0 + B_WIDTH
ATT_V0 = ATT_K0 + B_KV_WIDTH
ATT_QI0 = ATT_V0 + B_KV_WIDTH
ATT_KW0 = ATT_QI0 + IDX_WIDTH
ATT_WIDTH = -(-(ATT_END - ZA_END) // PROJ_TN) * PROJ_TN
assert ZA_END % PROJ_TN == 0 and ATT_KW0 + LANES <= ATT_WIDTH

VMEM_LIMIT_BYTES = 60 * 1024 * 1024


class Tile(NamedTuple):
    tm: int
    tn: int
    row_splits: int = 1


TILES = {
    "proj_att": Tile(1024, PROJ_TN),
    "proj_za": Tile(1024, 512),
    "gated_merge": Tile(1024, 256, row_splits=4),
    "out_proj": Tile(1024, 512),
    "ffn_up": Tile(2048, 256, row_splits=4),
    "ffn_down": Tile(512, 512),
}
ROW_BLOCK = {
    "rmsnorm": 512,
    "spatial_gating": 4 * CHUNK,
    "cross_attention": 1024,
    "norm_proj": 512,
    "xattn_out_norm": 512,
}

_BF16 = jnp.bfloat16
_F32 = jnp.float32
_NEG_INF = float("-inf")
_NN_DIMS = (((1,), (0,)), ((), ()))
_NT_DIMS = (((1,), (1,)), ((), ()))


def _compiler_params(n_axes):
    return pltpu.CompilerParams(
        dimension_semantics=("arbitrary",) * n_axes,
        vmem_limit_bytes=VMEM_LIMIT_BYTES,
    )


def _rmsnorm_kernel(x_ref, g_ref, o_ref):
    x = x_ref[...]
    ms = jnp.mean(x * x, axis=-1, keepdims=True)
    y = x * lax.rsqrt(ms + EPS)
    o_ref[...] = (y * g_ref[...]).astype(o_ref.dtype)


def _rmsnorm(x, g, out_dtype):
    m, d = x.shape
    tm = ROW_BLOCK["rmsnorm"]
    return pl.pallas_call(
        _rmsnorm_kernel,
        grid=(m // tm,),
        in_specs=[pl.BlockSpec((tm, d), lambda i: (i, 0)),
                  pl.BlockSpec((1, d), lambda i: (0, 0))],
        out_specs=pl.BlockSpec((tm, d), lambda i: (i, 0)),
        out_shape=jax.ShapeDtypeStruct((m, d), out_dtype),
        compiler_params=_compiler_params(1),
        name="rmsnorm",
    )(x, g.reshape(1, d))


def _ep_identity(accs, extras):
    return accs[0]


def _ep_residual(accs, extras):
    return extras[0] + accs[0]


def _ep_gated_merge(accs, extras):
    y_a, y_b, gate_a, gate_b = accs
    return jax.nn.sigmoid(gate_a) * y_a + jax.nn.sigmoid(gate_b) * y_b


def _ep_swiglu(accs, extras):
    return jax.nn.silu(accs[0]) * accs[1]


class Rhs(NamedTuple):
    array: jax.Array
    layer: Optional[int] = None
    col0: int = 0
    transposed: bool = False


def _rhs_spec(r, tn):
    if r.transposed and r.layer is None:
        assert r.col0 % tn == 0
        off = r.col0 // tn
        return pl.BlockSpec((tn, r.array.shape[1]), lambda i, j: (j + off, 0))
    if r.transposed:
        k = r.array.shape[2]
        if r.col0 % tn == 0:
            return pl.BlockSpec((None, tn, k), lambda i, j: (r.layer, j + r.col0 // tn, 0))
        assert r.col0 % SUBLANES == 0
        return pl.BlockSpec((pl.Element(1), pl.Element(tn), pl.Element(k)),
                            lambda i, j: (r.layer, (r.col0 // SUBLANES + j * (tn // SUBLANES))
                                          * SUBLANES, 0))
    assert r.col0 % tn == 0
    off = r.col0 // tn
    if r.layer is None:
        return pl.BlockSpec((r.array.shape[0], tn), lambda i, j: (0, j + off))
    return pl.BlockSpec((None, r.array.shape[1], tn), lambda i, j: (r.layer, 0, j + off))


class SideJob(NamedTuple):
    body: Callable
    arrays: tuple
    in_specs: tuple
    out_specs: tuple
    out_shape: tuple


def _cast_body(x_ref, o_ref):
    x = x_ref[0] if len(x_ref.shape) == 3 else x_ref[...]
    o_ref[...] = x.astype(o_ref.dtype)


def _cast_side_job(w_stack, layer, n_steps, row0=0, rows=None):
    rows = w_stack.shape[1] if rows is None else rows
    cols = w_stack.shape[2]
    slab = rows // n_steps
    assert slab * n_steps == rows and slab % (2 * SUBLANES) == 0
    if row0 % slab == 0:
        in_spec = pl.BlockSpec((None, slab, cols), lambda s: (layer, s + row0 // slab, 0))
    else:
        assert row0 % SUBLANES == 0
        in_spec = pl.BlockSpec(
            (pl.Element(1), pl.Element(slab), pl.Element(cols)),
            lambda s: (layer, (row0 // SUBLANES + s * (slab // SUBLANES)) * SUBLANES, 0))
    return SideJob(_cast_body, (w_stack,), (in_spec,),
                   (pl.BlockSpec((slab, cols), lambda s: (s, 0)),),
                   (jax.ShapeDtypeStruct((rows, cols), _BF16),))


def _merge_side_jobs(*jobs):
    n_in = [len(j.in_specs) for j in jobs]
    n_out = [len(j.out_specs) for j in jobs]

    def body(*refs):
        ins, outs = refs[:sum(n_in)], refs[sum(n_in):]
        i = o = 0
        for j, ni, no in zip(jobs, n_in, n_out):
            j.body(*ins[i:i + ni], *outs[o:o + no])
            i, o = i + ni, o + no

    return SideJob(body, sum((j.arrays for j in jobs), ()), sum((j.in_specs for j in jobs), ()),
                   sum((j.out_specs for j in jobs), ()), sum((j.out_shape for j in jobs), ()))


def _mm_kernel(*refs, n_lhs, pair_lhs, rhs_transposed, n_extra, epilogue, side_body, n_side_in,
               row_splits):
    n_rhs = len(pair_lhs)
    lhs_refs = refs[:n_lhs]
    rhs_refs = refs[n_lhs:n_lhs + n_rhs]
    n_in = n_lhs + n_rhs + n_extra
    extra_refs = refs[n_lhs + n_rhs:n_in]
    o_ref = refs[n_in + n_side_in]
    if side_body is not None:
        side_body(*refs[n_in:n_in + n_side_in], *refs[n_in + n_side_in + 1:])
    weights = [(r[0] if len(r.shape) == 3 else r[...]).astype(_BF16) for r in rhs_refs]
    rows_per_split = o_ref.shape[0] // row_splits
    for s in range(row_splits):
        rows = slice(s * rows_per_split, (s + 1) * rows_per_split)
        accs = [lax.dot_general(lhs_refs[li][rows, :], w, _NT_DIMS if tr else _NN_DIMS,
                                preferred_element_type=_F32)
                for li, w, tr in zip(pair_lhs, weights, rhs_transposed)]
        o_ref[rows, :] = epilogue(accs, [e[rows, :] for e in extra_refs]).astype(o_ref.dtype)


def _fused_matmul(name, lhs, pairs, extras, epilogue, n_out, out_dtype, side_job=None):
    tm, tn, row_splits = TILES[name]
    m = lhs[0].shape[0]
    assert m % tm == 0 and n_out % tn == 0
    grid = (m // tm, n_out // tn)
    in_specs = [pl.BlockSpec((tm, a.shape[1]), lambda i, j: (i, 0)) for a in lhs]
    in_specs += [_rhs_spec(r, tn) for _, r in pairs]
    in_specs += [pl.BlockSpec((tm, tn), lambda i, j: (i, j)) for _ in extras]
    out_specs = [pl.BlockSpec((tm, tn), lambda i, j: (i, j))]
    out_shape = [jax.ShapeDtypeStruct((m, n_out), out_dtype)]
    operands = list(lhs) + [p[1].array for p in pairs] + list(extras)
    side = None if side_job is None else side_job(grid[0] * grid[1])
    if side is not None:
        def per_step(spec):
            return pl.BlockSpec(spec.block_shape,
                                lambda i, j: spec.index_map(i * grid[1] + j))
        in_specs += [per_step(s) for s in side.in_specs]
        out_specs += [per_step(s) for s in side.out_specs]
        out_shape += list(side.out_shape)
        operands += list(side.arrays)
    kern = functools.partial(_mm_kernel, n_lhs=len(lhs), pair_lhs=tuple(p[0] for p in pairs),
                             rhs_transposed=tuple(p[1].transposed for p in pairs),
                             n_extra=len(extras), epilogue=epilogue,
                             side_body=None if side is None else side.body,
                             n_side_in=0 if side is None else len(side.in_specs),
                             row_splits=row_splits)
    outs = pl.pallas_call(
        kern,
        grid=grid,
        in_specs=in_specs,
        out_specs=out_specs,
        out_shape=out_shape,
        compiler_params=_compiler_params(2),
        name=name,
    )(*operands)
    return outs[0] if side is None else tuple(outs)


def _gating_kernel(za_ref, ng_ref, w_ref, bt_ref, o_ref):
    row = lax.broadcasted_iota(jnp.int32, (CHUNK, CHUNK), 0)
    col = lax.broadcasted_iota(jnp.int32, (CHUNK, CHUNK), 1)
    causal = col <= row
    for c in range(za_ref.shape[0] // CHUNK):
        rows = slice(c * CHUNK, (c + 1) * CHUNK)
        z = jax.nn.gelu(za_ref[rows, :])
        u = z[:, :A_WIDTH]
        v = z[:, A_WIDTH:]
        ms = jnp.mean(v * v, axis=-1, keepdims=True)
        vn = ((v * lax.rsqrt(ms + EPS)) * ng_ref[...]).astype(_BF16)
        for g in range(A_GROUPS):
            sl = slice(g * A_GROUP_DIM, (g + 1) * A_GROUP_DIM)
            w = jnp.where(causal, w_ref[g], 0.0).astype(_BF16)
            s = jnp.dot(w, vn[:, sl], preferred_element_type=_F32) + bt_ref[:, g:g + 1]
            o_ref[rows, sl] = (u[:, sl] * s).astype(o_ref.dtype)


def _spatial_gating(za, norm_g, w_s, b_s):
    m = za.shape[0]
    tm = ROW_BLOCK["spatial_gating"]
    return pl.pallas_call(
        _gating_kernel,
        grid=(m // tm,),
        in_specs=[pl.BlockSpec((tm, 2 * A_WIDTH), lambda i: (i, 0)),
                  pl.BlockSpec((1, A_WIDTH), lambda i: (0, 0)),
                  pl.BlockSpec((A_GROUPS, CHUNK, CHUNK), lambda i: (0, 0, 0)),
                  pl.BlockSpec((CHUNK, A_GROUPS), lambda i: (0, 0))],
        out_specs=pl.BlockSpec((tm, A_WIDTH), lambda i: (i, 0)),
        out_shape=jax.ShapeDtypeStruct((m, A_WIDTH), _BF16),
        compiler_params=_compiler_params(1),
        name="spatial_gating",
    )(za, norm_g.reshape(1, A_WIDTH), w_s, b_s.T)


def _lane_inv_freq(rot_dim, period):
    inv_freq = ROPE_THETA ** (-jnp.arange(0, rot_dim, 2, dtype=_F32) / rot_dim)
    head = jnp.concatenate([inv_freq, inv_freq, jnp.zeros((period - rot_dim,), _F32)])
    return jnp.tile(head, LANES // period).reshape(1, LANES)


def _rope_tables(pos, inv_freq_lanes, rot_dim, period):
    half = rot_dim // 2
    ang = pos * inv_freq_lanes
    cos, sin = jnp.cos(ang), jnp.sin(ang)
    lane = lax.broadcasted_iota(jnp.int32, ang.shape, 1) & (period - 1)
    c = jnp.where(lane < rot_dim, cos, 1.0)
    s1 = jnp.where(lane >= half, jnp.where(lane < rot_dim, sin, 0.0), 0.0)
    s2 = jnp.where(lane < half, -sin, 0.0)
    return c, s1, s2


def _rope(x, c, s1, s2, half):
    return x * c + pltpu.roll(x, half, 1) * s1 + pltpu.roll(x, LANES - half, 1) * s2


def _prep_kernel(att_ref, pos_ref, fb_ref, fi_ref,
                 q_ref, k_ref, vt_ref, qi_ref, ki_ref, wit_ref):
    pos = pos_ref[0]
    cb, s1b, s2b = _rope_tables(pos, fb_ref[...], B_ROT, B_HEAD_DIM)
    ci, s1i, s2i = _rope_tables(pos, fi_ref[...], IDX_ROT, IDX_DIM)
    for h in range(B_HEADS):
        x = att_ref[0, :, ATT_Q0 + h * LANES:ATT_Q0 + (h + 1) * LANES]
        q_ref[0, :, h * LANES:(h + 1) * LANES] = _rope(x, cb, s1b, s2b, B_ROT // 2).astype(_BF16)
    for h in range(B_KV_HEADS):
        x = att_ref[0, :, ATT_K0 + h * LANES:ATT_K0 + (h + 1) * LANES]
        k_ref[0, :, h * LANES:(h + 1) * LANES] = _rope(x, cb, s1b, s2b, B_ROT // 2).astype(_BF16)
    vt_ref[0, 0] = att_ref[0, :, ATT_V0:ATT_V0 + B_KV_WIDTH].T.astype(_BF16)
    lane = lax.broadcasted_iota(jnp.int32, ci.shape, 1)
    is_ki = lane < IDX_DIM
    heads_per_group = LANES // IDX_DIM
    for j in range(IDX_WIDTH // LANES):
        x = att_ref[0, :, ATT_QI0 + j * LANES:ATT_QI0 + (j + 1) * LANES]
        y = _rope(x, ci, s1i, s2i, IDX_ROT // 2)
        for r in range(heads_per_group):
            h = j * heads_per_group + r
            yr = y if r == 0 else pltpu.roll(y, LANES - r * IDX_DIM, 1)
            qi_ref[0, :, h * LANES:(h + 1) * LANES] = jnp.where(is_ki, yr, 0.0).astype(_BF16)
    x = att_ref[0, :, ATT_KW0:ATT_KW0 + LANES]
    y = _rope(x, jnp.where(is_ki, ci, 1.0), jnp.where(is_ki, s1i, 0.0),
              jnp.where(is_ki, s2i, 0.0), IDX_ROT // 2)
    ki_ref[0] = jnp.where(is_ki, y, 0.0).astype(_BF16)
    wit_ref[0] = y.T[IDX_DIM:IDX_DIM + IDX_HEADS, :]


def _prep_side_job(att, positions, n_steps):
    bsz, seq, _ = att.shape
    ts = bsz * seq // n_steps
    assert ts * n_steps == bsz * seq and seq % ts == 0 and KEY_CHUNK % ts == 0
    assert ts % (2 * SUBLANES) == 0
    per_seq = seq // ts
    per_chunk = KEY_CHUNK // ts

    def tokens(width):
        return pl.BlockSpec((1, ts, width), lambda s: (s // per_seq, s % per_seq, 0))

    def out(width, dtype):
        return tokens(width), jax.ShapeDtypeStruct((bsz, seq, width), dtype)

    vt_out = (pl.BlockSpec((1, 1, B_KV_WIDTH, ts),
                           lambda s: (s // per_seq, (s % per_seq) // per_chunk, 0,
                                      (s % per_seq) % per_chunk)),
              jax.ShapeDtypeStruct((bsz, seq // KEY_CHUNK, B_KV_WIDTH, KEY_CHUNK), _BF16))
    wit_out = (pl.BlockSpec((1, IDX_HEADS, ts), lambda s: (s // per_seq, 0, s % per_seq)),
               jax.ShapeDtypeStruct((bsz, IDX_HEADS, seq), _F32))
    outs = [out(B_WIDTH, _BF16), out(B_KV_WIDTH, _BF16), vt_out,
            out(IDX_HEADS * LANES, _BF16), out(LANES, _BF16), wit_out]
    freq_spec = pl.BlockSpec((1, LANES), lambda s: (0, 0))
    return SideJob(_prep_kernel,
                   (att, positions.astype(_F32).reshape(bsz, seq, 1),
                    _lane_inv_freq(B_ROT, B_HEAD_DIM), _lane_inv_freq(IDX_ROT, IDX_DIM)),
                   (tokens(ATT_WIDTH), tokens(1), freq_spec, freq_spec),
                   tuple(o[0] for o in outs), tuple(o[1] for o in outs))


def _ordered_int_to_float(key):
    bits = jnp.where(key < 0, key ^ jnp.int32(0x7FFFFFFF), key)
    return lax.bitcast_convert_type(bits, _F32)


def _fold_rows(x, op):
    while x.shape[0] > SUBLANES:
        half = x.shape[0] // 2
        x = op(x[:half], x[half:])
    return x


_DSA_N_IN = 6


def _dsa_kernel(*refs, n_sel, side_body, n_side_in, n_side_out):
    o_at = _DSA_N_IN + n_side_in
    if side_body is not None:
        side_body(*refs[_DSA_N_IN:o_at], *refs[o_at + 1:o_at + 1 + n_side_out])
    own = refs[:_DSA_N_IN] + (refs[o_at],) + refs[o_at + 1 + n_side_out:]
    seq = own[4].shape[1]
    n_chunks = (pl.program_id(1) * Q_BLOCK + Q_BLOCK + KEY_CHUNK - 1) // KEY_CHUNK
    for n in range(1, seq // KEY_CHUNK + 1):
        pl.when(n_chunks == n)(functools.partial(_dsa_block, *own, n_sel=n_sel, n_chunks=n))


def _dsa_block(qi_ref, wit_ref, ki_ref, q_ref, k_ref, vt_ref, o_ref,
               score_ref, bias_ref, logit_ref, acc_ref, *, n_sel, n_chunks):
    t = Q_BLOCK
    blk = pl.program_id(1)
    qpos = blk * t + lax.broadcasted_iota(jnp.int32, (1, t), 1)
    kiota = lax.broadcasted_iota(jnp.int32, (KEY_CHUNK, 1), 0)
    idx_scale = (IDX_DIM ** -0.5) * (IDX_HEADS ** -0.5)

    def rows(c):
        return slice(c * KEY_CHUNK, (c + 1) * KEY_CHUNK)

    def for_chunks(body, carry):
        for c in range(n_chunks):
            carry = body(c, carry)
        return carry

    def indexer_chunk(c, carry):
        ki = ki_ref[0, rows(c), :]
        acc = jnp.zeros((KEY_CHUNK, t), _F32)
        for h in range(0, IDX_HEADS, 2):
            qpair = jnp.concatenate([qi_ref[0, :, h * LANES:(h + 1) * LANES],
                                     qi_ref[0, :, (h + 1) * LANES:(h + 2) * LANES]], axis=0)
            d = lax.dot_general(ki, qpair, _NT_DIMS, preferred_element_type=_F32)
            acc = acc + wit_ref[0, h:h + 1, :] * jnp.maximum(d[:, :t], 0.0)
            acc = acc + wit_ref[0, h + 1:h + 2, :] * jnp.maximum(d[:, t:], 0.0)
        kpos = c * KEY_CHUNK + kiota
        score_ref[rows(c), :] = jnp.where(kpos <= qpos, acc * idx_scale + 0.0, _NEG_INF)
        return carry

    for_chunks(indexer_chunk, 0)

    def count(indicator):
        def chunk(c, part):
            return part + _fold_rows(indicator(score_ref[rows(c), :], c), jnp.add)
        part = for_chunks(chunk, jnp.zeros((SUBLANES, t), _F32))
        return jnp.sum(part, axis=0, keepdims=True)

    def count_ge(cand_f):
        return count(lambda s, c: jnp.where(s >= cand_f, 1.0, 0.0))

    def search():
        int_min = jnp.int32(-2 ** 31)
        base = jnp.where(count_ge(jnp.zeros((1, t), _F32)) >= n_sel, jnp.int32(0), int_min)

        def search_step(it, base):
            cand = base | lax.shift_left(jnp.int32(1), jnp.int32(30) - it)
            cnt = count_ge(_ordered_int_to_float(cand))
            return jnp.where(cnt >= n_sel, cand, base)

        return lax.fori_loop(0, 31, search_step, base)

    if (n_chunks - 1) * KEY_CHUNK >= n_sel:
        base = search()
    else:
        base = lax.cond((blk + 1) * t <= n_sel, lambda: jnp.zeros((1, t), jnp.int32), search)
    thr = jnp.where(qpos + 1 <= n_sel, _NEG_INF, _ordered_int_to_float(base))

    def bias_chunk(c, part):
        s = score_ref[rows(c), :]
        sel = jnp.where(c * KEY_CHUNK + kiota <= qpos, jnp.where(s >= thr, 1.0, 0.0), 0.0)
        bias_ref[rows(c), :] = jnp.where(sel > 0.0, 0.0, _NEG_INF)
        return part + _fold_rows(sel, jnp.add)

    n_ge = jnp.sum(for_chunks(bias_chunk, jnp.zeros((SUBLANES, t), _F32)), axis=0, keepdims=True)

    @pl.when(jnp.max(n_ge) > n_sel)
    def _():
        nxt = _ordered_int_to_float(base + 1)
        tied = n_ge > n_sel
        n_above = count(lambda s, c: jnp.where(s >= nxt, 1.0, 0.0))
        need = jnp.where(tied, n_sel - n_above, 0.0)
        no_index = float(k_ref.shape[1])
        front_v = jnp.where(tied, jnp.inf, _NEG_INF)
        front_i = jnp.where(tied, -1.0, no_index)

        def beyond(s, kposf, front_v, front_i):
            after = jnp.where(s < front_v, 1.0,
                              jnp.where(s == front_v, jnp.where(kposf > front_i, 1.0, 0.0), 0.0))
            return jnp.where(s >= thr, jnp.where(s < nxt, after, 0.0), 0.0)

        def kposf(c):
            return (c * KEY_CHUNK + kiota).astype(_F32)

        def advance(_, carry):
            front_v, front_i, need = carry

            def best_value(c, part):
                s = score_ref[rows(c), :]
                cand = jnp.where(beyond(s, kposf(c), front_v, front_i) > 0.0, s, _NEG_INF)
                return jnp.maximum(part, _fold_rows(cand, jnp.maximum))

            v = jnp.max(for_chunks(best_value, jnp.full((SUBLANES, t), _NEG_INF, _F32)),
                        axis=0, keepdims=True)

            def first_index(c, part):
                s = score_ref[rows(c), :]
                hit = jnp.where(s == v, beyond(s, kposf(c), front_v, front_i), 0.0)
                cand = jnp.where(hit > 0.0, kposf(c), no_index)
                return jnp.minimum(part, _fold_rows(cand, jnp.minimum))

            i = jnp.min(for_chunks(first_index, jnp.full((SUBLANES, t), no_index, _F32)),
                        axis=0, keepdims=True)
            active = need > 0.0
            return (jnp.where(active, v, front_v), jnp.where(active, i, front_i),
                    jnp.where(active, need - 1.0, need))

        front_v, front_i, _ = lax.fori_loop(0, jnp.max(need).astype(jnp.int32), advance,
                                            (front_v, front_i, need))

        def tie_chunk(c, carry):
            s = score_ref[rows(c), :]
            upto = jnp.where(s > front_v, 0.0,
                             jnp.where(s == front_v,
                                       jnp.where(kposf(c) <= front_i, 0.0, _NEG_INF), _NEG_INF))
            keep = jnp.where(s >= nxt, 0.0, jnp.where(s >= thr, upto, _NEG_INF))
            bias_ref[rows(c), :] = jnp.where(c * KEY_CHUNK + kiota <= qpos, keep, _NEG_INF)
            return carry

        for_chunks(tie_chunk, 0)

    grp = B_HEADS // B_KV_HEADS
    gw = grp * t
    exp2_scale = (B_HEAD_DIM ** -0.5) * 1.4426950408889634

    def logit_chunk(c, mx):
        bias = jnp.concatenate([bias_ref[rows(c), :]] * grp, axis=1)
        folded = []
        for g in range(B_KV_HEADS):
            qg = jnp.concatenate(
                [q_ref[0, :, (g * grp + hh) * B_HEAD_DIM:(g * grp + hh + 1) * B_HEAD_DIM]
                 for hh in range(grp)], axis=0)
            l = lax.dot_general(k_ref[0, rows(c), g * B_HEAD_DIM:(g + 1) * B_HEAD_DIM], qg,
                                _NT_DIMS, preferred_element_type=_F32) + bias
            logit_ref[rows(c), g * gw:(g + 1) * gw] = l
            folded.append(_fold_rows(l, jnp.maximum))
        return jnp.maximum(mx, jnp.concatenate(folded, axis=1))

    mx = for_chunks(logit_chunk, jnp.full((SUBLANES, B_HEADS * t), _NEG_INF, _F32))
    mx = jnp.max(mx, axis=0, keepdims=True)

    acc_ref[...] = jnp.zeros_like(acc_ref)
    ones_rows = jnp.ones((acc_ref.shape[0] - B_HEAD_DIM, KEY_CHUNK), _BF16)

    def pv_chunk(c, carry):
        for g in range(B_KV_HEADS):
            cols = slice(g * gw, (g + 1) * gw)
            p = jnp.exp2((logit_ref[rows(c), cols] - mx[:, cols]) * exp2_scale)
            vt_ext = jnp.concatenate(
                [vt_ref[0, c, g * B_HEAD_DIM:(g + 1) * B_HEAD_DIM, :], ones_rows], axis=0)
            acc_ref[:, cols] += jnp.dot(vt_ext, p.astype(_BF16), preferred_element_type=_F32)
        return carry

    for_chunks(pv_chunk, 0)
    out_t = acc_ref[:B_HEAD_DIM, :] / acc_ref[B_HEAD_DIM:B_HEAD_DIM + 1, :]
    for h in range(B_HEADS):
        o_ref[0, :, h * B_HEAD_DIM:(h + 1) * B_HEAD_DIM] = (
            out_t[:, h * t:(h + 1) * t].T.astype(o_ref.dtype))


def _dsa_attention(q, k, vt, qi, ki, wit, side_job=None):
    bsz, seq, _ = q.shape
    assert Q_BLOCK == LANES and seq % KEY_CHUNK == 0
    n_sel = min(TOPK_MAX, seq // 4)
    n_blk = seq // Q_BLOCK

    def q_spec(width):
        return pl.BlockSpec((1, Q_BLOCK, width), lambda b, i: (b, i, 0))

    def kv_spec(width):
        return pl.BlockSpec((1, seq, width), lambda b, i: (b, 0, 0))

    def per_step(spec):
        return pl.BlockSpec(spec.block_shape, lambda b, i: spec.index_map(b * n_blk + i))

    side = None if side_job is None else side_job(bsz * n_blk)
    side_in = [] if side is None else [per_step(s) for s in side.in_specs]
    side_out = [] if side is None else [per_step(s) for s in side.out_specs]
    outs = pl.pallas_call(
        functools.partial(_dsa_kernel, n_sel=n_sel,
                          side_body=None if side is None else side.body,
                          n_side_in=len(side_in), n_side_out=len(side_out)),
        grid=(bsz, n_blk),
        in_specs=[q_spec(IDX_HEADS * LANES),
                  pl.BlockSpec((1, IDX_HEADS, Q_BLOCK), lambda b, i: (b, 0, i)),
                  kv_spec(LANES), q_spec(B_WIDTH), kv_spec(B_KV_WIDTH),
                  pl.BlockSpec((1, seq // KEY_CHUNK, B_KV_WIDTH, KEY_CHUNK),
                               lambda b, i: (b, 0, 0, 0))] + side_in,
        out_specs=[q_spec(B_WIDTH)] + side_out,
        out_shape=[jax.ShapeDtypeStruct((bsz, seq, B_WIDTH), _BF16)]
        + ([] if side is None else list(side.out_shape)),
        scratch_shapes=[pltpu.VMEM((seq, Q_BLOCK), _F32), pltpu.VMEM((seq, Q_BLOCK), _F32),
                        pltpu.VMEM((seq, B_HEADS * Q_BLOCK), _F32),
                        pltpu.VMEM((B_HEAD_DIM + 2 * SUBLANES, B_HEADS * Q_BLOCK), _F32)],
        compiler_params=_compiler_params(2),
        name="dsa_attention",
    )(qi, wit, ki, q, k, vt, *(() if side is None else side.arrays))
    return outs[0] if side is None else tuple(outs)


def _xattn_kernel(q_ref, k_ref, v_ref, o_ref):
    scale = X_HEAD_DIM ** -0.5
    for h in range(X_HEADS):
        hs = slice(h * X_HEAD_DIM, (h + 1) * X_HEAD_DIM)
        logits = lax.dot_general(q_ref[0, :, hs], k_ref[0, :, hs], _NT_DIMS,
                                 preferred_element_type=_F32) * scale
        mx = jnp.max(logits, axis=-1, keepdims=True)
        p = jnp.exp(logits - mx)
        denom = jnp.sum(p, axis=-1, keepdims=True)
        o = jnp.dot(p.astype(_BF16), v_ref[0, :, hs], preferred_element_type=_F32)
        o_ref[0, :, hs] = (o / denom).astype(o_ref.dtype)


def _cross_attention(q, k, v):
    bsz, seq, _ = q.shape
    tq = ROW_BLOCK["cross_attention"]
    m = k.shape[1]
    mem_spec = pl.BlockSpec((1, m, X_WIDTH), lambda b, i: (b, 0, 0))
    return pl.pallas_call(
        _xattn_kernel,
        grid=(bsz, seq // tq),
        in_specs=[pl.BlockSpec((1, tq, X_WIDTH), lambda b, i: (b, i, 0)), mem_spec, mem_spec],
        out_specs=pl.BlockSpec((1, tq, X_WIDTH), lambda b, i: (b, i, 0)),
        out_shape=jax.ShapeDtypeStruct((bsz, seq, X_WIDTH), _BF16),
        compiler_params=_compiler_params(2),
        name="cross_attention",
    )(q, k, v)


def _norm_proj_kernel(x_ref, g_ref, w_ref, o_ref):
    x = x_ref[...]
    ms = jnp.mean(x * x, axis=-1, keepdims=True)
    h = ((x * lax.rsqrt(ms + EPS)) * g_ref[...]).astype(_BF16)
    o_ref[...] = jnp.dot(h, w_ref[...], preferred_element_type=_F32).astype(o_ref.dtype)


def _norm_proj(x, g, w_bf16, out_dtype, name):
    m, d = x.shape
    tm = ROW_BLOCK["norm_proj"]
    n = w_bf16.shape[1]
    return pl.pallas_call(
        _norm_proj_kernel,
        grid=(m // tm,),
        in_specs=[pl.BlockSpec((tm, d), lambda i: (i, 0)),
                  pl.BlockSpec((1, d), lambda i: (0, 0)),
                  pl.BlockSpec((d, n), lambda i: (0, 0), pipeline_mode=pl.Buffered(1))],
        out_specs=pl.BlockSpec((tm, n), lambda i: (i, 0)),
        out_shape=jax.ShapeDtypeStruct((m, n), out_dtype),
        compiler_params=_compiler_params(1),
        name=name,
    )(x, g.reshape(1, d), w_bf16)


def _xattn_out_kernel(o_ref, w_ref, x_ref, g_ref, x_out_ref, h_out_ref):
    x = x_ref[...] + jnp.dot(o_ref[...], w_ref[...], preferred_element_type=_F32)
    x_out_ref[...] = x
    ms = jnp.mean(x * x, axis=-1, keepdims=True)
    h_out_ref[...] = ((x * lax.rsqrt(ms + EPS)) * g_ref[...]).astype(h_out_ref.dtype)


def _xattn_out_and_norm(o, w_bf16, x, g):
    m, k = o.shape
    tm = ROW_BLOCK["xattn_out_norm"]
    d = x.shape[1]
    row_spec = pl.BlockSpec((tm, d), lambda i: (i, 0))
    return pl.pallas_call(
        _xattn_out_kernel,
        grid=(m // tm,),
        in_specs=[pl.BlockSpec((tm, k), lambda i: (i, 0)),
                  pl.BlockSpec((k, d), lambda i: (0, 0), pipeline_mode=pl.Buffered(1)),
                  row_spec,
                  pl.BlockSpec((1, d), lambda i: (0, 0))],
        out_specs=[row_spec, row_spec],
        out_shape=[jax.ShapeDtypeStruct((m, d), _F32), jax.ShapeDtypeStruct((m, d), _BF16)],
        compiler_params=_compiler_params(1),
        name="xattn_out_norm",
    )(o, w_bf16, x, g.reshape(1, d))


def kernel(x, mem, positions, norm_mix_g, w_in, a_norm_g, a_spatial_w, a_spatial_b, p_a, p_b,
           w_out, norm_x_g, norm_mem_g, xq_w, xk_w, xv_w, xo_w, norm_ffn_g, ffn_w1, ffn_w3,
           ffn_w2, final_norm_g):
    bsz, seq, d = x.shape
    m = bsz * seq
    depth = w_in.shape[0]
    ffn_hidden = ffn_w1.shape[-1]

    xf = x.reshape(m, d)
    mem_f = mem.reshape(bsz * MEM_LEN, d)
    w_in_t = jnp.transpose(w_in, (0, 2, 1))
    for l in range(depth):
        def w_in_cols(col0):
            return Rhs(w_in_t, l, col0, transposed=True)

        h = _rmsnorm(xf, norm_mix_g[l], _BF16)
        att = _fused_matmul("proj_att", [h], [(0, w_in_cols(ZA_END))], [], _ep_identity,
                            ATT_WIDTH, _F32)
        za, q, k, vt, qi, ki, wit = _fused_matmul(
            "proj_za", [h], [(0, w_in_cols(0))], [], _ep_identity, ZA_END, _F32,
            side_job=functools.partial(_prep_side_job, att.reshape(bsz, seq, ATT_WIDTH),
                                       positions))
        y_a = _spatial_gating(za, a_norm_g[l], a_spatial_w[l], a_spatial_b[l])
        def weight_casts(n_steps):
            return _merge_side_jobs(
                _cast_side_job(p_a, l, n_steps), _cast_side_job(p_b, l, n_steps),
                _cast_side_job(w_in_t, l, n_steps, row0=ATT_END, rows=2 * d),
                _cast_side_job(w_out, l, n_steps), _cast_side_job(xq_w, l, n_steps),
                _cast_side_job(xk_w, l, n_steps), _cast_side_job(xv_w, l, n_steps))

        y_b, pa_bf16, pb_bf16, gate_wt_bf16, wo_bf16, xq_bf16, xk_bf16, xv_bf16 = _dsa_attention(
            q, k, vt, qi, ki, wit, side_job=weight_casts)

        def ffn_up_weight_casts(n_steps):
            return _merge_side_jobs(_cast_side_job(ffn_w1, l, n_steps),
                                    _cast_side_job(ffn_w3, l, n_steps))

        merged, w1_bf16, w3_bf16 = _fused_matmul(
            "gated_merge", [y_a, y_b.reshape(m, B_WIDTH), h],
            [(0, Rhs(pa_bf16)), (1, Rhs(pb_bf16)), (2, Rhs(gate_wt_bf16, transposed=True)),
             (2, Rhs(gate_wt_bf16, col0=d, transposed=True))],
            [], _ep_gated_merge, d, _BF16, side_job=ffn_up_weight_casts)
        x1, xo_bf16 = _fused_matmul("out_proj", [merged], [(0, Rhs(wo_bf16))], [xf],
                                    _ep_residual, d, _F32,
                                    side_job=functools.partial(_cast_side_job, xo_w, l))

        qx = _norm_proj(x1, norm_x_g[l], xq_bf16, _BF16, "xattn_q")
        kx = _norm_proj(mem_f, norm_mem_g[l], xk_bf16, _BF16, "xattn_k")
        vx = _norm_proj(mem_f, norm_mem_g[l], xv_bf16, _BF16, "xattn_v")
        ox = _cross_attention(qx.reshape(bsz, seq, X_WIDTH),
                              kx.reshape(bsz, MEM_LEN, X_WIDTH),
                              vx.reshape(bsz, MEM_LEN, X_WIDTH)).reshape(m, X_WIDTH)
        x2, h2 = _xattn_out_and_norm(ox, xo_bf16, x1, norm_ffn_g[l])

        act, w2_bf16 = _fused_matmul("ffn_up", [h2],
                                     [(0, Rhs(w1_bf16)), (0, Rhs(w3_bf16))],
                                     [], _ep_swiglu, ffn_hidden, _BF16,
                                     side_job=functools.partial(_cast_side_job, ffn_w2, l))
        xf = _fused_matmul("ffn_down", [act], [(0, Rhs(w2_bf16))], [x2], _ep_residual, d, _F32)

    return _rmsnorm(xf, final_norm_g, _F32).reshape(bsz, seq, d)
```

```python
import functools
from typing import Callable, NamedTuple, Optional

import jax
import jax.numpy as jnp
from jax import lax
from jax.experimental import pallas as pl
from jax.experimental.pallas import tpu as pltpu

D_MODEL = 4096
MEM_LEN = 256
EPS = 1e-6
ROPE_THETA = 500000.0
CHUNK = 128
A_GROUPS = 16
A_WIDTH = D_MODEL // 2
A_GROUP_DIM = A_WIDTH // A_GROUPS
B_HEADS = 16
B_HEAD_DIM = 128
B_KV_HEADS = 4
B_WIDTH = B_HEADS * B_HEAD_DIM
B_KV_WIDTH = B_KV_HEADS * B_HEAD_DIM
B_ROT = B_HEAD_DIM // 4
IDX_HEADS = 16
IDX_DIM = 64
IDX_WIDTH = IDX_HEADS * IDX_DIM
IDX_ROT = IDX_DIM // 4
TOPK_MAX = 256
Q_BLOCK = 128
X_HEADS = 4
X_HEAD_DIM = 256
X_WIDTH = X_HEADS * X_HEAD_DIM
IN_SIZES = (2 * A_WIDTH, B_WIDTH, B_KV_WIDTH, B_KV_WIDTH, IDX_WIDTH, IDX_DIM, IDX_HEADS,
            2 * D_MODEL)

LANES = 128
SUBLANES = 8
KEY_CHUNK = 512
PROJ_TN = 512
ZA_END = IN_SIZES[0]
ATT_END = ZA_END + sum(IN_SIZES[1:7])
ATT_Q0 = 0
ATT_K0 = ATT_Q0 + B_WIDTH
ATT_V0 = ATT_K0 + B_KV_WIDTH
ATT_QI0 = ATT_V0 + B_KV_WIDTH
ATT_KW0 = ATT_QI0 + IDX_WIDTH
ATT_WIDTH = -(-(ATT_END - ZA_END) // PROJ_TN) * PROJ_TN
assert ZA_END % PROJ_TN == 0 and ATT_KW0 + LANES <= ATT_WIDTH

VMEM_LIMIT_BYTES = 60 * 1024 * 1024


class Tile(NamedTuple):
    tm: int
    tn: int
    row_splits: int = 1


TILES = {
    "proj_att": Tile(1024, PROJ_TN),
    "proj_za": Tile(1024, 512),
    "gated_merge": Tile(1024, 256, row_splits=4),
    "out_proj": Tile(1024, 512),
    "ffn_up": Tile(2048, 256, row_splits=4),
    "ffn_down": Tile(512, 512),
}
ROW_BLOCK = {
    "rmsnorm": 512,
    "spatial_gating": 4 * CHUNK,
    "cross_attention": 1024,
    "norm_proj": 512,
    "xattn_out_norm": 512,
}

_BF16 = jnp.bfloat16
_F32 = jnp.float32
_NEG_INF = float("-inf")
_NN_DIMS = (((1,), (0,)), ((), ()))
_NT_DIMS = (((1,), (1,)), ((), ()))


def _compiler_params(n_axes):
    return pltpu.CompilerParams(
        dimension_semantics=("arbitrary",) * n_axes,
        vmem_limit_bytes=VMEM_LIMIT_BYTES,
    )


def _rmsnorm_kernel(x_ref, g_ref, o_ref):
    x = x_ref[...]
    ms = jnp.mean(x * x, axis=-1, keepdims=True)
    y = x * lax.rsqrt(ms + EPS)
    o_ref[...] = (y * g_ref[...]).astype(o_ref.dtype)


def _rmsnorm(x, g, out_dtype):
    m, d = x.shape
    tm = ROW_BLOCK["rmsnorm"]
    return pl.pallas_call(
        _rmsnorm_kernel,
        grid=(m // tm,),
        in_specs=[pl.BlockSpec((tm, d), lambda i: (i, 0)),
                  pl.BlockSpec((1, d), lambda i: (0, 0))],
        out_specs=pl.BlockSpec((tm, d), lambda i: (i, 0)),
        out_shape=jax.ShapeDtypeStruct((m, d), out_dtype),
        compiler_params=_compiler_params(1),
        name="rmsnorm",
    )(x, g.reshape(1, d))


def _ep_identity(accs, extras):
    return accs[0]


def _ep_residual(accs, extras):
    return extras[0] + accs[0]


def _ep_gated_merge(accs, extras):
    y_a, y_b, gate_a, gate_b = accs
    return jax.nn.sigmoid(gate_a) * y_a + jax.nn.sigmoid(gate_b) * y_b


def _ep_swiglu(accs, extras):
    return jax.nn.silu(accs[0]) * accs[1]


class Rhs(NamedTuple):
    array: jax.Array
    layer: Optional[int] = None
    col0: int = 0
    transposed: bool = False


def _rhs_spec(r, tn):
    if r.transposed and r.layer is None:
        assert r.col0 % tn == 0
        off = r.col0 // tn
        return pl.BlockSpec((tn, r.array.shape[1]), lambda i, j: (j + off, 0))
    if r.transposed:
        k = r.array.shape[2]
        if r.col0 % tn == 0:
            return pl.BlockSpec((None, tn, k), lambda i, j: (r.layer, j + r.col0 // tn, 0))
        assert r.col0 % SUBLANES == 0
        return pl.BlockSpec((pl.Element(1), pl.Element(tn), pl.Element(k)),
                            lambda i, j: (r.layer, (r.col0 // SUBLANES + j * (tn // SUBLANES))
                                          * SUBLANES, 0))
    assert r.col0 % tn == 0
    off = r.col0 // tn
    if r.layer is None:
        return pl.BlockSpec((r.array.shape[0], tn), lambda i, j: (0, j + off))
    return pl.BlockSpec((None, r.array.shape[1], tn), lambda i, j: (r.layer, 0, j + off))


class SideJob(NamedTuple):
    body: Callable
    arrays: tuple
    in_specs: tuple
    out_specs: tuple
    out_shape: tuple


def _cast_body(x_ref, o_ref):
    x = x_ref[0] if len(x_ref.shape) == 3 else x_ref[...]
    o_ref[...] = x.astype(o_ref.dtype)


def _cast_side_job(w_stack, layer, n_steps, row0=0, rows=None):
    rows = w_stack.shape[1] if rows is None else rows
    cols = w_stack.shape[2]
    slab = rows // n_steps
    assert slab * n_steps == rows and slab % (2 * SUBLANES) == 0
    if row0 % slab == 0:
        in_spec = pl.BlockSpec((None, slab, cols), lambda s: (layer, s + row0 // slab, 0))
    else:
        assert row0 % SUBLANES == 0
        in_spec = pl.BlockSpec(
            (pl.Element(1), pl.Element(slab), pl.Element(cols)),
            lambda s: (layer, (row0 // SUBLANES + s * (slab // SUBLANES)) * SUBLANES, 0))
    return SideJob(_cast_body, (w_stack,), (in_spec,),
                   (pl.BlockSpec((slab, cols), lambda s: (s, 0)),),
                   (jax.ShapeDtypeStruct((rows, cols), _BF16),))


def _merge_side_jobs(*jobs):
    n_in = [len(j.in_specs) for j in jobs]
    n_out = [len(j.out_specs) for j in jobs]

    def body(*refs):
        ins, outs = refs[:sum(n_in)], refs[sum(n_in):]
        i = o = 0
        for j, ni, no in zip(jobs, n_in, n_out):
            j.body(*ins[i:i + ni], *outs[o:o + no])
            i, o = i + ni, o + no

    return SideJob(body, sum((j.arrays for j in jobs), ()), sum((j.in_specs for j in jobs), ()),
                   sum((j.out_specs for j in jobs), ()), sum((j.out_shape for j in jobs), ()))


def _mm_kernel(*refs, n_lhs, pair_lhs, rhs_transposed, n_extra, epilogue, side_body, n_side_in,
               row_splits):
    n_rhs = len(pair_lhs)
    lhs_refs = refs[:n_lhs]
    rhs_refs = refs[n_lhs:n_lhs + n_rhs]
    n_in = n_lhs + n_rhs + n_extra
    extra_refs = refs[n_lhs + n_rhs:n_in]
    o_ref = refs[n_in + n_side_in]
    if side_body is not None:
        side_body(*refs[n_in:n_in + n_side_in], *refs[n_in + n_side_in + 1:])
    weights = [(r[0] if len(r.shape) == 3 else r[...]).astype(_BF16) for r in rhs_refs]
    rows_per_split = o_ref.shape[0] // row_splits
    for s in range(row_splits):
        rows = slice(s * rows_per_split, (s + 1) * rows_per_split)
        accs = [lax.dot_general(lhs_refs[li][rows, :], w, _NT_DIMS if tr else _NN_DIMS,
                                preferred_element_type=_F32)
                for li, w, tr in zip(pair_lhs, weights, rhs_transposed)]
        o_ref[rows, :] = epilogue(accs, [e[rows, :] for e in extra_refs]).astype(o_ref.dtype)


def _fused_matmul(name, lhs, pairs, extras, epilogue, n_out, out_dtype, side_job=None):
    tm, tn, row_splits = TILES[name]
    m = lhs[0].shape[0]
    assert m % tm == 0 and n_out % tn == 0
    grid = (m // tm, n_out // tn)
    in_specs = [pl.BlockSpec((tm, a.shape[1]), lambda i, j: (i, 0)) for a in lhs]
    in_specs += [_rhs_spec(r, tn) for _, r in pairs]
    in_specs += [pl.BlockSpec((tm, tn), lambda i, j: (i, j)) for _ in extras]
    out_specs = [pl.BlockSpec((tm, tn), lambda i, j: (i, j))]
    out_shape = [jax.ShapeDtypeStruct((m, n_out), out_dtype)]
    operands = list(lhs) + [p[1].array for p in pairs] + list(extras)
    side = None if side_job is None else side_job(grid[0] * grid[1])
    if side is not None:
        def per_step(spec):
            return pl.BlockSpec(spec.block_shape,
                                lambda i, j: spec.index_map(i * grid[1] + j))
        in_specs += [per_step(s) for s in side.in_specs]
        out_specs += [per_step(s) for s in side.out_specs]
        out_shape += list(side.out_shape)
        operands += list(side.arrays)
    kern = functools.partial(_mm_kernel, n_lhs=len(lhs), pair_lhs=tuple(p[0] for p in pairs),
                             rhs_transposed=tuple(p[1].transposed for p in pairs),
                             n_extra=len(extras), epilogue=epilogue,
                             side_body=None if side is None else side.body,
                             n_side_in=0 if side is None else len(side.in_specs),
                             row_splits=row_splits)
    outs = pl.pallas_call(
        kern,
        grid=grid,
        in_specs=in_specs,
        out_specs=out_specs,
        out_shape=out_shape,
        compiler_params=_compiler_params(2),
        name=name,
    )(*operands)
    return outs[0] if side is None else tuple(outs)


def _gating_kernel(za_ref, ng_ref, w_ref, bt_ref, o_ref):
    row = lax.broadcasted_iota(jnp.int32, (CHUNK, CHUNK), 0)
    col = lax.broadcasted_iota(jnp.int32, (CHUNK, CHUNK), 1)
    causal = col <= row
    for c in range(za_ref.shape[0] // CHUNK):
        rows = slice(c * CHUNK, (c + 1) * CHUNK)
        z = jax.nn.gelu(za_ref[rows, :])
        u = z[:, :A_WIDTH]
        v = z[:, A_WIDTH:]
        ms = jnp.mean(v * v, axis=-1, keepdims=True)
        vn = ((v * lax.rsqrt(ms + EPS)) * ng_ref[...]).astype(_BF16)
        for g in range(A_GROUPS):
            sl = slice(g * A_GROUP_DIM, (g + 1) * A_GROUP_DIM)
            w = jnp.where(causal, w_ref[g], 0.0).astype(_BF16)
            s = jnp.dot(w, vn[:, sl], preferred_element_type=_F32) + bt_ref[:, g:g + 1]
            o_ref[rows, sl] = (u[:, sl] * s).astype(o_ref.dtype)


def _spatial_gating(za, norm_g, w_s, b_s):
    m = za.shape[0]
    tm = ROW_BLOCK["spatial_gating"]
    return pl.pallas_call(
        _gating_kernel,
        grid=(m // tm,),
        in_specs=[pl.BlockSpec((tm, 2 * A_WIDTH), lambda i: (i, 0)),
                  pl.BlockSpec((1, A_WIDTH), lambda i: (0, 0)),
                  pl.BlockSpec((A_GROUPS, CHUNK, CHUNK), lambda i: (0, 0, 0)),
                  pl.BlockSpec((CHUNK, A_GROUPS), lambda i: (0, 0))],
        out_specs=pl.BlockSpec((tm, A_WIDTH), lambda i: (i, 0)),
        out_shape=jax.ShapeDtypeStruct((m, A_WIDTH), _BF16),
        compiler_params=_compiler_params(1),
        name="spatial_gating",
    )(za, norm_g.reshape(1, A_WIDTH), w_s, b_s.T)


def _lane_inv_freq(rot_dim, period):
    inv_freq = ROPE_THETA ** (-jnp.arange(0, rot_dim, 2, dtype=_F32) / rot_dim)
    head = jnp.concatenate([inv_freq, inv_freq, jnp.zeros((period - rot_dim,), _F32)])
    return jnp.tile(head, LANES // period).reshape(1, LANES)


def _rope_tables(pos, inv_freq_lanes, rot_dim, period):
    half = rot_dim // 2
    ang = pos * inv_freq_lanes
    cos, sin = jnp.cos(ang), jnp.sin(ang)
    lane = lax.broadcasted_iota(jnp.int32, ang.shape, 1) & (period - 1)
    c = jnp.where(lane < rot_dim, cos, 1.0)
    s1 = jnp.where(lane >= half, jnp.where(lane < rot_dim, sin, 0.0), 0.0)
    s2 = jnp.where(lane < half, -sin, 0.0)
    return c, s1, s2


def _rope(x, c, s1, s2, half):
    return x * c + pltpu.roll(x, half, 1) * s1 + pltpu.roll(x, LANES - half, 1) * s2


def _prep_kernel(att_ref, pos_ref, fb_ref, fi_ref,
                 q_ref, k_ref, vt_ref, qi_ref, ki_ref, wit_ref):
    pos = pos_ref[0]
    cb, s1b, s2b = _rope_tables(pos, fb_ref[...], B_ROT, B_HEAD_DIM)
    ci, s1i, s2i = _rope_tables(pos, fi_ref[...], IDX_ROT, IDX_DIM)
    for h in range(B_HEADS):
        x = att_ref[0, :, ATT_Q0 + h * LANES:ATT_Q0 + (h + 1) * LANES]
        q_ref[0, :, h * LANES:(h + 1) * LANES] = _rope(x, cb, s1b, s2b, B_ROT // 2).astype(_BF16)
    for h in range(B_KV_HEADS):
        x = att_ref[0, :, ATT_K0 + h * LANES:ATT_K0 + (h + 1) * LANES]
        k_ref[0, :, h * LANES:(h + 1) * LANES] = _rope(x, cb, s1b, s2b, B_ROT // 2).astype(_BF16)
    vt_ref[0, 0] = att_ref[0, :, ATT_V0:ATT_V0 + B_KV_WIDTH].T.astype(_BF16)
    lane = lax.broadcasted_iota(jnp.int32, ci.shape, 1)
    is_ki = lane < IDX_DIM
    heads_per_group = LANES // IDX_DIM
    for j in range(IDX_WIDTH // LANES):
        x = att_ref[0, :, ATT_QI0 + j * LANES:ATT_QI0 + (j + 1) * LANES]
        y = _rope(x, ci, s1i, s2i, IDX_ROT // 2)
        for r in range(heads_per_group):
            h = j * heads_per_group + r
            yr = y if r == 0 else pltpu.roll(y, LANES - r * IDX_DIM, 1)
            qi_ref[0, :, h * LANES:(h + 1) * LANES] = jnp.where(is_ki, yr, 0.0).astype(_BF16)
    x = att_ref[0, :, ATT_KW0:ATT_KW0 + LANES]
    y = _rope(x, jnp.where(is_ki, ci, 1.0), jnp.where(is_ki, s1i, 0.0),
              jnp.where(is_ki, s2i, 0.0), IDX_ROT // 2)
    ki_ref[0] = jnp.where(is_ki, y, 0.0).astype(_BF16)
    wit_ref[0] = y.T[IDX_DIM:IDX_DIM + IDX_HEADS, :]


def _prep_side_job(att, positions, n_steps):
    bsz, seq, _ = att.shape
    ts = bsz * seq // n_steps
    assert ts * n_steps == bsz * seq and seq % ts == 0 and KEY_CHUNK % ts == 0
    assert ts % (2 * SUBLANES) == 0
    per_seq = seq // ts
    per_chunk = KEY_CHUNK // ts

    def tokens(width):
        return pl.BlockSpec((1, ts, width), lambda s: (s // per_seq, s % per_seq, 0))

    def out(width, dtype):
        return tokens(width), jax.ShapeDtypeStruct((bsz, seq, width), dtype)

    vt_out = (pl.BlockSpec((1, 1, B_KV_WIDTH, ts),
                           lambda s: (s // per_seq, (s % per_seq) // per_chunk, 0,
                                      (s % per_seq) % per_chunk)),
              jax.ShapeDtypeStruct((bsz, seq // KEY_CHUNK, B_KV_WIDTH, KEY_CHUNK), _BF16))
    wit_out = (pl.BlockSpec((1, IDX_HEADS, ts), lambda s: (s // per_seq, 0, s % per_seq)),
               jax.ShapeDtypeStruct((bsz, IDX_HEADS, seq), _F32))
    outs = [out(B_WIDTH, _BF16), out(B_KV_WIDTH, _BF16), vt_out,
            out(IDX_HEADS * LANES, _BF16), out(LANES, _BF16), wit_out]
    freq_spec = pl.BlockSpec((1, LANES), lambda s: (0, 0))
    return SideJob(_prep_kernel,
                   (att, positions.astype(_F32).reshape(bsz, seq, 1),
                    _lane_inv_freq(B_ROT, B_HEAD_DIM), _lane_inv_freq(IDX_ROT, IDX_DIM)),
                   (tokens(ATT_WIDTH), tokens(1), freq_spec, freq_spec),
                   tuple(o[0] for o in outs), tuple(o[1] for o in outs))


def _ordered_int_to_float(key):
    bits = jnp.where(key < 0, key ^ jnp.int32(0x7FFFFFFF), key)
    return lax.bitcast_convert_type(bits, _F32)


def _fold_rows(x, op):
    while x.shape[0] > SUBLANES:
        half = x.shape[0] // 2
        x = op(x[:half], x[half:])
    return x


_DSA_N_IN = 6


def _dsa_kernel(*refs, n_sel, side_body, n_side_in, n_side_out):
    o_at = _DSA_N_IN + n_side_in
    if side_body is not None:
        side_body(*refs[_DSA_N_IN:o_at], *refs[o_at + 1:o_at + 1 + n_side_out])
    own = refs[:_DSA_N_IN] + (refs[o_at],) + refs[o_at + 1 + n_side_out:]
    seq = own[4].shape[1]
    n_chunks = (pl.program_id(1) * Q_BLOCK + Q_BLOCK + KEY_CHUNK - 1) // KEY_CHUNK
    for n in range(1, seq // KEY_CHUNK + 1):
        pl.when(n_chunks == n)(functools.partial(_dsa_block, *own, n_sel=n_sel, n_chunks=n))


def _dsa_block(qi_ref, wit_ref, ki_ref, q_ref, k_ref, vt_ref, o_ref,
               score_ref, bias_ref, logit_ref, acc_ref, *, n_sel, n_chunks):
    t = Q_BLOCK
    blk = pl.program_id(1)
    qpos = blk * t + lax.broadcasted_iota(jnp.int32, (1, t), 1)
    kiota = lax.broadcasted_iota(jnp.int32, (KEY_CHUNK, 1), 0)
    idx_scale = (IDX_DIM ** -0.5) * (IDX_HEADS ** -0.5)

    def rows(c):
        return slice(c * KEY_CHUNK, (c + 1) * KEY_CHUNK)

    def for_chunks(body, carry):
        for c in range(n_chunks):
            carry = body(c, carry)
        return carry

    def indexer_chunk(c, carry):
        ki = ki_ref[0, rows(c), :]
        acc = jnp.zeros((KEY_CHUNK, t), _F32)
        for h in range(0, IDX_HEADS, 2):
            qpair = jnp.concatenate([qi_ref[0, :, h * LANES:(h + 1) * LANES],
                                     qi_ref[0, :, (h + 1) * LANES:(h + 2) * LANES]], axis=0)
            d = lax.dot_general(ki, qpair, _NT_DIMS, preferred_element_type=_F32)
            acc = acc + wit_ref[0, h:h + 1, :] * jnp.maximum(d[:, :t], 0.0)
            acc = acc + wit_ref[0, h + 1:h + 2, :] * jnp.maximum(d[:, t:], 0.0)
        kpos = c * KEY_CHUNK + kiota
        score_ref[rows(c), :] = jnp.where(kpos <= qpos, acc * idx_scale + 0.0, _NEG_INF)
        return carry

    for_chunks(indexer_chunk, 0)

    def count(indicator):
        def chunk(c, part):
            return part + _fold_rows(indicator(score_ref[rows(c), :], c), jnp.add)
        part = for_chunks(chunk, jnp.zeros((SUBLANES, t), _F32))
        return jnp.sum(part, axis=0, keepdims=True)

    def count_ge(cand_f):
        return count(lambda s, c: jnp.where(s >= cand_f, 1.0, 0.0))

    def search():
        int_min = jnp.int32(-2 ** 31)
        base = jnp.where(count_ge(jnp.zeros((1, t), _F32)) >= n_sel, jnp.int32(0), int_min)

        def search_step(it, base):
            cand = base | lax.shift_left(jnp.int32(1), jnp.int32(30) - it)
            cnt = count_ge(_ordered_int_to_float(cand))
            return jnp.where(cnt >= n_sel, cand, base)

        return lax.fori_loop(0, 31, search_step, base)

    if (n_chunks - 1) * KEY_CHUNK >= n_sel:
        base = search()
    else:
        base = lax.cond((blk + 1) * t <= n_sel, lambda: jnp.zeros((1, t), jnp.int32), search)
    thr = jnp.where(qpos + 1 <= n_sel, _NEG_INF, _ordered_int_to_float(base))

    def bias_chunk(c, part):
        s = score_ref[rows(c), :]
        sel = jnp.where(c * KEY_CHUNK + kiota <= qpos, jnp.where(s >= thr, 1.0, 0.0), 0.0)
        bias_ref[rows(c), :] = jnp.where(sel > 0.0, 0.0, _NEG_INF)
        return part + _fold_rows(sel, jnp.add)

    n_ge = jnp.sum(for_chunks(bias_chunk, jnp.zeros((SUBLANES, t), _F32)), axis=0, keepdims=True)

    @pl.when(jnp.max(n_ge) > n_sel)
    def _():
        nxt = _ordered_int_to_float(base + 1)
        tied = n_ge > n_sel
        n_above = count(lambda s, c: jnp.where(s >= nxt, 1.0, 0.0))
        need = jnp.where(tied, n_sel - n_above, 0.0)
        no_index = float(k_ref.shape[1])
        front_v = jnp.where(tied, jnp.inf, _NEG_INF)
        front_i = jnp.where(tied, -1.0, no_index)

        def beyond(s, kposf, front_v, front_i):
            after = jnp.where(s < front_v, 1.0,
                              jnp.where(s == front_v, jnp.where(kposf > front_i, 1.0, 0.0), 0.0))
            return jnp.where(s >= thr, jnp.where(s < nxt, after, 0.0), 0.0)

        def kposf(c):
            return (c * KEY_CHUNK + kiota).astype(_F32)

        def advance(_, carry):
            front_v, front_i, need = carry

            def best_value(c, part):
                s = score_ref[rows(c), :]
                cand = jnp.where(beyond(s, kposf(c), front_v, front_i) > 0.0, s, _NEG_INF)
                return jnp.maximum(part, _fold_rows(cand, jnp.maximum))

            v = jnp.max(for_chunks(best_value, jnp.full((SUBLANES, t), _NEG_INF, _F32)),
                        axis=0, keepdims=True)

            def first_index(c, part):
                s = score_ref[rows(c), :]
                hit = jnp.where(s == v, beyond(s, kposf(c), front_v, front_i), 0.0)
                cand = jnp.where(hit > 0.0, kposf(c), no_index)
                return jnp.minimum(part, _fold_rows(cand, jnp.minimum))

            i = jnp.min(for_chunks(first_index, jnp.full((SUBLANES, t), no_index, _F32)),
                        axis=0, keepdims=True)
            active = need > 0.0
            return (jnp.where(active, v, front_v), jnp.where(active, i, front_i),
                    jnp.where(active, need - 1.0, need))

        front_v, front_i, _ = lax.fori_loop(0, jnp.max(need).astype(jnp.int32), advance,
                                            (front_v, front_i, need))

        def tie_chunk(c, carry):
            s = score_ref[rows(c), :]
            upto = jnp.where(s > front_v, 0.0,
                             jnp.where(s == front_v,
                                       jnp.where(kposf(c) <= front_i, 0.0, _NEG_INF), _NEG_INF))
            keep = jnp.where(s >= nxt, 0.0, jnp.where(s >= thr, upto, _NEG_INF))
            bias_ref[rows(c), :] = jnp.where(c * KEY_CHUNK + kiota <= qpos, keep, _NEG_INF)
            return carry

        for_chunks(tie_chunk, 0)

    grp = B_HEADS // B_KV_HEADS
    gw = grp * t
    exp2_scale = (B_HEAD_DIM ** -0.5) * 1.4426950408889634

    def logit_chunk(c, mx):
        bias = jnp.concatenate([bias_ref[rows(c), :]] * grp, axis=1)
        folded = []
        for g in range(B_KV_HEADS):
            qg = jnp.concatenate(
                [q_ref[0, :, (g * grp + hh) * B_HEAD_DIM:(g * grp + hh + 1) * B_HEAD_DIM]
                 for hh in range(grp)], axis=0)
            l = lax.dot_general(k_ref[0, rows(c), g * B_HEAD_DIM:(g + 1) * B_HEAD_DIM], qg,
                                _NT_DIMS, preferred_element_type=_F32) + bias
            logit_ref[rows(c), g * gw:(g + 1) * gw] = l
            folded.append(_fold_rows(l, jnp.maximum))
        return jnp.maximum(mx, jnp.concatenate(folded, axis=1))

    mx = for_chunks(logit_chunk, jnp.full((SUBLANES, B_HEADS * t), _NEG_INF, _F32))
    mx = jnp.max(mx, axis=0, keepdims=True)

    acc_ref[...] = jnp.zeros_like(acc_ref)
    ones_rows = jnp.ones((acc_ref.shape[0] - B_HEAD_DIM, KEY_CHUNK), _BF16)

    def pv_chunk(c, carry):
        for g in range(B_KV_HEADS):
            cols = slice(g * gw, (g + 1) * gw)
            p = jnp.exp2((logit_ref[rows(c), cols] - mx[:, cols]) * exp2_scale)
            vt_ext = jnp.concatenate(
                [vt_ref[0, c, g * B_HEAD_DIM:(g + 1) * B_HEAD_DIM, :], ones_rows], axis=0)
            acc_ref[:, cols] += jnp.dot(vt_ext, p.astype(_BF16), preferred_element_type=_F32)
        return carry

    for_chunks(pv_chunk, 0)
    out_t = acc_ref[:B_HEAD_DIM, :] / acc_ref[B_HEAD_DIM:B_HEAD_DIM + 1, :]
    for h in range(B_HEADS):
        o_ref[0, :, h * B_HEAD_DIM:(h + 1) * B_HEAD_DIM] = (
            out_t[:, h * t:(h + 1) * t].T.astype(o_ref.dtype))


def _dsa_attention(q, k, vt, qi, ki, wit, side_job=None):
    bsz, seq, _ = q.shape
    assert Q_BLOCK == LANES and seq % KEY_CHUNK == 0
    n_sel = min(TOPK_MAX, seq // 4)
    n_blk = seq // Q_BLOCK

    def q_spec(width):
        return pl.BlockSpec((1, Q_BLOCK, width), lambda b, i: (b, i, 0))

    def kv_spec(width):
        return pl.BlockSpec((1, seq, width), lambda b, i: (b, 0, 0))

    def per_step(spec):
        return pl.BlockSpec(spec.block_shape, lambda b, i: spec.index_map(b * n_blk + i))

    side = None if side_job is None else side_job(bsz * n_blk)
    side_in = [] if side is None else [per_step(s) for s in side.in_specs]
    side_out = [] if side is None else [per_step(s) for s in side.out_specs]
    outs = pl.pallas_call(
        functools.partial(_dsa_kernel, n_sel=n_sel,
                          side_body=None if side is None else side.body,
                          n_side_in=len(side_in), n_side_out=len(side_out)),
        grid=(bsz, n_blk),
        in_specs=[q_spec(IDX_HEADS * LANES),
                  pl.BlockSpec((1, IDX_HEADS, Q_BLOCK), lambda b, i: (b, 0, i)),
                  kv_spec(LANES), q_spec(B_WIDTH), kv_spec(B_KV_WIDTH),
                  pl.BlockSpec((1, seq // KEY_CHUNK, B_KV_WIDTH, KEY_CHUNK),
                               lambda b, i: (b, 0, 0, 0))] + side_in,
        out_specs=[q_spec(B_WIDTH)] + side_out,
        out_shape=[jax.ShapeDtypeStruct((bsz, seq, B_WIDTH), _BF16)]
        + ([] if side is None else list(side.out_shape)),
        scratch_shapes=[pltpu.VMEM((seq, Q_BLOCK), _F32), pltpu.VMEM((seq, Q_BLOCK), _F32),
                        pltpu.VMEM((seq, B_HEADS * Q_BLOCK), _F32),
                        pltpu.VMEM((B_HEAD_DIM + 2 * SUBLANES, B_HEADS * Q_BLOCK), _F32)],
        compiler_params=_compiler_params(2),
        name="dsa_attention",
    )(qi, wit, ki, q, k, vt, *(() if side is None else side.arrays))
    return outs[0] if side is None else tuple(outs)


def _xattn_kernel(q_ref, k_ref, v_ref, o_ref):
    scale = X_HEAD_DIM ** -0.5
    for h in range(X_HEADS):
        hs = slice(h * X_HEAD_DIM, (h + 1) * X_HEAD_DIM)
        logits = lax.dot_general(q_ref[0, :, hs], k_ref[0, :, hs], _NT_DIMS,
                                 preferred_element_type=_F32) * scale
        mx = jnp.max(logits, axis=-1, keepdims=True)
        p = jnp.exp(logits - mx)
        denom = jnp.sum(p, axis=-1, keepdims=True)
        o = jnp.dot(p.astype(_BF16), v_ref[0, :, hs], preferred_element_type=_F32)
        o_ref[0, :, hs] = (o / denom).astype(o_ref.dtype)


def _cross_attention(q, k, v):
    bsz, seq, _ = q.shape
    tq = ROW_BLOCK["cross_attention"]
    m = k.shape[1]
    mem_spec = pl.BlockSpec((1, m, X_WIDTH), lambda b, i: (b, 0, 0))
    return pl.pallas_call(
        _xattn_kernel,
        grid=(bsz, seq // tq),
        in_specs=[pl.BlockSpec((1, tq, X_WIDTH), lambda b, i: (b, i, 0)), mem_spec, mem_spec],
        out_specs=pl.BlockSpec((1, tq, X_WIDTH), lambda b, i: (b, i, 0)),
        out_shape=jax.ShapeDtypeStruct((bsz, seq, X_WIDTH), _BF16),
        compiler_params=_compiler_params(2),
        name="cross_attention",
    )(q, k, v)


def _norm_proj_kernel(x_ref, g_ref, w_ref, o_ref):
    x = x_ref[...]
    ms = jnp.mean(x * x, axis=-1, keepdims=True)
    h = ((x * lax.rsqrt(ms + EPS)) * g_ref[...]).astype(_BF16)
    o_ref[...] = jnp.dot(h, w_ref[...], preferred_element_type=_F32).astype(o_ref.dtype)


def _norm_proj(x, g, w_bf16, out_dtype, name):
    m, d = x.shape
    tm = ROW_BLOCK["norm_proj"]
    n = w_bf16.shape[1]
    return pl.pallas_call(
        _norm_proj_kernel,
        grid=(m // tm,),
        in_specs=[pl.BlockSpec((tm, d), lambda i: (i, 0)),
                  pl.BlockSpec((1, d), lambda i: (0, 0)),
                  pl.BlockSpec((d, n), lambda i: (0, 0), pipeline_mode=pl.Buffered(1))],
        out_specs=pl.BlockSpec((tm, n), lambda i: (i, 0)),
        out_shape=jax.ShapeDtypeStruct((m, n), out_dtype),
        compiler_params=_compiler_params(1),
        name=name,
    )(x, g.reshape(1, d), w_bf16)


def _xattn_out_kernel(o_ref, w_ref, x_ref, g_ref, x_out_ref, h_out_ref):
    x = x_ref[...] + jnp.dot(o_ref[...], w_ref[...], preferred_element_type=_F32)
    x_out_ref[...] = x
    ms = jnp.mean(x * x, axis=-1, keepdims=True)
    h_out_ref[...] = ((x * lax.rsqrt(ms + EPS)) * g_ref[...]).astype(h_out_ref.dtype)


def _xattn_out_and_norm(o, w_bf16, x, g):
    m, k = o.shape
    tm = ROW_BLOCK["xattn_out_norm"]
    d = x.shape[1]
    row_spec = pl.BlockSpec((tm, d), lambda i: (i, 0))
    return pl.pallas_call(
        _xattn_out_kernel,
        grid=(m // tm,),
        in_specs=[pl.BlockSpec((tm, k), lambda i: (i, 0)),
                  pl.BlockSpec((k, d), lambda i: (0, 0), pipeline_mode=pl.Buffered(1)),
                  row_spec,
                  pl.BlockSpec((1, d), lambda i: (0, 0))],
        out_specs=[row_spec, row_spec],
        out_shape=[jax.ShapeDtypeStruct((m, d), _F32), jax.ShapeDtypeStruct((m, d), _BF16)],
        compiler_params=_compiler_params(1),
        name="xattn_out_norm",
    )(o, w_bf16, x, g.reshape(1, d))


def kernel(x, mem, positions, norm_mix_g, w_in, a_norm_g, a_spatial_w, a_spatial_b, p_a, p_b,
           w_out, norm_x_g, norm_mem_g, xq_w, xk_w, xv_w, xo_w, norm_ffn_g, ffn_w1, ffn_w3,
           ffn_w2, final_norm_g):
    bsz, seq, d = x.shape
    m = bsz * seq
    depth = w_in.shape[0]
    ffn_hidden = ffn_w1.shape[-1]

    xf = x.reshape(m, d)
    mem_f = mem.reshape(bsz * MEM_LEN, d)
    w_in_t = jnp.transpose(w_in, (0, 2, 1))
    for l in range(depth):
        def w_in_cols(col0):
            return Rhs(w_in_t, l, col0, transposed=True)

        h = _rmsnorm(xf, norm_mix_g[l], _BF16)
        att = _fused_matmul("proj_att", [h], [(0, w_in_cols(ZA_END))], [], _ep_identity,
                            ATT_WIDTH, _F32)
        za, q, k, vt, qi, ki, wit = _fused_matmul(
            "proj_za", [h], [(0, w_in_cols(0))], [], _ep_identity, ZA_END, _F32,
            side_job=functools.partial(_prep_side_job, att.reshape(bsz, seq, ATT_WIDTH),
                                       positions))
        def weight_casts(n_steps):
            tm = m // n_steps
            assert tm * n_steps == m and tm % CHUNK == 0
            gating = SideJob(
                _gating_kernel,
                (za, a_norm_g[l].reshape(1, A_WIDTH), a_spatial_w[l], a_spatial_b[l].T),
                (pl.BlockSpec((tm, 2 * A_WIDTH), lambda s: (s, 0)),
                 pl.BlockSpec((1, A_WIDTH), lambda s: (0, 0)),
                 pl.BlockSpec((A_GROUPS, CHUNK, CHUNK), lambda s: (0, 0, 0)),
                 pl.BlockSpec((CHUNK, A_GROUPS), lambda s: (0, 0))),
                (pl.BlockSpec((tm, A_WIDTH), lambda s: (s, 0)),),
                (jax.ShapeDtypeStruct((m, A_WIDTH), _BF16),))
            return _merge_side_jobs(
                _cast_side_job(p_a, l, n_steps), _cast_side_job(p_b, l, n_steps),
                _cast_side_job(w_in_t, l, n_steps, row0=ATT_END, rows=2 * d),
                _cast_side_job(w_out, l, n_steps), _cast_side_job(xq_w, l, n_steps),
                _cast_side_job(xk_w, l, n_steps), _cast_side_job(xv_w, l, n_steps), gating)

        (y_b, pa_bf16, pb_bf16, gate_wt_bf16, wo_bf16, xq_bf16, xk_bf16, xv_bf16,
         y_a) = _dsa_attention(q, k, vt, qi, ki, wit, side_job=weight_casts)

        def ffn_up_weight_casts(n_steps):
            return _merge_side_jobs(_cast_side_job(ffn_w1, l, n_steps),
                                    _cast_side_job(ffn_w3, l, n_steps))

        merged, w1_bf16, w3_bf16 = _fused_matmul(
            "gated_merge", [y_a, y_b.reshape(m, B_WIDTH), h],
            [(0, Rhs(pa_bf16)), (1, Rhs(pb_bf16)), (2, Rhs(gate_wt_bf16, transposed=True)),
             (2, Rhs(gate_wt_bf16, col0=d, transposed=True))],
            [], _ep_gated_merge, d, _BF16, side_job=ffn_up_weight_casts)
        x1, xo_bf16 = _fused_matmul("out_proj", [merged], [(0, Rhs(wo_bf16))], [xf],
                                    _ep_residual, d, _F32,
                                    side_job=functools.partial(_cast_side_job, xo_w, l))

        qx = _norm_proj(x1, norm_x_g[l], xq_bf16, _BF16, "xattn_q")
        kx = _norm_proj(mem_f, norm_mem_g[l], xk_bf16, _BF16, "xattn_k")
        vx = _norm_proj(mem_f, norm_mem_g[l], xv_bf16, _BF16, "xattn_v")
        ox = _cross_attention(qx.reshape(bsz, seq, X_WIDTH),
                              kx.reshape(bsz, MEM_LEN, X_WIDTH),
                              vx.reshape(bsz, MEM_LEN, X_WIDTH)).reshape(m, X_WIDTH)
        x2, h2 = _xattn_out_and_norm(ox, xo_bf16, x1, norm_ffn_g[l])

        act, w2_bf16 = _fused_matmul("ffn_up", [h2],
                                     [(0, Rhs(w1_bf16)), (0, Rhs(w3_bf16))],
                                     [], _ep_swiglu, ffn_hidden, _BF16,
                                     side_job=functools.partial(_cast_side_job, ffn_w2, l))
        xf = _fused_matmul("ffn_down", [act], [(0, Rhs(w2_bf16))], [x2], _ep_residual, d, _F32)

    return _rmsnorm(xf, final_norm_g, _F32).reshape(bsz, seq, d)
```

```python
import functools
from typing import Callable, NamedTuple, Optional

import jax
import jax.numpy as jnp
from jax import lax
from jax.experimental import pallas as pl
from jax.experimental.pallas import tpu as pltpu

D_MODEL = 4096
MEM_LEN = 256
EPS = 1e-6
ROPE_THETA = 500000.0
CHUNK = 128
A_GROUPS = 16
A_WIDTH = D_MODEL // 2
A_GROUP_DIM = A_WIDTH // A_GROUPS
B_HEADS = 16
B_HEAD_DIM = 128
B_KV_HEADS = 4
B_WIDTH = B_HEADS * B_HEAD_DIM
B_KV_WIDTH = B_KV_HEADS * B_HEAD_DIM
B_ROT = B_HEAD_DIM // 4
IDX_HEADS = 16
IDX_DIM = 64
IDX_WIDTH = IDX_HEADS * IDX_DIM
IDX_ROT = IDX_DIM // 4
TOPK_MAX = 256
Q_BLOCK = 128
X_HEADS = 4
X_HEAD_DIM = 256
X_WIDTH = X_HEADS * X_HEAD_DIM
IN_SIZES = (2 * A_WIDTH, B_WIDTH, B_KV_WIDTH, B_KV_WIDTH, IDX_WIDTH, IDX_DIM, IDX_HEADS,
            2 * D_MODEL)

LANES = 128
SUBLANES = 8
KEY_CHUNK = 512
PROJ_TN = 512
ZA_END = IN_SIZES[0]
ATT_END = ZA_END + sum(IN_SIZES[1:7])
ATT_Q0 = 0
ATT_K0 = ATT_Q0 + B_WIDTH
ATT_V0 = ATT_K0 + B_KV_WIDTH
ATT_QI0 = ATT_V0 + B_KV_WIDTH
ATT_KW0 = ATT_QI0 + IDX_WIDTH
ATT_WIDTH = -(-(ATT_END - ZA_END) // PROJ_TN) * PROJ_TN
assert ZA_END % PROJ_TN == 0 and ATT_KW0 + LANES <= ATT_WIDTH

VMEM_LIMIT_BYTES = 60 * 1024 * 1024


class Tile(NamedTuple):
    tm: int
    tn: int
    row_splits: int = 1


TILES = {
    "proj_att": Tile(1024, PROJ_TN),
    "proj_za": Tile(1024, 512),
    "gated_merge": Tile(1024, 256, row_splits=4),
    "out_proj": Tile(1024, 512),
    "ffn_up": Tile(2048, 256, row_splits=4),
    "ffn_down": Tile(512, 512),
}
ROW_BLOCK = {
    "rmsnorm": 512,
    "cross_attention": 1024,
    "norm_proj": 512,
    "xattn_out_norm": 512,
}

_BF16 = jnp.bfloat16
_F32 = jnp.float32
_NEG_INF = float("-inf")
_NN_DIMS = (((1,), (0,)), ((), ()))
_NT_DIMS = (((1,), (1,)), ((), ()))


def _compiler_params(n_axes):
    return pltpu.CompilerParams(
        dimension_semantics=("arbitrary",) * n_axes,
        vmem_limit_bytes=VMEM_LIMIT_BYTES,
    )


def _rmsnorm_kernel(x_ref, g_ref, o_ref):
    x = x_ref[...]
    ms = jnp.mean(x * x, axis=-1, keepdims=True)
    y = x * lax.rsqrt(ms + EPS)
    o_ref[...] = (y * g_ref[...]).astype(o_ref.dtype)


def _rmsnorm(x, g, out_dtype):
    m, d = x.shape
    tm = ROW_BLOCK["rmsnorm"]
    return pl.pallas_call(
        _rmsnorm_kernel,
        grid=(m // tm,),
        in_specs=[pl.BlockSpec((tm, d), lambda i: (i, 0)),
                  pl.BlockSpec((1, d), lambda i: (0, 0))],
        out_specs=pl.BlockSpec((tm, d), lambda i: (i, 0)),
        out_shape=jax.ShapeDtypeStruct((m, d), out_dtype),
        compiler_params=_compiler_params(1),
        name="rmsnorm",
    )(x, g.reshape(1, d))


def _ep_identity(accs, extras):
    return accs[0]


def _ep_residual(accs, extras):
    return extras[0] + accs[0]


def _ep_gated_merge(accs, extras):
    y_a, y_b, gate_a, gate_b = accs
    return jax.nn.sigmoid(gate_a) * y_a + jax.nn.sigmoid(gate_b) * y_b


def _ep_swiglu(accs, extras):
    return jax.nn.silu(accs[0]) * accs[1]


class Rhs(NamedTuple):
    array: jax.Array
    layer: Optional[int] = None
    col0: int = 0
    transposed: bool = False


def _rhs_spec(r, tn):
    if r.transposed and r.layer is None:
        assert r.col0 % tn == 0
        off = r.col0 // tn
        return pl.BlockSpec((tn, r.array.shape[1]), lambda i, j: (j + off, 0))
    if r.transposed:
        k = r.array.shape[2]
        if r.col0 % tn == 0:
            return pl.BlockSpec((None, tn, k), lambda i, j: (r.layer, j + r.col0 // tn, 0))
        assert r.col0 % SUBLANES == 0
        return pl.BlockSpec((pl.Element(1), pl.Element(tn), pl.Element(k)),
                            lambda i, j: (r.layer, (r.col0 // SUBLANES + j * (tn // SUBLANES))
                                          * SUBLANES, 0))
    assert r.col0 % tn == 0
    off = r.col0 // tn
    if r.layer is None:
        return pl.BlockSpec((r.array.shape[0], tn), lambda i, j: (0, j + off))
    return pl.BlockSpec((None, r.array.shape[1], tn), lambda i, j: (r.layer, 0, j + off))


class SideJob(NamedTuple):
    body: Callable
    arrays: tuple
    in_specs: tuple
    out_specs: tuple
    out_shape: tuple


def _cast_body(x_ref, o_ref):
    x = x_ref[0] if len(x_ref.shape) == 3 else x_ref[...]
    o_ref[...] = x.astype(o_ref.dtype)


def _cast_side_job(w_stack, layer, n_steps, row0=0, rows=None):
    rows = w_stack.shape[1] if rows is None else rows
    cols = w_stack.shape[2]
    slab = rows // n_steps
    assert slab * n_steps == rows and slab % (2 * SUBLANES) == 0
    if row0 % slab == 0:
        in_spec = pl.BlockSpec((None, slab, cols), lambda s: (layer, s + row0 // slab, 0))
    else:
        assert row0 % SUBLANES == 0
        in_spec = pl.BlockSpec(
            (pl.Element(1), pl.Element(slab), pl.Element(cols)),
            lambda s: (layer, (row0 // SUBLANES + s * (slab // SUBLANES)) * SUBLANES, 0))
    return SideJob(_cast_body, (w_stack,), (in_spec,),
                   (pl.BlockSpec((slab, cols), lambda s: (s, 0)),),
                   (jax.ShapeDtypeStruct((rows, cols), _BF16),))


def _merge_side_jobs(*jobs):
    n_in = [len(j.in_specs) for j in jobs]
    n_out = [len(j.out_specs) for j in jobs]

    def body(*refs):
        ins, outs = refs[:sum(n_in)], refs[sum(n_in):]
        i = o = 0
        for j, ni, no in zip(jobs, n_in, n_out):
            j.body(*ins[i:i + ni], *outs[o:o + no])
            i, o = i + ni, o + no

    return SideJob(body, sum((j.arrays for j in jobs), ()), sum((j.in_specs for j in jobs), ()),
                   sum((j.out_specs for j in jobs), ()), sum((j.out_shape for j in jobs), ()))


def _mm_kernel(*refs, n_lhs, pair_lhs, rhs_transposed, n_extra, epilogue, side_body, n_side_in,
               row_splits):
    n_rhs = len(pair_lhs)
    lhs_refs = refs[:n_lhs]
    rhs_refs = refs[n_lhs:n_lhs + n_rhs]
    n_in = n_lhs + n_rhs + n_extra
    extra_refs = refs[n_lhs + n_rhs:n_in]
    o_ref = refs[n_in + n_side_in]
    if side_body is not None:
        side_body(*refs[n_in:n_in + n_side_in], *refs[n_in + n_side_in + 1:])
    weights = [(r[0] if len(r.shape) == 3 else r[...]).astype(_BF16) for r in rhs_refs]
    rows_per_split = o_ref.shape[0] // row_splits
    for s in range(row_splits):
        rows = slice(s * rows_per_split, (s + 1) * rows_per_split)
        accs = [lax.dot_general(lhs_refs[li][rows, :], w, _NT_DIMS if tr else _NN_DIMS,
                                preferred_element_type=_F32)
                for li, w, tr in zip(pair_lhs, weights, rhs_transposed)]
        o_ref[rows, :] = epilogue(accs, [e[rows, :] for e in extra_refs]).astype(o_ref.dtype)


def _fused_matmul(name, lhs, pairs, extras, epilogue, n_out, out_dtype, side_job=None):
    tm, tn, row_splits = TILES[name]
    m = lhs[0].shape[0]
    assert m % tm == 0 and n_out % tn == 0
    grid = (m // tm, n_out // tn)
    in_specs = [pl.BlockSpec((tm, a.shape[1]), lambda i, j: (i, 0)) for a in lhs]
    in_specs += [_rhs_spec(r, tn) for _, r in pairs]
    in_specs += [pl.BlockSpec((tm, tn), lambda i, j: (i, j)) for _ in extras]
    out_specs = [pl.BlockSpec((tm, tn), lambda i, j: (i, j))]
    out_shape = [jax.ShapeDtypeStruct((m, n_out), out_dtype)]
    operands = list(lhs) + [p[1].array for p in pairs] + list(extras)
    side = None if side_job is None else side_job(grid[0] * grid[1])
    if side is not None:
        def per_step(spec):
            return pl.BlockSpec(spec.block_shape,
                                lambda i, j: spec.index_map(i * grid[1] + j))
        in_specs += [per_step(s) for s in side.in_specs]
        out_specs += [per_step(s) for s in side.out_specs]
        out_shape += list(side.out_shape)
        operands += list(side.arrays)
    kern = functools.partial(_mm_kernel, n_lhs=len(lhs), pair_lhs=tuple(p[0] for p in pairs),
                             rhs_transposed=tuple(p[1].transposed for p in pairs),
                             n_extra=len(extras), epilogue=epilogue,
                             side_body=None if side is None else side.body,
                             n_side_in=0 if side is None else len(side.in_specs),
                             row_splits=row_splits)
    outs = pl.pallas_call(
        kern,
        grid=grid,
        in_specs=in_specs,
        out_specs=out_specs,
        out_shape=out_shape,
        compiler_params=_compiler_params(2),
        name=name,
    )(*operands)
    return outs[0] if side is None else tuple(outs)


def _gating_kernel(za_ref, ng_ref, w_ref, bt_ref, o_ref):
    row = lax.broadcasted_iota(jnp.int32, (CHUNK, CHUNK), 0)
    col = lax.broadcasted_iota(jnp.int32, (CHUNK, CHUNK), 1)
    causal = col <= row
    for c in range(za_ref.shape[0] // CHUNK):
        rows = slice(c * CHUNK, (c + 1) * CHUNK)
        z = jax.nn.gelu(za_ref[rows, :])
        u = z[:, :A_WIDTH]
        v = z[:, A_WIDTH:]
        ms = jnp.mean(v * v, axis=-1, keepdims=True)
        vn = ((v * lax.rsqrt(ms + EPS)) * ng_ref[...]).astype(_BF16)
        for g in range(A_GROUPS):
            sl = slice(g * A_GROUP_DIM, (g + 1) * A_GROUP_DIM)
            w = jnp.where(causal, w_ref[g], 0.0).astype(_BF16)
            s = jnp.dot(w, vn[:, sl], preferred_element_type=_F32) + bt_ref[:, g:g + 1]
            o_ref[rows, sl] = (u[:, sl] * s).astype(o_ref.dtype)


def _gating_side_job(za, norm_g, w_s, b_s, n_steps):
    m = za.shape[0]
    tm = m // n_steps
    assert tm * n_steps == m and tm % CHUNK == 0
    return SideJob(_gating_kernel, (za, norm_g.reshape(1, A_WIDTH), w_s, b_s.T),
                   (pl.BlockSpec((tm, 2 * A_WIDTH), lambda s: (s, 0)),
                    pl.BlockSpec((1, A_WIDTH), lambda s: (0, 0)),
                    pl.BlockSpec((A_GROUPS, CHUNK, CHUNK), lambda s: (0, 0, 0)),
                    pl.BlockSpec((CHUNK, A_GROUPS), lambda s: (0, 0))),
                   (pl.BlockSpec((tm, A_WIDTH), lambda s: (s, 0)),),
                   (jax.ShapeDtypeStruct((m, A_WIDTH), _BF16),))


def _lane_inv_freq(rot_dim, period):
    inv_freq = ROPE_THETA ** (-jnp.arange(0, rot_dim, 2, dtype=_F32) / rot_dim)
    head = jnp.concatenate([inv_freq, inv_freq, jnp.zeros((period - rot_dim,), _F32)])
    return jnp.tile(head, LANES // period).reshape(1, LANES)


def _rope_tables(pos, inv_freq_lanes, rot_dim, period):
    half = rot_dim // 2
    ang = pos * inv_freq_lanes
    cos, sin = jnp.cos(ang), jnp.sin(ang)
    lane = lax.broadcasted_iota(jnp.int32, ang.shape, 1) & (period - 1)
    c = jnp.where(lane < rot_dim, cos, 1.0)
    s1 = jnp.where(lane >= half, jnp.where(lane < rot_dim, sin, 0.0), 0.0)
    s2 = jnp.where(lane < half, -sin, 0.0)
    return c, s1, s2


def _rope(x, c, s1, s2, half):
    return x * c + pltpu.roll(x, half, 1) * s1 + pltpu.roll(x, LANES - half, 1) * s2


def _prep_kernel(att_ref, pos_ref, fb_ref, fi_ref,
                 q_ref, k_ref, vt_ref, qi_ref, ki_ref, wit_ref):
    pos = pos_ref[0]
    cb, s1b, s2b = _rope_tables(pos, fb_ref[...], B_ROT, B_HEAD_DIM)
    ci, s1i, s2i = _rope_tables(pos, fi_ref[...], IDX_ROT, IDX_DIM)
    for h in range(B_HEADS):
        x = att_ref[0, :, ATT_Q0 + h * LANES:ATT_Q0 + (h + 1) * LANES]
        q_ref[0, :, h * LANES:(h + 1) * LANES] = _rope(x, cb, s1b, s2b, B_ROT // 2).astype(_BF16)
    for h in range(B_KV_HEADS):
        x = att_ref[0, :, ATT_K0 + h * LANES:ATT_K0 + (h + 1) * LANES]
        k_ref[0, :, h * LANES:(h + 1) * LANES] = _rope(x, cb, s1b, s2b, B_ROT // 2).astype(_BF16)
    vt_ref[0, 0] = att_ref[0, :, ATT_V0:ATT_V0 + B_KV_WIDTH].T.astype(_BF16)
    lane = lax.broadcasted_iota(jnp.int32, ci.shape, 1)
    is_ki = lane < IDX_DIM
    heads_per_group = LANES // IDX_DIM
    for j in range(IDX_WIDTH // LANES):
        x = att_ref[0, :, ATT_QI0 + j * LANES:ATT_QI0 + (j + 1) * LANES]
        y = _rope(x, ci, s1i, s2i, IDX_ROT // 2)
        for r in range(heads_per_group):
            h = j * heads_per_group + r
            yr = y if r == 0 else pltpu.roll(y, LANES - r * IDX_DIM, 1)
            qi_ref[0, :, h * LANES:(h + 1) * LANES] = jnp.where(is_ki, yr, 0.0).astype(_BF16)
    x = att_ref[0, :, ATT_KW0:ATT_KW0 + LANES]
    y = _rope(x, jnp.where(is_ki, ci, 1.0), jnp.where(is_ki, s1i, 0.0),
              jnp.where(is_ki, s2i, 0.0), IDX_ROT // 2)
    ki_ref[0] = jnp.where(is_ki, y, 0.0).astype(_BF16)
    wit_ref[0] = y.T[IDX_DIM:IDX_DIM + IDX_HEADS, :]


def _prep_side_job(att, positions, n_steps):
    bsz, seq, _ = att.shape
    ts = bsz * seq // n_steps
    assert ts * n_steps == bsz * seq and seq % ts == 0 and KEY_CHUNK % ts == 0
    assert ts % (2 * SUBLANES) == 0
    per_seq = seq // ts
    per_chunk = KEY_CHUNK // ts

    def tokens(width):
        return pl.BlockSpec((1, ts, width), lambda s: (s // per_seq, s % per_seq, 0))

    def out(width, dtype):
        return tokens(width), jax.ShapeDtypeStruct((bsz, seq, width), dtype)

    vt_out = (pl.BlockSpec((1, 1, B_KV_WIDTH, ts),
                           lambda s: (s // per_seq, (s % per_seq) // per_chunk, 0,
                                      (s % per_seq) % per_chunk)),
              jax.ShapeDtypeStruct((bsz, seq // KEY_CHUNK, B_KV_WIDTH, KEY_CHUNK), _BF16))
    wit_out = (pl.BlockSpec((1, IDX_HEADS, ts), lambda s: (s // per_seq, 0, s % per_seq)),
               jax.ShapeDtypeStruct((bsz, IDX_HEADS, seq), _F32))
    outs = [out(B_WIDTH, _BF16), out(B_KV_WIDTH, _BF16), vt_out,
            out(IDX_HEADS * LANES, _BF16), out(LANES, _BF16), wit_out]
    freq_spec = pl.BlockSpec((1, LANES), lambda s: (0, 0))
    return SideJob(_prep_kernel,
                   (att, positions.astype(_F32).reshape(bsz, seq, 1),
                    _lane_inv_freq(B_ROT, B_HEAD_DIM), _lane_inv_freq(IDX_ROT, IDX_DIM)),
                   (tokens(ATT_WIDTH), tokens(1), freq_spec, freq_spec),
                   tuple(o[0] for o in outs), tuple(o[1] for o in outs))


def _ordered_int_to_float(key):
    bits = jnp.where(key < 0, key ^ jnp.int32(0x7FFFFFFF), key)
    return lax.bitcast_convert_type(bits, _F32)


def _fold_rows(x, op):
    while x.shape[0] > SUBLANES:
        half = x.shape[0] // 2
        x = op(x[:half], x[half:])
    return x


_DSA_N_IN = 6


def _dsa_kernel(*refs, n_sel, side_body, n_side_in, n_side_out):
    o_at = _DSA_N_IN + n_side_in
    if side_body is not None:
        side_body(*refs[_DSA_N_IN:o_at], *refs[o_at + 1:o_at + 1 + n_side_out])
    own = refs[:_DSA_N_IN] + (refs[o_at],) + refs[o_at + 1 + n_side_out:]
    seq = own[4].shape[1]
    n_chunks = (pl.program_id(1) * Q_BLOCK + Q_BLOCK + KEY_CHUNK - 1) // KEY_CHUNK
    for n in range(1, seq // KEY_CHUNK + 1):
        pl.when(n_chunks == n)(functools.partial(_dsa_block, *own, n_sel=n_sel, n_chunks=n))


def _dsa_block(qi_ref, wit_ref, ki_ref, q_ref, k_ref, vt_ref, o_ref,
               score_ref, bias_ref, logit_ref, acc_ref, *, n_sel, n_chunks):
    t = Q_BLOCK
    blk = pl.program_id(1)
    qpos = blk * t + lax.broadcasted_iota(jnp.int32, (1, t), 1)
    kiota = lax.broadcasted_iota(jnp.int32, (KEY_CHUNK, 1), 0)
    idx_scale = (IDX_DIM ** -0.5) * (IDX_HEADS ** -0.5)

    def rows(c):
        return slice(c * KEY_CHUNK, (c + 1) * KEY_CHUNK)

    def for_chunks(body, carry):
        for c in range(n_chunks):
            carry = body(c, carry)
        return carry

    def indexer_chunk(c, carry):
        ki = ki_ref[0, rows(c), :]
        acc = jnp.zeros((KEY_CHUNK, t), _F32)
        for h in range(0, IDX_HEADS, 2):
            qpair = jnp.concatenate([qi_ref[0, :, h * LANES:(h + 1) * LANES],
                                     qi_ref[0, :, (h + 1) * LANES:(h + 2) * LANES]], axis=0)
            d = lax.dot_general(ki, qpair, _NT_DIMS, preferred_element_type=_F32)
            acc = acc + wit_ref[0, h:h + 1, :] * jnp.maximum(d[:, :t], 0.0)
            acc = acc + wit_ref[0, h + 1:h + 2, :] * jnp.maximum(d[:, t:], 0.0)
        kpos = c * KEY_CHUNK + kiota
        score_ref[rows(c), :] = jnp.where(kpos <= qpos, acc * idx_scale + 0.0, _NEG_INF)
        return carry

    for_chunks(indexer_chunk, 0)

    def count(indicator):
        def chunk(c, part):
            return part + _fold_rows(indicator(score_ref[rows(c), :], c), jnp.add)
        part = for_chunks(chunk, jnp.zeros((SUBLANES, t), _F32))
        return jnp.sum(part, axis=0, keepdims=True)

    def count_ge(cand_f):
        return count(lambda s, c: jnp.where(s >= cand_f, 1.0, 0.0))

    def search():
        int_min = jnp.int32(-2 ** 31)
        base = jnp.where(count_ge(jnp.zeros((1, t), _F32)) >= n_sel, jnp.int32(0), int_min)

        def search_step(it, base):
            cand = base | lax.shift_left(jnp.int32(1), jnp.int32(30) - it)
            cnt = count_ge(_ordered_int_to_float(cand))
            return jnp.where(cnt >= n_sel, cand, base)

        return lax.fori_loop(0, 31, search_step, base)

    if (n_chunks - 1) * KEY_CHUNK >= n_sel:
        base = search()
    else:
        base = lax.cond((blk + 1) * t <= n_sel, lambda: jnp.zeros((1, t), jnp.int32), search)
    thr = jnp.where(qpos + 1 <= n_sel, _NEG_INF, _ordered_int_to_float(base))

    def bias_chunk(c, part):
        s = score_ref[rows(c), :]
        sel = jnp.where(c * KEY_CHUNK + kiota <= qpos, jnp.where(s >= thr, 1.0, 0.0), 0.0)
        bias_ref[rows(c), :] = jnp.where(sel > 0.0, 0.0, _NEG_INF)
        return part + _fold_rows(sel, jnp.add)

    n_ge = jnp.sum(for_chunks(bias_chunk, jnp.zeros((SUBLANES, t), _F32)), axis=0, keepdims=True)

    @pl.when(jnp.max(n_ge) > n_sel)
    def _():
        nxt = _ordered_int_to_float(base + 1)
        tied = n_ge > n_sel
        n_above = count(lambda s, c: jnp.where(s >= nxt, 1.0, 0.0))
        need = jnp.where(tied, n_sel - n_above, 0.0)
        no_index = float(k_ref.shape[1])
        front_v = jnp.where(tied, jnp.inf, _NEG_INF)
        front_i = jnp.where(tied, -1.0, no_index)

        def beyond(s, kposf, front_v, front_i):
            after = jnp.where(s < front_v, 1.0,
                              jnp.where(s == front_v, jnp.where(kposf > front_i, 1.0, 0.0), 0.0))
            return jnp.where(s >= thr, jnp.where(s < nxt, after, 0.0), 0.0)

        def kposf(c):
            return (c * KEY_CHUNK + kiota).astype(_F32)

        def advance(_, carry):
            front_v, front_i, need = carry

            def best_value(c, part):
                s = score_ref[rows(c), :]
                cand = jnp.where(beyond(s, kposf(c), front_v, front_i) > 0.0, s, _NEG_INF)
                return jnp.maximum(part, _fold_rows(cand, jnp.maximum))

            v = jnp.max(for_chunks(best_value, jnp.full((SUBLANES, t), _NEG_INF, _F32)),
                        axis=0, keepdims=True)

            def first_index(c, part):
                s = score_ref[rows(c), :]
                hit = jnp.where(s == v, beyond(s, kposf(c), front_v, front_i), 0.0)
                cand = jnp.where(hit > 0.0, kposf(c), no_index)
                return jnp.minimum(part, _fold_rows(cand, jnp.minimum))

            i = jnp.min(for_chunks(first_index, jnp.full((SUBLANES, t), no_index, _F32)),
                        axis=0, keepdims=True)
            active = need > 0.0
            return (jnp.where(active, v, front_v), jnp.where(active, i, front_i),
                    jnp.where(active, need - 1.0, need))

        front_v, front_i, _ = lax.fori_loop(0, jnp.max(need).astype(jnp.int32), advance,
                                            (front_v, front_i, need))

        def tie_chunk(c, carry):
            s = score_ref[rows(c), :]
            upto = jnp.where(s > front_v, 0.0,
                             jnp.where(s == front_v,
                                       jnp.where(kposf(c) <= front_i, 0.0, _NEG_INF), _NEG_INF))
            keep = jnp.where(s >= nxt, 0.0, jnp.where(s >= thr, upto, _NEG_INF))
            bias_ref[rows(c), :] = jnp.where(c * KEY_CHUNK + kiota <= qpos, keep, _NEG_INF)
            return carry

        for_chunks(tie_chunk, 0)

    grp = B_HEADS // B_KV_HEADS
    gw = grp * t
    exp2_scale = (B_HEAD_DIM ** -0.5) * 1.4426950408889634

    def logit_chunk(c, mx):
        bias = jnp.concatenate([bias_ref[rows(c), :]] * grp, axis=1)
        folded = []
        for g in range(B_KV_HEADS):
            qg = jnp.concatenate(
                [q_ref[0, :, (g * grp + hh) * B_HEAD_DIM:(g * grp + hh + 1) * B_HEAD_DIM]
                 for hh in range(grp)], axis=0)
            l = lax.dot_general(k_ref[0, rows(c), g * B_HEAD_DIM:(g + 1) * B_HEAD_DIM], qg,
                                _NT_DIMS, preferred_element_type=_F32) + bias
            logit_ref[rows(c), g * gw:(g + 1) * gw] = l
            folded.append(_fold_rows(l, jnp.maximum))
        return jnp.maximum(mx, jnp.concatenate(folded, axis=1))

    mx = for_chunks(logit_chunk, jnp.full((SUBLANES, B_HEADS * t), _NEG_INF, _F32))
    mx = jnp.max(mx, axis=0, keepdims=True)

    acc_ref[...] = jnp.zeros_like(acc_ref)
    ones_rows = jnp.ones((acc_ref.shape[0] - B_HEAD_DIM, KEY_CHUNK), _BF16)

    def pv_chunk(c, carry):
        for g in range(B_KV_HEADS):
            cols = slice(g * gw, (g + 1) * gw)
            p = jnp.exp2((logit_ref[rows(c), cols] - mx[:, cols]) * exp2_scale)
            vt_ext = jnp.concatenate(
                [vt_ref[0, c, g * B_HEAD_DIM:(g + 1) * B_HEAD_DIM, :], ones_rows], axis=0)
            acc_ref[:, cols] += jnp.dot(vt_ext, p.astype(_BF16), preferred_element_type=_F32)
        return carry

    for_chunks(pv_chunk, 0)
    out_t = acc_ref[:B_HEAD_DIM, :] / acc_ref[B_HEAD_DIM:B_HEAD_DIM + 1, :]
    for h in range(B_HEADS):
        o_ref[0, :, h * B_HEAD_DIM:(h + 1) * B_HEAD_DIM] = (
            out_t[:, h * t:(h + 1) * t].T.astype(o_ref.dtype))


def _dsa_attention(q, k, vt, qi, ki, wit, side_job=None):
    bsz, seq, _ = q.shape
    assert Q_BLOCK == LANES and seq % KEY_CHUNK == 0
    n_sel = min(TOPK_MAX, seq // 4)
    n_blk = seq // Q_BLOCK

    def q_spec(width):
        return pl.BlockSpec((1, Q_BLOCK, width), lambda b, i: (b, i, 0))

    def kv_spec(width):
        return pl.BlockSpec((1, seq, width), lambda b, i: (b, 0, 0))

    def per_step(spec):
        return pl.BlockSpec(spec.block_shape, lambda b, i: spec.index_map(b * n_blk + i))

    side = None if side_job is None else side_job(bsz * n_blk)
    side_in = [] if side is None else [per_step(s) for s in side.in_specs]
    side_out = [] if side is None else [per_step(s) for s in side.out_specs]
    outs = pl.pallas_call(
        functools.partial(_dsa_kernel, n_sel=n_sel,
                          side_body=None if side is None else side.body,
                          n_side_in=len(side_in), n_side_out=len(side_out)),
        grid=(bsz, n_blk),
        in_specs=[q_spec(IDX_HEADS * LANES),
                  pl.BlockSpec((1, IDX_HEADS, Q_BLOCK), lambda b, i: (b, 0, i)),
                  kv_spec(LANES), q_spec(B_WIDTH), kv_spec(B_KV_WIDTH),
                  pl.BlockSpec((1, seq // KEY_CHUNK, B_KV_WIDTH, KEY_CHUNK),
                               lambda b, i: (b, 0, 0, 0))] + side_in,
        out_specs=[q_spec(B_WIDTH)] + side_out,
        out_shape=[jax.ShapeDtypeStruct((bsz, seq, B_WIDTH), _BF16)]
        + ([] if side is None else list(side.out_shape)),
        scratch_shapes=[pltpu.VMEM((seq, Q_BLOCK), _F32), pltpu.VMEM((seq, Q_BLOCK), _F32),
                        pltpu.VMEM((seq, B_HEADS * Q_BLOCK), _F32),
                        pltpu.VMEM((B_HEAD_DIM + 2 * SUBLANES, B_HEADS * Q_BLOCK), _F32)],
        compiler_params=_compiler_params(2),
        name="dsa_attention",
    )(qi, wit, ki, q, k, vt, *(() if side is None else side.arrays))
    return outs[0] if side is None else tuple(outs)


def _xattn_kernel(q_ref, k_ref, v_ref, o_ref):
    scale = X_HEAD_DIM ** -0.5
    for h in range(X_HEADS):
        hs = slice(h * X_HEAD_DIM, (h + 1) * X_HEAD_DIM)
        logits = lax.dot_general(q_ref[0, :, hs], k_ref[0, :, hs], _NT_DIMS,
                                 preferred_element_type=_F32) * scale
        mx = jnp.max(logits, axis=-1, keepdims=True)
        p = jnp.exp(logits - mx)
        denom = jnp.sum(p, axis=-1, keepdims=True)
        o = jnp.dot(p.astype(_BF16), v_ref[0, :, hs], preferred_element_type=_F32)
        o_ref[0, :, hs] = (o / denom).astype(o_ref.dtype)


def _cross_attention(q, k, v):
    bsz, seq, _ = q.shape
    tq = ROW_BLOCK["cross_attention"]
    m = k.shape[1]
    mem_spec = pl.BlockSpec((1, m, X_WIDTH), lambda b, i: (b, 0, 0))
    return pl.pallas_call(
        _xattn_kernel,
        grid=(bsz, seq // tq),
        in_specs=[pl.BlockSpec((1, tq, X_WIDTH), lambda b, i: (b, i, 0)), mem_spec, mem_spec],
        out_specs=pl.BlockSpec((1, tq, X_WIDTH), lambda b, i: (b, i, 0)),
        out_shape=jax.ShapeDtypeStruct((bsz, seq, X_WIDTH), _BF16),
        compiler_params=_compiler_params(2),
        name="cross_attention",
    )(q, k, v)


def _norm_proj_kernel(x_ref, g_ref, w_ref, o_ref):
    x = x_ref[...]
    ms = jnp.mean(x * x, axis=-1, keepdims=True)
    h = ((x * lax.rsqrt(ms + EPS)) * g_ref[...]).astype(_BF16)
    o_ref[...] = jnp.dot(h, w_ref[...], preferred_element_type=_F32).astype(o_ref.dtype)


def _norm_proj(x, g, w_bf16, out_dtype, name):
    m, d = x.shape
    tm = ROW_BLOCK["norm_proj"]
    n = w_bf16.shape[1]
    return pl.pallas_call(
        _norm_proj_kernel,
        grid=(m // tm,),
        in_specs=[pl.BlockSpec((tm, d), lambda i: (i, 0)),
                  pl.BlockSpec((1, d), lambda i: (0, 0)),
                  pl.BlockSpec((d, n), lambda i: (0, 0), pipeline_mode=pl.Buffered(1))],
        out_specs=pl.BlockSpec((tm, n), lambda i: (i, 0)),
        out_shape=jax.ShapeDtypeStruct((m, n), out_dtype),
        compiler_params=_compiler_params(1),
        name=name,
    )(x, g.reshape(1, d), w_bf16)


def _xattn_out_kernel(o_ref, w_ref, x_ref, g_ref, x_out_ref, h_out_ref):
    x = x_ref[...] + jnp.dot(o_ref[...], w_ref[...], preferred_element_type=_F32)
    x_out_ref[...] = x
    ms = jnp.mean(x * x, axis=-1, keepdims=True)
    h_out_ref[...] = ((x * lax.rsqrt(ms + EPS)) * g_ref[...]).astype(h_out_ref.dtype)


def _xattn_out_and_norm(o, w_bf16, x, g):
    m, k = o.shape
    tm = ROW_BLOCK["xattn_out_norm"]
    d = x.shape[1]
    row_spec = pl.BlockSpec((tm, d), lambda i: (i, 0))
    return pl.pallas_call(
        _xattn_out_kernel,
        grid=(m // tm,),
        in_specs=[pl.BlockSpec((tm, k), lambda i: (i, 0)),
                  pl.BlockSpec((k, d), lambda i: (0, 0), pipeline_mode=pl.Buffered(1)),
                  row_spec,
                  pl.BlockSpec((1, d), lambda i: (0, 0))],
        out_specs=[row_spec, row_spec],
        out_shape=[jax.ShapeDtypeStruct((m, d), _F32), jax.ShapeDtypeStruct((m, d), _BF16)],
        compiler_params=_compiler_params(1),
        name="xattn_out_norm",
    )(o, w_bf16, x, g.reshape(1, d))


def kernel(x, mem, positions, norm_mix_g, w_in, a_norm_g, a_spatial_w, a_spatial_b, p_a, p_b,
           w_out, norm_x_g, norm_mem_g, xq_w, xk_w, xv_w, xo_w, norm_ffn_g, ffn_w1, ffn_w3,
           ffn_w2, final_norm_g):
    bsz, seq, d = x.shape
    m = bsz * seq
    depth = w_in.shape[0]
    ffn_hidden = ffn_w1.shape[-1]

    xf = x.reshape(m, d)
    mem_f = mem.reshape(bsz * MEM_LEN, d)
    w_in_t = jnp.transpose(w_in, (0, 2, 1))
    for l in range(depth):
        def w_in_cols(col0):
            return Rhs(w_in_t, l, col0, transposed=True)

        h = _rmsnorm(xf, norm_mix_g[l], _BF16)
        att = _fused_matmul("proj_att", [h], [(0, w_in_cols(ZA_END))], [], _ep_identity,
                            ATT_WIDTH, _F32)
        za, q, k, vt, qi, ki, wit = _fused_matmul(
            "proj_za", [h], [(0, w_in_cols(0))], [], _ep_identity, ZA_END, _F32,
            side_job=functools.partial(_prep_side_job, att.reshape(bsz, seq, ATT_WIDTH),
                                       positions))
        def weight_casts(n_steps):
            return _merge_side_jobs(
                _cast_side_job(p_a, l, n_steps), _cast_side_job(p_b, l, n_steps),
                _cast_side_job(w_in_t, l, n_steps, row0=ATT_END, rows=2 * d),
                _cast_side_job(w_out, l, n_steps), _cast_side_job(xq_w, l, n_steps),
                _cast_side_job(xk_w, l, n_steps), _cast_side_job(xv_w, l, n_steps),
                _gating_side_job(za, a_norm_g[l], a_spatial_w[l], a_spatial_b[l], n_steps))

        (y_b, pa_bf16, pb_bf16, gate_wt_bf16, wo_bf16, xq_bf16, xk_bf16, xv_bf16,
         y_a) = _dsa_attention(q, k, vt, qi, ki, wit, side_job=weight_casts)

        def ffn_up_weight_casts(n_steps):
            return _merge_side_jobs(_cast_side_job(ffn_w1, l, n_steps),
                                    _cast_side_job(ffn_w3, l, n_steps))

        merged, w1_bf16, w3_bf16 = _fused_matmul(
            "gated_merge", [y_a, y_b.reshape(m, B_WIDTH), h],
            [(0, Rhs(pa_bf16)), (1, Rhs(pb_bf16)), (2, Rhs(gate_wt_bf16, transposed=True)),
             (2, Rhs(gate_wt_bf16, col0=d, transposed=True))],
            [], _ep_gated_merge, d, _BF16, side_job=ffn_up_weight_casts)
        x1, xo_bf16 = _fused_matmul("out_proj", [merged], [(0, Rhs(wo_bf16))], [xf],
                                    _ep_residual, d, _F32,
                                    side_job=functools.partial(_cast_side_job, xo_w, l))

        qx = _norm_proj(x1, norm_x_g[l], xq_bf16, _BF16, "xattn_q")
        kx = _norm_proj(mem_f, norm_mem_g[l], xk_bf16, _BF16, "xattn_k")
        vx = _norm_proj(mem_f, norm_mem_g[l], xv_bf16, _BF16, "xattn_v")
        ox = _cross_attention(qx.reshape(bsz, seq, X_WIDTH),
                              kx.reshape(bsz, MEM_LEN, X_WIDTH),
                              vx.reshape(bsz, MEM_LEN, X_WIDTH)).reshape(m, X_WIDTH)
        x2, h2 = _xattn_out_and_norm(ox, xo_bf16, x1, norm_ffn_g[l])

        act, w2_bf16 = _fused_matmul("ffn_up", [h2],
                                     [(0, Rhs(w1_bf16)), (0, Rhs(w3_bf16))],
                                     [], _ep_swiglu, ffn_hidden, _BF16,
                                     side_job=functools.partial(_cast_side_job, ffn_w2, l))
        xf = _fused_matmul("ffn_down", [act], [(0, Rhs(w2_bf16))], [x2], _ep_residual, d, _F32)

    return _rmsnorm(xf, final_norm_g, _F32).reshape(bsz, seq, d)
```

```python
import functools
from typing import Callable, NamedTuple, Optional

import jax
import jax.numpy as jnp
from jax import lax
from jax.experimental import pallas as pl
from jax.experimental.pallas import tpu as pltpu

D_MODEL = 4096
MEM_LEN = 256
EPS = 1e-6
ROPE_THETA = 500000.0
CHUNK = 128
A_GROUPS = 16
A_WIDTH = D_MODEL // 2
A_GROUP_DIM = A_WIDTH // A_GROUPS
B_HEADS = 16
B_HEAD_DIM = 128
B_KV_HEADS = 4
B_WIDTH = B_HEADS * B_HEAD_DIM
B_KV_WIDTH = B_KV_HEADS * B_HEAD_DIM
B_ROT = B_HEAD_DIM // 4
IDX_HEADS = 16
IDX_DIM = 64
IDX_WIDTH = IDX_HEADS * IDX_DIM
IDX_ROT = IDX_DIM // 4
TOPK_MAX = 256
Q_BLOCK = 128
X_HEADS = 4
X_HEAD_DIM = 256
X_WIDTH = X_HEADS * X_HEAD_DIM
IN_SIZES = (2 * A_WIDTH, B_WIDTH, B_KV_WIDTH, B_KV_WIDTH, IDX_WIDTH, IDX_DIM, IDX_HEADS,
            2 * D_MODEL)

LANES = 128
SUBLANES = 8
KEY_CHUNK = 512
PROJ_TN = 512
ZA_END = IN_SIZES[0]
ATT_END = ZA_END + sum(IN_SIZES[1:7])
ATT_Q0 = 0
ATT_K0 = ATT_Q0 + B_WIDTH
ATT_V0 = ATT_K0 + B_KV_WIDTH
ATT_QI0 = ATT_V0 + B_KV_WIDTH
ATT_KW0 = ATT_QI0 + IDX_WIDTH
ATT_WIDTH = -(-(ATT_END - ZA_END) // PROJ_TN) * PROJ_TN
assert ZA_END % PROJ_TN == 0 and ATT_KW0 + LANES <= ATT_WIDTH

VMEM_LIMIT_BYTES = 60 * 1024 * 1024


class Tile(NamedTuple):
    tm: int
    tn: int
    row_splits: int = 1


TILES = {
    "proj_att": Tile(1024, PROJ_TN),
    "proj_za": Tile(1024, 512),
    "gated_merge": Tile(1024, 256, row_splits=2),
    "out_proj": Tile(1024, 512),
    "ffn_up": Tile(2048, 256, row_splits=4),
    "ffn_down": Tile(512, 512),
}
ROW_BLOCK = {
    "rmsnorm": 512,
    "spatial_gating": 4 * CHUNK,
    "cross_attention": 1024,
    "norm_proj": 512,
    "xattn_out_norm": 512,
}

_BF16 = jnp.bfloat16
_F32 = jnp.float32
_NEG_INF = float("-inf")
_NN_DIMS = (((1,), (0,)), ((), ()))
_NT_DIMS = (((1,), (1,)), ((), ()))


def _compiler_params(n_axes):
    return pltpu.CompilerParams(
        dimension_semantics=("arbitrary",) * n_axes,
        vmem_limit_bytes=VMEM_LIMIT_BYTES,
    )


def _rmsnorm_kernel(x_ref, g_ref, o_ref):
    x = x_ref[...]
    ms = jnp.mean(x * x, axis=-1, keepdims=True)
    y = x * lax.rsqrt(ms + EPS)
    o_ref[...] = (y * g_ref[...]).astype(o_ref.dtype)


def _rmsnorm(x, g, out_dtype):
    m, d = x.shape
    tm = ROW_BLOCK["rmsnorm"]
    return pl.pallas_call(
        _rmsnorm_kernel,
        grid=(m // tm,),
        in_specs=[pl.BlockSpec((tm, d), lambda i: (i, 0)),
                  pl.BlockSpec((1, d), lambda i: (0, 0))],
        out_specs=pl.BlockSpec((tm, d), lambda i: (i, 0)),
        out_shape=jax.ShapeDtypeStruct((m, d), out_dtype),
        compiler_params=_compiler_params(1),
        name="rmsnorm",
    )(x, g.reshape(1, d))


def _ep_identity(accs, extras):
    return accs[0]


def _ep_residual(accs, extras):
    return extras[0] + accs[0]


def _ep_gated_merge(accs, extras):
    y_a, y_b, gate_a, gate_b = accs
    return jax.nn.sigmoid(gate_a) * y_a + jax.nn.sigmoid(gate_b) * y_b


def _ep_swiglu(accs, extras):
    return jax.nn.silu(accs[0]) * accs[1]


class Rhs(NamedTuple):
    array: jax.Array
    layer: Optional[int] = None
    col0: int = 0
    transposed: bool = False


def _rhs_spec(r, tn):
    if r.transposed and r.layer is None:
        assert r.col0 % tn == 0
        off = r.col0 // tn
        return pl.BlockSpec((tn, r.array.shape[1]), lambda i, j: (j + off, 0))
    if r.transposed:
        k = r.array.shape[2]
        if r.col0 % tn == 0:
            return pl.BlockSpec((None, tn, k), lambda i, j: (r.layer, j + r.col0 // tn, 0))
        assert r.col0 % SUBLANES == 0
        return pl.BlockSpec((pl.Element(1), pl.Element(tn), pl.Element(k)),
                            lambda i, j: (r.layer, (r.col0 // SUBLANES + j * (tn // SUBLANES))
                                          * SUBLANES, 0))
    assert r.col0 % tn == 0
    off = r.col0 // tn
    if r.layer is None:
        return pl.BlockSpec((r.array.shape[0], tn), lambda i, j: (0, j + off))
    return pl.BlockSpec((None, r.array.shape[1], tn), lambda i, j: (r.layer, 0, j + off))


class SideJob(NamedTuple):
    body: Callable
    arrays: tuple
    in_specs: tuple
    out_specs: tuple
    out_shape: tuple


def _cast_body(x_ref, o_ref):
    x = x_ref[0] if len(x_ref.shape) == 3 else x_ref[...]
    o_ref[...] = x.astype(o_ref.dtype)


def _cast_side_job(w_stack, layer, n_steps, row0=0, rows=None):
    rows = w_stack.shape[1] if rows is None else rows
    cols = w_stack.shape[2]
    slab = rows // n_steps
    assert slab * n_steps == rows and slab % (2 * SUBLANES) == 0
    if row0 % slab == 0:
        in_spec = pl.BlockSpec((None, slab, cols), lambda s: (layer, s + row0 // slab, 0))
    else:
        assert row0 % SUBLANES == 0
        in_spec = pl.BlockSpec(
            (pl.Element(1), pl.Element(slab), pl.Element(cols)),
            lambda s: (layer, (row0 // SUBLANES + s * (slab // SUBLANES)) * SUBLANES, 0))
    return SideJob(_cast_body, (w_stack,), (in_spec,),
                   (pl.BlockSpec((slab, cols), lambda s: (s, 0)),),
                   (jax.ShapeDtypeStruct((rows, cols), _BF16),))


def _merge_side_jobs(*jobs):
    n_in = [len(j.in_specs) for j in jobs]
    n_out = [len(j.out_specs) for j in jobs]

    def body(*refs):
        ins, outs = refs[:sum(n_in)], refs[sum(n_in):]
        i = o = 0
        for j, ni, no in zip(jobs, n_in, n_out):
            j.body(*ins[i:i + ni], *outs[o:o + no])
            i, o = i + ni, o + no

    return SideJob(body, sum((j.arrays for j in jobs), ()), sum((j.in_specs for j in jobs), ()),
                   sum((j.out_specs for j in jobs), ()), sum((j.out_shape for j in jobs), ()))


def _mm_kernel(*refs, n_lhs, pair_lhs, rhs_transposed, n_extra, epilogue, side_body, n_side_in,
               row_splits):
    n_rhs = len(pair_lhs)
    lhs_refs = refs[:n_lhs]
    rhs_refs = refs[n_lhs:n_lhs + n_rhs]
    n_in = n_lhs + n_rhs + n_extra
    extra_refs = refs[n_lhs + n_rhs:n_in]
    o_ref = refs[n_in + n_side_in]
    if side_body is not None:
        side_body(*refs[n_in:n_in + n_side_in], *refs[n_in + n_side_in + 1:])
    weights = [(r[0] if len(r.shape) == 3 else r[...]).astype(_BF16) for r in rhs_refs]
    rows_per_split = o_ref.shape[0] // row_splits
    for s in range(row_splits):
        rows = slice(s * rows_per_split, (s + 1) * rows_per_split)
        accs = [lax.dot_general(lhs_refs[li][rows, :], w, _NT_DIMS if tr else _NN_DIMS,
                                preferred_element_type=_F32)
                for li, w, tr in zip(pair_lhs, weights, rhs_transposed)]
        o_ref[rows, :] = epilogue(accs, [e[rows, :] for e in extra_refs]).astype(o_ref.dtype)


def _fused_matmul(name, lhs, pairs, extras, epilogue, n_out, out_dtype, side_job=None):
    tm, tn, row_splits = TILES[name]
    m = lhs[0].shape[0]
    assert m % tm == 0 and n_out % tn == 0
    grid = (m // tm, n_out // tn)
    in_specs = [pl.BlockSpec((tm, a.shape[1]), lambda i, j: (i, 0)) for a in lhs]
    in_specs += [_rhs_spec(r, tn) for _, r in pairs]
    in_specs += [pl.BlockSpec((tm, tn), lambda i, j: (i, j)) for _ in extras]
    out_specs = [pl.BlockSpec((tm, tn), lambda i, j: (i, j))]
    out_shape = [jax.ShapeDtypeStruct((m, n_out), out_dtype)]
    operands = list(lhs) + [p[1].array for p in pairs] + list(extras)
    side = None if side_job is None else side_job(grid[0] * grid[1])
    if side is not None:
        def per_step(spec):
            return pl.BlockSpec(spec.block_shape,
                                lambda i, j: spec.index_map(i * grid[1] + j))
        in_specs += [per_step(s) for s in side.in_specs]
        out_specs += [per_step(s) for s in side.out_specs]
        out_shape += list(side.out_shape)
        operands += list(side.arrays)
    kern = functools.partial(_mm_kernel, n_lhs=len(lhs), pair_lhs=tuple(p[0] for p in pairs),
                             rhs_transposed=tuple(p[1].transposed for p in pairs),
                             n_extra=len(extras), epilogue=epilogue,
                             side_body=None if side is None else side.body,
                             n_side_in=0 if side is None else len(side.in_specs),
                             row_splits=row_splits)
    outs = pl.pallas_call(
        kern,
        grid=grid,
        in_specs=in_specs,
        out_specs=out_specs,
        out_shape=out_shape,
        compiler_params=_compiler_params(2),
        name=name,
    )(*operands)
    return outs[0] if side is None else tuple(outs)


def _gating_kernel(za_ref, ng_ref, w_ref, bt_ref, o_ref):
    row = lax.broadcasted_iota(jnp.int32, (CHUNK, CHUNK), 0)
    col = lax.broadcasted_iota(jnp.int32, (CHUNK, CHUNK), 1)
    causal = col <= row
    for c in range(za_ref.shape[0] // CHUNK):
        rows = slice(c * CHUNK, (c + 1) * CHUNK)
        z = jax.nn.gelu(za_ref[rows, :])
        u = z[:, :A_WIDTH]
        v = z[:, A_WIDTH:]
        ms = jnp.mean(v * v, axis=-1, keepdims=True)
        vn = ((v * lax.rsqrt(ms + EPS)) * ng_ref[...]).astype(_BF16)
        for g in range(A_GROUPS):
            sl = slice(g * A_GROUP_DIM, (g + 1) * A_GROUP_DIM)
            w = jnp.where(causal, w_ref[g], 0.0).astype(_BF16)
            s = jnp.dot(w, vn[:, sl], preferred_element_type=_F32) + bt_ref[:, g:g + 1]
            o_ref[rows, sl] = (u[:, sl] * s).astype(o_ref.dtype)


def _spatial_gating(za, norm_g, w_s, b_s):
    m = za.shape[0]
    tm = ROW_BLOCK["spatial_gating"]
    return pl.pallas_call(
        _gating_kernel,
        grid=(m // tm,),
        in_specs=[pl.BlockSpec((tm, 2 * A_WIDTH), lambda i: (i, 0)),
                  pl.BlockSpec((1, A_WIDTH), lambda i: (0, 0)),
                  pl.BlockSpec((A_GROUPS, CHUNK, CHUNK), lambda i: (0, 0, 0)),
                  pl.BlockSpec((CHUNK, A_GROUPS), lambda i: (0, 0))],
        out_specs=pl.BlockSpec((tm, A_WIDTH), lambda i: (i, 0)),
        out_shape=jax.ShapeDtypeStruct((m, A_WIDTH), _BF16),
        compiler_params=_compiler_params(1),
        name="spatial_gating",
    )(za, norm_g.reshape(1, A_WIDTH), w_s, b_s.T)


def _lane_inv_freq(rot_dim, period):
    inv_freq = ROPE_THETA ** (-jnp.arange(0, rot_dim, 2, dtype=_F32) / rot_dim)
    head = jnp.concatenate([inv_freq, inv_freq, jnp.zeros((period - rot_dim,), _F32)])
    return jnp.tile(head, LANES // period).reshape(1, LANES)


def _rope_tables(pos, inv_freq_lanes, rot_dim, period):
    half = rot_dim // 2
    ang = pos * inv_freq_lanes
    cos, sin = jnp.cos(ang), jnp.sin(ang)
    lane = lax.broadcasted_iota(jnp.int32, ang.shape, 1) & (period - 1)
    c = jnp.where(lane < rot_dim, cos, 1.0)
    s1 = jnp.where(lane >= half, jnp.where(lane < rot_dim, sin, 0.0), 0.0)
    s2 = jnp.where(lane < half, -sin, 0.0)
    return c, s1, s2


def _rope(x, c, s1, s2, half):
    return x * c + pltpu.roll(x, half, 1) * s1 + pltpu.roll(x, LANES - half, 1) * s2


def _prep_kernel(att_ref, pos_ref, fb_ref, fi_ref,
                 q_ref, k_ref, vt_ref, qi_ref, ki_ref, wit_ref):
    pos = pos_ref[0]
    cb, s1b, s2b = _rope_tables(pos, fb_ref[...], B_ROT, B_HEAD_DIM)
    ci, s1i, s2i = _rope_tables(pos, fi_ref[...], IDX_ROT, IDX_DIM)
    for h in range(B_HEADS):
        x = att_ref[0, :, ATT_Q0 + h * LANES:ATT_Q0 + (h + 1) * LANES]
        q_ref[0, :, h * LANES:(h + 1) * LANES] = _rope(x, cb, s1b, s2b, B_ROT // 2).astype(_BF16)
    for h in range(B_KV_HEADS):
        x = att_ref[0, :, ATT_K0 + h * LANES:ATT_K0 + (h + 1) * LANES]
        k_ref[0, :, h * LANES:(h + 1) * LANES] = _rope(x, cb, s1b, s2b, B_ROT // 2).astype(_BF16)
    vt_ref[0, 0] = att_ref[0, :, ATT_V0:ATT_V0 + B_KV_WIDTH].T.astype(_BF16)
    lane = lax.broadcasted_iota(jnp.int32, ci.shape, 1)
    is_ki = lane < IDX_DIM
    heads_per_group = LANES // IDX_DIM
    for j in range(IDX_WIDTH // LANES):
        x = att_ref[0, :, ATT_QI0 + j * LANES:ATT_QI0 + (j + 1) * LANES]
        y = _rope(x, ci, s1i, s2i, IDX_ROT // 2)
        for r in range(heads_per_group):
            h = j * heads_per_group + r
            yr = y if r == 0 else pltpu.roll(y, LANES - r * IDX_DIM, 1)
            qi_ref[0, :, h * LANES:(h + 1) * LANES] = jnp.where(is_ki, yr, 0.0).astype(_BF16)
    x = att_ref[0, :, ATT_KW0:ATT_KW0 + LANES]
    y = _rope(x, jnp.where(is_ki, ci, 1.0), jnp.where(is_ki, s1i, 0.0),
              jnp.where(is_ki, s2i, 0.0), IDX_ROT // 2)
    ki_ref[0] = jnp.where(is_ki, y, 0.0).astype(_BF16)
    wit_ref[0] = y.T[IDX_DIM:IDX_DIM + IDX_HEADS, :]


def _prep_side_job(att, positions, n_steps):
    bsz, seq, _ = att.shape
    ts = bsz * seq // n_steps
    assert ts * n_steps == bsz * seq and seq % ts == 0 and KEY_CHUNK % ts == 0
    assert ts % (2 * SUBLANES) == 0
    per_seq = seq // ts
    per_chunk = KEY_CHUNK // ts

    def tokens(width):
        return pl.BlockSpec((1, ts, width), lambda s: (s // per_seq, s % per_seq, 0))

    def out(width, dtype):
        return tokens(width), jax.ShapeDtypeStruct((bsz, seq, width), dtype)

    vt_out = (pl.BlockSpec((1, 1, B_KV_WIDTH, ts),
                           lambda s: (s // per_seq, (s % per_seq) // per_chunk, 0,
                                      (s % per_seq) % per_chunk)),
              jax.ShapeDtypeStruct((bsz, seq // KEY_CHUNK, B_KV_WIDTH, KEY_CHUNK), _BF16))
    wit_out = (pl.BlockSpec((1, IDX_HEADS, ts), lambda s: (s // per_seq, 0, s % per_seq)),
               jax.ShapeDtypeStruct((bsz, IDX_HEADS, seq), _F32))
    outs = [out(B_WIDTH, _BF16), out(B_KV_WIDTH, _BF16), vt_out,
            out(IDX_HEADS * LANES, _BF16), out(LANES, _BF16), wit_out]
    freq_spec = pl.BlockSpec((1, LANES), lambda s: (0, 0))
    return SideJob(_prep_kernel,
                   (att, positions.astype(_F32).reshape(bsz, seq, 1),
                    _lane_inv_freq(B_ROT, B_HEAD_DIM), _lane_inv_freq(IDX_ROT, IDX_DIM)),
                   (tokens(ATT_WIDTH), tokens(1), freq_spec, freq_spec),
                   tuple(o[0] for o in outs), tuple(o[1] for o in outs))


def _ordered_int_to_float(key):
    bits = jnp.where(key < 0, key ^ jnp.int32(0x7FFFFFFF), key)
    return lax.bitcast_convert_type(bits, _F32)


def _fold_rows(x, op):
    while x.shape[0] > SUBLANES:
        half = x.shape[0] // 2
        x = op(x[:half], x[half:])
    return x


_DSA_N_IN = 6


def _dsa_kernel(*refs, n_sel, side_body, n_side_in, n_side_out):
    o_at = _DSA_N_IN + n_side_in
    if side_body is not None:
        side_body(*refs[_DSA_N_IN:o_at], *refs[o_at + 1:o_at + 1 + n_side_out])
    own = refs[:_DSA_N_IN] + (refs[o_at],) + refs[o_at + 1 + n_side_out:]
    seq = own[4].shape[1]
    n_chunks = (pl.program_id(1) * Q_BLOCK + Q_BLOCK + KEY_CHUNK - 1) // KEY_CHUNK
    for n in range(1, seq // KEY_CHUNK + 1):
        pl.when(n_chunks == n)(functools.partial(_dsa_block, *own, n_sel=n_sel, n_chunks=n))


def _dsa_block(qi_ref, wit_ref, ki_ref, q_ref, k_ref, vt_ref, o_ref,
               score_ref, bias_ref, logit_ref, acc_ref, *, n_sel, n_chunks):
    t = Q_BLOCK
    blk = pl.program_id(1)
    qpos = blk * t + lax.broadcasted_iota(jnp.int32, (1, t), 1)
    kiota = lax.broadcasted_iota(jnp.int32, (KEY_CHUNK, 1), 0)
    idx_scale = (IDX_DIM ** -0.5) * (IDX_HEADS ** -0.5)

    def rows(c):
        return slice(c * KEY_CHUNK, (c + 1) * KEY_CHUNK)

    def for_chunks(body, carry):
        for c in range(n_chunks):
            carry = body(c, carry)
        return carry

    def indexer_chunk(c, carry):
        ki = ki_ref[0, rows(c), :]
        acc = jnp.zeros((KEY_CHUNK, t), _F32)
        for h in range(0, IDX_HEADS, 2):
            qpair = jnp.concatenate([qi_ref[0, :, h * LANES:(h + 1) * LANES],
                                     qi_ref[0, :, (h + 1) * LANES:(h + 2) * LANES]], axis=0)
            d = lax.dot_general(ki, qpair, _NT_DIMS, preferred_element_type=_F32)
            acc = acc + wit_ref[0, h:h + 1, :] * jnp.maximum(d[:, :t], 0.0)
            acc = acc + wit_ref[0, h + 1:h + 2, :] * jnp.maximum(d[:, t:], 0.0)
        kpos = c * KEY_CHUNK + kiota
        score_ref[rows(c), :] = jnp.where(kpos <= qpos, acc * idx_scale + 0.0, _NEG_INF)
        return carry

    for_chunks(indexer_chunk, 0)

    def count(indicator):
        def chunk(c, part):
            return part + _fold_rows(indicator(score_ref[rows(c), :], c), jnp.add)
        part = for_chunks(chunk, jnp.zeros((SUBLANES, t), _F32))
        return jnp.sum(part, axis=0, keepdims=True)

    def count_ge(cand_f):
        return count(lambda s, c: jnp.where(s >= cand_f, 1.0, 0.0))

    def search():
        int_min = jnp.int32(-2 ** 31)
        base = jnp.where(count_ge(jnp.zeros((1, t), _F32)) >= n_sel, jnp.int32(0), int_min)

        def search_step(it, base):
            cand = base | lax.shift_left(jnp.int32(1), jnp.int32(30) - it)
            cnt = count_ge(_ordered_int_to_float(cand))
            return jnp.where(cnt >= n_sel, cand, base)

        return lax.fori_loop(0, 31, search_step, base)

    if (n_chunks - 1) * KEY_CHUNK >= n_sel:
        base = search()
    else:
        base = lax.cond((blk + 1) * t <= n_sel, lambda: jnp.zeros((1, t), jnp.int32), search)
    thr = jnp.where(qpos + 1 <= n_sel, _NEG_INF, _ordered_int_to_float(base))

    def bias_chunk(c, part):
        s = score_ref[rows(c), :]
        sel = jnp.where(c * KEY_CHUNK + kiota <= qpos, jnp.where(s >= thr, 1.0, 0.0), 0.0)
        bias_ref[rows(c), :] = jnp.where(sel > 0.0, 0.0, _NEG_INF)
        return part + _fold_rows(sel, jnp.add)

    n_ge = jnp.sum(for_chunks(bias_chunk, jnp.zeros((SUBLANES, t), _F32)), axis=0, keepdims=True)

    @pl.when(jnp.max(n_ge) > n_sel)
    def _():
        nxt = _ordered_int_to_float(base + 1)
        tied = n_ge > n_sel
        n_above = count(lambda s, c: jnp.where(s >= nxt, 1.0, 0.0))
        need = jnp.where(tied, n_sel - n_above, 0.0)
        no_index = float(k_ref.shape[1])
        front_v = jnp.where(tied, jnp.inf, _NEG_INF)
        front_i = jnp.where(tied, -1.0, no_index)

        def beyond(s, kposf, front_v, front_i):
            after = jnp.where(s < front_v, 1.0,
                              jnp.where(s == front_v, jnp.where(kposf > front_i, 1.0, 0.0), 0.0))
            return jnp.where(s >= thr, jnp.where(s < nxt, after, 0.0), 0.0)

        def kposf(c):
            return (c * KEY_CHUNK + kiota).astype(_F32)

        def advance(_, carry):
            front_v, front_i, need = carry

            def best_value(c, part):
                s = score_ref[rows(c), :]
                cand = jnp.where(beyond(s, kposf(c), front_v, front_i) > 0.0, s, _NEG_INF)
                return jnp.maximum(part, _fold_rows(cand, jnp.maximum))

            v = jnp.max(for_chunks(best_value, jnp.full((SUBLANES, t), _NEG_INF, _F32)),
                        axis=0, keepdims=True)

            def first_index(c, part):
                s = score_ref[rows(c), :]
                hit = jnp.where(s == v, beyond(s, kposf(c), front_v, front_i), 0.0)
                cand = jnp.where(hit > 0.0, kposf(c), no_index)
                return jnp.minimum(part, _fold_rows(cand, jnp.minimum))

            i = jnp.min(for_chunks(first_index, jnp.full((SUBLANES, t), no_index, _F32)),
                        axis=0, keepdims=True)
            active = need > 0.0
            return (jnp.where(active, v, front_v), jnp.where(active, i, front_i),
                    jnp.where(active, need - 1.0, need))

        front_v, front_i, _ = lax.fori_loop(0, jnp.max(need).astype(jnp.int32), advance,
                                            (front_v, front_i, need))

        def tie_chunk(c, carry):
            s = score_ref[rows(c), :]
            upto = jnp.where(s > front_v, 0.0,
                             jnp.where(s == front_v,
                                       jnp.where(kposf(c) <= front_i, 0.0, _NEG_INF), _NEG_INF))
            keep = jnp.where(s >= nxt, 0.0, jnp.where(s >= thr, upto, _NEG_INF))
            bias_ref[rows(c), :] = jnp.where(c * KEY_CHUNK + kiota <= qpos, keep, _NEG_INF)
            return carry

        for_chunks(tie_chunk, 0)

    grp = B_HEADS // B_KV_HEADS
    gw = grp * t
    exp2_scale = (B_HEAD_DIM ** -0.5) * 1.4426950408889634

    def logit_chunk(c, mx):
        bias = jnp.concatenate([bias_ref[rows(c), :]] * grp, axis=1)
        folded = []
        for g in range(B_KV_HEADS):
            qg = jnp.concatenate(
                [q_ref[0, :, (g * grp + hh) * B_HEAD_DIM:(g * grp + hh + 1) * B_HEAD_DIM]
                 for hh in range(grp)], axis=0)
            l = lax.dot_general(k_ref[0, rows(c), g * B_HEAD_DIM:(g + 1) * B_HEAD_DIM], qg,
                                _NT_DIMS, preferred_element_type=_F32) + bias
            logit_ref[rows(c), g * gw:(g + 1) * gw] = l
            folded.append(_fold_rows(l, jnp.maximum))
        return jnp.maximum(mx, jnp.concatenate(folded, axis=1))

    mx = for_chunks(logit_chunk, jnp.full((SUBLANES, B_HEADS * t), _NEG_INF, _F32))
    mx = jnp.max(mx, axis=0, keepdims=True)

    acc_ref[...] = jnp.zeros_like(acc_ref)
    ones_rows = jnp.ones((acc_ref.shape[0] - B_HEAD_DIM, KEY_CHUNK), _BF16)

    def pv_chunk(c, carry):
        for g in range(B_KV_HEADS):
            cols = slice(g * gw, (g + 1) * gw)
            p = jnp.exp2((logit_ref[rows(c), cols] - mx[:, cols]) * exp2_scale)
            vt_ext = jnp.concatenate(
                [vt_ref[0, c, g * B_HEAD_DIM:(g + 1) * B_HEAD_DIM, :], ones_rows], axis=0)
            acc_ref[:, cols] += jnp.dot(vt_ext, p.astype(_BF16), preferred_element_type=_F32)
        return carry

    for_chunks(pv_chunk, 0)
    out_t = acc_ref[:B_HEAD_DIM, :] / acc_ref[B_HEAD_DIM:B_HEAD_DIM + 1, :]
    for h in range(B_HEADS):
        o_ref[0, :, h * B_HEAD_DIM:(h + 1) * B_HEAD_DIM] = (
            out_t[:, h * t:(h + 1) * t].T.astype(o_ref.dtype))


def _dsa_attention(q, k, vt, qi, ki, wit, side_job=None):
    bsz, seq, _ = q.shape
    assert Q_BLOCK == LANES and seq % KEY_CHUNK == 0
    n_sel = min(TOPK_MAX, seq // 4)
    n_blk = seq // Q_BLOCK

    def q_spec(width):
        return pl.BlockSpec((1, Q_BLOCK, width), lambda b, i: (b, i, 0))

    def kv_spec(width):
        return pl.BlockSpec((1, seq, width), lambda b, i: (b, 0, 0))

    def per_step(spec):
        return pl.BlockSpec(spec.block_shape, lambda b, i: spec.index_map(b * n_blk + i))

    side = None if side_job is None else side_job(bsz * n_blk)
    side_in = [] if side is None else [per_step(s) for s in side.in_specs]
    side_out = [] if side is None else [per_step(s) for s in side.out_specs]
    outs = pl.pallas_call(
        functools.partial(_dsa_kernel, n_sel=n_sel,
                          side_body=None if side is None else side.body,
                          n_side_in=len(side_in), n_side_out=len(side_out)),
        grid=(bsz, n_blk),
        in_specs=[q_spec(IDX_HEADS * LANES),
                  pl.BlockSpec((1, IDX_HEADS, Q_BLOCK), lambda b, i: (b, 0, i)),
                  kv_spec(LANES), q_spec(B_WIDTH), kv_spec(B_KV_WIDTH),
                  pl.BlockSpec((1, seq // KEY_CHUNK, B_KV_WIDTH, KEY_CHUNK),
                               lambda b, i: (b, 0, 0, 0))] + side_in,
        out_specs=[q_spec(B_WIDTH)] + side_out,
        out_shape=[jax.ShapeDtypeStruct((bsz, seq, B_WIDTH), _BF16)]
        + ([] if side is None else list(side.out_shape)),
        scratch_shapes=[pltpu.VMEM((seq, Q_BLOCK), _F32), pltpu.VMEM((seq, Q_BLOCK), _F32),
                        pltpu.VMEM((seq, B_HEADS * Q_BLOCK), _F32),
                        pltpu.VMEM((B_HEAD_DIM + 2 * SUBLANES, B_HEADS * Q_BLOCK), _F32)],
        compiler_params=_compiler_params(2),
        name="dsa_attention",
    )(qi, wit, ki, q, k, vt, *(() if side is None else side.arrays))
    return outs[0] if side is None else tuple(outs)


def _xattn_kernel(q_ref, k_ref, v_ref, o_ref):
    scale = X_HEAD_DIM ** -0.5
    for h in range(X_HEADS):
        hs = slice(h * X_HEAD_DIM, (h + 1) * X_HEAD_DIM)
        logits = lax.dot_general(q_ref[0, :, hs], k_ref[0, :, hs], _NT_DIMS,
                                 preferred_element_type=_F32) * scale
        mx = jnp.max(logits, axis=-1, keepdims=True)
        p = jnp.exp(logits - mx)
        denom = jnp.sum(p, axis=-1, keepdims=True)
        o = jnp.dot(p.astype(_BF16), v_ref[0, :, hs], preferred_element_type=_F32)
        o_ref[0, :, hs] = (o / denom).astype(o_ref.dtype)


def _cross_attention(q, k, v):
    bsz, seq, _ = q.shape
    tq = ROW_BLOCK["cross_attention"]
    m = k.shape[1]
    mem_spec = pl.BlockSpec((1, m, X_WIDTH), lambda b, i: (b, 0, 0))
    return pl.pallas_call(
        _xattn_kernel,
        grid=(bsz, seq // tq),
        in_specs=[pl.BlockSpec((1, tq, X_WIDTH), lambda b, i: (b, i, 0)), mem_spec, mem_spec],
        out_specs=pl.BlockSpec((1, tq, X_WIDTH), lambda b, i: (b, i, 0)),
        out_shape=jax.ShapeDtypeStruct((bsz, seq, X_WIDTH), _BF16),
        compiler_params=_compiler_params(2),
        name="cross_attention",
    )(q, k, v)


def _norm_proj_kernel(x_ref, g_ref, w_ref, o_ref):
    x = x_ref[...]
    ms = jnp.mean(x * x, axis=-1, keepdims=True)
    h = ((x * lax.rsqrt(ms + EPS)) * g_ref[...]).astype(_BF16)
    o_ref[...] = jnp.dot(h, w_ref[...], preferred_element_type=_F32).astype(o_ref.dtype)


def _norm_proj(x, g, w_bf16, out_dtype, name):
    m, d = x.shape
    tm = ROW_BLOCK["norm_proj"]
    n = w_bf16.shape[1]
    return pl.pallas_call(
        _norm_proj_kernel,
        grid=(m // tm,),
        in_specs=[pl.BlockSpec((tm, d), lambda i: (i, 0)),
                  pl.BlockSpec((1, d), lambda i: (0, 0)),
                  pl.BlockSpec((d, n), lambda i: (0, 0), pipeline_mode=pl.Buffered(1))],
        out_specs=pl.BlockSpec((tm, n), lambda i: (i, 0)),
        out_shape=jax.ShapeDtypeStruct((m, n), out_dtype),
        compiler_params=_compiler_params(1),
        name=name,
    )(x, g.reshape(1, d), w_bf16)


def _xattn_out_kernel(o_ref, w_ref, x_ref, g_ref, x_out_ref, h_out_ref):
    x = x_ref[...] + jnp.dot(o_ref[...], w_ref[...], preferred_element_type=_F32)
    x_out_ref[...] = x
    ms = jnp.mean(x * x, axis=-1, keepdims=True)
    h_out_ref[...] = ((x * lax.rsqrt(ms + EPS)) * g_ref[...]).astype(h_out_ref.dtype)


def _xattn_out_and_norm(o, w_bf16, x, g):
    m, k = o.shape
    tm = ROW_BLOCK["xattn_out_norm"]
    d = x.shape[1]
    row_spec = pl.BlockSpec((tm, d), lambda i: (i, 0))
    return pl.pallas_call(
        _xattn_out_kernel,
        grid=(m // tm,),
        in_specs=[pl.BlockSpec((tm, k), lambda i: (i, 0)),
                  pl.BlockSpec((k, d), lambda i: (0, 0), pipeline_mode=pl.Buffered(1)),
                  row_spec,
                  pl.BlockSpec((1, d), lambda i: (0, 0))],
        out_specs=[row_spec, row_spec],
        out_shape=[jax.ShapeDtypeStruct((m, d), _F32), jax.ShapeDtypeStruct((m, d), _BF16)],
        compiler_params=_compiler_params(1),
        name="xattn_out_norm",
    )(o, w_bf16, x, g.reshape(1, d))


def kernel(x, mem, positions, norm_mix_g, w_in, a_norm_g, a_spatial_w, a_spatial_b, p_a, p_b,
           w_out, norm_x_g, norm_mem_g, xq_w, xk_w, xv_w, xo_w, norm_ffn_g, ffn_w1, ffn_w3,
           ffn_w2, final_norm_g):
    bsz, seq, d = x.shape
    m = bsz * seq
    depth = w_in.shape[0]
    ffn_hidden = ffn_w1.shape[-1]

    xf = x.reshape(m, d)
    mem_f = mem.reshape(bsz * MEM_LEN, d)
    w_in_t = jnp.transpose(w_in, (0, 2, 1))
    for l in range(depth):
        def w_in_cols(col0):
            return Rhs(w_in_t, l, col0, transposed=True)

        h = _rmsnorm(xf, norm_mix_g[l], _BF16)
        att = _fused_matmul("proj_att", [h], [(0, w_in_cols(ZA_END))], [], _ep_identity,
                            ATT_WIDTH, _F32)
        za, q, k, vt, qi, ki, wit = _fused_matmul(
            "proj_za", [h], [(0, w_in_cols(0))], [], _ep_identity, ZA_END, _F32,
            side_job=functools.partial(_prep_side_job, att.reshape(bsz, seq, ATT_WIDTH),
                                       positions))
        y_a = _spatial_gating(za, a_norm_g[l], a_spatial_w[l], a_spatial_b[l])
        def weight_casts(n_steps):
            return _merge_side_jobs(
                _cast_side_job(p_a, l, n_steps), _cast_side_job(p_b, l, n_steps),
                _cast_side_job(w_in_t, l, n_steps, row0=ATT_END, rows=2 * d),
                _cast_side_job(w_out, l, n_steps), _cast_side_job(xq_w, l, n_steps),
                _cast_side_job(xk_w, l, n_steps), _cast_side_job(xv_w, l, n_steps))

        y_b, pa_bf16, pb_bf16, gate_wt_bf16, wo_bf16, xq_bf16, xk_bf16, xv_bf16 = _dsa_attention(
            q, k, vt, qi, ki, wit, side_job=weight_casts)

        def ffn_up_weight_casts(n_steps):
            return _merge_side_jobs(_cast_side_job(ffn_w1, l, n_steps),
                                    _cast_side_job(ffn_w3, l, n_steps))

        merged, w1_bf16, w3_bf16 = _fused_matmul(
            "gated_merge", [y_a, y_b.reshape(m, B_WIDTH), h],
            [(0, Rhs(pa_bf16)), (1, Rhs(pb_bf16)), (2, Rhs(gate_wt_bf16, transposed=True)),
             (2, Rhs(gate_wt_bf16, col0=d, transposed=True))],
            [], _ep_gated_merge, d, _BF16, side_job=ffn_up_weight_casts)
        x1, xo_bf16 = _fused_matmul("out_proj", [merged], [(0, Rhs(wo_bf16))], [xf],
                                    _ep_residual, d, _F32,
                                    side_job=functools.partial(_cast_side_job, xo_w, l))

        qx = _norm_proj(x1, norm_x_g[l], xq_bf16, _BF16, "xattn_q")
        kx = _norm_proj(mem_f, norm_mem_g[l], xk_bf16, _BF16, "xattn_k")
        vx = _norm_proj(mem_f, norm_mem_g[l], xv_bf16, _BF16, "xattn_v")
        ox = _cross_attention(qx.reshape(bsz, seq, X_WIDTH),
                              kx.reshape(bsz, MEM_LEN, X_WIDTH),
                              vx.reshape(bsz, MEM_LEN, X_WIDTH)).reshape(m, X_WIDTH)
        x2, h2 = _xattn_out_and_norm(ox, xo_bf16, x1, norm_ffn_g[l])

        act, w2_bf16 = _fused_matmul("ffn_up", [h2],
                                     [(0, Rhs(w1_bf16)), (0, Rhs(w3_bf16))],
                                     [], _ep_swiglu, ffn_hidden, _BF16,
                                     side_job=functools.partial(_cast_side_job, ffn_w2, l))
        xf = _fused_matmul("ffn_down", [act], [(0, Rhs(w2_bf16))], [x2], _ep_residual, d, _F32)

    return _rmsnorm(xf, final_norm_g, _F32).reshape(bsz, seq, d)
```

```python
import functools
from typing import Callable, NamedTuple, Optional

import jax
import jax.numpy as jnp
from jax import lax
from jax.experimental import pallas as pl
from jax.experimental.pallas import tpu as pltpu

D_MODEL = 4096
MEM_LEN = 256
EPS = 1e-6
ROPE_THETA = 500000.0
CHUNK = 128
A_GROUPS = 16
A_WIDTH = D_MODEL // 2
A_GROUP_DIM = A_WIDTH // A_GROUPS
B_HEADS = 16
B_HEAD_DIM = 128
B_KV_HEADS = 4
B_WIDTH = B_HEADS * B_HEAD_DIM
B_KV_WIDTH = B_KV_HEADS * B_HEAD_DIM
B_ROT = B_HEAD_DIM // 4
IDX_HEADS = 16
IDX_DIM = 64
IDX_WIDTH = IDX_HEADS * IDX_DIM
IDX_ROT = IDX_DIM // 4
TOPK_MAX = 256
Q_BLOCK = 128
X_HEADS = 4
X_HEAD_DIM = 256
X_WIDTH = X_HEADS * X_HEAD_DIM
IN_SIZES = (2 * A_WIDTH, B_WIDTH, B_KV_WIDTH, B_KV_WIDTH, IDX_WIDTH, IDX_DIM, IDX_HEADS,
            2 * D_MODEL)

LANES = 128
SUBLANES = 8
KEY_CHUNK = 512
PROJ_TN = 512
ZA_END = IN_SIZES[0]
ATT_END = ZA_END + sum(IN_SIZES[1:7])
ATT_Q0 = 0
ATT_K0 = ATT_Q0 + B_WIDTH
ATT_V0 = ATT_K0 + B_KV_WIDTH
ATT_QI0 = ATT_V0 + B_KV_WIDTH
ATT_KW0 = ATT_QI0 + IDX_WIDTH
ATT_WIDTH = -(-(ATT_END - ZA_END) // PROJ_TN) * PROJ_TN
assert ZA_END % PROJ_TN == 0 and ATT_KW0 + LANES <= ATT_WIDTH

VMEM_LIMIT_BYTES = 60 * 1024 * 1024


class Tile(NamedTuple):
    tm: int
    tn: int
    row_splits: int = 1


TILES = {
    "proj_att": Tile(1024, PROJ_TN),
    "proj_za": Tile(1024, 512),
    "gated_merge": Tile(1024, 256, row_splits=4),
    "out_proj": Tile(1024, 512),
    "ffn_up": Tile(2048, 256, row_splits=8),
    "ffn_down": Tile(512, 512),
}
ROW_BLOCK = {
    "rmsnorm": 512,
    "spatial_gating": 4 * CHUNK,
    "cross_attention": 1024,
    "norm_proj": 512,
    "xattn_out_norm": 512,
}

_BF16 = jnp.bfloat16
_F32 = jnp.float32
_NEG_INF = float("-inf")
_NN_DIMS = (((1,), (0,)), ((), ()))
_NT_DIMS = (((1,), (1,)), ((), ()))


def _compiler_params(n_axes):
    return pltpu.CompilerParams(
        dimension_semantics=("arbitrary",) * n_axes,
        vmem_limit_bytes=VMEM_LIMIT_BYTES,
    )


def _rmsnorm_kernel(x_ref, g_ref, o_ref):
    x = x_ref[...]
    ms = jnp.mean(x * x, axis=-1, keepdims=True)
    y = x * lax.rsqrt(ms + EPS)
    o_ref[...] = (y * g_ref[...]).astype(o_ref.dtype)


def _rmsnorm(x, g, out_dtype):
    m, d = x.shape
    tm = ROW_BLOCK["rmsnorm"]
    return pl.pallas_call(
        _rmsnorm_kernel,
        grid=(m // tm,),
        in_specs=[pl.BlockSpec((tm, d), lambda i: (i, 0)),
                  pl.BlockSpec((1, d), lambda i: (0, 0))],
        out_specs=pl.BlockSpec((tm, d), lambda i: (i, 0)),
        out_shape=jax.ShapeDtypeStruct((m, d), out_dtype),
        compiler_params=_compiler_params(1),
        name="rmsnorm",
    )(x, g.reshape(1, d))


def _ep_identity(accs, extras):
    return accs[0]


def _ep_residual(accs, extras):
    return extras[0] + accs[0]


def _ep_gated_merge(accs, extras):
    y_a, y_b, gate_a, gate_b = accs
    return jax.nn.sigmoid(gate_a) * y_a + jax.nn.sigmoid(gate_b) * y_b


def _ep_swiglu(accs, extras):
    return jax.nn.silu(accs[0]) * accs[1]


class Rhs(NamedTuple):
    array: jax.Array
    layer: Optional[int] = None
    col0: int = 0
    transposed: bool = False


def _rhs_spec(r, tn):
    if r.transposed and r.layer is None:
        assert r.col0 % tn == 0
        off = r.col0 // tn
        return pl.BlockSpec((tn, r.array.shape[1]), lambda i, j: (j + off, 0))
    if r.transposed:
        k = r.array.shape[2]
        if r.col0 % tn == 0:
            return pl.BlockSpec((None, tn, k), lambda i, j: (r.layer, j + r.col0 // tn, 0))
        assert r.col0 % SUBLANES == 0
        return pl.BlockSpec((pl.Element(1), pl.Element(tn), pl.Element(k)),
                            lambda i, j: (r.layer, (r.col0 // SUBLANES + j * (tn // SUBLANES))
                                          * SUBLANES, 0))
    assert r.col0 % tn == 0
    off = r.col0 // tn
    if r.layer is None:
        return pl.BlockSpec((r.array.shape[0], tn), lambda i, j: (0, j + off))
    return pl.BlockSpec((None, r.array.shape[1], tn), lambda i, j: (r.layer, 0, j + off))


class SideJob(NamedTuple):
    body: Callable
    arrays: tuple
    in_specs: tuple
    out_specs: tuple
    out_shape: tuple


def _cast_body(x_ref, o_ref):
    x = x_ref[0] if len(x_ref.shape) == 3 else x_ref[...]
    o_ref[...] = x.astype(o_ref.dtype)


def _cast_side_job(w_stack, layer, n_steps, row0=0, rows=None):
    rows = w_stack.shape[1] if rows is None else rows
    cols = w_stack.shape[2]
    slab = rows // n_steps
    assert slab * n_steps == rows and slab % (2 * SUBLANES) == 0
    if row0 % slab == 0:
        in_spec = pl.BlockSpec((None, slab, cols), lambda s: (layer, s + row0 // slab, 0))
    else:
        assert row0 % SUBLANES == 0
        in_spec = pl.BlockSpec(
            (pl.Element(1), pl.Element(slab), pl.Element(cols)),
            lambda s: (layer, (row0 // SUBLANES + s * (slab // SUBLANES)) * SUBLANES, 0))
    return SideJob(_cast_body, (w_stack,), (in_spec,),
                   (pl.BlockSpec((slab, cols), lambda s: (s, 0)),),
                   (jax.ShapeDtypeStruct((rows, cols), _BF16),))


def _merge_side_jobs(*jobs):
    n_in = [len(j.in_specs) for j in jobs]
    n_out = [len(j.out_specs) for j in jobs]

    def body(*refs):
        ins, outs = refs[:sum(n_in)], refs[sum(n_in):]
        i = o = 0
        for j, ni, no in zip(jobs, n_in, n_out):
            j.body(*ins[i:i + ni], *outs[o:o + no])
            i, o = i + ni, o + no

    return SideJob(body, sum((j.arrays for j in jobs), ()), sum((j.in_specs for j in jobs), ()),
                   sum((j.out_specs for j in jobs), ()), sum((j.out_shape for j in jobs), ()))


def _mm_kernel(*refs, n_lhs, pair_lhs, rhs_transposed, n_extra, epilogue, side_body, n_side_in,
               row_splits):
    n_rhs = len(pair_lhs)
    lhs_refs = refs[:n_lhs]
    rhs_refs = refs[n_lhs:n_lhs + n_rhs]
    n_in = n_lhs + n_rhs + n_extra
    extra_refs = refs[n_lhs + n_rhs:n_in]
    o_ref = refs[n_in + n_side_in]
    if side_body is not None:
        side_body(*refs[n_in:n_in + n_side_in], *refs[n_in + n_side_in + 1:])
    weights = [(r[0] if len(r.shape) == 3 else r[...]).astype(_BF16) for r in rhs_refs]
    rows_per_split = o_ref.shape[0] // row_splits
    for s in range(row_splits):
        rows = slice(s * rows_per_split, (s + 1) * rows_per_split)
        accs = [lax.dot_general(lhs_refs[li][rows, :], w, _NT_DIMS if tr else _NN_DIMS,
                                preferred_element_type=_F32)
                for li, w, tr in zip(pair_lhs, weights, rhs_transposed)]
        o_ref[rows, :] = epilogue(accs, [e[rows, :] for e in extra_refs]).astype(o_ref.dtype)


def _fused_matmul(name, lhs, pairs, extras, epilogue, n_out, out_dtype, side_job=None):
    tm, tn, row_splits = TILES[name]
    m = lhs[0].shape[0]
    assert m % tm == 0 and n_out % tn == 0
    grid = (m // tm, n_out // tn)
    in_specs = [pl.BlockSpec((tm, a.shape[1]), lambda i, j: (i, 0)) for a in lhs]
    in_specs += [_rhs_spec(r, tn) for _, r in pairs]
    in_specs += [pl.BlockSpec((tm, tn), lambda i, j: (i, j)) for _ in extras]
    out_specs = [pl.BlockSpec((tm, tn), lambda i, j: (i, j))]
    out_shape = [jax.ShapeDtypeStruct((m, n_out), out_dtype)]
    operands = list(lhs) + [p[1].array for p in pairs] + list(extras)
    side = None if side_job is None else side_job(grid[0] * grid[1])
    if side is not None:
        def per_step(spec):
            return pl.BlockSpec(spec.block_shape,
                                lambda i, j: spec.index_map(i * grid[1] + j))
        in_specs += [per_step(s) for s in side.in_specs]
        out_specs += [per_step(s) for s in side.out_specs]
        out_shape += list(side.out_shape)
        operands += list(side.arrays)
    kern = functools.partial(_mm_kernel, n_lhs=len(lhs), pair_lhs=tuple(p[0] for p in pairs),
                             rhs_transposed=tuple(p[1].transposed for p in pairs),
                             n_extra=len(extras), epilogue=epilogue,
                             side_body=None if side is None else side.body,
                             n_side_in=0 if side is None else len(side.in_specs),
                             row_splits=row_splits)
    outs = pl.pallas_call(
        kern,
        grid=grid,
        in_specs=in_specs,
        out_specs=out_specs,
        out_shape=out_shape,
        compiler_params=_compiler_params(2),
        name=name,
    )(*operands)
    return outs[0] if side is None else tuple(outs)


def _gating_kernel(za_ref, ng_ref, w_ref, bt_ref, o_ref):
    row = lax.broadcasted_iota(jnp.int32, (CHUNK, CHUNK), 0)
    col = lax.broadcasted_iota(jnp.int32, (CHUNK, CHUNK), 1)
    causal = col <= row
    for c in range(za_ref.shape[0] // CHUNK):
        rows = slice(c * CHUNK, (c + 1) * CHUNK)
        z = jax.nn.gelu(za_ref[rows, :])
        u = z[:, :A_WIDTH]
        v = z[:, A_WIDTH:]
        ms = jnp.mean(v * v, axis=-1, keepdims=True)
        vn = ((v * lax.rsqrt(ms + EPS)) * ng_ref[...]).astype(_BF16)
        for g in range(A_GROUPS):
            sl = slice(g * A_GROUP_DIM, (g + 1) * A_GROUP_DIM)
            w = jnp.where(causal, w_ref[g], 0.0).astype(_BF16)
            s = jnp.dot(w, vn[:, sl], preferred_element_type=_F32) + bt_ref[:, g:g + 1]
            o_ref[rows, sl] = (u[:, sl] * s).astype(o_ref.dtype)


def _spatial_gating(za, norm_g, w_s, b_s):
    m = za.shape[0]
    tm = ROW_BLOCK["spatial_gating"]
    return pl.pallas_call(
        _gating_kernel,
        grid=(m // tm,),
        in_specs=[pl.BlockSpec((tm, 2 * A_WIDTH), lambda i: (i, 0)),
                  pl.BlockSpec((1, A_WIDTH), lambda i: (0, 0)),
                  pl.BlockSpec((A_GROUPS, CHUNK, CHUNK), lambda i: (0, 0, 0)),
                  pl.BlockSpec((CHUNK, A_GROUPS), lambda i: (0, 0))],
        out_specs=pl.BlockSpec((tm, A_WIDTH), lambda i: (i, 0)),
        out_shape=jax.ShapeDtypeStruct((m, A_WIDTH), _BF16),
        compiler_params=_compiler_params(1),
        name="spatial_gating",
    )(za, norm_g.reshape(1, A_WIDTH), w_s, b_s.T)


def _lane_inv_freq(rot_dim, period):
    inv_freq = ROPE_THETA ** (-jnp.arange(0, rot_dim, 2, dtype=_F32) / rot_dim)
    head = jnp.concatenate([inv_freq, inv_freq, jnp.zeros((period - rot_dim,), _F32)])
    return jnp.tile(head, LANES // period).reshape(1, LANES)


def _rope_tables(pos, inv_freq_lanes, rot_dim, period):
    half = rot_dim // 2
    ang = pos * inv_freq_lanes
    cos, sin = jnp.cos(ang), jnp.sin(ang)
    lane = lax.broadcasted_iota(jnp.int32, ang.shape, 1) & (period - 1)
    c = jnp.where(lane < rot_dim, cos, 1.0)
    s1 = jnp.where(lane >= half, jnp.where(lane < rot_dim, sin, 0.0), 0.0)
    s2 = jnp.where(lane < half, -sin, 0.0)
    return c, s1, s2


def _rope(x, c, s1, s2, half):
    return x * c + pltpu.roll(x, half, 1) * s1 + pltpu.roll(x, LANES - half, 1) * s2


def _prep_kernel(att_ref, pos_ref, fb_ref, fi_ref,
                 q_ref, k_ref, vt_ref, qi_ref, ki_ref, wit_ref):
    pos = pos_ref[0]
    cb, s1b, s2b = _rope_tables(pos, fb_ref[...], B_ROT, B_HEAD_DIM)
    ci, s1i, s2i = _rope_tables(pos, fi_ref[...], IDX_ROT, IDX_DIM)
    for h in range(B_HEADS):
        x = att_ref[0, :, ATT_Q0 + h * LANES:ATT_Q0 + (h + 1) * LANES]
        q_ref[0, :, h * LANES:(h + 1) * LANES] = _rope(x, cb, s1b, s2b, B_ROT // 2).astype(_BF16)
    for h in range(B_KV_HEADS):
        x = att_ref[0, :, ATT_K0 + h * LANES:ATT_K0 + (h + 1) * LANES]
        k_ref[0, :, h * LANES:(h + 1) * LANES] = _rope(x, cb, s1b, s2b, B_ROT // 2).astype(_BF16)
    vt_ref[0, 0] = att_ref[0, :, ATT_V0:ATT_V0 + B_KV_WIDTH].T.astype(_BF16)
    lane = lax.broadcasted_iota(jnp.int32, ci.shape, 1)
    is_ki = lane < IDX_DIM
    heads_per_group = LANES // IDX_DIM
    for j in range(IDX_WIDTH // LANES):
        x = att_ref[0, :, ATT_QI0 + j * LANES:ATT_QI0 + (j + 1) * LANES]
        y = _rope(x, ci, s1i, s2i, IDX_ROT // 2)
        for r in range(heads_per_group):
            h = j * heads_per_group + r
            yr = y if r == 0 else pltpu.roll(y, LANES - r * IDX_DIM, 1)
            qi_ref[0, :, h * LANES:(h + 1) * LANES] = jnp.where(is_ki, yr, 0.0).astype(_BF16)
    x = att_ref[0, :, ATT_KW0:ATT_KW0 + LANES]
    y = _rope(x, jnp.where(is_ki, ci, 1.0), jnp.where(is_ki, s1i, 0.0),
              jnp.where(is_ki, s2i, 0.0), IDX_ROT // 2)
    ki_ref[0] = jnp.where(is_ki, y, 0.0).astype(_BF16)
    wit_ref[0] = y.T[IDX_DIM:IDX_DIM + IDX_HEADS, :]


def _prep_side_job(att, positions, n_steps):
    bsz, seq, _ = att.shape
    ts = bsz * seq // n_steps
    assert ts * n_steps == bsz * seq and seq % ts == 0 and KEY_CHUNK % ts == 0
    assert ts % (2 * SUBLANES) == 0
    per_seq = seq // ts
    per_chunk = KEY_CHUNK // ts

    def tokens(width):
        return pl.BlockSpec((1, ts, width), lambda s: (s // per_seq, s % per_seq, 0))

    def out(width, dtype):
        return tokens(width), jax.ShapeDtypeStruct((bsz, seq, width), dtype)

    vt_out = (pl.BlockSpec((1, 1, B_KV_WIDTH, ts),
                           lambda s: (s // per_seq, (s % per_seq) // per_chunk, 0,
                                      (s % per_seq) % per_chunk)),
              jax.ShapeDtypeStruct((bsz, seq // KEY_CHUNK, B_KV_WIDTH, KEY_CHUNK), _BF16))
    wit_out = (pl.BlockSpec((1, IDX_HEADS, ts), lambda s: (s // per_seq, 0, s % per_seq)),
               jax.ShapeDtypeStruct((bsz, IDX_HEADS, seq), _F32))
    outs = [out(B_WIDTH, _BF16), out(B_KV_WIDTH, _BF16), vt_out,
            out(IDX_HEADS * LANES, _BF16), out(LANES, _BF16), wit_out]
    freq_spec = pl.BlockSpec((1, LANES), lambda s: (0, 0))
    return SideJob(_prep_kernel,
                   (att, positions.astype(_F32).reshape(bsz, seq, 1),
                    _lane_inv_freq(B_ROT, B_HEAD_DIM), _lane_inv_freq(IDX_ROT, IDX_DIM)),
                   (tokens(ATT_WIDTH), tokens(1), freq_spec, freq_spec),
                   tuple(o[0] for o in outs), tuple(o[1] for o in outs))


def _ordered_int_to_float(key):
    bits = jnp.where(key < 0, key ^ jnp.int32(0x7FFFFFFF), key)
    return lax.bitcast_convert_type(bits, _F32)


def _fold_rows(x, op):
    while x.shape[0] > SUBLANES:
        half = x.shape[0] // 2
        x = op(x[:half], x[half:])
    return x


_DSA_N_IN = 6


def _dsa_kernel(*refs, n_sel, side_body, n_side_in, n_side_out):
    o_at = _DSA_N_IN + n_side_in
    if side_body is not None:
        side_body(*refs[_DSA_N_IN:o_at], *refs[o_at + 1:o_at + 1 + n_side_out])
    own = refs[:_DSA_N_IN] + (refs[o_at],) + refs[o_at + 1 + n_side_out:]
    seq = own[4].shape[1]
    n_chunks = (pl.program_id(1) * Q_BLOCK + Q_BLOCK + KEY_CHUNK - 1) // KEY_CHUNK
    for n in range(1, seq // KEY_CHUNK + 1):
        pl.when(n_chunks == n)(functools.partial(_dsa_block, *own, n_sel=n_sel, n_chunks=n))


def _dsa_block(qi_ref, wit_ref, ki_ref, q_ref, k_ref, vt_ref, o_ref,
               score_ref, bias_ref, logit_ref, acc_ref, *, n_sel, n_chunks):
    t = Q_BLOCK
    blk = pl.program_id(1)
    qpos = blk * t + lax.broadcasted_iota(jnp.int32, (1, t), 1)
    kiota = lax.broadcasted_iota(jnp.int32, (KEY_CHUNK, 1), 0)
    idx_scale = (IDX_DIM ** -0.5) * (IDX_HEADS ** -0.5)

    def rows(c):
        return slice(c * KEY_CHUNK, (c + 1) * KEY_CHUNK)

    def for_chunks(body, carry):
        for c in range(n_chunks):
            carry = body(c, carry)
        return carry

    def indexer_chunk(c, carry):
        ki = ki_ref[0, rows(c), :]
        acc = jnp.zeros((KEY_CHUNK, t), _F32)
        for h in range(0, IDX_HEADS, 2):
            qpair = jnp.concatenate([qi_ref[0, :, h * LANES:(h + 1) * LANES],
                                     qi_ref[0, :, (h + 1) * LANES:(h + 2) * LANES]], axis=0)
            d = lax.dot_general(ki, qpair, _NT_DIMS, preferred_element_type=_F32)
            acc = acc + wit_ref[0, h:h + 1, :] * jnp.maximum(d[:, :t], 0.0)
            acc = acc + wit_ref[0, h + 1:h + 2, :] * jnp.maximum(d[:, t:], 0.0)
        kpos = c * KEY_CHUNK + kiota
        score_ref[rows(c), :] = jnp.where(kpos <= qpos, acc * idx_scale + 0.0, _NEG_INF)
        return carry

    for_chunks(indexer_chunk, 0)

    def count(indicator):
        def chunk(c, part):
            return part + _fold_rows(indicator(score_ref[rows(c), :], c), jnp.add)
        part = for_chunks(chunk, jnp.zeros((SUBLANES, t), _F32))
        return jnp.sum(part, axis=0, keepdims=True)

    def count_ge(cand_f):
        return count(lambda s, c: jnp.where(s >= cand_f, 1.0, 0.0))

    def search():
        int_min = jnp.int32(-2 ** 31)
        base = jnp.where(count_ge(jnp.zeros((1, t), _F32)) >= n_sel, jnp.int32(0), int_min)

        def search_step(it, base):
            cand = base | lax.shift_left(jnp.int32(1), jnp.int32(30) - it)
            cnt = count_ge(_ordered_int_to_float(cand))
            return jnp.where(cnt >= n_sel, cand, base)

        return lax.fori_loop(0, 31, search_step, base)

    if (n_chunks - 1) * KEY_CHUNK >= n_sel:
        base = search()
    else:
        base = lax.cond((blk + 1) * t <= n_sel, lambda: jnp.zeros((1, t), jnp.int32), search)
    thr = jnp.where(qpos + 1 <= n_sel, _NEG_INF, _ordered_int_to_float(base))

    def bias_chunk(c, part):
        s = score_ref[rows(c), :]
        sel = jnp.where(c * KEY_CHUNK + kiota <= qpos, jnp.where(s >= thr, 1.0, 0.0), 0.0)
        bias_ref[rows(c), :] = jnp.where(sel > 0.0, 0.0, _NEG_INF)
        return part + _fold_rows(sel, jnp.add)

    n_ge = jnp.sum(for_chunks(bias_chunk, jnp.zeros((SUBLANES, t), _F32)), axis=0, keepdims=True)

    @pl.when(jnp.max(n_ge) > n_sel)
    def _():
        nxt = _ordered_int_to_float(base + 1)
        tied = n_ge > n_sel
        n_above = count(lambda s, c: jnp.where(s >= nxt, 1.0, 0.0))
        need = jnp.where(tied, n_sel - n_above, 0.0)
        no_index = float(k_ref.shape[1])
        front_v = jnp.where(tied, jnp.inf, _NEG_INF)
        front_i = jnp.where(tied, -1.0, no_index)

        def beyond(s, kposf, front_v, front_i):
            after = jnp.where(s < front_v, 1.0,
                              jnp.where(s == front_v, jnp.where(kposf > front_i, 1.0, 0.0), 0.0))
            return jnp.where(s >= thr, jnp.where(s < nxt, after, 0.0), 0.0)

        def kposf(c):
            return (c * KEY_CHUNK + kiota).astype(_F32)

        def advance(_, carry):
            front_v, front_i, need = carry

            def best_value(c, part):
                s = score_ref[rows(c), :]
                cand = jnp.where(beyond(s, kposf(c), front_v, front_i) > 0.0, s, _NEG_INF)
                return jnp.maximum(part, _fold_rows(cand, jnp.maximum))

            v = jnp.max(for_chunks(best_value, jnp.full((SUBLANES, t), _NEG_INF, _F32)),
                        axis=0, keepdims=True)

            def first_index(c, part):
                s = score_ref[rows(c), :]
                hit = jnp.where(s == v, beyond(s, kposf(c), front_v, front_i), 0.0)
                cand = jnp.where(hit > 0.0, kposf(c), no_index)
                return jnp.minimum(part, _fold_rows(cand, jnp.minimum))

            i = jnp.min(for_chunks(first_index, jnp.full((SUBLANES, t), no_index, _F32)),
                        axis=0, keepdims=True)
            active = need > 0.0
            return (jnp.where(active, v, front_v), jnp.where(active, i, front_i),
                    jnp.where(active, need - 1.0, need))

        front_v, front_i, _ = lax.fori_loop(0, jnp.max(need).astype(jnp.int32), advance,
                                            (front_v, front_i, need))

        def tie_chunk(c, carry):
            s = score_ref[rows(c), :]
            upto = jnp.where(s > front_v, 0.0,
                             jnp.where(s == front_v,
                                       jnp.where(kposf(c) <= front_i, 0.0, _NEG_INF), _NEG_INF))
            keep = jnp.where(s >= nxt, 0.0, jnp.where(s >= thr, upto, _NEG_INF))
            bias_ref[rows(c), :] = jnp.where(c * KEY_CHUNK + kiota <= qpos, keep, _NEG_INF)
            return carry

        for_chunks(tie_chunk, 0)

    grp = B_HEADS // B_KV_HEADS
    gw = grp * t
    exp2_scale = (B_HEAD_DIM ** -0.5) * 1.4426950408889634

    def logit_chunk(c, mx):
        bias = jnp.concatenate([bias_ref[rows(c), :]] * grp, axis=1)
        folded = []
        for g in range(B_KV_HEADS):
            qg = jnp.concatenate(
                [q_ref[0, :, (g * grp + hh) * B_HEAD_DIM:(g * grp + hh + 1) * B_HEAD_DIM]
                 for hh in range(grp)], axis=0)
            l = lax.dot_general(k_ref[0, rows(c), g * B_HEAD_DIM:(g + 1) * B_HEAD_DIM], qg,
                                _NT_DIMS, preferred_element_type=_F32) + bias
            logit_ref[rows(c), g * gw:(g + 1) * gw] = l
            folded.append(_fold_rows(l, jnp.maximum))
        return jnp.maximum(mx, jnp.concatenate(folded, axis=1))

    mx = for_chunks(logit_chunk, jnp.full((SUBLANES, B_HEADS * t), _NEG_INF, _F32))
    mx = jnp.max(mx, axis=0, keepdims=True)

    acc_ref[...] = jnp.zeros_like(acc_ref)
    ones_rows = jnp.ones((acc_ref.shape[0] - B_HEAD_DIM, KEY_CHUNK), _BF16)

    def pv_chunk(c, carry):
        for g in range(B_KV_HEADS):
            cols = slice(g * gw, (g + 1) * gw)
            p = jnp.exp2((logit_ref[rows(c), cols] - mx[:, cols]) * exp2_scale)
            vt_ext = jnp.concatenate(
                [vt_ref[0, c, g * B_HEAD_DIM:(g + 1) * B_HEAD_DIM, :], ones_rows], axis=0)
            acc_ref[:, cols] += jnp.dot(vt_ext, p.astype(_BF16), preferred_element_type=_F32)
        return carry

    for_chunks(pv_chunk, 0)
    out_t = acc_ref[:B_HEAD_DIM, :] / acc_ref[B_HEAD_DIM:B_HEAD_DIM + 1, :]
    for h in range(B_HEADS):
        o_ref[0, :, h * B_HEAD_DIM:(h + 1) * B_HEAD_DIM] = (
            out_t[:, h * t:(h + 1) * t].T.astype(o_ref.dtype))


def _dsa_attention(q, k, vt, qi, ki, wit, side_job=None):
    bsz, seq, _ = q.shape
    assert Q_BLOCK == LANES and seq % KEY_CHUNK == 0
    n_sel = min(TOPK_MAX, seq // 4)
    n_blk = seq // Q_BLOCK

    def q_spec(width):
        return pl.BlockSpec((1, Q_BLOCK, width), lambda b, i: (b, i, 0))

    def kv_spec(width):
        return pl.BlockSpec((1, seq, width), lambda b, i: (b, 0, 0))

    def per_step(spec):
        return pl.BlockSpec(spec.block_shape, lambda b, i: spec.index_map(b * n_blk + i))

    side = None if side_job is None else side_job(bsz * n_blk)
    side_in = [] if side is None else [per_step(s) for s in side.in_specs]
    side_out = [] if side is None else [per_step(s) for s in side.out_specs]
    outs = pl.pallas_call(
        functools.partial(_dsa_kernel, n_sel=n_sel,
                          side_body=None if side is None else side.body,
                          n_side_in=len(side_in), n_side_out=len(side_out)),
        grid=(bsz, n_blk),
        in_specs=[q_spec(IDX_HEADS * LANES),
                  pl.BlockSpec((1, IDX_HEADS, Q_BLOCK), lambda b, i: (b, 0, i)),
                  kv_spec(LANES), q_spec(B_WIDTH), kv_spec(B_KV_WIDTH),
                  pl.BlockSpec((1, seq // KEY_CHUNK, B_KV_WIDTH, KEY_CHUNK),
                               lambda b, i: (b, 0, 0, 0))] + side_in,
        out_specs=[q_spec(B_WIDTH)] + side_out,
        out_shape=[jax.ShapeDtypeStruct((bsz, seq, B_WIDTH), _BF16)]
        + ([] if side is None else list(side.out_shape)),
        scratch_shapes=[pltpu.VMEM((seq, Q_BLOCK), _F32), pltpu.VMEM((seq, Q_BLOCK), _F32),
                        pltpu.VMEM((seq, B_HEADS * Q_BLOCK), _F32),
                        pltpu.VMEM((B_HEAD_DIM + 2 * SUBLANES, B_HEADS * Q_BLOCK), _F32)],
        compiler_params=_compiler_params(2),
        name="dsa_attention",
    )(qi, wit, ki, q, k, vt, *(() if side is None else side.arrays))
    return outs[0] if side is None else tuple(outs)


def _xattn_kernel(q_ref, k_ref, v_ref, o_ref):
    scale = X_HEAD_DIM ** -0.5
    for h in range(X_HEADS):
        hs = slice(h * X_HEAD_DIM, (h + 1) * X_HEAD_DIM)
        logits = lax.dot_general(q_ref[0, :, hs], k_ref[0, :, hs], _NT_DIMS,
                                 preferred_element_type=_F32) * scale
        mx = jnp.max(logits, axis=-1, keepdims=True)
        p = jnp.exp(logits - mx)
        denom = jnp.sum(p, axis=-1, keepdims=True)
        o = jnp.dot(p.astype(_BF16), v_ref[0, :, hs], preferred_element_type=_F32)
        o_ref[0, :, hs] = (o / denom).astype(o_ref.dtype)


def _cross_attention(q, k, v):
    bsz, seq, _ = q.shape
    tq = ROW_BLOCK["cross_attention"]
    m = k.shape[1]
    mem_spec = pl.BlockSpec((1, m, X_WIDTH), lambda b, i: (b, 0, 0))
    return pl.pallas_call(
        _xattn_kernel,
        grid=(bsz, seq // tq),
        in_specs=[pl.BlockSpec((1, tq, X_WIDTH), lambda b, i: (b, i, 0)), mem_spec, mem_spec],
        out_specs=pl.BlockSpec((1, tq, X_WIDTH), lambda b, i: (b, i, 0)),
        out_shape=jax.ShapeDtypeStruct((bsz, seq, X_WIDTH), _BF16),
        compiler_params=_compiler_params(2),
        name="cross_attention",
    )(q, k, v)


def _norm_proj_kernel(x_ref, g_ref, w_ref, o_ref):
    x = x_ref[...]
    ms = jnp.mean(x * x, axis=-1, keepdims=True)
    h = ((x * lax.rsqrt(ms + EPS)) * g_ref[...]).astype(_BF16)
    o_ref[...] = jnp.dot(h, w_ref[...], preferred_element_type=_F32).astype(o_ref.dtype)


def _norm_proj(x, g, w_bf16, out_dtype, name):
    m, d = x.shape
    tm = ROW_BLOCK["norm_proj"]
    n = w_bf16.shape[1]
    return pl.pallas_call(
        _norm_proj_kernel,
        grid=(m // tm,),
        in_specs=[pl.BlockSpec((tm, d), lambda i: (i, 0)),
                  pl.BlockSpec((1, d), lambda i: (0, 0)),
                  pl.BlockSpec((d, n), lambda i: (0, 0), pipeline_mode=pl.Buffered(1))],
        out_specs=pl.BlockSpec((tm, n), lambda i: (i, 0)),
        out_shape=jax.ShapeDtypeStruct((m, n), out_dtype),
        compiler_params=_compiler_params(1),
        name=name,
    )(x, g.reshape(1, d), w_bf16)


def _xattn_out_kernel(o_ref, w_ref, x_ref, g_ref, x_out_ref, h_out_ref):
    x = x_ref[...] + jnp.dot(o_ref[...], w_ref[...], preferred_element_type=_F32)
    x_out_ref[...] = x
    ms = jnp.mean(x * x, axis=-1, keepdims=True)
    h_out_ref[...] = ((x * lax.rsqrt(ms + EPS)) * g_ref[...]).astype(h_out_ref.dtype)


def _xattn_out_and_norm(o, w_bf16, x, g):
    m, k = o.shape
    tm = ROW_BLOCK["xattn_out_norm"]
    d = x.shape[1]
    row_spec = pl.BlockSpec((tm, d), lambda i: (i, 0))
    return pl.pallas_call(
        _xattn_out_kernel,
        grid=(m // tm,),
        in_specs=[pl.BlockSpec((tm, k), lambda i: (i, 0)),
                  pl.BlockSpec((k, d), lambda i: (0, 0), pipeline_mode=pl.Buffered(1)),
                  row_spec,
                  pl.BlockSpec((1, d), lambda i: (0, 0))],
        out_specs=[row_spec, row_spec],
        out_shape=[jax.ShapeDtypeStruct((m, d), _F32), jax.ShapeDtypeStruct((m, d), _BF16)],
        compiler_params=_compiler_params(1),
        name="xattn_out_norm",
    )(o, w_bf16, x, g.reshape(1, d))


def kernel(x, mem, positions, norm_mix_g, w_in, a_norm_g, a_spatial_w, a_spatial_b, p_a, p_b,
           w_out, norm_x_g, norm_mem_g, xq_w, xk_w, xv_w, xo_w, norm_ffn_g, ffn_w1, ffn_w3,
           ffn_w2, final_norm_g):
    bsz, seq, d = x.shape
    m = bsz * seq
    depth = w_in.shape[0]
    ffn_hidden = ffn_w1.shape[-1]

    xf = x.reshape(m, d)
    mem_f = mem.reshape(bsz * MEM_LEN, d)
    w_in_t = jnp.transpose(w_in, (0, 2, 1))
    for l in range(depth):
        def w_in_cols(col0):
            return Rhs(w_in_t, l, col0, transposed=True)

        h = _rmsnorm(xf, norm_mix_g[l], _BF16)
        att = _fused_matmul("proj_att", [h], [(0, w_in_cols(ZA_END))], [], _ep_identity,
                            ATT_WIDTH, _F32)
        za, q, k, vt, qi, ki, wit = _fused_matmul(
            "proj_za", [h], [(0, w_in_cols(0))], [], _ep_identity, ZA_END, _F32,
            side_job=functools.partial(_prep_side_job, att.reshape(bsz, seq, ATT_WIDTH),
                                       positions))
        y_a = _spatial_gating(za, a_norm_g[l], a_spatial_w[l], a_spatial_b[l])
        def weight_casts(n_steps):
            return _merge_side_jobs(
                _cast_side_job(p_a, l, n_steps), _cast_side_job(p_b, l, n_steps),
                _cast_side_job(w_in_t, l, n_steps, row0=ATT_END, rows=2 * d),
                _cast_side_job(w_out, l, n_steps), _cast_side_job(xq_w, l, n_steps),
                _cast_side_job(xk_w, l, n_steps), _cast_side_job(xv_w, l, n_steps))

        y_b, pa_bf16, pb_bf16, gate_wt_bf16, wo_bf16, xq_bf16, xk_bf16, xv_bf16 = _dsa_attention(
            q, k, vt, qi, ki, wit, side_job=weight_casts)

        def ffn_up_weight_casts(n_steps):
            return _merge_side_jobs(_cast_side_job(ffn_w1, l, n_steps),
                                    _cast_side_job(ffn_w3, l, n_steps))

        merged, w1_bf16, w3_bf16 = _fused_matmul(
            "gated_merge", [y_a, y_b.reshape(m, B_WIDTH), h],
            [(0, Rhs(pa_bf16)), (1, Rhs(pb_bf16)), (2, Rhs(gate_wt_bf16, transposed=True)),
             (2, Rhs(gate_wt_bf16, col0=d, transposed=True))],
            [], _ep_gated_merge, d, _BF16, side_job=ffn_up_weight_casts)
        x1, xo_bf16 = _fused_matmul("out_proj", [merged], [(0, Rhs(wo_bf16))], [xf],
                                    _ep_residual, d, _F32,
                                    side_job=functools.partial(_cast_side_job, xo_w, l))

        qx = _norm_proj(x1, norm_x_g[l], xq_bf16, _BF16, "xattn_q")
        kx = _norm_proj(mem_f, norm_mem_g[l], xk_bf16, _BF16, "xattn_k")
        vx = _norm_proj(mem_f, norm_mem_g[l], xv_bf16, _BF16, "xattn_v")
        ox = _cross_attention(qx.reshape(bsz, seq, X_WIDTH),
                              kx.reshape(bsz, MEM_LEN, X_WIDTH),
                              vx.reshape(bsz, MEM_LEN, X_WIDTH)).reshape(m, X_WIDTH)
        x2, h2 = _xattn_out_and_norm(ox, xo_bf16, x1, norm_ffn_g[l])

        act, w2_bf16 = _fused_matmul("ffn_up", [h2],
                                     [(0, Rhs(w1_bf16)), (0, Rhs(w3_bf16))],
                                     [], _ep_swiglu, ffn_hidden, _BF16,
                                     side_job=functools.partial(_cast_side_job, ffn_w2, l))
        xf = _fused_matmul("ffn_down", [act], [(0, Rhs(w2_bf16))], [x2], _ep_residual, d, _F32)

    return _rmsnorm(xf, final_norm_g, _F32).reshape(bsz, seq, d)
```

```python
import functools
from typing import Callable, NamedTuple, Optional

import jax
import jax.numpy as jnp
from jax import lax
from jax.experimental import pallas as pl
from jax.experimental.pallas import tpu as pltpu

D_MODEL = 4096
MEM_LEN = 256
EPS = 1e-6
ROPE_THETA = 500000.0
CHUNK = 128
A_GROUPS = 16
A_WIDTH = D_MODEL // 2
A_GROUP_DIM = A_WIDTH // A_GROUPS
B_HEADS = 16
B_HEAD_DIM = 128
B_KV_HEADS = 4
B_WIDTH = B_HEADS * B_HEAD_DIM
B_KV_WIDTH = B_KV_HEADS * B_HEAD_DIM
B_ROT = B_HEAD_DIM // 4
IDX_HEADS = 16
IDX_DIM = 64
IDX_WIDTH = IDX_HEADS * IDX_DIM
IDX_ROT = IDX_DIM // 4
TOPK_MAX = 256
Q_BLOCK = 128
X_HEADS = 4
X_HEAD_DIM = 256
X_WIDTH = X_HEADS * X_HEAD_DIM
IN_SIZES = (2 * A_WIDTH, B_WIDTH, B_KV_WIDTH, B_KV_WIDTH, IDX_WIDTH, IDX_DIM, IDX_HEADS,
            2 * D_MODEL)

LANES = 128
SUBLANES = 8
KEY_CHUNK = 512
PROJ_TN = 256
ZA_END = IN_SIZES[0]
ATT_END = ZA_END + sum(IN_SIZES[1:7])
ATT_Q0 = 0
ATT_K0 = ATT_Q0 + B_WIDTH
ATT_V0 = ATT_K0 + B_KV_WIDTH
ATT_QI0 = ATT_V0 + B_KV_WIDTH
ATT_KW0 = ATT_QI0 + IDX_WIDTH
ATT_WIDTH = -(-(ATT_END - ZA_END) // PROJ_TN) * PROJ_TN
assert ZA_END % PROJ_TN == 0 and ATT_KW0 + LANES <= ATT_WIDTH

VMEM_LIMIT_BYTES = 60 * 1024 * 1024


class Tile(NamedTuple):
    tm: int
    tn: int
    row_splits: int = 1


TILES = {
    "proj_att": Tile(1024, PROJ_TN),
    "proj_za": Tile(1024, 512),
    "gated_merge": Tile(1024, 256, row_splits=4),
    "out_proj": Tile(1024, 512),
    "ffn_up": Tile(2048, 256, row_splits=4),
    "ffn_down": Tile(512, 512),
}
ROW_BLOCK = {
    "rmsnorm": 512,
    "spatial_gating": 4 * CHUNK,
    "cross_attention": 1024,
    "norm_proj": 512,
    "xattn_out_norm": 512,
}

_BF16 = jnp.bfloat16
_F32 = jnp.float32
_NEG_INF = float("-inf")
_NN_DIMS = (((1,), (0,)), ((), ()))
_NT_DIMS = (((1,), (1,)), ((), ()))


def _compiler_params(n_axes):
    return pltpu.CompilerParams(
        dimension_semantics=("arbitrary",) * n_axes,
        vmem_limit_bytes=VMEM_LIMIT_BYTES,
    )


def _rmsnorm_kernel(x_ref, g_ref, o_ref):
    x = x_ref[...]
    ms = jnp.mean(x * x, axis=-1, keepdims=True)
    y = x * lax.rsqrt(ms + EPS)
    o_ref[...] = (y * g_ref[...]).astype(o_ref.dtype)


def _rmsnorm(x, g, out_dtype):
    m, d = x.shape
    tm = ROW_BLOCK["rmsnorm"]
    return pl.pallas_call(
        _rmsnorm_kernel,
        grid=(m // tm,),
        in_specs=[pl.BlockSpec((tm, d), lambda i: (i, 0)),
                  pl.BlockSpec((1, d), lambda i: (0, 0))],
        out_specs=pl.BlockSpec((tm, d), lambda i: (i, 0)),
        out_shape=jax.ShapeDtypeStruct((m, d), out_dtype),
        compiler_params=_compiler_params(1),
        name="rmsnorm",
    )(x, g.reshape(1, d))


def _ep_identity(accs, extras):
    return accs[0]


def _ep_residual(accs, extras):
    return extras[0] + accs[0]


def _ep_gated_merge(accs, extras):
    y_a, y_b, gate_a, gate_b = accs
    return jax.nn.sigmoid(gate_a) * y_a + jax.nn.sigmoid(gate_b) * y_b


def _ep_swiglu(accs, extras):
    return jax.nn.silu(accs[0]) * accs[1]


class Rhs(NamedTuple):
    array: jax.Array
    layer: Optional[int] = None
    col0: int = 0
    transposed: bool = False


def _rhs_spec(r, tn):
    if r.transposed and r.layer is None:
        assert r.col0 % tn == 0
        off = r.col0 // tn
        return pl.BlockSpec((tn, r.array.shape[1]), lambda i, j: (j + off, 0))
    if r.transposed:
        k = r.array.shape[2]
        if r.col0 % tn == 0:
            return pl.BlockSpec((None, tn, k), lambda i, j: (r.layer, j + r.col0 // tn, 0))
        assert r.col0 % SUBLANES == 0
        return pl.BlockSpec((pl.Element(1), pl.Element(tn), pl.Element(k)),
                            lambda i, j: (r.layer, (r.col0 // SUBLANES + j * (tn // SUBLANES))
                                          * SUBLANES, 0))
    assert r.col0 % tn == 0
    off = r.col0 // tn
    if r.layer is None:
        return pl.BlockSpec((r.array.shape[0], tn), lambda i, j: (0, j + off))
    return pl.BlockSpec((None, r.array.shape[1], tn), lambda i, j: (r.layer, 0, j + off))


class SideJob(NamedTuple):
    body: Callable
    arrays: tuple
    in_specs: tuple
    out_specs: tuple
    out_shape: tuple


def _cast_body(x_ref, o_ref):
    x = x_ref[0] if len(x_ref.shape) == 3 else x_ref[...]
    o_ref[...] = x.astype(o_ref.dtype)


def _cast_side_job(w_stack, layer, n_steps, row0=0, rows=None):
    rows = w_stack.shape[1] if rows is None else rows
    cols = w_stack.shape[2]
    slab = rows // n_steps
    assert slab * n_steps == rows and slab % (2 * SUBLANES) == 0
    if row0 % slab == 0:
        in_spec = pl.BlockSpec((None, slab, cols), lambda s: (layer, s + row0 // slab, 0))
    else:
        assert row0 % SUBLANES == 0
        in_spec = pl.BlockSpec(
            (pl.Element(1), pl.Element(slab), pl.Element(cols)),
            lambda s: (layer, (row0 // SUBLANES + s * (slab // SUBLANES)) * SUBLANES, 0))
    return SideJob(_cast_body, (w_stack,), (in_spec,),
                   (pl.BlockSpec((slab, cols), lambda s: (s, 0)),),
                   (jax.ShapeDtypeStruct((rows, cols), _BF16),))


def _merge_side_jobs(*jobs):
    n_in = [len(j.in_specs) for j in jobs]
    n_out = [len(j.out_specs) for j in jobs]

    def body(*refs):
        ins, outs = refs[:sum(n_in)], refs[sum(n_in):]
        i = o = 0
        for j, ni, no in zip(jobs, n_in, n_out):
            j.body(*ins[i:i + ni], *outs[o:o + no])
            i, o = i + ni, o + no

    return SideJob(body, sum((j.arrays for j in jobs), ()), sum((j.in_specs for j in jobs), ()),
                   sum((j.out_specs for j in jobs), ()), sum((j.out_shape for j in jobs), ()))


def _mm_kernel(*refs, n_lhs, pair_lhs, rhs_transposed, n_extra, epilogue, side_body, n_side_in,
               row_splits):
    n_rhs = len(pair_lhs)
    lhs_refs = refs[:n_lhs]
    rhs_refs = refs[n_lhs:n_lhs + n_rhs]
    n_in = n_lhs + n_rhs + n_extra
    extra_refs = refs[n_lhs + n_rhs:n_in]
    o_ref = refs[n_in + n_side_in]
    if side_body is not None:
        side_body(*refs[n_in:n_in + n_side_in], *refs[n_in + n_side_in + 1:])
    weights = [(r[0] if len(r.shape) == 3 else r[...]).astype(_BF16) for r in rhs_refs]
    rows_per_split = o_ref.shape[0] // row_splits
    for s in range(row_splits):
        rows = slice(s * rows_per_split, (s + 1) * rows_per_split)
        accs = [lax.dot_general(lhs_refs[li][rows, :], w, _NT_DIMS if tr else _NN_DIMS,
                                preferred_element_type=_F32)
                for li, w, tr in zip(pair_lhs, weights, rhs_transposed)]
        o_ref[rows, :] = epilogue(accs, [e[rows, :] for e in extra_refs]).astype(o_ref.dtype)


def _fused_matmul(name, lhs, pairs, extras, epilogue, n_out, out_dtype, side_job=None):
    tm, tn, row_splits = TILES[name]
    m = lhs[0].shape[0]
    assert m % tm == 0 and n_out % tn == 0
    grid = (m // tm, n_out // tn)
    in_specs = [pl.BlockSpec((tm, a.shape[1]), lambda i, j: (i, 0)) for a in lhs]
    in_specs += [_rhs_spec(r, tn) for _, r in pairs]
    in_specs += [pl.BlockSpec((tm, tn), lambda i, j: (i, j)) for _ in extras]
    out_specs = [pl.BlockSpec((tm, tn), lambda i, j: (i, j))]
    out_shape = [jax.ShapeDtypeStruct((m, n_out), out_dtype)]
    operands = list(lhs) + [p[1].array for p in pairs] + list(extras)
    side = None if side_job is None else side_job(grid[0] * grid[1])
    if side is not None:
        def per_step(spec):
            return pl.BlockSpec(spec.block_shape,
                                lambda i, j: spec.index_map(i * grid[1] + j))
        in_specs += [per_step(s) for s in side.in_specs]
        out_specs += [per_step(s) for s in side.out_specs]
        out_shape += list(side.out_shape)
        operands += list(side.arrays)
    kern = functools.partial(_mm_kernel, n_lhs=len(lhs), pair_lhs=tuple(p[0] for p in pairs),
                             rhs_transposed=tuple(p[1].transposed for p in pairs),
                             n_extra=len(extras), epilogue=epilogue,
                             side_body=None if side is None else side.body,
                             n_side_in=0 if side is None else len(side.in_specs),
                             row_splits=row_splits)
    outs = pl.pallas_call(
        kern,
        grid=grid,
        in_specs=in_specs,
        out_specs=out_specs,
        out_shape=out_shape,
        compiler_params=_compiler_params(2),
        name=name,
    )(*operands)
    return outs[0] if side is None else tuple(outs)


def _gating_kernel(za_ref, ng_ref, w_ref, bt_ref, o_ref):
    row = lax.broadcasted_iota(jnp.int32, (CHUNK, CHUNK), 0)
    col = lax.broadcasted_iota(jnp.int32, (CHUNK, CHUNK), 1)
    causal = col <= row
    for c in range(za_ref.shape[0] // CHUNK):
        rows = slice(c * CHUNK, (c + 1) * CHUNK)
        z = jax.nn.gelu(za_ref[rows, :])
        u = z[:, :A_WIDTH]
        v = z[:, A_WIDTH:]
        ms = jnp.mean(v * v, axis=-1, keepdims=True)
        vn = ((v * lax.rsqrt(ms + EPS)) * ng_ref[...]).astype(_BF16)
        for g in range(A_GROUPS):
            sl = slice(g * A_GROUP_DIM, (g + 1) * A_GROUP_DIM)
            w = jnp.where(causal, w_ref[g], 0.0).astype(_BF16)
            s = jnp.dot(w, vn[:, sl], preferred_element_type=_F32) + bt_ref[:, g:g + 1]
            o_ref[rows, sl] = (u[:, sl] * s).astype(o_ref.dtype)


def _spatial_gating(za, norm_g, w_s, b_s):
    m = za.shape[0]
    tm = ROW_BLOCK["spatial_gating"]
    return pl.pallas_call(
        _gating_kernel,
        grid=(m // tm,),
        in_specs=[pl.BlockSpec((tm, 2 * A_WIDTH), lambda i: (i, 0)),
                  pl.BlockSpec((1, A_WIDTH), lambda i: (0, 0)),
                  pl.BlockSpec((A_GROUPS, CHUNK, CHUNK), lambda i: (0, 0, 0)),
                  pl.BlockSpec((CHUNK, A_GROUPS), lambda i: (0, 0))],
        out_specs=pl.BlockSpec((tm, A_WIDTH), lambda i: (i, 0)),
        out_shape=jax.ShapeDtypeStruct((m, A_WIDTH), _BF16),
        compiler_params=_compiler_params(1),
        name="spatial_gating",
    )(za, norm_g.reshape(1, A_WIDTH), w_s, b_s.T)


def _lane_inv_freq(rot_dim, period):
    inv_freq = ROPE_THETA ** (-jnp.arange(0, rot_dim, 2, dtype=_F32) / rot_dim)
    head = jnp.concatenate([inv_freq, inv_freq, jnp.zeros((period - rot_dim,), _F32)])
    return jnp.tile(head, LANES // period).reshape(1, LANES)


def _rope_tables(pos, inv_freq_lanes, rot_dim, period):
    half = rot_dim // 2
    ang = pos * inv_freq_lanes
    cos, sin = jnp.cos(ang), jnp.sin(ang)
    lane = lax.broadcasted_iota(jnp.int32, ang.shape, 1) & (period - 1)
    c = jnp.where(lane < rot_dim, cos, 1.0)
    s1 = jnp.where(lane >= half, jnp.where(lane < rot_dim, sin, 0.0), 0.0)
    s2 = jnp.where(lane < half, -sin, 0.0)
    return c, s1, s2


def _rope(x, c, s1, s2, half):
    return x * c + pltpu.roll(x, half, 1) * s1 + pltpu.roll(x, LANES - half, 1) * s2


def _prep_kernel(att_ref, pos_ref, fb_ref, fi_ref,
                 q_ref, k_ref, vt_ref, qi_ref, ki_ref, wit_ref):
    pos = pos_ref[0]
    cb, s1b, s2b = _rope_tables(pos, fb_ref[...], B_ROT, B_HEAD_DIM)
    ci, s1i, s2i = _rope_tables(pos, fi_ref[...], IDX_ROT, IDX_DIM)
    for h in range(B_HEADS):
        x = att_ref[0, :, ATT_Q0 + h * LANES:ATT_Q0 + (h + 1) * LANES]
        q_ref[0, :, h * LANES:(h + 1) * LANES] = _rope(x, cb, s1b, s2b, B_ROT // 2).astype(_BF16)
    for h in range(B_KV_HEADS):
        x = att_ref[0, :, ATT_K0 + h * LANES:ATT_K0 + (h + 1) * LANES]
        k_ref[0, :, h * LANES:(h + 1) * LANES] = _rope(x, cb, s1b, s2b, B_ROT // 2).astype(_BF16)
    vt_ref[0, 0] = att_ref[0, :, ATT_V0:ATT_V0 + B_KV_WIDTH].T.astype(_BF16)
    lane = lax.broadcasted_iota(jnp.int32, ci.shape, 1)
    is_ki = lane < IDX_DIM
    heads_per_group = LANES // IDX_DIM
    for j in range(IDX_WIDTH // LANES):
        x = att_ref[0, :, ATT_QI0 + j * LANES:ATT_QI0 + (j + 1) * LANES]
        y = _rope(x, ci, s1i, s2i, IDX_ROT // 2)
        for r in range(heads_per_group):
            h = j * heads_per_group + r
            yr = y if r == 0 else pltpu.roll(y, LANES - r * IDX_DIM, 1)
            qi_ref[0, :, h * LANES:(h + 1) * LANES] = jnp.where(is_ki, yr, 0.0).astype(_BF16)
    x = att_ref[0, :, ATT_KW0:ATT_KW0 + LANES]
    y = _rope(x, jnp.where(is_ki, ci, 1.0), jnp.where(is_ki, s1i, 0.0),
              jnp.where(is_ki, s2i, 0.0), IDX_ROT // 2)
    ki_ref[0] = jnp.where(is_ki, y, 0.0).astype(_BF16)
    wit_ref[0] = y.T[IDX_DIM:IDX_DIM + IDX_HEADS, :]


def _prep_side_job(att, positions, n_steps):
    bsz, seq, _ = att.shape
    ts = bsz * seq // n_steps
    assert ts * n_steps == bsz * seq and seq % ts == 0 and KEY_CHUNK % ts == 0
    assert ts % (2 * SUBLANES) == 0
    per_seq = seq // ts
    per_chunk = KEY_CHUNK // ts

    def tokens(width):
        return pl.BlockSpec((1, ts, width), lambda s: (s // per_seq, s % per_seq, 0))

    def out(width, dtype):
        return tokens(width), jax.ShapeDtypeStruct((bsz, seq, width), dtype)

    vt_out = (pl.BlockSpec((1, 1, B_KV_WIDTH, ts),
                           lambda s: (s // per_seq, (s % per_seq) // per_chunk, 0,
                                      (s % per_seq) % per_chunk)),
              jax.ShapeDtypeStruct((bsz, seq // KEY_CHUNK, B_KV_WIDTH, KEY_CHUNK), _BF16))
    wit_out = (pl.BlockSpec((1, IDX_HEADS, ts), lambda s: (s // per_seq, 0, s % per_seq)),
               jax.ShapeDtypeStruct((bsz, IDX_HEADS, seq), _F32))
    outs = [out(B_WIDTH, _BF16), out(B_KV_WIDTH, _BF16), vt_out,
            out(IDX_HEADS * LANES, _BF16), out(LANES, _BF16), wit_out]
    freq_spec = pl.BlockSpec((1, LANES), lambda s: (0, 0))
    return SideJob(_prep_kernel,
                   (att, positions.astype(_F32).reshape(bsz, seq, 1),
                    _lane_inv_freq(B_ROT, B_HEAD_DIM), _lane_inv_freq(IDX_ROT, IDX_DIM)),
                   (tokens(ATT_WIDTH), tokens(1), freq_spec, freq_spec),
                   tuple(o[0] for o in outs), tuple(o[1] for o in outs))


def _ordered_int_to_float(key):
    bits = jnp.where(key < 0, key ^ jnp.int32(0x7FFFFFFF), key)
    return lax.bitcast_convert_type(bits, _F32)


def _fold_rows(x, op):
    while x.shape[0] > SUBLANES:
        half = x.shape[0] // 2
        x = op(x[:half], x[half:])
    return x


_DSA_N_IN = 6


def _dsa_kernel(*refs, n_sel, side_body, n_side_in, n_side_out):
    o_at = _DSA_N_IN + n_side_in
    if side_body is not None:
        side_body(*refs[_DSA_N_IN:o_at], *refs[o_at + 1:o_at + 1 + n_side_out])
    own = refs[:_DSA_N_IN] + (refs[o_at],) + refs[o_at + 1 + n_side_out:]
    seq = own[4].shape[1]
    n_chunks = (pl.program_id(1) * Q_BLOCK + Q_BLOCK + KEY_CHUNK - 1) // KEY_CHUNK
    for n in range(1, seq // KEY_CHUNK + 1):
        pl.when(n_chunks == n)(functools.partial(_dsa_block, *own, n_sel=n_sel, n_chunks=n))


def _dsa_block(qi_ref, wit_ref, ki_ref, q_ref, k_ref, vt_ref, o_ref,
               score_ref, bias_ref, logit_ref, acc_ref, *, n_sel, n_chunks):
    t = Q_BLOCK
    blk = pl.program_id(1)
    qpos = blk * t + lax.broadcasted_iota(jnp.int32, (1, t), 1)
    kiota = lax.broadcasted_iota(jnp.int32, (KEY_CHUNK, 1), 0)
    idx_scale = (IDX_DIM ** -0.5) * (IDX_HEADS ** -0.5)

    def rows(c):
        return slice(c * KEY_CHUNK, (c + 1) * KEY_CHUNK)

    def for_chunks(body, carry):
        for c in range(n_chunks):
            carry = body(c, carry)
        return carry

    def indexer_chunk(c, carry):
        ki = ki_ref[0, rows(c), :]
        acc = jnp.zeros((KEY_CHUNK, t), _F32)
        for h in range(0, IDX_HEADS, 2):
            qpair = jnp.concatenate([qi_ref[0, :, h * LANES:(h + 1) * LANES],
                                     qi_ref[0, :, (h + 1) * LANES:(h + 2) * LANES]], axis=0)
            d = lax.dot_general(ki, qpair, _NT_DIMS, preferred_element_type=_F32)
            acc = acc + wit_ref[0, h:h + 1, :] * jnp.maximum(d[:, :t], 0.0)
            acc = acc + wit_ref[0, h + 1:h + 2, :] * jnp.maximum(d[:, t:], 0.0)
        kpos = c * KEY_CHUNK + kiota
        score_ref[rows(c), :] = jnp.where(kpos <= qpos, acc * idx_scale + 0.0, _NEG_INF)
        return carry

    for_chunks(indexer_chunk, 0)

    def count(indicator):
        def chunk(c, part):
            return part + _fold_rows(indicator(score_ref[rows(c), :], c), jnp.add)
        part = for_chunks(chunk, jnp.zeros((SUBLANES, t), _F32))
        return jnp.sum(part, axis=0, keepdims=True)

    def count_ge(cand_f):
        return count(lambda s, c: jnp.where(s >= cand_f, 1.0, 0.0))

    def search():
        int_min = jnp.int32(-2 ** 31)
        base = jnp.where(count_ge(jnp.zeros((1, t), _F32)) >= n_sel, jnp.int32(0), int_min)

        def search_step(it, base):
            cand = base | lax.shift_left(jnp.int32(1), jnp.int32(30) - it)
            cnt = count_ge(_ordered_int_to_float(cand))
            return jnp.where(cnt >= n_sel, cand, base)

        return lax.fori_loop(0, 31, search_step, base)

    if (n_chunks - 1) * KEY_CHUNK >= n_sel:
        base = search()
    else:
        base = lax.cond((blk + 1) * t <= n_sel, lambda: jnp.zeros((1, t), jnp.int32), search)
    thr = jnp.where(qpos + 1 <= n_sel, _NEG_INF, _ordered_int_to_float(base))

    def bias_chunk(c, part):
        s = score_ref[rows(c), :]
        sel = jnp.where(c * KEY_CHUNK + kiota <= qpos, jnp.where(s >= thr, 1.0, 0.0), 0.0)
        bias_ref[rows(c), :] = jnp.where(sel > 0.0, 0.0, _NEG_INF)
        return part + _fold_rows(sel, jnp.add)

    n_ge = jnp.sum(for_chunks(bias_chunk, jnp.zeros((SUBLANES, t), _F32)), axis=0, keepdims=True)

    @pl.when(jnp.max(n_ge) > n_sel)
    def _():
        nxt = _ordered_int_to_float(base + 1)
        tied = n_ge > n_sel
        n_above = count(lambda s, c: jnp.where(s >= nxt, 1.0, 0.0))
        need = jnp.where(tied, n_sel - n_above, 0.0)
        no_index = float(k_ref.shape[1])
        front_v = jnp.where(tied, jnp.inf, _NEG_INF)
        front_i = jnp.where(tied, -1.0, no_index)

        def beyond(s, kposf, front_v, front_i):
            after = jnp.where(s < front_v, 1.0,
                              jnp.where(s == front_v, jnp.where(kposf > front_i, 1.0, 0.0), 0.0))
            return jnp.where(s >= thr, jnp.where(s < nxt, after, 0.0), 0.0)

        def kposf(c):
            return (c * KEY_CHUNK + kiota).astype(_F32)

        def advance(_, carry):
            front_v, front_i, need = carry

            def best_value(c, part):
                s = score_ref[rows(c), :]
                cand = jnp.where(beyond(s, kposf(c), front_v, front_i) > 0.0, s, _NEG_INF)
                return jnp.maximum(part, _fold_rows(cand, jnp.maximum))

            v = jnp.max(for_chunks(best_value, jnp.full((SUBLANES, t), _NEG_INF, _F32)),
                        axis=0, keepdims=True)

            def first_index(c, part):
                s = score_ref[rows(c), :]
                hit = jnp.where(s == v, beyond(s, kposf(c), front_v, front_i), 0.0)
                cand = jnp.where(hit > 0.0, kposf(c), no_index)
                return jnp.minimum(part, _fold_rows(cand, jnp.minimum))

            i = jnp.min(for_chunks(first_index, jnp.full((SUBLANES, t), no_index, _F32)),
                        axis=0, keepdims=True)
            active = need > 0.0
            return (jnp.where(active, v, front_v), jnp.where(active, i, front_i),
                    jnp.where(active, need - 1.0, need))

        front_v, front_i, _ = lax.fori_loop(0, jnp.max(need).astype(jnp.int32), advance,
                                            (front_v, front_i, need))

        def tie_chunk(c, carry):
            s = score_ref[rows(c), :]
            upto = jnp.where(s > front_v, 0.0,
                             jnp.where(s == front_v,
                                       jnp.where(kposf(c) <= front_i, 0.0, _NEG_INF), _NEG_INF))
            keep = jnp.where(s >= nxt, 0.0, jnp.where(s >= thr, upto, _NEG_INF))
            bias_ref[rows(c), :] = jnp.where(c * KEY_CHUNK + kiota <= qpos, keep, _NEG_INF)
            return carry

        for_chunks(tie_chunk, 0)

    grp = B_HEADS // B_KV_HEADS
    gw = grp * t
    exp2_scale = (B_HEAD_DIM ** -0.5) * 1.4426950408889634

    def logit_chunk(c, mx):
        bias = jnp.concatenate([bias_ref[rows(c), :]] * grp, axis=1)
        folded = []
        for g in range(B_KV_HEADS):
            qg = jnp.concatenate(
                [q_ref[0, :, (g * grp + hh) * B_HEAD_DIM:(g * grp + hh + 1) * B_HEAD_DIM]
                 for hh in range(grp)], axis=0)
            l = lax.dot_general(k_ref[0, rows(c), g * B_HEAD_DIM:(g + 1) * B_HEAD_DIM], qg,
                                _NT_DIMS, preferred_element_type=_F32) + bias
            logit_ref[rows(c), g * gw:(g + 1) * gw] = l
            folded.append(_fold_rows(l, jnp.maximum))
        return jnp.maximum(mx, jnp.concatenate(folded, axis=1))

    mx = for_chunks(logit_chunk, jnp.full((SUBLANES, B_HEADS * t), _NEG_INF, _F32))
    mx = jnp.max(mx, axis=0, keepdims=True)

    acc_ref[...] = jnp.zeros_like(acc_ref)
    ones_rows = jnp.ones((acc_ref.shape[0] - B_HEAD_DIM, KEY_CHUNK), _BF16)

    def pv_chunk(c, carry):
        for g in range(B_KV_HEADS):
            cols = slice(g * gw, (g + 1) * gw)
            p = jnp.exp2((logit_ref[rows(c), cols] - mx[:, cols]) * exp2_scale)
            vt_ext = jnp.concatenate(
                [vt_ref[0, c, g * B_HEAD_DIM:(g + 1) * B_HEAD_DIM, :], ones_rows], axis=0)
            acc_ref[:, cols] += jnp.dot(vt_ext, p.astype(_BF16), preferred_element_type=_F32)
        return carry

    for_chunks(pv_chunk, 0)
    out_t = acc_ref[:B_HEAD_DIM, :] / acc_ref[B_HEAD_DIM:B_HEAD_DIM + 1, :]
    for h in range(B_HEADS):
        o_ref[0, :, h * B_HEAD_DIM:(h + 1) * B_HEAD_DIM] = (
            out_t[:, h * t:(h + 1) * t].T.astype(o_ref.dtype))


def _dsa_attention(q, k, vt, qi, ki, wit, side_job=None):
    bsz, seq, _ = q.shape
    assert Q_BLOCK == LANES and seq % KEY_CHUNK == 0
    n_sel = min(TOPK_MAX, seq // 4)
    n_blk = seq // Q_BLOCK

    def q_spec(width):
        return pl.BlockSpec((1, Q_BLOCK, width), lambda b, i: (b, i, 0))

    def kv_spec(width):
        return pl.BlockSpec((1, seq, width), lambda b, i: (b, 0, 0))

    def per_step(spec):
        return pl.BlockSpec(spec.block_shape, lambda b, i: spec.index_map(b * n_blk + i))

    side = None if side_job is None else side_job(bsz * n_blk)
    side_in = [] if side is None else [per_step(s) for s in side.in_specs]
    side_out = [] if side is None else [per_step(s) for s in side.out_specs]
    outs = pl.pallas_call(
        functools.partial(_dsa_kernel, n_sel=n_sel,
                          side_body=None if side is None else side.body,
                          n_side_in=len(side_in), n_side_out=len(side_out)),
        grid=(bsz, n_blk),
        in_specs=[q_spec(IDX_HEADS * LANES),
                  pl.BlockSpec((1, IDX_HEADS, Q_BLOCK), lambda b, i: (b, 0, i)),
                  kv_spec(LANES), q_spec(B_WIDTH), kv_spec(B_KV_WIDTH),
                  pl.BlockSpec((1, seq // KEY_CHUNK, B_KV_WIDTH, KEY_CHUNK),
                               lambda b, i: (b, 0, 0, 0))] + side_in,
        out_specs=[q_spec(B_WIDTH)] + side_out,
        out_shape=[jax.ShapeDtypeStruct((bsz, seq, B_WIDTH), _BF16)]
        + ([] if side is None else list(side.out_shape)),
        scratch_shapes=[pltpu.VMEM((seq, Q_BLOCK), _F32), pltpu.VMEM((seq, Q_BLOCK), _F32),
                        pltpu.VMEM((seq, B_HEADS * Q_BLOCK), _F32),
                        pltpu.VMEM((B_HEAD_DIM + 2 * SUBLANES, B_HEADS * Q_BLOCK), _F32)],
        compiler_params=_compiler_params(2),
        name="dsa_attention",
    )(qi, wit, ki, q, k, vt, *(() if side is None else side.arrays))
    return outs[0] if side is None else tuple(outs)


def _xattn_kernel(q_ref, k_ref, v_ref, o_ref):
    scale = X_HEAD_DIM ** -0.5
    for h in range(X_HEADS):
        hs = slice(h * X_HEAD_DIM, (h + 1) * X_HEAD_DIM)
        logits = lax.dot_general(q_ref[0, :, hs], k_ref[0, :, hs], _NT_DIMS,
                                 preferred_element_type=_F32) * scale
        mx = jnp.max(logits, axis=-1, keepdims=True)
        p = jnp.exp(logits - mx)
        denom = jnp.sum(p, axis=-1, keepdims=True)
        o = jnp.dot(p.astype(_BF16), v_ref[0, :, hs], preferred_element_type=_F32)
        o_ref[0, :, hs] = (o / denom).astype(o_ref.dtype)


def _cross_attention(q, k, v):
    bsz, seq, _ = q.shape
    tq = ROW_BLOCK["cross_attention"]
    m = k.shape[1]
    mem_spec = pl.BlockSpec((1, m, X_WIDTH), lambda b, i: (b, 0, 0))
    return pl.pallas_call(
        _xattn_kernel,
        grid=(bsz, seq // tq),
        in_specs=[pl.BlockSpec((1, tq, X_WIDTH), lambda b, i: (b, i, 0)), mem_spec, mem_spec],
        out_specs=pl.BlockSpec((1, tq, X_WIDTH), lambda b, i: (b, i, 0)),
        out_shape=jax.ShapeDtypeStruct((bsz, seq, X_WIDTH), _BF16),
        compiler_params=_compiler_params(2),
        name="cross_attention",
    )(q, k, v)


def _norm_proj_kernel(x_ref, g_ref, w_ref, o_ref):
    x = x_ref[...]
    ms = jnp.mean(x * x, axis=-1, keepdims=True)
    h = ((x * lax.rsqrt(ms + EPS)) * g_ref[...]).astype(_BF16)
    o_ref[...] = jnp.dot(h, w_ref[...], preferred_element_type=_F32).astype(o_ref.dtype)


def _norm_proj(x, g, w_bf16, out_dtype, name):
    m, d = x.shape
    tm = ROW_BLOCK["norm_proj"]
    n = w_bf16.shape[1]
    return pl.pallas_call(
        _norm_proj_kernel,
        grid=(m // tm,),
        in_specs=[pl.BlockSpec((tm, d), lambda i: (i, 0)),
                  pl.BlockSpec((1, d), lambda i: (0, 0)),
                  pl.BlockSpec((d, n), lambda i: (0, 0), pipeline_mode=pl.Buffered(1))],
        out_specs=pl.BlockSpec((tm, n), lambda i: (i, 0)),
        out_shape=jax.ShapeDtypeStruct((m, n), out_dtype),
        compiler_params=_compiler_params(1),
        name=name,
    )(x, g.reshape(1, d), w_bf16)


def _xattn_out_kernel(o_ref, w_ref, x_ref, g_ref, x_out_ref, h_out_ref):
    x = x_ref[...] + jnp.dot(o_ref[...], w_ref[...], preferred_element_type=_F32)
    x_out_ref[...] = x
    ms = jnp.mean(x * x, axis=-1, keepdims=True)
    h_out_ref[...] = ((x * lax.rsqrt(ms + EPS)) * g_ref[...]).astype(h_out_ref.dtype)


def _xattn_out_and_norm(o, w_bf16, x, g):
    m, k = o.shape
    tm = ROW_BLOCK["xattn_out_norm"]
    d = x.shape[1]
    row_spec = pl.BlockSpec((tm, d), lambda i: (i, 0))
    return pl.pallas_call(
        _xattn_out_kernel,
        grid=(m // tm,),
        in_specs=[pl.BlockSpec((tm, k), lambda i: (i, 0)),
                  pl.BlockSpec((k, d), lambda i: (0, 0), pipeline_mode=pl.Buffered(1)),
                  row_spec,
                  pl.BlockSpec((1, d), lambda i: (0, 0))],
        out_specs=[row_spec, row_spec],
        out_shape=[jax.ShapeDtypeStruct((m, d), _F32), jax.ShapeDtypeStruct((m, d), _BF16)],
        compiler_params=_compiler_params(1),
        name="xattn_out_norm",
    )(o, w_bf16, x, g.reshape(1, d))


def kernel(x, mem, positions, norm_mix_g, w_in, a_norm_g, a_spatial_w, a_spatial_b, p_a, p_b,
           w_out, norm_x_g, norm_mem_g, xq_w, xk_w, xv_w, xo_w, norm_ffn_g, ffn_w1, ffn_w3,
           ffn_w2, final_norm_g):
    bsz, seq, d = x.shape
    m = bsz * seq
    depth = w_in.shape[0]
    ffn_hidden = ffn_w1.shape[-1]

    xf = x.reshape(m, d)
    mem_f = mem.reshape(bsz * MEM_LEN, d)
    w_in_t = jnp.transpose(w_in, (0, 2, 1))
    for l in range(depth):
        def w_in_cols(col0):
            return Rhs(w_in_t, l, col0, transposed=True)

        h = _rmsnorm(xf, norm_mix_g[l], _BF16)
        att = _fused_matmul("proj_att", [h], [(0, w_in_cols(ZA_END))], [], _ep_identity,
                            ATT_WIDTH, _F32)
        za, q, k, vt, qi, ki, wit = _fused_matmul(
            "proj_za", [h], [(0, w_in_cols(0))], [], _ep_identity, ZA_END, _F32,
            side_job=functools.partial(_prep_side_job, att.reshape(bsz, seq, ATT_WIDTH),
                                       positions))
        y_a = _spatial_gating(za, a_norm_g[l], a_spatial_w[l], a_spatial_b[l])
        def weight_casts(n_steps):
            return _merge_side_jobs(
                _cast_side_job(p_a, l, n_steps), _cast_side_job(p_b, l, n_steps),
                _cast_side_job(w_in_t, l, n_steps, row0=ATT_END, rows=2 * d),
                _cast_side_job(w_out, l, n_steps), _cast_side_job(xq_w, l, n_steps),
                _cast_side_job(xk_w, l, n_steps), _cast_side_job(xv_w, l, n_steps))

        y_b, pa_bf16, pb_bf16, gate_wt_bf16, wo_bf16, xq_bf16, xk_bf16, xv_bf16 = _dsa_attention(
            q, k, vt, qi, ki, wit, side_job=weight_casts)

        def ffn_up_weight_casts(n_steps):
            return _merge_side_jobs(_cast_side_job(ffn_w1, l, n_steps),
                                    _cast_side_job(ffn_w3, l, n_steps))

        merged, w1_bf16, w3_bf16 = _fused_matmul(
            "gated_merge", [y_a, y_b.reshape(m, B_WIDTH), h],
            [(0, Rhs(pa_bf16)), (1, Rhs(pb_bf16)), (2, Rhs(gate_wt_bf16, transposed=True)),
             (2, Rhs(gate_wt_bf16, col0=d, transposed=True))],
            [], _ep_gated_merge, d, _BF16, side_job=ffn_up_weight_casts)
        x1, xo_bf16 = _fused_matmul("out_proj", [merged], [(0, Rhs(wo_bf16))], [xf],
                                    _ep_residual, d, _F32,
                                    side_job=functools.partial(_cast_side_job, xo_w, l))

        qx = _norm_proj(x1, norm_x_g[l], xq_bf16, _BF16, "xattn_q")
        kx = _norm_proj(mem_f, norm_mem_g[l], xk_bf16, _BF16, "xattn_k")
        vx = _norm_proj(mem_f, norm_mem_g[l], xv_bf16, _BF16, "xattn_v")
        ox = _cross_attention(qx.reshape(bsz, seq, X_WIDTH),
                              kx.reshape(bsz, MEM_LEN, X_WIDTH),
                              vx.reshape(bsz, MEM_LEN, X_WIDTH)).reshape(m, X_WIDTH)
        x2, h2 = _xattn_out_and_norm(ox, xo_bf16, x1, norm_ffn_g[l])

        act, w2_bf16 = _fused_matmul("ffn_up", [h2],
                                     [(0, Rhs(w1_bf16)), (0, Rhs(w3_bf16))],
                                     [], _ep_swiglu, ffn_hidden, _BF16,
                                     side_job=functools.partial(_cast_side_job, ffn_w2, l))
        xf = _fused_matmul("ffn_down", [act], [(0, Rhs(w2_bf16))], [x2], _ep_residual, d, _F32)

    return _rmsnorm(xf, final_norm_g, _F32).reshape(bsz, seq, d)
```

```python
import functools
from typing import Callable, NamedTuple, Optional

import jax
import jax.numpy as jnp
from jax import lax
from jax.experimental import pallas as pl
from jax.experimental.pallas import tpu as pltpu

D_MODEL = 4096
MEM_LEN = 256
EPS = 1e-6
ROPE_THETA = 500000.0
CHUNK = 128
A_GROUPS = 16
A_WIDTH = D_MODEL // 2
A_GROUP_DIM = A_WIDTH // A_GROUPS
B_HEADS = 16
B_HEAD_DIM = 128
B_KV_HEADS = 4
B_WIDTH = B_HEADS * B_HEAD_DIM
B_KV_WIDTH = B_KV_HEADS * B_HEAD_DIM
B_ROT = B_HEAD_DIM // 4
IDX_HEADS = 16
IDX_DIM = 64
IDX_WIDTH = IDX_HEADS * IDX_DIM
IDX_ROT = IDX_DIM // 4
TOPK_MAX = 256
Q_BLOCK = 128
X_HEADS = 4
X_HEAD_DIM = 256
X_WIDTH = X_HEADS * X_HEAD_DIM
IN_SIZES = (2 * A_WIDTH, B_WIDTH, B_KV_WIDTH, B_KV_WIDTH, IDX_WIDTH, IDX_DIM, IDX_HEADS,
            2 * D_MODEL)

LANES = 128
SUBLANES = 8
KEY_CHUNK = 512
PROJ_TN = 512
ZA_END = IN_SIZES[0]
ATT_END = ZA_END + sum(IN_SIZES[1:7])
ATT_Q0 = 0
ATT_K0 = ATT_Q0 + B_WIDTH
ATT_V0 = ATT_K0 + B_KV_WIDTH
ATT_QI0 = ATT_V0 + B_KV_WIDTH
ATT_KW0 = ATT_QI0 + IDX_WIDTH
ATT_WIDTH = -(-(ATT_END - ZA_END) // PROJ_TN) * PROJ_TN
assert ZA_END % PROJ_TN == 0 and ATT_KW0 + LANES <= ATT_WIDTH

VMEM_LIMIT_BYTES = 60 * 1024 * 1024


class Tile(NamedTuple):
    tm: int
    tn: int
    row_splits: int = 1


TILES = {
    "proj_att": Tile(1024, PROJ_TN),
    "proj_za": Tile(1024, 512),
    "gated_merge": Tile(1024, 256, row_splits=4),
    "out_proj": Tile(1024, 512),
    "ffn_up": Tile(2048, 256, row_splits=4),
    "ffn_down": Tile(512, 512),
}
ROW_BLOCK = {
    "rmsnorm": 512,
    "spatial_gating": 4 * CHUNK,
    "cross_attention": 1024,
    "norm_proj": 512,
    "xattn_out_norm": 512,
}

_BF16 = jnp.bfloat16
_F32 = jnp.float32
_NEG_INF = float("-inf")
_NN_DIMS = (((1,), (0,)), ((), ()))
_NT_DIMS = (((1,), (1,)), ((), ()))


def _compiler_params(n_axes):
    return pltpu.CompilerParams(
        dimension_semantics=("arbitrary",) * n_axes,
        vmem_limit_bytes=VMEM_LIMIT_BYTES,
    )


def _rmsnorm_kernel(x_ref, g_ref, o_ref):
    x = x_ref[...]
    ms = jnp.mean(x * x, axis=-1, keepdims=True)
    y = x * lax.rsqrt(ms + EPS)
    o_ref[...] = (y * g_ref[...]).astype(o_ref.dtype)


def _rmsnorm(x, g, out_dtype):
    m, d = x.shape
    tm = ROW_BLOCK["rmsnorm"]
    return pl.pallas_call(
        _rmsnorm_kernel,
        grid=(m // tm,),
        in_specs=[pl.BlockSpec((tm, d), lambda i: (i, 0)),
                  pl.BlockSpec((1, d), lambda i: (0, 0))],
        out_specs=pl.BlockSpec((tm, d), lambda i: (i, 0)),
        out_shape=jax.ShapeDtypeStruct((m, d), out_dtype),
        compiler_params=_compiler_params(1),
        name="rmsnorm",
    )(x, g.reshape(1, d))


def _ep_identity(accs, extras):
    return accs[0]


def _ep_residual(accs, extras):
    return extras[0] + accs[0]


def _ep_gated_merge(accs, extras):
    y_a, y_b, gate_a, gate_b = accs
    return jax.nn.sigmoid(gate_a) * y_a + jax.nn.sigmoid(gate_b) * y_b


def _ep_swiglu(accs, extras):
    return jax.nn.silu(accs[0]) * accs[1]


class Rhs(NamedTuple):
    array: jax.Array
    layer: Optional[int] = None
    col0: int = 0
    transposed: bool = False


def _rhs_spec(r, tn):
    if r.transposed and r.layer is None:
        assert r.col0 % tn == 0
        off = r.col0 // tn
        return pl.BlockSpec((tn, r.array.shape[1]), lambda i, j: (j + off, 0))
    if r.transposed:
        k = r.array.shape[2]
        if r.col0 % tn == 0:
            return pl.BlockSpec((None, tn, k), lambda i, j: (r.layer, j + r.col0 // tn, 0))
        assert r.col0 % SUBLANES == 0
        return pl.BlockSpec((pl.Element(1), pl.Element(tn), pl.Element(k)),
                            lambda i, j: (r.layer, (r.col0 // SUBLANES + j * (tn // SUBLANES))
                                          * SUBLANES, 0))
    assert r.col0 % tn == 0
    off = r.col0 // tn
    if r.layer is None:
        return pl.BlockSpec((r.array.shape[0], tn), lambda i, j: (0, j + off))
    return pl.BlockSpec((None, r.array.shape[1], tn), lambda i, j: (r.layer, 0, j + off))


class SideJob(NamedTuple):
    body: Callable
    arrays: tuple
    in_specs: tuple
    out_specs: tuple
    out_shape: tuple


def _cast_body(x_ref, o_ref):
    x = x_ref[0] if len(x_ref.shape) == 3 else x_ref[...]
    o_ref[...] = x.astype(o_ref.dtype)


def _cast_side_job(w_stack, layer, n_steps, row0=0, rows=None):
    rows = w_stack.shape[1] if rows is None else rows
    cols = w_stack.shape[2]
    slab = rows // n_steps
    assert slab * n_steps == rows and slab % (2 * SUBLANES) == 0
    if row0 % slab == 0:
        in_spec = pl.BlockSpec((None, slab, cols), lambda s: (layer, s + row0 // slab, 0))
    else:
        assert row0 % SUBLANES == 0
        in_spec = pl.BlockSpec(
            (pl.Element(1), pl.Element(slab), pl.Element(cols)),
            lambda s: (layer, (row0 // SUBLANES + s * (slab // SUBLANES)) * SUBLANES, 0))
    return SideJob(_cast_body, (w_stack,), (in_spec,),
                   (pl.BlockSpec((slab, cols), lambda s: (s, 0)),),
                   (jax.ShapeDtypeStruct((rows, cols), _BF16),))


def _merge_side_jobs(*jobs):
    n_in = [len(j.in_specs) for j in jobs]
    n_out = [len(j.out_specs) for j in jobs]

    def body(*refs):
        ins, outs = refs[:sum(n_in)], refs[sum(n_in):]
        i = o = 0
        for j, ni, no in zip(jobs, n_in, n_out):
            j.body(*ins[i:i + ni], *outs[o:o + no])
            i, o = i + ni, o + no

    return SideJob(body, sum((j.arrays for j in jobs), ()), sum((j.in_specs for j in jobs), ()),
                   sum((j.out_specs for j in jobs), ()), sum((j.out_shape for j in jobs), ()))


def _mm_kernel(*refs, n_lhs, pair_lhs, rhs_transposed, n_extra, epilogue, side_body, n_side_in,
               row_splits):
    n_rhs = len(pair_lhs)
    lhs_refs = refs[:n_lhs]
    rhs_refs = refs[n_lhs:n_lhs + n_rhs]
    n_in = n_lhs + n_rhs + n_extra
    extra_refs = refs[n_lhs + n_rhs:n_in]
    o_ref = refs[n_in + n_side_in]
    if side_body is not None:
        side_body(*refs[n_in:n_in + n_side_in], *refs[n_in + n_side_in + 1:])
    weights = [(r[0] if len(r.shape) == 3 else r[...]).astype(_BF16) for r in rhs_refs]
    rows_per_split = o_ref.shape[0] // row_splits
    for s in range(row_splits):
        rows = slice(s * rows_per_split, (s + 1) * rows_per_split)
        accs = [lax.dot_general(lhs_refs[li][rows, :], w, _NT_DIMS if tr else _NN_DIMS,
                                preferred_element_type=_F32)
                for li, w, tr in zip(pair_lhs, weights, rhs_transposed)]
        o_ref[rows, :] = epilogue(accs, [e[rows, :] for e in extra_refs]).astype(o_ref.dtype)


def _fused_matmul(name, lhs, pairs, extras, epilogue, n_out, out_dtype, side_job=None):
    tm, tn, row_splits = TILES[name]
    m = lhs[0].shape[0]
    assert m % tm == 0 and n_out % tn == 0
    grid = (m // tm, n_out // tn)
    in_specs = [pl.BlockSpec((tm, a.shape[1]), lambda i, j: (i, 0)) for a in lhs]
    in_specs += [_rhs_spec(r, tn) for _, r in pairs]
    in_specs += [pl.BlockSpec((tm, tn), lambda i, j: (i, j)) for _ in extras]
    out_specs = [pl.BlockSpec((tm, tn), lambda i, j: (i, j))]
    out_shape = [jax.ShapeDtypeStruct((m, n_out), out_dtype)]
    operands = list(lhs) + [p[1].array for p in pairs] + list(extras)
    side = None if side_job is None else side_job(grid[0] * grid[1])
    if side is not None:
        def per_step(spec):
            return pl.BlockSpec(spec.block_shape,
                                lambda i, j: spec.index_map(i * grid[1] + j))
        in_specs += [per_step(s) for s in side.in_specs]
        out_specs += [per_step(s) for s in side.out_specs]
        out_shape += list(side.out_shape)
        operands += list(side.arrays)
    kern = functools.partial(_mm_kernel, n_lhs=len(lhs), pair_lhs=tuple(p[0] for p in pairs),
                             rhs_transposed=tuple(p[1].transposed for p in pairs),
                             n_extra=len(extras), epilogue=epilogue,
                             side_body=None if side is None else side.body,
                             n_side_in=0 if side is None else len(side.in_specs),
                             row_splits=row_splits)
    outs = pl.pallas_call(
        kern,
        grid=grid,
        in_specs=in_specs,
        out_specs=out_specs,
        out_shape=out_shape,
        compiler_params=_compiler_params(2),
        name=name,
    )(*operands)
    return outs[0] if side is None else tuple(outs)


def _gating_kernel(za_ref, ng_ref, w_ref, bt_ref, o_ref):
    row = lax.broadcasted_iota(jnp.int32, (CHUNK, CHUNK), 0)
    col = lax.broadcasted_iota(jnp.int32, (CHUNK, CHUNK), 1)
    causal = col <= row
    for c in range(za_ref.shape[0] // CHUNK):
        rows = slice(c * CHUNK, (c + 1) * CHUNK)
        z = jax.nn.gelu(za_ref[rows, :])
        u = z[:, :A_WIDTH]
        v = z[:, A_WIDTH:]
        ms = jnp.mean(v * v, axis=-1, keepdims=True)
        vn = ((v * lax.rsqrt(ms + EPS)) * ng_ref[...]).astype(_BF16)
        for g in range(A_GROUPS):
            sl = slice(g * A_GROUP_DIM, (g + 1) * A_GROUP_DIM)
            w = jnp.where(causal, w_ref[g], 0.0).astype(_BF16)
            s = jnp.dot(w, vn[:, sl], preferred_element_type=_F32) + bt_ref[:, g:g + 1]
            o_ref[rows, sl] = (u[:, sl] * s).astype(o_ref.dtype)


def _spatial_gating(za, norm_g, w_s, b_s):
    m = za.shape[0]
    tm = ROW_BLOCK["spatial_gating"]
    return pl.pallas_call(
        _gating_kernel,
        grid=(m // tm,),
        in_specs=[pl.BlockSpec((tm, 2 * A_WIDTH), lambda i: (i, 0)),
                  pl.BlockSpec((1, A_WIDTH), lambda i: (0, 0)),
                  pl.BlockSpec((A_GROUPS, CHUNK, CHUNK), lambda i: (0, 0, 0)),
                  pl.BlockSpec((CHUNK, A_GROUPS), lambda i: (0, 0))],
        out_specs=pl.BlockSpec((tm, A_WIDTH), lambda i: (i, 0)),
        out_shape=jax.ShapeDtypeStruct((m, A_WIDTH), _BF16),
        compiler_params=_compiler_params(1),
        name="spatial_gating",
    )(za, norm_g.reshape(1, A_WIDTH), w_s, b_s.T)


def _lane_inv_freq(rot_dim, period):
    inv_freq = ROPE_THETA ** (-jnp.arange(0, rot_dim, 2, dtype=_F32) / rot_dim)
    head = jnp.concatenate([inv_freq, inv_freq, jnp.zeros((period - rot_dim,), _F32)])
    return jnp.tile(head, LANES // period).reshape(1, LANES)


def _rope_tables(pos, inv_freq_lanes, rot_dim, period):
    half = rot_dim // 2
    ang = pos * inv_freq_lanes
    cos, sin = jnp.cos(ang), jnp.sin(ang)
    lane = lax.broadcasted_iota(jnp.int32, ang.shape, 1) & (period - 1)
    c = jnp.where(lane < rot_dim, cos, 1.0)
    s1 = jnp.where(lane >= half, jnp.where(lane < rot_dim, sin, 0.0), 0.0)
    s2 = jnp.where(lane < half, -sin, 0.0)
    return c, s1, s2


def _rope(x, c, s1, s2, half):
    return x * c + pltpu.roll(x, half, 1) * s1 + pltpu.roll(x, LANES - half, 1) * s2


def _prep_kernel(att_ref, pos_ref, fb_ref, fi_ref,
                 q_ref, k_ref, vt_ref, qi_ref, ki_ref, wit_ref):
    pos = pos_ref[0]
    cb, s1b, s2b = _rope_tables(pos, fb_ref[...], B_ROT, B_HEAD_DIM)
    ci, s1i, s2i = _rope_tables(pos, fi_ref[...], IDX_ROT, IDX_DIM)
    for h in range(B_HEADS):
        x = att_ref[0, :, ATT_Q0 + h * LANES:ATT_Q0 + (h + 1) * LANES]
        q_ref[0, :, h * LANES:(h + 1) * LANES] = _rope(x, cb, s1b, s2b, B_ROT // 2).astype(_BF16)
    for h in range(B_KV_HEADS):
        x = att_ref[0, :, ATT_K0 + h * LANES:ATT_K0 + (h + 1) * LANES]
        k_ref[0, :, h * LANES:(h + 1) * LANES] = _rope(x, cb, s1b, s2b, B_ROT // 2).astype(_BF16)
    vt_ref[0, 0] = att_ref[0, :, ATT_V0:ATT_V0 + B_KV_WIDTH].T.astype(_BF16)
    lane = lax.broadcasted_iota(jnp.int32, ci.shape, 1)
    is_ki = lane < IDX_DIM
    heads_per_group = LANES // IDX_DIM
    for j in range(IDX_WIDTH // LANES):
        x = att_ref[0, :, ATT_QI0 + j * LANES:ATT_QI0 + (j + 1) * LANES]
        y = _rope(x, ci, s1i, s2i, IDX_ROT // 2)
        for r in range(heads_per_group):
            h = j * heads_per_group + r
            yr = y if r == 0 else pltpu.roll(y, LANES - r * IDX_DIM, 1)
            qi_ref[0, :, h * LANES:(h + 1) * LANES] = jnp.where(is_ki, yr, 0.0).astype(_BF16)
    x = att_ref[0, :, ATT_KW0:ATT_KW0 + LANES]
    y = _rope(x, jnp.where(is_ki, ci, 1.0), jnp.where(is_ki, s1i, 0.0),
              jnp.where(is_ki, s2i, 0.0), IDX_ROT // 2)
    ki_ref[0] = jnp.where(is_ki, y, 0.0).astype(_BF16)
    wit_ref[0] = y.T[IDX_DIM:IDX_DIM + IDX_HEADS, :]


def _prep_side_job(att, positions, n_steps):
    bsz, seq, _ = att.shape
    ts = bsz * seq // n_steps
    assert ts * n_steps == bsz * seq and seq % ts == 0 and KEY_CHUNK % ts == 0
    assert ts % (2 * SUBLANES) == 0
    per_seq = seq // ts
    per_chunk = KEY_CHUNK // ts

    def tokens(width):
        return pl.BlockSpec((1, ts, width), lambda s: (s // per_seq, s % per_seq, 0))

    def out(width, dtype):
        return tokens(width), jax.ShapeDtypeStruct((bsz, seq, width), dtype)

    vt_out = (pl.BlockSpec((1, 1, B_KV_WIDTH, ts),
                           lambda s: (s // per_seq, (s % per_seq) // per_chunk, 0,
                                      (s % per_seq) % per_chunk)),
              jax.ShapeDtypeStruct((bsz, seq // KEY_CHUNK, B_KV_WIDTH, KEY_CHUNK), _BF16))
    wit_out = (pl.BlockSpec((1, IDX_HEADS, ts), lambda s: (s // per_seq, 0, s % per_seq)),
               jax.ShapeDtypeStruct((bsz, IDX_HEADS, seq), _F32))
    outs = [out(B_WIDTH, _BF16), out(B_KV_WIDTH, _BF16), vt_out,
            out(IDX_HEADS * LANES, _BF16), out(LANES, _BF16), wit_out]
    freq_spec = pl.BlockSpec((1, LANES), lambda s: (0, 0))
    return SideJob(_prep_kernel,
                   (att, positions.astype(_F32).reshape(bsz, seq, 1),
                    _lane_inv_freq(B_ROT, B_HEAD_DIM), _lane_inv_freq(IDX_ROT, IDX_DIM)),
                   (tokens(ATT_WIDTH), tokens(1), freq_spec, freq_spec),
                   tuple(o[0] for o in outs), tuple(o[1] for o in outs))


def _ordered_int_to_float(key):
    bits = jnp.where(key < 0, key ^ jnp.int32(0x7FFFFFFF), key)
    return lax.bitcast_convert_type(bits, _F32)


def _fold_rows(x, op):
    while x.shape[0] > SUBLANES:
        half = x.shape[0] // 2
        x = op(x[:half], x[half:])
    return x


_DSA_N_IN = 6


def _dsa_kernel(*refs, n_sel, side_body, n_side_in, n_side_out):
    o_at = _DSA_N_IN + n_side_in
    if side_body is not None:
        side_body(*refs[_DSA_N_IN:o_at], *refs[o_at + 1:o_at + 1 + n_side_out])
    own = refs[:_DSA_N_IN] + (refs[o_at],) + refs[o_at + 1 + n_side_out:]
    seq = own[4].shape[1]
    n_chunks = (pl.program_id(1) * Q_BLOCK + Q_BLOCK + KEY_CHUNK - 1) // KEY_CHUNK
    for n in range(1, seq // KEY_CHUNK + 1):
        pl.when(n_chunks == n)(functools.partial(_dsa_block, *own, n_sel=n_sel, n_chunks=n))


def _dsa_block(qi_ref, wit_ref, ki_ref, q_ref, k_ref, vt_ref, o_ref,
               score_ref, bias_ref, logit_ref, acc_ref, *, n_sel, n_chunks):
    t = Q_BLOCK
    blk = pl.program_id(1)
    qpos = blk * t + lax.broadcasted_iota(jnp.int32, (1, t), 1)
    kiota = lax.broadcasted_iota(jnp.int32, (KEY_CHUNK, 1), 0)
    idx_scale = (IDX_DIM ** -0.5) * (IDX_HEADS ** -0.5)

    def rows(c):
        return slice(c * KEY_CHUNK, (c + 1) * KEY_CHUNK)

    def for_chunks(body, carry):
        for c in range(n_chunks):
            carry = body(c, carry)
        return carry

    def indexer_chunk(c, carry):
        ki = ki_ref[0, rows(c), :]
        acc = jnp.zeros((KEY_CHUNK, t), _F32)
        for h in range(0, IDX_HEADS, 2):
            qpair = jnp.concatenate([qi_ref[0, :, h * LANES:(h + 1) * LANES],
                                     qi_ref[0, :, (h + 1) * LANES:(h + 2) * LANES]], axis=0)
            d = lax.dot_general(ki, qpair, _NT_DIMS, preferred_element_type=_F32)
            acc = acc + wit_ref[0, h:h + 1, :] * jnp.maximum(d[:, :t], 0.0)
            acc = acc + wit_ref[0, h + 1:h + 2, :] * jnp.maximum(d[:, t:], 0.0)
        kpos = c * KEY_CHUNK + kiota
        score_ref[rows(c), :] = jnp.where(kpos <= qpos, acc * idx_scale + 0.0, _NEG_INF)
        return carry

    for_chunks(indexer_chunk, 0)

    def count(indicator):
        def chunk(c, part):
            return part + _fold_rows(indicator(score_ref[rows(c), :], c), jnp.add)
        part = for_chunks(chunk, jnp.zeros((SUBLANES, t), _F32))
        return jnp.sum(part, axis=0, keepdims=True)

    def count_ge(cand_f):
        return count(lambda s, c: jnp.where(s >= cand_f, 1.0, 0.0))

    def search():
        int_min = jnp.int32(-2 ** 31)
        base = jnp.where(count_ge(jnp.zeros((1, t), _F32)) >= n_sel, jnp.int32(0), int_min)

        def search_step(it, base):
            cand = base | lax.shift_left(jnp.int32(1), jnp.int32(30) - it)
            cnt = count_ge(_ordered_int_to_float(cand))
            return jnp.where(cnt >= n_sel, cand, base)

        return lax.fori_loop(0, 31, search_step, base)

    if (n_chunks - 1) * KEY_CHUNK >= n_sel:
        base = search()
    else:
        base = lax.cond((blk + 1) * t <= n_sel, lambda: jnp.zeros((1, t), jnp.int32), search)
    thr = jnp.where(qpos + 1 <= n_sel, _NEG_INF, _ordered_int_to_float(base))

    def bias_chunk(c, part):
        s = score_ref[rows(c), :]
        sel = jnp.where(c * KEY_CHUNK + kiota <= qpos, jnp.where(s >= thr, 1.0, 0.0), 0.0)
        bias_ref[rows(c), :] = jnp.where(sel > 0.0, 0.0, _NEG_INF)
        return part + _fold_rows(sel, jnp.add)

    n_ge = jnp.sum(for_chunks(bias_chunk, jnp.zeros((SUBLANES, t), _F32)), axis=0, keepdims=True)

    @pl.when(jnp.max(n_ge) > n_sel)
    def _():
        nxt = _ordered_int_to_float(base + 1)
        tied = n_ge > n_sel
        n_above = count(lambda s, c: jnp.where(s >= nxt, 1.0, 0.0))
        need = jnp.where(tied, n_sel - n_above, 0.0)
        no_index = float(k_ref.shape[1])
        front_v = jnp.where(tied, jnp.inf, _NEG_INF)
        front_i = jnp.where(tied, -1.0, no_index)

        def beyond(s, kposf, front_v, front_i):
            after = jnp.where(s < front_v, 1.0,
                              jnp.where(s == front_v, jnp.where(kposf > front_i, 1.0, 0.0), 0.0))
            return jnp.where(s >= thr, jnp.where(s < nxt, after, 0.0), 0.0)

        def kposf(c):
            return (c * KEY_CHUNK + kiota).astype(_F32)

        def advance(_, carry):
            front_v, front_i, need = carry

            def best_value(c, part):
                s = score_ref[rows(c), :]
                cand = jnp.where(beyond(s, kposf(c), front_v, front_i) > 0.0, s, _NEG_INF)
                return jnp.maximum(part, _fold_rows(cand, jnp.maximum))

            v = jnp.max(for_chunks(best_value, jnp.full((SUBLANES, t), _NEG_INF, _F32)),
                        axis=0, keepdims=True)

            def first_index(c, part):
                s = score_ref[rows(c), :]
                hit = jnp.where(s == v, beyond(s, kposf(c), front_v, front_i), 0.0)
                cand = jnp.where(hit > 0.0, kposf(c), no_index)
                return jnp.minimum(part, _fold_rows(cand, jnp.minimum))

            i = jnp.min(for_chunks(first_index, jnp.full((SUBLANES, t), no_index, _F32)),
                        axis=0, keepdims=True)
            active = need > 0.0
            return (jnp.where(active, v, front_v), jnp.where(active, i, front_i),
                    jnp.where(active, need - 1.0, need))

        front_v, front_i, _ = lax.fori_loop(0, jnp.max(need).astype(jnp.int32), advance,
                                            (front_v, front_i, need))

        def tie_chunk(c, carry):
            s = score_ref[rows(c), :]
            upto = jnp.where(s > front_v, 0.0,
                             jnp.where(s == front_v,
                                       jnp.where(kposf(c) <= front_i, 0.0, _NEG_INF), _NEG_INF))
            keep = jnp.where(s >= nxt, 0.0, jnp.where(s >= thr, upto, _NEG_INF))
            bias_ref[rows(c), :] = jnp.where(c * KEY_CHUNK + kiota <= qpos, keep, _NEG_INF)
            return carry

        for_chunks(tie_chunk, 0)

    grp = B_HEADS // B_KV_HEADS
    gw = grp * t
    exp2_scale = (B_HEAD_DIM ** -0.5) * 1.4426950408889634

    def logit_chunk(c, mx):
        bias = jnp.concatenate([bias_ref[rows(c), :]] * grp, axis=1)
        folded = []
        for g in range(B_KV_HEADS):
            qg = jnp.concatenate(
                [q_ref[0, :, (g * grp + hh) * B_HEAD_DIM:(g * grp + hh + 1) * B_HEAD_DIM]
                 for hh in range(grp)], axis=0)
            l = lax.dot_general(k_ref[0, rows(c), g * B_HEAD_DIM:(g + 1) * B_HEAD_DIM], qg,
                                _NT_DIMS, preferred_element_type=_F32) + bias
            logit_ref[rows(c), g * gw:(g + 1) * gw] = l
            folded.append(_fold_rows(l, jnp.maximum))
        return jnp.maximum(mx, jnp.concatenate(folded, axis=1))

    mx = for_chunks(logit_chunk, jnp.full((SUBLANES, B_HEADS * t), _NEG_INF, _F32))
    mx = jnp.max(mx, axis=0, keepdims=True)

    acc_ref[...] = jnp.zeros_like(acc_ref)
    ones_rows = jnp.ones((acc_ref.shape[0] - B_HEAD_DIM, KEY_CHUNK), _BF16)

    def pv_chunk(c, carry):
        for g in range(B_KV_HEADS):
            cols = slice(g * gw, (g + 1) * gw)
            p = jnp.exp2((logit_ref[rows(c), cols] - mx[:, cols]) * exp2_scale)
            vt_ext = jnp.concatenate(
                [vt_ref[0, c, g * B_HEAD_DIM:(g + 1) * B_HEAD_DIM, :], ones_rows], axis=0)
            acc_ref[:, cols] += jnp.dot(vt_ext, p.astype(_BF16), preferred_element_type=_F32)
        return carry

    for_chunks(pv_chunk, 0)
    out_t = acc_ref[:B_HEAD_DIM, :] / acc_ref[B_HEAD_DIM:B_HEAD_DIM + 1, :]
    for h in range(B_HEADS):
        o_ref[0, :, h * B_HEAD_DIM:(h + 1) * B_HEAD_DIM] = (
            out_t[:, h * t:(h + 1) * t].T.astype(o_ref.dtype))


def _dsa_attention(q, k, vt, qi, ki, wit, side_job=None):
    bsz, seq, _ = q.shape
    assert Q_BLOCK == LANES and seq % KEY_CHUNK == 0
    n_sel = min(TOPK_MAX, seq // 4)
    n_blk = seq // Q_BLOCK

    def q_spec(width):
        return pl.BlockSpec((1, Q_BLOCK, width), lambda b, i: (b, i, 0))

    def kv_spec(width):
        return pl.BlockSpec((1, seq, width), lambda b, i: (b, 0, 0))

    def per_step(spec):
        return pl.BlockSpec(spec.block_shape, lambda b, i: spec.index_map(b * n_blk + i))

    side = None if side_job is None else side_job(bsz * n_blk)
    side_in = [] if side is None else [per_step(s) for s in side.in_specs]
    side_out = [] if side is None else [per_step(s) for s in side.out_specs]
    outs = pl.pallas_call(
        functools.partial(_dsa_kernel, n_sel=n_sel,
                          side_body=None if side is None else side.body,
                          n_side_in=len(side_in), n_side_out=len(side_out)),
        grid=(bsz, n_blk),
        in_specs=[q_spec(IDX_HEADS * LANES),
                  pl.BlockSpec((1, IDX_HEADS, Q_BLOCK), lambda b, i: (b, 0, i)),
                  kv_spec(LANES), q_spec(B_WIDTH), kv_spec(B_KV_WIDTH),
                  pl.BlockSpec((1, seq // KEY_CHUNK, B_KV_WIDTH, KEY_CHUNK),
                               lambda b, i: (b, 0, 0, 0))] + side_in,
        out_specs=[q_spec(B_WIDTH)] + side_out,
        out_shape=[jax.ShapeDtypeStruct((bsz, seq, B_WIDTH), _BF16)]
        + ([] if side is None else list(side.out_shape)),
        scratch_shapes=[pltpu.VMEM((seq, Q_BLOCK), _F32), pltpu.VMEM((seq, Q_BLOCK), _F32),
                        pltpu.VMEM((seq, B_HEADS * Q_BLOCK), _F32),
                        pltpu.VMEM((B_HEAD_DIM + 2 * SUBLANES, B_HEADS * Q_BLOCK), _F32)],
        compiler_params=_compiler_params(2),
        name="dsa_attention",
    )(qi, wit, ki, q, k, vt, *(() if side is None else side.arrays))
    return outs[0] if side is None else tuple(outs)


def _xattn_kernel(q_ref, k_ref, v_ref, o_ref):
    scale = X_HEAD_DIM ** -0.5
    for h in range(X_HEADS):
        hs = slice(h * X_HEAD_DIM, (h + 1) * X_HEAD_DIM)
        logits = lax.dot_general(q_ref[0, :, hs], k_ref[0, :, hs], _NT_DIMS,
                                 preferred_element_type=_F32) * scale
        mx = jnp.max(logits, axis=-1, keepdims=True)
        p = jnp.exp(logits - mx)
        denom = jnp.sum(p, axis=-1, keepdims=True)
        o = jnp.dot(p.astype(_BF16), v_ref[0, :, hs], preferred_element_type=_F32)
        o_ref[0, :, hs] = (o / denom).astype(o_ref.dtype)


def _cross_attention(q, k, v):
    bsz, seq, _ = q.shape
    tq = ROW_BLOCK["cross_attention"]
    m = k.shape[1]
    mem_spec = pl.BlockSpec((1, m, X_WIDTH), lambda b, i: (b, 0, 0))
    return pl.pallas_call(
        _xattn_kernel,
        grid=(bsz, seq // tq),
        in_specs=[pl.BlockSpec((1, tq, X_WIDTH), lambda b, i: (b, i, 0)), mem_spec, mem_spec],
        out_specs=pl.BlockSpec((1, tq, X_WIDTH), lambda b, i: (b, i, 0)),
        out_shape=jax.ShapeDtypeStruct((bsz, seq, X_WIDTH), _BF16),
        compiler_params=_compiler_params(2),
        name="cross_attention",
    )(q, k, v)


def _norm_proj_kernel(x_ref, g_ref, *refs):
    x = x_ref[...]
    ms = jnp.mean(x * x, axis=-1, keepdims=True)
    h = ((x * lax.rsqrt(ms + EPS)) * g_ref[...]).astype(_BF16)
    n_w = len(refs) // 2
    for w_ref, o_ref in zip(refs[:n_w], refs[n_w:]):
        o_ref[...] = jnp.dot(h, w_ref[...], preferred_element_type=_F32).astype(o_ref.dtype)


def _norm_proj(x, g, weights_bf16, out_dtype, name):
    m, d = x.shape
    tm = ROW_BLOCK["norm_proj"]
    return pl.pallas_call(
        _norm_proj_kernel,
        grid=(m // tm,),
        in_specs=[pl.BlockSpec((tm, d), lambda i: (i, 0)),
                  pl.BlockSpec((1, d), lambda i: (0, 0))]
        + [pl.BlockSpec(w.shape, lambda i: (0, 0), pipeline_mode=pl.Buffered(1))
           for w in weights_bf16],
        out_specs=[pl.BlockSpec((tm, w.shape[1]), lambda i: (i, 0)) for w in weights_bf16],
        out_shape=[jax.ShapeDtypeStruct((m, w.shape[1]), out_dtype) for w in weights_bf16],
        compiler_params=_compiler_params(1),
        name=name,
    )(x, g.reshape(1, d), *weights_bf16)


def _xattn_out_kernel(o_ref, w_ref, x_ref, g_ref, x_out_ref, h_out_ref):
    x = x_ref[...] + jnp.dot(o_ref[...], w_ref[...], preferred_element_type=_F32)
    x_out_ref[...] = x
    ms = jnp.mean(x * x, axis=-1, keepdims=True)
    h_out_ref[...] = ((x * lax.rsqrt(ms + EPS)) * g_ref[...]).astype(h_out_ref.dtype)


def _xattn_out_and_norm(o, w_bf16, x, g):
    m, k = o.shape
    tm = ROW_BLOCK["xattn_out_norm"]
    d = x.shape[1]
    row_spec = pl.BlockSpec((tm, d), lambda i: (i, 0))
    return pl.pallas_call(
        _xattn_out_kernel,
        grid=(m // tm,),
        in_specs=[pl.BlockSpec((tm, k), lambda i: (i, 0)),
                  pl.BlockSpec((k, d), lambda i: (0, 0), pipeline_mode=pl.Buffered(1)),
                  row_spec,
                  pl.BlockSpec((1, d), lambda i: (0, 0))],
        out_specs=[row_spec, row_spec],
        out_shape=[jax.ShapeDtypeStruct((m, d), _F32), jax.ShapeDtypeStruct((m, d), _BF16)],
        compiler_params=_compiler_params(1),
        name="xattn_out_norm",
    )(o, w_bf16, x, g.reshape(1, d))


def kernel(x, mem, positions, norm_mix_g, w_in, a_norm_g, a_spatial_w, a_spatial_b, p_a, p_b,
           w_out, norm_x_g, norm_mem_g, xq_w, xk_w, xv_w, xo_w, norm_ffn_g, ffn_w1, ffn_w3,
           ffn_w2, final_norm_g):
    bsz, seq, d = x.shape
    m = bsz * seq
    depth = w_in.shape[0]
    ffn_hidden = ffn_w1.shape[-1]

    xf = x.reshape(m, d)
    mem_f = mem.reshape(bsz * MEM_LEN, d)
    w_in_t = jnp.transpose(w_in, (0, 2, 1))
    for l in range(depth):
        def w_in_cols(col0):
            return Rhs(w_in_t, l, col0, transposed=True)

        h = _rmsnorm(xf, norm_mix_g[l], _BF16)
        att = _fused_matmul("proj_att", [h], [(0, w_in_cols(ZA_END))], [], _ep_identity,
                            ATT_WIDTH, _F32)
        za, q, k, vt, qi, ki, wit = _fused_matmul(
            "proj_za", [h], [(0, w_in_cols(0))], [], _ep_identity, ZA_END, _F32,
            side_job=functools.partial(_prep_side_job, att.reshape(bsz, seq, ATT_WIDTH),
                                       positions))
        y_a = _spatial_gating(za, a_norm_g[l], a_spatial_w[l], a_spatial_b[l])
        def weight_casts(n_steps):
            return _merge_side_jobs(
                _cast_side_job(p_a, l, n_steps), _cast_side_job(p_b, l, n_steps),
                _cast_side_job(w_in_t, l, n_steps, row0=ATT_END, rows=2 * d),
                _cast_side_job(w_out, l, n_steps), _cast_side_job(xq_w, l, n_steps),
                _cast_side_job(xk_w, l, n_steps), _cast_side_job(xv_w, l, n_steps))

        y_b, pa_bf16, pb_bf16, gate_wt_bf16, wo_bf16, xq_bf16, xk_bf16, xv_bf16 = _dsa_attention(
            q, k, vt, qi, ki, wit, side_job=weight_casts)

        def ffn_up_weight_casts(n_steps):
            return _merge_side_jobs(_cast_side_job(ffn_w1, l, n_steps),
                                    _cast_side_job(ffn_w3, l, n_steps))

        merged, w1_bf16, w3_bf16 = _fused_matmul(
            "gated_merge", [y_a, y_b.reshape(m, B_WIDTH), h],
            [(0, Rhs(pa_bf16)), (1, Rhs(pb_bf16)), (2, Rhs(gate_wt_bf16, transposed=True)),
             (2, Rhs(gate_wt_bf16, col0=d, transposed=True))],
            [], _ep_gated_merge, d, _BF16, side_job=ffn_up_weight_casts)
        x1, xo_bf16 = _fused_matmul("out_proj", [merged], [(0, Rhs(wo_bf16))], [xf],
                                    _ep_residual, d, _F32,
                                    side_job=functools.partial(_cast_side_job, xo_w, l))

        (qx,) = _norm_proj(x1, norm_x_g[l], [xq_bf16], _BF16, "xattn_q")
        kx, vx = _norm_proj(mem_f, norm_mem_g[l], [xk_bf16, xv_bf16], _BF16, "xattn_kv")
        ox = _cross_attention(qx.reshape(bsz, seq, X_WIDTH),
                              kx.reshape(bsz, MEM_LEN, X_WIDTH),
                              vx.reshape(bsz, MEM_LEN, X_WIDTH)).reshape(m, X_WIDTH)
        x2, h2 = _xattn_out_and_norm(ox, xo_bf16, x1, norm_ffn_g[l])

        act, w2_bf16 = _fused_matmul("ffn_up", [h2],
                                     [(0, Rhs(w1_bf16)), (0, Rhs(w3_bf16))],
                                     [], _ep_swiglu, ffn_hidden, _BF16,
                                     side_job=functools.partial(_cast_side_job, ffn_w2, l))
        xf = _fused_matmul("ffn_down", [act], [(0, Rhs(w2_bf16))], [x2], _ep_residual, d, _F32)

    return _rmsnorm(xf, final_norm_g, _F32).reshape(bsz, seq, d)
```
